```python
import math
import jax, jax.numpy as jnp
from jax import lax
import numpy as np

D_MODEL = 2048
BATCH = 8
SEQ = 2048
DEPTH = 4

HEAD_DIM = 64
N_Q_HEADS = D_MODEL // 2 // HEAD_DIM
N_KV_HEADS = N_Q_HEADS // 4
ATTN_WIDTH = N_Q_HEADS * HEAD_DIM
KV_WIDTH = N_KV_HEADS * HEAD_DIM
WINDOW = 128
ATTN_BLOCK = 128
ROPE_THETA = 10000.0
POOL_WINDOWS = (2, 4, 8, 16)
POOL_WIDTH = D_MODEL // 4
POOL_GROUP = POOL_WIDTH // len(POOL_WINDOWS)
SSM_WIDTH = D_MODEL // 4
SSM_GROUP = 16
SSM_N_GROUPS = SSM_WIDTH // SSM_GROUP
SSM_STATE = 64
MIX_WIDTH = ATTN_WIDTH + POOL_WIDTH + SSM_WIDTH
IN_WIDTH = ATTN_WIDTH + 2 * KV_WIDTH + POOL_WIDTH + SSM_WIDTH
D_FF = 5504
CONV_WIDTH = 3
LN_EPS = 1e-5
DEEPNORM_ALPHA = (2 * DEPTH) ** 0.25
DEEPNORM_BETA = (8 * DEPTH) ** -0.25

kernel_name = 'hybrid_swa_pool_s5_convffn'

F32 = jnp.float32


def layer_norm(x, g, b):
    xf = x.astype(F32)
    mu = xf.mean(-1, keepdims=True)
    var = jnp.square(xf - mu).mean(-1, keepdims=True)
    return ((xf - mu) * lax.rsqrt(var + LN_EPS) * g.astype(F32) + b.astype(F32)).astype(x.dtype)


def rope(t, pos):
    half = HEAD_DIM // 2
    inv = ROPE_THETA ** (-jnp.arange(half, dtype=F32) / half)
    ang = pos.astype(F32)[:, None] * inv[None, :]
    cos = jnp.cos(ang)[None, :, None, :]
    sin = jnp.sin(ang)[None, :, None, :]
    tf = t.astype(F32)
    t1, t2 = tf[..., :half], tf[..., half:]
    return jnp.concatenate([t1 * cos - t2 * sin, t2 * cos + t1 * sin], -1).astype(t.dtype)


def sliding_window_attention(q, k, v, sinks):
    bsz, s_len = q.shape[0], q.shape[1]
    nb = s_len // ATTN_BLOCK
    grp = N_Q_HEADS // N_KV_HEADS
    qb = q.reshape(bsz, nb, ATTN_BLOCK, N_KV_HEADS, grp, HEAD_DIM).astype(F32)

    def band(t):
        tb = t.reshape(bsz, nb, ATTN_BLOCK, N_KV_HEADS, HEAD_DIM)
        prev = jnp.concatenate([jnp.zeros_like(tb[:, :1]), tb[:, :-1]], axis=1)
        return jnp.concatenate([prev, tb], axis=2).astype(F32)

    kw, vw = band(k), band(v)
    s = jnp.einsum('bnqkgd,bnjkd->bnkgqj', qb, kw) * (HEAD_DIM ** -0.5)
    i = jnp.arange(ATTN_BLOCK)[:, None]
    j = jnp.arange(2 * ATTN_BLOCK)[None, :]
    dist = i + ATTN_BLOCK - j
    in_win = (dist >= 0) & (dist < WINDOW)
    blk = jnp.arange(nb)[:, None, None]
    valid = in_win[None] & ((blk > 0) | (j[None] >= ATTN_BLOCK))
    s = jnp.where(valid[None, :, None, None], s, -1e30)
    sink = sinks.astype(F32).reshape(N_KV_HEADS, grp)[None, None, :, :, None, None]
    m = jnp.maximum(s.max(-1, keepdims=True), sink)
    p = jnp.exp(s - m)
    denom = p.sum(-1, keepdims=True) + jnp.exp(sink - m)
    o = jnp.einsum('bnkgqj,bnjkd->bnqkgd', p / denom, vw)
    return o.reshape(bsz, s_len, ATTN_WIDTH).astype(q.dtype)


def multiscale_pool(u, pool_w, pool_scale):
    uf = u.astype(F32)
    s_len = u.shape[1]
    t = jnp.arange(s_len, dtype=F32)[None, :, None]
    outs = []
    for gi, w in enumerate(POOL_WINDOWS):
        ug = uf[..., gi * POOL_GROUP:(gi + 1) * POOL_GROUP]
        c = jnp.cumsum(ug, axis=1)
        c_shift = jnp.pad(c, ((0, 0), (w, 0), (0, 0)))[:, :s_len]
        mean = (c - c_shift) / jnp.minimum(t + 1.0, float(w))
        outs.append(jnp.einsum('bsc,cd->bsd', mean - ug, pool_w[gi].astype(F32)))
    return (jnp.concatenate(outs, -1) * pool_scale.astype(F32)).astype(u.dtype)


def s5_ssm(u, lam_re, lam_im, log_dt, b_re, b_im, c_re, c_im, d, glu_w):
    bsz, s_len = u.shape[0], u.shape[1]
    uf = u.astype(F32).reshape(bsz, s_len, SSM_N_GROUPS, SSM_GROUP)
    lr, li = lam_re.astype(F32), lam_im.astype(F32)
    dt = jnp.exp(log_dt.astype(F32))[:, None]
    mag = jnp.exp(lr * dt)
    ab_re, ab_im = mag * jnp.cos(li * dt), mag * jnp.sin(li * dt)
    nr, ni = ab_re - 1.0, ab_im
    den = lr * lr + li * li
    zr = (nr * lr + ni * li) / den
    zi = (ni * lr - nr * li) / den
    br, bi = b_re.astype(F32), b_im.astype(F32)
    bbr = zr[..., None] * br - zi[..., None] * bi
    bbi = zr[..., None] * bi + zi[..., None] * br
    xr = jnp.einsum('gph,bsgh->bsgp', bbr, uf)
    xi = jnp.einsum('gph,bsgh->bsgp', bbi, uf)
    ar = jnp.broadcast_to(ab_re, xr.shape)
    ai = jnp.broadcast_to(ab_im, xi.shape)

    def combine(e1, e2):
        a1r, a1i, x1r, x1i = e1
        a2r, a2i, x2r, x2i = e2
        return (a2r * a1r - a2i * a1i, a2r * a1i + a2i * a1r,
                a2r * x1r - a2i * x1i + x2r, a2r * x1i + a2i * x1r + x2i)

    _, _, sr, si = lax.associative_scan(combine, (ar, ai, xr, xi), axis=1)
    y = (jnp.einsum('ghp,bsgp->bsgh', c_re.astype(F32), sr)
         - jnp.einsum('ghp,bsgp->bsgh', c_im.astype(F32), si)
         + d.astype(F32) * uf)
    y = jax.nn.gelu(y.reshape(bsz, s_len, SSM_WIDTH))
    ab = jnp.einsum('bsc,ce->bse', y, glu_w.astype(F32))
    out = ab[..., :SSM_WIDTH] * jax.nn.sigmoid(ab[..., SSM_WIDTH:])
    return out.astype(u.dtype)


def conv_glu_ffn(x, w_up, conv_w, conv_b, w_down):
    s_len = x.shape[1]
    h = jnp.einsum('bsd,df->bsf', x, w_up)
    hp = jnp.pad(h, ((0, 0), (CONV_WIDTH - 1, 0), (0, 0)))
    hc = conv_b + hp[:, 0:s_len] * conv_w[0]
    for tap in range(1, CONV_WIDTH):
        hc = hc + hp[:, tap:tap + s_len] * conv_w[tap]
    val, gate = hc[..., :D_FF], hc[..., D_FF:]
    act = (jax.nn.silu(gate.astype(F32)) * val.astype(F32)).astype(x.dtype)
    return jnp.einsum('bsf,fd->bsd', act, w_down)


def _fwd_setup_inputs(seed: int = 0) -> dict:
    key = jax.random.key(seed)
    ks = jax.random.split(key, 32)
    L = DEPTH
    nrm = lambda k, shape, std: jax.random.normal(k, shape, F32) * std
    lam_im_base = math.pi * jnp.arange(SSM_STATE, dtype=F32)
    return {
        'x': nrm(ks[0], (BATCH, SEQ, D_MODEL), 1.0),
        'w_in': nrm(ks[1], (L, D_MODEL, IN_WIDTH), D_MODEL ** -0.5),
        'attn_sinks': nrm(ks[2], (L, N_Q_HEADS), 0.5),
        'pool_w': nrm(ks[3], (L, len(POOL_WINDOWS), POOL_GROUP, POOL_GROUP), POOL_GROUP ** -0.5),
        'pool_scale': 1.0 + nrm(ks[4], (L, POOL_WIDTH), 0.02),
        'ssm_lam_re': -0.5 + nrm(ks[5], (L, SSM_N_GROUPS, SSM_STATE), 0.01),
        'ssm_lam_im': lam_im_base + nrm(ks[6], (L, SSM_N_GROUPS, SSM_STATE), 0.01),
        'ssm_log_dt': jax.random.uniform(ks[7], (L, SSM_N_GROUPS), F32, math.log(1e-3), math.log(1e-1)),
        'ssm_b_re': nrm(ks[8], (L, SSM_N_GROUPS, SSM_STATE, SSM_GROUP), (2 * SSM_GROUP) ** -0.5),
        'ssm_b_im': nrm(ks[9], (L, SSM_N_GROUPS, SSM_STATE, SSM_GROUP), (2 * SSM_GROUP) ** -0.5),
        'ssm_c_re': nrm(ks[10], (L, SSM_N_GROUPS, SSM_GROUP, SSM_STATE), (2 * SSM_STATE) ** -0.5),
        'ssm_c_im': nrm(ks[11], (L, SSM_N_GROUPS, SSM_GROUP, SSM_STATE), (2 * SSM_STATE) ** -0.5),
        'ssm_d': nrm(ks[12], (L, SSM_N_GROUPS, SSM_GROUP), 1.0),
        'ssm_glu_w': nrm(ks[13], (L, SSM_WIDTH, 2 * SSM_WIDTH), SSM_WIDTH ** -0.5),
        'w_out': nrm(ks[14], (L, MIX_WIDTH, D_MODEL), MIX_WIDTH ** -0.5 * DEEPNORM_BETA),
        'ln1_g': 1.0 + nrm(ks[15], (L, D_MODEL), 0.02),
        'ln1_b': nrm(ks[16], (L, D_MODEL), 0.02),
        'ffn_w_up': nrm(ks[17], (L, D_MODEL, 2 * D_FF), D_MODEL ** -0.5),
        'ffn_conv_w': nrm(ks[18], (L, CONV_WIDTH, 2 * D_FF), CONV_WIDTH ** -0.5),
        'ffn_conv_b': nrm(ks[19], (L, 2 * D_FF), 0.01),
        'ffn_w_down': nrm(ks[20], (L, D_FF, D_MODEL), D_FF ** -0.5 * DEEPNORM_BETA),
        'ln2_g': 1.0 + nrm(ks[21], (L, D_MODEL), 0.02),
        'ln2_b': nrm(ks[22], (L, D_MODEL), 0.02),
    }


def _fwd_reference(x, w_in, attn_sinks, pool_w, pool_scale, ssm_lam_re, ssm_lam_im, ssm_log_dt,
              ssm_b_re, ssm_b_im, ssm_c_re, ssm_c_im, ssm_d, ssm_glu_w, w_out, ln1_g, ln1_b,
              ffn_w_up, ffn_conv_w, ffn_conv_b, ffn_w_down, ln2_g, ln2_b):
    bsz, s_len = x.shape[0], x.shape[1]
    pos = jnp.arange(s_len)
    o_k = ATTN_WIDTH
    o_v = o_k + KV_WIDTH
    o_p = o_v + KV_WIDTH
    o_s = o_p + POOL_WIDTH
    for l in range(DEPTH):
        h = jnp.einsum('bsd,de->bse', x, w_in[l])
        q = rope(h[..., :o_k].reshape(bsz, s_len, N_Q_HEADS, HEAD_DIM), pos)
        k = rope(h[..., o_k:o_v].reshape(bsz, s_len, N_KV_HEADS, HEAD_DIM), pos)
        v = h[..., o_v:o_p].reshape(bsz, s_len, N_KV_HEADS, HEAD_DIM)
        y_attn = sliding_window_attention(q, k, v, attn_sinks[l])
        y_pool = multiscale_pool(h[..., o_p:o_s], pool_w[l], pool_scale[l])
        y_ssm = s5_ssm(h[..., o_s:], ssm_lam_re[l], ssm_lam_im[l], ssm_log_dt[l],
                       ssm_b_re[l], ssm_b_im[l], ssm_c_re[l], ssm_c_im[l], ssm_d[l],
                       ssm_glu_w[l])
        mix = jnp.concatenate([y_attn, y_pool, y_ssm], -1)
        mix = jnp.einsum('bse,ed->bsd', mix, w_out[l])
        x = layer_norm(DEEPNORM_ALPHA * x + mix, ln1_g[l], ln1_b[l])
        f = conv_glu_ffn(x, ffn_w_up[l], ffn_conv_w[l], ffn_conv_b[l], ffn_w_down[l])
        x = layer_norm(DEEPNORM_ALPHA * x + f, ln2_g[l], ln2_b[l])
    return x


import jax as _jax
import jax.numpy as _jnp

TWIN_FORMAT = 'train_step'
FWD_PARAMS = ['x', 'w_in', 'attn_sinks', 'pool_w', 'pool_scale', 'ssm_lam_re', 'ssm_lam_im', 'ssm_log_dt', 'ssm_b_re', 'ssm_b_im', 'ssm_c_re', 'ssm_c_im', 'ssm_d', 'ssm_glu_w', 'w_out', 'ln1_g', 'ln1_b', 'ffn_w_up', 'ffn_conv_w', 'ffn_conv_b', 'ffn_w_down', 'ln2_g', 'ln2_b']
TWIN_WEIGHTS = ['w_in', 'attn_sinks', 'pool_w', 'pool_scale', 'ssm_lam_re', 'ssm_lam_im', 'ssm_log_dt', 'ssm_b_re', 'ssm_b_im', 'ssm_c_re', 'ssm_c_im', 'ssm_d', 'ssm_glu_w', 'w_out', 'ln1_g', 'ln1_b', 'ffn_w_up', 'ffn_conv_w', 'ffn_conv_b', 'ffn_w_down', 'ln2_g', 'ln2_b']
TWIN_DIFF_INPUT = 'x'
TWIN_INPUTS = ['x', 'w_in', 'attn_sinks', 'pool_w', 'pool_scale', 'ssm_lam_re', 'ssm_lam_im', 'ssm_log_dt', 'ssm_b_re', 'ssm_b_im', 'ssm_c_re', 'ssm_c_im', 'ssm_d', 'ssm_glu_w', 'w_out', 'ln1_g', 'ln1_b', 'ffn_w_up', 'ffn_conv_w', 'ffn_conv_b', 'ffn_w_down', 'ln2_g', 'ln2_b', 'loss_target', 'm_w_in', 'm_attn_sinks', 'm_pool_w', 'm_pool_scale', 'm_ssm_lam_re', 'm_ssm_lam_im', 'm_ssm_log_dt', 'm_ssm_b_re', 'm_ssm_b_im', 'm_ssm_c_re', 'm_ssm_c_im', 'm_ssm_d', 'm_ssm_glu_w', 'm_w_out', 'm_ln1_g', 'm_ln1_b', 'm_ffn_w_up', 'm_ffn_conv_w', 'm_ffn_conv_b', 'm_ffn_w_down', 'm_ln2_g', 'm_ln2_b', 'v_w_in', 'v_attn_sinks', 'v_pool_w', 'v_pool_scale', 'v_ssm_lam_re', 'v_ssm_lam_im', 'v_ssm_log_dt', 'v_ssm_b_re', 'v_ssm_b_im', 'v_ssm_c_re', 'v_ssm_c_im', 'v_ssm_d', 'v_ssm_glu_w', 'v_w_out', 'v_ln1_g', 'v_ln1_b', 'v_ffn_w_up', 'v_ffn_conv_w', 'v_ffn_conv_b', 'v_ffn_w_down', 'v_ln2_g', 'v_ln2_b']
TWIN_OUTPUTS = ['loss', 'grad_x', 'grad_w_in', 'grad_attn_sinks', 'grad_pool_w', 'grad_pool_scale', 'grad_ssm_lam_re', 'grad_ssm_lam_im', 'grad_ssm_log_dt', 'grad_ssm_b_re', 'grad_ssm_b_im', 'grad_ssm_c_re', 'grad_ssm_c_im', 'grad_ssm_d', 'grad_ssm_glu_w', 'grad_w_out', 'grad_ln1_g', 'grad_ln1_b', 'grad_ffn_w_up', 'grad_ffn_conv_w', 'grad_ffn_conv_b', 'grad_ffn_w_down', 'grad_ln2_g', 'grad_ln2_b', 'delta_w_in', 'delta_attn_sinks', 'delta_pool_w', 'delta_pool_scale', 'delta_ssm_lam_re', 'delta_ssm_lam_im', 'delta_ssm_log_dt', 'delta_ssm_b_re', 'delta_ssm_b_im', 'delta_ssm_c_re', 'delta_ssm_c_im', 'delta_ssm_d', 'delta_ssm_glu_w', 'delta_w_out', 'delta_ln1_g', 'delta_ln1_b', 'delta_ffn_w_up', 'delta_ffn_conv_w', 'delta_ffn_conv_b', 'delta_ffn_w_down', 'delta_ln2_g', 'delta_ln2_b', 'new_m_w_in', 'new_m_attn_sinks', 'new_m_pool_w', 'new_m_pool_scale', 'new_m_ssm_lam_re', 'new_m_ssm_lam_im', 'new_m_ssm_log_dt', 'new_m_ssm_b_re', 'new_m_ssm_b_im', 'new_m_ssm_c_re', 'new_m_ssm_c_im', 'new_m_ssm_d', 'new_m_ssm_glu_w', 'new_m_w_out', 'new_m_ln1_g', 'new_m_ln1_b', 'new_m_ffn_w_up', 'new_m_ffn_conv_w', 'new_m_ffn_conv_b', 'new_m_ffn_w_down', 'new_m_ln2_g', 'new_m_ln2_b', 'new_v_w_in', 'new_v_attn_sinks', 'new_v_pool_w', 'new_v_pool_scale', 'new_v_ssm_lam_re', 'new_v_ssm_lam_im', 'new_v_ssm_log_dt', 'new_v_ssm_b_re', 'new_v_ssm_b_im', 'new_v_ssm_c_re', 'new_v_ssm_c_im', 'new_v_ssm_d', 'new_v_ssm_glu_w', 'new_v_w_out', 'new_v_ln1_g', 'new_v_ln1_b', 'new_v_ffn_w_up', 'new_v_ffn_conv_w', 'new_v_ffn_conv_b', 'new_v_ffn_w_down', 'new_v_ln2_g', 'new_v_ln2_b']
TWIN_LEAF_KINDS = {'loss': 'loss', 'grad_x': 'grad_x', 'grad_w_in': 'grad_w', 'grad_attn_sinks': 'grad_w', 'grad_pool_w': 'grad_w', 'grad_pool_scale': 'grad_w', 'grad_ssm_lam_re': 'grad_w', 'grad_ssm_lam_im': 'grad_w', 'grad_ssm_log_dt': 'grad_w', 'grad_ssm_b_re': 'grad_w', 'grad_ssm_b_im': 'grad_w', 'grad_ssm_c_re': 'grad_w', 'grad_ssm_c_im': 'grad_w', 'grad_ssm_d': 'grad_w', 'grad_ssm_glu_w': 'grad_w', 'grad_w_out': 'grad_w', 'grad_ln1_g': 'grad_w', 'grad_ln1_b': 'grad_w', 'grad_ffn_w_up': 'grad_w', 'grad_ffn_conv_w': 'grad_w', 'grad_ffn_conv_b': 'grad_w', 'grad_ffn_w_down': 'grad_w', 'grad_ln2_g': 'grad_w', 'grad_ln2_b': 'grad_w', 'delta_w_in': 'delta_w', 'delta_attn_sinks': 'delta_w', 'delta_pool_w': 'delta_w', 'delta_pool_scale': 'delta_w', 'delta_ssm_lam_re': 'delta_w', 'delta_ssm_lam_im': 'delta_w', 'delta_ssm_log_dt': 'delta_w', 'delta_ssm_b_re': 'delta_w', 'delta_ssm_b_im': 'delta_w', 'delta_ssm_c_re': 'delta_w', 'delta_ssm_c_im': 'delta_w', 'delta_ssm_d': 'delta_w', 'delta_ssm_glu_w': 'delta_w', 'delta_w_out': 'delta_w', 'delta_ln1_g': 'delta_w', 'delta_ln1_b': 'delta_w', 'delta_ffn_w_up': 'delta_w', 'delta_ffn_conv_w': 'delta_w', 'delta_ffn_conv_b': 'delta_w', 'delta_ffn_w_down': 'delta_w', 'delta_ln2_g': 'delta_w', 'delta_ln2_b': 'delta_w', 'new_m_w_in': 'new_m', 'new_m_attn_sinks': 'new_m', 'new_m_pool_w': 'new_m', 'new_m_pool_scale': 'new_m', 'new_m_ssm_lam_re': 'new_m', 'new_m_ssm_lam_im': 'new_m', 'new_m_ssm_log_dt': 'new_m', 'new_m_ssm_b_re': 'new_m', 'new_m_ssm_b_im': 'new_m', 'new_m_ssm_c_re': 'new_m', 'new_m_ssm_c_im': 'new_m', 'new_m_ssm_d': 'new_m', 'new_m_ssm_glu_w': 'new_m', 'new_m_w_out': 'new_m', 'new_m_ln1_g': 'new_m', 'new_m_ln1_b': 'new_m', 'new_m_ffn_w_up': 'new_m', 'new_m_ffn_conv_w': 'new_m', 'new_m_ffn_conv_b': 'new_m', 'new_m_ffn_w_down': 'new_m', 'new_m_ln2_g': 'new_m', 'new_m_ln2_b': 'new_m', 'new_v_w_in': 'new_v', 'new_v_attn_sinks': 'new_v', 'new_v_pool_w': 'new_v', 'new_v_pool_scale': 'new_v', 'new_v_ssm_lam_re': 'new_v', 'new_v_ssm_lam_im': 'new_v', 'new_v_ssm_log_dt': 'new_v', 'new_v_ssm_b_re': 'new_v', 'new_v_ssm_b_im': 'new_v', 'new_v_ssm_c_re': 'new_v', 'new_v_ssm_c_im': 'new_v', 'new_v_ssm_d': 'new_v', 'new_v_ssm_glu_w': 'new_v', 'new_v_w_out': 'new_v', 'new_v_ln1_g': 'new_v', 'new_v_ln1_b': 'new_v', 'new_v_ffn_w_up': 'new_v', 'new_v_ffn_conv_w': 'new_v', 'new_v_ffn_conv_b': 'new_v', 'new_v_ffn_w_down': 'new_v', 'new_v_ln2_g': 'new_v', 'new_v_ln2_b': 'new_v'}


def _forward(args):
    return _fwd_reference(*[args[k] for k in FWD_PARAMS])


def _output_shape():
    out = _jax.eval_shape(lambda: _forward(_fwd_setup_inputs(0)))
    return out.shape, out.dtype

N_MICROBATCH = 1
ADAM_LR = 0.001
ADAM_B1 = 0.9
ADAM_B2 = 0.999
ADAM_EPS = 1e-08
ADAM_WD = 0.01
ADAM_STEP = 10
PER_EXAMPLE_BATCH_AXIS = {'x': 0, 'loss_target': 0}
SHARED_INPUTS = []
_WEIGHT_DTYPES = {'w_in': _jnp.float32, 'attn_sinks': _jnp.float32, 'pool_w': _jnp.float32, 'pool_scale': _jnp.float32, 'ssm_lam_re': _jnp.float32, 'ssm_lam_im': _jnp.float32, 'ssm_log_dt': _jnp.float32, 'ssm_b_re': _jnp.float32, 'ssm_b_im': _jnp.float32, 'ssm_c_re': _jnp.float32, 'ssm_c_im': _jnp.float32, 'ssm_d': _jnp.float32, 'ssm_glu_w': _jnp.float32, 'w_out': _jnp.float32, 'ln1_g': _jnp.float32, 'ln1_b': _jnp.float32, 'ffn_w_up': _jnp.float32, 'ffn_conv_w': _jnp.float32, 'ffn_conv_b': _jnp.float32, 'ffn_w_down': _jnp.float32, 'ln2_g': _jnp.float32, 'ln2_b': _jnp.float32}
MOMENT_SCALE = {'w_in': 7.544305e-03, 'attn_sinks': 2.646247e-03, 'pool_w': 1.402149e-02, 'pool_scale': 1.480870e-02, 'ssm_lam_re': 2.791293e-04, 'ssm_lam_im': 2.845344e-04, 'ssm_log_dt': 2.324382e-01, 'ssm_b_re': 1.844160e-04, 'ssm_b_im': 1.833934e-04, 'ssm_c_re': 3.684630e-04, 'ssm_c_im': 3.777896e-04, 'ssm_d': 7.978712e-03, 'ssm_glu_w': 5.246263e-03, 'w_out': 1.897375e-02, 'ln1_g': 2.750052e-01, 'ln1_b': 1.491740e-01, 'ffn_w_up': 5.969360e-03, 'ffn_conv_w': 5.976668e-03, 'ffn_conv_b': 6.396818e-03, 'ffn_w_down': 2.291829e-02, 'ln2_g': 4.022920e+00, 'ln2_b': 2.814499e-01}


def _to_microbatches(a, axis):
    t = _jnp.moveaxis(a, axis, 0)
    t = t.reshape((N_MICROBATCH, t.shape[0] // N_MICROBATCH) + t.shape[1:])
    return _jnp.moveaxis(t, 1, axis + 1)


def setup_inputs(seed: int = 0) -> dict:
    inp = _fwd_setup_inputs(seed)
    key = _jax.random.fold_in(_jax.random.key(seed), 7919)
    shape, _ = _output_shape()
    out = dict(inp)
    out["loss_target"] = _jax.random.normal(_jax.random.fold_in(key, 0), shape, _jnp.float32)
    for i, name in enumerate(TWIN_WEIGHTS):
        w = inp[name].astype(_jnp.float32)
        if MOMENT_SCALE is None:
            s = _jnp.sqrt(_jnp.mean(_jnp.square(w)) + 1e-30)
        else:
            s = MOMENT_SCALE[name]
        km, kv = _jax.random.split(_jax.random.fold_in(key, i + 1))
        out[name] = w
        out["m_" + name] = s * _jax.random.normal(km, w.shape, _jnp.float32)
        out["v_" + name] = (s * s) * _jax.random.uniform(kv, w.shape, _jnp.float32, 0.5, 1.5)
    if N_MICROBATCH > 1:
        for name, axis in PER_EXAMPLE_BATCH_AXIS.items():
            out[name] = _to_microbatches(out[name], axis)
    return {'x': out['x'], 'w_in': out['w_in'], 'attn_sinks': out['attn_sinks'], 'pool_w': out['pool_w'], 'pool_scale': out['pool_scale'], 'ssm_lam_re': out['ssm_lam_re'], 'ssm_lam_im': out['ssm_lam_im'], 'ssm_log_dt': out['ssm_log_dt'], 'ssm_b_re': out['ssm_b_re'], 'ssm_b_im': out['ssm_b_im'], 'ssm_c_re': out['ssm_c_re'], 'ssm_c_im': out['ssm_c_im'], 'ssm_d': out['ssm_d'], 'ssm_glu_w': out['ssm_glu_w'], 'w_out': out['w_out'], 'ln1_g': out['ln1_g'], 'ln1_b': out['ln1_b'], 'ffn_w_up': out['ffn_w_up'], 'ffn_conv_w': out['ffn_conv_w'], 'ffn_conv_b': out['ffn_conv_b'], 'ffn_w_down': out['ffn_w_down'], 'ln2_g': out['ln2_g'], 'ln2_b': out['ln2_b'], 'loss_target': out['loss_target'], 'm_w_in': out['m_w_in'], 'm_attn_sinks': out['m_attn_sinks'], 'm_pool_w': out['m_pool_w'], 'm_pool_scale': out['m_pool_scale'], 'm_ssm_lam_re': out['m_ssm_lam_re'], 'm_ssm_lam_im': out['m_ssm_lam_im'], 'm_ssm_log_dt': out['m_ssm_log_dt'], 'm_ssm_b_re': out['m_ssm_b_re'], 'm_ssm_b_im': out['m_ssm_b_im'], 'm_ssm_c_re': out['m_ssm_c_re'], 'm_ssm_c_im': out['m_ssm_c_im'], 'm_ssm_d': out['m_ssm_d'], 'm_ssm_glu_w': out['m_ssm_glu_w'], 'm_w_out': out['m_w_out'], 'm_ln1_g': out['m_ln1_g'], 'm_ln1_b': out['m_ln1_b'], 'm_ffn_w_up': out['m_ffn_w_up'], 'm_ffn_conv_w': out['m_ffn_conv_w'], 'm_ffn_conv_b': out['m_ffn_conv_b'], 'm_ffn_w_down': out['m_ffn_w_down'], 'm_ln2_g': out['m_ln2_g'], 'm_ln2_b': out['m_ln2_b'], 'v_w_in': out['v_w_in'], 'v_attn_sinks': out['v_attn_sinks'], 'v_pool_w': out['v_pool_w'], 'v_pool_scale': out['v_pool_scale'], 'v_ssm_lam_re': out['v_ssm_lam_re'], 'v_ssm_lam_im': out['v_ssm_lam_im'], 'v_ssm_log_dt': out['v_ssm_log_dt'], 'v_ssm_b_re': out['v_ssm_b_re'], 'v_ssm_b_im': out['v_ssm_b_im'], 'v_ssm_c_re': out['v_ssm_c_re'], 'v_ssm_c_im': out['v_ssm_c_im'], 'v_ssm_d': out['v_ssm_d'], 'v_ssm_glu_w': out['v_ssm_glu_w'], 'v_w_out': out['v_w_out'], 'v_ln1_g': out['v_ln1_g'], 'v_ln1_b': out['v_ln1_b'], 'v_ffn_w_up': out['v_ffn_w_up'], 'v_ffn_conv_w': out['v_ffn_conv_w'], 'v_ffn_conv_b': out['v_ffn_conv_b'], 'v_ffn_w_down': out['v_ffn_w_down'], 'v_ln2_g': out['v_ln2_g'], 'v_ln2_b': out['v_ln2_b']}


def _loss(weights, diff, rest, loss_target):
    with _jax.named_scope("forward"):
        args = {**rest, TWIN_DIFF_INPUT: diff, **{k: w.astype(_WEIGHT_DTYPES[k]) for k, w in weights.items()}}
        y = _forward(args)
    with _jax.named_scope("loss_head"):
        err = _jnp.square(y.astype(_jnp.float32) - loss_target)
        return 0.5 * _jnp.sum(_jnp.mean(err, axis=-1)) if err.ndim else 0.5 * err


def _adamw(w, g, m, v):
    m = ADAM_B1 * m + (1.0 - ADAM_B1) * g
    v = ADAM_B2 * v + (1.0 - ADAM_B2) * _jnp.square(g)
    m_hat = m / (1.0 - ADAM_B1 ** ADAM_STEP)
    v_hat = v / (1.0 - ADAM_B2 ** ADAM_STEP)
    delta = -ADAM_LR * (m_hat / (_jnp.sqrt(v_hat) + ADAM_EPS) + ADAM_WD * w)
    return delta, m, v


def reference(x, w_in, attn_sinks, pool_w, pool_scale, ssm_lam_re, ssm_lam_im, ssm_log_dt, ssm_b_re, ssm_b_im, ssm_c_re, ssm_c_im, ssm_d, ssm_glu_w, w_out, ln1_g, ln1_b, ffn_w_up, ffn_conv_w, ffn_conv_b, ffn_w_down, ln2_g, ln2_b, loss_target, m_w_in, m_attn_sinks, m_pool_w, m_pool_scale, m_ssm_lam_re, m_ssm_lam_im, m_ssm_log_dt, m_ssm_b_re, m_ssm_b_im, m_ssm_c_re, m_ssm_c_im, m_ssm_d, m_ssm_glu_w, m_w_out, m_ln1_g, m_ln1_b, m_ffn_w_up, m_ffn_conv_w, m_ffn_conv_b, m_ffn_w_down, m_ln2_g, m_ln2_b, v_w_in, v_attn_sinks, v_pool_w, v_pool_scale, v_ssm_lam_re, v_ssm_lam_im, v_ssm_log_dt, v_ssm_b_re, v_ssm_b_im, v_ssm_c_re, v_ssm_c_im, v_ssm_d, v_ssm_glu_w, v_w_out, v_ln1_g, v_ln1_b, v_ffn_w_up, v_ffn_conv_w, v_ffn_conv_b, v_ffn_w_down, v_ln2_g, v_ln2_b):
    given = dict(x=x, w_in=w_in, attn_sinks=attn_sinks, pool_w=pool_w, pool_scale=pool_scale, ssm_lam_re=ssm_lam_re, ssm_lam_im=ssm_lam_im, ssm_log_dt=ssm_log_dt, ssm_b_re=ssm_b_re, ssm_b_im=ssm_b_im, ssm_c_re=ssm_c_re, ssm_c_im=ssm_c_im, ssm_d=ssm_d, ssm_glu_w=ssm_glu_w, w_out=w_out, ln1_g=ln1_g, ln1_b=ln1_b, ffn_w_up=ffn_w_up, ffn_conv_w=ffn_conv_w, ffn_conv_b=ffn_conv_b, ffn_w_down=ffn_w_down, ln2_g=ln2_g, ln2_b=ln2_b, loss_target=loss_target, m_w_in=m_w_in, m_attn_sinks=m_attn_sinks, m_pool_w=m_pool_w, m_pool_scale=m_pool_scale, m_ssm_lam_re=m_ssm_lam_re, m_ssm_lam_im=m_ssm_lam_im, m_ssm_log_dt=m_ssm_log_dt, m_ssm_b_re=m_ssm_b_re, m_ssm_b_im=m_ssm_b_im, m_ssm_c_re=m_ssm_c_re, m_ssm_c_im=m_ssm_c_im, m_ssm_d=m_ssm_d, m_ssm_glu_w=m_ssm_glu_w, m_w_out=m_w_out, m_ln1_g=m_ln1_g, m_ln1_b=m_ln1_b, m_ffn_w_up=m_ffn_w_up, m_ffn_conv_w=m_ffn_conv_w, m_ffn_conv_b=m_ffn_conv_b, m_ffn_w_down=m_ffn_w_down, m_ln2_g=m_ln2_g, m_ln2_b=m_ln2_b, v_w_in=v_w_in, v_attn_sinks=v_attn_sinks, v_pool_w=v_pool_w, v_pool_scale=v_pool_scale, v_ssm_lam_re=v_ssm_lam_re, v_ssm_lam_im=v_ssm_lam_im, v_ssm_log_dt=v_ssm_log_dt, v_ssm_b_re=v_ssm_b_re, v_ssm_b_im=v_ssm_b_im, v_ssm_c_re=v_ssm_c_re, v_ssm_c_im=v_ssm_c_im, v_ssm_d=v_ssm_d, v_ssm_glu_w=v_ssm_glu_w, v_w_out=v_w_out, v_ln1_g=v_ln1_g, v_ln1_b=v_ln1_b, v_ffn_w_up=v_ffn_w_up, v_ffn_conv_w=v_ffn_conv_w, v_ffn_conv_b=v_ffn_conv_b, v_ffn_w_down=v_ffn_w_down, v_ln2_g=v_ln2_g, v_ln2_b=v_ln2_b)
    weights = {n: given[n] for n in TWIN_WEIGHTS}
    shared = {n: given[n] for n in SHARED_INPUTS}
    per_example = {n: given[n] for n in ['x']}
    grad_fn = _jax.value_and_grad(_loss, argnums=(0, 1))

    def one_microbatch(ex, loss_target):
        ex = dict(ex)
        diff = ex.pop(TWIN_DIFF_INPUT)
        return grad_fn(weights, diff, {**shared, **ex}, loss_target)

    if N_MICROBATCH == 1:
        loss, (grad_w, grad_x) = one_microbatch(per_example, given["loss_target"])
    else:
        def body(carry, xs):
            loss_sum, grad_sum = carry
            l_k, (gw_k, gx_k) = one_microbatch(xs[0], xs[1])
            with _jax.named_scope("update"):
                return (loss_sum + l_k, _jax.tree.map(_jnp.add, grad_sum, gw_k)), gx_k

        init = (_jnp.zeros((), _jnp.float32), _jax.tree.map(_jnp.zeros_like, weights))
        (loss, grad_w), grad_x = _jax.lax.scan(body, init, (per_example, given["loss_target"]))
    with _jax.named_scope("update"):
        delta_w, new_m, new_v = {}, {}, {}
        for n in TWIN_WEIGHTS:
            delta_w[n], new_m[n], new_v[n] = _adamw(weights[n], grad_w[n], given["m_" + n], given["v_" + n])
    return (loss, grad_x, *[grad_w[n] for n in TWIN_WEIGHTS], *[delta_w[n] for n in TWIN_WEIGHTS],
            *[new_m[n] for n in TWIN_WEIGHTS], *[new_v[n] for n in TWIN_WEIGHTS])
```

```python
import functools
import math

import jax
import jax.numpy as jnp
from jax import lax
from jax.experimental import pallas as pl
from jax.experimental.pallas import tpu as pltpu

F32 = jnp.float32
_MXU = jnp.bfloat16
_WIRE = jnp.bfloat16

HEAD_DIM = 64
GQA = 4
ATTN_BLOCK = 128
ROPE_THETA = 10000.0
POOL_WINDOWS = (2, 4, 8, 16)
SSM_GROUP = 16
SSM_STATE = 64
CONV_WIDTH = 3
LN_EPS = 1e-5
ADAM_LR, ADAM_B1, ADAM_B2, ADAM_EPS, ADAM_WD, ADAM_STEP = 0.001, 0.9, 0.999, 1e-08, 0.01, 10

N_DEV = 8
V7X_LANES = 128
V7X_VMEM_LIMIT = 56 * 1024 * 1024
SCAN_T = 64
SCAN_LANES = 256
MESH = pl.DeviceIdType.MESH
ANY = pl.BlockSpec(memory_space=pl.ANY)


def _cp():
    return pltpu.CompilerParams(vmem_limit_bytes=V7X_VMEM_LIMIT)


def _resident(block, index_map):
    return pl.BlockSpec(block, index_map, pipeline_mode=pl.Buffered(1))


def _sds(shape, dtype):
    return jax.ShapeDtypeStruct(tuple(shape), dtype)


def _mm(name, a, b, dims, grid, a_spec, b_spec, o_spec, out_shape, out_dtype, add=None, add_spec=None, add_scale=1.0):
    nk = grid[1]
    oblk = tuple(d for d in o_spec.block_shape if d is not None)

    def body(*refs):
        if add is None:
            a_ref, b_ref, o_ref = refs[:3]
            add_ref = None
        else:
            a_ref, b_ref, add_ref, o_ref = refs[:4]
        acc_ref = refs[-1] if nk > 1 else None

        def finish(r):
            if add_ref is not None:
                r = r + add_scale * add_ref[...]
            o_ref[...] = r.astype(o_ref.dtype)

        part = lax.dot_general(a_ref[...], b_ref[...], (dims, ((), ())), preferred_element_type=F32)
        if nk == 1:
            finish(part)
        else:
            k = pl.program_id(1)

            @pl.when(k == 0)
            def _():
                acc_ref[...] = part

            @pl.when(k > 0)
            def _():
                acc_ref[...] += part

            @pl.when(k == nk - 1)
            def _():
                finish(acc_ref[...])

    ins = [a, b] + ([] if add is None else [add])
    in_specs = [a_spec, b_spec] + ([] if add is None else [add_spec])
    return pl.pallas_call(
        body, grid=grid, in_specs=in_specs, out_specs=o_spec, out_shape=_sds(out_shape, out_dtype),
        scratch_shapes=[pltpu.VMEM(oblk, F32)] if nk > 1 else [], compiler_params=_cp(), name=name,
    )(*ins)


NN = ((1,), (0,))
NT = ((1,), (1,))
TN = ((0,), (0,))


def _col_tile(n, cap=512):
    if n % V7X_LANES:
        return n
    t = min(cap, n)
    t -= t % V7X_LANES
    while n % t:
        t -= V7X_LANES
    return t


def _mm_nn(name, a, b, out_dtype=F32, cap=512):
    m, k = a.shape
    n = b.shape[1]
    tn = _col_tile(n, cap)
    return _mm(name, a, b, NN, (n // tn, 1), _resident((m, k), lambda j, kk: (0, 0)),
               pl.BlockSpec((k, tn), lambda j, kk: (0, j)), pl.BlockSpec((m, tn), lambda j, kk: (0, j)), (m, n), out_dtype)


def _mm_nt(name, a, b, out_dtype=F32, add=None, add_scale=1.0, cap=512):
    m, k = a.shape
    n = b.shape[0]
    tn = _col_tile(n, cap)
    o_spec = pl.BlockSpec((m, tn), lambda j, kk: (0, j))
    return _mm(name, a, b, NT, (n // tn, 1), _resident((m, k), lambda j, kk: (0, 0)),
               pl.BlockSpec((tn, k), lambda j, kk: (j, 0)), o_spec, (m, n), out_dtype,
               add=add, add_spec=None if add is None else o_spec, add_scale=add_scale)


def _mm_tn_bcols(name, a, b, out_dtype, cap=512):
    s, k = a.shape
    n = b.shape[1]
    tn = _col_tile(n, cap)
    return _mm(name, a, b, TN, (n // tn, 1), _resident((s, k), lambda j, kk: (0, 0)),
               pl.BlockSpec((s, tn), lambda j, kk: (0, j)), pl.BlockSpec((k, tn), lambda j, kk: (0, j)), (k, n), out_dtype)


def _mm_tn_acols(name, a, b, out_dtype, cap=512):
    s, k = a.shape
    n = b.shape[1]
    tk = _col_tile(k, cap)
    return _mm(name, a, b, TN, (k // tk, 1), pl.BlockSpec((s, tk), lambda i, kk: (0, i)),
               _resident((s, n), lambda i, kk: (0, 0)), pl.BlockSpec((tk, n), lambda i, kk: (i, 0)), (k, n), out_dtype)


def _ln_fwd(name, x, a, g, b, alpha):
    s, d = x.shape
    tr = min(256, s)

    def body(x_ref, a_ref, g_ref, b_ref, y_ref, yb_ref, xh_ref, rs_ref):
        r = alpha * x_ref[...] + a_ref[...]
        mu = jnp.mean(r, -1, keepdims=True)
        c = r - mu
        var = jnp.mean(c * c, -1, keepdims=True)
        rstd = lax.rsqrt(var + LN_EPS)
        xh = c * rstd
        y = xh * g_ref[...] + b_ref[...]
        y_ref[...] = y
        yb_ref[...] = y.astype(_MXU)
        xh_ref[...] = xh
        rs_ref[...] = rstd

    row = pl.BlockSpec((tr, d), lambda i: (i, 0))
    vec = pl.BlockSpec((1, d), lambda i: (0, 0))
    return pl.pallas_call(
        body, grid=(s // tr,), in_specs=[row, row, vec, vec],
        out_specs=[row, row, row, pl.BlockSpec((tr, 1), lambda i: (i, 0))],
        out_shape=[_sds((s, d), F32), _sds((s, d), _MXU), _sds((s, d), F32), _sds((s, 1), F32)],
        compiler_params=_cp(), name=name,
    )(x, a, g, b)


def _ln_bwd(name, dy, xh, rstd, g):
    s, d = dy.shape
    tr = min(256, s)

    def body(dy_ref, xh_ref, rs_ref, g_ref, dr_ref, drb_ref, dg_ref, db_ref):
        i = pl.program_id(0)
        dy_ = dy_ref[...]
        xh_ = xh_ref[...]
        dxh = dy_ * g_ref[...]
        m1 = jnp.mean(dxh, -1, keepdims=True)
        m2 = jnp.mean(dxh * xh_, -1, keepdims=True)
        dr = rs_ref[...] * (dxh - m1 - xh_ * m2)
        dr_ref[...] = dr
        drb_ref[...] = dr.astype(_MXU)
        pg = jnp.sum(dy_ * xh_, 0, keepdims=True)
        pb = jnp.sum(dy_, 0, keepdims=True)

        @pl.when(i == 0)
        def _():
            dg_ref[...] = pg
            db_ref[...] = pb

        @pl.when(i > 0)
        def _():
            dg_ref[...] += pg
            db_ref[...] += pb

    row = pl.BlockSpec((tr, d), lambda i: (i, 0))
    vec = pl.BlockSpec((1, d), lambda i: (0, 0))
    return pl.pallas_call(
        body, grid=(s // tr,), in_specs=[row, row, pl.BlockSpec((tr, 1), lambda i: (i, 0)), vec],
        out_specs=[row, row, vec, vec],
        out_shape=[_sds((s, d), F32), _sds((s, d), _MXU), _sds((1, d), F32), _sds((1, d), F32)],
        compiler_params=_cp(), name=name,
    )(dy, xh, rstd, g)


def _loss_head(name, y, target):
    s, d = y.shape
    tr = min(256, s)

    def body(y_ref, t_ref, dy_ref, l_ref):
        i = pl.program_id(0)
        e = y_ref[...] - t_ref[...]
        dy_ref[...] = e * (1.0 / d)
        part = 0.5 * jnp.sum(jnp.mean(e * e, -1, keepdims=True), 0, keepdims=True)

        @pl.when(i == 0)
        def _():
            l_ref[...] = part

        @pl.when(i > 0)
        def _():
            l_ref[...] += part

    row = pl.BlockSpec((tr, d), lambda i: (i, 0))
    return pl.pallas_call(
        body, grid=(s // tr,), in_specs=[row, row], out_specs=[row, pl.BlockSpec((1, 1), lambda i: (0, 0))],
        out_shape=[_sds((s, d), F32), _sds((1, 1), F32)], compiler_params=_cp(), name=name,
    )(y, target)


def _rope(name, t, width, cos, sin, out_dtype, splits):
    s = t.shape[0]
    tr = min(256, s)
    assert width % V7X_LANES == 0

    def body(t_ref, c_ref, s_ref, *o_refs):
        lane = lax.broadcasted_iota(jnp.int32, (tr, V7X_LANES), 1)
        first = (lane % HEAD_DIM) < (HEAD_DIM // 2)
        cs, sn = c_ref[...], s_ref[...]
        for (lo, hi), o_ref in zip(splits, o_refs):
            for c0 in range(lo, hi, V7X_LANES):
                v = t_ref[:, c0:c0 + V7X_LANES].astype(F32)
                partner = jnp.where(first, pltpu.roll(v, V7X_LANES - HEAD_DIM // 2, 1), pltpu.roll(v, HEAD_DIM // 2, 1))
                o_ref[:, c0 - lo:c0 - lo + V7X_LANES] = (v * cs + partner * sn).astype(o_ref.dtype)

    tab = pl.BlockSpec((tr, V7X_LANES), lambda i: (i, 0))
    return pl.pallas_call(
        body, grid=(s // tr,), in_specs=[pl.BlockSpec((tr, width), lambda i: (i, 0)), tab, tab],
        out_specs=[pl.BlockSpec((tr, hi - lo), lambda i: (i, 0)) for lo, hi in splits],
        out_shape=[_sds((s, hi - lo), out_dtype) for lo, hi in splits], compiler_params=_cp(), name=name,
    )(t, cos, sin)


def _attn_masks():
    i = lax.broadcasted_iota(jnp.int32, (ATTN_BLOCK, ATTN_BLOCK), 0)
    j = lax.broadcasted_iota(jnp.int32, (ATTN_BLOCK, ATTN_BLOCK), 1)
    return j <= i, j > i


def _attn_scores(qg, kc, kp, n, cur_ok, prev_ok):
    scale = HEAD_DIM ** -0.5
    s_c = lax.dot_general(qg, kc, (NT, ((), ())), preferred_element_type=F32) * scale
    s_p = lax.dot_general(qg, kp, (NT, ((), ())), preferred_element_type=F32) * scale
    s_c = jnp.where(cur_ok, s_c, -1e30)
    s_p = jnp.where(jnp.logical_and(prev_ok, n > 0), s_p, -1e30)
    return s_c, s_p


def _attn_fwd(name, q, k, v, sinks):
    s = q.shape[0]
    nkv = k.shape[0]
    gw = GQA * HEAD_DIM
    nb = s // ATTN_BLOCK

    def body(sk_ref, q_ref, k_ref, v_ref, o_ref, lse_ref):
        h = pl.program_id(0)
        cur_ok, prev_ok = _attn_masks()

        def blk(n, carry):
            off = pl.multiple_of(n * ATTN_BLOCK, ATTN_BLOCK)
            poff = pl.multiple_of(jnp.maximum(n - 1, 0) * ATTN_BLOCK, ATTN_BLOCK)
            kc, kp = k_ref[pl.ds(off, ATTN_BLOCK), :], k_ref[pl.ds(poff, ATTN_BLOCK), :]
            vc, vp = v_ref[pl.ds(off, ATTN_BLOCK), :], v_ref[pl.ds(poff, ATTN_BLOCK), :]
            for g in range(GQA):
                qg = q_ref[pl.ds(off, ATTN_BLOCK), g * HEAD_DIM:(g + 1) * HEAD_DIM]
                s_c, s_p = _attn_scores(qg, kc, kp, n, cur_ok, prev_ok)
                sink = sk_ref[h, g]
                m = jnp.maximum(jnp.maximum(s_c.max(-1, keepdims=True), s_p.max(-1, keepdims=True)), sink)
                p_c, p_p = jnp.exp(s_c - m), jnp.exp(s_p - m)
                den = p_c.sum(-1, keepdims=True) + p_p.sum(-1, keepdims=True) + jnp.exp(sink - m)
                o = (jnp.dot((p_c / den).astype(_MXU), vc, preferred_element_type=F32)
                     + jnp.dot((p_p / den).astype(_MXU), vp, preferred_element_type=F32))
                o_ref[pl.ds(off, ATTN_BLOCK), g * HEAD_DIM:(g + 1) * HEAD_DIM] = o
                lse_ref[pl.ds(off, ATTN_BLOCK), g:g + 1] = m + jnp.log(den)
            return carry

        lax.fori_loop(0, nb, blk, 0)

    kv_spec = pl.BlockSpec((None, s, HEAD_DIM), lambda h: (h, 0, 0))
    return pl.pallas_call(
        body, grid=(nkv,),
        in_specs=[pl.BlockSpec(memory_space=pltpu.SMEM), pl.BlockSpec((s, gw), lambda h: (0, h)), kv_spec, kv_spec],
        out_specs=[pl.BlockSpec((s, gw), lambda h: (0, h)), pl.BlockSpec((None, s, GQA), lambda h: (h, 0, 0))],
        out_shape=[_sds((s, nkv * gw), F32), _sds((nkv, s, GQA), F32)], compiler_params=_cp(), name=name,
    )(sinks, q, k, v)


def _attn_bwd(name, q, k, v, o, dmix, lse, sinks):
    s = q.shape[0]
    nkv = k.shape[0]
    gw = GQA * HEAD_DIM
    nb = s // ATTN_BLOCK
    scale = HEAD_DIM ** -0.5

    def body(sk_ref, q_ref, k_ref, v_ref, o_ref, do_ref, lse_ref, dq_ref, dk_ref, dv_ref, dsk_ref):
        h = pl.program_id(0)
        cur_ok, prev_ok = _attn_masks()
        dk_ref[...] = jnp.zeros_like(dk_ref)
        dv_ref[...] = jnp.zeros_like(dv_ref)

        def blk(n, acc):
            off = pl.multiple_of(n * ATTN_BLOCK, ATTN_BLOCK)
            poff = pl.multiple_of(jnp.maximum(n - 1, 0) * ATTN_BLOCK, ATTN_BLOCK)
            rows, prows = pl.ds(off, ATTN_BLOCK), pl.ds(poff, ATTN_BLOCK)
            kc, kp, vc, vp = k_ref[rows, :], k_ref[prows, :], v_ref[rows, :], v_ref[prows, :]
            dk_c = jnp.zeros((ATTN_BLOCK, HEAD_DIM), F32)
            dk_p = jnp.zeros((ATTN_BLOCK, HEAD_DIM), F32)
            dv_c = jnp.zeros((ATTN_BLOCK, HEAD_DIM), F32)
            dv_p = jnp.zeros((ATTN_BLOCK, HEAD_DIM), F32)
            out_acc = []
            for g in range(GQA):
                cols = slice(g * HEAD_DIM, (g + 1) * HEAD_DIM)
                qg = q_ref[rows, cols]
                do_g = do_ref[rows, cols]
                delta = jnp.sum(do_g * o_ref[rows, cols], -1, keepdims=True)
                dob = do_g.astype(_MXU)
                lse_g = lse_ref[rows, g:g + 1]
                s_c, s_p = _attn_scores(qg, kc, kp, n, cur_ok, prev_ok)
                p_c, p_p = jnp.exp(s_c - lse_g), jnp.exp(s_p - lse_g)
                dp_c = lax.dot_general(dob, vc, (NT, ((), ())), preferred_element_type=F32)
                dp_p = lax.dot_general(dob, vp, (NT, ((), ())), preferred_element_type=F32)
                ds_c = (p_c * (dp_c - delta) * scale).astype(_MXU)
                ds_p = (p_p * (dp_p - delta) * scale).astype(_MXU)
                dq_ref[rows, cols] = (jnp.dot(ds_c, kc, preferred_element_type=F32)
                                      + jnp.dot(ds_p, kp, preferred_element_type=F32))
                dk_c += lax.dot_general(ds_c, qg, (TN, ((), ())), preferred_element_type=F32)
                dk_p += lax.dot_general(ds_p, qg, (TN, ((), ())), preferred_element_type=F32)
                dv_c += lax.dot_general(p_c.astype(_MXU), dob, (TN, ((), ())), preferred_element_type=F32)
                dv_p += lax.dot_general(p_p.astype(_MXU), dob, (TN, ((), ())), preferred_element_type=F32)
                out_acc.append(acc[g] - jnp.exp(sk_ref[h, g] - lse_g) * delta)
            dk_ref[rows, :] += dk_c
            dv_ref[rows, :] += dv_c
            dk_ref[prows, :] += dk_p
            dv_ref[prows, :] += dv_p
            return tuple(out_acc)

        acc = lax.fori_loop(0, nb, blk, tuple(jnp.zeros((ATTN_BLOCK, 1), F32) for _ in range(GQA)))
        for g in range(GQA):
            dsk_ref[:, g:g + 1] = jnp.sum(acc[g], 0, keepdims=True)

    kv_spec = pl.BlockSpec((None, s, HEAD_DIM), lambda h: (h, 0, 0))
    qcols = pl.BlockSpec((s, gw), lambda h: (0, h))
    return pl.pallas_call(
        body, grid=(nkv,),
        in_specs=[pl.BlockSpec(memory_space=pltpu.SMEM), qcols, kv_spec, kv_spec, qcols, qcols,
                  pl.BlockSpec((None, s, GQA), lambda h: (h, 0, 0))],
        out_specs=[qcols, kv_spec, kv_spec, pl.BlockSpec((None, 1, GQA), lambda h: (h, 0, 0))],
        out_shape=[_sds((s, nkv * gw), F32), _sds((nkv, s, HEAD_DIM), F32), _sds((nkv, s, HEAD_DIM), F32),
                   _sds((nkv, 1, GQA), F32)],
        compiler_params=_cp(), name=name,
    )(sinks, q, k, v, o, dmix, lse)


def _shift_down(a, k, t):
    return jnp.where(t >= k, pltpu.roll(a, k, 0), 0.0)


def _shift_up(a, k, t):
    n = a.shape[0]
    return jnp.where(t < n - k, pltpu.roll(a, n - k, 0), 0.0)


def _pool_fwd(name, h, col_block, pool_w, pool_scale):
    s = h.shape[0]
    ng, pg = pool_w.shape[0], pool_w.shape[1]
    pw_ = ng * pg

    def body(u_ref, w_ref, sc_ref, y_ref, pre_ref):
        t = lax.broadcasted_iota(jnp.int32, (s, pg), 0)
        for gi, win in enumerate(POOL_WINDOWS):
            cols = slice(gi * pg, (gi + 1) * pg)
            u = u_ref[:, cols]
            a = u
            k = 1
            while k < win:
                a = a + _shift_down(a, k, t)
                k *= 2
            div = jnp.minimum(t + 1, win).astype(F32)
            pre = (a / div - u).astype(_MXU)
            pre_ref[:, cols] = pre
            out = jnp.dot(pre, w_ref[gi], preferred_element_type=F32)
            y_ref[:, cols] = (out * sc_ref[:, cols]).astype(y_ref.dtype)

    blk = pl.BlockSpec((s, pw_), lambda i: (0, 0))
    return pl.pallas_call(
        body, grid=(1,),
        in_specs=[pl.BlockSpec((s, pw_), lambda i: (0, col_block)), pl.BlockSpec((ng, pg, pg), lambda i: (0, 0, 0)),
                  pl.BlockSpec((1, pw_), lambda i: (0, 0))],
        out_specs=[blk, blk], out_shape=[_sds((s, pw_), _MXU), _sds((s, pw_), _MXU)], compiler_params=_cp(), name=name,
    )(h, pool_w, pool_scale)


def _pool_bwd(name, dmix, col_block, pre, pool_w, pool_scale):
    s = pre.shape[0]
    ng, pg = pool_w.shape[0], pool_w.shape[1]
    pw_ = ng * pg

    def body(dy_ref, pre_ref, w_ref, sc_ref, du_ref, dw_ref, dsc_ref):
        t = lax.broadcasted_iota(jnp.int32, (s, pg), 0)
        for gi, win in enumerate(POOL_WINDOWS):
            cols = slice(gi * pg, (gi + 1) * pg)
            pre_g = pre_ref[:, cols]
            dy = dy_ref[:, cols]
            out = jnp.dot(pre_g, w_ref[gi], preferred_element_type=F32)
            dsc_ref[:, cols] = jnp.sum(dy * out, 0, keepdims=True)
            dout = (dy * sc_ref[:, cols]).astype(_MXU)
            dw_ref[gi] = lax.dot_general(pre_g, dout, (TN, ((), ())), preferred_element_type=F32)
            dpre = lax.dot_general(dout, w_ref[gi], (NT, ((), ())), preferred_element_type=F32)
            div = jnp.minimum(t + 1, win).astype(F32)
            a = dpre / div
            k = 1
            while k < win:
                a = a + _shift_up(a, k, t)
                k *= 2
            du_ref[:, cols] = (a - dpre).astype(du_ref.dtype)

    blk = pl.BlockSpec((s, pw_), lambda i: (0, 0))
    wspec = pl.BlockSpec((ng, pg, pg), lambda i: (0, 0, 0))
    vec = pl.BlockSpec((1, pw_), lambda i: (0, 0))
    return pl.pallas_call(
        body, grid=(1,), in_specs=[pl.BlockSpec((s, pw_), lambda i: (0, col_block)), blk, wspec, vec],
        out_specs=[blk, wspec, vec], out_shape=[_sds((s, pw_), _MXU), _sds((ng, pg, pg), F32), _sds((1, pw_), F32)],
        compiler_params=_cp(), name=name,
    )(dmix, pre, pool_w, pool_scale)


def _scan_chunks(xr_ref, xi_ref, sr_ref, si_ref, ar, ai, reverse):
    n, c = xr_ref.shape
    tt = min(SCAN_T, n)
    lw = min(SCAN_LANES, c)
    nchunk = n // tt
    t = lax.broadcasted_iota(jnp.int32, (tt, lw), 0)

    for l0 in range(0, c, lw):
        lanes = slice(l0, l0 + lw)
        a_r, a_i = ar[:, lanes], ai[:, lanes]

        def local(vr, vi, a_r=a_r, a_i=a_i):
            pr, pi = a_r, a_i
            k = 1
            while k < tt:
                if reverse:
                    hr, hi = _shift_up(vr, k, t), _shift_up(vi, k, t)
                else:
                    hr, hi = _shift_down(vr, k, t), _shift_down(vi, k, t)
                vr, vi = vr + pr * hr - pi * hi, vi + pr * hi + pi * hr
                pr, pi = pr * pr - pi * pi, 2.0 * pr * pi
                k *= 2
            return vr, vi

        edge = tt - 1 if reverse else 0
        pw_r, pw_i = local(jnp.where(t == edge, a_r, 0.0), jnp.where(t == edge, a_i, 0.0))
        last = 0 if reverse else tt - 1

        def body(i, carry, lanes=lanes, local=local, pw_r=pw_r, pw_i=pw_i):
            cr, ci = carry
            ch = nchunk - 1 - i if reverse else i
            rows = pl.ds(pl.multiple_of(ch * tt, tt), tt)
            vr, vi = local(xr_ref[rows, lanes], xi_ref[rows, lanes])
            vr2 = vr + pw_r * cr - pw_i * ci
            vi2 = vi + pw_r * ci + pw_i * cr
            sr_ref[rows, lanes] = vr2
            si_ref[rows, lanes] = vi2
            return vr2[last:last + 1, :], vi2[last:last + 1, :]

        lax.fori_loop(0, nchunk, body, (jnp.zeros((1, lw), F32), jnp.zeros((1, lw), F32)))


_GELU_K = math.sqrt(2.0 / math.pi)


def _gelu_grad(y):
    inner = _GELU_K * (y + 0.044715 * y * y * y)
    th = jnp.tanh(inner)
    return 0.5 * (1.0 + th) + 0.5 * y * (1.0 - th * th) * _GELU_K * (1.0 + 3.0 * 0.044715 * y * y)


def _ssm_fwd(name, h, u_block0, bdr, bdi, cdr, cdi, dvec, ar, ai):
    s = h.shape[0]
    nt, cw, lw = bdr.shape
    rc = min(256, s)

    def body(u_ref, bdr_ref, bdi_ref, cdr_ref, cdi_ref, d_ref, ar_ref, ai_ref, sr_ref, si_ref, y_ref, yg_ref):
        def mm_in(c, _):
            rows = pl.ds(pl.multiple_of(c * rc, rc), rc)
            ub = u_ref[rows, :].astype(_MXU)
            sr_ref[rows, :] = jnp.dot(ub, bdr_ref[...], preferred_element_type=F32)
            si_ref[rows, :] = jnp.dot(ub, bdi_ref[...], preferred_element_type=F32)
            return 0

        lax.fori_loop(0, s // rc, mm_in, 0)
        _scan_chunks(sr_ref, si_ref, sr_ref, si_ref, ar_ref[...], ai_ref[...], reverse=False)

        def mm_out(c, _):
            rows = pl.ds(pl.multiple_of(c * rc, rc), rc)
            y = (jnp.dot(sr_ref[rows, :].astype(_MXU), cdr_ref[...], preferred_element_type=F32)
                 - jnp.dot(si_ref[rows, :].astype(_MXU), cdi_ref[...], preferred_element_type=F32)
                 + d_ref[...] * u_ref[rows, :])
            y_ref[rows, :] = y
            yg_ref[rows, :] = jax.nn.gelu(y).astype(yg_ref.dtype)
            return 0

        lax.fori_loop(0, s // rc, mm_out, 0)

    st = pl.BlockSpec((s, lw), lambda j: (0, j))
    ch = pl.BlockSpec((s, cw), lambda j: (0, j))
    bspec = pl.BlockSpec((None, cw, lw), lambda j: (j, 0, 0))
    cspec = pl.BlockSpec((None, lw, cw), lambda j: (j, 0, 0))
    return pl.pallas_call(
        body, grid=(nt,),
        in_specs=[pl.BlockSpec((s, cw), lambda j: (0, u_block0 + j)), bspec, bspec, cspec, cspec,
                  pl.BlockSpec((1, cw), lambda j: (0, j)), pl.BlockSpec((1, lw), lambda j: (0, j)),
                  pl.BlockSpec((1, lw), lambda j: (0, j))],
        out_specs=[st, st, ch, ch],
        out_shape=[_sds((s, nt * lw), F32), _sds((s, nt * lw), F32), _sds((s, nt * cw), F32), _sds((s, nt * cw), _MXU)],
        compiler_params=_cp(), name=name,
    )(h, bdr, bdi, cdr, cdi, dvec, ar, ai)


def _ssm_bwd(name, dyg, ypre, h, u_block0, sr, si, bdr, bdi, cdr, cdi, dvec, ar, ai):
    s = h.shape[0]
    nt, cw, lw = bdr.shape
    rc = min(256, s)

    def body(dyg_ref, yp_ref, u_ref, sr_ref, si_ref, bdr_ref, bdi_ref, cdr_ref, cdi_ref, d_ref, ar_ref, ai_ref,
             du_ref, dd_ref, dcr_ref, dci_ref, dbr_ref, dbi_ref, dar_ref, dai_ref, lr_scr, li_scr, dy_scr):
        for ref in (dd_ref, dcr_ref, dci_ref, dbr_ref, dbi_ref, dar_ref, dai_ref):
            ref[...] = jnp.zeros_like(ref)

        def p1(c, _):
            rows = pl.ds(pl.multiple_of(c * rc, rc), rc)
            dy = dyg_ref[rows, :] * _gelu_grad(yp_ref[rows, :])
            dy_scr[rows, :] = dy
            dd_ref[...] += jnp.sum(dy * u_ref[rows, :], 0, keepdims=True)
            dyb = dy.astype(_MXU)
            lr_scr[rows, :] = lax.dot_general(dyb, cdr_ref[...], (NT, ((), ())), preferred_element_type=F32)
            li_scr[rows, :] = -lax.dot_general(dyb, cdi_ref[...], (NT, ((), ())), preferred_element_type=F32)
            dcr_ref[...] += lax.dot_general(sr_ref[rows, :].astype(_MXU), dyb, (TN, ((), ())), preferred_element_type=F32)
            dci_ref[...] -= lax.dot_general(si_ref[rows, :].astype(_MXU), dyb, (TN, ((), ())), preferred_element_type=F32)
            return 0

        lax.fori_loop(0, s // rc, p1, 0)
        _scan_chunks(lr_scr, li_scr, lr_scr, li_scr, ar_ref[...], -ai_ref[...], reverse=True)
        t = lax.broadcasted_iota(jnp.int32, (rc, lw), 0)

        def p2(c, _):
            r0 = pl.multiple_of(c * rc, rc)
            rows = pl.ds(r0, rc)
            before = pl.ds(pl.multiple_of(jnp.maximum(r0 - 8, 0), 8), 8)
            have = (c > 0).astype(F32)
            lr, li = lr_scr[rows, :], li_scr[rows, :]
            spr = jnp.where(t == 0, sr_ref[before, :][7:8, :] * have, pltpu.roll(sr_ref[rows, :], 1, 0))
            spi = jnp.where(t == 0, si_ref[before, :][7:8, :] * have, pltpu.roll(si_ref[rows, :], 1, 0))
            dar_ref[...] += jnp.sum(lr * spr + li * spi, 0, keepdims=True)
            dai_ref[...] += jnp.sum(li * spr - lr * spi, 0, keepdims=True)
            lrb, lib = lr.astype(_MXU), li.astype(_MXU)
            du = (dy_scr[rows, :] * d_ref[...]
                  + lax.dot_general(lrb, bdr_ref[...], (NT, ((), ())), preferred_element_type=F32)
                  + lax.dot_general(lib, bdi_ref[...], (NT, ((), ())), preferred_element_type=F32))
            du_ref[rows, :] = du.astype(du_ref.dtype)
            ub = u_ref[rows, :].astype(_MXU)
            dbr_ref[...] += lax.dot_general(ub, lrb, (TN, ((), ())), preferred_element_type=F32)
            dbi_ref[...] += lax.dot_general(ub, lib, (TN, ((), ())), preferred_element_type=F32)
            return 0

        lax.fori_loop(0, s // rc, p2, 0)

    st = pl.BlockSpec((s, lw), lambda j: (0, j))
    ch = pl.BlockSpec((s, cw), lambda j: (0, j))
    bspec = pl.BlockSpec((None, cw, lw), lambda j: (j, 0, 0))
    cspec = pl.BlockSpec((None, lw, cw), lambda j: (j, 0, 0))
    cvec = pl.BlockSpec((1, cw), lambda j: (0, j))
    svec = pl.BlockSpec((1, lw), lambda j: (0, j))
    return pl.pallas_call(
        body, grid=(nt,),
        in_specs=[ch, ch, pl.BlockSpec((s, cw), lambda j: (0, u_block0 + j)), st, st, bspec, bspec, cspec, cspec, cvec, svec, svec],
        out_specs=[ch, cvec, cspec, cspec, bspec, bspec, svec, svec],
        out_shape=[_sds((s, nt * cw), _MXU), _sds((1, nt * cw), F32), _sds((nt, lw, cw), F32), _sds((nt, lw, cw), F32),
                   _sds((nt, cw, lw), F32), _sds((nt, cw, lw), F32), _sds((1, nt * lw), F32), _sds((1, nt * lw), F32)],
        scratch_shapes=[pltpu.VMEM((s, lw), F32), pltpu.VMEM((s, lw), F32), pltpu.VMEM((s, cw), F32)],
        compiler_params=_cp(), name=name,
    )(dyg, ypre, h, sr, si, bdr, bdi, cdr, cdi, dvec, ar, ai)


def _glu_fwd(name, yg, gw):
    s, w = yg.shape
    tr = min(512, s)

    def body(y_ref, w_ref, o_ref, ab_ref):
        ab = jnp.dot(y_ref[...], w_ref[...], preferred_element_type=F32)
        ab_ref[...] = ab
        o_ref[...] = (ab[:, :w] * jax.nn.sigmoid(ab[:, w:])).astype(o_ref.dtype)

    return pl.pallas_call(
        body, grid=(s // tr,), in_specs=[pl.BlockSpec((tr, w), lambda i: (i, 0)), _resident((w, 2 * w), lambda i: (0, 0))],
        out_specs=[pl.BlockSpec((tr, w), lambda i: (i, 0)), pl.BlockSpec((tr, 2 * w), lambda i: (i, 0))],
        out_shape=[_sds((s, w), _MXU), _sds((s, 2 * w), F32)], compiler_params=_cp(), name=name,
    )(yg, gw)


def _glu_bwd(name, dmix, col_block, ab, gw):
    s = ab.shape[0]
    w = ab.shape[1] // 2
    tr = min(512, s)

    def body(do_ref, ab_ref, w_ref, dab_ref, dy_ref):
        do = do_ref[...]
        a, b = ab_ref[:, :w], ab_ref[:, w:]
        sg = jax.nn.sigmoid(b)
        da = (do * sg).astype(_MXU)
        db = (do * a * sg * (1.0 - sg)).astype(_MXU)
        dab_ref[:, :w] = da
        dab_ref[:, w:] = db
        dy_ref[...] = (lax.dot_general(da, w_ref[:, :w], (NT, ((), ())), preferred_element_type=F32)
                       + lax.dot_general(db, w_ref[:, w:], (NT, ((), ())), preferred_element_type=F32))

    return pl.pallas_call(
        body, grid=(s // tr,),
        in_specs=[pl.BlockSpec((tr, w), lambda i: (i, col_block)), pl.BlockSpec((tr, 2 * w), lambda i: (i, 0)),
                  _resident((w, 2 * w), lambda i: (0, 0))],
        out_specs=[pl.BlockSpec((tr, 2 * w), lambda i: (i, 0)), pl.BlockSpec((tr, w), lambda i: (i, 0))],
        out_shape=[_sds((s, 2 * w), _MXU), _sds((s, w), F32)], compiler_params=_cp(), name=name,
    )(dmix, ab, gw)


def _conv(hu, w_ref, b_ref, t):
    hc = b_ref[...] + _shift_down(hu, 2, t) * w_ref[0:1, :]
    hc = hc + _shift_down(hu, 1, t) * w_ref[1:2, :]
    return hc + hu * w_ref[2:3, :]


def _conv_act_fwd(name, hu, cw, cb):
    s, f2 = hu.shape
    f = f2 // 2
    tw = _col_tile(f, 256)
    nt = f // tw

    def body(v_ref, g_ref, wv_ref, wg_ref, bv_ref, bg_ref, act_ref):
        t = lax.broadcasted_iota(jnp.int32, (s, tw), 0)
        val = _conv(v_ref[...], wv_ref, bv_ref, t)
        gate = _conv(g_ref[...], wg_ref, bg_ref, t)
        act_ref[...] = (jax.nn.silu(gate) * val).astype(act_ref.dtype)

    cv = lambda rows: pl.BlockSpec((rows, tw), lambda i: (0, i))
    cg = lambda rows: pl.BlockSpec((rows, tw), lambda i: (0, nt + i))
    return pl.pallas_call(
        body, grid=(nt,), in_specs=[cv(s), cg(s), cv(CONV_WIDTH), cg(CONV_WIDTH), cv(1), cg(1)],
        out_specs=cv(s), out_shape=_sds((s, f), _MXU), compiler_params=_cp(), name=name,
    )(hu, hu, cw, cw, cb, cb)


def _conv_act_bwd(name, dact, hu, cw, cb):
    s, f2 = hu.shape
    f = f2 // 2
    tw = _col_tile(f, 256)
    nt = f // tw

    def body(da_ref, v_ref, g_ref, wv_ref, wg_ref, bv_ref, bg_ref, dv_ref, dg_ref, dwv_ref, dwg_ref, dbv_ref, dbg_ref):
        t = lax.broadcasted_iota(jnp.int32, (s, tw), 0)
        hv, hg = v_ref[...], g_ref[...]
        val = _conv(hv, wv_ref, bv_ref, t)
        gate = _conv(hg, wg_ref, bg_ref, t)
        sg = jax.nn.sigmoid(gate)
        da = da_ref[...]
        dval = da * (gate * sg)
        dgate = da * val * sg * (1.0 + gate * (1.0 - sg))
        for dhc, hu_, w_ref, dh_ref, dw_ref, db_ref in ((dval, hv, wv_ref, dv_ref, dwv_ref, dbv_ref),
                                                       (dgate, hg, wg_ref, dg_ref, dwg_ref, dbg_ref)):
            db_ref[...] = jnp.sum(dhc, 0, keepdims=True)
            dw_ref[0:1, :] = jnp.sum(dhc * _shift_down(hu_, 2, t), 0, keepdims=True)
            dw_ref[1:2, :] = jnp.sum(dhc * _shift_down(hu_, 1, t), 0, keepdims=True)
            dw_ref[2:3, :] = jnp.sum(dhc * hu_, 0, keepdims=True)
            dh = dhc * w_ref[2:3, :] + _shift_up(dhc, 1, t) * w_ref[1:2, :] + _shift_up(dhc, 2, t) * w_ref[0:1, :]
            dh_ref[...] = dh.astype(dh_ref.dtype)

    cv = lambda rows: pl.BlockSpec((rows, tw), lambda i: (0, i))
    cg = lambda rows: pl.BlockSpec((rows, tw), lambda i: (0, nt + i))
    outs = pl.pallas_call(
        body, grid=(nt,), in_specs=[cv(s), cv(s), cg(s), cv(CONV_WIDTH), cg(CONV_WIDTH), cv(1), cg(1)],
        out_specs=[cv(s), cv(s), cv(CONV_WIDTH), cv(CONV_WIDTH), cv(1), cv(1)],
        out_shape=[_sds((s, f), _MXU), _sds((s, f), _MXU), _sds((CONV_WIDTH, f), F32), _sds((CONV_WIDTH, f), F32),
                   _sds((1, f), F32), _sds((1, f), F32)],
        compiler_params=_cp(), name=name,
    )(dact, hu, hu, cw, cw, cb, cb)
    dv, dg, dwv, dwg, dbv, dbg = outs
    return jnp.concatenate([dv, dg], 1), jnp.concatenate([dwv, dwg], 1), jnp.concatenate([dbv, dbg], 1)


def _elem_tiles(r, c):
    tc = _col_tile(c, 1024)
    budget = 256 * 1024
    tr = r
    if r * tc > budget:
        cands = [d for d in range(8, r, 8) if r % d == 0 and d * tc <= budget]
        tr = max(cands) if cands else r
    return tr, tc


def _reduce_adamw(name, parts, w, m, v):
    n, nl, r, c = parts.shape
    tr, tc = _elem_tiles(r, c)
    c1 = 1.0 - ADAM_B1 ** ADAM_STEP
    c2 = 1.0 - ADAM_B2 ** ADAM_STEP

    def body(p_ref, w_ref, m_ref, v_ref, g_ref, d_ref, nm_ref, nv_ref):
        g = p_ref[0].astype(F32)
        for i in range(1, n):
            g = g + p_ref[i].astype(F32)
        nm = ADAM_B1 * m_ref[...] + (1.0 - ADAM_B1) * g
        nv = ADAM_B2 * v_ref[...] + (1.0 - ADAM_B2) * (g * g)
        m_hat = nm / c1
        v_hat = nv / c2
        g_ref[...] = g
        nm_ref[...] = nm
        nv_ref[...] = nv
        d_ref[...] = -ADAM_LR * (m_hat / (jnp.sqrt(v_hat) + ADAM_EPS) + ADAM_WD * w_ref[...])

    blk = pl.BlockSpec((None, tr, tc), lambda l, i, j: (l, i, j))
    out = _sds((nl, r, c), F32)
    return pl.pallas_call(
        body, grid=(nl, r // tr, c // tc),
        in_specs=[pl.BlockSpec((n, None, tr, tc), lambda l, i, j: (0, l, i, j)), blk, blk, blk],
        out_specs=[blk, blk, blk, blk], out_shape=[out, out, out, out], compiler_params=_cp(), name=name,
    )(parts, w, m, v)


def _pair_sum(name, mine, theirs, c_idx):
    _, _, r, c = mine.shape
    tr, tc = _elem_tiles(r, c)

    def body(c_ref, a_ref, b_ref, o_ref):
        o_ref[...] = (a_ref[...].astype(F32) + b_ref[...].astype(F32)).astype(o_ref.dtype)

    return pl.pallas_call(
        body,
        grid_spec=pltpu.PrefetchScalarGridSpec(
            num_scalar_prefetch=1, grid=(4, r // tr, c // tc),
            in_specs=[pl.BlockSpec((None, None, tr, tc), lambda p, i, j, cref: (p, cref[0], i, j)),
                      pl.BlockSpec((None, tr, tc), lambda p, i, j, cref: (p, i, j))],
            out_specs=pl.BlockSpec((None, tr, tc), lambda p, i, j, cref: (p, i, j))),
        out_shape=_sds((4, r, c), _WIRE), compiler_params=_cp(), name=name,
    )(c_idx, mine, theirs)


def _place():
    return lax.axis_index("x"), lax.axis_index("y"), lax.axis_index("c")


def _all_gather(name, xs):
    n = len(xs)

    def body(*refs):
        x_refs, o_refs = refs[:n], refs[n:2 * n]
        send_sems, recv_sems, local_sems = refs[2 * n:]
        x, y, c = _place()
        me, sibling = (x, y, c), (x, y, 1 - c)
        chips = [(1 - x, y), (x, 1 - y), (1 - x, 1 - y)]

        def copy(a, k, block, to, src=None):
            px, py, pc = block
            rows = o_refs[a].at[4 * px + 2 * py + pc]
            return pltpu.make_async_remote_copy(
                src_ref=rows if src is None else src, dst_ref=rows, send_sem=send_sems.at[a, k], recv_sem=recv_sems.at[a, k],
                device_id=to, device_id_type=MESH)

        sent = []
        mine = []
        for a in range(n):
            mx, my, mc = me
            cp = pltpu.make_async_copy(x_refs[a], o_refs[a].at[4 * mx + 2 * my + mc], local_sems.at[a])
            cp.start()
            mine.append(cp)
            first = [copy(a, 0, me, sibling, src=x_refs[a])]
            first += [copy(a, 1 + j, me, (*chip, c), src=x_refs[a]) for j, chip in enumerate(chips)]
            for cp in first:
                cp.start()
            sent += first
        for a in range(n):
            for j, chip in enumerate(chips):
                copy(a, 1 + j, (*chip, c), me).wait_recv()
                fwd = copy(a, 4 + j, (*chip, c), sibling)
                fwd.start()
                sent.append(fwd)
        for a in range(n):
            copy(a, 0, sibling, me).wait_recv()
            for j, chip in enumerate(chips):
                copy(a, 4 + j, (*chip, 1 - c), me).wait_recv()
        for cp in sent:
            cp.wait_send()
        for cp in mine:
            cp.wait()

    return pl.pallas_call(
        body, in_specs=[ANY] * n, out_specs=[ANY] * n,
        out_shape=[_sds((N_DEV,) + a.shape, a.dtype) for a in xs],
        scratch_shapes=[pltpu.SemaphoreType.DMA((n, 7)), pltpu.SemaphoreType.DMA((n, 7)), pltpu.SemaphoreType.DMA((n,))],
        name=name,
    )(*xs)


def _swap_sibling(name, xs):
    n = len(xs)

    def body(*refs):
        x_refs, o_refs = refs[:n], refs[n:2 * n]
        send_sems, recv_sems = refs[2 * n:]
        x, y, c = _place()
        sibling = (x, y, 1 - c)
        cps = []
        for a in range(n):
            for p in range(4):
                cp = pltpu.make_async_remote_copy(
                    src_ref=x_refs[a].at[p, 1 - c], dst_ref=o_refs[a].at[p], send_sem=send_sems.at[a, p],
                    recv_sem=recv_sems.at[a, p], device_id=sibling, device_id_type=MESH)
                cp.start()
                cps.append(cp)
        for cp in cps:
            cp.wait_recv()
        for cp in cps:
            cp.wait_send()

    return pl.pallas_call(
        body, in_specs=[ANY] * n, out_specs=[ANY] * n,
        out_shape=[_sds((4,) + a.shape[2:], a.dtype) for a in xs],
        scratch_shapes=[pltpu.SemaphoreType.DMA((n, 4)), pltpu.SemaphoreType.DMA((n, 4))],
        name=name,
    )(*xs)


def _swap_chips(name, xs):
    n = len(xs)

    def body(*refs):
        x_refs, o_refs = refs[:n], refs[n:2 * n]
        send_sems, recv_sems, local_sems = refs[2 * n:]
        x, y, c = _place()
        my_chip = 2 * x + y
        chips = [(1 - x, y), (x, 1 - y), (1 - x, 1 - y)]
        cps, own = [], []
        for a in range(n):
            cp = pltpu.make_async_copy(x_refs[a].at[my_chip], o_refs[a].at[my_chip], local_sems.at[a])
            cp.start()
            own.append(cp)
            for j, (px, py) in enumerate(chips):
                cp = pltpu.make_async_remote_copy(
                    src_ref=x_refs[a].at[2 * px + py], dst_ref=o_refs[a].at[my_chip], send_sem=send_sems.at[a, j],
                    recv_sem=recv_sems.at[a, j], device_id=(px, py, c), device_id_type=MESH)
                cp.start()
                cps.append((a, j, px, py, cp))
        for a, j, px, py, cp in cps:
            pltpu.make_async_remote_copy(
                src_ref=x_refs[a].at[my_chip], dst_ref=o_refs[a].at[2 * px + py], send_sem=send_sems.at[a, j],
                recv_sem=recv_sems.at[a, j], device_id=(px, py, c), device_id_type=MESH).wait_recv()
        for a, j, px, py, cp in cps:
            cp.wait_send()
        for cp in own:
            cp.wait()

    return pl.pallas_call(
        body, in_specs=[ANY] * n, out_specs=[ANY] * n, out_shape=[_sds(a.shape, a.dtype) for a in xs],
        scratch_shapes=[pltpu.SemaphoreType.DMA((n, 3)), pltpu.SemaphoreType.DMA((n, 3)), pltpu.SemaphoreType.DMA((n,))],
        name=name,
    )(*xs)


def _pad_pairs(a, axis, half, half_pad):
    shp = a.shape
    a = a.reshape(shp[:axis] + (2, half) + shp[axis + 1:])
    pad = [(0, 0)] * a.ndim
    pad[axis + 1] = (0, half_pad - half)
    a = jnp.pad(a, pad)
    return a.reshape(shp[:axis] + (2 * half_pad,) + shp[axis + 1:])


def _unpad_pairs(a, axis, half, half_pad):
    shp = a.shape
    a = a.reshape(shp[:axis] + (2, half_pad) + shp[axis + 1:])
    a = lax.slice_in_dim(a, 0, half, axis=axis + 1)
    return a.reshape(shp[:axis] + (2 * half,) + shp[axis + 1:])


def _blockdiag(w, nt):
    g, a, b = w.shape
    gl = g // nt
    e = jnp.eye(gl, dtype=w.dtype).reshape(1, gl, 1, gl, 1)
    return (w.reshape(nt, gl, a, 1, b) * e).reshape(nt, gl * a, gl * b)


def _diagblocks(m, g, a, b):
    nt = m.shape[0]
    gl = g // nt
    d = jnp.diagonal(m.reshape(nt, gl, a, gl, b), axis1=1, axis2=3)
    return jnp.moveaxis(d, -1, 1).reshape(g, a, b)


def _ssm_discretise(lam_re, lam_im, log_dt, b_re, b_im):
    dt = jnp.exp(log_dt)[:, None]
    mag = jnp.exp(lam_re * dt)
    ab_re, ab_im = mag * jnp.cos(lam_im * dt), mag * jnp.sin(lam_im * dt)
    nr, ni = ab_re - 1.0, ab_im
    den = lam_re * lam_re + lam_im * lam_im
    zr = (nr * lam_re + ni * lam_im) / den
    zi = (ni * lam_re - nr * lam_im) / den
    bbr = zr[..., None] * b_re - zi[..., None] * b_im
    bbi = zr[..., None] * b_im + zi[..., None] * b_re
    return ab_re, ab_im, bbr, bbi


def _rope_tables(s):
    half = HEAD_DIM // 2
    inv = ROPE_THETA ** (-jnp.arange(half, dtype=F32) / half)
    ang = jnp.arange(s).astype(F32)[:, None] * inv[None, :]
    cos, sin = jnp.cos(ang), jnp.sin(ang)
    reps = V7X_LANES // HEAD_DIM
    return jnp.tile(jnp.concatenate([cos, cos], -1), (1, reps)), jnp.tile(jnp.concatenate([-sin, sin], -1), (1, reps))


_SMALL = ("attn_sinks", "pool_w", "pool_scale", "ssm_lam_re", "ssm_lam_im", "ssm_log_dt", "ssm_b_re", "ssm_b_im",
          "ssm_c_re", "ssm_c_im", "ssm_d", "ln1_g", "ln1_b", "ffn_conv_b", "ln2_g", "ln2_b")
_BIG = ("w_in", "ssm_glu_w", "w_out", "ffn_w_up", "ffn_w_down")
_ORDER = ("w_in", "attn_sinks", "pool_w", "pool_scale", "ssm_lam_re", "ssm_lam_im", "ssm_log_dt", "ssm_b_re", "ssm_b_im",
          "ssm_c_re", "ssm_c_im", "ssm_d", "ssm_glu_w", "w_out", "ln1_g", "ln1_b", "ffn_w_up", "ffn_conv_w", "ffn_conv_b",
          "ffn_w_down", "ln2_g", "ln2_b")


def kernel(x, w_in, attn_sinks, pool_w, pool_scale, ssm_lam_re, ssm_lam_im, ssm_log_dt, ssm_b_re, ssm_b_im, ssm_c_re, ssm_c_im, ssm_d, ssm_glu_w, w_out, ln1_g, ln1_b, ffn_w_up, ffn_conv_w, ffn_conv_b, ffn_w_down, ln2_g, ln2_b, loss_target, m_w_in, m_attn_sinks, m_pool_w, m_pool_scale, m_ssm_lam_re, m_ssm_lam_im, m_ssm_log_dt, m_ssm_b_re, m_ssm_b_im, m_ssm_c_re, m_ssm_c_im, m_ssm_d, m_ssm_glu_w, m_w_out, m_ln1_g, m_ln1_b, m_ffn_w_up, m_ffn_conv_w, m_ffn_conv_b, m_ffn_w_down, m_ln2_g, m_ln2_b, v_w_in, v_attn_sinks, v_pool_w, v_pool_scale, v_ssm_lam_re, v_ssm_lam_im, v_ssm_log_dt, v_ssm_b_re, v_ssm_b_im, v_ssm_c_re, v_ssm_c_im, v_ssm_d, v_ssm_glu_w, v_w_out, v_ln1_g, v_ln1_b, v_ffn_w_up, v_ffn_conv_w, v_ffn_conv_b, v_ffn_w_down, v_ln2_g, v_ln2_b):
    W = dict(w_in=w_in, attn_sinks=attn_sinks, pool_w=pool_w, pool_scale=pool_scale, ssm_lam_re=ssm_lam_re, ssm_lam_im=ssm_lam_im, ssm_log_dt=ssm_log_dt, ssm_b_re=ssm_b_re, ssm_b_im=ssm_b_im, ssm_c_re=ssm_c_re, ssm_c_im=ssm_c_im, ssm_d=ssm_d, ssm_glu_w=ssm_glu_w, w_out=w_out, ln1_g=ln1_g, ln1_b=ln1_b, ffn_w_up=ffn_w_up, ffn_conv_w=ffn_conv_w, ffn_conv_b=ffn_conv_b, ffn_w_down=ffn_w_down, ln2_g=ln2_g, ln2_b=ln2_b)
    M = dict(w_in=m_w_in, attn_sinks=m_attn_sinks, pool_w=m_pool_w, pool_scale=m_pool_scale, ssm_lam_re=m_ssm_lam_re, ssm_lam_im=m_ssm_lam_im, ssm_log_dt=m_ssm_log_dt, ssm_b_re=m_ssm_b_re, ssm_b_im=m_ssm_b_im, ssm_c_re=m_ssm_c_re, ssm_c_im=m_ssm_c_im, ssm_d=m_ssm_d, ssm_glu_w=m_ssm_glu_w, w_out=m_w_out, ln1_g=m_ln1_g, ln1_b=m_ln1_b, ffn_w_up=m_ffn_w_up, ffn_conv_w=m_ffn_conv_w, ffn_conv_b=m_ffn_conv_b, ffn_w_down=m_ffn_w_down, ln2_g=m_ln2_g, ln2_b=m_ln2_b)
    V = dict(w_in=v_w_in, attn_sinks=v_attn_sinks, pool_w=v_pool_w, pool_scale=v_pool_scale, ssm_lam_re=v_ssm_lam_re, ssm_lam_im=v_ssm_lam_im, ssm_log_dt=v_ssm_log_dt, ssm_b_re=v_ssm_b_re, ssm_b_im=v_ssm_b_im, ssm_c_re=v_ssm_c_re, ssm_c_im=v_ssm_c_im, ssm_d=v_ssm_d, ssm_glu_w=v_ssm_glu_w, w_out=v_w_out, ln1_g=v_ln1_g, ln1_b=v_ln1_b, ffn_w_up=v_ffn_w_up, ffn_conv_w=v_ffn_conv_w, ffn_conv_b=v_ffn_conv_b, ffn_w_down=v_ffn_w_down, ln2_g=v_ln2_g, ln2_b=v_ln2_b)

    depth = w_in.shape[0]
    s, d = x.shape[1], x.shape[2]
    alpha = (2 * depth) ** 0.25
    attn_w = d // 2
    kv_w = attn_w // GQA
    nkv = kv_w // HEAD_DIM
    pool_wd = d // 4
    ssm_wd = d // 4
    n_groups = ssm_wd // SSM_GROUP
    state_w = n_groups * SSM_STATE
    nt_ssm = max(1, state_w // 512)
    o_k, o_v, o_p, o_s = attn_w, attn_w + kv_w, attn_w + 2 * kv_w, attn_w + 2 * kv_w + pool_wd
    in_w = o_s + ssm_wd
    half = ffn_w_down.shape[1]
    half_pad = -(-half // 64) * 64
    ffp = 4 * 2 * half_pad
    xi, yi, ci = _place()
    me = 4 * xi + 2 * yi + ci
    c_idx = jnp.reshape(ci, (1,)).astype(jnp.int32)

    cos_t, sin_t = _rope_tables(s)

    def gather_layer(l):
        shards = [
            w_in[l].astype(_WIRE), ssm_glu_w[l].astype(_WIRE), w_out[l].astype(_WIRE),
            _pad_pairs(ffn_w_up[l].astype(_WIRE), 1, half, half_pad),
            jnp.pad(ffn_w_down[l].astype(_WIRE), ((0, half_pad - half), (0, 0))),
            _pad_pairs(ffn_conv_w[l], 1, half, half_pad),
        ]
        g_in, g_glu, g_out, g_up, g_down, g_cw = _all_gather(f"gather_weights_{l}", shards)
        return dict(
            win=jnp.transpose(g_in, (1, 0, 2)).reshape(d, in_w),
            glu=jnp.transpose(g_glu, (1, 0, 2)).reshape(ssm_wd, 2 * ssm_wd),
            wout=g_out.reshape(d, d),
            wup=g_up,
            wdown=g_down.reshape(ffp, d),
            cw=jnp.transpose(g_cw, (1, 0, 2)).reshape(CONV_WIDTH, 2 * ffp),
            cb=_pad_pairs(ffn_conv_b[l].reshape(N_DEV, 2 * half), 1, half, half_pad).reshape(1, 2 * ffp),
        )

    full = [gather_layer(l) for l in range(depth)]

    def ssm_params(l):
        return (ssm_lam_re[l], ssm_lam_im[l], ssm_log_dt[l], ssm_b_re[l], ssm_b_im[l])

    saved = []
    xf = x[0]
    xb = xf.astype(_MXU)
    for l in range(depth):
        fw = full[l]
        h = _mm_nn(f"in_proj_{l}", xb, fw["win"])
        q_rot, k_rot = _rope(f"rope_{l}", h, o_v, cos_t, sin_t, _MXU, ((0, o_k), (o_k, o_v)))
        k_hm = jnp.transpose(k_rot.reshape(s, nkv, HEAD_DIM), (1, 0, 2))
        v_hm = jnp.transpose(h[:, o_v:o_p].astype(_MXU).reshape(s, nkv, HEAD_DIM), (1, 0, 2))
        sinks = attn_sinks[l].reshape(nkv, GQA)
        o_attn, lse = _attn_fwd(f"attn_{l}", q_rot, k_hm, v_hm, sinks)
        pw_b = pool_w[l].astype(_MXU)
        psc = pool_scale[l].reshape(1, pool_wd)
        y_pool, pre = _pool_fwd(f"pool_{l}", h, o_p // pool_wd, pw_b, psc)
        ab_re, ab_im, bbr, bbi = _ssm_discretise(*ssm_params(l))
        bdr = _blockdiag(jnp.transpose(bbr, (0, 2, 1)), nt_ssm).astype(_MXU)
        bdi = _blockdiag(jnp.transpose(bbi, (0, 2, 1)), nt_ssm).astype(_MXU)
        cdr = _blockdiag(jnp.transpose(ssm_c_re[l], (0, 2, 1)), nt_ssm).astype(_MXU)
        cdi = _blockdiag(jnp.transpose(ssm_c_im[l], (0, 2, 1)), nt_ssm).astype(_MXU)
        dvec = ssm_d[l].reshape(1, ssm_wd)
        ar, ai = ab_re.reshape(1, state_w), ab_im.reshape(1, state_w)
        cw_ssm = ssm_wd // nt_ssm
        sr, si, ypre, yg = _ssm_fwd(f"ssm_{l}", h, o_s // cw_ssm, bdr, bdi, cdr, cdi, dvec, ar, ai)
        y_ssm, ab2 = _glu_fwd(f"glu_{l}", yg, fw["glu"])
        mix = jnp.concatenate([o_attn.astype(_MXU), y_pool, y_ssm], -1)
        a1 = _mm_nn(f"out_proj_{l}", mix, fw["wout"])
        g1, b1 = ln1_g[l].reshape(1, d), ln1_b[l].reshape(1, d)
        x1, x1b, xh1, rs1 = _ln_fwd(f"ln1_{l}", xf, a1, g1, b1, alpha)
        wup = fw["wup"]
        hu = _mm(f"ffn_up_{l}", x1b, wup, NN, (N_DEV, 1), _resident((s, d), lambda j, kk: (0, 0)),
                 pl.BlockSpec((None, d, 2 * half_pad), lambda j, kk: (j, 0, 0)),
                 pl.BlockSpec((s, 2 * half_pad), lambda j, kk: (0, j)), (s, 2 * ffp), F32)
        act = _conv_act_fwd(f"ffn_act_{l}", hu, fw["cw"], fw["cb"])
        tkd = 2 * half_pad
        tnd = _col_tile(d, 1024)
        f_out = _mm(f"ffn_down_{l}", act, fw["wdown"], NN, (d // tnd, ffp // tkd),
                    pl.BlockSpec((s, tkd), lambda j, kk: (0, kk)), pl.BlockSpec((tkd, tnd), lambda j, kk: (kk, j)),
                    pl.BlockSpec((s, tnd), lambda j, kk: (0, j)), (s, d), F32)
        g2, b2 = ln2_g[l].reshape(1, d), ln2_b[l].reshape(1, d)
        x2, x2b, xh2, rs2 = _ln_fwd(f"ln2_{l}", x1, f_out, g2, b2, alpha)
        saved.append(dict(xb=xb, h=h, q_rot=q_rot, k_hm=k_hm, v_hm=v_hm, sinks=sinks, o_attn=o_attn, lse=lse, pw_b=pw_b, psc=psc,
                          pre=pre, bdr=bdr, bdi=bdi, cdr=cdr, cdi=cdi, dvec=dvec, ar=ar, ai=ai, sr=sr, si=si, ypre=ypre, yg=yg,
                          ab2=ab2, mix=mix, g1=g1, xh1=xh1, rs1=rs1, x1b=x1b, hu=hu, act=act, g2=g2, xh2=xh2, rs2=rs2))
        xf, xb = x2, x2b

    dy, loss_part = _loss_head("loss_head", xf, loss_target[0])
    loss = lax.psum(loss_part[0, 0], ("x", "y", "c"))

    small_g = {k: [None] * depth for k in _SMALL}
    cw_g = [None] * depth
    outs = {}
    big_res = {k: [None] * depth for k in _BIG}
    for l in reversed(range(depth)):
        fw, sv = full[l], saved[l]
        dr2, dr2b, dg2, db2 = _ln_bwd(f"ln2_bwd_{l}", dy, sv["xh2"], sv["rs2"], sv["g2"])
        d_wdown = _mm_tn_acols(f"ffn_down_dw_{l}", sv["act"], dr2b, _WIRE, cap=2 * half_pad)
        dact = _mm_nt(f"ffn_down_dx_{l}", dr2b, fw["wdown"], cap=2 * half_pad)
        dhu, dcw, dcb = _conv_act_bwd(f"ffn_act_bwd_{l}", dact, sv["hu"], fw["cw"], fw["cb"])
        d_wup = _mm(f"ffn_up_dw_{l}", sv["x1b"], dhu, TN, (N_DEV, 1), _resident((s, d), lambda j, kk: (0, 0)),
                    pl.BlockSpec((s, 2 * half_pad), lambda j, kk: (0, j)),
                    pl.BlockSpec((None, d, 2 * half_pad), lambda j, kk: (j, 0, 0)), (N_DEV, d, 2 * half_pad), _WIRE)
        tnd = _col_tile(d, 512)
        dy1 = _mm(f"ffn_up_dx_{l}", dhu, fw["wup"], NT, (d // tnd, N_DEV),
                  pl.BlockSpec((s, 2 * half_pad), lambda j, kk: (0, kk)),
                  pl.BlockSpec((None, tnd, 2 * half_pad), lambda j, kk: (kk, j, 0)),
                  pl.BlockSpec((s, tnd), lambda j, kk: (0, j)), (s, d), F32,
                  add=dr2, add_spec=pl.BlockSpec((s, tnd), lambda j, kk: (0, j)), add_scale=alpha)
        dr1, dr1b, dg1, db1 = _ln_bwd(f"ln1_bwd_{l}", dy1, sv["xh1"], sv["rs1"], sv["g1"])
        d_wout = _mm_tn_acols(f"out_proj_dw_{l}", sv["mix"], dr1b, _WIRE, cap=d // N_DEV)
        dmix = _mm_nt(f"out_proj_dx_{l}", dr1b, fw["wout"])
        dq_rot, dk_hm, dv_hm, dsk = _attn_bwd(f"attn_bwd_{l}", sv["q_rot"], sv["k_hm"], sv["v_hm"], sv["o_attn"], dmix,
                                             sv["lse"], sv["sinks"])
        dqk = jnp.concatenate([dq_rot, jnp.transpose(dk_hm, (1, 0, 2)).reshape(s, kv_w)], -1)
        dhq, dhk = _rope(f"rope_bwd_{l}", dqk, o_v, cos_t, -sin_t, _MXU, ((0, o_k), (o_k, o_v)))
        dhv = jnp.transpose(dv_hm, (1, 0, 2)).reshape(s, kv_w).astype(_MXU)
        dhp, dpw, dpsc = _pool_bwd(f"pool_bwd_{l}", dmix, attn_w // pool_wd, sv["pre"], sv["pw_b"], sv["psc"])
        dab2, dyg = _glu_bwd(f"glu_bwd_{l}", dmix, (attn_w + pool_wd) // ssm_wd, sv["ab2"], fw["glu"])
        d_glu = _mm_tn_bcols(f"glu_dw_{l}", sv["yg"], dab2, _WIRE)
        cw_ssm = ssm_wd // nt_ssm
        dhs, dd, dcdr, dcdi, dbdr, dbdi, dar, dai = _ssm_bwd(
            f"ssm_bwd_{l}", dyg, sv["ypre"], sv["h"], o_s // cw_ssm, sv["sr"], sv["si"], sv["bdr"], sv["bdi"], sv["cdr"],
            sv["cdi"], sv["dvec"], sv["ar"], sv["ai"])
        dh = jnp.concatenate([dhq, dhk, dhv, dhp, dhs], -1)
        d_win = _mm_tn_bcols(f"in_proj_dw_{l}", sv["xb"], dh, _WIRE)
        dy = _mm_nt(f"in_proj_dx_{l}", dh, fw["win"], add=dr1, add_scale=alpha)

        dbbr = jnp.transpose(_diagblocks(dbdr, n_groups, SSM_GROUP, SSM_STATE), (0, 2, 1))
        dbbi = jnp.transpose(_diagblocks(dbdi, n_groups, SSM_GROUP, SSM_STATE), (0, 2, 1))
        _, vjp = jax.vjp(_ssm_discretise, *ssm_params(l))
        dlr, dli, dldt, dbr, dbi = vjp((dar.reshape(n_groups, SSM_STATE), dai.reshape(n_groups, SSM_STATE), dbbr, dbbi))
        small_g["attn_sinks"][l] = dsk.reshape(-1)
        small_g["pool_w"][l] = dpw
        small_g["pool_scale"][l] = dpsc.reshape(-1)
        small_g["ssm_lam_re"][l], small_g["ssm_lam_im"][l], small_g["ssm_log_dt"][l] = dlr, dli, dldt
        small_g["ssm_b_re"][l], small_g["ssm_b_im"][l] = dbr, dbi
        small_g["ssm_c_re"][l] = jnp.transpose(_diagblocks(dcdr, n_groups, SSM_STATE, SSM_GROUP), (0, 2, 1))
        small_g["ssm_c_im"][l] = jnp.transpose(_diagblocks(dcdi, n_groups, SSM_STATE, SSM_GROUP), (0, 2, 1))
        small_g["ssm_d"][l] = dd.reshape(n_groups, SSM_GROUP)
        small_g["ln1_g"][l], small_g["ln1_b"][l] = dg1.reshape(-1), db1.reshape(-1)
        small_g["ln2_g"][l], small_g["ln2_b"][l] = dg2.reshape(-1), db2.reshape(-1)
        small_g["ffn_conv_b"][l] = _unpad_pairs(dcb.reshape(N_DEV, 2 * half_pad), 1, half, half_pad).reshape(-1)
        cw_g[l] = _unpad_pairs(dcw.reshape(CONV_WIDTH, N_DEV, 2 * half_pad), 2, half, half_pad)

        stacked = [
            jnp.transpose(d_win.reshape(d, N_DEV, in_w // N_DEV), (1, 0, 2)),
            jnp.transpose(d_glu.reshape(ssm_wd, N_DEV, 2 * ssm_wd // N_DEV), (1, 0, 2)),
            d_wout.reshape(N_DEV, d // N_DEV, d),
            d_wup,
            d_wdown.reshape(N_DEV, half_pad, d),
        ]
        by_owner = [a.reshape((4, 2) + a.shape[1:]) for a in stacked]
        theirs = _swap_sibling(f"grads_to_sibling_{l}", by_owner)
        pair = [_pair_sum(f"pair_sum_{l}_{i}", a, b, c_idx) for i, (a, b) in enumerate(zip(by_owner, theirs))]
        parts = list(_swap_chips(f"grads_between_chips_{l}", pair))
        parts[3] = _unpad_pairs(parts[3], 2, half, half_pad)
        parts[4] = parts[4][:, :half, :]
        for name_, p in zip(_BIG, parts):
            big_res[name_][l] = _reduce_adamw(f"adamw_{name_}_{l}", p[:, None], W[name_][l][None], M[name_][l][None], V[name_][l][None])

    for name_ in _BIG:
        outs[name_] = tuple(jnp.concatenate([big_res[name_][l][i] for l in range(depth)], 0) for i in range(4))

    flat_parts = [jnp.stack(small_g[k]).reshape(-1) for k in _SMALL] + [jnp.stack(cw_g).reshape(-1)]
    sizes = [a.shape[0] for a in flat_parts]
    total = sum(sizes)
    rows = -(-total // (512 * V7X_LANES)) * 512
    flat = jnp.pad(jnp.concatenate(flat_parts), (0, rows * V7X_LANES - total)).reshape(rows, V7X_LANES)
    (gathered,) = _all_gather("gather_small_grads", [flat])
    gathered = gathered.reshape(N_DEV, -1)
    n_rep = sum(sizes[:-1])
    cw_all = gathered[:, n_rep:total].reshape(N_DEV, depth, CONV_WIDTH, N_DEV, 2 * half)
    cw_mine = lax.dynamic_index_in_dim(cw_all, me, axis=3, keepdims=False)
    n_small = n_rep + cw_mine[0].size
    rows2 = -(-n_small // (512 * V7X_LANES)) * 512

    def flatten(parts_):
        return jnp.pad(jnp.concatenate(parts_, -1), [(0, 0)] * (parts_[0].ndim - 1) + [(0, rows2 * V7X_LANES - n_small)])

    p_small = flatten([gathered[:, :n_rep], cw_mine.reshape(N_DEV, -1)]).reshape(N_DEV, 1, rows2, V7X_LANES)
    w_small, m_small, v_small = (
        flatten([jnp.concatenate([t[k].reshape(-1) for k in _SMALL]), t["ffn_conv_w"].reshape(-1)]).reshape(1, rows2, V7X_LANES)
        for t in (W, M, V))
    res_small = [a.reshape(-1) for a in _reduce_adamw("adamw_small", p_small, w_small, m_small, v_small)]
    off = 0
    for k in _SMALL + ("ffn_conv_w",):
        n = W[k].size
        outs[k] = tuple(a[off:off + n].reshape(W[k].shape) for a in res_small)
        off += n

    grad_x = dy[None]
    result = [loss, grad_x]
    for i in range(4):
        result += [outs[k][i] for k in _ORDER]
    return tuple(result)
```

```python
import functools
import math

import jax
import jax.numpy as jnp
from jax import lax
from jax.experimental import pallas as pl
from jax.experimental.pallas import tpu as pltpu

F32 = jnp.float32
_MXU = jnp.bfloat16
_WIRE = jnp.bfloat16

HEAD_DIM = 64
GQA = 4
ATTN_BLOCK = 128
ROPE_THETA = 10000.0
POOL_WINDOWS = (2, 4, 8, 16)
SSM_GROUP = 16
SSM_STATE = 64
CONV_WIDTH = 3
LN_EPS = 1e-5
ADAM_LR, ADAM_B1, ADAM_B2, ADAM_EPS, ADAM_WD, ADAM_STEP = 0.001, 0.9, 0.999, 1e-08, 0.01, 10

N_DEV = 8
V7X_LANES = 128
V7X_VMEM_LIMIT = 56 * 1024 * 1024
SCAN_T = 64
SCAN_LANES = 256
MESH = pl.DeviceIdType.MESH
ANY = pl.BlockSpec(memory_space=pl.ANY)


def _cp():
    return pltpu.CompilerParams(vmem_limit_bytes=V7X_VMEM_LIMIT)


def _resident(block, index_map):
    return pl.BlockSpec(block, index_map, pipeline_mode=pl.Buffered(1))


def _sds(shape, dtype):
    return jax.ShapeDtypeStruct(tuple(shape), dtype)


def _mm(name, a, b, dims, grid, a_spec, b_spec, o_spec, out_shape, out_dtype, add=None, add_spec=None, add_scale=1.0, deps=()):
    nk = grid[1]
    n_in = 2 + (add is not None) + len(deps)
    oblk = tuple(d for d in o_spec.block_shape if d is not None)

    def body(*refs):
        a_ref, b_ref = refs[:2]
        add_ref = None if add is None else refs[2]
        o_ref = refs[n_in]
        acc_ref = refs[-1] if nk > 1 else None

        def finish(r):
            if add_ref is not None:
                r = r + add_scale * add_ref[...]
            o_ref[...] = r.astype(o_ref.dtype)

        part = lax.dot_general(a_ref[...], b_ref[...], (dims, ((), ())), preferred_element_type=F32)
        if nk == 1:
            finish(part)
        else:
            k = pl.program_id(1)

            @pl.when(k == 0)
            def _():
                acc_ref[...] = part

            @pl.when(k > 0)
            def _():
                acc_ref[...] += part

            @pl.when(k == nk - 1)
            def _():
                finish(acc_ref[...])

    ins = [a, b] + ([] if add is None else [add]) + list(deps)
    in_specs = [a_spec, b_spec] + ([] if add is None else [add_spec]) + [ANY] * len(deps)
    return pl.pallas_call(
        body, grid=grid, in_specs=in_specs, out_specs=o_spec, out_shape=_sds(out_shape, out_dtype),
        scratch_shapes=[pltpu.VMEM(oblk, F32)] if nk > 1 else [], compiler_params=_cp(), name=name,
    )(*ins)


NN = ((1,), (0,))
NT = ((1,), (1,))
TN = ((0,), (0,))


def _col_tile(n, cap=512):
    if n % V7X_LANES:
        return n
    t = min(cap, n)
    t -= t % V7X_LANES
    while n % t:
        t -= V7X_LANES
    return t


def _mm_nn(name, a, b, out_dtype=F32, cap=512, deps=()):
    m, k = a.shape
    n = b.shape[1]
    tn = _col_tile(n, cap)
    return _mm(name, a, b, NN, (n // tn, 1), _resident((m, k), lambda j, kk: (0, 0)),
               pl.BlockSpec((k, tn), lambda j, kk: (0, j)), pl.BlockSpec((m, tn), lambda j, kk: (0, j)), (m, n), out_dtype,
               deps=deps)


def _mm_nt(name, a, b, out_dtype=F32, add=None, add_scale=1.0, cap=512):
    m, k = a.shape
    n = b.shape[0]
    tn = _col_tile(n, cap)
    o_spec = pl.BlockSpec((m, tn), lambda j, kk: (0, j))
    return _mm(name, a, b, NT, (n // tn, 1), _resident((m, k), lambda j, kk: (0, 0)),
               pl.BlockSpec((tn, k), lambda j, kk: (j, 0)), o_spec, (m, n), out_dtype,
               add=add, add_spec=None if add is None else o_spec, add_scale=add_scale)


def _mm_tn_bcols(name, a, b, out_dtype, cap=512):
    s, k = a.shape
    n = b.shape[1]
    tn = _col_tile(n, cap)
    return _mm(name, a, b, TN, (n // tn, 1), _resident((s, k), lambda j, kk: (0, 0)),
               pl.BlockSpec((s, tn), lambda j, kk: (0, j)), pl.BlockSpec((k, tn), lambda j, kk: (0, j)), (k, n), out_dtype)


def _mm_tn_acols(name, a, b, out_dtype, cap=512):
    s, k = a.shape
    n = b.shape[1]
    tk = _col_tile(k, cap)
    return _mm(name, a, b, TN, (k // tk, 1), pl.BlockSpec((s, tk), lambda i, kk: (0, i)),
               _resident((s, n), lambda i, kk: (0, 0)), pl.BlockSpec((tk, n), lambda i, kk: (i, 0)), (k, n), out_dtype)


def _ln_fwd(name, x, a, g, b, alpha):
    s, d = x.shape
    tr = min(256, s)

    def body(x_ref, a_ref, g_ref, b_ref, y_ref, yb_ref, xh_ref, rs_ref):
        r = alpha * x_ref[...] + a_ref[...]
        mu = jnp.mean(r, -1, keepdims=True)
        c = r - mu
        var = jnp.mean(c * c, -1, keepdims=True)
        rstd = lax.rsqrt(var + LN_EPS)
        xh = c * rstd
        y = xh * g_ref[...] + b_ref[...]
        y_ref[...] = y
        yb_ref[...] = y.astype(_MXU)
        xh_ref[...] = xh
        rs_ref[...] = rstd

    row = pl.BlockSpec((tr, d), lambda i: (i, 0))
    vec = pl.BlockSpec((1, d), lambda i: (0, 0))
    return pl.pallas_call(
        body, grid=(s // tr,), in_specs=[row, row, vec, vec],
        out_specs=[row, row, row, pl.BlockSpec((tr, 1), lambda i: (i, 0))],
        out_shape=[_sds((s, d), F32), _sds((s, d), _MXU), _sds((s, d), F32), _sds((s, 1), F32)],
        compiler_params=_cp(), name=name,
    )(x, a, g, b)


def _ln_bwd(name, dy, xh, rstd, g, deps=()):
    s, d = dy.shape
    tr = min(256, s)
    nd = len(deps)

    def body(dy_ref, xh_ref, rs_ref, g_ref, *rest):
        dr_ref, drb_ref, dg_ref, db_ref = rest[nd:]
        i = pl.program_id(0)
        dy_ = dy_ref[...]
        xh_ = xh_ref[...]
        dxh = dy_ * g_ref[...]
        m1 = jnp.mean(dxh, -1, keepdims=True)
        m2 = jnp.mean(dxh * xh_, -1, keepdims=True)
        dr = rs_ref[...] * (dxh - m1 - xh_ * m2)
        dr_ref[...] = dr
        drb_ref[...] = dr.astype(_MXU)
        pg = jnp.sum(dy_ * xh_, 0, keepdims=True)
        pb = jnp.sum(dy_, 0, keepdims=True)

        @pl.when(i == 0)
        def _():
            dg_ref[...] = pg
            db_ref[...] = pb

        @pl.when(i > 0)
        def _():
            dg_ref[...] += pg
            db_ref[...] += pb

    row = pl.BlockSpec((tr, d), lambda i: (i, 0))
    vec = pl.BlockSpec((1, d), lambda i: (0, 0))
    return pl.pallas_call(
        body, grid=(s // tr,), in_specs=[row, row, pl.BlockSpec((tr, 1), lambda i: (i, 0)), vec] + [ANY] * nd,
        out_specs=[row, row, vec, vec],
        out_shape=[_sds((s, d), F32), _sds((s, d), _MXU), _sds((1, d), F32), _sds((1, d), F32)],
        compiler_params=_cp(), name=name,
    )(dy, xh, rstd, g, *deps)


def _loss_head(name, y, target):
    s, d = y.shape
    tr = min(256, s)

    def body(y_ref, t_ref, dy_ref, l_ref):
        i = pl.program_id(0)
        e = y_ref[...] - t_ref[...]
        dy_ref[...] = e * (1.0 / d)
        part = 0.5 * jnp.sum(jnp.mean(e * e, -1, keepdims=True), 0, keepdims=True)

        @pl.when(i == 0)
        def _():
            l_ref[...] = part

        @pl.when(i > 0)
        def _():
            l_ref[...] += part

    row = pl.BlockSpec((tr, d), lambda i: (i, 0))
    return pl.pallas_call(
        body, grid=(s // tr,), in_specs=[row, row], out_specs=[row, pl.BlockSpec((1, 1), lambda i: (0, 0))],
        out_shape=[_sds((s, d), F32), _sds((1, 1), F32)], compiler_params=_cp(), name=name,
    )(y, target)


def _rope(name, t, width, cos, sin, out_dtype, splits):
    s = t.shape[0]
    tr = min(256, s)
    assert width % V7X_LANES == 0

    def body(t_ref, c_ref, s_ref, *o_refs):
        lane = lax.broadcasted_iota(jnp.int32, (tr, V7X_LANES), 1)
        first = (lane % HEAD_DIM) < (HEAD_DIM // 2)
        cs, sn = c_ref[...], s_ref[...]
        for (lo, hi), o_ref in zip(splits, o_refs):
            for c0 in range(lo, hi, V7X_LANES):
                v = t_ref[:, c0:c0 + V7X_LANES].astype(F32)
                partner = jnp.where(first, pltpu.roll(v, V7X_LANES - HEAD_DIM // 2, 1), pltpu.roll(v, HEAD_DIM // 2, 1))
                o_ref[:, c0 - lo:c0 - lo + V7X_LANES] = (v * cs + partner * sn).astype(o_ref.dtype)

    tab = pl.BlockSpec((tr, V7X_LANES), lambda i: (i, 0))
    return pl.pallas_call(
        body, grid=(s // tr,), in_specs=[pl.BlockSpec((tr, width), lambda i: (i, 0)), tab, tab],
        out_specs=[pl.BlockSpec((tr, hi - lo), lambda i: (i, 0)) for lo, hi in splits],
        out_shape=[_sds((s, hi - lo), out_dtype) for lo, hi in splits], compiler_params=_cp(), name=name,
    )(t, cos, sin)


def _attn_masks():
    i = lax.broadcasted_iota(jnp.int32, (ATTN_BLOCK, ATTN_BLOCK), 0)
    j = lax.broadcasted_iota(jnp.int32, (ATTN_BLOCK, ATTN_BLOCK), 1)
    return j <= i, j > i


def _attn_scores(qg, kc, kp, n, cur_ok, prev_ok):
    scale = HEAD_DIM ** -0.5
    s_c = lax.dot_general(qg, kc, (NT, ((), ())), preferred_element_type=F32) * scale
    s_p = lax.dot_general(qg, kp, (NT, ((), ())), preferred_element_type=F32) * scale
    s_c = jnp.where(cur_ok, s_c, -1e30)
    s_p = jnp.where(jnp.logical_and(prev_ok, n > 0), s_p, -1e30)
    return s_c, s_p


def _attn_fwd(name, q, k, v, sinks):
    s = q.shape[0]
    nkv = k.shape[0]
    gw = GQA * HEAD_DIM
    nb = s // ATTN_BLOCK

    def body(sk_ref, q_ref, k_ref, v_ref, o_ref, lse_ref):
        h = pl.program_id(0)
        cur_ok, prev_ok = _attn_masks()

        def blk(n, carry):
            off = pl.multiple_of(n * ATTN_BLOCK, ATTN_BLOCK)
            poff = pl.multiple_of(jnp.maximum(n - 1, 0) * ATTN_BLOCK, ATTN_BLOCK)
            kc, kp = k_ref[pl.ds(off, ATTN_BLOCK), :], k_ref[pl.ds(poff, ATTN_BLOCK), :]
            vc, vp = v_ref[pl.ds(off, ATTN_BLOCK), :], v_ref[pl.ds(poff, ATTN_BLOCK), :]
            for g in range(GQA):
                qg = q_ref[pl.ds(off, ATTN_BLOCK), g * HEAD_DIM:(g + 1) * HEAD_DIM]
                s_c, s_p = _attn_scores(qg, kc, kp, n, cur_ok, prev_ok)
                sink = sk_ref[h, g]
                m = jnp.maximum(jnp.maximum(s_c.max(-1, keepdims=True), s_p.max(-1, keepdims=True)), sink)
                p_c, p_p = jnp.exp(s_c - m), jnp.exp(s_p - m)
                den = p_c.sum(-1, keepdims=True) + p_p.sum(-1, keepdims=True) + jnp.exp(sink - m)
                o = (jnp.dot((p_c / den).astype(_MXU), vc, preferred_element_type=F32)
                     + jnp.dot((p_p / den).astype(_MXU), vp, preferred_element_type=F32))
                o_ref[pl.ds(off, ATTN_BLOCK), g * HEAD_DIM:(g + 1) * HEAD_DIM] = o
                lse_ref[pl.ds(off, ATTN_BLOCK), g:g + 1] = m + jnp.log(den)
            return carry

        lax.fori_loop(0, nb, blk, 0)

    kv_spec = pl.BlockSpec((None, s, HEAD_DIM), lambda h: (h, 0, 0))
    return pl.pallas_call(
        body, grid=(nkv,),
        in_specs=[pl.BlockSpec(memory_space=pltpu.SMEM), pl.BlockSpec((s, gw), lambda h: (0, h)), kv_spec, kv_spec],
        out_specs=[pl.BlockSpec((s, gw), lambda h: (0, h)), pl.BlockSpec((None, s, GQA), lambda h: (h, 0, 0))],
        out_shape=[_sds((s, nkv * gw), F32), _sds((nkv, s, GQA), F32)], compiler_params=_cp(), name=name,
    )(sinks, q, k, v)


def _attn_bwd(name, q, k, v, o, dmix, lse, sinks):
    s = q.shape[0]
    nkv = k.shape[0]
    gw = GQA * HEAD_DIM
    nb = s // ATTN_BLOCK
    scale = HEAD_DIM ** -0.5

    def body(sk_ref, q_ref, k_ref, v_ref, o_ref, do_ref, lse_ref, dq_ref, dk_ref, dv_ref, dsk_ref):
        h = pl.program_id(0)
        cur_ok, prev_ok = _attn_masks()
        dk_ref[...] = jnp.zeros_like(dk_ref)
        dv_ref[...] = jnp.zeros_like(dv_ref)

        def blk(n, acc):
            off = pl.multiple_of(n * ATTN_BLOCK, ATTN_BLOCK)
            poff = pl.multiple_of(jnp.maximum(n - 1, 0) * ATTN_BLOCK, ATTN_BLOCK)
            rows, prows = pl.ds(off, ATTN_BLOCK), pl.ds(poff, ATTN_BLOCK)
            kc, kp, vc, vp = k_ref[rows, :], k_ref[prows, :], v_ref[rows, :], v_ref[prows, :]
            dk_c = jnp.zeros((ATTN_BLOCK, HEAD_DIM), F32)
            dk_p = jnp.zeros((ATTN_BLOCK, HEAD_DIM), F32)
            dv_c = jnp.zeros((ATTN_BLOCK, HEAD_DIM), F32)
            dv_p = jnp.zeros((ATTN_BLOCK, HEAD_DIM), F32)
            out_acc = []
            for g in range(GQA):
                cols = slice(g * HEAD_DIM, (g + 1) * HEAD_DIM)
                qg = q_ref[rows, cols]
                do_g = do_ref[rows, cols]
                delta = jnp.sum(do_g * o_ref[rows, cols], -1, keepdims=True)
                dob = do_g.astype(_MXU)
                lse_g = lse_ref[rows, g:g + 1]
                s_c, s_p = _attn_scores(qg, kc, kp, n, cur_ok, prev_ok)
                p_c, p_p = jnp.exp(s_c - lse_g), jnp.exp(s_p - lse_g)
                dp_c = lax.dot_general(dob, vc, (NT, ((), ())), preferred_element_type=F32)
                dp_p = lax.dot_general(dob, vp, (NT, ((), ())), preferred_element_type=F32)
                ds_c = (p_c * (dp_c - delta) * scale).astype(_MXU)
                ds_p = (p_p * (dp_p - delta) * scale).astype(_MXU)
                dq_ref[rows, cols] = (jnp.dot(ds_c, kc, preferred_element_type=F32)
                                      + jnp.dot(ds_p, kp, preferred_element_type=F32))
                dk_c += lax.dot_general(ds_c, qg, (TN, ((), ())), preferred_element_type=F32)
                dk_p += lax.dot_general(ds_p, qg, (TN, ((), ())), preferred_element_type=F32)
                dv_c += lax.dot_general(p_c.astype(_MXU), dob, (TN, ((), ())), preferred_element_type=F32)
                dv_p += lax.dot_general(p_p.astype(_MXU), dob, (TN, ((), ())), preferred_element_type=F32)
                out_acc.append(acc[g] - jnp.exp(sk_ref[h, g] - lse_g) * delta)
            dk_ref[rows, :] += dk_c
            dv_ref[rows, :] += dv_c
            dk_ref[prows, :] += dk_p
            dv_ref[prows, :] += dv_p
            return tuple(out_acc)

        acc = lax.fori_loop(0, nb, blk, tuple(jnp.zeros((ATTN_BLOCK, 1), F32) for _ in range(GQA)))
        for g in range(GQA):
            dsk_ref[:, g:g + 1] = jnp.sum(acc[g], 0, keepdims=True)

    kv_spec = pl.BlockSpec((None, s, HEAD_DIM), lambda h: (h, 0, 0))
    qcols = pl.BlockSpec((s, gw), lambda h: (0, h))
    return pl.pallas_call(
        body, grid=(nkv,),
        in_specs=[pl.BlockSpec(memory_space=pltpu.SMEM), qcols, kv_spec, kv_spec, qcols, qcols,
                  pl.BlockSpec((None, s, GQA), lambda h: (h, 0, 0))],
        out_specs=[qcols, kv_spec, kv_spec, pl.BlockSpec((None, 1, GQA), lambda h: (h, 0, 0))],
        out_shape=[_sds((s, nkv * gw), F32), _sds((nkv, s, HEAD_DIM), F32), _sds((nkv, s, HEAD_DIM), F32),
                   _sds((nkv, 1, GQA), F32)],
        compiler_params=_cp(), name=name,
    )(sinks, q, k, v, o, dmix, lse)


def _shift_down(a, k, t):
    return jnp.where(t >= k, pltpu.roll(a, k, 0), 0.0)


def _shift_up(a, k, t):
    n = a.shape[0]
    return jnp.where(t < n - k, pltpu.roll(a, n - k, 0), 0.0)


def _pool_fwd(name, h, col_block, pool_w, pool_scale):
    s = h.shape[0]
    ng, pg = pool_w.shape[0], pool_w.shape[1]
    pw_ = ng * pg

    def body(u_ref, w_ref, sc_ref, y_ref, pre_ref):
        t = lax.broadcasted_iota(jnp.int32, (s, pg), 0)
        for gi, win in enumerate(POOL_WINDOWS):
            cols = slice(gi * pg, (gi + 1) * pg)
            u = u_ref[:, cols]
            a = u
            k = 1
            while k < win:
                a = a + _shift_down(a, k, t)
                k *= 2
            div = jnp.minimum(t + 1, win).astype(F32)
            pre = (a / div - u).astype(_MXU)
            pre_ref[:, cols] = pre
            out = jnp.dot(pre, w_ref[gi], preferred_element_type=F32)
            y_ref[:, cols] = (out * sc_ref[:, cols]).astype(y_ref.dtype)

    blk = pl.BlockSpec((s, pw_), lambda i: (0, 0))
    return pl.pallas_call(
        body, grid=(1,),
        in_specs=[pl.BlockSpec((s, pw_), lambda i: (0, col_block)), pl.BlockSpec((ng, pg, pg), lambda i: (0, 0, 0)),
                  pl.BlockSpec((1, pw_), lambda i: (0, 0))],
        out_specs=[blk, blk], out_shape=[_sds((s, pw_), _MXU), _sds((s, pw_), _MXU)], compiler_params=_cp(), name=name,
    )(h, pool_w, pool_scale)


def _pool_bwd(name, dmix, col_block, pre, pool_w, pool_scale):
    s = pre.shape[0]
    ng, pg = pool_w.shape[0], pool_w.shape[1]
    pw_ = ng * pg

    def body(dy_ref, pre_ref, w_ref, sc_ref, du_ref, dw_ref, dsc_ref):
        t = lax.broadcasted_iota(jnp.int32, (s, pg), 0)
        for gi, win in enumerate(POOL_WINDOWS):
            cols = slice(gi * pg, (gi + 1) * pg)
            pre_g = pre_ref[:, cols]
            dy = dy_ref[:, cols]
            out = jnp.dot(pre_g, w_ref[gi], preferred_element_type=F32)
            dsc_ref[:, cols] = jnp.sum(dy * out, 0, keepdims=True)
            dout = (dy * sc_ref[:, cols]).astype(_MXU)
            dw_ref[gi] = lax.dot_general(pre_g, dout, (TN, ((), ())), preferred_element_type=F32)
            dpre = lax.dot_general(dout, w_ref[gi], (NT, ((), ())), preferred_element_type=F32)
            div = jnp.minimum(t + 1, win).astype(F32)
            a = dpre / div
            k = 1
            while k < win:
                a = a + _shift_up(a, k, t)
                k *= 2
            du_ref[:, cols] = (a - dpre).astype(du_ref.dtype)

    blk = pl.BlockSpec((s, pw_), lambda i: (0, 0))
    wspec = pl.BlockSpec((ng, pg, pg), lambda i: (0, 0, 0))
    vec = pl.BlockSpec((1, pw_), lambda i: (0, 0))
    return pl.pallas_call(
        body, grid=(1,), in_specs=[pl.BlockSpec((s, pw_), lambda i: (0, col_block)), blk, wspec, vec],
        out_specs=[blk, wspec, vec], out_shape=[_sds((s, pw_), _MXU), _sds((ng, pg, pg), F32), _sds((1, pw_), F32)],
        compiler_params=_cp(), name=name,
    )(dmix, pre, pool_w, pool_scale)


def _scan_chunks(xr_ref, xi_ref, sr_ref, si_ref, ar, ai, reverse):
    n, c = xr_ref.shape
    tt = min(SCAN_T, n)
    lw = min(SCAN_LANES, c)
    nchunk = n // tt
    t = lax.broadcasted_iota(jnp.int32, (tt, lw), 0)

    for l0 in range(0, c, lw):
        lanes = slice(l0, l0 + lw)
        a_r, a_i = ar[:, lanes], ai[:, lanes]

        def local(vr, vi, a_r=a_r, a_i=a_i):
            pr, pi = a_r, a_i
            k = 1
            while k < tt:
                if reverse:
                    hr, hi = _shift_up(vr, k, t), _shift_up(vi, k, t)
                else:
                    hr, hi = _shift_down(vr, k, t), _shift_down(vi, k, t)
                vr, vi = vr + pr * hr - pi * hi, vi + pr * hi + pi * hr
                pr, pi = pr * pr - pi * pi, 2.0 * pr * pi
                k *= 2
            return vr, vi

        edge = tt - 1 if reverse else 0
        pw_r, pw_i = local(jnp.where(t == edge, a_r, 0.0), jnp.where(t == edge, a_i, 0.0))
        last = 0 if reverse else tt - 1

        def body(i, carry, lanes=lanes, local=local, pw_r=pw_r, pw_i=pw_i):
            cr, ci = carry
            ch = nchunk - 1 - i if reverse else i
            rows = pl.ds(pl.multiple_of(ch * tt, tt), tt)
            vr, vi = local(xr_ref[rows, lanes], xi_ref[rows, lanes])
            vr2 = vr + pw_r * cr - pw_i * ci
            vi2 = vi + pw_r * ci + pw_i * cr
            sr_ref[rows, lanes] = vr2
            si_ref[rows, lanes] = vi2
            return vr2[last:last + 1, :], vi2[last:last + 1, :]

        lax.fori_loop(0, nchunk, body, (jnp.zeros((1, lw), F32), jnp.zeros((1, lw), F32)))


_GELU_K = math.sqrt(2.0 / math.pi)


def _gelu_grad(y):
    inner = _GELU_K * (y + 0.044715 * y * y * y)
    th = jnp.tanh(inner)
    return 0.5 * (1.0 + th) + 0.5 * y * (1.0 - th * th) * _GELU_K * (1.0 + 3.0 * 0.044715 * y * y)


def _ssm_fwd(name, h, u_block0, bdr, bdi, cdr, cdi, dvec, ar, ai):
    s = h.shape[0]
    nt, cw, lw = bdr.shape
    rc = min(256, s)

    def body(u_ref, bdr_ref, bdi_ref, cdr_ref, cdi_ref, d_ref, ar_ref, ai_ref, sr_ref, si_ref, y_ref, yg_ref):
        def mm_in(c, _):
            rows = pl.ds(pl.multiple_of(c * rc, rc), rc)
            ub = u_ref[rows, :].astype(_MXU)
            sr_ref[rows, :] = jnp.dot(ub, bdr_ref[...], preferred_element_type=F32)
            si_ref[rows, :] = jnp.dot(ub, bdi_ref[...], preferred_element_type=F32)
            return 0

        lax.fori_loop(0, s // rc, mm_in, 0)
        _scan_chunks(sr_ref, si_ref, sr_ref, si_ref, ar_ref[...], ai_ref[...], reverse=False)

        def mm_out(c, _):
            rows = pl.ds(pl.multiple_of(c * rc, rc), rc)
            y = (jnp.dot(sr_ref[rows, :].astype(_MXU), cdr_ref[...], preferred_element_type=F32)
                 - jnp.dot(si_ref[rows, :].astype(_MXU), cdi_ref[...], preferred_element_type=F32)
                 + d_ref[...] * u_ref[rows, :])
            y_ref[rows, :] = y
            yg_ref[rows, :] = jax.nn.gelu(y).astype(yg_ref.dtype)
            return 0

        lax.fori_loop(0, s // rc, mm_out, 0)

    st = pl.BlockSpec((s, lw), lambda j: (0, j))
    ch = pl.BlockSpec((s, cw), lambda j: (0, j))
    bspec = pl.BlockSpec((None, cw, lw), lambda j: (j, 0, 0))
    cspec = pl.BlockSpec((None, lw, cw), lambda j: (j, 0, 0))
    return pl.pallas_call(
        body, grid=(nt,),
        in_specs=[pl.BlockSpec((s, cw), lambda j: (0, u_block0 + j)), bspec, bspec, cspec, cspec,
                  pl.BlockSpec((1, cw), lambda j: (0, j)), pl.BlockSpec((1, lw), lambda j: (0, j)),
                  pl.BlockSpec((1, lw), lambda j: (0, j))],
        out_specs=[st, st, ch, ch],
        out_shape=[_sds((s, nt * lw), F32), _sds((s, nt * lw), F32), _sds((s, nt * cw), F32), _sds((s, nt * cw), _MXU)],
        compiler_params=_cp(), name=name,
    )(h, bdr, bdi, cdr, cdi, dvec, ar, ai)


def _ssm_bwd(name, dyg, ypre, h, u_block0, sr, si, bdr, bdi, cdr, cdi, dvec, ar, ai):
    s = h.shape[0]
    nt, cw, lw = bdr.shape
    rc = min(256, s)

    def body(dyg_ref, yp_ref, u_ref, sr_ref, si_ref, bdr_ref, bdi_ref, cdr_ref, cdi_ref, d_ref, ar_ref, ai_ref,
             du_ref, dd_ref, dcr_ref, dci_ref, dbr_ref, dbi_ref, dar_ref, dai_ref, lr_scr, li_scr, dy_scr):
        for ref in (dd_ref, dcr_ref, dci_ref, dbr_ref, dbi_ref, dar_ref, dai_ref):
            ref[...] = jnp.zeros_like(ref)

        def p1(c, _):
            rows = pl.ds(pl.multiple_of(c * rc, rc), rc)
            dy = dyg_ref[rows, :] * _gelu_grad(yp_ref[rows, :])
            dy_scr[rows, :] = dy
            dd_ref[...] += jnp.sum(dy * u_ref[rows, :], 0, keepdims=True)
            dyb = dy.astype(_MXU)
            lr_scr[rows, :] = lax.dot_general(dyb, cdr_ref[...], (NT, ((), ())), preferred_element_type=F32)
            li_scr[rows, :] = -lax.dot_general(dyb, cdi_ref[...], (NT, ((), ())), preferred_element_type=F32)
            dcr_ref[...] += lax.dot_general(sr_ref[rows, :].astype(_MXU), dyb, (TN, ((), ())), preferred_element_type=F32)
            dci_ref[...] -= lax.dot_general(si_ref[rows, :].astype(_MXU), dyb, (TN, ((), ())), preferred_element_type=F32)
            return 0

        lax.fori_loop(0, s // rc, p1, 0)
        _scan_chunks(lr_scr, li_scr, lr_scr, li_scr, ar_ref[...], -ai_ref[...], reverse=True)
        t = lax.broadcasted_iota(jnp.int32, (rc, lw), 0)

        def p2(c, _):
            r0 = pl.multiple_of(c * rc, rc)
            rows = pl.ds(r0, rc)
            before = pl.ds(pl.multiple_of(jnp.maximum(r0 - 8, 0), 8), 8)
            have = (c > 0).astype(F32)
            lr, li = lr_scr[rows, :], li_scr[rows, :]
            spr = jnp.where(t == 0, sr_ref[before, :][7:8, :] * have, pltpu.roll(sr_ref[rows, :], 1, 0))
            spi = jnp.where(t == 0, si_ref[before, :][7:8, :] * have, pltpu.roll(si_ref[rows, :], 1, 0))
            dar_ref[...] += jnp.sum(lr * spr + li * spi, 0, keepdims=True)
            dai_ref[...] += jnp.sum(li * spr - lr * spi, 0, keepdims=True)
            lrb, lib = lr.astype(_MXU), li.astype(_MXU)
            du = (dy_scr[rows, :] * d_ref[...]
                  + lax.dot_general(lrb, bdr_ref[...], (NT, ((), ())), preferred_element_type=F32)
                  + lax.dot_general(lib, bdi_ref[...], (NT, ((), ())), preferred_element_type=F32))
            du_ref[rows, :] = du.astype(du_ref.dtype)
            ub = u_ref[rows, :].astype(_MXU)
            dbr_ref[...] += lax.dot_general(ub, lrb, (TN, ((), ())), preferred_element_type=F32)
            dbi_ref[...] += lax.dot_general(ub, lib, (TN, ((), ())), preferred_element_type=F32)
            return 0

        lax.fori_loop(0, s // rc, p2, 0)

    st = pl.BlockSpec((s, lw), lambda j: (0, j))
    ch = pl.BlockSpec((s, cw), lambda j: (0, j))
    bspec = pl.BlockSpec((None, cw, lw), lambda j: (j, 0, 0))
    cspec = pl.BlockSpec((None, lw, cw), lambda j: (j, 0, 0))
    cvec = pl.BlockSpec((1, cw), lambda j: (0, j))
    svec = pl.BlockSpec((1, lw), lambda j: (0, j))
    return pl.pallas_call(
        body, grid=(nt,),
        in_specs=[ch, ch, pl.BlockSpec((s, cw), lambda j: (0, u_block0 + j)), st, st, bspec, bspec, cspec, cspec, cvec, svec, svec],
        out_specs=[ch, cvec, cspec, cspec, bspec, bspec, svec, svec],
        out_shape=[_sds((s, nt * cw), _MXU), _sds((1, nt * cw), F32), _sds((nt, lw, cw), F32), _sds((nt, lw, cw), F32),
                   _sds((nt, cw, lw), F32), _sds((nt, cw, lw), F32), _sds((1, nt * lw), F32), _sds((1, nt * lw), F32)],
        scratch_shapes=[pltpu.VMEM((s, lw), F32), pltpu.VMEM((s, lw), F32), pltpu.VMEM((s, cw), F32)],
        compiler_params=_cp(), name=name,
    )(dyg, ypre, h, sr, si, bdr, bdi, cdr, cdi, dvec, ar, ai)


def _glu_fwd(name, yg, gw):
    s, w = yg.shape
    tr = min(512, s)

    def body(y_ref, w_ref, o_ref, ab_ref):
        ab = jnp.dot(y_ref[...], w_ref[...], preferred_element_type=F32)
        ab_ref[...] = ab
        o_ref[...] = (ab[:, :w] * jax.nn.sigmoid(ab[:, w:])).astype(o_ref.dtype)

    return pl.pallas_call(
        body, grid=(s // tr,), in_specs=[pl.BlockSpec((tr, w), lambda i: (i, 0)), _resident((w, 2 * w), lambda i: (0, 0))],
        out_specs=[pl.BlockSpec((tr, w), lambda i: (i, 0)), pl.BlockSpec((tr, 2 * w), lambda i: (i, 0))],
        out_shape=[_sds((s, w), _MXU), _sds((s, 2 * w), F32)], compiler_params=_cp(), name=name,
    )(yg, gw)


def _glu_bwd(name, dmix, col_block, ab, gw):
    s = ab.shape[0]
    w = ab.shape[1] // 2
    tr = min(512, s)

    def body(do_ref, ab_ref, w_ref, dab_ref, dy_ref):
        do = do_ref[...]
        a, b = ab_ref[:, :w], ab_ref[:, w:]
        sg = jax.nn.sigmoid(b)
        da = (do * sg).astype(_MXU)
        db = (do * a * sg * (1.0 - sg)).astype(_MXU)
        dab_ref[:, :w] = da
        dab_ref[:, w:] = db
        dy_ref[...] = (lax.dot_general(da, w_ref[:, :w], (NT, ((), ())), preferred_element_type=F32)
                       + lax.dot_general(db, w_ref[:, w:], (NT, ((), ())), preferred_element_type=F32))

    return pl.pallas_call(
        body, grid=(s // tr,),
        in_specs=[pl.BlockSpec((tr, w), lambda i: (i, col_block)), pl.BlockSpec((tr, 2 * w), lambda i: (i, 0)),
                  _resident((w, 2 * w), lambda i: (0, 0))],
        out_specs=[pl.BlockSpec((tr, 2 * w), lambda i: (i, 0)), pl.BlockSpec((tr, w), lambda i: (i, 0))],
        out_shape=[_sds((s, 2 * w), _MXU), _sds((s, w), F32)], compiler_params=_cp(), name=name,
    )(dmix, ab, gw)


def _conv(hu, w_ref, b_ref, t):
    hc = b_ref[...] + _shift_down(hu, 2, t) * w_ref[0:1, :]
    hc = hc + _shift_down(hu, 1, t) * w_ref[1:2, :]
    return hc + hu * w_ref[2:3, :]


def _conv_act_fwd(name, hu, cw, cb):
    s, f2 = hu.shape
    f = f2 // 2
    tw = _col_tile(f, 256)
    nt = f // tw

    def body(v_ref, g_ref, wv_ref, wg_ref, bv_ref, bg_ref, act_ref):
        t = lax.broadcasted_iota(jnp.int32, (s, tw), 0)
        val = _conv(v_ref[...], wv_ref, bv_ref, t)
        gate = _conv(g_ref[...], wg_ref, bg_ref, t)
        act_ref[...] = (jax.nn.silu(gate) * val).astype(act_ref.dtype)

    cv = lambda rows: pl.BlockSpec((rows, tw), lambda i: (0, i))
    cg = lambda rows: pl.BlockSpec((rows, tw), lambda i: (0, nt + i))
    return pl.pallas_call(
        body, grid=(nt,), in_specs=[cv(s), cg(s), cv(CONV_WIDTH), cg(CONV_WIDTH), cv(1), cg(1)],
        out_specs=cv(s), out_shape=_sds((s, f), _MXU), compiler_params=_cp(), name=name,
    )(hu, hu, cw, cw, cb, cb)


def _conv_act_bwd(name, dact, hu, cw, cb):
    s, f2 = hu.shape
    f = f2 // 2
    tw = _col_tile(f, 256)
    nt = f // tw

    def body(da_ref, v_ref, g_ref, wv_ref, wg_ref, bv_ref, bg_ref, dh_ref, dwv_ref, dwg_ref, dbv_ref, dbg_ref):
        t = lax.broadcasted_iota(jnp.int32, (s, tw), 0)
        hv, hg = v_ref[...], g_ref[...]
        val = _conv(hv, wv_ref, bv_ref, t)
        gate = _conv(hg, wg_ref, bg_ref, t)
        sg = jax.nn.sigmoid(gate)
        da = da_ref[...]
        dval = da * (gate * sg)
        dgate = da * val * sg * (1.0 + gate * (1.0 - sg))
        for part, dhc, hu_, w_ref, dw_ref, db_ref in ((0, dval, hv, wv_ref, dwv_ref, dbv_ref),
                                                      (1, dgate, hg, wg_ref, dwg_ref, dbg_ref)):
            db_ref[...] = jnp.sum(dhc, 0, keepdims=True)
            dw_ref[0:1, :] = jnp.sum(dhc * _shift_down(hu_, 2, t), 0, keepdims=True)
            dw_ref[1:2, :] = jnp.sum(dhc * _shift_down(hu_, 1, t), 0, keepdims=True)
            dw_ref[2:3, :] = jnp.sum(dhc * hu_, 0, keepdims=True)
            dh = dhc * w_ref[2:3, :] + _shift_up(dhc, 1, t) * w_ref[1:2, :] + _shift_up(dhc, 2, t) * w_ref[0:1, :]
            dh_ref[part] = dh.astype(dh_ref.dtype)

    cv = lambda rows: pl.BlockSpec((rows, tw), lambda i: (0, i))
    cg = lambda rows: pl.BlockSpec((rows, tw), lambda i: (0, nt + i))
    both = pl.BlockSpec((2, s, tw), lambda i: (0, 0, i))
    dh, dwv, dwg, dbv, dbg = pl.pallas_call(
        body, grid=(nt,), in_specs=[cv(s), cv(s), cg(s), cv(CONV_WIDTH), cg(CONV_WIDTH), cv(1), cg(1)],
        out_specs=[both, cv(CONV_WIDTH), cv(CONV_WIDTH), cv(1), cv(1)],
        out_shape=[_sds((2, s, f), _MXU), _sds((CONV_WIDTH, f), F32), _sds((CONV_WIDTH, f), F32),
                   _sds((1, f), F32), _sds((1, f), F32)],
        compiler_params=_cp(), name=name,
    )(dact, hu, hu, cw, cw, cb, cb)
    return dh, jnp.concatenate([dwv, dwg], 1), jnp.concatenate([dbv, dbg], 1)


def _elem_tiles(r, c):
    tc = _col_tile(c, 1024)
    budget = 256 * 1024
    tr = r
    if r * tc > budget:
        cands = [d for d in range(8, r, 8) if r % d == 0 and d * tc <= budget]
        tr = max(cands) if cands else r
    return tr, tc


def _reduce_adamw(name, parts, w, m, v):
    n, nl, r, c = parts.shape
    tr, tc = _elem_tiles(r, c)
    c1 = 1.0 - ADAM_B1 ** ADAM_STEP
    c2 = 1.0 - ADAM_B2 ** ADAM_STEP

    def body(p_ref, w_ref, m_ref, v_ref, g_ref, d_ref, nm_ref, nv_ref):
        g = p_ref[0].astype(F32)
        for i in range(1, n):
            g = g + p_ref[i].astype(F32)
        nm = ADAM_B1 * m_ref[...] + (1.0 - ADAM_B1) * g
        nv = ADAM_B2 * v_ref[...] + (1.0 - ADAM_B2) * (g * g)
        m_hat = nm / c1
        v_hat = nv / c2
        g_ref[...] = g
        nm_ref[...] = nm
        nv_ref[...] = nv
        d_ref[...] = -ADAM_LR * (m_hat / (jnp.sqrt(v_hat) + ADAM_EPS) + ADAM_WD * w_ref[...])

    blk = pl.BlockSpec((None, tr, tc), lambda l, i, j: (l, i, j))
    out = _sds((nl, r, c), F32)
    return pl.pallas_call(
        body, grid=(nl, r // tr, c // tc),
        in_specs=[pl.BlockSpec((n, None, tr, tc), lambda l, i, j: (0, l, i, j)), blk, blk, blk],
        out_specs=[blk, blk, blk, blk], out_shape=[out, out, out, out], compiler_params=_cp(), name=name,
    )(parts, w, m, v)


def _pair_sum(name, mine, theirs, c_idx):
    _, _, r, c = mine.shape
    tr, tc = _elem_tiles(r, c)

    def body(c_ref, a_ref, b_ref, o_ref):
        o_ref[...] = (a_ref[...].astype(F32) + b_ref[...].astype(F32)).astype(o_ref.dtype)

    return pl.pallas_call(
        body,
        grid_spec=pltpu.PrefetchScalarGridSpec(
            num_scalar_prefetch=1, grid=(4, r // tr, c // tc),
            in_specs=[pl.BlockSpec((None, None, tr, tc), lambda p, i, j, cref: (p, cref[0], i, j)),
                      pl.BlockSpec((None, tr, tc), lambda p, i, j, cref: (p, i, j))],
            out_specs=pl.BlockSpec((None, tr, tc), lambda p, i, j, cref: (p, i, j))),
        out_shape=_sds((4, r, c), _WIRE), compiler_params=_cp(), name=name,
    )(c_idx, mine, theirs)


def _place():
    return lax.axis_index("x"), lax.axis_index("y"), lax.axis_index("c")


def _all_gather(name, xs):
    n = len(xs)

    def body(*refs):
        x_refs, o_refs = refs[:n], refs[n:2 * n]
        send_sems, recv_sems, local_sems = refs[2 * n:]
        x, y, c = _place()
        me, sibling = (x, y, c), (x, y, 1 - c)
        chips = [(1 - x, y), (x, 1 - y), (1 - x, 1 - y)]

        def copy(a, k, block, to, src=None):
            px, py, pc = block
            rows = o_refs[a].at[4 * px + 2 * py + pc]
            return pltpu.make_async_remote_copy(
                src_ref=rows if src is None else src, dst_ref=rows, send_sem=send_sems.at[a, k], recv_sem=recv_sems.at[a, k],
                device_id=to, device_id_type=MESH)

        sent = []
        mine = []
        for a in range(n):
            mx, my, mc = me
            cp = pltpu.make_async_copy(x_refs[a], o_refs[a].at[4 * mx + 2 * my + mc], local_sems.at[a])
            cp.start()
            mine.append(cp)
            first = [copy(a, 0, me, sibling, src=x_refs[a])]
            first += [copy(a, 1 + j, me, (*chip, c), src=x_refs[a]) for j, chip in enumerate(chips)]
            for cp in first:
                cp.start()
            sent += first
        for a in range(n):
            for j, chip in enumerate(chips):
                copy(a, 1 + j, (*chip, c), me).wait_recv()
                fwd = copy(a, 4 + j, (*chip, c), sibling)
                fwd.start()
                sent.append(fwd)
        for a in range(n):
            copy(a, 0, sibling, me).wait_recv()
            for j, chip in enumerate(chips):
                copy(a, 4 + j, (*chip, 1 - c), me).wait_recv()
        for cp in sent:
            cp.wait_send()
        for cp in mine:
            cp.wait()

    return pl.pallas_call(
        body, in_specs=[ANY] * n, out_specs=[ANY] * n,
        out_shape=[_sds((N_DEV,) + a.shape, a.dtype) for a in xs],
        scratch_shapes=[pltpu.SemaphoreType.DMA((n, 7)), pltpu.SemaphoreType.DMA((n, 7)), pltpu.SemaphoreType.DMA((n,))],
        name=name,
    )(*xs)


def _swap_sibling(name, xs):
    n = len(xs)

    def body(*refs):
        x_refs, o_refs = refs[:n], refs[n:2 * n]
        send_sems, recv_sems = refs[2 * n:]
        x, y, c = _place()
        sibling = (x, y, 1 - c)
        cps = []
        for a in range(n):
            for p in range(4):
                cp = pltpu.make_async_remote_copy(
                    src_ref=x_refs[a].at[p, 1 - c], dst_ref=o_refs[a].at[p], send_sem=send_sems.at[a, p],
                    recv_sem=recv_sems.at[a, p], device_id=sibling, device_id_type=MESH)
                cp.start()
                cps.append(cp)
        for cp in cps:
            cp.wait_recv()
        for cp in cps:
            cp.wait_send()

    return pl.pallas_call(
        body, in_specs=[ANY] * n, out_specs=[ANY] * n,
        out_shape=[_sds((4,) + a.shape[2:], a.dtype) for a in xs],
        scratch_shapes=[pltpu.SemaphoreType.DMA((n, 4)), pltpu.SemaphoreType.DMA((n, 4))],
        name=name,
    )(*xs)


def _swap_chips(name, xs):
    n = len(xs)

    def body(*refs):
        x_refs, o_refs = refs[:n], refs[n:2 * n]
        send_sems, recv_sems, local_sems = refs[2 * n:]
        x, y, c = _place()
        my_chip = 2 * x + y
        chips = [(1 - x, y), (x, 1 - y), (1 - x, 1 - y)]
        cps, own = [], []
        for a in range(n):
            cp = pltpu.make_async_copy(x_refs[a].at[my_chip], o_refs[a].at[my_chip], local_sems.at[a])
            cp.start()
            own.append(cp)
            for j, (px, py) in enumerate(chips):
                cp = pltpu.make_async_remote_copy(
                    src_ref=x_refs[a].at[2 * px + py], dst_ref=o_refs[a].at[my_chip], send_sem=send_sems.at[a, j],
                    recv_sem=recv_sems.at[a, j], device_id=(px, py, c), device_id_type=MESH)
                cp.start()
                cps.append((a, j, px, py, cp))
        for a, j, px, py, cp in cps:
            pltpu.make_async_remote_copy(
                src_ref=x_refs[a].at[my_chip], dst_ref=o_refs[a].at[2 * px + py], send_sem=send_sems.at[a, j],
                recv_sem=recv_sems.at[a, j], device_id=(px, py, c), device_id_type=MESH).wait_recv()
        for a, j, px, py, cp in cps:
            cp.wait_send()
        for cp in own:
            cp.wait()

    return pl.pallas_call(
        body, in_specs=[ANY] * n, out_specs=[ANY] * n, out_shape=[_sds(a.shape, a.dtype) for a in xs],
        scratch_shapes=[pltpu.SemaphoreType.DMA((n, 3)), pltpu.SemaphoreType.DMA((n, 3)), pltpu.SemaphoreType.DMA((n,))],
        name=name,
    )(*xs)


HBM = pl.BlockSpec(memory_space=pltpu.HBM)
SEM = pl.BlockSpec(memory_space=pltpu.SEMAPHORE)
DATAFLOW = pltpu.SideEffectType.DATAFLOW_SIDE_EFFECTING


def _in_hbm(a):
    return pltpu.with_memory_space_constraint(a, pltpu.HBM)


def _split_copy_start(name, srcs, lands, copies, deps):
    n, nd = len(srcs), len(deps)
    per = len(copies([None] * n, [None] * n, probe=True)) // n

    def body(*refs):
        s_refs, l_refs = refs[:n], refs[n:2 * n]
        send_sems, recv_sems = refs[2 * n + nd], refs[2 * n + nd + 1]
        token = refs[-1]
        for a, k, src, dst, to in copies(s_refs, l_refs):
            pltpu.make_async_remote_copy(src_ref=src, dst_ref=dst, send_sem=send_sems.at[a * per + k],
                                         recv_sem=recv_sems.at[a * per + k], device_id=to, device_id_type=MESH).start()
        token[...] = jnp.zeros_like(token)

    both = list(srcs) + list(lands)
    outs = pl.pallas_call(
        body, name=name,
        out_shape=(pltpu.SemaphoreType.DMA((n * per,)), pltpu.SemaphoreType.DMA((n * per,)),
                   *[pltpu.HBM(a.shape, a.dtype) for a in both], _sds((8, V7X_LANES), F32)),
        in_specs=[HBM] * (2 * n) + [ANY] * nd,
        out_specs=(SEM, SEM, *[HBM] * (2 * n), pl.BlockSpec(memory_space=pltpu.VMEM)),
        input_output_aliases={i: 2 + i for i in range(2 * n)},
        compiler_params=pltpu.CompilerParams(has_side_effects=DATAFLOW),
    )(*[_in_hbm(a) for a in both], *deps)
    return outs[0], outs[1], list(outs[2:2 + n]), list(outs[2 + n:2 + 2 * n]), outs[-1]


def _split_copy_wait(name, send_sems, recv_sems, srcs, lands, arrivals, after):
    n = len(srcs)
    per = len(arrivals([None] * n, [None] * n, probe=True)) // n

    def body(*refs):
        s_refs, l_refs = refs[:n], refs[n:2 * n]
        send_sems_, recv_sems_ = refs[2 * n], refs[2 * n + 1]
        for a, k, src, dst, frm in arrivals(s_refs, l_refs):
            cp = pltpu.make_async_remote_copy(src_ref=src, dst_ref=dst, send_sem=send_sems_.at[a * per + k],
                                              recv_sem=recv_sems_.at[a * per + k], device_id=frm, device_id_type=MESH)
            cp.wait_send()
            cp.wait_recv()

    both = list(srcs) + list(lands)
    outs = pl.pallas_call(
        body, name=name, out_shape=tuple(pltpu.HBM(a.shape, a.dtype) for a in both),
        in_specs=[HBM] * (2 * n) + [SEM, SEM, ANY], out_specs=tuple([HBM] * (2 * n)),
        input_output_aliases={i: i for i in range(2 * n)},
        compiler_params=pltpu.CompilerParams(has_side_effects=DATAFLOW),
    )(*both, send_sems, recv_sems, after)
    return list(outs[n:])


def _gather_copies(arriving):
    def copies(s_refs, l_refs, probe=False):
        if probe:
            return [None] * (4 * len(s_refs))
        x, y, c = _place()
        out = []
        for a in range(len(s_refs)):
            for k, (px, py, pc) in enumerate([(x, y, 1 - c), (1 - x, y, c), (x, 1 - y, c), (1 - x, 1 - y, c)]):
                slot = 4 * px + 2 * py + pc if arriving else 4 * x + 2 * y + c
                out.append((a, k, s_refs[a], l_refs[a].at[slot], (px, py, pc)))
        return out
    return copies


def _chip_copies(arriving):
    def copies(s_refs, l_refs, probe=False):
        if probe:
            return [None] * (3 * len(s_refs))
        x, y, c = _place()
        out = []
        for a in range(len(s_refs)):
            for j, (px, py) in enumerate([(1 - x, y), (x, 1 - y), (1 - x, 1 - y)]):
                src = s_refs[a].at[2 * x + y] if arriving else s_refs[a].at[2 * px + py]
                out.append((a, j, src, l_refs[a].at[j], (px, py, c)))
        return out
    return copies


def _gather_begin(name, shards, deps):
    lands = [lax.empty((N_DEV,) + a.shape, a.dtype) for a in shards]
    return _split_copy_start(name + "_start", shards, lands, _gather_copies(False), deps)


def _gather_end(name, handle, shards, after):
    send_sems, recv_sems, srcs, lands, _ = handle
    lands = _split_copy_wait(name + "_wait", send_sems, recv_sems, srcs, lands, _gather_copies(True), after)
    return _gather_forward(name + "_forward", shards, lands)


def _gather_forward(name, shards, lands):
    n = len(lands)

    def body(*refs):
        x_refs, o_refs = refs[:n], refs[2 * n:3 * n]
        send_sems, recv_sems, local_sems = refs[3 * n:]
        x, y, c = _place()
        sibling = (x, y, 1 - c)
        chips = [(1 - x, y), (x, 1 - y), (1 - x, 1 - y)]
        mine, sent = [], []
        for a in range(n):
            cp = pltpu.make_async_copy(x_refs[a], o_refs[a].at[4 * x + 2 * y + c], local_sems.at[a])
            cp.start()
            mine.append(cp)
            for j, (px, py) in enumerate(chips):
                rows = o_refs[a].at[4 * px + 2 * py + c]
                cp = pltpu.make_async_remote_copy(src_ref=rows, dst_ref=rows, send_sem=send_sems.at[a, j],
                                                  recv_sem=recv_sems.at[a, j], device_id=sibling, device_id_type=MESH)
                cp.start()
                sent.append(cp)
        for a in range(n):
            for j, (px, py) in enumerate(chips):
                rows = o_refs[a].at[4 * px + 2 * py + 1 - c]
                pltpu.make_async_remote_copy(src_ref=rows, dst_ref=rows, send_sem=send_sems.at[a, j],
                                             recv_sem=recv_sems.at[a, j], device_id=sibling, device_id_type=MESH).wait_recv()
        for cp in sent:
            cp.wait_send()
        for cp in mine:
            cp.wait()

    return pl.pallas_call(
        body, in_specs=[ANY] * (2 * n), out_specs=[ANY] * n, out_shape=[_sds(a.shape, a.dtype) for a in lands],
        input_output_aliases={n + i: i for i in range(n)},
        scratch_shapes=[pltpu.SemaphoreType.DMA((n, 3)), pltpu.SemaphoreType.DMA((n, 3)), pltpu.SemaphoreType.DMA((n,))],
        name=name,
    )(*shards, *lands)


def _chips_begin(name, pairs, deps):
    lands = [lax.empty((3,) + a.shape[1:], a.dtype) for a in pairs]
    return _split_copy_start(name + "_start", pairs, lands, _chip_copies(False), deps)


def _chips_end(name, handle, after):
    send_sems, recv_sems, srcs, lands, _ = handle
    return _split_copy_wait(name + "_wait", send_sems, recv_sems, srcs, lands, _chip_copies(True), after)


def _adamw_layer(name, l, own, lands, w, m, v, prev):
    nl, r, c = w.shape
    tr, tc = _elem_tiles(r, c)
    c1 = 1.0 - ADAM_B1 ** ADAM_STEP
    c2 = 1.0 - ADAM_B2 ** ADAM_STEP

    def body(own_ref, lands_ref, w_ref, m_ref, v_ref, *rest):
        g_ref, d_ref, nm_ref, nv_ref = rest[-4:]
        g = own_ref[...].astype(F32) + lands_ref[0].astype(F32) + lands_ref[1].astype(F32) + lands_ref[2].astype(F32)
        nm = ADAM_B1 * m_ref[...] + (1.0 - ADAM_B1) * g
        nv = ADAM_B2 * v_ref[...] + (1.0 - ADAM_B2) * (g * g)
        m_hat = nm / c1
        v_hat = nv / c2
        g_ref[...] = g
        nm_ref[...] = nm
        nv_ref[...] = nv
        d_ref[...] = -ADAM_LR * (m_hat / (jnp.sqrt(v_hat) + ADAM_EPS) + ADAM_WD * w_ref[...])

    lay = pl.BlockSpec((None, tr, tc), lambda i, j: (l, i, j))
    out = _sds((nl, r, c), F32)
    prev = [] if prev is None else list(prev)
    return pl.pallas_call(
        body, grid=(r // tr, c // tc),
        in_specs=[pl.BlockSpec((tr, tc), lambda i, j: (i, j)), pl.BlockSpec((3, tr, tc), lambda i, j: (0, i, j)), lay, lay, lay]
        + [ANY] * len(prev),
        out_specs=[lay] * 4, out_shape=[out] * 4, input_output_aliases={5 + i: i for i in range(len(prev))},
        compiler_params=_cp(), name=name,
    )(own, lands, w, m, v, *prev)


def _pad_pairs(a, axis, half, half_pad):
    shp = a.shape
    a = a.reshape(shp[:axis] + (2, half) + shp[axis + 1:])
    pad = [(0, 0)] * a.ndim
    pad[axis + 1] = (0, half_pad - half)
    a = jnp.pad(a, pad)
    return a.reshape(shp[:axis] + (2 * half_pad,) + shp[axis + 1:])


def _unpad_pairs(a, axis, half, half_pad):
    shp = a.shape
    a = a.reshape(shp[:axis] + (2, half_pad) + shp[axis + 1:])
    a = lax.slice_in_dim(a, 0, half, axis=axis + 1)
    return a.reshape(shp[:axis] + (2 * half,) + shp[axis + 1:])


def _blockdiag(w, nt):
    g, a, b = w.shape
    gl = g // nt
    e = jnp.eye(gl, dtype=w.dtype).reshape(1, gl, 1, gl, 1)
    return (w.reshape(nt, gl, a, 1, b) * e).reshape(nt, gl * a, gl * b)


def _diagblocks(m, g, a, b):
    nt = m.shape[0]
    gl = g // nt
    d = jnp.diagonal(m.reshape(nt, gl, a, gl, b), axis1=1, axis2=3)
    return jnp.moveaxis(d, -1, 1).reshape(g, a, b)


def _ssm_discretise(lam_re, lam_im, log_dt, b_re, b_im):
    dt = jnp.exp(log_dt)[:, None]
    mag = jnp.exp(lam_re * dt)
    ab_re, ab_im = mag * jnp.cos(lam_im * dt), mag * jnp.sin(lam_im * dt)
    nr, ni = ab_re - 1.0, ab_im
    den = lam_re * lam_re + lam_im * lam_im
    zr = (nr * lam_re + ni * lam_im) / den
    zi = (ni * lam_re - nr * lam_im) / den
    bbr = zr[..., None] * b_re - zi[..., None] * b_im
    bbi = zr[..., None] * b_im + zi[..., None] * b_re
    return ab_re, ab_im, bbr, bbi


def _rope_tables(s):
    half = HEAD_DIM // 2
    inv = ROPE_THETA ** (-jnp.arange(half, dtype=F32) / half)
    ang = jnp.arange(s).astype(F32)[:, None] * inv[None, :]
    cos, sin = jnp.cos(ang), jnp.sin(ang)
    reps = V7X_LANES // HEAD_DIM
    return jnp.tile(jnp.concatenate([cos, cos], -1), (1, reps)), jnp.tile(jnp.concatenate([-sin, sin], -1), (1, reps))


_SMALL = ("attn_sinks", "pool_w", "pool_scale", "ssm_lam_re", "ssm_lam_im", "ssm_log_dt", "ssm_b_re", "ssm_b_im",
          "ssm_c_re", "ssm_c_im", "ssm_d", "ln1_g", "ln1_b", "ffn_conv_b", "ln2_g", "ln2_b")
_BIG = ("w_in", "ssm_glu_w", "w_out", "ffn_w_up", "ffn_w_down")
_ORDER = ("w_in", "attn_sinks", "pool_w", "pool_scale", "ssm_lam_re", "ssm_lam_im", "ssm_log_dt", "ssm_b_re", "ssm_b_im",
          "ssm_c_re", "ssm_c_im", "ssm_d", "ssm_glu_w", "w_out", "ln1_g", "ln1_b", "ffn_w_up", "ffn_conv_w", "ffn_conv_b",
          "ffn_w_down", "ln2_g", "ln2_b")


def kernel(x, w_in, attn_sinks, pool_w, pool_scale, ssm_lam_re, ssm_lam_im, ssm_log_dt, ssm_b_re, ssm_b_im, ssm_c_re, ssm_c_im, ssm_d, ssm_glu_w, w_out, ln1_g, ln1_b, ffn_w_up, ffn_conv_w, ffn_conv_b, ffn_w_down, ln2_g, ln2_b, loss_target, m_w_in, m_attn_sinks, m_pool_w, m_pool_scale, m_ssm_lam_re, m_ssm_lam_im, m_ssm_log_dt, m_ssm_b_re, m_ssm_b_im, m_ssm_c_re, m_ssm_c_im, m_ssm_d, m_ssm_glu_w, m_w_out, m_ln1_g, m_ln1_b, m_ffn_w_up, m_ffn_conv_w, m_ffn_conv_b, m_ffn_w_down, m_ln2_g, m_ln2_b, v_w_in, v_attn_sinks, v_pool_w, v_pool_scale, v_ssm_lam_re, v_ssm_lam_im, v_ssm_log_dt, v_ssm_b_re, v_ssm_b_im, v_ssm_c_re, v_ssm_c_im, v_ssm_d, v_ssm_glu_w, v_w_out, v_ln1_g, v_ln1_b, v_ffn_w_up, v_ffn_conv_w, v_ffn_conv_b, v_ffn_w_down, v_ln2_g, v_ln2_b):
    W = dict(w_in=w_in, attn_sinks=attn_sinks, pool_w=pool_w, pool_scale=pool_scale, ssm_lam_re=ssm_lam_re, ssm_lam_im=ssm_lam_im, ssm_log_dt=ssm_log_dt, ssm_b_re=ssm_b_re, ssm_b_im=ssm_b_im, ssm_c_re=ssm_c_re, ssm_c_im=ssm_c_im, ssm_d=ssm_d, ssm_glu_w=ssm_glu_w, w_out=w_out, ln1_g=ln1_g, ln1_b=ln1_b, ffn_w_up=ffn_w_up, ffn_conv_w=ffn_conv_w, ffn_conv_b=ffn_conv_b, ffn_w_down=ffn_w_down, ln2_g=ln2_g, ln2_b=ln2_b)
    M = dict(w_in=m_w_in, attn_sinks=m_attn_sinks, pool_w=m_pool_w, pool_scale=m_pool_scale, ssm_lam_re=m_ssm_lam_re, ssm_lam_im=m_ssm_lam_im, ssm_log_dt=m_ssm_log_dt, ssm_b_re=m_ssm_b_re, ssm_b_im=m_ssm_b_im, ssm_c_re=m_ssm_c_re, ssm_c_im=m_ssm_c_im, ssm_d=m_ssm_d, ssm_glu_w=m_ssm_glu_w, w_out=m_w_out, ln1_g=m_ln1_g, ln1_b=m_ln1_b, ffn_w_up=m_ffn_w_up, ffn_conv_w=m_ffn_conv_w, ffn_conv_b=m_ffn_conv_b, ffn_w_down=m_ffn_w_down, ln2_g=m_ln2_g, ln2_b=m_ln2_b)
    V = dict(w_in=v_w_in, attn_sinks=v_attn_sinks, pool_w=v_pool_w, pool_scale=v_pool_scale, ssm_lam_re=v_ssm_lam_re, ssm_lam_im=v_ssm_lam_im, ssm_log_dt=v_ssm_log_dt, ssm_b_re=v_ssm_b_re, ssm_b_im=v_ssm_b_im, ssm_c_re=v_ssm_c_re, ssm_c_im=v_ssm_c_im, ssm_d=v_ssm_d, ssm_glu_w=v_ssm_glu_w, w_out=v_w_out, ln1_g=v_ln1_g, ln1_b=v_ln1_b, ffn_w_up=v_ffn_w_up, ffn_conv_w=v_ffn_conv_w, ffn_conv_b=v_ffn_conv_b, ffn_w_down=v_ffn_w_down, ln2_g=v_ln2_g, ln2_b=v_ln2_b)

    depth = w_in.shape[0]
    s, d = x.shape[1], x.shape[2]
    alpha = (2 * depth) ** 0.25
    attn_w = d // 2
    kv_w = attn_w // GQA
    nkv = kv_w // HEAD_DIM
    pool_wd = d // 4
    ssm_wd = d // 4
    n_groups = ssm_wd // SSM_GROUP
    state_w = n_groups * SSM_STATE
    nt_ssm = max(1, state_w // 512)
    o_k, o_v, o_p, o_s = attn_w, attn_w + kv_w, attn_w + 2 * kv_w, attn_w + 2 * kv_w + pool_wd
    in_w = o_s + ssm_wd
    half = ffn_w_down.shape[1]
    half_pad = -(-half // 64) * 64
    ffp = 4 * 2 * half_pad
    xi, yi, ci = _place()
    me = 4 * xi + 2 * yi + ci
    c_idx = jnp.reshape(ci, (1,)).astype(jnp.int32)

    cos_t, sin_t = _rope_tables(s)

    def layer_shards(l):
        return [
            w_in[l].astype(_WIRE), ssm_glu_w[l].astype(_WIRE), w_out[l].astype(_WIRE),
            _pad_pairs(ffn_w_up[l].astype(_WIRE), 1, half, half_pad),
            jnp.pad(ffn_w_down[l].astype(_WIRE), ((0, half_pad - half), (0, 0))),
        ]

    (g_cw,) = _all_gather("gather_conv_w", [_pad_pairs(ffn_conv_w, 2, half, half_pad)])

    def full_weights(l, gathered):
        g_in, g_glu, g_out, g_up, g_down = gathered
        return dict(
            win=jnp.transpose(g_in, (1, 0, 2)).reshape(d, in_w),
            glu=jnp.transpose(g_glu, (1, 0, 2)).reshape(ssm_wd, 2 * ssm_wd),
            wout=g_out.reshape(d, d),
            wup=g_up,
            wdown=g_down.reshape(ffp, d),
            cw=jnp.transpose(g_cw[:, l], (1, 0, 2)).reshape(CONV_WIDTH, 2 * ffp),
            cb=_pad_pairs(ffn_conv_b[l].reshape(N_DEV, 2 * half), 1, half, half_pad).reshape(1, 2 * ffp),
        )

    shards = [layer_shards(l) for l in range(depth)]
    full = [None] * depth
    handle = _gather_begin("gather_weights_0", shards[0], [g_cw])
    full[0] = full_weights(0, _gather_end("gather_weights_0", handle, shards[0], g_cw))

    def ssm_params(l):
        return (ssm_lam_re[l], ssm_lam_im[l], ssm_log_dt[l], ssm_b_re[l], ssm_b_im[l])

    saved = []
    xf = x[0]
    xb = xf.astype(_MXU)
    for l in range(depth):
        fw = full[l]
        deps = ()
        if l + 1 < depth:
            handle = _gather_begin(f"gather_weights_{l + 1}", shards[l + 1], [fw["wup"]])
            deps = (handle[-1],)
        h = _mm_nn(f"in_proj_{l}", xb, fw["win"], deps=deps)
        q_rot, k_rot = _rope(f"rope_{l}", h, o_v, cos_t, sin_t, _MXU, ((0, o_k), (o_k, o_v)))
        k_hm = jnp.transpose(k_rot.reshape(s, nkv, HEAD_DIM), (1, 0, 2))
        v_hm = jnp.transpose(h[:, o_v:o_p].astype(_MXU).reshape(s, nkv, HEAD_DIM), (1, 0, 2))
        sinks = attn_sinks[l].reshape(nkv, GQA)
        o_attn, lse = _attn_fwd(f"attn_{l}", q_rot, k_hm, v_hm, sinks)
        pw_b = pool_w[l].astype(_MXU)
        psc = pool_scale[l].reshape(1, pool_wd)
        y_pool, pre = _pool_fwd(f"pool_{l}", h, o_p // pool_wd, pw_b, psc)
        ab_re, ab_im, bbr, bbi = _ssm_discretise(*ssm_params(l))
        bdr = _blockdiag(jnp.transpose(bbr, (0, 2, 1)), nt_ssm).astype(_MXU)
        bdi = _blockdiag(jnp.transpose(bbi, (0, 2, 1)), nt_ssm).astype(_MXU)
        cdr = _blockdiag(jnp.transpose(ssm_c_re[l], (0, 2, 1)), nt_ssm).astype(_MXU)
        cdi = _blockdiag(jnp.transpose(ssm_c_im[l], (0, 2, 1)), nt_ssm).astype(_MXU)
        dvec = ssm_d[l].reshape(1, ssm_wd)
        ar, ai = ab_re.reshape(1, state_w), ab_im.reshape(1, state_w)
        cw_ssm = ssm_wd // nt_ssm
        sr, si, ypre, yg = _ssm_fwd(f"ssm_{l}", h, o_s // cw_ssm, bdr, bdi, cdr, cdi, dvec, ar, ai)
        y_ssm, ab2 = _glu_fwd(f"glu_{l}", yg, fw["glu"])
        mix = jnp.concatenate([o_attn.astype(_MXU), y_pool, y_ssm], -1)
        a1 = _mm_nn(f"out_proj_{l}", mix, fw["wout"])
        g1, b1 = ln1_g[l].reshape(1, d), ln1_b[l].reshape(1, d)
        x1, x1b, xh1, rs1 = _ln_fwd(f"ln1_{l}", xf, a1, g1, b1, alpha)
        wup = fw["wup"]
        hu = _mm(f"ffn_up_{l}", x1b, wup, NN, (N_DEV, 1), _resident((s, d), lambda j, kk: (0, 0)),
                 pl.BlockSpec((None, d, 2 * half_pad), lambda j, kk: (j, 0, 0)),
                 pl.BlockSpec((s, 2 * half_pad), lambda j, kk: (0, j)), (s, 2 * ffp), F32)
        act = _conv_act_fwd(f"ffn_act_{l}", hu, fw["cw"], fw["cb"])
        tkd = 2 * half_pad
        tnd = _col_tile(d, 1024)
        f_out = _mm(f"ffn_down_{l}", act, fw["wdown"], NN, (d // tnd, ffp // tkd),
                    pl.BlockSpec((s, tkd), lambda j, kk: (0, kk)), pl.BlockSpec((tkd, tnd), lambda j, kk: (kk, j)),
                    pl.BlockSpec((s, tnd), lambda j, kk: (0, j)), (s, d), F32)
        g2, b2 = ln2_g[l].reshape(1, d), ln2_b[l].reshape(1, d)
        x2, x2b, xh2, rs2 = _ln_fwd(f"ln2_{l}", x1, f_out, g2, b2, alpha)
        saved.append(dict(xb=xb, h=h, q_rot=q_rot, k_hm=k_hm, v_hm=v_hm, sinks=sinks, o_attn=o_attn, lse=lse, pw_b=pw_b, psc=psc,
                          pre=pre, bdr=bdr, bdi=bdi, cdr=cdr, cdi=cdi, dvec=dvec, ar=ar, ai=ai, sr=sr, si=si, ypre=ypre, yg=yg,
                          ab2=ab2, mix=mix, g1=g1, xh1=xh1, rs1=rs1, x1b=x1b, hu=hu, act=act, g2=g2, xh2=xh2, rs2=rs2))
        xf, xb = x2, x2b
        if l + 1 < depth:
            full[l + 1] = full_weights(l + 1, _gather_end(f"gather_weights_{l + 1}", handle, shards[l + 1], x2b))

    dy, loss_part = _loss_head("loss_head", xf, loss_target[0])
    loss = lax.psum(loss_part[0, 0], ("x", "y", "c"))

    small_g = {k: [None] * depth for k in _SMALL}
    cw_g = [None] * depth
    outs = {}
    big_res = {k: None for k in _BIG}
    my_chip = 2 * xi + yi
    in_flight = None

    def finish_exchange(after):
        lay, handle, own = in_flight
        lands = _chips_end(f"grads_between_chips_{lay}", handle, after)
        lands[3] = _unpad_pairs(lands[3], 2, half, half_pad)
        own[3] = _unpad_pairs(own[3], 1, half, half_pad)
        for name_, o, ld in zip(_BIG, own, lands):
            big_res[name_] = _adamw_layer(f"adamw_{name_}_{lay}", lay, o, ld, W[name_], M[name_], V[name_], big_res[name_])

    for l in reversed(range(depth)):
        fw, sv = full[l], saved[l]
        deps = () if in_flight is None else (in_flight[1][-1],)
        dr2, dr2b, dg2, db2 = _ln_bwd(f"ln2_bwd_{l}", dy, sv["xh2"], sv["rs2"], sv["g2"], deps=deps)
        d_wdown = _mm_tn_acols(f"ffn_down_dw_{l}", sv["act"], dr2b, _WIRE, cap=2 * half_pad)
        dact = _mm_nt(f"ffn_down_dx_{l}", dr2b, fw["wdown"], cap=2 * half_pad)
        dhu, dcw, dcb = _conv_act_bwd(f"ffn_act_bwd_{l}", dact, sv["hu"], fw["cw"], fw["cb"])
        d_wup = _mm(f"ffn_up_dw_{l}", sv["x1b"], dhu, TN, (N_DEV, 1), _resident((s, d), lambda j, kk: (0, 0)),
                    pl.BlockSpec((None, s, 2 * half_pad), lambda j, kk: (j // 4, 0, j % 4)),
                    pl.BlockSpec((None, d, 2 * half_pad), lambda j, kk: (j, 0, 0)), (N_DEV, d, 2 * half_pad), _WIRE)
        tnd = _col_tile(d, 512)
        dy1 = _mm(f"ffn_up_dx_{l}", dhu, fw["wup"], NT, (d // tnd, N_DEV),
                  pl.BlockSpec((None, s, 2 * half_pad), lambda j, kk: (kk // 4, 0, kk % 4)),
                  pl.BlockSpec((None, tnd, 2 * half_pad), lambda j, kk: (kk, j, 0)),
                  pl.BlockSpec((s, tnd), lambda j, kk: (0, j)), (s, d), F32,
                  add=dr2, add_spec=pl.BlockSpec((s, tnd), lambda j, kk: (0, j)), add_scale=alpha)
        dr1, dr1b, dg1, db1 = _ln_bwd(f"ln1_bwd_{l}", dy1, sv["xh1"], sv["rs1"], sv["g1"])
        d_wout = _mm_tn_acols(f"out_proj_dw_{l}", sv["mix"], dr1b, _WIRE, cap=d // N_DEV)
        dmix = _mm_nt(f"out_proj_dx_{l}", dr1b, fw["wout"])
        dq_rot, dk_hm, dv_hm, dsk = _attn_bwd(f"attn_bwd_{l}", sv["q_rot"], sv["k_hm"], sv["v_hm"], sv["o_attn"], dmix,
                                             sv["lse"], sv["sinks"])
        dqk = jnp.concatenate([dq_rot, jnp.transpose(dk_hm, (1, 0, 2)).reshape(s, kv_w)], -1)
        dhq, dhk = _rope(f"rope_bwd_{l}", dqk, o_v, cos_t, -sin_t, _MXU, ((0, o_k), (o_k, o_v)))
        dhv = jnp.transpose(dv_hm, (1, 0, 2)).reshape(s, kv_w).astype(_MXU)
        dhp, dpw, dpsc = _pool_bwd(f"pool_bwd_{l}", dmix, attn_w // pool_wd, sv["pre"], sv["pw_b"], sv["psc"])
        dab2, dyg = _glu_bwd(f"glu_bwd_{l}", dmix, (attn_w + pool_wd) // ssm_wd, sv["ab2"], fw["glu"])
        d_glu = _mm_tn_bcols(f"glu_dw_{l}", sv["yg"], dab2, _WIRE)
        cw_ssm = ssm_wd // nt_ssm
        dhs, dd, dcdr, dcdi, dbdr, dbdi, dar, dai = _ssm_bwd(
            f"ssm_bwd_{l}", dyg, sv["ypre"], sv["h"], o_s // cw_ssm, sv["sr"], sv["si"], sv["bdr"], sv["bdi"], sv["cdr"],
            sv["cdi"], sv["dvec"], sv["ar"], sv["ai"])
        dh = jnp.concatenate([dhq, dhk, dhv, dhp, dhs], -1)
        d_win = _mm_tn_bcols(f"in_proj_dw_{l}", sv["xb"], dh, _WIRE)
        dy = _mm_nt(f"in_proj_dx_{l}", dh, fw["win"], add=dr1, add_scale=alpha)

        dbbr = jnp.transpose(_diagblocks(dbdr, n_groups, SSM_GROUP, SSM_STATE), (0, 2, 1))
        dbbi = jnp.transpose(_diagblocks(dbdi, n_groups, SSM_GROUP, SSM_STATE), (0, 2, 1))
        _, vjp = jax.vjp(_ssm_discretise, *ssm_params(l))
        dlr, dli, dldt, dbr, dbi = vjp((dar.reshape(n_groups, SSM_STATE), dai.reshape(n_groups, SSM_STATE), dbbr, dbbi))
        small_g["attn_sinks"][l] = dsk.reshape(-1)
        small_g["pool_w"][l] = dpw
        small_g["pool_scale"][l] = dpsc.reshape(-1)
        small_g["ssm_lam_re"][l], small_g["ssm_lam_im"][l], small_g["ssm_log_dt"][l] = dlr, dli, dldt
        small_g["ssm_b_re"][l], small_g["ssm_b_im"][l] = dbr, dbi
        small_g["ssm_c_re"][l] = jnp.transpose(_diagblocks(dcdr, n_groups, SSM_STATE, SSM_GROUP), (0, 2, 1))
        small_g["ssm_c_im"][l] = jnp.transpose(_diagblocks(dcdi, n_groups, SSM_STATE, SSM_GROUP), (0, 2, 1))
        small_g["ssm_d"][l] = dd.reshape(n_groups, SSM_GROUP)
        small_g["ln1_g"][l], small_g["ln1_b"][l] = dg1.reshape(-1), db1.reshape(-1)
        small_g["ln2_g"][l], small_g["ln2_b"][l] = dg2.reshape(-1), db2.reshape(-1)
        small_g["ffn_conv_b"][l] = _unpad_pairs(dcb.reshape(N_DEV, 2 * half_pad), 1, half, half_pad).reshape(-1)
        cw_g[l] = _unpad_pairs(dcw.reshape(CONV_WIDTH, N_DEV, 2 * half_pad), 2, half, half_pad)

        stacked = [
            jnp.transpose(d_win.reshape(d, N_DEV, in_w // N_DEV), (1, 0, 2)),
            jnp.transpose(d_glu.reshape(ssm_wd, N_DEV, 2 * ssm_wd // N_DEV), (1, 0, 2)),
            d_wout.reshape(N_DEV, d // N_DEV, d),
            d_wup,
            d_wdown.reshape(N_DEV, half_pad, d),
        ]
        by_owner = [a.reshape((4, 2) + a.shape[1:]) for a in stacked]
        theirs = _swap_sibling(f"grads_to_sibling_{l}", by_owner)
        pair = [_pair_sum(f"pair_sum_{l}_{i}", a, b, c_idx) for i, (a, b) in enumerate(zip(by_owner, theirs))]
        if in_flight is not None:
            finish_exchange(dy)
        handle = _chips_begin(f"grads_between_chips_{l}", pair, [])
        in_flight = (l, handle, [lax.dynamic_index_in_dim(p, my_chip, 0, keepdims=False) for p in pair])

    flat_parts = [jnp.stack(small_g[k]).reshape(-1) for k in _SMALL] + [jnp.stack(cw_g).reshape(-1)]
    sizes = [a.shape[0] for a in flat_parts]
    total = sum(sizes)
    rows = -(-total // (512 * V7X_LANES)) * 512
    flat = jnp.pad(jnp.concatenate(flat_parts), (0, rows * V7X_LANES - total)).reshape(rows, V7X_LANES)
    (gathered,) = _all_gather("gather_small_grads", [flat])
    gathered = gathered.reshape(N_DEV, -1)
    n_rep = sum(sizes[:-1])
    cw_all = gathered[:, n_rep:total].reshape(N_DEV, depth, CONV_WIDTH, N_DEV, 2 * half)
    cw_mine = lax.dynamic_index_in_dim(cw_all, me, axis=3, keepdims=False)
    n_small = n_rep + cw_mine[0].size
    rows2 = -(-n_small // (512 * V7X_LANES)) * 512

    def flatten(parts_):
        return jnp.pad(jnp.concatenate(parts_, -1), [(0, 0)] * (parts_[0].ndim - 1) + [(0, rows2 * V7X_LANES - n_small)])

    p_small = flatten([gathered[:, :n_rep], cw_mine.reshape(N_DEV, -1)]).reshape(N_DEV, 1, rows2, V7X_LANES)
    w_small, m_small, v_small = (
        flatten([jnp.concatenate([t[k].reshape(-1) for k in _SMALL]), t["ffn_conv_w"].reshape(-1)]).reshape(1, rows2, V7X_LANES)
        for t in (W, M, V))
    res_small = [a.reshape(-1) for a in _reduce_adamw("adamw_small", p_small, w_small, m_small, v_small)]
    off = 0
    for k in _SMALL + ("ffn_conv_w",):
        n = W[k].size
        outs[k] = tuple(a[off:off + n].reshape(W[k].shape) for a in res_small)
        off += n

    finish_exchange(res_small[0])
    for name_ in _BIG:
        outs[name_] = tuple(big_res[name_])

    grad_x = dy[None]
    result = [loss, grad_x]
    for i in range(4):
        result += [outs[k][i] for k in _ORDER]
    return tuple(result)
```

```python
import functools
import math

import jax
import jax.numpy as jnp
from jax import lax
from jax.experimental import pallas as pl
from jax.experimental.pallas import tpu as pltpu

F32 = jnp.float32
_MXU = jnp.bfloat16
_WIRE = jnp.bfloat16

HEAD_DIM = 64
GQA = 4
ATTN_BLOCK = 128
ROPE_THETA = 10000.0
POOL_WINDOWS = (2, 4, 8, 16)
SSM_GROUP = 16
SSM_STATE = 64
CONV_WIDTH = 3
LN_EPS = 1e-5
ADAM_LR, ADAM_B1, ADAM_B2, ADAM_EPS, ADAM_WD, ADAM_STEP = 0.001, 0.9, 0.999, 1e-08, 0.01, 10

N_DEV = 8
V7X_LANES = 128
V7X_VMEM_LIMIT = 56 * 1024 * 1024
SCAN_T = 64
SCAN_LANES = 256
MESH = pl.DeviceIdType.MESH
ANY = pl.BlockSpec(memory_space=pl.ANY)


def _cp():
    return pltpu.CompilerParams(vmem_limit_bytes=V7X_VMEM_LIMIT)


def _resident(block, index_map):
    return pl.BlockSpec(block, index_map, pipeline_mode=pl.Buffered(1))


def _sds(shape, dtype):
    return jax.ShapeDtypeStruct(tuple(shape), dtype)


def _mm(name, a, b, dims, grid, a_spec, b_spec, o_spec, out_shape, out_dtype, add=None, add_spec=None, add_scale=1.0, deps=()):
    nk = grid[1]
    n_in = 2 + (add is not None) + len(deps)
    oblk = tuple(d for d in o_spec.block_shape if d is not None)

    def body(*refs):
        a_ref, b_ref = refs[:2]
        add_ref = None if add is None else refs[2]
        o_ref = refs[n_in]
        acc_ref = refs[-1] if nk > 1 else None

        def finish(r):
            if add_ref is not None:
                r = r + add_scale * add_ref[...]
            o_ref[...] = r.astype(o_ref.dtype)

        part = lax.dot_general(a_ref[...], b_ref[...], (dims, ((), ())), preferred_element_type=F32)
        if nk == 1:
            finish(part)
        else:
            k = pl.program_id(1)

            @pl.when(k == 0)
            def _():
                acc_ref[...] = part

            @pl.when(k > 0)
            def _():
                acc_ref[...] += part

            @pl.when(k == nk - 1)
            def _():
                finish(acc_ref[...])

    ins = [a, b] + ([] if add is None else [add]) + list(deps)
    in_specs = [a_spec, b_spec] + ([] if add is None else [add_spec]) + [ANY] * len(deps)
    return pl.pallas_call(
        body, grid=grid, in_specs=in_specs, out_specs=o_spec, out_shape=_sds(out_shape, out_dtype),
        scratch_shapes=[pltpu.VMEM(oblk, F32)] if nk > 1 else [], compiler_params=_cp(), name=name,
    )(*ins)


NN = ((1,), (0,))
NT = ((1,), (1,))
TN = ((0,), (0,))


def _col_tile(n, cap=512):
    if n % V7X_LANES:
        return n
    t = min(cap, n)
    t -= t % V7X_LANES
    while n % t:
        t -= V7X_LANES
    return t


def _mm_nn(name, a, b, out_dtype=F32, cap=512, add=None, add_scale=1.0):
    m, k = a.shape
    n = b.shape[1]
    tn = _col_tile(n, cap)
    o_spec = pl.BlockSpec((m, tn), lambda j, kk: (0, j))
    return _mm(name, a, b, NN, (n // tn, 1), _resident((m, k), lambda j, kk: (0, 0)),
               pl.BlockSpec((k, tn), lambda j, kk: (0, j)), o_spec, (m, n), out_dtype,
               add=add, add_spec=None if add is None else o_spec, add_scale=add_scale)


def _mm_nt(name, a, b, out_dtype=F32, add=None, add_scale=1.0, cap=512, deps=()):
    m, k = a.shape
    n = b.shape[0]
    tn = _col_tile(n, cap)
    o_spec = pl.BlockSpec((m, tn), lambda j, kk: (0, j))
    return _mm(name, a, b, NT, (n // tn, 1), _resident((m, k), lambda j, kk: (0, 0)),
               pl.BlockSpec((tn, k), lambda j, kk: (j, 0)), o_spec, (m, n), out_dtype,
               add=add, add_spec=None if add is None else o_spec, add_scale=add_scale, deps=deps)


def _mm_tn_bcols(name, a, b, out_dtype, cap=512):
    s, k = a.shape
    n = b.shape[1]
    tn = _col_tile(n, cap)
    return _mm(name, a, b, TN, (n // tn, 1), _resident((s, k), lambda j, kk: (0, 0)),
               pl.BlockSpec((s, tn), lambda j, kk: (0, j)), pl.BlockSpec((k, tn), lambda j, kk: (0, j)), (k, n), out_dtype)


def _mm_tn_acols(name, a, b, out_dtype, cap=512):
    s, k = a.shape
    n = b.shape[1]
    tk = _col_tile(k, cap)
    return _mm(name, a, b, TN, (k // tk, 1), pl.BlockSpec((s, tk), lambda i, kk: (0, i)),
               _resident((s, n), lambda i, kk: (0, 0)), pl.BlockSpec((tk, n), lambda i, kk: (i, 0)), (k, n), out_dtype)


def _ln_fwd(name, x, a, g, b, alpha):
    s, d = x.shape
    tr = min(256, s)

    def body(x_ref, a_ref, g_ref, b_ref, y_ref, yb_ref, xh_ref, rs_ref):
        r = alpha * x_ref[...] + a_ref[...]
        mu = jnp.mean(r, -1, keepdims=True)
        c = r - mu
        var = jnp.mean(c * c, -1, keepdims=True)
        rstd = lax.rsqrt(var + LN_EPS)
        xh = c * rstd
        y = xh * g_ref[...] + b_ref[...]
        y_ref[...] = y
        yb_ref[...] = y.astype(_MXU)
        xh_ref[...] = xh
        rs_ref[...] = rstd

    row = pl.BlockSpec((tr, d), lambda i: (i, 0))
    vec = pl.BlockSpec((1, d), lambda i: (0, 0))
    return pl.pallas_call(
        body, grid=(s // tr,), in_specs=[row, row, vec, vec],
        out_specs=[row, row, row, pl.BlockSpec((tr, 1), lambda i: (i, 0))],
        out_shape=[_sds((s, d), F32), _sds((s, d), _MXU), _sds((s, d), F32), _sds((s, 1), F32)],
        compiler_params=_cp(), name=name,
    )(x, a, g, b)


def _ln_bwd(name, dy, xh, rstd, g, deps=()):
    s, d = dy.shape
    tr = min(256, s)
    nd = len(deps)

    def body(dy_ref, xh_ref, rs_ref, g_ref, *rest):
        dr_ref, drb_ref, dg_ref, db_ref = rest[nd:]
        i = pl.program_id(0)
        dy_ = dy_ref[...]
        xh_ = xh_ref[...]
        dxh = dy_ * g_ref[...]
        m1 = jnp.mean(dxh, -1, keepdims=True)
        m2 = jnp.mean(dxh * xh_, -1, keepdims=True)
        dr = rs_ref[...] * (dxh - m1 - xh_ * m2)
        dr_ref[...] = dr
        drb_ref[...] = dr.astype(_MXU)
        pg = jnp.sum(dy_ * xh_, 0, keepdims=True)
        pb = jnp.sum(dy_, 0, keepdims=True)

        @pl.when(i == 0)
        def _():
            dg_ref[...] = pg
            db_ref[...] = pb

        @pl.when(i > 0)
        def _():
            dg_ref[...] += pg
            db_ref[...] += pb

    row = pl.BlockSpec((tr, d), lambda i: (i, 0))
    vec = pl.BlockSpec((1, d), lambda i: (0, 0))
    return pl.pallas_call(
        body, grid=(s // tr,), in_specs=[row, row, pl.BlockSpec((tr, 1), lambda i: (i, 0)), vec] + [ANY] * nd,
        out_specs=[row, row, vec, vec],
        out_shape=[_sds((s, d), F32), _sds((s, d), _MXU), _sds((1, d), F32), _sds((1, d), F32)],
        compiler_params=_cp(), name=name,
    )(dy, xh, rstd, g, *deps)


def _loss_head(name, y, target):
    s, d = y.shape
    tr = min(256, s)

    def body(y_ref, t_ref, dy_ref, l_ref):
        i = pl.program_id(0)
        e = y_ref[...] - t_ref[...]
        dy_ref[...] = e * (1.0 / d)
        part = 0.5 * jnp.sum(jnp.mean(e * e, -1, keepdims=True), 0, keepdims=True)

        @pl.when(i == 0)
        def _():
            l_ref[...] = part

        @pl.when(i > 0)
        def _():
            l_ref[...] += part

    row = pl.BlockSpec((tr, d), lambda i: (i, 0))
    return pl.pallas_call(
        body, grid=(s // tr,), in_specs=[row, row], out_specs=[row, pl.BlockSpec((1, 1), lambda i: (0, 0))],
        out_shape=[_sds((s, d), F32), _sds((1, 1), F32)], compiler_params=_cp(), name=name,
    )(y, target)


def _rope(name, t, width, cos, sin, out_dtype, splits):
    s = t.shape[0]
    tr = min(256, s)
    assert width % V7X_LANES == 0

    def body(t_ref, c_ref, s_ref, *o_refs):
        lane = lax.broadcasted_iota(jnp.int32, (tr, V7X_LANES), 1)
        first = (lane % HEAD_DIM) < (HEAD_DIM // 2)
        cs, sn = c_ref[...], s_ref[...]
        for (lo, hi), o_ref in zip(splits, o_refs):
            for c0 in range(lo, hi, V7X_LANES):
                v = t_ref[:, c0:c0 + V7X_LANES].astype(F32)
                partner = jnp.where(first, pltpu.roll(v, V7X_LANES - HEAD_DIM // 2, 1), pltpu.roll(v, HEAD_DIM // 2, 1))
                o_ref[:, c0 - lo:c0 - lo + V7X_LANES] = (v * cs + partner * sn).astype(o_ref.dtype)

    tab = pl.BlockSpec((tr, V7X_LANES), lambda i: (i, 0))
    return pl.pallas_call(
        body, grid=(s // tr,), in_specs=[pl.BlockSpec((tr, width), lambda i: (i, 0)), tab, tab],
        out_specs=[pl.BlockSpec((tr, hi - lo), lambda i: (i, 0)) for lo, hi in splits],
        out_shape=[_sds((s, hi - lo), out_dtype) for lo, hi in splits], compiler_params=_cp(), name=name,
    )(t, cos, sin)


def _attn_masks():
    i = lax.broadcasted_iota(jnp.int32, (ATTN_BLOCK, ATTN_BLOCK), 0)
    j = lax.broadcasted_iota(jnp.int32, (ATTN_BLOCK, ATTN_BLOCK), 1)
    return j <= i, j > i


def _attn_scores(qg, kc, kp, n, cur_ok, prev_ok):
    scale = HEAD_DIM ** -0.5
    s_c = lax.dot_general(qg, kc, (NT, ((), ())), preferred_element_type=F32) * scale
    s_p = lax.dot_general(qg, kp, (NT, ((), ())), preferred_element_type=F32) * scale
    s_c = jnp.where(cur_ok, s_c, -1e30)
    s_p = jnp.where(jnp.logical_and(prev_ok, n > 0), s_p, -1e30)
    return s_c, s_p


def _attn_fwd(name, q, k, v, sinks):
    s = q.shape[0]
    nkv = k.shape[0]
    gw = GQA * HEAD_DIM
    nb = s // ATTN_BLOCK

    def body(sk_ref, q_ref, k_ref, v_ref, o_ref, lse_ref):
        h = pl.program_id(0)
        cur_ok, prev_ok = _attn_masks()

        def blk(n, carry):
            off = pl.multiple_of(n * ATTN_BLOCK, ATTN_BLOCK)
            poff = pl.multiple_of(jnp.maximum(n - 1, 0) * ATTN_BLOCK, ATTN_BLOCK)
            kc, kp = k_ref[pl.ds(off, ATTN_BLOCK), :], k_ref[pl.ds(poff, ATTN_BLOCK), :]
            vc, vp = v_ref[pl.ds(off, ATTN_BLOCK), :], v_ref[pl.ds(poff, ATTN_BLOCK), :]
            for g in range(GQA):
                qg = q_ref[pl.ds(off, ATTN_BLOCK), g * HEAD_DIM:(g + 1) * HEAD_DIM]
                s_c, s_p = _attn_scores(qg, kc, kp, n, cur_ok, prev_ok)
                sink = sk_ref[h, g]
                m = jnp.maximum(jnp.maximum(s_c.max(-1, keepdims=True), s_p.max(-1, keepdims=True)), sink)
                p_c, p_p = jnp.exp(s_c - m), jnp.exp(s_p - m)
                den = p_c.sum(-1, keepdims=True) + p_p.sum(-1, keepdims=True) + jnp.exp(sink - m)
                o = (jnp.dot((p_c / den).astype(_MXU), vc, preferred_element_type=F32)
                     + jnp.dot((p_p / den).astype(_MXU), vp, preferred_element_type=F32))
                o_ref[pl.ds(off, ATTN_BLOCK), g * HEAD_DIM:(g + 1) * HEAD_DIM] = o
                lse_ref[pl.ds(off, ATTN_BLOCK), g:g + 1] = m + jnp.log(den)
            return carry

        lax.fori_loop(0, nb, blk, 0)

    kv_spec = pl.BlockSpec((None, s, HEAD_DIM), lambda h: (h, 0, 0))
    return pl.pallas_call(
        body, grid=(nkv,),
        in_specs=[pl.BlockSpec(memory_space=pltpu.SMEM), pl.BlockSpec((s, gw), lambda h: (0, h)), kv_spec, kv_spec],
        out_specs=[pl.BlockSpec((s, gw), lambda h: (0, h)), pl.BlockSpec((None, s, GQA), lambda h: (h, 0, 0))],
        out_shape=[_sds((s, nkv * gw), F32), _sds((nkv, s, GQA), F32)], compiler_params=_cp(), name=name,
    )(sinks, q, k, v)


def _attn_bwd(name, q, k, v, o, dmix, lse, sinks):
    s = q.shape[0]
    nkv = k.shape[0]
    gw = GQA * HEAD_DIM
    nb = s // ATTN_BLOCK
    scale = HEAD_DIM ** -0.5

    def body(sk_ref, q_ref, k_ref, v_ref, o_ref, do_ref, lse_ref, dq_ref, dk_ref, dv_ref, dsk_ref):
        h = pl.program_id(0)
        cur_ok, prev_ok = _attn_masks()
        dk_ref[...] = jnp.zeros_like(dk_ref)
        dv_ref[...] = jnp.zeros_like(dv_ref)

        def blk(n, acc):
            off = pl.multiple_of(n * ATTN_BLOCK, ATTN_BLOCK)
            poff = pl.multiple_of(jnp.maximum(n - 1, 0) * ATTN_BLOCK, ATTN_BLOCK)
            rows, prows = pl.ds(off, ATTN_BLOCK), pl.ds(poff, ATTN_BLOCK)
            kc, kp, vc, vp = k_ref[rows, :], k_ref[prows, :], v_ref[rows, :], v_ref[prows, :]
            dk_c = jnp.zeros((ATTN_BLOCK, HEAD_DIM), F32)
            dk_p = jnp.zeros((ATTN_BLOCK, HEAD_DIM), F32)
            dv_c = jnp.zeros((ATTN_BLOCK, HEAD_DIM), F32)
            dv_p = jnp.zeros((ATTN_BLOCK, HEAD_DIM), F32)
            out_acc = []
            for g in range(GQA):
                cols = slice(g * HEAD_DIM, (g + 1) * HEAD_DIM)
                qg = q_ref[rows, cols]
                do_g = do_ref[rows, cols]
                delta = jnp.sum(do_g * o_ref[rows, cols], -1, keepdims=True)
                dob = do_g.astype(_MXU)
                lse_g = lse_ref[rows, g:g + 1]
                s_c, s_p = _attn_scores(qg, kc, kp, n, cur_ok, prev_ok)
                p_c, p_p = jnp.exp(s_c - lse_g), jnp.exp(s_p - lse_g)
                dp_c = lax.dot_general(dob, vc, (NT, ((), ())), preferred_element_type=F32)
                dp_p = lax.dot_general(dob, vp, (NT, ((), ())), preferred_element_type=F32)
                ds_c = (p_c * (dp_c - delta) * scale).astype(_MXU)
                ds_p = (p_p * (dp_p - delta) * scale).astype(_MXU)
                dq_ref[rows, cols] = (jnp.dot(ds_c, kc, preferred_element_type=F32)
                                      + jnp.dot(ds_p, kp, preferred_element_type=F32))
                dk_c += lax.dot_general(ds_c, qg, (TN, ((), ())), preferred_element_type=F32)
                dk_p += lax.dot_general(ds_p, qg, (TN, ((), ())), preferred_element_type=F32)
                dv_c += lax.dot_general(p_c.astype(_MXU), dob, (TN, ((), ())), preferred_element_type=F32)
                dv_p += lax.dot_general(p_p.astype(_MXU), dob, (TN, ((), ())), preferred_element_type=F32)
                out_acc.append(acc[g] - jnp.exp(sk_ref[h, g] - lse_g) * delta)
            dk_ref[rows, :] += dk_c
            dv_ref[rows, :] += dv_c
            dk_ref[prows, :] += dk_p
            dv_ref[prows, :] += dv_p
            return tuple(out_acc)

        acc = lax.fori_loop(0, nb, blk, tuple(jnp.zeros((ATTN_BLOCK, 1), F32) for _ in range(GQA)))
        for g in range(GQA):
            dsk_ref[:, g:g + 1] = jnp.sum(acc[g], 0, keepdims=True)

    kv_spec = pl.BlockSpec((None, s, HEAD_DIM), lambda h: (h, 0, 0))
    qcols = pl.BlockSpec((s, gw), lambda h: (0, h))
    return pl.pallas_call(
        body, grid=(nkv,),
        in_specs=[pl.BlockSpec(memory_space=pltpu.SMEM), qcols, kv_spec, kv_spec, qcols, qcols,
                  pl.BlockSpec((None, s, GQA), lambda h: (h, 0, 0))],
        out_specs=[qcols, kv_spec, kv_spec, pl.BlockSpec((None, 1, GQA), lambda h: (h, 0, 0))],
        out_shape=[_sds((s, nkv * gw), F32), _sds((nkv, s, HEAD_DIM), F32), _sds((nkv, s, HEAD_DIM), F32),
                   _sds((nkv, 1, GQA), F32)],
        compiler_params=_cp(), name=name,
    )(sinks, q, k, v, o, dmix, lse)


def _shift_down(a, k, t):
    return jnp.where(t >= k, pltpu.roll(a, k, 0), 0.0)


def _shift_up(a, k, t):
    n = a.shape[0]
    return jnp.where(t < n - k, pltpu.roll(a, n - k, 0), 0.0)


def _pool_fwd(name, h, col_block, pool_w, pool_scale):
    s = h.shape[0]
    ng, pg = pool_w.shape[0], pool_w.shape[1]
    pw_ = ng * pg

    def body(u_ref, w_ref, sc_ref, y_ref, pre_ref):
        t = lax.broadcasted_iota(jnp.int32, (s, pg), 0)
        for gi, win in enumerate(POOL_WINDOWS):
            cols = slice(gi * pg, (gi + 1) * pg)
            u = u_ref[:, cols]
            a = u
            k = 1
            while k < win:
                a = a + _shift_down(a, k, t)
                k *= 2
            div = jnp.minimum(t + 1, win).astype(F32)
            pre = (a / div - u).astype(_MXU)
            pre_ref[:, cols] = pre
            out = jnp.dot(pre, w_ref[gi], preferred_element_type=F32)
            y_ref[:, cols] = (out * sc_ref[:, cols]).astype(y_ref.dtype)

    blk = pl.BlockSpec((s, pw_), lambda i: (0, 0))
    return pl.pallas_call(
        body, grid=(1,),
        in_specs=[pl.BlockSpec((s, pw_), lambda i: (0, col_block)), pl.BlockSpec((ng, pg, pg), lambda i: (0, 0, 0)),
                  pl.BlockSpec((1, pw_), lambda i: (0, 0))],
        out_specs=[blk, blk], out_shape=[_sds((s, pw_), _MXU), _sds((s, pw_), _MXU)], compiler_params=_cp(), name=name,
    )(h, pool_w, pool_scale)


def _pool_bwd(name, dmix, col_block, pre, pool_w, pool_scale):
    s = pre.shape[0]
    ng, pg = pool_w.shape[0], pool_w.shape[1]
    pw_ = ng * pg

    def body(dy_ref, pre_ref, w_ref, sc_ref, du_ref, dw_ref, dsc_ref):
        t = lax.broadcasted_iota(jnp.int32, (s, pg), 0)
        for gi, win in enumerate(POOL_WINDOWS):
            cols = slice(gi * pg, (gi + 1) * pg)
            pre_g = pre_ref[:, cols]
            dy = dy_ref[:, cols]
            out = jnp.dot(pre_g, w_ref[gi], preferred_element_type=F32)
            dsc_ref[:, cols] = jnp.sum(dy * out, 0, keepdims=True)
            dout = (dy * sc_ref[:, cols]).astype(_MXU)
            dw_ref[gi] = lax.dot_general(pre_g, dout, (TN, ((), ())), preferred_element_type=F32)
            dpre = lax.dot_general(dout, w_ref[gi], (NT, ((), ())), preferred_element_type=F32)
            div = jnp.minimum(t + 1, win).astype(F32)
            a = dpre / div
            k = 1
            while k < win:
                a = a + _shift_up(a, k, t)
                k *= 2
            du_ref[:, cols] = (a - dpre).astype(du_ref.dtype)

    blk = pl.BlockSpec((s, pw_), lambda i: (0, 0))
    wspec = pl.BlockSpec((ng, pg, pg), lambda i: (0, 0, 0))
    vec = pl.BlockSpec((1, pw_), lambda i: (0, 0))
    return pl.pallas_call(
        body, grid=(1,), in_specs=[pl.BlockSpec((s, pw_), lambda i: (0, col_block)), blk, wspec, vec],
        out_specs=[blk, wspec, vec], out_shape=[_sds((s, pw_), _MXU), _sds((ng, pg, pg), F32), _sds((1, pw_), F32)],
        compiler_params=_cp(), name=name,
    )(dmix, pre, pool_w, pool_scale)


def _scan_chunks(xr_ref, xi_ref, sr_ref, si_ref, ar, ai, reverse):
    n, c = xr_ref.shape
    tt = min(SCAN_T, n)
    lw = min(SCAN_LANES, c)
    nchunk = n // tt
    t = lax.broadcasted_iota(jnp.int32, (tt, lw), 0)

    for l0 in range(0, c, lw):
        lanes = slice(l0, l0 + lw)
        a_r, a_i = ar[:, lanes], ai[:, lanes]

        def local(vr, vi, a_r=a_r, a_i=a_i):
            pr, pi = a_r, a_i
            k = 1
            while k < tt:
                if reverse:
                    hr, hi = _shift_up(vr, k, t), _shift_up(vi, k, t)
                else:
                    hr, hi = _shift_down(vr, k, t), _shift_down(vi, k, t)
                vr, vi = vr + pr * hr - pi * hi, vi + pr * hi + pi * hr
                pr, pi = pr * pr - pi * pi, 2.0 * pr * pi
                k *= 2
            return vr, vi

        edge = tt - 1 if reverse else 0
        pw_r, pw_i = local(jnp.where(t == edge, a_r, 0.0), jnp.where(t == edge, a_i, 0.0))
        last = 0 if reverse else tt - 1

        def body(i, carry, lanes=lanes, local=local, pw_r=pw_r, pw_i=pw_i):
            cr, ci = carry
            ch = nchunk - 1 - i if reverse else i
            rows = pl.ds(pl.multiple_of(ch * tt, tt), tt)
            vr, vi = local(xr_ref[rows, lanes], xi_ref[rows, lanes])
            vr2 = vr + pw_r * cr - pw_i * ci
            vi2 = vi + pw_r * ci + pw_i * cr
            sr_ref[rows, lanes] = vr2
            si_ref[rows, lanes] = vi2
            return vr2[last:last + 1, :], vi2[last:last + 1, :]

        lax.fori_loop(0, nchunk, body, (jnp.zeros((1, lw), F32), jnp.zeros((1, lw), F32)))


_GELU_K = math.sqrt(2.0 / math.pi)


def _gelu_grad(y):
    inner = _GELU_K * (y + 0.044715 * y * y * y)
    th = jnp.tanh(inner)
    return 0.5 * (1.0 + th) + 0.5 * y * (1.0 - th * th) * _GELU_K * (1.0 + 3.0 * 0.044715 * y * y)


def _ssm_fwd(name, h, u_block0, bdr, bdi, cdr, cdi, dvec, ar, ai):
    s = h.shape[0]
    nt, cw, lw = bdr.shape
    rc = min(256, s)

    def body(u_ref, bdr_ref, bdi_ref, cdr_ref, cdi_ref, d_ref, ar_ref, ai_ref, sr_ref, si_ref, y_ref, yg_ref):
        def mm_in(c, _):
            rows = pl.ds(pl.multiple_of(c * rc, rc), rc)
            ub = u_ref[rows, :].astype(_MXU)
            sr_ref[rows, :] = jnp.dot(ub, bdr_ref[...], preferred_element_type=F32)
            si_ref[rows, :] = jnp.dot(ub, bdi_ref[...], preferred_element_type=F32)
            return 0

        lax.fori_loop(0, s // rc, mm_in, 0)
        _scan_chunks(sr_ref, si_ref, sr_ref, si_ref, ar_ref[...], ai_ref[...], reverse=False)

        def mm_out(c, _):
            rows = pl.ds(pl.multiple_of(c * rc, rc), rc)
            y = (jnp.dot(sr_ref[rows, :].astype(_MXU), cdr_ref[...], preferred_element_type=F32)
                 - jnp.dot(si_ref[rows, :].astype(_MXU), cdi_ref[...], preferred_element_type=F32)
                 + d_ref[...] * u_ref[rows, :])
            y_ref[rows, :] = y
            yg_ref[rows, :] = jax.nn.gelu(y).astype(yg_ref.dtype)
            return 0

        lax.fori_loop(0, s // rc, mm_out, 0)

    st = pl.BlockSpec((s, lw), lambda j: (0, j))
    ch = pl.BlockSpec((s, cw), lambda j: (0, j))
    bspec = pl.BlockSpec((None, cw, lw), lambda j: (j, 0, 0))
    cspec = pl.BlockSpec((None, lw, cw), lambda j: (j, 0, 0))
    return pl.pallas_call(
        body, grid=(nt,),
        in_specs=[pl.BlockSpec((s, cw), lambda j: (0, u_block0 + j)), bspec, bspec, cspec, cspec,
                  pl.BlockSpec((1, cw), lambda j: (0, j)), pl.BlockSpec((1, lw), lambda j: (0, j)),
                  pl.BlockSpec((1, lw), lambda j: (0, j))],
        out_specs=[st, st, ch, ch],
        out_shape=[_sds((s, nt * lw), F32), _sds((s, nt * lw), F32), _sds((s, nt * cw), F32), _sds((s, nt * cw), _MXU)],
        compiler_params=_cp(), name=name,
    )(h, bdr, bdi, cdr, cdi, dvec, ar, ai)


def _ssm_bwd(name, dyg, ypre, h, u_block0, sr, si, bdr, bdi, cdr, cdi, dvec, ar, ai):
    s = h.shape[0]
    nt, cw, lw = bdr.shape
    rc = min(256, s)

    def body(dyg_ref, yp_ref, u_ref, sr_ref, si_ref, bdr_ref, bdi_ref, cdr_ref, cdi_ref, d_ref, ar_ref, ai_ref,
             du_ref, dd_ref, dcr_ref, dci_ref, dbr_ref, dbi_ref, dar_ref, dai_ref, lr_scr, li_scr, dy_scr):
        for ref in (dd_ref, dcr_ref, dci_ref, dbr_ref, dbi_ref, dar_ref, dai_ref):
            ref[...] = jnp.zeros_like(ref)

        def p1(c, _):
            rows = pl.ds(pl.multiple_of(c * rc, rc), rc)
            dy = dyg_ref[rows, :] * _gelu_grad(yp_ref[rows, :])
            dy_scr[rows, :] = dy
            dd_ref[...] += jnp.sum(dy * u_ref[rows, :], 0, keepdims=True)
            dyb = dy.astype(_MXU)
            lr_scr[rows, :] = lax.dot_general(dyb, cdr_ref[...], (NT, ((), ())), preferred_element_type=F32)
            li_scr[rows, :] = -lax.dot_general(dyb, cdi_ref[...], (NT, ((), ())), preferred_element_type=F32)
            dcr_ref[...] += lax.dot_general(sr_ref[rows, :].astype(_MXU), dyb, (TN, ((), ())), preferred_element_type=F32)
            dci_ref[...] -= lax.dot_general(si_ref[rows, :].astype(_MXU), dyb, (TN, ((), ())), preferred_element_type=F32)
            return 0

        lax.fori_loop(0, s // rc, p1, 0)
        _scan_chunks(lr_scr, li_scr, lr_scr, li_scr, ar_ref[...], -ai_ref[...], reverse=True)
        t = lax.broadcasted_iota(jnp.int32, (rc, lw), 0)

        def p2(c, _):
            r0 = pl.multiple_of(c * rc, rc)
            rows = pl.ds(r0, rc)
            before = pl.ds(pl.multiple_of(jnp.maximum(r0 - 8, 0), 8), 8)
            have = (c > 0).astype(F32)
            lr, li = lr_scr[rows, :], li_scr[rows, :]
            spr = jnp.where(t == 0, sr_ref[before, :][7:8, :] * have, pltpu.roll(sr_ref[rows, :], 1, 0))
            spi = jnp.where(t == 0, si_ref[before, :][7:8, :] * have, pltpu.roll(si_ref[rows, :], 1, 0))
            dar_ref[...] += jnp.sum(lr * spr + li * spi, 0, keepdims=True)
            dai_ref[...] += jnp.sum(li * spr - lr * spi, 0, keepdims=True)
            lrb, lib = lr.astype(_MXU), li.astype(_MXU)
            du = (dy_scr[rows, :] * d_ref[...]
                  + lax.dot_general(lrb, bdr_ref[...], (NT, ((), ())), preferred_element_type=F32)
                  + lax.dot_general(lib, bdi_ref[...], (NT, ((), ())), preferred_element_type=F32))
            du_ref[rows, :] = du.astype(du_ref.dtype)
            ub = u_ref[rows, :].astype(_MXU)
            dbr_ref[...] += lax.dot_general(ub, lrb, (TN, ((), ())), preferred_element_type=F32)
            dbi_ref[...] += lax.dot_general(ub, lib, (TN, ((), ())), preferred_element_type=F32)
            return 0

        lax.fori_loop(0, s // rc, p2, 0)

    st = pl.BlockSpec((s, lw), lambda j: (0, j))
    ch = pl.BlockSpec((s, cw), lambda j: (0, j))
    bspec = pl.BlockSpec((None, cw, lw), lambda j: (j, 0, 0))
    cspec = pl.BlockSpec((None, lw, cw), lambda j: (j, 0, 0))
    cvec = pl.BlockSpec((1, cw), lambda j: (0, j))
    svec = pl.BlockSpec((1, lw), lambda j: (0, j))
    return pl.pallas_call(
        body, grid=(nt,),
        in_specs=[ch, ch, pl.BlockSpec((s, cw), lambda j: (0, u_block0 + j)), st, st, bspec, bspec, cspec, cspec, cvec, svec, svec],
        out_specs=[ch, cvec, cspec, cspec, bspec, bspec, svec, svec],
        out_shape=[_sds((s, nt * cw), _MXU), _sds((1, nt * cw), F32), _sds((nt, lw, cw), F32), _sds((nt, lw, cw), F32),
                   _sds((nt, cw, lw), F32), _sds((nt, cw, lw), F32), _sds((1, nt * lw), F32), _sds((1, nt * lw), F32)],
        scratch_shapes=[pltpu.VMEM((s, lw), F32), pltpu.VMEM((s, lw), F32), pltpu.VMEM((s, cw), F32)],
        compiler_params=_cp(), name=name,
    )(dyg, ypre, h, sr, si, bdr, bdi, cdr, cdi, dvec, ar, ai)


def _glu_fwd(name, yg, gw):
    s, w = yg.shape
    tr = min(512, s)

    def body(y_ref, w_ref, o_ref, ab_ref):
        ab = jnp.dot(y_ref[...], w_ref[...], preferred_element_type=F32)
        ab_ref[...] = ab
        o_ref[...] = (ab[:, :w] * jax.nn.sigmoid(ab[:, w:])).astype(o_ref.dtype)

    return pl.pallas_call(
        body, grid=(s // tr,), in_specs=[pl.BlockSpec((tr, w), lambda i: (i, 0)), _resident((w, 2 * w), lambda i: (0, 0))],
        out_specs=[pl.BlockSpec((tr, w), lambda i: (i, 0)), pl.BlockSpec((tr, 2 * w), lambda i: (i, 0))],
        out_shape=[_sds((s, w), _MXU), _sds((s, 2 * w), F32)], compiler_params=_cp(), name=name,
    )(yg, gw)


def _glu_bwd(name, dmix, col_block, ab, gw):
    s = ab.shape[0]
    w = ab.shape[1] // 2
    tr = min(512, s)

    def body(do_ref, ab_ref, w_ref, dab_ref, dy_ref):
        do = do_ref[...]
        a, b = ab_ref[:, :w], ab_ref[:, w:]
        sg = jax.nn.sigmoid(b)
        da = (do * sg).astype(_MXU)
        db = (do * a * sg * (1.0 - sg)).astype(_MXU)
        dab_ref[:, :w] = da
        dab_ref[:, w:] = db
        dy_ref[...] = (lax.dot_general(da, w_ref[:, :w], (NT, ((), ())), preferred_element_type=F32)
                       + lax.dot_general(db, w_ref[:, w:], (NT, ((), ())), preferred_element_type=F32))

    return pl.pallas_call(
        body, grid=(s // tr,),
        in_specs=[pl.BlockSpec((tr, w), lambda i: (i, col_block)), pl.BlockSpec((tr, 2 * w), lambda i: (i, 0)),
                  _resident((w, 2 * w), lambda i: (0, 0))],
        out_specs=[pl.BlockSpec((tr, 2 * w), lambda i: (i, 0)), pl.BlockSpec((tr, w), lambda i: (i, 0))],
        out_shape=[_sds((s, 2 * w), _MXU), _sds((s, w), F32)], compiler_params=_cp(), name=name,
    )(dmix, ab, gw)


def _conv(hu, w_ref, b_ref, t):
    hc = b_ref[...] + _shift_down(hu, 2, t) * w_ref[0:1, :]
    hc = hc + _shift_down(hu, 1, t) * w_ref[1:2, :]
    return hc + hu * w_ref[2:3, :]


def _conv_act_fwd(name, hu, cw, cb):
    s, f2 = hu.shape
    f = f2 // 2
    tw = _col_tile(f, 256)
    nt = f // tw

    def body(v_ref, g_ref, wv_ref, wg_ref, bv_ref, bg_ref, act_ref):
        t = lax.broadcasted_iota(jnp.int32, (s, tw), 0)
        val = _conv(v_ref[...], wv_ref, bv_ref, t)
        gate = _conv(g_ref[...], wg_ref, bg_ref, t)
        act_ref[...] = (jax.nn.silu(gate) * val).astype(act_ref.dtype)

    cv = lambda rows: pl.BlockSpec((rows, tw), lambda i: (0, i))
    cg = lambda rows: pl.BlockSpec((rows, tw), lambda i: (0, nt + i))
    return pl.pallas_call(
        body, grid=(nt,), in_specs=[cv(s), cg(s), cv(CONV_WIDTH), cg(CONV_WIDTH), cv(1), cg(1)],
        out_specs=cv(s), out_shape=_sds((s, f), _MXU), compiler_params=_cp(), name=name,
    )(hu, hu, cw, cw, cb, cb)


def _conv_act_bwd(name, dact, hu, cw, cb):
    s, f2 = hu.shape
    f = f2 // 2
    tw = _col_tile(f, 256)
    nt = f // tw

    def body(da_ref, v_ref, g_ref, wv_ref, wg_ref, bv_ref, bg_ref, dh_ref, dwv_ref, dwg_ref, dbv_ref, dbg_ref):
        t = lax.broadcasted_iota(jnp.int32, (s, tw), 0)
        hv, hg = v_ref[...], g_ref[...]
        val = _conv(hv, wv_ref, bv_ref, t)
        gate = _conv(hg, wg_ref, bg_ref, t)
        sg = jax.nn.sigmoid(gate)
        da = da_ref[...]
        dval = da * (gate * sg)
        dgate = da * val * sg * (1.0 + gate * (1.0 - sg))
        for part, dhc, hu_, w_ref, dw_ref, db_ref in ((0, dval, hv, wv_ref, dwv_ref, dbv_ref),
                                                      (1, dgate, hg, wg_ref, dwg_ref, dbg_ref)):
            db_ref[...] = jnp.sum(dhc, 0, keepdims=True)
            dw_ref[0:1, :] = jnp.sum(dhc * _shift_down(hu_, 2, t), 0, keepdims=True)
            dw_ref[1:2, :] = jnp.sum(dhc * _shift_down(hu_, 1, t), 0, keepdims=True)
            dw_ref[2:3, :] = jnp.sum(dhc * hu_, 0, keepdims=True)
            dh = dhc * w_ref[2:3, :] + _shift_up(dhc, 1, t) * w_ref[1:2, :] + _shift_up(dhc, 2, t) * w_ref[0:1, :]
            dh_ref[part] = dh.astype(dh_ref.dtype)

    cv = lambda rows: pl.BlockSpec((rows, tw), lambda i: (0, i))
    cg = lambda rows: pl.BlockSpec((rows, tw), lambda i: (0, nt + i))
    both = pl.BlockSpec((2, s, tw), lambda i: (0, 0, i))
    dh, dwv, dwg, dbv, dbg = pl.pallas_call(
        body, grid=(nt,), in_specs=[cv(s), cv(s), cg(s), cv(CONV_WIDTH), cg(CONV_WIDTH), cv(1), cg(1)],
        out_specs=[both, cv(CONV_WIDTH), cv(CONV_WIDTH), cv(1), cv(1)],
        out_shape=[_sds((2, s, f), _MXU), _sds((CONV_WIDTH, f), F32), _sds((CONV_WIDTH, f), F32),
                   _sds((1, f), F32), _sds((1, f), F32)],
        compiler_params=_cp(), name=name,
    )(dact, hu, hu, cw, cw, cb, cb)
    return dh, jnp.concatenate([dwv, dwg], 1), jnp.concatenate([dbv, dbg], 1)


def _elem_tiles(r, c):
    tc = _col_tile(c, 1024)
    budget = 256 * 1024
    tr = r
    if r * tc > budget:
        cands = [d for d in range(8, r, 8) if r % d == 0 and d * tc <= budget]
        tr = max(cands) if cands else r
    return tr, tc


def _reduce_adamw(name, parts, w, m, v):
    n, nl, r, c = parts.shape
    tr, tc = _elem_tiles(r, c)
    c1 = 1.0 - ADAM_B1 ** ADAM_STEP
    c2 = 1.0 - ADAM_B2 ** ADAM_STEP

    def body(p_ref, w_ref, m_ref, v_ref, g_ref, d_ref, nm_ref, nv_ref):
        g = p_ref[0].astype(F32)
        for i in range(1, n):
            g = g + p_ref[i].astype(F32)
        nm = ADAM_B1 * m_ref[...] + (1.0 - ADAM_B1) * g
        nv = ADAM_B2 * v_ref[...] + (1.0 - ADAM_B2) * (g * g)
        m_hat = nm / c1
        v_hat = nv / c2
        g_ref[...] = g
        nm_ref[...] = nm
        nv_ref[...] = nv
        d_ref[...] = -ADAM_LR * (m_hat / (jnp.sqrt(v_hat) + ADAM_EPS) + ADAM_WD * w_ref[...])

    blk = pl.BlockSpec((None, tr, tc), lambda l, i, j: (l, i, j))
    out = _sds((nl, r, c), F32)
    return pl.pallas_call(
        body, grid=(nl, r // tr, c // tc),
        in_specs=[pl.BlockSpec((n, None, tr, tc), lambda l, i, j: (0, l, i, j)), blk, blk, blk],
        out_specs=[blk, blk, blk, blk], out_shape=[out, out, out, out], compiler_params=_cp(), name=name,
    )(parts, w, m, v)


def _pair_sum(name, mine, theirs, c_idx):
    _, _, r, c = mine.shape
    tr, tc = _elem_tiles(r, c)

    def body(c_ref, a_ref, b_ref, o_ref):
        o_ref[...] = (a_ref[...].astype(F32) + b_ref[...].astype(F32)).astype(o_ref.dtype)

    return pl.pallas_call(
        body,
        grid_spec=pltpu.PrefetchScalarGridSpec(
            num_scalar_prefetch=1, grid=(4, r // tr, c // tc),
            in_specs=[pl.BlockSpec((None, None, tr, tc), lambda p, i, j, cref: (p, cref[0], i, j)),
                      pl.BlockSpec((None, tr, tc), lambda p, i, j, cref: (p, i, j))],
            out_specs=pl.BlockSpec((None, tr, tc), lambda p, i, j, cref: (p, i, j))),
        out_shape=_sds((4, r, c), _WIRE), compiler_params=_cp(), name=name,
    )(c_idx, mine, theirs)


def _place():
    return lax.axis_index("x"), lax.axis_index("y"), lax.axis_index("c")


def _all_gather(name, xs):
    n = len(xs)

    def body(*refs):
        x_refs, o_refs = refs[:n], refs[n:2 * n]
        send_sems, recv_sems, local_sems = refs[2 * n:]
        x, y, c = _place()
        me, sibling = (x, y, c), (x, y, 1 - c)
        chips = [(1 - x, y), (x, 1 - y), (1 - x, 1 - y)]

        def copy(a, k, block, to, src=None):
            px, py, pc = block
            rows = o_refs[a].at[4 * px + 2 * py + pc]
            return pltpu.make_async_remote_copy(
                src_ref=rows if src is None else src, dst_ref=rows, send_sem=send_sems.at[a, k], recv_sem=recv_sems.at[a, k],
                device_id=to, device_id_type=MESH)

        sent = []
        mine = []
        for a in range(n):
            mx, my, mc = me
            cp = pltpu.make_async_copy(x_refs[a], o_refs[a].at[4 * mx + 2 * my + mc], local_sems.at[a])
            cp.start()
            mine.append(cp)
            first = [copy(a, 0, me, sibling, src=x_refs[a])]
            first += [copy(a, 1 + j, me, (*chip, c), src=x_refs[a]) for j, chip in enumerate(chips)]
            for cp in first:
                cp.start()
            sent += first
        for a in range(n):
            for j, chip in enumerate(chips):
                copy(a, 1 + j, (*chip, c), me).wait_recv()
                fwd = copy(a, 4 + j, (*chip, c), sibling)
                fwd.start()
                sent.append(fwd)
        for a in range(n):
            copy(a, 0, sibling, me).wait_recv()
            for j, chip in enumerate(chips):
                copy(a, 4 + j, (*chip, 1 - c), me).wait_recv()
        for cp in sent:
            cp.wait_send()
        for cp in mine:
            cp.wait()

    return pl.pallas_call(
        body, in_specs=[ANY] * n, out_specs=[ANY] * n,
        out_shape=[_sds((N_DEV,) + a.shape, a.dtype) for a in xs],
        scratch_shapes=[pltpu.SemaphoreType.DMA((n, 7)), pltpu.SemaphoreType.DMA((n, 7)), pltpu.SemaphoreType.DMA((n,))],
        name=name,
    )(*xs)


def _swap_sibling(name, xs):
    n = len(xs)

    def body(*refs):
        x_refs, o_refs = refs[:n], refs[n:2 * n]
        send_sems, recv_sems = refs[2 * n:]
        x, y, c = _place()
        sibling = (x, y, 1 - c)
        cps = []
        for a in range(n):
            for p in range(4):
                cp = pltpu.make_async_remote_copy(
                    src_ref=x_refs[a].at[p, 1 - c], dst_ref=o_refs[a].at[p], send_sem=send_sems.at[a, p],
                    recv_sem=recv_sems.at[a, p], device_id=sibling, device_id_type=MESH)
                cp.start()
                cps.append(cp)
        for cp in cps:
            cp.wait_recv()
        for cp in cps:
            cp.wait_send()

    return pl.pallas_call(
        body, in_specs=[ANY] * n, out_specs=[ANY] * n,
        out_shape=[_sds((4,) + a.shape[2:], a.dtype) for a in xs],
        scratch_shapes=[pltpu.SemaphoreType.DMA((n, 4)), pltpu.SemaphoreType.DMA((n, 4))],
        name=name,
    )(*xs)


HBM = pl.BlockSpec(memory_space=pltpu.HBM)
SEM = pl.BlockSpec(memory_space=pltpu.SEMAPHORE)
DATAFLOW = pltpu.SideEffectType.DATAFLOW_SIDE_EFFECTING


def _in_hbm(a):
    return pltpu.with_memory_space_constraint(a, pltpu.HBM)


def _split_copy_start(name, srcs, lands, copies, deps):
    n, nd = len(srcs), len(deps)
    per = len(copies([None] * n, [None] * n, probe=True)) // n

    def body(*refs):
        s_refs, l_refs = refs[:n], refs[n:2 * n]
        send_sems, recv_sems = refs[2 * n + nd], refs[2 * n + nd + 1]
        token = refs[-1]
        for a, k, src, dst, to in copies(s_refs, l_refs):
            pltpu.make_async_remote_copy(src_ref=src, dst_ref=dst, send_sem=send_sems.at[a * per + k],
                                         recv_sem=recv_sems.at[a * per + k], device_id=to, device_id_type=MESH).start()
        token[...] = jnp.zeros_like(token)

    both = list(srcs) + list(lands)
    outs = pl.pallas_call(
        body, name=name,
        out_shape=(pltpu.SemaphoreType.DMA((n * per,)), pltpu.SemaphoreType.DMA((n * per,)),
                   *[pltpu.HBM(a.shape, a.dtype) for a in both], _sds((8, V7X_LANES), F32)),
        in_specs=[HBM] * (2 * n) + [ANY] * nd,
        out_specs=(SEM, SEM, *[HBM] * (2 * n), pl.BlockSpec(memory_space=pltpu.VMEM)),
        input_output_aliases={i: 2 + i for i in range(2 * n)},
        compiler_params=pltpu.CompilerParams(has_side_effects=DATAFLOW),
    )(*[_in_hbm(a) for a in both], *deps)
    return outs[0], outs[1], list(outs[2:2 + n]), list(outs[2 + n:2 + 2 * n]), outs[-1]


def _split_copy_wait(name, send_sems, recv_sems, srcs, lands, arrivals, after):
    n = len(srcs)
    per = len(arrivals([None] * n, [None] * n, probe=True)) // n

    def body(*refs):
        s_refs, l_refs = refs[:n], refs[n:2 * n]
        send_sems_, recv_sems_ = refs[2 * n], refs[2 * n + 1]
        for a, k, src, dst, frm in arrivals(s_refs, l_refs):
            cp = pltpu.make_async_remote_copy(src_ref=src, dst_ref=dst, send_sem=send_sems_.at[a * per + k],
                                              recv_sem=recv_sems_.at[a * per + k], device_id=frm, device_id_type=MESH)
            cp.wait_send()
            cp.wait_recv()

    both = list(srcs) + list(lands)
    outs = pl.pallas_call(
        body, name=name, out_shape=tuple(pltpu.HBM(a.shape, a.dtype) for a in both),
        in_specs=[HBM] * (2 * n) + [SEM, SEM, ANY], out_specs=tuple([HBM] * (2 * n)),
        input_output_aliases={i: i for i in range(2 * n)},
        compiler_params=pltpu.CompilerParams(has_side_effects=DATAFLOW),
    )(*both, send_sems, recv_sems, after)
    return list(outs[:n]), list(outs[n:])


def _gather_copies(arriving):
    def copies(s_refs, l_refs, probe=False):
        if probe:
            return [None] * (4 * len(s_refs))
        x, y, c = _place()
        out = []
        for a in range(len(s_refs)):
            for k, (px, py, pc) in enumerate([(x, y, 1 - c), (1 - x, y, c), (x, 1 - y, c), (1 - x, 1 - y, c)]):
                slot = 4 * px + 2 * py + pc if arriving else 4 * x + 2 * y + c
                out.append((a, k, s_refs[a], l_refs[a].at[slot], (px, py, pc)))
        return out
    return copies


def _chip_copies(arriving):
    def copies(s_refs, l_refs, probe=False):
        if probe:
            return [None] * (3 * len(s_refs))
        x, y, c = _place()
        out = []
        for a in range(len(s_refs)):
            for j, (px, py) in enumerate([(1 - x, y), (x, 1 - y), (1 - x, 1 - y)]):
                src = s_refs[a].at[2 * x + y] if arriving else s_refs[a].at[2 * px + py]
                out.append((a, j, src, l_refs[a].at[j], (px, py, c)))
        return out
    return copies


def _gather_begin(name, shards, deps):
    x, y, c = _place()
    lands = [lax.dynamic_update_slice_in_dim(lax.empty((N_DEV,) + a.shape, a.dtype), a[None], 4 * x + 2 * y + c, 0)
             for a in shards]
    return _split_copy_start(name + "_start", shards, lands, _gather_copies(False), deps)


def _gather_end(name, handle, after):
    send_sems, recv_sems, srcs, lands, _ = handle
    _, lands = _split_copy_wait(name + "_wait", send_sems, recv_sems, srcs, lands, _gather_copies(True), after)
    return _gather_forward(name + "_forward", lands)


def _gather_forward(name, lands):
    n = len(lands)

    def body(*refs):
        o_refs = refs[n:2 * n]
        send_sems, recv_sems = refs[2 * n:]
        x, y, c = _place()
        sibling = (x, y, 1 - c)
        chips = [(1 - x, y), (x, 1 - y), (1 - x, 1 - y)]
        sent = []
        for a in range(n):
            for j, (px, py) in enumerate(chips):
                rows = o_refs[a].at[4 * px + 2 * py + c]
                cp = pltpu.make_async_remote_copy(src_ref=rows, dst_ref=rows, send_sem=send_sems.at[a, j],
                                                  recv_sem=recv_sems.at[a, j], device_id=sibling, device_id_type=MESH)
                cp.start()
                sent.append(cp)
        for a in range(n):
            for j, (px, py) in enumerate(chips):
                rows = o_refs[a].at[4 * px + 2 * py + 1 - c]
                pltpu.make_async_remote_copy(src_ref=rows, dst_ref=rows, send_sem=send_sems.at[a, j],
                                             recv_sem=recv_sems.at[a, j], device_id=sibling, device_id_type=MESH).wait_recv()
        for cp in sent:
            cp.wait_send()

    return pl.pallas_call(
        body, in_specs=[ANY] * n, out_specs=[ANY] * n, out_shape=[_sds(a.shape, a.dtype) for a in lands],
        input_output_aliases={i: i for i in range(n)},
        scratch_shapes=[pltpu.SemaphoreType.DMA((n, 3)), pltpu.SemaphoreType.DMA((n, 3))], name=name,
    )(*lands)


def _chips_begin(name, pairs, deps):
    lands = [lax.empty((3,) + a.shape[1:], a.dtype) for a in pairs]
    return _split_copy_start(name + "_start", pairs, lands, _chip_copies(False), deps)


def _chips_end(name, handle, after):
    send_sems, recv_sems, srcs, lands, _ = handle
    return _split_copy_wait(name + "_wait", send_sems, recv_sems, srcs, lands, _chip_copies(True), after)


def _adamw_layer(name, l, own, lands, w, m, v, prev):
    nl, ng, r, c = w.shape
    tr, tc = _elem_tiles(r, c)
    c1 = 1.0 - ADAM_B1 ** ADAM_STEP
    c2 = 1.0 - ADAM_B2 ** ADAM_STEP

    def body(own_ref, lands_ref, w_ref, m_ref, v_ref, *rest):
        g_ref, d_ref, nm_ref, nv_ref = rest[-4:]
        g = own_ref[...].astype(F32) + lands_ref[0].astype(F32) + lands_ref[1].astype(F32) + lands_ref[2].astype(F32)
        nm = ADAM_B1 * m_ref[...] + (1.0 - ADAM_B1) * g
        nv = ADAM_B2 * v_ref[...] + (1.0 - ADAM_B2) * (g * g)
        m_hat = nm / c1
        v_hat = nv / c2
        g_ref[...] = g
        nm_ref[...] = nm
        nv_ref[...] = nv
        d_ref[...] = -ADAM_LR * (m_hat / (jnp.sqrt(v_hat) + ADAM_EPS) + ADAM_WD * w_ref[...])

    lay = pl.BlockSpec((None, None, tr, tc), lambda g, i, j: (l, g, i, j))
    out = _sds((nl, ng, r, c), F32)
    prev = [] if prev is None else list(prev)
    return pl.pallas_call(
        body, grid=(ng, r // tr, c // tc),
        in_specs=[pl.BlockSpec((None, tr, tc), lambda g, i, j: (g, i, j)),
                  pl.BlockSpec((3, None, tr, tc), lambda g, i, j: (0, g, i, j)), lay, lay, lay] + [ANY] * len(prev),
        out_specs=[lay] * 4, out_shape=[out] * 4, input_output_aliases={5 + i: i for i in range(len(prev))},
        compiler_params=_cp(), name=name,
    )(own, lands, w, m, v, *prev)


def _pad_pairs(a, axis, half, half_pad):
    shp = a.shape
    a = a.reshape(shp[:axis] + (2, half) + shp[axis + 1:])
    pad = [(0, 0)] * a.ndim
    pad[axis + 1] = (0, half_pad - half)
    a = jnp.pad(a, pad)
    return a.reshape(shp[:axis] + (2 * half_pad,) + shp[axis + 1:])


def _unpad_pairs(a, axis, half, half_pad):
    shp = a.shape
    a = a.reshape(shp[:axis] + (2, half_pad) + shp[axis + 1:])
    a = lax.slice_in_dim(a, 0, half, axis=axis + 1)
    return a.reshape(shp[:axis] + (2 * half,) + shp[axis + 1:])


def _blockdiag(w, nt):
    g, a, b = w.shape
    gl = g // nt
    e = jnp.eye(gl, dtype=w.dtype).reshape(1, gl, 1, gl, 1)
    return (w.reshape(nt, gl, a, 1, b) * e).reshape(nt, gl * a, gl * b)


def _diagblocks(m, g, a, b):
    nt = m.shape[0]
    gl = g // nt
    d = jnp.diagonal(m.reshape(nt, gl, a, gl, b), axis1=1, axis2=3)
    return jnp.moveaxis(d, -1, 1).reshape(g, a, b)


def _ssm_discretise(lam_re, lam_im, log_dt, b_re, b_im):
    dt = jnp.exp(log_dt)[:, None]
    mag = jnp.exp(lam_re * dt)
    ab_re, ab_im = mag * jnp.cos(lam_im * dt), mag * jnp.sin(lam_im * dt)
    nr, ni = ab_re - 1.0, ab_im
    den = lam_re * lam_re + lam_im * lam_im
    zr = (nr * lam_re + ni * lam_im) / den
    zi = (ni * lam_re - nr * lam_im) / den
    bbr = zr[..., None] * b_re - zi[..., None] * b_im
    bbi = zr[..., None] * b_im + zi[..., None] * b_re
    return ab_re, ab_im, bbr, bbi


def _rope_tables(s):
    half = HEAD_DIM // 2
    inv = ROPE_THETA ** (-jnp.arange(half, dtype=F32) / half)
    ang = jnp.arange(s).astype(F32)[:, None] * inv[None, :]
    cos, sin = jnp.cos(ang), jnp.sin(ang)
    reps = V7X_LANES // HEAD_DIM
    return jnp.tile(jnp.concatenate([cos, cos], -1), (1, reps)), jnp.tile(jnp.concatenate([-sin, sin], -1), (1, reps))


_SMALL = ("attn_sinks", "pool_w", "pool_scale", "ssm_lam_re", "ssm_lam_im", "ssm_log_dt", "ssm_b_re", "ssm_b_im",
          "ssm_c_re", "ssm_c_im", "ssm_d", "ln1_g", "ln1_b", "ffn_conv_b", "ln2_g", "ln2_b")
_BIG = ("w_in", "ssm_glu_w", "w_out", "ffn_w_up", "ffn_w_down")
_ORDER = ("w_in", "attn_sinks", "pool_w", "pool_scale", "ssm_lam_re", "ssm_lam_im", "ssm_log_dt", "ssm_b_re", "ssm_b_im",
          "ssm_c_re", "ssm_c_im", "ssm_d", "ssm_glu_w", "w_out", "ln1_g", "ln1_b", "ffn_w_up", "ffn_conv_w", "ffn_conv_b",
          "ffn_w_down", "ln2_g", "ln2_b")


def kernel(x, w_in, attn_sinks, pool_w, pool_scale, ssm_lam_re, ssm_lam_im, ssm_log_dt, ssm_b_re, ssm_b_im, ssm_c_re, ssm_c_im, ssm_d, ssm_glu_w, w_out, ln1_g, ln1_b, ffn_w_up, ffn_conv_w, ffn_conv_b, ffn_w_down, ln2_g, ln2_b, loss_target, m_w_in, m_attn_sinks, m_pool_w, m_pool_scale, m_ssm_lam_re, m_ssm_lam_im, m_ssm_log_dt, m_ssm_b_re, m_ssm_b_im, m_ssm_c_re, m_ssm_c_im, m_ssm_d, m_ssm_glu_w, m_w_out, m_ln1_g, m_ln1_b, m_ffn_w_up, m_ffn_conv_w, m_ffn_conv_b, m_ffn_w_down, m_ln2_g, m_ln2_b, v_w_in, v_attn_sinks, v_pool_w, v_pool_scale, v_ssm_lam_re, v_ssm_lam_im, v_ssm_log_dt, v_ssm_b_re, v_ssm_b_im, v_ssm_c_re, v_ssm_c_im, v_ssm_d, v_ssm_glu_w, v_w_out, v_ln1_g, v_ln1_b, v_ffn_w_up, v_ffn_conv_w, v_ffn_conv_b, v_ffn_w_down, v_ln2_g, v_ln2_b):
    W = dict(w_in=w_in, attn_sinks=attn_sinks, pool_w=pool_w, pool_scale=pool_scale, ssm_lam_re=ssm_lam_re, ssm_lam_im=ssm_lam_im, ssm_log_dt=ssm_log_dt, ssm_b_re=ssm_b_re, ssm_b_im=ssm_b_im, ssm_c_re=ssm_c_re, ssm_c_im=ssm_c_im, ssm_d=ssm_d, ssm_glu_w=ssm_glu_w, w_out=w_out, ln1_g=ln1_g, ln1_b=ln1_b, ffn_w_up=ffn_w_up, ffn_conv_w=ffn_conv_w, ffn_conv_b=ffn_conv_b, ffn_w_down=ffn_w_down, ln2_g=ln2_g, ln2_b=ln2_b)
    M = dict(w_in=m_w_in, attn_sinks=m_attn_sinks, pool_w=m_pool_w, pool_scale=m_pool_scale, ssm_lam_re=m_ssm_lam_re, ssm_lam_im=m_ssm_lam_im, ssm_log_dt=m_ssm_log_dt, ssm_b_re=m_ssm_b_re, ssm_b_im=m_ssm_b_im, ssm_c_re=m_ssm_c_re, ssm_c_im=m_ssm_c_im, ssm_d=m_ssm_d, ssm_glu_w=m_ssm_glu_w, w_out=m_w_out, ln1_g=m_ln1_g, ln1_b=m_ln1_b, ffn_w_up=m_ffn_w_up, ffn_conv_w=m_ffn_conv_w, ffn_conv_b=m_ffn_conv_b, ffn_w_down=m_ffn_w_down, ln2_g=m_ln2_g, ln2_b=m_ln2_b)
    V = dict(w_in=v_w_in, attn_sinks=v_attn_sinks, pool_w=v_pool_w, pool_scale=v_pool_scale, ssm_lam_re=v_ssm_lam_re, ssm_lam_im=v_ssm_lam_im, ssm_log_dt=v_ssm_log_dt, ssm_b_re=v_ssm_b_re, ssm_b_im=v_ssm_b_im, ssm_c_re=v_ssm_c_re, ssm_c_im=v_ssm_c_im, ssm_d=v_ssm_d, ssm_glu_w=v_ssm_glu_w, w_out=v_w_out, ln1_g=v_ln1_g, ln1_b=v_ln1_b, ffn_w_up=v_ffn_w_up, ffn_conv_w=v_ffn_conv_w, ffn_conv_b=v_ffn_conv_b, ffn_w_down=v_ffn_w_down, ln2_g=v_ln2_g, ln2_b=v_ln2_b)

    depth = w_in.shape[0]
    s, d = x.shape[1], x.shape[2]
    alpha = (2 * depth) ** 0.25
    attn_w = d // 2
    kv_w = attn_w // GQA
    nkv = kv_w // HEAD_DIM
    pool_wd = d // 4
    ssm_wd = d // 4
    n_groups = ssm_wd // SSM_GROUP
    state_w = n_groups * SSM_STATE
    nt_ssm = max(1, state_w // 512)
    o_k, o_v, o_p, o_s = attn_w, attn_w + kv_w, attn_w + 2 * kv_w, attn_w + 2 * kv_w + pool_wd
    in_w = o_s + ssm_wd
    half = ffn_w_down.shape[1]
    half_pad = -(-half // 64) * 64
    ffp = 4 * 2 * half_pad
    xi, yi, ci = _place()
    me = 4 * xi + 2 * yi + ci
    c_idx = jnp.reshape(ci, (1,)).astype(jnp.int32)

    cos_t, sin_t = _rope_tables(s)

    def layer_shards(l):
        return [
            jnp.transpose(w_in[l]).astype(_WIRE), ssm_glu_w[l].astype(_WIRE), w_out[l].astype(_WIRE),
            _pad_pairs(jnp.transpose(ffn_w_up[l]).astype(_WIRE), 0, half, half_pad),
            jnp.pad(ffn_w_down[l].astype(_WIRE), ((0, half_pad - half), (0, 0))),
        ]

    (g_cw,) = _all_gather("gather_conv_w", [_pad_pairs(ffn_conv_w, 2, half, half_pad)])

    def full_weights(l, gathered):
        g_in, g_glu, g_out, g_up, g_down = gathered
        return dict(
            win_t=g_in.reshape(in_w, d),
            glu=jnp.transpose(g_glu, (1, 0, 2)).reshape(ssm_wd, 2 * ssm_wd),
            wout=g_out.reshape(d, d),
            wup_t=g_up.reshape(2 * ffp, d),
            wdown=g_down.reshape(ffp, d),
            cw=jnp.transpose(g_cw[:, l], (1, 0, 2)).reshape(CONV_WIDTH, 2 * ffp),
            cb=_pad_pairs(ffn_conv_b[l].reshape(N_DEV, 2 * half), 1, half, half_pad).reshape(1, 2 * ffp),
        )

    shards = [layer_shards(l) for l in range(depth)]
    full = [None] * depth
    handle = _gather_begin("gather_weights_0", shards[0], [g_cw])
    full[0] = full_weights(0, _gather_end("gather_weights_0", handle, g_cw))

    def ssm_params(l):
        return (ssm_lam_re[l], ssm_lam_im[l], ssm_log_dt[l], ssm_b_re[l], ssm_b_im[l])

    saved = []
    xf = x[0]
    xb = xf.astype(_MXU)
    for l in range(depth):
        fw = full[l]
        deps = ()
        if l + 1 < depth:
            handle = _gather_begin(f"gather_weights_{l + 1}", shards[l + 1], [fw["wup_t"]])
            deps = (handle[-1],)
        h = _mm_nt(f"in_proj_{l}", xb, fw["win_t"], deps=deps)
        q_rot, k_rot = _rope(f"rope_{l}", h, o_v, cos_t, sin_t, _MXU, ((0, o_k), (o_k, o_v)))
        k_hm = jnp.transpose(k_rot.reshape(s, nkv, HEAD_DIM), (1, 0, 2))
        v_hm = jnp.transpose(h[:, o_v:o_p].astype(_MXU).reshape(s, nkv, HEAD_DIM), (1, 0, 2))
        sinks = attn_sinks[l].reshape(nkv, GQA)
        o_attn, lse = _attn_fwd(f"attn_{l}", q_rot, k_hm, v_hm, sinks)
        pw_b = pool_w[l].astype(_MXU)
        psc = pool_scale[l].reshape(1, pool_wd)
        y_pool, pre = _pool_fwd(f"pool_{l}", h, o_p // pool_wd, pw_b, psc)
        ab_re, ab_im, bbr, bbi = _ssm_discretise(*ssm_params(l))
        bdr = _blockdiag(jnp.transpose(bbr, (0, 2, 1)), nt_ssm).astype(_MXU)
        bdi = _blockdiag(jnp.transpose(bbi, (0, 2, 1)), nt_ssm).astype(_MXU)
        cdr = _blockdiag(jnp.transpose(ssm_c_re[l], (0, 2, 1)), nt_ssm).astype(_MXU)
        cdi = _blockdiag(jnp.transpose(ssm_c_im[l], (0, 2, 1)), nt_ssm).astype(_MXU)
        dvec = ssm_d[l].reshape(1, ssm_wd)
        ar, ai = ab_re.reshape(1, state_w), ab_im.reshape(1, state_w)
        cw_ssm = ssm_wd // nt_ssm
        sr, si, ypre, yg = _ssm_fwd(f"ssm_{l}", h, o_s // cw_ssm, bdr, bdi, cdr, cdi, dvec, ar, ai)
        y_ssm, ab2 = _glu_fwd(f"glu_{l}", yg, fw["glu"])
        mix = jnp.concatenate([o_attn.astype(_MXU), y_pool, y_ssm], -1)
        a1 = _mm_nn(f"out_proj_{l}", mix, fw["wout"])
        g1, b1 = ln1_g[l].reshape(1, d), ln1_b[l].reshape(1, d)
        x1, x1b, xh1, rs1 = _ln_fwd(f"ln1_{l}", xf, a1, g1, b1, alpha)
        hu = _mm_nt(f"ffn_up_{l}", x1b, fw["wup_t"], cap=2 * half_pad)
        act = _conv_act_fwd(f"ffn_act_{l}", hu, fw["cw"], fw["cb"])
        tkd = 2 * half_pad
        tnd = _col_tile(d, 1024)
        f_out = _mm(f"ffn_down_{l}", act, fw["wdown"], NN, (d // tnd, ffp // tkd),
                    pl.BlockSpec((s, tkd), lambda j, kk: (0, kk)), pl.BlockSpec((tkd, tnd), lambda j, kk: (kk, j)),
                    pl.BlockSpec((s, tnd), lambda j, kk: (0, j)), (s, d), F32)
        g2, b2 = ln2_g[l].reshape(1, d), ln2_b[l].reshape(1, d)
        x2, x2b, xh2, rs2 = _ln_fwd(f"ln2_{l}", x1, f_out, g2, b2, alpha)
        saved.append(dict(xb=xb, h=h, q_rot=q_rot, k_hm=k_hm, v_hm=v_hm, sinks=sinks, o_attn=o_attn, lse=lse, pw_b=pw_b, psc=psc,
                          pre=pre, bdr=bdr, bdi=bdi, cdr=cdr, cdi=cdi, dvec=dvec, ar=ar, ai=ai, sr=sr, si=si, ypre=ypre, yg=yg,
                          ab2=ab2, mix=mix, g1=g1, xh1=xh1, rs1=rs1, x1b=x1b, hu=hu, act=act, g2=g2, xh2=xh2, rs2=rs2))
        xf, xb = x2, x2b
        if l + 1 < depth:
            full[l + 1] = full_weights(l + 1, _gather_end(f"gather_weights_{l + 1}", handle, x2b))

    dy, loss_part = _loss_head("loss_head", xf, loss_target[0])
    loss = lax.psum(loss_part[0, 0], ("x", "y", "c"))

    small_g = {k: [None] * depth for k in _SMALL}
    cw_g = [None] * depth
    outs = {}
    big_res = {k: None for k in _BIG}
    my_chip = 2 * xi + yi
    in_flight = None

    transposed = ("w_in", "ffn_w_up")

    def row_groups(name_, t):
        g = 2 if name_ == "ffn_w_up" else 1
        return t.reshape(t.shape[:-2] + (g, t.shape[-2] // g, t.shape[-1]))

    def as_groups(name_, t):
        return row_groups(name_, jnp.transpose(t, (0, 2, 1)) if name_ in transposed else t)

    def from_groups(name_, t):
        t = t.reshape(t.shape[0], t.shape[1] * t.shape[2], t.shape[3])
        return jnp.transpose(t, (0, 2, 1)) if name_ in transposed else t

    grouped = {name_: tuple(as_groups(name_, t[name_]) for t in (W, M, V)) for name_ in _BIG}

    def finish_exchange(after):
        lay, handle = in_flight
        pairs, lands = _chips_end(f"grads_between_chips_{lay}", handle, after)
        for name_, p, ld in zip(_BIG, pairs, lands):
            own = row_groups(name_, lax.dynamic_index_in_dim(p, my_chip, 0, keepdims=False))
            big_res[name_] = _adamw_layer(f"adamw_{name_}_{lay}", lay, own, row_groups(name_, ld), *grouped[name_],
                                          big_res[name_])

    for l in reversed(range(depth)):
        fw, sv = full[l], saved[l]
        deps = () if in_flight is None else (in_flight[1][-1],)
        dr2, dr2b, dg2, db2 = _ln_bwd(f"ln2_bwd_{l}", dy, sv["xh2"], sv["rs2"], sv["g2"], deps=deps)
        d_wdown = _mm_tn_acols(f"ffn_down_dw_{l}", sv["act"], dr2b, _WIRE, cap=2 * half_pad)
        dact = _mm_nt(f"ffn_down_dx_{l}", dr2b, fw["wdown"], cap=2 * half_pad)
        dhu, dcw, dcb = _conv_act_bwd(f"ffn_act_bwd_{l}", dact, sv["hu"], fw["cw"], fw["cb"])
        d_wup = _mm(f"ffn_up_dw_{l}", dhu, sv["x1b"], TN, (N_DEV, 1),
                    pl.BlockSpec((None, s, 2 * half_pad), lambda j, kk: (j // 4, 0, j % 4)),
                    _resident((s, d), lambda j, kk: (0, 0)),
                    pl.BlockSpec((2 * half_pad, d), lambda j, kk: (j, 0)), (2 * ffp, d), _WIRE)
        tnd = _col_tile(d, 512)
        dy1 = _mm(f"ffn_up_dx_{l}", dhu, fw["wup_t"], NN, (d // tnd, N_DEV),
                  pl.BlockSpec((None, s, 2 * half_pad), lambda j, kk: (kk // 4, 0, kk % 4)),
                  pl.BlockSpec((2 * half_pad, tnd), lambda j, kk: (kk, j)),
                  pl.BlockSpec((s, tnd), lambda j, kk: (0, j)), (s, d), F32,
                  add=dr2, add_spec=pl.BlockSpec((s, tnd), lambda j, kk: (0, j)), add_scale=alpha)
        dr1, dr1b, dg1, db1 = _ln_bwd(f"ln1_bwd_{l}", dy1, sv["xh1"], sv["rs1"], sv["g1"])
        d_wout = _mm_tn_acols(f"out_proj_dw_{l}", sv["mix"], dr1b, _WIRE, cap=d // N_DEV)
        dmix = _mm_nt(f"out_proj_dx_{l}", dr1b, fw["wout"])
        dq_rot, dk_hm, dv_hm, dsk = _attn_bwd(f"attn_bwd_{l}", sv["q_rot"], sv["k_hm"], sv["v_hm"], sv["o_attn"], dmix,
                                             sv["lse"], sv["sinks"])
        dqk = jnp.concatenate([dq_rot, jnp.transpose(dk_hm, (1, 0, 2)).reshape(s, kv_w)], -1)
        dhq, dhk = _rope(f"rope_bwd_{l}", dqk, o_v, cos_t, -sin_t, _MXU, ((0, o_k), (o_k, o_v)))
        dhv = jnp.transpose(dv_hm, (1, 0, 2)).reshape(s, kv_w).astype(_MXU)
        dhp, dpw, dpsc = _pool_bwd(f"pool_bwd_{l}", dmix, attn_w // pool_wd, sv["pre"], sv["pw_b"], sv["psc"])
        dab2, dyg = _glu_bwd(f"glu_bwd_{l}", dmix, (attn_w + pool_wd) // ssm_wd, sv["ab2"], fw["glu"])
        d_glu = _mm_tn_bcols(f"glu_dw_{l}", sv["yg"], dab2, _WIRE)
        cw_ssm = ssm_wd // nt_ssm
        dhs, dd, dcdr, dcdi, dbdr, dbdi, dar, dai = _ssm_bwd(
            f"ssm_bwd_{l}", dyg, sv["ypre"], sv["h"], o_s // cw_ssm, sv["sr"], sv["si"], sv["bdr"], sv["bdi"], sv["cdr"],
            sv["cdi"], sv["dvec"], sv["ar"], sv["ai"])
        dh = jnp.concatenate([dhq, dhk, dhv, dhp, dhs], -1)
        d_win = _mm_tn_acols(f"in_proj_dw_{l}", dh, sv["xb"], _WIRE)
        dy = _mm_nn(f"in_proj_dx_{l}", dh, fw["win_t"], add=dr1, add_scale=alpha)

        dbbr = jnp.transpose(_diagblocks(dbdr, n_groups, SSM_GROUP, SSM_STATE), (0, 2, 1))
        dbbi = jnp.transpose(_diagblocks(dbdi, n_groups, SSM_GROUP, SSM_STATE), (0, 2, 1))
        _, vjp = jax.vjp(_ssm_discretise, *ssm_params(l))
        dlr, dli, dldt, dbr, dbi = vjp((dar.reshape(n_groups, SSM_STATE), dai.reshape(n_groups, SSM_STATE), dbbr, dbbi))
        small_g["attn_sinks"][l] = dsk.reshape(-1)
        small_g["pool_w"][l] = dpw
        small_g["pool_scale"][l] = dpsc.reshape(-1)
        small_g["ssm_lam_re"][l], small_g["ssm_lam_im"][l], small_g["ssm_log_dt"][l] = dlr, dli, dldt
        small_g["ssm_b_re"][l], small_g["ssm_b_im"][l] = dbr, dbi
        small_g["ssm_c_re"][l] = jnp.transpose(_diagblocks(dcdr, n_groups, SSM_STATE, SSM_GROUP), (0, 2, 1))
        small_g["ssm_c_im"][l] = jnp.transpose(_diagblocks(dcdi, n_groups, SSM_STATE, SSM_GROUP), (0, 2, 1))
        small_g["ssm_d"][l] = dd.reshape(n_groups, SSM_GROUP)
        small_g["ln1_g"][l], small_g["ln1_b"][l] = dg1.reshape(-1), db1.reshape(-1)
        small_g["ln2_g"][l], small_g["ln2_b"][l] = dg2.reshape(-1), db2.reshape(-1)
        small_g["ffn_conv_b"][l] = _unpad_pairs(dcb.reshape(N_DEV, 2 * half_pad), 1, half, half_pad).reshape(-1)
        cw_g[l] = _unpad_pairs(dcw.reshape(CONV_WIDTH, N_DEV, 2 * half_pad), 2, half, half_pad)

        stacked = [
            d_win.reshape(N_DEV, in_w // N_DEV, d),
            jnp.transpose(d_glu.reshape(ssm_wd, N_DEV, 2 * ssm_wd // N_DEV), (1, 0, 2)),
            d_wout.reshape(N_DEV, d // N_DEV, d),
            d_wup.reshape(N_DEV, 2 * half_pad, d),
            d_wdown.reshape(N_DEV, half_pad, d),
        ]
        by_owner = [a.reshape((4, 2) + a.shape[1:]) for a in stacked]
        theirs = _swap_sibling(f"grads_to_sibling_{l}", by_owner)
        pair = [_pair_sum(f"pair_sum_{l}_{i}", a, b, c_idx) for i, (a, b) in enumerate(zip(by_owner, theirs))]
        if in_flight is not None:
            finish_exchange(dy)
        handle = _chips_begin(f"grads_between_chips_{l}", pair, [])
        in_flight = (l, handle)

    flat_parts = [jnp.stack(small_g[k]).reshape(-1) for k in _SMALL] + [jnp.stack(cw_g).reshape(-1)]
    sizes = [a.shape[0] for a in flat_parts]
    total = sum(sizes)
    rows = -(-total // (512 * V7X_LANES)) * 512
    flat = jnp.pad(jnp.concatenate(flat_parts), (0, rows * V7X_LANES - total)).reshape(rows, V7X_LANES)
    (gathered,) = _all_gather("gather_small_grads", [flat])
    gathered = gathered.reshape(N_DEV, -1)
    n_rep = sum(sizes[:-1])
    cw_all = gathered[:, n_rep:total].reshape(N_DEV, depth, CONV_WIDTH, N_DEV, 2 * half)
    cw_mine = lax.dynamic_index_in_dim(cw_all, me, axis=3, keepdims=False)
    n_small = n_rep + cw_mine[0].size
    rows2 = -(-n_small // (512 * V7X_LANES)) * 512

    def flatten(parts_):
        return jnp.pad(jnp.concatenate(parts_, -1), [(0, 0)] * (parts_[0].ndim - 1) + [(0, rows2 * V7X_LANES - n_small)])

    p_small = flatten([gathered[:, :n_rep], cw_mine.reshape(N_DEV, -1)]).reshape(N_DEV, 1, rows2, V7X_LANES)
    w_small, m_small, v_small = (
        flatten([jnp.concatenate([t[k].reshape(-1) for k in _SMALL]), t["ffn_conv_w"].reshape(-1)]).reshape(1, rows2, V7X_LANES)
        for t in (W, M, V))
    res_small = [a.reshape(-1) for a in _reduce_adamw("adamw_small", p_small, w_small, m_small, v_small)]
    off = 0
    for k in _SMALL + ("ffn_conv_w",):
        n = W[k].size
        outs[k] = tuple(a[off:off + n].reshape(W[k].shape) for a in res_small)
        off += n

    finish_exchange(res_small[0])
    for name_ in _BIG:
        outs[name_] = tuple(from_groups(name_, t) for t in big_res[name_])

    grad_x = dy[None]
    result = [loss, grad_x]
    for i in range(4):
        result += [outs[k][i] for k in _ORDER]
    return tuple(result)
```

```python
import functools
import math

import jax
import jax.numpy as jnp
from jax import lax
from jax.experimental import pallas as pl
from jax.experimental.pallas import tpu as pltpu

F32 = jnp.float32
_MXU = jnp.bfloat16
_WIRE = jnp.bfloat16

HEAD_DIM = 64
GQA = 4
ATTN_BLOCK = 128
ROPE_THETA = 10000.0
POOL_WINDOWS = (2, 4, 8, 16)
SSM_GROUP = 16
SSM_STATE = 64
CONV_WIDTH = 3
LN_EPS = 1e-5
ADAM_LR, ADAM_B1, ADAM_B2, ADAM_EPS, ADAM_WD, ADAM_STEP = 0.001, 0.9, 0.999, 1e-08, 0.01, 10

N_DEV = 8
V7X_LANES = 128
V7X_VMEM_LIMIT = 56 * 1024 * 1024
SCAN_T = 64
SCAN_LANES = 256
MESH = pl.DeviceIdType.MESH
ANY = pl.BlockSpec(memory_space=pl.ANY)


def _cp():
    return pltpu.CompilerParams(vmem_limit_bytes=V7X_VMEM_LIMIT)


def _resident(block, index_map):
    return pl.BlockSpec(block, index_map, pipeline_mode=pl.Buffered(1))


def _sds(shape, dtype):
    return jax.ShapeDtypeStruct(tuple(shape), dtype)


def _mm(name, a, b, dims, grid, a_spec, b_spec, o_spec, out_shape, out_dtype, add=None, add_spec=None, add_scale=1.0, deps=()):
    nk = grid[1]
    n_in = 2 + (add is not None) + len(deps)
    oblk = tuple(d for d in o_spec.block_shape if d is not None)

    def body(*refs):
        a_ref, b_ref = refs[:2]
        add_ref = None if add is None else refs[2]
        o_ref = refs[n_in]
        acc_ref = refs[-1] if nk > 1 else None

        def finish(r):
            if add_ref is not None:
                r = r + add_scale * add_ref[...]
            o_ref[...] = r.astype(o_ref.dtype)

        part = lax.dot_general(a_ref[...], b_ref[...], (dims, ((), ())), preferred_element_type=F32)
        if nk == 1:
            finish(part)
        else:
            k = pl.program_id(1)

            @pl.when(k == 0)
            def _():
                acc_ref[...] = part

            @pl.when(k > 0)
            def _():
                acc_ref[...] += part

            @pl.when(k == nk - 1)
            def _():
                finish(acc_ref[...])

    ins = [a, b] + ([] if add is None else [add]) + list(deps)
    in_specs = [a_spec, b_spec] + ([] if add is None else [add_spec]) + [ANY] * len(deps)
    return pl.pallas_call(
        body, grid=grid, in_specs=in_specs, out_specs=o_spec, out_shape=_sds(out_shape, out_dtype),
        scratch_shapes=[pltpu.VMEM(oblk, F32)] if nk > 1 else [], compiler_params=_cp(), name=name,
    )(*ins)


NN = ((1,), (0,))
NT = ((1,), (1,))
TN = ((0,), (0,))


def _col_tile(n, cap=512):
    if n % V7X_LANES:
        return n
    t = min(cap, n)
    t -= t % V7X_LANES
    while n % t:
        t -= V7X_LANES
    return t


def _mm_nn(name, a, b, out_dtype=F32, cap=512, add=None, add_scale=1.0):
    m, k = a.shape
    n = b.shape[1]
    tn = _col_tile(n, cap)
    o_spec = pl.BlockSpec((m, tn), lambda j, kk: (0, j))
    return _mm(name, a, b, NN, (n // tn, 1), _resident((m, k), lambda j, kk: (0, 0)),
               pl.BlockSpec((k, tn), lambda j, kk: (0, j)), o_spec, (m, n), out_dtype,
               add=add, add_spec=None if add is None else o_spec, add_scale=add_scale)


def _mm_nt(name, a, b, out_dtype=F32, add=None, add_scale=1.0, cap=512, deps=()):
    m, k = a.shape
    n = b.shape[0]
    tn = _col_tile(n, cap)
    o_spec = pl.BlockSpec((m, tn), lambda j, kk: (0, j))
    return _mm(name, a, b, NT, (n // tn, 1), _resident((m, k), lambda j, kk: (0, 0)),
               pl.BlockSpec((tn, k), lambda j, kk: (j, 0)), o_spec, (m, n), out_dtype,
               add=add, add_spec=None if add is None else o_spec, add_scale=add_scale, deps=deps)


def _mm_tn_bcols(name, a, b, out_dtype, cap=512):
    s, k = a.shape
    n = b.shape[1]
    tn = _col_tile(n, cap)
    return _mm(name, a, b, TN, (n // tn, 1), _resident((s, k), lambda j, kk: (0, 0)),
               pl.BlockSpec((s, tn), lambda j, kk: (0, j)), pl.BlockSpec((k, tn), lambda j, kk: (0, j)), (k, n), out_dtype)


def _mm_tn_acols(name, a, b, out_dtype, cap=512):
    s, k = a.shape
    n = b.shape[1]
    tk = _col_tile(k, cap)
    return _mm(name, a, b, TN, (k // tk, 1), pl.BlockSpec((s, tk), lambda i, kk: (0, i)),
               _resident((s, n), lambda i, kk: (0, 0)), pl.BlockSpec((tk, n), lambda i, kk: (i, 0)), (k, n), out_dtype)


def _ln_fwd(name, x, a, g, b, alpha):
    s, d = x.shape
    tr = min(256, s)

    def body(x_ref, a_ref, g_ref, b_ref, y_ref, yb_ref, xh_ref, rs_ref):
        r = alpha * x_ref[...] + a_ref[...]
        mu = jnp.mean(r, -1, keepdims=True)
        c = r - mu
        var = jnp.mean(c * c, -1, keepdims=True)
        rstd = lax.rsqrt(var + LN_EPS)
        xh = c * rstd
        y = xh * g_ref[...] + b_ref[...]
        y_ref[...] = y
        yb_ref[...] = y.astype(_MXU)
        xh_ref[...] = xh
        rs_ref[...] = rstd

    row = pl.BlockSpec((tr, d), lambda i: (i, 0))
    vec = pl.BlockSpec((1, d), lambda i: (0, 0))
    return pl.pallas_call(
        body, grid=(s // tr,), in_specs=[row, row, vec, vec],
        out_specs=[row, row, row, pl.BlockSpec((tr, 1), lambda i: (i, 0))],
        out_shape=[_sds((s, d), F32), _sds((s, d), _MXU), _sds((s, d), F32), _sds((s, 1), F32)],
        compiler_params=_cp(), name=name,
    )(x, a, g, b)


def _ln_bwd(name, dy, xh, rstd, g, deps=()):
    s, d = dy.shape
    tr = min(256, s)
    nd = len(deps)

    def body(dy_ref, xh_ref, rs_ref, g_ref, *rest):
        dr_ref, drb_ref, dg_ref, db_ref = rest[nd:]
        i = pl.program_id(0)
        dy_ = dy_ref[...]
        xh_ = xh_ref[...]
        dxh = dy_ * g_ref[...]
        m1 = jnp.mean(dxh, -1, keepdims=True)
        m2 = jnp.mean(dxh * xh_, -1, keepdims=True)
        dr = rs_ref[...] * (dxh - m1 - xh_ * m2)
        dr_ref[...] = dr
        drb_ref[...] = dr.astype(_MXU)
        pg = jnp.sum(dy_ * xh_, 0, keepdims=True)
        pb = jnp.sum(dy_, 0, keepdims=True)

        @pl.when(i == 0)
        def _():
            dg_ref[...] = pg
            db_ref[...] = pb

        @pl.when(i > 0)
        def _():
            dg_ref[...] += pg
            db_ref[...] += pb

    row = pl.BlockSpec((tr, d), lambda i: (i, 0))
    vec = pl.BlockSpec((1, d), lambda i: (0, 0))
    return pl.pallas_call(
        body, grid=(s // tr,), in_specs=[row, row, pl.BlockSpec((tr, 1), lambda i: (i, 0)), vec] + [ANY] * nd,
        out_specs=[row, row, vec, vec],
        out_shape=[_sds((s, d), F32), _sds((s, d), _MXU), _sds((1, d), F32), _sds((1, d), F32)],
        compiler_params=_cp(), name=name,
    )(dy, xh, rstd, g, *deps)


def _loss_head(name, y, target):
    s, d = y.shape
    tr = min(256, s)

    def body(y_ref, t_ref, dy_ref, l_ref):
        i = pl.program_id(0)
        e = y_ref[...] - t_ref[...]
        dy_ref[...] = e * (1.0 / d)
        part = 0.5 * jnp.sum(jnp.mean(e * e, -1, keepdims=True), 0, keepdims=True)

        @pl.when(i == 0)
        def _():
            l_ref[...] = part

        @pl.when(i > 0)
        def _():
            l_ref[...] += part

    row = pl.BlockSpec((tr, d), lambda i: (i, 0))
    return pl.pallas_call(
        body, grid=(s // tr,), in_specs=[row, row], out_specs=[row, pl.BlockSpec((1, 1), lambda i: (0, 0))],
        out_shape=[_sds((s, d), F32), _sds((1, 1), F32)], compiler_params=_cp(), name=name,
    )(y, target)


def _rope(name, t, width, cos, sin, out_dtype, splits):
    s = t.shape[0]
    tr = min(256, s)
    assert width % V7X_LANES == 0

    def body(t_ref, c_ref, s_ref, *o_refs):
        lane = lax.broadcasted_iota(jnp.int32, (tr, V7X_LANES), 1)
        first = (lane % HEAD_DIM) < (HEAD_DIM // 2)
        cs, sn = c_ref[...], s_ref[...]
        for (lo, hi), o_ref in zip(splits, o_refs):
            for c0 in range(lo, hi, V7X_LANES):
                v = t_ref[:, c0:c0 + V7X_LANES].astype(F32)
                partner = jnp.where(first, pltpu.roll(v, V7X_LANES - HEAD_DIM // 2, 1), pltpu.roll(v, HEAD_DIM // 2, 1))
                o_ref[:, c0 - lo:c0 - lo + V7X_LANES] = (v * cs + partner * sn).astype(o_ref.dtype)

    tab = pl.BlockSpec((tr, V7X_LANES), lambda i: (i, 0))
    return pl.pallas_call(
        body, grid=(s // tr,), in_specs=[pl.BlockSpec((tr, width), lambda i: (i, 0)), tab, tab],
        out_specs=[pl.BlockSpec((tr, hi - lo), lambda i: (i, 0)) for lo, hi in splits],
        out_shape=[_sds((s, hi - lo), out_dtype) for lo, hi in splits], compiler_params=_cp(), name=name,
    )(t, cos, sin)


def _attn_masks():
    i = lax.broadcasted_iota(jnp.int32, (ATTN_BLOCK, ATTN_BLOCK), 0)
    j = lax.broadcasted_iota(jnp.int32, (ATTN_BLOCK, ATTN_BLOCK), 1)
    return j <= i, j > i


def _attn_scores(qg, kc, kp, n, cur_ok, prev_ok):
    scale = HEAD_DIM ** -0.5
    s_c = lax.dot_general(qg, kc, (NT, ((), ())), preferred_element_type=F32) * scale
    s_p = lax.dot_general(qg, kp, (NT, ((), ())), preferred_element_type=F32) * scale
    s_c = jnp.where(cur_ok, s_c, -1e30)
    s_p = jnp.where(jnp.logical_and(prev_ok, n > 0), s_p, -1e30)
    return s_c, s_p


def _attn_fwd(name, q, k, v, sinks):
    s = q.shape[0]
    nkv = k.shape[0]
    gw = GQA * HEAD_DIM
    nb = s // ATTN_BLOCK

    def body(sk_ref, q_ref, k_ref, v_ref, o_ref, lse_ref):
        h = pl.program_id(0)
        cur_ok, prev_ok = _attn_masks()

        def blk(n, carry):
            off = pl.multiple_of(n * ATTN_BLOCK, ATTN_BLOCK)
            poff = pl.multiple_of(jnp.maximum(n - 1, 0) * ATTN_BLOCK, ATTN_BLOCK)
            kc, kp = k_ref[pl.ds(off, ATTN_BLOCK), :], k_ref[pl.ds(poff, ATTN_BLOCK), :]
            vc, vp = v_ref[pl.ds(off, ATTN_BLOCK), :], v_ref[pl.ds(poff, ATTN_BLOCK), :]
            for g in range(GQA):
                qg = q_ref[pl.ds(off, ATTN_BLOCK), g * HEAD_DIM:(g + 1) * HEAD_DIM]
                s_c, s_p = _attn_scores(qg, kc, kp, n, cur_ok, prev_ok)
                sink = sk_ref[h, g]
                m = jnp.maximum(jnp.maximum(s_c.max(-1, keepdims=True), s_p.max(-1, keepdims=True)), sink)
                p_c, p_p = jnp.exp(s_c - m), jnp.exp(s_p - m)
                den = p_c.sum(-1, keepdims=True) + p_p.sum(-1, keepdims=True) + jnp.exp(sink - m)
                o = (jnp.dot((p_c / den).astype(_MXU), vc, preferred_element_type=F32)
                     + jnp.dot((p_p / den).astype(_MXU), vp, preferred_element_type=F32))
                o_ref[pl.ds(off, ATTN_BLOCK), g * HEAD_DIM:(g + 1) * HEAD_DIM] = o
                lse_ref[pl.ds(off, ATTN_BLOCK), g:g + 1] = m + jnp.log(den)
            return carry

        lax.fori_loop(0, nb, blk, 0)

    kv_spec = pl.BlockSpec((None, s, HEAD_DIM), lambda h: (h, 0, 0))
    return pl.pallas_call(
        body, grid=(nkv,),
        in_specs=[pl.BlockSpec(memory_space=pltpu.SMEM), pl.BlockSpec((s, gw), lambda h: (0, h)), kv_spec, kv_spec],
        out_specs=[pl.BlockSpec((s, gw), lambda h: (0, h)), pl.BlockSpec((None, s, GQA), lambda h: (h, 0, 0))],
        out_shape=[_sds((s, nkv * gw), F32), _sds((nkv, s, GQA), F32)], compiler_params=_cp(), name=name,
    )(sinks, q, k, v)


def _attn_bwd(name, q, k, v, o, dmix, lse, sinks):
    s = q.shape[0]
    nkv = k.shape[0]
    gw = GQA * HEAD_DIM
    nb = s // ATTN_BLOCK
    scale = HEAD_DIM ** -0.5

    def body(sk_ref, q_ref, k_ref, v_ref, o_ref, do_ref, lse_ref, dq_ref, dk_ref, dv_ref, dsk_ref):
        h = pl.program_id(0)
        cur_ok, prev_ok = _attn_masks()
        dk_ref[...] = jnp.zeros_like(dk_ref)
        dv_ref[...] = jnp.zeros_like(dv_ref)

        def blk(n, acc):
            off = pl.multiple_of(n * ATTN_BLOCK, ATTN_BLOCK)
            poff = pl.multiple_of(jnp.maximum(n - 1, 0) * ATTN_BLOCK, ATTN_BLOCK)
            rows, prows = pl.ds(off, ATTN_BLOCK), pl.ds(poff, ATTN_BLOCK)
            kc, kp, vc, vp = k_ref[rows, :], k_ref[prows, :], v_ref[rows, :], v_ref[prows, :]
            dk_c = jnp.zeros((ATTN_BLOCK, HEAD_DIM), F32)
            dk_p = jnp.zeros((ATTN_BLOCK, HEAD_DIM), F32)
            dv_c = jnp.zeros((ATTN_BLOCK, HEAD_DIM), F32)
            dv_p = jnp.zeros((ATTN_BLOCK, HEAD_DIM), F32)
            out_acc = []
            for g in range(GQA):
                cols = slice(g * HEAD_DIM, (g + 1) * HEAD_DIM)
                qg = q_ref[rows, cols]
                do_g = do_ref[rows, cols]
                delta = jnp.sum(do_g * o_ref[rows, cols], -1, keepdims=True)
                dob = do_g.astype(_MXU)
                lse_g = lse_ref[rows, g:g + 1]
                s_c, s_p = _attn_scores(qg, kc, kp, n, cur_ok, prev_ok)
                p_c, p_p = jnp.exp(s_c - lse_g), jnp.exp(s_p - lse_g)
                dp_c = lax.dot_general(dob, vc, (NT, ((), ())), preferred_element_type=F32)
                dp_p = lax.dot_general(dob, vp, (NT, ((), ())), preferred_element_type=F32)
                ds_c = (p_c * (dp_c - delta) * scale).astype(_MXU)
                ds_p = (p_p * (dp_p - delta) * scale).astype(_MXU)
                dq_ref[rows, cols] = (jnp.dot(ds_c, kc, preferred_element_type=F32)
                                      + jnp.dot(ds_p, kp, preferred_element_type=F32))
                dk_c += lax.dot_general(ds_c, qg, (TN, ((), ())), preferred_element_type=F32)
                dk_p += lax.dot_general(ds_p, qg, (TN, ((), ())), preferred_element_type=F32)
                dv_c += lax.dot_general(p_c.astype(_MXU), dob, (TN, ((), ())), preferred_element_type=F32)
                dv_p += lax.dot_general(p_p.astype(_MXU), dob, (TN, ((), ())), preferred_element_type=F32)
                out_acc.append(acc[g] - jnp.exp(sk_ref[h, g] - lse_g) * delta)
            dk_ref[rows, :] += dk_c
            dv_ref[rows, :] += dv_c
            dk_ref[prows, :] += dk_p
            dv_ref[prows, :] += dv_p
            return tuple(out_acc)

        acc = lax.fori_loop(0, nb, blk, tuple(jnp.zeros((ATTN_BLOCK, 1), F32) for _ in range(GQA)))
        for g in range(GQA):
            dsk_ref[:, g:g + 1] = jnp.sum(acc[g], 0, keepdims=True)

    kv_spec = pl.BlockSpec((None, s, HEAD_DIM), lambda h: (h, 0, 0))
    qcols = pl.BlockSpec((s, gw), lambda h: (0, h))
    return pl.pallas_call(
        body, grid=(nkv,),
        in_specs=[pl.BlockSpec(memory_space=pltpu.SMEM), qcols, kv_spec, kv_spec, qcols, qcols,
                  pl.BlockSpec((None, s, GQA), lambda h: (h, 0, 0))],
        out_specs=[qcols, kv_spec, kv_spec, pl.BlockSpec((None, 1, GQA), lambda h: (h, 0, 0))],
        out_shape=[_sds((s, nkv * gw), F32), _sds((nkv, s, HEAD_DIM), F32), _sds((nkv, s, HEAD_DIM), F32),
                   _sds((nkv, 1, GQA), F32)],
        compiler_params=_cp(), name=name,
    )(sinks, q, k, v, o, dmix, lse)


def _shift_down(a, k, t):
    return jnp.where(t >= k, pltpu.roll(a, k, 0), 0.0)


def _shift_up(a, k, t):
    n = a.shape[0]
    return jnp.where(t < n - k, pltpu.roll(a, n - k, 0), 0.0)


def _pool_fwd(name, h, col_block, pool_w, pool_scale):
    s = h.shape[0]
    ng, pg = pool_w.shape[0], pool_w.shape[1]
    pw_ = ng * pg

    def body(u_ref, w_ref, sc_ref, y_ref, pre_ref):
        t = lax.broadcasted_iota(jnp.int32, (s, pg), 0)
        for gi, win in enumerate(POOL_WINDOWS):
            cols = slice(gi * pg, (gi + 1) * pg)
            u = u_ref[:, cols]
            a = u
            k = 1
            while k < win:
                a = a + _shift_down(a, k, t)
                k *= 2
            div = jnp.minimum(t + 1, win).astype(F32)
            pre = (a / div - u).astype(_MXU)
            pre_ref[:, cols] = pre
            out = jnp.dot(pre, w_ref[gi], preferred_element_type=F32)
            y_ref[:, cols] = (out * sc_ref[:, cols]).astype(y_ref.dtype)

    blk = pl.BlockSpec((s, pw_), lambda i: (0, 0))
    return pl.pallas_call(
        body, grid=(1,),
        in_specs=[pl.BlockSpec((s, pw_), lambda i: (0, col_block)), pl.BlockSpec((ng, pg, pg), lambda i: (0, 0, 0)),
                  pl.BlockSpec((1, pw_), lambda i: (0, 0))],
        out_specs=[blk, blk], out_shape=[_sds((s, pw_), _MXU), _sds((s, pw_), _MXU)], compiler_params=_cp(), name=name,
    )(h, pool_w, pool_scale)


def _pool_bwd(name, dmix, col_block, pre, pool_w, pool_scale):
    s = pre.shape[0]
    ng, pg = pool_w.shape[0], pool_w.shape[1]
    pw_ = ng * pg

    def body(dy_ref, pre_ref, w_ref, sc_ref, du_ref, dw_ref, dsc_ref):
        t = lax.broadcasted_iota(jnp.int32, (s, pg), 0)
        for gi, win in enumerate(POOL_WINDOWS):
            cols = slice(gi * pg, (gi + 1) * pg)
            pre_g = pre_ref[:, cols]
            dy = dy_ref[:, cols]
            out = jnp.dot(pre_g, w_ref[gi], preferred_element_type=F32)
            dsc_ref[:, cols] = jnp.sum(dy * out, 0, keepdims=True)
            dout = (dy * sc_ref[:, cols]).astype(_MXU)
            dw_ref[gi] = lax.dot_general(pre_g, dout, (TN, ((), ())), preferred_element_type=F32)
            dpre = lax.dot_general(dout, w_ref[gi], (NT, ((), ())), preferred_element_type=F32)
            div = jnp.minimum(t + 1, win).astype(F32)
            a = dpre / div
            k = 1
            while k < win:
                a = a + _shift_up(a, k, t)
                k *= 2
            du_ref[:, cols] = (a - dpre).astype(du_ref.dtype)

    blk = pl.BlockSpec((s, pw_), lambda i: (0, 0))
    wspec = pl.BlockSpec((ng, pg, pg), lambda i: (0, 0, 0))
    vec = pl.BlockSpec((1, pw_), lambda i: (0, 0))
    return pl.pallas_call(
        body, grid=(1,), in_specs=[pl.BlockSpec((s, pw_), lambda i: (0, col_block)), blk, wspec, vec],
        out_specs=[blk, wspec, vec], out_shape=[_sds((s, pw_), _MXU), _sds((ng, pg, pg), F32), _sds((1, pw_), F32)],
        compiler_params=_cp(), name=name,
    )(dmix, pre, pool_w, pool_scale)


def _scan_chunks(xr_ref, xi_ref, sr_ref, si_ref, ar, ai, reverse):
    n, c = xr_ref.shape
    tt = min(SCAN_T, n)
    lw = min(SCAN_LANES, c)
    nchunk = n // tt
    t = lax.broadcasted_iota(jnp.int32, (tt, lw), 0)

    for l0 in range(0, c, lw):
        lanes = slice(l0, l0 + lw)
        a_r, a_i = ar[:, lanes], ai[:, lanes]

        def local(vr, vi, a_r=a_r, a_i=a_i):
            pr, pi = a_r, a_i
            k = 1
            while k < tt:
                if reverse:
                    hr, hi = _shift_up(vr, k, t), _shift_up(vi, k, t)
                else:
                    hr, hi = _shift_down(vr, k, t), _shift_down(vi, k, t)
                vr, vi = vr + pr * hr - pi * hi, vi + pr * hi + pi * hr
                pr, pi = pr * pr - pi * pi, 2.0 * pr * pi
                k *= 2
            return vr, vi

        edge = tt - 1 if reverse else 0
        pw_r, pw_i = local(jnp.where(t == edge, a_r, 0.0), jnp.where(t == edge, a_i, 0.0))
        last = 0 if reverse else tt - 1

        def body(i, carry, lanes=lanes, local=local, pw_r=pw_r, pw_i=pw_i):
            cr, ci = carry
            ch = nchunk - 1 - i if reverse else i
            rows = pl.ds(pl.multiple_of(ch * tt, tt), tt)
            vr, vi = local(xr_ref[rows, lanes], xi_ref[rows, lanes])
            vr2 = vr + pw_r * cr - pw_i * ci
            vi2 = vi + pw_r * ci + pw_i * cr
            sr_ref[rows, lanes] = vr2
            si_ref[rows, lanes] = vi2
            return vr2[last:last + 1, :], vi2[last:last + 1, :]

        lax.fori_loop(0, nchunk, body, (jnp.zeros((1, lw), F32), jnp.zeros((1, lw), F32)))


_GELU_K = math.sqrt(2.0 / math.pi)


def _gelu_grad(y):
    inner = _GELU_K * (y + 0.044715 * y * y * y)
    th = jnp.tanh(inner)
    return 0.5 * (1.0 + th) + 0.5 * y * (1.0 - th * th) * _GELU_K * (1.0 + 3.0 * 0.044715 * y * y)


def _ssm_fwd(name, h, u_block0, bdr, bdi, cdr, cdi, dvec, ar, ai):
    s = h.shape[0]
    nt, cw, lw = bdr.shape
    rc = min(256, s)

    def body(u_ref, bdr_ref, bdi_ref, cdr_ref, cdi_ref, d_ref, ar_ref, ai_ref, sr_ref, si_ref, y_ref, yg_ref):
        def mm_in(c, _):
            rows = pl.ds(pl.multiple_of(c * rc, rc), rc)
            ub = u_ref[rows, :].astype(_MXU)
            sr_ref[rows, :] = jnp.dot(ub, bdr_ref[...], preferred_element_type=F32)
            si_ref[rows, :] = jnp.dot(ub, bdi_ref[...], preferred_element_type=F32)
            return 0

        lax.fori_loop(0, s // rc, mm_in, 0)
        _scan_chunks(sr_ref, si_ref, sr_ref, si_ref, ar_ref[...], ai_ref[...], reverse=False)

        def mm_out(c, _):
            rows = pl.ds(pl.multiple_of(c * rc, rc), rc)
            y = (jnp.dot(sr_ref[rows, :].astype(_MXU), cdr_ref[...], preferred_element_type=F32)
                 - jnp.dot(si_ref[rows, :].astype(_MXU), cdi_ref[...], preferred_element_type=F32)
                 + d_ref[...] * u_ref[rows, :])
            y_ref[rows, :] = y
            yg_ref[rows, :] = jax.nn.gelu(y).astype(yg_ref.dtype)
            return 0

        lax.fori_loop(0, s // rc, mm_out, 0)

    st = pl.BlockSpec((s, lw), lambda j: (0, j))
    ch = pl.BlockSpec((s, cw), lambda j: (0, j))
    bspec = pl.BlockSpec((None, cw, lw), lambda j: (j, 0, 0))
    cspec = pl.BlockSpec((None, lw, cw), lambda j: (j, 0, 0))
    return pl.pallas_call(
        body, grid=(nt,),
        in_specs=[pl.BlockSpec((s, cw), lambda j: (0, u_block0 + j)), bspec, bspec, cspec, cspec,
                  pl.BlockSpec((1, cw), lambda j: (0, j)), pl.BlockSpec((1, lw), lambda j: (0, j)),
                  pl.BlockSpec((1, lw), lambda j: (0, j))],
        out_specs=[st, st, ch, ch],
        out_shape=[_sds((s, nt * lw), F32), _sds((s, nt * lw), F32), _sds((s, nt * cw), F32), _sds((s, nt * cw), _MXU)],
        compiler_params=_cp(), name=name,
    )(h, bdr, bdi, cdr, cdi, dvec, ar, ai)


def _ssm_bwd(name, dyg, ypre, h, u_block0, sr, si, bdr, bdi, cdr, cdi, dvec, ar, ai):
    s = h.shape[0]
    nt, cw, lw = bdr.shape
    rc = min(256, s)

    def body(dyg_ref, yp_ref, u_ref, sr_ref, si_ref, bdr_ref, bdi_ref, cdr_ref, cdi_ref, d_ref, ar_ref, ai_ref,
             du_ref, dd_ref, dcr_ref, dci_ref, dbr_ref, dbi_ref, dar_ref, dai_ref, lr_scr, li_scr, dy_scr):
        for ref in (dd_ref, dcr_ref, dci_ref, dbr_ref, dbi_ref, dar_ref, dai_ref):
            ref[...] = jnp.zeros_like(ref)

        def p1(c, _):
            rows = pl.ds(pl.multiple_of(c * rc, rc), rc)
            dy = dyg_ref[rows, :] * _gelu_grad(yp_ref[rows, :])
            dy_scr[rows, :] = dy
            dd_ref[...] += jnp.sum(dy * u_ref[rows, :], 0, keepdims=True)
            dyb = dy.astype(_MXU)
            lr_scr[rows, :] = lax.dot_general(dyb, cdr_ref[...], (NT, ((), ())), preferred_element_type=F32)
            li_scr[rows, :] = -lax.dot_general(dyb, cdi_ref[...], (NT, ((), ())), preferred_element_type=F32)
            dcr_ref[...] += lax.dot_general(sr_ref[rows, :].astype(_MXU), dyb, (TN, ((), ())), preferred_element_type=F32)
            dci_ref[...] -= lax.dot_general(si_ref[rows, :].astype(_MXU), dyb, (TN, ((), ())), preferred_element_type=F32)
            return 0

        lax.fori_loop(0, s // rc, p1, 0)
        _scan_chunks(lr_scr, li_scr, lr_scr, li_scr, ar_ref[...], -ai_ref[...], reverse=True)
        t = lax.broadcasted_iota(jnp.int32, (rc, lw), 0)

        def p2(c, _):
            r0 = pl.multiple_of(c * rc, rc)
            rows = pl.ds(r0, rc)
            before = pl.ds(pl.multiple_of(jnp.maximum(r0 - 8, 0), 8), 8)
            have = (c > 0).astype(F32)
            lr, li = lr_scr[rows, :], li_scr[rows, :]
            spr = jnp.where(t == 0, sr_ref[before, :][7:8, :] * have, pltpu.roll(sr_ref[rows, :], 1, 0))
            spi = jnp.where(t == 0, si_ref[before, :][7:8, :] * have, pltpu.roll(si_ref[rows, :], 1, 0))
            dar_ref[...] += jnp.sum(lr * spr + li * spi, 0, keepdims=True)
            dai_ref[...] += jnp.sum(li * spr - lr * spi, 0, keepdims=True)
            lrb, lib = lr.astype(_MXU), li.astype(_MXU)
            du = (dy_scr[rows, :] * d_ref[...]
                  + lax.dot_general(lrb, bdr_ref[...], (NT, ((), ())), preferred_element_type=F32)
                  + lax.dot_general(lib, bdi_ref[...], (NT, ((), ())), preferred_element_type=F32))
            du_ref[rows, :] = du.astype(du_ref.dtype)
            ub = u_ref[rows, :].astype(_MXU)
            dbr_ref[...] += lax.dot_general(ub, lrb, (TN, ((), ())), preferred_element_type=F32)
            dbi_ref[...] += lax.dot_general(ub, lib, (TN, ((), ())), preferred_element_type=F32)
            return 0

        lax.fori_loop(0, s // rc, p2, 0)

    st = pl.BlockSpec((s, lw), lambda j: (0, j))
    ch = pl.BlockSpec((s, cw), lambda j: (0, j))
    bspec = pl.BlockSpec((None, cw, lw), lambda j: (j, 0, 0))
    cspec = pl.BlockSpec((None, lw, cw), lambda j: (j, 0, 0))
    cvec = pl.BlockSpec((1, cw), lambda j: (0, j))
    svec = pl.BlockSpec((1, lw), lambda j: (0, j))
    return pl.pallas_call(
        body, grid=(nt,),
        in_specs=[ch, ch, pl.BlockSpec((s, cw), lambda j: (0, u_block0 + j)), st, st, bspec, bspec, cspec, cspec, cvec, svec, svec],
        out_specs=[ch, cvec, cspec, cspec, bspec, bspec, svec, svec],
        out_shape=[_sds((s, nt * cw), _MXU), _sds((1, nt * cw), F32), _sds((nt, lw, cw), F32), _sds((nt, lw, cw), F32),
                   _sds((nt, cw, lw), F32), _sds((nt, cw, lw), F32), _sds((1, nt * lw), F32), _sds((1, nt * lw), F32)],
        scratch_shapes=[pltpu.VMEM((s, lw), F32), pltpu.VMEM((s, lw), F32), pltpu.VMEM((s, cw), F32)],
        compiler_params=_cp(), name=name,
    )(dyg, ypre, h, sr, si, bdr, bdi, cdr, cdi, dvec, ar, ai)


def _glu_fwd(name, yg, gw):
    s, w = yg.shape
    tr = min(512, s)

    def body(y_ref, w_ref, o_ref, ab_ref):
        ab = jnp.dot(y_ref[...], w_ref[...], preferred_element_type=F32)
        ab_ref[...] = ab
        o_ref[...] = (ab[:, :w] * jax.nn.sigmoid(ab[:, w:])).astype(o_ref.dtype)

    return pl.pallas_call(
        body, grid=(s // tr,), in_specs=[pl.BlockSpec((tr, w), lambda i: (i, 0)), _resident((w, 2 * w), lambda i: (0, 0))],
        out_specs=[pl.BlockSpec((tr, w), lambda i: (i, 0)), pl.BlockSpec((tr, 2 * w), lambda i: (i, 0))],
        out_shape=[_sds((s, w), _MXU), _sds((s, 2 * w), F32)], compiler_params=_cp(), name=name,
    )(yg, gw)


def _glu_bwd(name, dmix, col_block, ab, gw):
    s = ab.shape[0]
    w = ab.shape[1] // 2
    tr = min(512, s)

    def body(do_ref, ab_ref, w_ref, dab_ref, dy_ref):
        do = do_ref[...]
        a, b = ab_ref[:, :w], ab_ref[:, w:]
        sg = jax.nn.sigmoid(b)
        da = (do * sg).astype(_MXU)
        db = (do * a * sg * (1.0 - sg)).astype(_MXU)
        dab_ref[:, :w] = da
        dab_ref[:, w:] = db
        dy_ref[...] = (lax.dot_general(da, w_ref[:, :w], (NT, ((), ())), preferred_element_type=F32)
                       + lax.dot_general(db, w_ref[:, w:], (NT, ((), ())), preferred_element_type=F32))

    return pl.pallas_call(
        body, grid=(s // tr,),
        in_specs=[pl.BlockSpec((tr, w), lambda i: (i, col_block)), pl.BlockSpec((tr, 2 * w), lambda i: (i, 0)),
                  _resident((w, 2 * w), lambda i: (0, 0))],
        out_specs=[pl.BlockSpec((tr, 2 * w), lambda i: (i, 0)), pl.BlockSpec((tr, w), lambda i: (i, 0))],
        out_shape=[_sds((s, 2 * w), _MXU), _sds((s, w), F32)], compiler_params=_cp(), name=name,
    )(dmix, ab, gw)


def _conv(hu, w_ref, b_ref, t):
    hc = b_ref[...] + _shift_down(hu, 2, t) * w_ref[0:1, :]
    hc = hc + _shift_down(hu, 1, t) * w_ref[1:2, :]
    return hc + hu * w_ref[2:3, :]


def _conv_act_fwd(name, hu, cw, cb):
    s, f2 = hu.shape
    f = f2 // 2
    tw = _col_tile(f, 256)
    nt = f // tw

    def body(v_ref, g_ref, wv_ref, wg_ref, bv_ref, bg_ref, act_ref):
        t = lax.broadcasted_iota(jnp.int32, (s, tw), 0)
        val = _conv(v_ref[...], wv_ref, bv_ref, t)
        gate = _conv(g_ref[...], wg_ref, bg_ref, t)
        act_ref[...] = (jax.nn.silu(gate) * val).astype(act_ref.dtype)

    cv = lambda rows: pl.BlockSpec((rows, tw), lambda i: (0, i))
    cg = lambda rows: pl.BlockSpec((rows, tw), lambda i: (0, nt + i))
    return pl.pallas_call(
        body, grid=(nt,), in_specs=[cv(s), cg(s), cv(CONV_WIDTH), cg(CONV_WIDTH), cv(1), cg(1)],
        out_specs=cv(s), out_shape=_sds((s, f), _MXU), compiler_params=_cp(), name=name,
    )(hu, hu, cw, cw, cb, cb)


def _conv_act_bwd(name, dact, hu, cw, cb):
    s, f2 = hu.shape
    f = f2 // 2
    tw = _col_tile(f, 256)
    nt = f // tw

    def body(da_ref, v_ref, g_ref, wv_ref, wg_ref, bv_ref, bg_ref, dh_ref, dwv_ref, dwg_ref, dbv_ref, dbg_ref):
        t = lax.broadcasted_iota(jnp.int32, (s, tw), 0)
        hv, hg = v_ref[...], g_ref[...]
        val = _conv(hv, wv_ref, bv_ref, t)
        gate = _conv(hg, wg_ref, bg_ref, t)
        sg = jax.nn.sigmoid(gate)
        da = da_ref[...]
        dval = da * (gate * sg)
        dgate = da * val * sg * (1.0 + gate * (1.0 - sg))
        for part, dhc, hu_, w_ref, dw_ref, db_ref in ((0, dval, hv, wv_ref, dwv_ref, dbv_ref),
                                                      (1, dgate, hg, wg_ref, dwg_ref, dbg_ref)):
            db_ref[...] = jnp.sum(dhc, 0, keepdims=True)
            dw_ref[0:1, :] = jnp.sum(dhc * _shift_down(hu_, 2, t), 0, keepdims=True)
            dw_ref[1:2, :] = jnp.sum(dhc * _shift_down(hu_, 1, t), 0, keepdims=True)
            dw_ref[2:3, :] = jnp.sum(dhc * hu_, 0, keepdims=True)
            dh = dhc * w_ref[2:3, :] + _shift_up(dhc, 1, t) * w_ref[1:2, :] + _shift_up(dhc, 2, t) * w_ref[0:1, :]
            dh_ref[part] = dh.astype(dh_ref.dtype)

    cv = lambda rows: pl.BlockSpec((rows, tw), lambda i: (0, i))
    cg = lambda rows: pl.BlockSpec((rows, tw), lambda i: (0, nt + i))
    both = pl.BlockSpec((2, s, tw), lambda i: (0, 0, i))
    dh, dwv, dwg, dbv, dbg = pl.pallas_call(
        body, grid=(nt,), in_specs=[cv(s), cv(s), cg(s), cv(CONV_WIDTH), cg(CONV_WIDTH), cv(1), cg(1)],
        out_specs=[both, cv(CONV_WIDTH), cv(CONV_WIDTH), cv(1), cv(1)],
        out_shape=[_sds((2, s, f), _MXU), _sds((CONV_WIDTH, f), F32), _sds((CONV_WIDTH, f), F32),
                   _sds((1, f), F32), _sds((1, f), F32)],
        compiler_params=_cp(), name=name,
    )(dact, hu, hu, cw, cw, cb, cb)
    return dh, jnp.concatenate([dwv, dwg], 1), jnp.concatenate([dbv, dbg], 1)


ELEM_BLOCK = 512 * 1024


def _elem_tiles(r, c, budget=ELEM_BLOCK):
    rows = [t for t in range(8, r + 1, 8) if r % t == 0] or [r]
    cols = [t for t in range(V7X_LANES, c + 1, V7X_LANES) if c % t == 0] or [c]
    fits = [(tr * tc, tc, tr) for tr in rows for tc in cols if tr * tc <= budget]
    if not fits:
        return min(rows), min(cols)
    _, tc, tr = max(fits)
    return tr, tc


def _reduce_adamw(name, parts, w, m, v):
    n, nl, r, c = parts.shape
    tr, tc = _elem_tiles(r, c, ELEM_BLOCK // n)
    c1 = 1.0 - ADAM_B1 ** ADAM_STEP
    c2 = 1.0 - ADAM_B2 ** ADAM_STEP

    def body(p_ref, w_ref, m_ref, v_ref, g_ref, d_ref, nm_ref, nv_ref):
        g = p_ref[0].astype(F32)
        for i in range(1, n):
            g = g + p_ref[i].astype(F32)
        nm = ADAM_B1 * m_ref[...] + (1.0 - ADAM_B1) * g
        nv = ADAM_B2 * v_ref[...] + (1.0 - ADAM_B2) * (g * g)
        m_hat = nm / c1
        v_hat = nv / c2
        g_ref[...] = g
        nm_ref[...] = nm
        nv_ref[...] = nv
        d_ref[...] = -ADAM_LR * (m_hat / (jnp.sqrt(v_hat) + ADAM_EPS) + ADAM_WD * w_ref[...])

    blk = pl.BlockSpec((None, tr, tc), lambda l, i, j: (l, i, j))
    out = _sds((nl, r, c), F32)
    return pl.pallas_call(
        body, grid=(nl, r // tr, c // tc),
        in_specs=[pl.BlockSpec((n, None, tr, tc), lambda l, i, j: (0, l, i, j)), blk, blk, blk],
        out_specs=[blk, blk, blk, blk], out_shape=[out, out, out, out], compiler_params=_cp(), name=name,
    )(parts, w, m, v)


def _pair_sum(name, mine, theirs, c_idx):
    _, _, r, c = mine.shape
    tr, tc = _elem_tiles(r, c)

    def body(c_ref, a_ref, b_ref, o_ref):
        o_ref[...] = (a_ref[...].astype(F32) + b_ref[...].astype(F32)).astype(o_ref.dtype)

    return pl.pallas_call(
        body,
        grid_spec=pltpu.PrefetchScalarGridSpec(
            num_scalar_prefetch=1, grid=(4, r // tr, c // tc),
            in_specs=[pl.BlockSpec((None, None, tr, tc), lambda p, i, j, cref: (p, cref[0], i, j)),
                      pl.BlockSpec((None, tr, tc), lambda p, i, j, cref: (p, i, j))],
            out_specs=pl.BlockSpec((None, tr, tc), lambda p, i, j, cref: (p, i, j))),
        out_shape=_sds((4, r, c), _WIRE), compiler_params=_cp(), name=name,
    )(c_idx, mine, theirs)


def _place():
    return lax.axis_index("x"), lax.axis_index("y"), lax.axis_index("c")


def _all_gather(name, xs):
    n = len(xs)

    def body(*refs):
        x_refs, o_refs = refs[:n], refs[n:2 * n]
        send_sems, recv_sems, local_sems = refs[2 * n:]
        x, y, c = _place()
        me, sibling = (x, y, c), (x, y, 1 - c)
        chips = [(1 - x, y), (x, 1 - y), (1 - x, 1 - y)]

        def copy(a, k, block, to, src=None):
            px, py, pc = block
            rows = o_refs[a].at[4 * px + 2 * py + pc]
            return pltpu.make_async_remote_copy(
                src_ref=rows if src is None else src, dst_ref=rows, send_sem=send_sems.at[a, k], recv_sem=recv_sems.at[a, k],
                device_id=to, device_id_type=MESH)

        sent = []
        mine = []
        for a in range(n):
            mx, my, mc = me
            cp = pltpu.make_async_copy(x_refs[a], o_refs[a].at[4 * mx + 2 * my + mc], local_sems.at[a])
            cp.start()
            mine.append(cp)
            first = [copy(a, 0, me, sibling, src=x_refs[a])]
            first += [copy(a, 1 + j, me, (*chip, c), src=x_refs[a]) for j, chip in enumerate(chips)]
            for cp in first:
                cp.start()
            sent += first
        for a in range(n):
            for j, chip in enumerate(chips):
                copy(a, 1 + j, (*chip, c), me).wait_recv()
                fwd = copy(a, 4 + j, (*chip, c), sibling)
                fwd.start()
                sent.append(fwd)
        for a in range(n):
            copy(a, 0, sibling, me).wait_recv()
            for j, chip in enumerate(chips):
                copy(a, 4 + j, (*chip, 1 - c), me).wait_recv()
        for cp in sent:
            cp.wait_send()
        for cp in mine:
            cp.wait()

    return pl.pallas_call(
        body, in_specs=[ANY] * n, out_specs=[ANY] * n,
        out_shape=[_sds((N_DEV,) + a.shape, a.dtype) for a in xs],
        scratch_shapes=[pltpu.SemaphoreType.DMA((n, 7)), pltpu.SemaphoreType.DMA((n, 7)), pltpu.SemaphoreType.DMA((n,))],
        name=name,
    )(*xs)


def _swap_sibling(name, xs):
    n = len(xs)

    def body(*refs):
        x_refs, o_refs = refs[:n], refs[n:2 * n]
        send_sems, recv_sems = refs[2 * n:]
        x, y, c = _place()
        sibling = (x, y, 1 - c)
        cps = []
        for a in range(n):
            for p in range(4):
                cp = pltpu.make_async_remote_copy(
                    src_ref=x_refs[a].at[p, 1 - c], dst_ref=o_refs[a].at[p], send_sem=send_sems.at[a, p],
                    recv_sem=recv_sems.at[a, p], device_id=sibling, device_id_type=MESH)
                cp.start()
                cps.append(cp)
        for cp in cps:
            cp.wait_recv()
        for cp in cps:
            cp.wait_send()

    return pl.pallas_call(
        body, in_specs=[ANY] * n, out_specs=[ANY] * n,
        out_shape=[_sds((4,) + a.shape[2:], a.dtype) for a in xs],
        scratch_shapes=[pltpu.SemaphoreType.DMA((n, 4)), pltpu.SemaphoreType.DMA((n, 4))],
        name=name,
    )(*xs)


HBM = pl.BlockSpec(memory_space=pltpu.HBM)
SEM = pl.BlockSpec(memory_space=pltpu.SEMAPHORE)
DATAFLOW = pltpu.SideEffectType.DATAFLOW_SIDE_EFFECTING


def _in_hbm(a):
    return pltpu.with_memory_space_constraint(a, pltpu.HBM)


def _split_copy_start(name, srcs, lands, copies, deps):
    n, nd = len(srcs), len(deps)
    per = len(copies([None] * n, [None] * n, probe=True)) // n

    def body(*refs):
        s_refs, l_refs = refs[:n], refs[n:2 * n]
        send_sems, recv_sems = refs[2 * n + nd], refs[2 * n + nd + 1]
        token = refs[-1]
        for a, k, src, dst, to in copies(s_refs, l_refs):
            pltpu.make_async_remote_copy(src_ref=src, dst_ref=dst, send_sem=send_sems.at[a * per + k],
                                         recv_sem=recv_sems.at[a * per + k], device_id=to, device_id_type=MESH).start()
        token[...] = jnp.zeros_like(token)

    both = list(srcs) + list(lands)
    outs = pl.pallas_call(
        body, name=name,
        out_shape=(pltpu.SemaphoreType.DMA((n * per,)), pltpu.SemaphoreType.DMA((n * per,)),
                   *[pltpu.HBM(a.shape, a.dtype) for a in both], _sds((8, V7X_LANES), F32)),
        in_specs=[HBM] * (2 * n) + [ANY] * nd,
        out_specs=(SEM, SEM, *[HBM] * (2 * n), pl.BlockSpec(memory_space=pltpu.VMEM)),
        input_output_aliases={i: 2 + i for i in range(2 * n)},
        compiler_params=pltpu.CompilerParams(has_side_effects=DATAFLOW),
    )(*[_in_hbm(a) for a in both], *deps)
    return outs[0], outs[1], list(outs[2:2 + n]), list(outs[2 + n:2 + 2 * n]), outs[-1]


def _split_copy_wait(name, send_sems, recv_sems, srcs, lands, arrivals, after):
    n = len(srcs)
    per = len(arrivals([None] * n, [None] * n, probe=True)) // n

    def body(*refs):
        s_refs, l_refs = refs[:n], refs[n:2 * n]
        send_sems_, recv_sems_ = refs[2 * n], refs[2 * n + 1]
        for a, k, src, dst, frm in arrivals(s_refs, l_refs):
            cp = pltpu.make_async_remote_copy(src_ref=src, dst_ref=dst, send_sem=send_sems_.at[a * per + k],
                                              recv_sem=recv_sems_.at[a * per + k], device_id=frm, device_id_type=MESH)
            cp.wait_send()
            cp.wait_recv()

    both = list(srcs) + list(lands)
    outs = pl.pallas_call(
        body, name=name, out_shape=tuple(pltpu.HBM(a.shape, a.dtype) for a in both),
        in_specs=[HBM] * (2 * n) + [SEM, SEM, ANY], out_specs=tuple([HBM] * (2 * n)),
        input_output_aliases={i: i for i in range(2 * n)},
        compiler_params=pltpu.CompilerParams(has_side_effects=DATAFLOW),
    )(*both, send_sems, recv_sems, after)
    return list(outs[:n]), list(outs[n:])


def _gather_copies(arriving):
    def copies(s_refs, l_refs, probe=False):
        if probe:
            return [None] * (4 * len(s_refs))
        x, y, c = _place()
        out = []
        for a in range(len(s_refs)):
            for k, (px, py, pc) in enumerate([(x, y, 1 - c), (1 - x, y, c), (x, 1 - y, c), (1 - x, 1 - y, c)]):
                slot = 4 * px + 2 * py + pc if arriving else 4 * x + 2 * y + c
                out.append((a, k, s_refs[a], l_refs[a].at[slot], (px, py, pc)))
        return out
    return copies


def _chip_copies(arriving):
    def copies(s_refs, l_refs, probe=False):
        if probe:
            return [None] * (3 * len(s_refs))
        x, y, c = _place()
        out = []
        for a in range(len(s_refs)):
            for j, (px, py) in enumerate([(1 - x, y), (x, 1 - y), (1 - x, 1 - y)]):
                src = s_refs[a].at[2 * x + y] if arriving else s_refs[a].at[2 * px + py]
                out.append((a, j, src, l_refs[a].at[j], (px, py, c)))
        return out
    return copies


def _gather_begin(name, shards, deps):
    x, y, c = _place()
    lands = [lax.dynamic_update_slice_in_dim(lax.empty((N_DEV,) + a.shape, a.dtype), a[None], 4 * x + 2 * y + c, 0)
             for a in shards]
    return _split_copy_start(name + "_start", shards, lands, _gather_copies(False), deps)


def _gather_end(name, handle, after):
    send_sems, recv_sems, srcs, lands, _ = handle
    _, lands = _split_copy_wait(name + "_wait", send_sems, recv_sems, srcs, lands, _gather_copies(True), after)
    return _gather_forward(name + "_forward", lands)


def _gather_forward(name, lands):
    n = len(lands)

    def body(*refs):
        o_refs = refs[n:2 * n]
        send_sems, recv_sems = refs[2 * n:]
        x, y, c = _place()
        sibling = (x, y, 1 - c)
        chips = [(1 - x, y), (x, 1 - y), (1 - x, 1 - y)]
        sent = []
        for a in range(n):
            for j, (px, py) in enumerate(chips):
                rows = o_refs[a].at[4 * px + 2 * py + c]
                cp = pltpu.make_async_remote_copy(src_ref=rows, dst_ref=rows, send_sem=send_sems.at[a, j],
                                                  recv_sem=recv_sems.at[a, j], device_id=sibling, device_id_type=MESH)
                cp.start()
                sent.append(cp)
        for a in range(n):
            for j, (px, py) in enumerate(chips):
                rows = o_refs[a].at[4 * px + 2 * py + 1 - c]
                pltpu.make_async_remote_copy(src_ref=rows, dst_ref=rows, send_sem=send_sems.at[a, j],
                                             recv_sem=recv_sems.at[a, j], device_id=sibling, device_id_type=MESH).wait_recv()
        for cp in sent:
            cp.wait_send()

    return pl.pallas_call(
        body, in_specs=[ANY] * n, out_specs=[ANY] * n, out_shape=[_sds(a.shape, a.dtype) for a in lands],
        input_output_aliases={i: i for i in range(n)},
        scratch_shapes=[pltpu.SemaphoreType.DMA((n, 3)), pltpu.SemaphoreType.DMA((n, 3))], name=name,
    )(*lands)


def _chips_begin(name, pairs, deps):
    lands = [lax.empty((3,) + a.shape[1:], a.dtype) for a in pairs]
    return _split_copy_start(name + "_start", pairs, lands, _chip_copies(False), deps)


def _chips_end(name, handle, after):
    send_sems, recv_sems, srcs, lands, _ = handle
    return _split_copy_wait(name + "_wait", send_sems, recv_sems, srcs, lands, _chip_copies(True), after)


def _adamw_layer(name, l, own, lands, w, m, v, prev):
    nl, ng, r, c = w.shape
    tr, tc = _elem_tiles(r, c)
    c1 = 1.0 - ADAM_B1 ** ADAM_STEP
    c2 = 1.0 - ADAM_B2 ** ADAM_STEP

    def body(own_ref, lands_ref, w_ref, m_ref, v_ref, *rest):
        g_ref, d_ref, nm_ref, nv_ref = rest[-4:]
        g = own_ref[...].astype(F32) + lands_ref[0].astype(F32) + lands_ref[1].astype(F32) + lands_ref[2].astype(F32)
        nm = ADAM_B1 * m_ref[...] + (1.0 - ADAM_B1) * g
        nv = ADAM_B2 * v_ref[...] + (1.0 - ADAM_B2) * (g * g)
        m_hat = nm / c1
        v_hat = nv / c2
        g_ref[...] = g
        nm_ref[...] = nm
        nv_ref[...] = nv
        d_ref[...] = -ADAM_LR * (m_hat / (jnp.sqrt(v_hat) + ADAM_EPS) + ADAM_WD * w_ref[...])

    lay = pl.BlockSpec((None, None, tr, tc), lambda g, i, j: (l, g, i, j))
    out = _sds((nl, ng, r, c), F32)
    prev = [] if prev is None else list(prev)
    return pl.pallas_call(
        body, grid=(ng, r // tr, c // tc),
        in_specs=[pl.BlockSpec((None, tr, tc), lambda g, i, j: (g, i, j)),
                  pl.BlockSpec((3, None, tr, tc), lambda g, i, j: (0, g, i, j)), lay, lay, lay] + [ANY] * len(prev),
        out_specs=[lay] * 4, out_shape=[out] * 4, input_output_aliases={5 + i: i for i in range(len(prev))},
        compiler_params=_cp(), name=name,
    )(own, lands, w, m, v, *prev)


def _pad_pairs(a, axis, half, half_pad):
    shp = a.shape
    a = a.reshape(shp[:axis] + (2, half) + shp[axis + 1:])
    pad = [(0, 0)] * a.ndim
    pad[axis + 1] = (0, half_pad - half)
    a = jnp.pad(a, pad)
    return a.reshape(shp[:axis] + (2 * half_pad,) + shp[axis + 1:])


def _unpad_pairs(a, axis, half, half_pad):
    shp = a.shape
    a = a.reshape(shp[:axis] + (2, half_pad) + shp[axis + 1:])
    a = lax.slice_in_dim(a, 0, half, axis=axis + 1)
    return a.reshape(shp[:axis] + (2 * half,) + shp[axis + 1:])


def _blockdiag(w, nt):
    g, a, b = w.shape
    gl = g // nt
    e = jnp.eye(gl, dtype=w.dtype).reshape(1, gl, 1, gl, 1)
    return (w.reshape(nt, gl, a, 1, b) * e).reshape(nt, gl * a, gl * b)


def _diagblocks(m, g, a, b):
    nt = m.shape[0]
    gl = g // nt
    d = jnp.diagonal(m.reshape(nt, gl, a, gl, b), axis1=1, axis2=3)
    return jnp.moveaxis(d, -1, 1).reshape(g, a, b)


def _ssm_discretise(lam_re, lam_im, log_dt, b_re, b_im):
    dt = jnp.exp(log_dt)[:, None]
    mag = jnp.exp(lam_re * dt)
    ab_re, ab_im = mag * jnp.cos(lam_im * dt), mag * jnp.sin(lam_im * dt)
    nr, ni = ab_re - 1.0, ab_im
    den = lam_re * lam_re + lam_im * lam_im
    zr = (nr * lam_re + ni * lam_im) / den
    zi = (ni * lam_re - nr * lam_im) / den
    bbr = zr[..., None] * b_re - zi[..., None] * b_im
    bbi = zr[..., None] * b_im + zi[..., None] * b_re
    return ab_re, ab_im, bbr, bbi


def _rope_tables(s):
    half = HEAD_DIM // 2
    inv = ROPE_THETA ** (-jnp.arange(half, dtype=F32) / half)
    ang = jnp.arange(s).astype(F32)[:, None] * inv[None, :]
    cos, sin = jnp.cos(ang), jnp.sin(ang)
    reps = V7X_LANES // HEAD_DIM
    return jnp.tile(jnp.concatenate([cos, cos], -1), (1, reps)), jnp.tile(jnp.concatenate([-sin, sin], -1), (1, reps))


_SMALL = ("attn_sinks", "pool_w", "pool_scale", "ssm_lam_re", "ssm_lam_im", "ssm_log_dt", "ssm_b_re", "ssm_b_im",
          "ssm_c_re", "ssm_c_im", "ssm_d", "ln1_g", "ln1_b", "ffn_conv_b", "ln2_g", "ln2_b")
_BIG = ("w_in", "ssm_glu_w", "w_out", "ffn_w_up", "ffn_w_down")
_ORDER = ("w_in", "attn_sinks", "pool_w", "pool_scale", "ssm_lam_re", "ssm_lam_im", "ssm_log_dt", "ssm_b_re", "ssm_b_im",
          "ssm_c_re", "ssm_c_im", "ssm_d", "ssm_glu_w", "w_out", "ln1_g", "ln1_b", "ffn_w_up", "ffn_conv_w", "ffn_conv_b",
          "ffn_w_down", "ln2_g", "ln2_b")


def kernel(x, w_in, attn_sinks, pool_w, pool_scale, ssm_lam_re, ssm_lam_im, ssm_log_dt, ssm_b_re, ssm_b_im, ssm_c_re, ssm_c_im, ssm_d, ssm_glu_w, w_out, ln1_g, ln1_b, ffn_w_up, ffn_conv_w, ffn_conv_b, ffn_w_down, ln2_g, ln2_b, loss_target, m_w_in, m_attn_sinks, m_pool_w, m_pool_scale, m_ssm_lam_re, m_ssm_lam_im, m_ssm_log_dt, m_ssm_b_re, m_ssm_b_im, m_ssm_c_re, m_ssm_c_im, m_ssm_d, m_ssm_glu_w, m_w_out, m_ln1_g, m_ln1_b, m_ffn_w_up, m_ffn_conv_w, m_ffn_conv_b, m_ffn_w_down, m_ln2_g, m_ln2_b, v_w_in, v_attn_sinks, v_pool_w, v_pool_scale, v_ssm_lam_re, v_ssm_lam_im, v_ssm_log_dt, v_ssm_b_re, v_ssm_b_im, v_ssm_c_re, v_ssm_c_im, v_ssm_d, v_ssm_glu_w, v_w_out, v_ln1_g, v_ln1_b, v_ffn_w_up, v_ffn_conv_w, v_ffn_conv_b, v_ffn_w_down, v_ln2_g, v_ln2_b):
    W = dict(w_in=w_in, attn_sinks=attn_sinks, pool_w=pool_w, pool_scale=pool_scale, ssm_lam_re=ssm_lam_re, ssm_lam_im=ssm_lam_im, ssm_log_dt=ssm_log_dt, ssm_b_re=ssm_b_re, ssm_b_im=ssm_b_im, ssm_c_re=ssm_c_re, ssm_c_im=ssm_c_im, ssm_d=ssm_d, ssm_glu_w=ssm_glu_w, w_out=w_out, ln1_g=ln1_g, ln1_b=ln1_b, ffn_w_up=ffn_w_up, ffn_conv_w=ffn_conv_w, ffn_conv_b=ffn_conv_b, ffn_w_down=ffn_w_down, ln2_g=ln2_g, ln2_b=ln2_b)
    M = dict(w_in=m_w_in, attn_sinks=m_attn_sinks, pool_w=m_pool_w, pool_scale=m_pool_scale, ssm_lam_re=m_ssm_lam_re, ssm_lam_im=m_ssm_lam_im, ssm_log_dt=m_ssm_log_dt, ssm_b_re=m_ssm_b_re, ssm_b_im=m_ssm_b_im, ssm_c_re=m_ssm_c_re, ssm_c_im=m_ssm_c_im, ssm_d=m_ssm_d, ssm_glu_w=m_ssm_glu_w, w_out=m_w_out, ln1_g=m_ln1_g, ln1_b=m_ln1_b, ffn_w_up=m_ffn_w_up, ffn_conv_w=m_ffn_conv_w, ffn_conv_b=m_ffn_conv_b, ffn_w_down=m_ffn_w_down, ln2_g=m_ln2_g, ln2_b=m_ln2_b)
    V = dict(w_in=v_w_in, attn_sinks=v_attn_sinks, pool_w=v_pool_w, pool_scale=v_pool_scale, ssm_lam_re=v_ssm_lam_re, ssm_lam_im=v_ssm_lam_im, ssm_log_dt=v_ssm_log_dt, ssm_b_re=v_ssm_b_re, ssm_b_im=v_ssm_b_im, ssm_c_re=v_ssm_c_re, ssm_c_im=v_ssm_c_im, ssm_d=v_ssm_d, ssm_glu_w=v_ssm_glu_w, w_out=v_w_out, ln1_g=v_ln1_g, ln1_b=v_ln1_b, ffn_w_up=v_ffn_w_up, ffn_conv_w=v_ffn_conv_w, ffn_conv_b=v_ffn_conv_b, ffn_w_down=v_ffn_w_down, ln2_g=v_ln2_g, ln2_b=v_ln2_b)

    depth = w_in.shape[0]
    s, d = x.shape[1], x.shape[2]
    alpha = (2 * depth) ** 0.25
    attn_w = d // 2
    kv_w = attn_w // GQA
    nkv = kv_w // HEAD_DIM
    pool_wd = d // 4
    ssm_wd = d // 4
    n_groups = ssm_wd // SSM_GROUP
    state_w = n_groups * SSM_STATE
    nt_ssm = max(1, state_w // 512)
    o_k, o_v, o_p, o_s = attn_w, attn_w + kv_w, attn_w + 2 * kv_w, attn_w + 2 * kv_w + pool_wd
    in_w = o_s + ssm_wd
    half = ffn_w_down.shape[1]
    half_pad = -(-half // 64) * 64
    ffp = 4 * 2 * half_pad
    xi, yi, ci = _place()
    me = 4 * xi + 2 * yi + ci
    c_idx = jnp.reshape(ci, (1,)).astype(jnp.int32)

    cos_t, sin_t = _rope_tables(s)

    def layer_shards(l):
        return [
            jnp.transpose(w_in[l]).astype(_WIRE), ssm_glu_w[l].astype(_WIRE), w_out[l].astype(_WIRE),
            _pad_pairs(jnp.transpose(ffn_w_up[l]).astype(_WIRE), 0, half, half_pad),
            jnp.pad(ffn_w_down[l].astype(_WIRE), ((0, half_pad - half), (0, 0))),
        ]

    (g_cw,) = _all_gather("gather_conv_w", [_pad_pairs(ffn_conv_w, 2, half, half_pad)])

    def mixer_weights(l, gathered):
        g_in, g_glu, g_out = gathered
        return dict(
            win_t=g_in.reshape(in_w, d),
            glu=jnp.transpose(g_glu, (1, 0, 2)).reshape(ssm_wd, 2 * ssm_wd),
            wout=g_out.reshape(d, d),
            cw=jnp.transpose(g_cw[:, l], (1, 0, 2)).reshape(CONV_WIDTH, 2 * ffp),
            cb=_pad_pairs(ffn_conv_b[l].reshape(N_DEV, 2 * half), 1, half, half_pad).reshape(1, 2 * ffp),
        )

    def ffn_weights(gathered):
        g_up, g_down = gathered
        return dict(wup_t=g_up.reshape(2 * ffp, d), wdown=g_down.reshape(ffp, d))

    shards = [layer_shards(l) for l in range(depth)]
    full = [None] * depth
    handle = _gather_begin("gather_mixer_weights_0", shards[0][:3], [g_cw])
    handle_ffn0 = _gather_begin("gather_ffn_weights_0", shards[0][3:], [handle[-1]])
    full[0] = mixer_weights(0, _gather_end("gather_mixer_weights_0", handle, g_cw))

    def ssm_params(l):
        return (ssm_lam_re[l], ssm_lam_im[l], ssm_log_dt[l], ssm_b_re[l], ssm_b_im[l])

    saved = []
    xf = x[0]
    xb = xf.astype(_MXU)
    for l in range(depth):
        fw = full[l]
        deps = ()
        if l + 1 < depth:
            handle = _gather_begin(f"gather_weights_{l + 1}", shards[l + 1], [fw["win_t"], handle_ffn0[-1]])
            deps = (handle[-1],)
        h = _mm_nt(f"in_proj_{l}", xb, fw["win_t"], deps=deps)
        q_rot, k_rot = _rope(f"rope_{l}", h, o_v, cos_t, sin_t, _MXU, ((0, o_k), (o_k, o_v)))
        k_hm = jnp.transpose(k_rot.reshape(s, nkv, HEAD_DIM), (1, 0, 2))
        v_hm = jnp.transpose(h[:, o_v:o_p].astype(_MXU).reshape(s, nkv, HEAD_DIM), (1, 0, 2))
        sinks = attn_sinks[l].reshape(nkv, GQA)
        o_attn, lse = _attn_fwd(f"attn_{l}", q_rot, k_hm, v_hm, sinks)
        pw_b = pool_w[l].astype(_MXU)
        psc = pool_scale[l].reshape(1, pool_wd)
        y_pool, pre = _pool_fwd(f"pool_{l}", h, o_p // pool_wd, pw_b, psc)
        ab_re, ab_im, bbr, bbi = _ssm_discretise(*ssm_params(l))
        bdr = _blockdiag(jnp.transpose(bbr, (0, 2, 1)), nt_ssm).astype(_MXU)
        bdi = _blockdiag(jnp.transpose(bbi, (0, 2, 1)), nt_ssm).astype(_MXU)
        cdr = _blockdiag(jnp.transpose(ssm_c_re[l], (0, 2, 1)), nt_ssm).astype(_MXU)
        cdi = _blockdiag(jnp.transpose(ssm_c_im[l], (0, 2, 1)), nt_ssm).astype(_MXU)
        dvec = ssm_d[l].reshape(1, ssm_wd)
        ar, ai = ab_re.reshape(1, state_w), ab_im.reshape(1, state_w)
        cw_ssm = ssm_wd // nt_ssm
        sr, si, ypre, yg = _ssm_fwd(f"ssm_{l}", h, o_s // cw_ssm, bdr, bdi, cdr, cdi, dvec, ar, ai)
        y_ssm, ab2 = _glu_fwd(f"glu_{l}", yg, fw["glu"])
        mix = jnp.concatenate([o_attn.astype(_MXU), y_pool, y_ssm], -1)
        a1 = _mm_nn(f"out_proj_{l}", mix, fw["wout"])
        g1, b1 = ln1_g[l].reshape(1, d), ln1_b[l].reshape(1, d)
        x1, x1b, xh1, rs1 = _ln_fwd(f"ln1_{l}", xf, a1, g1, b1, alpha)
        if l == 0:
            fw.update(ffn_weights(_gather_end("gather_ffn_weights_0", handle_ffn0, x1b)))
        hu = _mm_nt(f"ffn_up_{l}", x1b, fw["wup_t"], cap=2 * half_pad)
        act = _conv_act_fwd(f"ffn_act_{l}", hu, fw["cw"], fw["cb"])
        tkd = 2 * half_pad
        tnd = _col_tile(d, 1024)
        f_out = _mm(f"ffn_down_{l}", act, fw["wdown"], NN, (d // tnd, ffp // tkd),
                    pl.BlockSpec((s, tkd), lambda j, kk: (0, kk)), pl.BlockSpec((tkd, tnd), lambda j, kk: (kk, j)),
                    pl.BlockSpec((s, tnd), lambda j, kk: (0, j)), (s, d), F32)
        g2, b2 = ln2_g[l].reshape(1, d), ln2_b[l].reshape(1, d)
        x2, x2b, xh2, rs2 = _ln_fwd(f"ln2_{l}", x1, f_out, g2, b2, alpha)
        saved.append(dict(xb=xb, h=h, q_rot=q_rot, k_hm=k_hm, v_hm=v_hm, sinks=sinks, o_attn=o_attn, lse=lse, pw_b=pw_b, psc=psc,
                          pre=pre, bdr=bdr, bdi=bdi, cdr=cdr, cdi=cdi, dvec=dvec, ar=ar, ai=ai, sr=sr, si=si, ypre=ypre, yg=yg,
                          ab2=ab2, mix=mix, g1=g1, xh1=xh1, rs1=rs1, x1b=x1b, hu=hu, act=act, g2=g2, xh2=xh2, rs2=rs2))
        xf, xb = x2, x2b
        if l + 1 < depth:
            gathered = _gather_end(f"gather_weights_{l + 1}", handle, x2b)
            full[l + 1] = {**mixer_weights(l + 1, gathered[:3]), **ffn_weights(gathered[3:])}

    dy, loss_part = _loss_head("loss_head", xf, loss_target[0])
    loss = lax.psum(loss_part[0, 0], ("x", "y", "c"))

    small_g = {k: [None] * depth for k in _SMALL}
    cw_g = [None] * depth
    outs = {}
    big_res = {k: None for k in _BIG}
    my_chip = 2 * xi + yi
    pending = []

    transposed = ("w_in", "ffn_w_up")

    def row_groups(name_, t):
        g = 2 if name_ == "ffn_w_up" else 1
        return t.reshape(t.shape[:-2] + (g, t.shape[-2] // g, t.shape[-1]))

    def as_groups(name_, t):
        return row_groups(name_, jnp.transpose(t, (0, 2, 1)) if name_ in transposed else t)

    def from_groups(name_, t):
        t = t.reshape(t.shape[0], t.shape[1] * t.shape[2], t.shape[3])
        return jnp.transpose(t, (0, 2, 1)) if name_ in transposed else t

    grouped = {name_: tuple(as_groups(name_, t[name_]) for t in (W, M, V)) for name_ in _BIG}

    def finish_exchanges(after):
        while pending:
            lay, part, names, handle = pending.pop(0)
            pairs, lands = _chips_end(f"grads_between_chips_{part}_{lay}", handle, after)
            for name_, p, ld in zip(names, pairs, lands):
                own = row_groups(name_, lax.dynamic_index_in_dim(p, my_chip, 0, keepdims=False))
                big_res[name_] = _adamw_layer(f"adamw_{name_}_{lay}", lay, own, row_groups(name_, ld), *grouped[name_],
                                              big_res[name_])

    def begin_exchange(lay, part, names, grads, after):
        by_owner = [a.reshape((4, 2) + a.shape[1:]) for a in grads]
        theirs = _swap_sibling(f"grads_to_sibling_{part}_{lay}", by_owner)
        pair = [_pair_sum(f"pair_sum_{name_}_{lay}", a, b, c_idx) for name_, a, b in zip(names, by_owner, theirs)]
        finish_exchanges(after)
        handle = _chips_begin(f"grads_between_chips_{part}_{lay}", pair, [])
        pending.append((lay, part, names, handle))
        return handle[-1]

    token = None
    small_handle = None
    for l in reversed(range(depth)):
        fw, sv = full[l], saved[l]
        deps = () if token is None else (token,)
        dr2, dr2b, dg2, db2 = _ln_bwd(f"ln2_bwd_{l}", dy, sv["xh2"], sv["rs2"], sv["g2"], deps=deps)
        d_wdown = _mm_tn_acols(f"ffn_down_dw_{l}", sv["act"], dr2b, _WIRE, cap=2 * half_pad)
        dact = _mm_nt(f"ffn_down_dx_{l}", dr2b, fw["wdown"], cap=2 * half_pad)
        dhu, dcw, dcb = _conv_act_bwd(f"ffn_act_bwd_{l}", dact, sv["hu"], fw["cw"], fw["cb"])
        d_wup = _mm(f"ffn_up_dw_{l}", dhu, sv["x1b"], TN, (N_DEV, 1),
                    pl.BlockSpec((None, s, 2 * half_pad), lambda j, kk: (j // 4, 0, j % 4)),
                    _resident((s, d), lambda j, kk: (0, 0)),
                    pl.BlockSpec((2 * half_pad, d), lambda j, kk: (j, 0)), (2 * ffp, d), _WIRE)
        tnd = _col_tile(d, 512)
        dy1 = _mm(f"ffn_up_dx_{l}", dhu, fw["wup_t"], NN, (d // tnd, N_DEV),
                  pl.BlockSpec((None, s, 2 * half_pad), lambda j, kk: (kk // 4, 0, kk % 4)),
                  pl.BlockSpec((2 * half_pad, tnd), lambda j, kk: (kk, j)),
                  pl.BlockSpec((s, tnd), lambda j, kk: (0, j)), (s, d), F32,
                  add=dr2, add_spec=pl.BlockSpec((s, tnd), lambda j, kk: (0, j)), add_scale=alpha)
        token = begin_exchange(l, "ffn", ("ffn_w_up", "ffn_w_down"),
                               [d_wup.reshape(N_DEV, 2 * half_pad, d), d_wdown.reshape(N_DEV, half_pad, d)], dy1)
        dr1, dr1b, dg1, db1 = _ln_bwd(f"ln1_bwd_{l}", dy1, sv["xh1"], sv["rs1"], sv["g1"], deps=(token,))
        d_wout = _mm_tn_acols(f"out_proj_dw_{l}", sv["mix"], dr1b, _WIRE, cap=d // N_DEV)
        dmix = _mm_nt(f"out_proj_dx_{l}", dr1b, fw["wout"])
        dq_rot, dk_hm, dv_hm, dsk = _attn_bwd(f"attn_bwd_{l}", sv["q_rot"], sv["k_hm"], sv["v_hm"], sv["o_attn"], dmix,
                                             sv["lse"], sv["sinks"])
        dqk = jnp.concatenate([dq_rot, jnp.transpose(dk_hm, (1, 0, 2)).reshape(s, kv_w)], -1)
        dhq, dhk = _rope(f"rope_bwd_{l}", dqk, o_v, cos_t, -sin_t, _MXU, ((0, o_k), (o_k, o_v)))
        dhv = jnp.transpose(dv_hm, (1, 0, 2)).reshape(s, kv_w).astype(_MXU)
        dhp, dpw, dpsc = _pool_bwd(f"pool_bwd_{l}", dmix, attn_w // pool_wd, sv["pre"], sv["pw_b"], sv["psc"])
        dab2, dyg = _glu_bwd(f"glu_bwd_{l}", dmix, (attn_w + pool_wd) // ssm_wd, sv["ab2"], fw["glu"])
        d_glu = _mm_tn_bcols(f"glu_dw_{l}", sv["yg"], dab2, _WIRE)
        cw_ssm = ssm_wd // nt_ssm
        dhs, dd, dcdr, dcdi, dbdr, dbdi, dar, dai = _ssm_bwd(
            f"ssm_bwd_{l}", dyg, sv["ypre"], sv["h"], o_s // cw_ssm, sv["sr"], sv["si"], sv["bdr"], sv["bdi"], sv["cdr"],
            sv["cdi"], sv["dvec"], sv["ar"], sv["ai"])
        dh = jnp.concatenate([dhq, dhk, dhv, dhp, dhs], -1)
        d_win = _mm_tn_acols(f"in_proj_dw_{l}", dh, sv["xb"], _WIRE)
        dy = _mm_nn(f"in_proj_dx_{l}", dh, fw["win_t"], add=dr1, add_scale=alpha)

        dbbr = jnp.transpose(_diagblocks(dbdr, n_groups, SSM_GROUP, SSM_STATE), (0, 2, 1))
        dbbi = jnp.transpose(_diagblocks(dbdi, n_groups, SSM_GROUP, SSM_STATE), (0, 2, 1))
        _, vjp = jax.vjp(_ssm_discretise, *ssm_params(l))
        dlr, dli, dldt, dbr, dbi = vjp((dar.reshape(n_groups, SSM_STATE), dai.reshape(n_groups, SSM_STATE), dbbr, dbbi))
        small_g["attn_sinks"][l] = dsk.reshape(-1)
        small_g["pool_w"][l] = dpw
        small_g["pool_scale"][l] = dpsc.reshape(-1)
        small_g["ssm_lam_re"][l], small_g["ssm_lam_im"][l], small_g["ssm_log_dt"][l] = dlr, dli, dldt
        small_g["ssm_b_re"][l], small_g["ssm_b_im"][l] = dbr, dbi
        small_g["ssm_c_re"][l] = jnp.transpose(_diagblocks(dcdr, n_groups, SSM_STATE, SSM_GROUP), (0, 2, 1))
        small_g["ssm_c_im"][l] = jnp.transpose(_diagblocks(dcdi, n_groups, SSM_STATE, SSM_GROUP), (0, 2, 1))
        small_g["ssm_d"][l] = dd.reshape(n_groups, SSM_GROUP)
        small_g["ln1_g"][l], small_g["ln1_b"][l] = dg1.reshape(-1), db1.reshape(-1)
        small_g["ln2_g"][l], small_g["ln2_b"][l] = dg2.reshape(-1), db2.reshape(-1)
        small_g["ffn_conv_b"][l] = _unpad_pairs(dcb.reshape(N_DEV, 2 * half_pad), 1, half, half_pad).reshape(-1)
        cw_g[l] = _unpad_pairs(dcw.reshape(CONV_WIDTH, N_DEV, 2 * half_pad), 2, half, half_pad)

        if l == 0:
            flat_parts = [jnp.stack(small_g[k]).reshape(-1) for k in _SMALL] + [jnp.stack(cw_g).reshape(-1)]
            sizes = [a.shape[0] for a in flat_parts]
            total = sum(sizes)
            rows = -(-total // (512 * V7X_LANES)) * 512
            flat = jnp.pad(jnp.concatenate(flat_parts), (0, rows * V7X_LANES - total)).reshape(rows, V7X_LANES)
            small_handle = _gather_begin("gather_small_grads", [flat], [dy])
        token = begin_exchange(l, "mixer", ("w_in", "ssm_glu_w", "w_out"),
                               [d_win.reshape(N_DEV, in_w // N_DEV, d),
                                jnp.transpose(d_glu.reshape(ssm_wd, N_DEV, 2 * ssm_wd // N_DEV), (1, 0, 2)),
                                d_wout.reshape(N_DEV, d // N_DEV, d)], dy)

    (gathered,) = _gather_end("gather_small_grads", small_handle, token)
    gathered = gathered.reshape(N_DEV, -1)
    n_rep = sum(sizes[:-1])
    cw_all = gathered[:, n_rep:total].reshape(N_DEV, depth, CONV_WIDTH, N_DEV, 2 * half)
    cw_mine = lax.dynamic_index_in_dim(cw_all, me, axis=3, keepdims=False)
    n_small = n_rep + cw_mine[0].size
    rows2 = -(-n_small // (512 * V7X_LANES)) * 512

    def flatten(parts_):
        return jnp.pad(jnp.concatenate(parts_, -1), [(0, 0)] * (parts_[0].ndim - 1) + [(0, rows2 * V7X_LANES - n_small)])

    p_small = flatten([gathered[:, :n_rep], cw_mine.reshape(N_DEV, -1)]).reshape(N_DEV, 1, rows2, V7X_LANES)
    w_small, m_small, v_small = (
        flatten([jnp.concatenate([t[k].reshape(-1) for k in _SMALL]), t["ffn_conv_w"].reshape(-1)]).reshape(1, rows2, V7X_LANES)
        for t in (W, M, V))
    res_small = [a.reshape(-1) for a in _reduce_adamw("adamw_small", p_small, w_small, m_small, v_small)]
    off = 0
    for k in _SMALL + ("ffn_conv_w",):
        n = W[k].size
        outs[k] = tuple(a[off:off + n].reshape(W[k].shape) for a in res_small)
        off += n

    finish_exchanges(res_small[0])
    for name_ in _BIG:
        outs[name_] = tuple(from_groups(name_, t) for t in big_res[name_])

    grad_x = dy[None]
    result = [loss, grad_x]
    for i in range(4):
        result += [outs[k][i] for k in _ORDER]
    return tuple(result)
```

```python
import functools
import math

import jax
import jax.numpy as jnp
from jax import lax
from jax.experimental import pallas as pl
from jax.experimental.pallas import tpu as pltpu

F32 = jnp.float32
_MXU = jnp.bfloat16
_WIRE = jnp.bfloat16

HEAD_DIM = 64
GQA = 4
ATTN_BLOCK = 128
ROPE_THETA = 10000.0
POOL_WINDOWS = (2, 4, 8, 16)
SSM_GROUP = 16
SSM_STATE = 64
CONV_WIDTH = 3
LN_EPS = 1e-5
ADAM_LR, ADAM_B1, ADAM_B2, ADAM_EPS, ADAM_WD, ADAM_STEP = 0.001, 0.9, 0.999, 1e-08, 0.01, 10

N_DEV = 8
V7X_LANES = 128
V7X_VMEM_LIMIT = 56 * 1024 * 1024
SCAN_T = 64
SCAN_LANES = 256
MESH = pl.DeviceIdType.MESH
ANY = pl.BlockSpec(memory_space=pl.ANY)


def _cp():
    return pltpu.CompilerParams(vmem_limit_bytes=V7X_VMEM_LIMIT)


def _resident(block, index_map):
    return pl.BlockSpec(block, index_map, pipeline_mode=pl.Buffered(1))


def _sds(shape, dtype):
    return jax.ShapeDtypeStruct(tuple(shape), dtype)


def _mm(name, a, b, dims, grid, a_spec, b_spec, o_spec, out_shape, out_dtype, add=None, add_spec=None, add_scale=1.0, deps=()):
    nk = grid[1]
    n_in = 2 + (add is not None) + len(deps)
    oblk = tuple(d for d in o_spec.block_shape if d is not None)

    def body(*refs):
        a_ref, b_ref = refs[:2]
        add_ref = None if add is None else refs[2]
        o_ref = refs[n_in]
        acc_ref = refs[-1] if nk > 1 else None

        def finish(r):
            if add_ref is not None:
                r = r + add_scale * add_ref[...]
            o_ref[...] = r.astype(o_ref.dtype)

        part = lax.dot_general(a_ref[...], b_ref[...], (dims, ((), ())), preferred_element_type=F32)
        if nk == 1:
            finish(part)
        else:
            k = pl.program_id(1)

            @pl.when(k == 0)
            def _():
                acc_ref[...] = part

            @pl.when(k > 0)
            def _():
                acc_ref[...] += part

            @pl.when(k == nk - 1)
            def _():
                finish(acc_ref[...])

    ins = [a, b] + ([] if add is None else [add]) + list(deps)
    in_specs = [a_spec, b_spec] + ([] if add is None else [add_spec]) + [ANY] * len(deps)
    return pl.pallas_call(
        body, grid=grid, in_specs=in_specs, out_specs=o_spec, out_shape=_sds(out_shape, out_dtype),
        scratch_shapes=[pltpu.VMEM(oblk, F32)] if nk > 1 else [], compiler_params=_cp(), name=name,
    )(*ins)


NN = ((1,), (0,))
NT = ((1,), (1,))
TN = ((0,), (0,))


def _col_tile(n, cap=512):
    if n % V7X_LANES:
        return n
    t = min(cap, n)
    t -= t % V7X_LANES
    while n % t:
        t -= V7X_LANES
    return t


def _mm_nn(name, a, b, out_dtype=F32, cap=512, add=None, add_scale=1.0):
    m, k = a.shape
    n = b.shape[1]
    tn = _col_tile(n, cap)
    o_spec = pl.BlockSpec((m, tn), lambda j, kk: (0, j))
    return _mm(name, a, b, NN, (n // tn, 1), _resident((m, k), lambda j, kk: (0, 0)),
               pl.BlockSpec((k, tn), lambda j, kk: (0, j)), o_spec, (m, n), out_dtype,
               add=add, add_spec=None if add is None else o_spec, add_scale=add_scale)


def _mm_nt(name, a, b, out_dtype=F32, add=None, add_scale=1.0, cap=512, deps=()):
    m, k = a.shape
    n = b.shape[0]
    tn = _col_tile(n, cap)
    o_spec = pl.BlockSpec((m, tn), lambda j, kk: (0, j))
    return _mm(name, a, b, NT, (n // tn, 1), _resident((m, k), lambda j, kk: (0, 0)),
               pl.BlockSpec((tn, k), lambda j, kk: (j, 0)), o_spec, (m, n), out_dtype,
               add=add, add_spec=None if add is None else o_spec, add_scale=add_scale, deps=deps)


def _mm_tn_bcols(name, a, b, out_dtype, cap=512):
    s, k = a.shape
    n = b.shape[1]
    tn = _col_tile(n, cap)
    return _mm(name, a, b, TN, (n // tn, 1), _resident((s, k), lambda j, kk: (0, 0)),
               pl.BlockSpec((s, tn), lambda j, kk: (0, j)), pl.BlockSpec((k, tn), lambda j, kk: (0, j)), (k, n), out_dtype)


def _mm_tn_acols(name, a, b, out_dtype, cap=512):
    s, k = a.shape
    n = b.shape[1]
    tk = _col_tile(k, cap)
    return _mm(name, a, b, TN, (k // tk, 1), pl.BlockSpec((s, tk), lambda i, kk: (0, i)),
               _resident((s, n), lambda i, kk: (0, 0)), pl.BlockSpec((tk, n), lambda i, kk: (i, 0)), (k, n), out_dtype)


def _ln_fwd(name, x, a, g, b, alpha):
    s, d = x.shape
    tr = min(256, s)

    def body(x_ref, a_ref, g_ref, b_ref, y_ref, yb_ref, xh_ref, rs_ref):
        r = alpha * x_ref[...] + a_ref[...]
        mu = jnp.mean(r, -1, keepdims=True)
        c = r - mu
        var = jnp.mean(c * c, -1, keepdims=True)
        rstd = lax.rsqrt(var + LN_EPS)
        xh = c * rstd
        y = xh * g_ref[...] + b_ref[...]
        y_ref[...] = y
        yb_ref[...] = y.astype(_MXU)
        xh_ref[...] = xh
        rs_ref[...] = rstd

    row = pl.BlockSpec((tr, d), lambda i: (i, 0))
    vec = pl.BlockSpec((1, d), lambda i: (0, 0))
    return pl.pallas_call(
        body, grid=(s // tr,), in_specs=[row, row, vec, vec],
        out_specs=[row, row, row, pl.BlockSpec((tr, 1), lambda i: (i, 0))],
        out_shape=[_sds((s, d), F32), _sds((s, d), _MXU), _sds((s, d), F32), _sds((s, 1), F32)],
        compiler_params=_cp(), name=name,
    )(x, a, g, b)


def _ln_bwd(name, dy, xh, rstd, g, deps=()):
    s, d = dy.shape
    tr = min(256, s)
    nd = len(deps)

    def body(dy_ref, xh_ref, rs_ref, g_ref, *rest):
        dr_ref, drb_ref, dg_ref, db_ref = rest[nd:]
        i = pl.program_id(0)
        dy_ = dy_ref[...]
        xh_ = xh_ref[...]
        dxh = dy_ * g_ref[...]
        m1 = jnp.mean(dxh, -1, keepdims=True)
        m2 = jnp.mean(dxh * xh_, -1, keepdims=True)
        dr = rs_ref[...] * (dxh - m1 - xh_ * m2)
        dr_ref[...] = dr
        drb_ref[...] = dr.astype(_MXU)
        pg = jnp.sum(dy_ * xh_, 0, keepdims=True)
        pb = jnp.sum(dy_, 0, keepdims=True)

        @pl.when(i == 0)
        def _():
            dg_ref[...] = pg
            db_ref[...] = pb

        @pl.when(i > 0)
        def _():
            dg_ref[...] += pg
            db_ref[...] += pb

    row = pl.BlockSpec((tr, d), lambda i: (i, 0))
    vec = pl.BlockSpec((1, d), lambda i: (0, 0))
    return pl.pallas_call(
        body, grid=(s // tr,), in_specs=[row, row, pl.BlockSpec((tr, 1), lambda i: (i, 0)), vec] + [ANY] * nd,
        out_specs=[row, row, vec, vec],
        out_shape=[_sds((s, d), F32), _sds((s, d), _MXU), _sds((1, d), F32), _sds((1, d), F32)],
        compiler_params=_cp(), name=name,
    )(dy, xh, rstd, g, *deps)


def _loss_head(name, y, target):
    s, d = y.shape
    tr = min(256, s)

    def body(y_ref, t_ref, dy_ref, l_ref):
        i = pl.program_id(0)
        e = y_ref[...] - t_ref[...]
        dy_ref[...] = e * (1.0 / d)
        part = 0.5 * jnp.sum(jnp.mean(e * e, -1, keepdims=True), 0, keepdims=True)

        @pl.when(i == 0)
        def _():
            l_ref[...] = part

        @pl.when(i > 0)
        def _():
            l_ref[...] += part

    row = pl.BlockSpec((tr, d), lambda i: (i, 0))
    return pl.pallas_call(
        body, grid=(s // tr,), in_specs=[row, row], out_specs=[row, pl.BlockSpec((1, 1), lambda i: (0, 0))],
        out_shape=[_sds((s, d), F32), _sds((1, 1), F32)], compiler_params=_cp(), name=name,
    )(y, target)


def _rope(name, t, width, cos, sin, out_dtype, splits):
    s = t.shape[0]
    tr = min(256, s)
    assert width % V7X_LANES == 0

    def body(t_ref, c_ref, s_ref, *o_refs):
        lane = lax.broadcasted_iota(jnp.int32, (tr, V7X_LANES), 1)
        first = (lane % HEAD_DIM) < (HEAD_DIM // 2)
        cs, sn = c_ref[...], s_ref[...]
        for (lo, hi), o_ref in zip(splits, o_refs):
            for c0 in range(lo, hi, V7X_LANES):
                v = t_ref[:, c0:c0 + V7X_LANES].astype(F32)
                partner = jnp.where(first, pltpu.roll(v, V7X_LANES - HEAD_DIM // 2, 1), pltpu.roll(v, HEAD_DIM // 2, 1))
                o_ref[:, c0 - lo:c0 - lo + V7X_LANES] = (v * cs + partner * sn).astype(o_ref.dtype)

    tab = pl.BlockSpec((tr, V7X_LANES), lambda i: (i, 0))
    return pl.pallas_call(
        body, grid=(s // tr,), in_specs=[pl.BlockSpec((tr, width), lambda i: (i, 0)), tab, tab],
        out_specs=[pl.BlockSpec((tr, hi - lo), lambda i: (i, 0)) for lo, hi in splits],
        out_shape=[_sds((s, hi - lo), out_dtype) for lo, hi in splits], compiler_params=_cp(), name=name,
    )(t, cos, sin)


def _attn_masks():
    i = lax.broadcasted_iota(jnp.int32, (GQA * ATTN_BLOCK, 2 * ATTN_BLOCK), 0) % ATTN_BLOCK
    j = lax.broadcasted_iota(jnp.int32, (GQA * ATTN_BLOCK, 2 * ATTN_BLOCK), 1)
    cur_ok = jnp.logical_and(j >= ATTN_BLOCK, j - ATTN_BLOCK <= i)
    prev_ok = jnp.logical_and(j < ATTN_BLOCK, j > i)
    return cur_ok, prev_ok


def _attn_scores(q4, kcat, n, cur_ok, prev_ok):
    sc = lax.dot_general(q4, kcat, (NT, ((), ())), preferred_element_type=F32) * (HEAD_DIM ** -0.5)
    return jnp.where(jnp.logical_or(cur_ok, jnp.logical_and(prev_ok, n > 0)), sc, -1e30)


def _stack_heads(ref, rows):
    return jnp.concatenate([ref[rows, g * HEAD_DIM:(g + 1) * HEAD_DIM] for g in range(GQA)], 0)


def _per_head_column(values):
    r = lax.broadcasted_iota(jnp.int32, (GQA * ATTN_BLOCK, 1), 0) // ATTN_BLOCK
    col = jnp.zeros((GQA * ATTN_BLOCK, 1), F32)
    for g, val in enumerate(values):
        col = jnp.where(r == g, val, col)
    return col


def _attn_fwd(name, q, k, v, sinks):
    s = q.shape[0]
    nkv = k.shape[0]
    gw = GQA * HEAD_DIM
    nb = s // ATTN_BLOCK

    def body(sk_ref, q_ref, k_ref, v_ref, o_ref, lse_ref):
        h = pl.program_id(0)
        cur_ok, prev_ok = _attn_masks()
        sink = _per_head_column([sk_ref[h, g] for g in range(GQA)])

        def blk(n, carry):
            rows = pl.ds(pl.multiple_of(n * ATTN_BLOCK, ATTN_BLOCK), ATTN_BLOCK)
            prows = pl.ds(pl.multiple_of(jnp.maximum(n - 1, 0) * ATTN_BLOCK, ATTN_BLOCK), ATTN_BLOCK)
            kcat = jnp.concatenate([k_ref[prows, :], k_ref[rows, :]], 0)
            vcat = jnp.concatenate([v_ref[prows, :], v_ref[rows, :]], 0)
            sc = _attn_scores(_stack_heads(q_ref, rows), kcat, n, cur_ok, prev_ok)
            m = jnp.maximum(sc.max(-1, keepdims=True), sink)
            p = jnp.exp(sc - m)
            den = p.sum(-1, keepdims=True) + jnp.exp(sink - m)
            o = jnp.dot((p / den).astype(_MXU), vcat, preferred_element_type=F32)
            lse = m + jnp.log(den)
            for g in range(GQA):
                mine = slice(g * ATTN_BLOCK, (g + 1) * ATTN_BLOCK)
                o_ref[rows, g * HEAD_DIM:(g + 1) * HEAD_DIM] = o[mine, :]
                lse_ref[rows, g:g + 1] = lse[mine, :]
            return carry

        lax.fori_loop(0, nb, blk, 0)

    kv_spec = pl.BlockSpec((None, s, HEAD_DIM), lambda h: (h, 0, 0))
    return pl.pallas_call(
        body, grid=(nkv,),
        in_specs=[pl.BlockSpec(memory_space=pltpu.SMEM), pl.BlockSpec((s, gw), lambda h: (0, h)), kv_spec, kv_spec],
        out_specs=[pl.BlockSpec((s, gw), lambda h: (0, h)), pl.BlockSpec((None, s, GQA), lambda h: (h, 0, 0))],
        out_shape=[_sds((s, nkv * gw), F32), _sds((nkv, s, GQA), F32)], compiler_params=_cp(), name=name,
    )(sinks, q, k, v)


def _attn_bwd(name, q, k, v, o, dmix, lse, sinks):
    s = q.shape[0]
    nkv = k.shape[0]
    gw = GQA * HEAD_DIM
    nb = s // ATTN_BLOCK
    scale = HEAD_DIM ** -0.5

    def body(sk_ref, q_ref, k_ref, v_ref, o_ref, do_ref, lse_ref, dq_ref, dk_ref, dv_ref, dsk_ref):
        h = pl.program_id(0)
        cur_ok, prev_ok = _attn_masks()
        dk_ref[...] = jnp.zeros_like(dk_ref)
        dv_ref[...] = jnp.zeros_like(dv_ref)

        sink = _per_head_column([sk_ref[h, g] for g in range(GQA)])

        def blk(n, acc):
            rows = pl.ds(pl.multiple_of(n * ATTN_BLOCK, ATTN_BLOCK), ATTN_BLOCK)
            prows = pl.ds(pl.multiple_of(jnp.maximum(n - 1, 0) * ATTN_BLOCK, ATTN_BLOCK), ATTN_BLOCK)
            kcat = jnp.concatenate([k_ref[prows, :], k_ref[rows, :]], 0)
            vcat = jnp.concatenate([v_ref[prows, :], v_ref[rows, :]], 0)
            q4 = _stack_heads(q_ref, rows)
            do4 = _stack_heads(do_ref, rows)
            delta = jnp.sum(do4 * _stack_heads(o_ref, rows), -1, keepdims=True)
            dob = do4.astype(_MXU)
            lse = jnp.concatenate([lse_ref[rows, g:g + 1] for g in range(GQA)], 0)
            p = jnp.exp(_attn_scores(q4, kcat, n, cur_ok, prev_ok) - lse)
            dp = lax.dot_general(dob, vcat, (NT, ((), ())), preferred_element_type=F32)
            ds = (p * (dp - delta) * scale).astype(_MXU)
            dq = jnp.dot(ds, kcat, preferred_element_type=F32)
            for g in range(GQA):
                dq_ref[rows, g * HEAD_DIM:(g + 1) * HEAD_DIM] = dq[g * ATTN_BLOCK:(g + 1) * ATTN_BLOCK, :]
            dk = lax.dot_general(ds, q4, (TN, ((), ())), preferred_element_type=F32)
            dv = lax.dot_general(p.astype(_MXU), dob, (TN, ((), ())), preferred_element_type=F32)
            dk_ref[prows, :] += dk[:ATTN_BLOCK, :]
            dv_ref[prows, :] += dv[:ATTN_BLOCK, :]
            dk_ref[rows, :] += dk[ATTN_BLOCK:, :]
            dv_ref[rows, :] += dv[ATTN_BLOCK:, :]
            return acc - jnp.exp(sink - lse) * delta

        acc = lax.fori_loop(0, nb, blk, jnp.zeros((GQA * ATTN_BLOCK, 1), F32))
        for g in range(GQA):
            dsk_ref[:, g:g + 1] = jnp.sum(acc[g * ATTN_BLOCK:(g + 1) * ATTN_BLOCK, :], 0, keepdims=True)

    kv_spec = pl.BlockSpec((None, s, HEAD_DIM), lambda h: (h, 0, 0))
    qcols = pl.BlockSpec((s, gw), lambda h: (0, h))
    return pl.pallas_call(
        body, grid=(nkv,),
        in_specs=[pl.BlockSpec(memory_space=pltpu.SMEM), qcols, kv_spec, kv_spec, qcols, qcols,
                  pl.BlockSpec((None, s, GQA), lambda h: (h, 0, 0))],
        out_specs=[qcols, kv_spec, kv_spec, pl.BlockSpec((None, 1, GQA), lambda h: (h, 0, 0))],
        out_shape=[_sds((s, nkv * gw), F32), _sds((nkv, s, HEAD_DIM), F32), _sds((nkv, s, HEAD_DIM), F32),
                   _sds((nkv, 1, GQA), F32)],
        compiler_params=_cp(), name=name,
    )(sinks, q, k, v, o, dmix, lse)


def _shift_down(a, k, t):
    return jnp.where(t >= k, pltpu.roll(a, k, 0), 0.0)


def _shift_up(a, k, t):
    n = a.shape[0]
    return jnp.where(t < n - k, pltpu.roll(a, n - k, 0), 0.0)


def _pool_fwd(name, h, col_block, pool_w, pool_scale):
    s = h.shape[0]
    ng, pg = pool_w.shape[0], pool_w.shape[1]
    pw_ = ng * pg

    def body(u_ref, w_ref, sc_ref, y_ref, pre_ref):
        t = lax.broadcasted_iota(jnp.int32, (s, pg), 0)
        for gi, win in enumerate(POOL_WINDOWS):
            cols = slice(gi * pg, (gi + 1) * pg)
            u = u_ref[:, cols]
            a = u
            k = 1
            while k < win:
                a = a + _shift_down(a, k, t)
                k *= 2
            div = jnp.minimum(t + 1, win).astype(F32)
            pre = (a / div - u).astype(_MXU)
            pre_ref[:, cols] = pre
            out = jnp.dot(pre, w_ref[gi], preferred_element_type=F32)
            y_ref[:, cols] = (out * sc_ref[:, cols]).astype(y_ref.dtype)

    blk = pl.BlockSpec((s, pw_), lambda i: (0, 0))
    return pl.pallas_call(
        body, grid=(1,),
        in_specs=[pl.BlockSpec((s, pw_), lambda i: (0, col_block)), pl.BlockSpec((ng, pg, pg), lambda i: (0, 0, 0)),
                  pl.BlockSpec((1, pw_), lambda i: (0, 0))],
        out_specs=[blk, blk], out_shape=[_sds((s, pw_), _MXU), _sds((s, pw_), _MXU)], compiler_params=_cp(), name=name,
    )(h, pool_w, pool_scale)


def _pool_bwd(name, dmix, col_block, pre, pool_w, pool_scale):
    s = pre.shape[0]
    ng, pg = pool_w.shape[0], pool_w.shape[1]
    pw_ = ng * pg

    def body(dy_ref, pre_ref, w_ref, sc_ref, du_ref, dw_ref, dsc_ref):
        t = lax.broadcasted_iota(jnp.int32, (s, pg), 0)
        for gi, win in enumerate(POOL_WINDOWS):
            cols = slice(gi * pg, (gi + 1) * pg)
            pre_g = pre_ref[:, cols]
            dy = dy_ref[:, cols]
            out = jnp.dot(pre_g, w_ref[gi], preferred_element_type=F32)
            dsc_ref[:, cols] = jnp.sum(dy * out, 0, keepdims=True)
            dout = (dy * sc_ref[:, cols]).astype(_MXU)
            dw_ref[gi] = lax.dot_general(pre_g, dout, (TN, ((), ())), preferred_element_type=F32)
            dpre = lax.dot_general(dout, w_ref[gi], (NT, ((), ())), preferred_element_type=F32)
            div = jnp.minimum(t + 1, win).astype(F32)
            a = dpre / div
            k = 1
            while k < win:
                a = a + _shift_up(a, k, t)
                k *= 2
            du_ref[:, cols] = (a - dpre).astype(du_ref.dtype)

    blk = pl.BlockSpec((s, pw_), lambda i: (0, 0))
    wspec = pl.BlockSpec((ng, pg, pg), lambda i: (0, 0, 0))
    vec = pl.BlockSpec((1, pw_), lambda i: (0, 0))
    return pl.pallas_call(
        body, grid=(1,), in_specs=[pl.BlockSpec((s, pw_), lambda i: (0, col_block)), blk, wspec, vec],
        out_specs=[blk, wspec, vec], out_shape=[_sds((s, pw_), _MXU), _sds((ng, pg, pg), F32), _sds((1, pw_), F32)],
        compiler_params=_cp(), name=name,
    )(dmix, pre, pool_w, pool_scale)


def _scan_chunks(xr_ref, xi_ref, sr_ref, si_ref, ar, ai, reverse):
    n, c = xr_ref.shape
    tt = min(SCAN_T, n)
    lw = min(SCAN_LANES, c)
    nchunk = n // tt
    t = lax.broadcasted_iota(jnp.int32, (tt, lw), 0)

    for l0 in range(0, c, lw):
        lanes = slice(l0, l0 + lw)
        a_r, a_i = ar[:, lanes], ai[:, lanes]

        def local(vr, vi, a_r=a_r, a_i=a_i):
            pr, pi = a_r, a_i
            k = 1
            while k < tt:
                if reverse:
                    hr, hi = _shift_up(vr, k, t), _shift_up(vi, k, t)
                else:
                    hr, hi = _shift_down(vr, k, t), _shift_down(vi, k, t)
                vr, vi = vr + pr * hr - pi * hi, vi + pr * hi + pi * hr
                pr, pi = pr * pr - pi * pi, 2.0 * pr * pi
                k *= 2
            return vr, vi

        edge = tt - 1 if reverse else 0
        pw_r, pw_i = local(jnp.where(t == edge, a_r, 0.0), jnp.where(t == edge, a_i, 0.0))
        last = 0 if reverse else tt - 1

        def body(i, carry, lanes=lanes, local=local, pw_r=pw_r, pw_i=pw_i):
            cr, ci = carry
            ch = nchunk - 1 - i if reverse else i
            rows = pl.ds(pl.multiple_of(ch * tt, tt), tt)
            vr, vi = local(xr_ref[rows, lanes], xi_ref[rows, lanes])
            vr2 = vr + pw_r * cr - pw_i * ci
            vi2 = vi + pw_r * ci + pw_i * cr
            sr_ref[rows, lanes] = vr2
            si_ref[rows, lanes] = vi2
            return vr2[last:last + 1, :], vi2[last:last + 1, :]

        lax.fori_loop(0, nchunk, body, (jnp.zeros((1, lw), F32), jnp.zeros((1, lw), F32)))


_GELU_K = math.sqrt(2.0 / math.pi)


def _gelu_grad(y):
    inner = _GELU_K * (y + 0.044715 * y * y * y)
    th = jnp.tanh(inner)
    return 0.5 * (1.0 + th) + 0.5 * y * (1.0 - th * th) * _GELU_K * (1.0 + 3.0 * 0.044715 * y * y)


def _ssm_fwd(name, h, u_block0, bdr, bdi, cdr, cdi, dvec, ar, ai):
    s = h.shape[0]
    nt, cw, lw = bdr.shape
    rc = min(256, s)

    def body(u_ref, bdr_ref, bdi_ref, cdr_ref, cdi_ref, d_ref, ar_ref, ai_ref, sr_ref, si_ref, y_ref, yg_ref):
        def mm_in(c, _):
            rows = pl.ds(pl.multiple_of(c * rc, rc), rc)
            ub = u_ref[rows, :].astype(_MXU)
            sr_ref[rows, :] = jnp.dot(ub, bdr_ref[...], preferred_element_type=F32)
            si_ref[rows, :] = jnp.dot(ub, bdi_ref[...], preferred_element_type=F32)
            return 0

        lax.fori_loop(0, s // rc, mm_in, 0)
        _scan_chunks(sr_ref, si_ref, sr_ref, si_ref, ar_ref[...], ai_ref[...], reverse=False)

        def mm_out(c, _):
            rows = pl.ds(pl.multiple_of(c * rc, rc), rc)
            y = (jnp.dot(sr_ref[rows, :].astype(_MXU), cdr_ref[...], preferred_element_type=F32)
                 - jnp.dot(si_ref[rows, :].astype(_MXU), cdi_ref[...], preferred_element_type=F32)
                 + d_ref[...] * u_ref[rows, :])
            y_ref[rows, :] = y
            yg_ref[rows, :] = jax.nn.gelu(y).astype(yg_ref.dtype)
            return 0

        lax.fori_loop(0, s // rc, mm_out, 0)

    st = pl.BlockSpec((s, lw), lambda j: (0, j))
    ch = pl.BlockSpec((s, cw), lambda j: (0, j))
    bspec = pl.BlockSpec((None, cw, lw), lambda j: (j, 0, 0))
    cspec = pl.BlockSpec((None, lw, cw), lambda j: (j, 0, 0))
    return pl.pallas_call(
        body, grid=(nt,),
        in_specs=[pl.BlockSpec((s, cw), lambda j: (0, u_block0 + j)), bspec, bspec, cspec, cspec,
                  pl.BlockSpec((1, cw), lambda j: (0, j)), pl.BlockSpec((1, lw), lambda j: (0, j)),
                  pl.BlockSpec((1, lw), lambda j: (0, j))],
        out_specs=[st, st, ch, ch],
        out_shape=[_sds((s, nt * lw), F32), _sds((s, nt * lw), F32), _sds((s, nt * cw), F32), _sds((s, nt * cw), _MXU)],
        compiler_params=_cp(), name=name,
    )(h, bdr, bdi, cdr, cdi, dvec, ar, ai)


def _ssm_bwd(name, dyg, ypre, h, u_block0, sr, si, bdr, bdi, cdr, cdi, dvec, ar, ai):
    s = h.shape[0]
    nt, cw, lw = bdr.shape
    rc = min(256, s)

    def body(dyg_ref, yp_ref, u_ref, sr_ref, si_ref, bdr_ref, bdi_ref, cdr_ref, cdi_ref, d_ref, ar_ref, ai_ref,
             du_ref, dd_ref, dcr_ref, dci_ref, dbr_ref, dbi_ref, dar_ref, dai_ref, lr_scr, li_scr, dy_scr):
        for ref in (dd_ref, dcr_ref, dci_ref, dbr_ref, dbi_ref, dar_ref, dai_ref):
            ref[...] = jnp.zeros_like(ref)

        def p1(c, _):
            rows = pl.ds(pl.multiple_of(c * rc, rc), rc)
            dy = dyg_ref[rows, :] * _gelu_grad(yp_ref[rows, :])
            dy_scr[rows, :] = dy
            dd_ref[...] += jnp.sum(dy * u_ref[rows, :], 0, keepdims=True)
            dyb = dy.astype(_MXU)
            lr_scr[rows, :] = lax.dot_general(dyb, cdr_ref[...], (NT, ((), ())), preferred_element_type=F32)
            li_scr[rows, :] = -lax.dot_general(dyb, cdi_ref[...], (NT, ((), ())), preferred_element_type=F32)
            dcr_ref[...] += lax.dot_general(sr_ref[rows, :].astype(_MXU), dyb, (TN, ((), ())), preferred_element_type=F32)
            dci_ref[...] -= lax.dot_general(si_ref[rows, :].astype(_MXU), dyb, (TN, ((), ())), preferred_element_type=F32)
            return 0

        lax.fori_loop(0, s // rc, p1, 0)
        _scan_chunks(lr_scr, li_scr, lr_scr, li_scr, ar_ref[...], -ai_ref[...], reverse=True)
        t = lax.broadcasted_iota(jnp.int32, (rc, lw), 0)

        def p2(c, _):
            r0 = pl.multiple_of(c * rc, rc)
            rows = pl.ds(r0, rc)
            before = pl.ds(pl.multiple_of(jnp.maximum(r0 - 8, 0), 8), 8)
            have = (c > 0).astype(F32)
            lr, li = lr_scr[rows, :], li_scr[rows, :]
            spr = jnp.where(t == 0, sr_ref[before, :][7:8, :] * have, pltpu.roll(sr_ref[rows, :], 1, 0))
            spi = jnp.where(t == 0, si_ref[before, :][7:8, :] * have, pltpu.roll(si_ref[rows, :], 1, 0))
            dar_ref[...] += jnp.sum(lr * spr + li * spi, 0, keepdims=True)
            dai_ref[...] += jnp.sum(li * spr - lr * spi, 0, keepdims=True)
            lrb, lib = lr.astype(_MXU), li.astype(_MXU)
            du = (dy_scr[rows, :] * d_ref[...]
                  + lax.dot_general(lrb, bdr_ref[...], (NT, ((), ())), preferred_element_type=F32)
                  + lax.dot_general(lib, bdi_ref[...], (NT, ((), ())), preferred_element_type=F32))
            du_ref[rows, :] = du.astype(du_ref.dtype)
            ub = u_ref[rows, :].astype(_MXU)
            dbr_ref[...] += lax.dot_general(ub, lrb, (TN, ((), ())), preferred_element_type=F32)
            dbi_ref[...] += lax.dot_general(ub, lib, (TN, ((), ())), preferred_element_type=F32)
            return 0

        lax.fori_loop(0, s // rc, p2, 0)

    st = pl.BlockSpec((s, lw), lambda j: (0, j))
    ch = pl.BlockSpec((s, cw), lambda j: (0, j))
    bspec = pl.BlockSpec((None, cw, lw), lambda j: (j, 0, 0))
    cspec = pl.BlockSpec((None, lw, cw), lambda j: (j, 0, 0))
    cvec = pl.BlockSpec((1, cw), lambda j: (0, j))
    svec = pl.BlockSpec((1, lw), lambda j: (0, j))
    return pl.pallas_call(
        body, grid=(nt,),
        in_specs=[ch, ch, pl.BlockSpec((s, cw), lambda j: (0, u_block0 + j)), st, st, bspec, bspec, cspec, cspec, cvec, svec, svec],
        out_specs=[ch, cvec, cspec, cspec, bspec, bspec, svec, svec],
        out_shape=[_sds((s, nt * cw), _MXU), _sds((1, nt * cw), F32), _sds((nt, lw, cw), F32), _sds((nt, lw, cw), F32),
                   _sds((nt, cw, lw), F32), _sds((nt, cw, lw), F32), _sds((1, nt * lw), F32), _sds((1, nt * lw), F32)],
        scratch_shapes=[pltpu.VMEM((s, lw), F32), pltpu.VMEM((s, lw), F32), pltpu.VMEM((s, cw), F32)],
        compiler_params=_cp(), name=name,
    )(dyg, ypre, h, sr, si, bdr, bdi, cdr, cdi, dvec, ar, ai)


def _glu_fwd(name, yg, gw):
    s, w = yg.shape
    tr = min(512, s)

    def body(y_ref, w_ref, o_ref, ab_ref):
        ab = jnp.dot(y_ref[...], w_ref[...], preferred_element_type=F32)
        ab_ref[...] = ab
        o_ref[...] = (ab[:, :w] * jax.nn.sigmoid(ab[:, w:])).astype(o_ref.dtype)

    return pl.pallas_call(
        body, grid=(s // tr,), in_specs=[pl.BlockSpec((tr, w), lambda i: (i, 0)), _resident((w, 2 * w), lambda i: (0, 0))],
        out_specs=[pl.BlockSpec((tr, w), lambda i: (i, 0)), pl.BlockSpec((tr, 2 * w), lambda i: (i, 0))],
        out_shape=[_sds((s, w), _MXU), _sds((s, 2 * w), F32)], compiler_params=_cp(), name=name,
    )(yg, gw)


def _glu_bwd(name, dmix, col_block, ab, gw):
    s = ab.shape[0]
    w = ab.shape[1] // 2
    tr = min(512, s)

    def body(do_ref, ab_ref, w_ref, dab_ref, dy_ref):
        do = do_ref[...]
        a, b = ab_ref[:, :w], ab_ref[:, w:]
        sg = jax.nn.sigmoid(b)
        da = (do * sg).astype(_MXU)
        db = (do * a * sg * (1.0 - sg)).astype(_MXU)
        dab_ref[:, :w] = da
        dab_ref[:, w:] = db
        dy_ref[...] = (lax.dot_general(da, w_ref[:, :w], (NT, ((), ())), preferred_element_type=F32)
                       + lax.dot_general(db, w_ref[:, w:], (NT, ((), ())), preferred_element_type=F32))

    return pl.pallas_call(
        body, grid=(s // tr,),
        in_specs=[pl.BlockSpec((tr, w), lambda i: (i, col_block)), pl.BlockSpec((tr, 2 * w), lambda i: (i, 0)),
                  _resident((w, 2 * w), lambda i: (0, 0))],
        out_specs=[pl.BlockSpec((tr, 2 * w), lambda i: (i, 0)), pl.BlockSpec((tr, w), lambda i: (i, 0))],
        out_shape=[_sds((s, 2 * w), _MXU), _sds((s, w), F32)], compiler_params=_cp(), name=name,
    )(dmix, ab, gw)


CONV_ROWS = 64


def _conv_chunk(ref, w_ref, b_ref, c, tt):
    r0 = pl.multiple_of(c * tt, tt)
    before = ref[pl.ds(pl.multiple_of(jnp.maximum(r0 - 8, 0), 8), 8), :]
    before = jnp.where(c > 0, before, 0.0)
    main = ref[pl.ds(r0, tt), :]
    ext = jnp.concatenate([before, main], 0)
    d1 = pltpu.roll(ext, 1, 0)[8:, :]
    d2 = pltpu.roll(ext, 2, 0)[8:, :]
    hc = b_ref[...] + d2 * w_ref[0:1, :]
    hc = hc + d1 * w_ref[1:2, :]
    return hc + main * w_ref[2:3, :], main, d1, d2


def _conv_act_fwd(name, hu, cw, cb):
    s, f2 = hu.shape
    f = f2 // 2
    tw = _col_tile(f, 256)
    nt = f // tw

    tt = min(CONV_ROWS, s)

    def body(v_ref, g_ref, wv_ref, wg_ref, bv_ref, bg_ref, act_ref):
        def chunk(c, _):
            val = _conv_chunk(v_ref, wv_ref, bv_ref, c, tt)[0]
            gate = _conv_chunk(g_ref, wg_ref, bg_ref, c, tt)[0]
            act_ref[pl.ds(pl.multiple_of(c * tt, tt), tt), :] = (jax.nn.silu(gate) * val).astype(act_ref.dtype)
            return 0

        lax.fori_loop(0, s // tt, chunk, 0)

    cv = lambda rows: pl.BlockSpec((rows, tw), lambda i: (0, i))
    cg = lambda rows: pl.BlockSpec((rows, tw), lambda i: (0, nt + i))
    return pl.pallas_call(
        body, grid=(nt,), in_specs=[cv(s), cg(s), cv(CONV_WIDTH), cg(CONV_WIDTH), cv(1), cg(1)],
        out_specs=cv(s), out_shape=_sds((s, f), _MXU), compiler_params=_cp(), name=name,
    )(hu, hu, cw, cw, cb, cb)


def _conv_act_bwd(name, dact, hu, cw, cb):
    s, f2 = hu.shape
    f = f2 // 2
    tw = _col_tile(f, 256)
    nt = f // tw

    tt = min(CONV_ROWS, s)
    nchunk = s // tt

    def body(da_ref, v_ref, g_ref, wv_ref, wg_ref, bv_ref, bg_ref, dh_ref, dwv_ref, dwg_ref, dbv_ref, dbg_ref):
        def chunk(i, carry):
            c = nchunk - 1 - i
            rows = pl.ds(pl.multiple_of(c * tt, tt), tt)
            val, hv, hv1, hv2 = _conv_chunk(v_ref, wv_ref, bv_ref, c, tt)
            gate, hg, hg1, hg2 = _conv_chunk(g_ref, wg_ref, bg_ref, c, tt)
            sg = jax.nn.sigmoid(gate)
            da = da_ref[rows, :]
            dval = da * (gate * sg)
            dgate = da * val * sg * (1.0 + gate * (1.0 - sg))
            out = []
            for part, dhc, taps, w_ref, (after, acc) in ((0, dval, (hv2, hv1, hv), wv_ref, carry[0]),
                                                        (1, dgate, (hg2, hg1, hg), wg_ref, carry[1])):
                ext = jnp.concatenate([dhc, after], 0)
                u1 = pltpu.roll(ext, tt + 8 - 1, 0)[:tt, :]
                u2 = pltpu.roll(ext, tt + 8 - 2, 0)[:tt, :]
                dh = dhc * w_ref[2:3, :] + u1 * w_ref[1:2, :] + u2 * w_ref[0:1, :]
                dh_ref[part, rows, :] = dh.astype(dh_ref.dtype)
                sums = [jnp.sum(dhc * tap, 0, keepdims=True) for tap in taps] + [jnp.sum(dhc, 0, keepdims=True)]
                out.append((dhc[0:8, :], tuple(a + b for a, b in zip(acc, sums))))
            return tuple(out)

        zero = (jnp.zeros((8, tw), F32), tuple(jnp.zeros((1, tw), F32) for _ in range(CONV_WIDTH + 1)))
        (_, acc_v), (_, acc_g) = lax.fori_loop(0, nchunk, chunk, (zero, zero))
        for acc, dw_ref, db_ref in ((acc_v, dwv_ref, dbv_ref), (acc_g, dwg_ref, dbg_ref)):
            for tap in range(CONV_WIDTH):
                dw_ref[tap:tap + 1, :] = acc[tap]
            db_ref[...] = acc[CONV_WIDTH]

    cv = lambda rows: pl.BlockSpec((rows, tw), lambda i: (0, i))
    cg = lambda rows: pl.BlockSpec((rows, tw), lambda i: (0, nt + i))
    both = pl.BlockSpec((2, s, tw), lambda i: (0, 0, i))
    dh, dwv, dwg, dbv, dbg = pl.pallas_call(
        body, grid=(nt,), in_specs=[cv(s), cv(s), cg(s), cv(CONV_WIDTH), cg(CONV_WIDTH), cv(1), cg(1)],
        out_specs=[both, cv(CONV_WIDTH), cv(CONV_WIDTH), cv(1), cv(1)],
        out_shape=[_sds((2, s, f), _MXU), _sds((CONV_WIDTH, f), F32), _sds((CONV_WIDTH, f), F32),
                   _sds((1, f), F32), _sds((1, f), F32)],
        compiler_params=_cp(), name=name,
    )(dact, hu, hu, cw, cw, cb, cb)
    return dh, jnp.concatenate([dwv, dwg], 1), jnp.concatenate([dbv, dbg], 1)


ELEM_BLOCK = 512 * 1024


def _elem_tiles(r, c, budget=ELEM_BLOCK):
    rows = [t for t in range(8, r + 1, 8) if r % t == 0] or [r]
    cols = [t for t in range(V7X_LANES, c + 1, V7X_LANES) if c % t == 0] or [c]
    fits = [(tr * tc, tc, tr) for tr in rows for tc in cols if tr * tc <= budget]
    if not fits:
        return min(rows), min(cols)
    _, tc, tr = max(fits)
    return tr, tc


def _reduce_adamw(name, parts, w, m, v):
    n, nl, r, c = parts.shape
    tr, tc = _elem_tiles(r, c, ELEM_BLOCK // n)
    c1 = 1.0 - ADAM_B1 ** ADAM_STEP
    c2 = 1.0 - ADAM_B2 ** ADAM_STEP

    def body(p_ref, w_ref, m_ref, v_ref, g_ref, d_ref, nm_ref, nv_ref):
        g = p_ref[0].astype(F32)
        for i in range(1, n):
            g = g + p_ref[i].astype(F32)
        nm = ADAM_B1 * m_ref[...] + (1.0 - ADAM_B1) * g
        nv = ADAM_B2 * v_ref[...] + (1.0 - ADAM_B2) * (g * g)
        m_hat = nm / c1
        v_hat = nv / c2
        g_ref[...] = g
        nm_ref[...] = nm
        nv_ref[...] = nv
        d_ref[...] = -ADAM_LR * (m_hat / (jnp.sqrt(v_hat) + ADAM_EPS) + ADAM_WD * w_ref[...])

    blk = pl.BlockSpec((None, tr, tc), lambda l, i, j: (l, i, j))
    out = _sds((nl, r, c), F32)
    return pl.pallas_call(
        body, grid=(nl, r // tr, c // tc),
        in_specs=[pl.BlockSpec((n, None, tr, tc), lambda l, i, j: (0, l, i, j)), blk, blk, blk],
        out_specs=[blk, blk, blk, blk], out_shape=[out, out, out, out], compiler_params=_cp(), name=name,
    )(parts, w, m, v)


def _pair_sum(name, mine, theirs, c_idx):
    _, _, r, c = mine.shape
    tr, tc = _elem_tiles(r, c)

    def body(c_ref, a_ref, b_ref, o_ref):
        o_ref[...] = (a_ref[...].astype(F32) + b_ref[...].astype(F32)).astype(o_ref.dtype)

    return pl.pallas_call(
        body,
        grid_spec=pltpu.PrefetchScalarGridSpec(
            num_scalar_prefetch=1, grid=(4, r // tr, c // tc),
            in_specs=[pl.BlockSpec((None, None, tr, tc), lambda p, i, j, cref: (p, cref[0], i, j)),
                      pl.BlockSpec((None, tr, tc), lambda p, i, j, cref: (p, i, j))],
            out_specs=pl.BlockSpec((None, tr, tc), lambda p, i, j, cref: (p, i, j))),
        out_shape=_sds((4, r, c), _WIRE), compiler_params=_cp(), name=name,
    )(c_idx, mine, theirs)


def _place():
    return lax.axis_index("x"), lax.axis_index("y"), lax.axis_index("c")


def _all_gather(name, xs):
    n = len(xs)

    def body(*refs):
        x_refs, o_refs = refs[:n], refs[n:2 * n]
        send_sems, recv_sems, local_sems = refs[2 * n:]
        x, y, c = _place()
        me, sibling = (x, y, c), (x, y, 1 - c)
        chips = [(1 - x, y), (x, 1 - y), (1 - x, 1 - y)]

        def copy(a, k, block, to, src=None):
            px, py, pc = block
            rows = o_refs[a].at[4 * px + 2 * py + pc]
            return pltpu.make_async_remote_copy(
                src_ref=rows if src is None else src, dst_ref=rows, send_sem=send_sems.at[a, k], recv_sem=recv_sems.at[a, k],
                device_id=to, device_id_type=MESH)

        sent = []
        mine = []
        for a in range(n):
            mx, my, mc = me
            cp = pltpu.make_async_copy(x_refs[a], o_refs[a].at[4 * mx + 2 * my + mc], local_sems.at[a])
            cp.start()
            mine.append(cp)
            first = [copy(a, 0, me, sibling, src=x_refs[a])]
            first += [copy(a, 1 + j, me, (*chip, c), src=x_refs[a]) for j, chip in enumerate(chips)]
            for cp in first:
                cp.start()
            sent += first
        for a in range(n):
            for j, chip in enumerate(chips):
                copy(a, 1 + j, (*chip, c), me).wait_recv()
                fwd = copy(a, 4 + j, (*chip, c), sibling)
                fwd.start()
                sent.append(fwd)
        for a in range(n):
            copy(a, 0, sibling, me).wait_recv()
            for j, chip in enumerate(chips):
                copy(a, 4 + j, (*chip, 1 - c), me).wait_recv()
        for cp in sent:
            cp.wait_send()
        for cp in mine:
            cp.wait()

    return pl.pallas_call(
        body, in_specs=[ANY] * n, out_specs=[ANY] * n,
        out_shape=[_sds((N_DEV,) + a.shape, a.dtype) for a in xs],
        scratch_shapes=[pltpu.SemaphoreType.DMA((n, 7)), pltpu.SemaphoreType.DMA((n, 7)), pltpu.SemaphoreType.DMA((n,))],
        name=name,
    )(*xs)


def _swap_sibling(name, xs):
    n = len(xs)

    def body(*refs):
        x_refs, o_refs = refs[:n], refs[n:2 * n]
        send_sems, recv_sems = refs[2 * n:]
        x, y, c = _place()
        sibling = (x, y, 1 - c)
        cps = []
        for a in range(n):
            for p in range(4):
                cp = pltpu.make_async_remote_copy(
                    src_ref=x_refs[a].at[p, 1 - c], dst_ref=o_refs[a].at[p], send_sem=send_sems.at[a, p],
                    recv_sem=recv_sems.at[a, p], device_id=sibling, device_id_type=MESH)
                cp.start()
                cps.append(cp)
        for cp in cps:
            cp.wait_recv()
        for cp in cps:
            cp.wait_send()

    return pl.pallas_call(
        body, in_specs=[ANY] * n, out_specs=[ANY] * n,
        out_shape=[_sds((4,) + a.shape[2:], a.dtype) for a in xs],
        scratch_shapes=[pltpu.SemaphoreType.DMA((n, 4)), pltpu.SemaphoreType.DMA((n, 4))],
        name=name,
    )(*xs)


HBM = pl.BlockSpec(memory_space=pltpu.HBM)
SEM = pl.BlockSpec(memory_space=pltpu.SEMAPHORE)
DATAFLOW = pltpu.SideEffectType.DATAFLOW_SIDE_EFFECTING


def _in_hbm(a):
    return pltpu.with_memory_space_constraint(a, pltpu.HBM)


def _split_copy_start(name, srcs, lands, copies, deps):
    n, nd = len(srcs), len(deps)
    per = len(copies([None] * n, [None] * n, probe=True)) // n

    def body(*refs):
        s_refs, l_refs = refs[:n], refs[n:2 * n]
        send_sems, recv_sems = refs[2 * n + nd], refs[2 * n + nd + 1]
        token = refs[-1]
        for a, k, src, dst, to in copies(s_refs, l_refs):
            pltpu.make_async_remote_copy(src_ref=src, dst_ref=dst, send_sem=send_sems.at[a * per + k],
                                         recv_sem=recv_sems.at[a * per + k], device_id=to, device_id_type=MESH).start()
        token[...] = jnp.zeros_like(token)

    both = list(srcs) + list(lands)
    outs = pl.pallas_call(
        body, name=name,
        out_shape=(pltpu.SemaphoreType.DMA((n * per,)), pltpu.SemaphoreType.DMA((n * per,)),
                   *[pltpu.HBM(a.shape, a.dtype) for a in both], _sds((8, V7X_LANES), F32)),
        in_specs=[HBM] * (2 * n) + [ANY] * nd,
        out_specs=(SEM, SEM, *[HBM] * (2 * n), pl.BlockSpec(memory_space=pltpu.VMEM)),
        input_output_aliases={i: 2 + i for i in range(2 * n)},
        compiler_params=pltpu.CompilerParams(has_side_effects=DATAFLOW),
    )(*[_in_hbm(a) for a in both], *deps)
    return outs[0], outs[1], list(outs[2:2 + n]), list(outs[2 + n:2 + 2 * n]), outs[-1]


def _split_copy_wait(name, send_sems, recv_sems, srcs, lands, arrivals, after):
    n = len(srcs)
    per = len(arrivals([None] * n, [None] * n, probe=True)) // n

    def body(*refs):
        s_refs, l_refs = refs[:n], refs[n:2 * n]
        send_sems_, recv_sems_ = refs[2 * n], refs[2 * n + 1]
        for a, k, src, dst, frm in arrivals(s_refs, l_refs):
            cp = pltpu.make_async_remote_copy(src_ref=src, dst_ref=dst, send_sem=send_sems_.at[a * per + k],
                                              recv_sem=recv_sems_.at[a * per + k], device_id=frm, device_id_type=MESH)
            cp.wait_send()
            cp.wait_recv()

    both = list(srcs) + list(lands)
    outs = pl.pallas_call(
        body, name=name, out_shape=tuple(pltpu.HBM(a.shape, a.dtype) for a in both),
        in_specs=[HBM] * (2 * n) + [SEM, SEM, ANY], out_specs=tuple([HBM] * (2 * n)),
        input_output_aliases={i: i for i in range(2 * n)},
        compiler_params=pltpu.CompilerParams(has_side_effects=DATAFLOW),
    )(*both, send_sems, recv_sems, after)
    return list(outs[:n]), list(outs[n:])


def _gather_copies(arriving):
    def copies(s_refs, l_refs, probe=False):
        if probe:
            return [None] * (4 * len(s_refs))
        x, y, c = _place()
        out = []
        for a in range(len(s_refs)):
            for k, (px, py, pc) in enumerate([(x, y, 1 - c), (1 - x, y, c), (x, 1 - y, c), (1 - x, 1 - y, c)]):
                slot = 4 * px + 2 * py + pc if arriving else 4 * x + 2 * y + c
                out.append((a, k, s_refs[a], l_refs[a].at[slot], (px, py, pc)))
        return out
    return copies


def _chip_copies(arriving):
    def copies(s_refs, l_refs, probe=False):
        if probe:
            return [None] * (3 * len(s_refs))
        x, y, c = _place()
        out = []
        for a in range(len(s_refs)):
            for j, (px, py) in enumerate([(1 - x, y), (x, 1 - y), (1 - x, 1 - y)]):
                src = s_refs[a].at[2 * x + y] if arriving else s_refs[a].at[2 * px + py]
                out.append((a, j, src, l_refs[a].at[j], (px, py, c)))
        return out
    return copies


def _gather_begin(name, shards, deps):
    x, y, c = _place()
    lands = [lax.dynamic_update_slice_in_dim(lax.empty((N_DEV,) + a.shape, a.dtype), a[None], 4 * x + 2 * y + c, 0)
             for a in shards]
    return _split_copy_start(name + "_start", shards, lands, _gather_copies(False), deps)


def _gather_end(name, handle, after):
    send_sems, recv_sems, srcs, lands, _ = handle
    _, lands = _split_copy_wait(name + "_wait", send_sems, recv_sems, srcs, lands, _gather_copies(True), after)
    return _gather_forward(name + "_forward", lands)


def _gather_forward(name, lands):
    n = len(lands)

    def body(*refs):
        o_refs = refs[n:2 * n]
        send_sems, recv_sems = refs[2 * n:]
        x, y, c = _place()
        sibling = (x, y, 1 - c)
        chips = [(1 - x, y), (x, 1 - y), (1 - x, 1 - y)]
        sent = []
        for a in range(n):
            for j, (px, py) in enumerate(chips):
                rows = o_refs[a].at[4 * px + 2 * py + c]
                cp = pltpu.make_async_remote_copy(src_ref=rows, dst_ref=rows, send_sem=send_sems.at[a, j],
                                                  recv_sem=recv_sems.at[a, j], device_id=sibling, device_id_type=MESH)
                cp.start()
                sent.append(cp)
        for a in range(n):
            for j, (px, py) in enumerate(chips):
                rows = o_refs[a].at[4 * px + 2 * py + 1 - c]
                pltpu.make_async_remote_copy(src_ref=rows, dst_ref=rows, send_sem=send_sems.at[a, j],
                                             recv_sem=recv_sems.at[a, j], device_id=sibling, device_id_type=MESH).wait_recv()
        for cp in sent:
            cp.wait_send()

    return pl.pallas_call(
        body, in_specs=[ANY] * n, out_specs=[ANY] * n, out_shape=[_sds(a.shape, a.dtype) for a in lands],
        input_output_aliases={i: i for i in range(n)},
        scratch_shapes=[pltpu.SemaphoreType.DMA((n, 3)), pltpu.SemaphoreType.DMA((n, 3))], name=name,
    )(*lands)


def _chips_begin(name, pairs, deps):
    lands = [lax.empty((3,) + a.shape[1:], a.dtype) for a in pairs]
    return _split_copy_start(name + "_start", pairs, lands, _chip_copies(False), deps)


def _chips_end(name, handle, after):
    send_sems, recv_sems, srcs, lands, _ = handle
    return _split_copy_wait(name + "_wait", send_sems, recv_sems, srcs, lands, _chip_copies(True), after)


def _adamw_layer(name, l, own, lands, w, m, v, prev):
    nl, ng, r, c = w.shape
    tr, tc = _elem_tiles(r, c)
    c1 = 1.0 - ADAM_B1 ** ADAM_STEP
    c2 = 1.0 - ADAM_B2 ** ADAM_STEP

    def body(own_ref, lands_ref, w_ref, m_ref, v_ref, *rest):
        g_ref, d_ref, nm_ref, nv_ref = rest[-4:]
        g = own_ref[...].astype(F32) + lands_ref[0].astype(F32) + lands_ref[1].astype(F32) + lands_ref[2].astype(F32)
        nm = ADAM_B1 * m_ref[...] + (1.0 - ADAM_B1) * g
        nv = ADAM_B2 * v_ref[...] + (1.0 - ADAM_B2) * (g * g)
        m_hat = nm / c1
        v_hat = nv / c2
        g_ref[...] = g
        nm_ref[...] = nm
        nv_ref[...] = nv
        d_ref[...] = -ADAM_LR * (m_hat / (jnp.sqrt(v_hat) + ADAM_EPS) + ADAM_WD * w_ref[...])

    lay = pl.BlockSpec((None, None, tr, tc), lambda g, i, j: (l, g, i, j))
    out = _sds((nl, ng, r, c), F32)
    prev = [] if prev is None else list(prev)
    return pl.pallas_call(
        body, grid=(ng, r // tr, c // tc),
        in_specs=[pl.BlockSpec((None, tr, tc), lambda g, i, j: (g, i, j)),
                  pl.BlockSpec((3, None, tr, tc), lambda g, i, j: (0, g, i, j)), lay, lay, lay] + [ANY] * len(prev),
        out_specs=[lay] * 4, out_shape=[out] * 4, input_output_aliases={5 + i: i for i in range(len(prev))},
        compiler_params=_cp(), name=name,
    )(own, lands, w, m, v, *prev)


def _pad_pairs(a, axis, half, half_pad):
    shp = a.shape
    a = a.reshape(shp[:axis] + (2, half) + shp[axis + 1:])
    pad = [(0, 0)] * a.ndim
    pad[axis + 1] = (0, half_pad - half)
    a = jnp.pad(a, pad)
    return a.reshape(shp[:axis] + (2 * half_pad,) + shp[axis + 1:])


def _unpad_pairs(a, axis, half, half_pad):
    shp = a.shape
    a = a.reshape(shp[:axis] + (2, half_pad) + shp[axis + 1:])
    a = lax.slice_in_dim(a, 0, half, axis=axis + 1)
    return a.reshape(shp[:axis] + (2 * half,) + shp[axis + 1:])


def _blockdiag(w, nt):
    g, a, b = w.shape
    gl = g // nt
    e = jnp.eye(gl, dtype=w.dtype).reshape(1, gl, 1, gl, 1)
    return (w.reshape(nt, gl, a, 1, b) * e).reshape(nt, gl * a, gl * b)


def _diagblocks(m, g, a, b):
    nt = m.shape[0]
    gl = g // nt
    d = jnp.diagonal(m.reshape(nt, gl, a, gl, b), axis1=1, axis2=3)
    return jnp.moveaxis(d, -1, 1).reshape(g, a, b)


def _ssm_discretise(lam_re, lam_im, log_dt, b_re, b_im):
    dt = jnp.exp(log_dt)[:, None]
    mag = jnp.exp(lam_re * dt)
    ab_re, ab_im = mag * jnp.cos(lam_im * dt), mag * jnp.sin(lam_im * dt)
    nr, ni = ab_re - 1.0, ab_im
    den = lam_re * lam_re + lam_im * lam_im
    zr = (nr * lam_re + ni * lam_im) / den
    zi = (ni * lam_re - nr * lam_im) / den
    bbr = zr[..., None] * b_re - zi[..., None] * b_im
    bbi = zr[..., None] * b_im + zi[..., None] * b_re
    return ab_re, ab_im, bbr, bbi


def _rope_tables(s):
    half = HEAD_DIM // 2
    inv = ROPE_THETA ** (-jnp.arange(half, dtype=F32) / half)
    ang = jnp.arange(s).astype(F32)[:, None] * inv[None, :]
    cos, sin = jnp.cos(ang), jnp.sin(ang)
    reps = V7X_LANES // HEAD_DIM
    return jnp.tile(jnp.concatenate([cos, cos], -1), (1, reps)), jnp.tile(jnp.concatenate([-sin, sin], -1), (1, reps))


_SMALL = ("attn_sinks", "pool_w", "pool_scale", "ssm_lam_re", "ssm_lam_im", "ssm_log_dt", "ssm_b_re", "ssm_b_im",
          "ssm_c_re", "ssm_c_im", "ssm_d", "ln1_g", "ln1_b", "ffn_conv_b", "ln2_g", "ln2_b")
_BIG = ("w_in", "ssm_glu_w", "w_out", "ffn_w_up", "ffn_w_down")
_ORDER = ("w_in", "attn_sinks", "pool_w", "pool_scale", "ssm_lam_re", "ssm_lam_im", "ssm_log_dt", "ssm_b_re", "ssm_b_im",
          "ssm_c_re", "ssm_c_im", "ssm_d", "ssm_glu_w", "w_out", "ln1_g", "ln1_b", "ffn_w_up", "ffn_conv_w", "ffn_conv_b",
          "ffn_w_down", "ln2_g", "ln2_b")


def kernel(x, w_in, attn_sinks, pool_w, pool_scale, ssm_lam_re, ssm_lam_im, ssm_log_dt, ssm_b_re, ssm_b_im, ssm_c_re, ssm_c_im, ssm_d, ssm_glu_w, w_out, ln1_g, ln1_b, ffn_w_up, ffn_conv_w, ffn_conv_b, ffn_w_down, ln2_g, ln2_b, loss_target, m_w_in, m_attn_sinks, m_pool_w, m_pool_scale, m_ssm_lam_re, m_ssm_lam_im, m_ssm_log_dt, m_ssm_b_re, m_ssm_b_im, m_ssm_c_re, m_ssm_c_im, m_ssm_d, m_ssm_glu_w, m_w_out, m_ln1_g, m_ln1_b, m_ffn_w_up, m_ffn_conv_w, m_ffn_conv_b, m_ffn_w_down, m_ln2_g, m_ln2_b, v_w_in, v_attn_sinks, v_pool_w, v_pool_scale, v_ssm_lam_re, v_ssm_lam_im, v_ssm_log_dt, v_ssm_b_re, v_ssm_b_im, v_ssm_c_re, v_ssm_c_im, v_ssm_d, v_ssm_glu_w, v_w_out, v_ln1_g, v_ln1_b, v_ffn_w_up, v_ffn_conv_w, v_ffn_conv_b, v_ffn_w_down, v_ln2_g, v_ln2_b):
    W = dict(w_in=w_in, attn_sinks=attn_sinks, pool_w=pool_w, pool_scale=pool_scale, ssm_lam_re=ssm_lam_re, ssm_lam_im=ssm_lam_im, ssm_log_dt=ssm_log_dt, ssm_b_re=ssm_b_re, ssm_b_im=ssm_b_im, ssm_c_re=ssm_c_re, ssm_c_im=ssm_c_im, ssm_d=ssm_d, ssm_glu_w=ssm_glu_w, w_out=w_out, ln1_g=ln1_g, ln1_b=ln1_b, ffn_w_up=ffn_w_up, ffn_conv_w=ffn_conv_w, ffn_conv_b=ffn_conv_b, ffn_w_down=ffn_w_down, ln2_g=ln2_g, ln2_b=ln2_b)
    M = dict(w_in=m_w_in, attn_sinks=m_attn_sinks, pool_w=m_pool_w, pool_scale=m_pool_scale, ssm_lam_re=m_ssm_lam_re, ssm_lam_im=m_ssm_lam_im, ssm_log_dt=m_ssm_log_dt, ssm_b_re=m_ssm_b_re, ssm_b_im=m_ssm_b_im, ssm_c_re=m_ssm_c_re, ssm_c_im=m_ssm_c_im, ssm_d=m_ssm_d, ssm_glu_w=m_ssm_glu_w, w_out=m_w_out, ln1_g=m_ln1_g, ln1_b=m_ln1_b, ffn_w_up=m_ffn_w_up, ffn_conv_w=m_ffn_conv_w, ffn_conv_b=m_ffn_conv_b, ffn_w_down=m_ffn_w_down, ln2_g=m_ln2_g, ln2_b=m_ln2_b)
    V = dict(w_in=v_w_in, attn_sinks=v_attn_sinks, pool_w=v_pool_w, pool_scale=v_pool_scale, ssm_lam_re=v_ssm_lam_re, ssm_lam_im=v_ssm_lam_im, ssm_log_dt=v_ssm_log_dt, ssm_b_re=v_ssm_b_re, ssm_b_im=v_ssm_b_im, ssm_c_re=v_ssm_c_re, ssm_c_im=v_ssm_c_im, ssm_d=v_ssm_d, ssm_glu_w=v_ssm_glu_w, w_out=v_w_out, ln1_g=v_ln1_g, ln1_b=v_ln1_b, ffn_w_up=v_ffn_w_up, ffn_conv_w=v_ffn_conv_w, ffn_conv_b=v_ffn_conv_b, ffn_w_down=v_ffn_w_down, ln2_g=v_ln2_g, ln2_b=v_ln2_b)

    depth = w_in.shape[0]
    s, d = x.shape[1], x.shape[2]
    alpha = (2 * depth) ** 0.25
    attn_w = d // 2
    kv_w = attn_w // GQA
    nkv = kv_w // HEAD_DIM
    pool_wd = d // 4
    ssm_wd = d // 4
    n_groups = ssm_wd // SSM_GROUP
    state_w = n_groups * SSM_STATE
    nt_ssm = max(1, state_w // 512)
    o_k, o_v, o_p, o_s = attn_w, attn_w + kv_w, attn_w + 2 * kv_w, attn_w + 2 * kv_w + pool_wd
    in_w = o_s + ssm_wd
    half = ffn_w_down.shape[1]
    half_pad = -(-half // 64) * 64
    ffp = 4 * 2 * half_pad
    xi, yi, ci = _place()
    me = 4 * xi + 2 * yi + ci
    c_idx = jnp.reshape(ci, (1,)).astype(jnp.int32)

    cos_t, sin_t = _rope_tables(s)

    def layer_shards(l):
        return [
            jnp.transpose(w_in[l]).astype(_WIRE), ssm_glu_w[l].astype(_WIRE), w_out[l].astype(_WIRE),
            _pad_pairs(jnp.transpose(ffn_w_up[l]).astype(_WIRE), 0, half, half_pad),
            jnp.pad(ffn_w_down[l].astype(_WIRE), ((0, half_pad - half), (0, 0))),
        ]

    (g_cw,) = _all_gather("gather_conv_w", [_pad_pairs(ffn_conv_w, 2, half, half_pad)])

    def mixer_weights(l, gathered):
        g_in, g_glu, g_out = gathered
        return dict(
            win_t=g_in.reshape(in_w, d),
            glu=jnp.transpose(g_glu, (1, 0, 2)).reshape(ssm_wd, 2 * ssm_wd),
            wout=g_out.reshape(d, d),
            cw=jnp.transpose(g_cw[:, l], (1, 0, 2)).reshape(CONV_WIDTH, 2 * ffp),
            cb=_pad_pairs(ffn_conv_b[l].reshape(N_DEV, 2 * half), 1, half, half_pad).reshape(1, 2 * ffp),
        )

    def ffn_weights(gathered):
        g_up, g_down = gathered
        return dict(wup_t=g_up.reshape(2 * ffp, d), wdown=g_down.reshape(ffp, d))

    shards = [layer_shards(l) for l in range(depth)]
    full = [None] * depth
    handle = _gather_begin("gather_mixer_weights_0", shards[0][:3], [g_cw])
    handle_ffn0 = _gather_begin("gather_ffn_weights_0", shards[0][3:], [handle[-1]])
    full[0] = mixer_weights(0, _gather_end("gather_mixer_weights_0", handle, g_cw))

    def ssm_params(l):
        return (ssm_lam_re[l], ssm_lam_im[l], ssm_log_dt[l], ssm_b_re[l], ssm_b_im[l])

    saved = []
    xf = x[0]
    xb = xf.astype(_MXU)
    for l in range(depth):
        fw = full[l]
        deps = ()
        if l + 1 < depth:
            handle = _gather_begin(f"gather_weights_{l + 1}", shards[l + 1], [fw["win_t"], handle_ffn0[-1]])
            deps = (handle[-1],)
        h = _mm_nt(f"in_proj_{l}", xb, fw["win_t"], deps=deps)
        q_rot, k_rot = _rope(f"rope_{l}", h, o_v, cos_t, sin_t, _MXU, ((0, o_k), (o_k, o_v)))
        k_hm = jnp.transpose(k_rot.reshape(s, nkv, HEAD_DIM), (1, 0, 2))
        v_hm = jnp.transpose(h[:, o_v:o_p].astype(_MXU).reshape(s, nkv, HEAD_DIM), (1, 0, 2))
        sinks = attn_sinks[l].reshape(nkv, GQA)
        o_attn, lse = _attn_fwd(f"attn_{l}", q_rot, k_hm, v_hm, sinks)
        pw_b = pool_w[l].astype(_MXU)
        psc = pool_scale[l].reshape(1, pool_wd)
        y_pool, pre = _pool_fwd(f"pool_{l}", h, o_p // pool_wd, pw_b, psc)
        ab_re, ab_im, bbr, bbi = _ssm_discretise(*ssm_params(l))
        bdr = _blockdiag(jnp.transpose(bbr, (0, 2, 1)), nt_ssm).astype(_MXU)
        bdi = _blockdiag(jnp.transpose(bbi, (0, 2, 1)), nt_ssm).astype(_MXU)
        cdr = _blockdiag(jnp.transpose(ssm_c_re[l], (0, 2, 1)), nt_ssm).astype(_MXU)
        cdi = _blockdiag(jnp.transpose(ssm_c_im[l], (0, 2, 1)), nt_ssm).astype(_MXU)
        dvec = ssm_d[l].reshape(1, ssm_wd)
        ar, ai = ab_re.reshape(1, state_w), ab_im.reshape(1, state_w)
        cw_ssm = ssm_wd // nt_ssm
        sr, si, ypre, yg = _ssm_fwd(f"ssm_{l}", h, o_s // cw_ssm, bdr, bdi, cdr, cdi, dvec, ar, ai)
        y_ssm, ab2 = _glu_fwd(f"glu_{l}", yg, fw["glu"])
        mix = jnp.concatenate([o_attn.astype(_MXU), y_pool, y_ssm], -1)
        a1 = _mm_nn(f"out_proj_{l}", mix, fw["wout"])
        g1, b1 = ln1_g[l].reshape(1, d), ln1_b[l].reshape(1, d)
        x1, x1b, xh1, rs1 = _ln_fwd(f"ln1_{l}", xf, a1, g1, b1, alpha)
        if l == 0:
            fw.update(ffn_weights(_gather_end("gather_ffn_weights_0", handle_ffn0, x1b)))
        hu = _mm_nt(f"ffn_up_{l}", x1b, fw["wup_t"], cap=2 * half_pad)
        act = _conv_act_fwd(f"ffn_act_{l}", hu, fw["cw"], fw["cb"])
        tkd = 2 * half_pad
        tnd = _col_tile(d, 1024)
        f_out = _mm(f"ffn_down_{l}", act, fw["wdown"], NN, (d // tnd, ffp // tkd),
                    pl.BlockSpec((s, tkd), lambda j, kk: (0, kk)), pl.BlockSpec((tkd, tnd), lambda j, kk: (kk, j)),
                    pl.BlockSpec((s, tnd), lambda j, kk: (0, j)), (s, d), F32)
        g2, b2 = ln2_g[l].reshape(1, d), ln2_b[l].reshape(1, d)
        x2, x2b, xh2, rs2 = _ln_fwd(f"ln2_{l}", x1, f_out, g2, b2, alpha)
        saved.append(dict(xb=xb, h=h, q_rot=q_rot, k_hm=k_hm, v_hm=v_hm, sinks=sinks, o_attn=o_attn, lse=lse, pw_b=pw_b, psc=psc,
                          pre=pre, bdr=bdr, bdi=bdi, cdr=cdr, cdi=cdi, dvec=dvec, ar=ar, ai=ai, sr=sr, si=si, ypre=ypre, yg=yg,
                          ab2=ab2, mix=mix, g1=g1, xh1=xh1, rs1=rs1, x1b=x1b, hu=hu, act=act, g2=g2, xh2=xh2, rs2=rs2))
        xf, xb = x2, x2b
        if l + 1 < depth:
            gathered = _gather_end(f"gather_weights_{l + 1}", handle, x2b)
            full[l + 1] = {**mixer_weights(l + 1, gathered[:3]), **ffn_weights(gathered[3:])}

    dy, loss_part = _loss_head("loss_head", xf, loss_target[0])
    loss = lax.psum(loss_part[0, 0], ("x", "y", "c"))

    small_g = {k: [None] * depth for k in _SMALL}
    cw_g = [None] * depth
    outs = {}
    big_res = {k: None for k in _BIG}
    my_chip = 2 * xi + yi
    pending = []

    transposed = ("w_in", "ffn_w_up")

    def row_groups(name_, t):
        g = 2 if name_ == "ffn_w_up" else 1
        return t.reshape(t.shape[:-2] + (g, t.shape[-2] // g, t.shape[-1]))

    def as_groups(name_, t):
        return row_groups(name_, jnp.transpose(t, (0, 2, 1)) if name_ in transposed else t)

    def from_groups(name_, t):
        t = t.reshape(t.shape[0], t.shape[1] * t.shape[2], t.shape[3])
        return jnp.transpose(t, (0, 2, 1)) if name_ in transposed else t

    grouped = {name_: tuple(as_groups(name_, t[name_]) for t in (W, M, V)) for name_ in _BIG}

    def finish_exchanges(after):
        while pending:
            lay, part, names, handle = pending.pop(0)
            pairs, lands = _chips_end(f"grads_between_chips_{part}_{lay}", handle, after)
            for name_, p, ld in zip(names, pairs, lands):
                own = row_groups(name_, lax.dynamic_index_in_dim(p, my_chip, 0, keepdims=False))
                big_res[name_] = _adamw_layer(f"adamw_{name_}_{lay}", lay, own, row_groups(name_, ld), *grouped[name_],
                                              big_res[name_])

    def begin_exchange(lay, part, names, grads, after):
        by_owner = [a.reshape((4, 2) + a.shape[1:]) for a in grads]
        theirs = _swap_sibling(f"grads_to_sibling_{part}_{lay}", by_owner)
        pair = [_pair_sum(f"pair_sum_{name_}_{lay}", a, b, c_idx) for name_, a, b in zip(names, by_owner, theirs)]
        finish_exchanges(after)
        handle = _chips_begin(f"grads_between_chips_{part}_{lay}", pair, [])
        pending.append((lay, part, names, handle))
        return handle[-1]

    token = None
    small_handle = None
    for l in reversed(range(depth)):
        fw, sv = full[l], saved[l]
        deps = () if token is None else (token,)
        dr2, dr2b, dg2, db2 = _ln_bwd(f"ln2_bwd_{l}", dy, sv["xh2"], sv["rs2"], sv["g2"], deps=deps)
        d_wdown = _mm_tn_acols(f"ffn_down_dw_{l}", sv["act"], dr2b, _WIRE, cap=2 * half_pad)
        dact = _mm_nt(f"ffn_down_dx_{l}", dr2b, fw["wdown"], cap=2 * half_pad)
        dhu, dcw, dcb = _conv_act_bwd(f"ffn_act_bwd_{l}", dact, sv["hu"], fw["cw"], fw["cb"])
        d_wup = _mm(f"ffn_up_dw_{l}", dhu, sv["x1b"], TN, (N_DEV, 1),
                    pl.BlockSpec((None, s, 2 * half_pad), lambda j, kk: (j // 4, 0, j % 4)),
                    _resident((s, d), lambda j, kk: (0, 0)),
                    pl.BlockSpec((2 * half_pad, d), lambda j, kk: (j, 0)), (2 * ffp, d), _WIRE)
        tnd = _col_tile(d, 512)
        dy1 = _mm(f"ffn_up_dx_{l}", dhu, fw["wup_t"], NN, (d // tnd, N_DEV),
                  pl.BlockSpec((None, s, 2 * half_pad), lambda j, kk: (kk // 4, 0, kk % 4)),
                  pl.BlockSpec((2 * half_pad, tnd), lambda j, kk: (kk, j)),
                  pl.BlockSpec((s, tnd), lambda j, kk: (0, j)), (s, d), F32,
                  add=dr2, add_spec=pl.BlockSpec((s, tnd), lambda j, kk: (0, j)), add_scale=alpha)
        token = begin_exchange(l, "ffn", ("ffn_w_up", "ffn_w_down"),
                               [d_wup.reshape(N_DEV, 2 * half_pad, d), d_wdown.reshape(N_DEV, half_pad, d)], dy1)
        dr1, dr1b, dg1, db1 = _ln_bwd(f"ln1_bwd_{l}", dy1, sv["xh1"], sv["rs1"], sv["g1"], deps=(token,))
        d_wout = _mm_tn_acols(f"out_proj_dw_{l}", sv["mix"], dr1b, _WIRE, cap=d // N_DEV)
        dmix = _mm_nt(f"out_proj_dx_{l}", dr1b, fw["wout"])
        dq_rot, dk_hm, dv_hm, dsk = _attn_bwd(f"attn_bwd_{l}", sv["q_rot"], sv["k_hm"], sv["v_hm"], sv["o_attn"], dmix,
                                             sv["lse"], sv["sinks"])
        dqk = jnp.concatenate([dq_rot, jnp.transpose(dk_hm, (1, 0, 2)).reshape(s, kv_w)], -1)
        dhq, dhk = _rope(f"rope_bwd_{l}", dqk, o_v, cos_t, -sin_t, _MXU, ((0, o_k), (o_k, o_v)))
        dhv = jnp.transpose(dv_hm, (1, 0, 2)).reshape(s, kv_w).astype(_MXU)
        dhp, dpw, dpsc = _pool_bwd(f"pool_bwd_{l}", dmix, attn_w // pool_wd, sv["pre"], sv["pw_b"], sv["psc"])
        dab2, dyg = _glu_bwd(f"glu_bwd_{l}", dmix, (attn_w + pool_wd) // ssm_wd, sv["ab2"], fw["glu"])
        d_glu = _mm_tn_bcols(f"glu_dw_{l}", sv["yg"], dab2, _WIRE)
        cw_ssm = ssm_wd // nt_ssm
        dhs, dd, dcdr, dcdi, dbdr, dbdi, dar, dai = _ssm_bwd(
            f"ssm_bwd_{l}", dyg, sv["ypre"], sv["h"], o_s // cw_ssm, sv["sr"], sv["si"], sv["bdr"], sv["bdi"], sv["cdr"],
            sv["cdi"], sv["dvec"], sv["ar"], sv["ai"])
        dh = jnp.concatenate([dhq, dhk, dhv, dhp, dhs], -1)
        d_win = _mm_tn_acols(f"in_proj_dw_{l}", dh, sv["xb"], _WIRE)
        dy = _mm_nn(f"in_proj_dx_{l}", dh, fw["win_t"], add=dr1, add_scale=alpha)

        dbbr = jnp.transpose(_diagblocks(dbdr, n_groups, SSM_GROUP, SSM_STATE), (0, 2, 1))
        dbbi = jnp.transpose(_diagblocks(dbdi, n_groups, SSM_GROUP, SSM_STATE), (0, 2, 1))
        _, vjp = jax.vjp(_ssm_discretise, *ssm_params(l))
        dlr, dli, dldt, dbr, dbi = vjp((dar.reshape(n_groups, SSM_STATE), dai.reshape(n_groups, SSM_STATE), dbbr, dbbi))
        small_g["attn_sinks"][l] = dsk.reshape(-1)
        small_g["pool_w"][l] = dpw
        small_g["pool_scale"][l] = dpsc.reshape(-1)
        small_g["ssm_lam_re"][l], small_g["ssm_lam_im"][l], small_g["ssm_log_dt"][l] = dlr, dli, dldt
        small_g["ssm_b_re"][l], small_g["ssm_b_im"][l] = dbr, dbi
        small_g["ssm_c_re"][l] = jnp.transpose(_diagblocks(dcdr, n_groups, SSM_STATE, SSM_GROUP), (0, 2, 1))
        small_g["ssm_c_im"][l] = jnp.transpose(_diagblocks(dcdi, n_groups, SSM_STATE, SSM_GROUP), (0, 2, 1))
        small_g["ssm_d"][l] = dd.reshape(n_groups, SSM_GROUP)
        small_g["ln1_g"][l], small_g["ln1_b"][l] = dg1.reshape(-1), db1.reshape(-1)
        small_g["ln2_g"][l], small_g["ln2_b"][l] = dg2.reshape(-1), db2.reshape(-1)
        small_g["ffn_conv_b"][l] = _unpad_pairs(dcb.reshape(N_DEV, 2 * half_pad), 1, half, half_pad).reshape(-1)
        cw_g[l] = _unpad_pairs(dcw.reshape(CONV_WIDTH, N_DEV, 2 * half_pad), 2, half, half_pad)

        if l == 0:
            flat_parts = [jnp.stack(small_g[k]).reshape(-1) for k in _SMALL] + [jnp.stack(cw_g).reshape(-1)]
            sizes = [a.shape[0] for a in flat_parts]
            total = sum(sizes)
            rows = -(-total // (512 * V7X_LANES)) * 512
            flat = jnp.pad(jnp.concatenate(flat_parts), (0, rows * V7X_LANES - total)).reshape(rows, V7X_LANES)
            small_handle = _gather_begin("gather_small_grads", [flat], [dy])
        token = begin_exchange(l, "mixer", ("w_in", "ssm_glu_w", "w_out"),
                               [d_win.reshape(N_DEV, in_w // N_DEV, d),
                                jnp.transpose(d_glu.reshape(ssm_wd, N_DEV, 2 * ssm_wd // N_DEV), (1, 0, 2)),
                                d_wout.reshape(N_DEV, d // N_DEV, d)], dy)

    (gathered,) = _gather_end("gather_small_grads", small_handle, token)
    gathered = gathered.reshape(N_DEV, -1)
    n_rep = sum(sizes[:-1])
    cw_all = gathered[:, n_rep:total].reshape(N_DEV, depth, CONV_WIDTH, N_DEV, 2 * half)
    cw_mine = lax.dynamic_index_in_dim(cw_all, me, axis=3, keepdims=False)
    n_small = n_rep + cw_mine[0].size
    rows2 = -(-n_small // (512 * V7X_LANES)) * 512

    def flatten(parts_):
        return jnp.pad(jnp.concatenate(parts_, -1), [(0, 0)] * (parts_[0].ndim - 1) + [(0, rows2 * V7X_LANES - n_small)])

    p_small = flatten([gathered[:, :n_rep], cw_mine.reshape(N_DEV, -1)]).reshape(N_DEV, 1, rows2, V7X_LANES)
    w_small, m_small, v_small = (
        flatten([jnp.concatenate([t[k].reshape(-1) for k in _SMALL]), t["ffn_conv_w"].reshape(-1)]).reshape(1, rows2, V7X_LANES)
        for t in (W, M, V))
    res_small = [a.reshape(-1) for a in _reduce_adamw("adamw_small", p_small, w_small, m_small, v_small)]
    off = 0
    for k in _SMALL + ("ffn_conv_w",):
        n = W[k].size
        outs[k] = tuple(a[off:off + n].reshape(W[k].shape) for a in res_small)
        off += n

    finish_exchanges(res_small[0])
    for name_ in _BIG:
        outs[name_] = tuple(from_groups(name_, t) for t in big_res[name_])

    grad_x = dy[None]
    result = [loss, grad_x]
    for i in range(4):
        result += [outs[k][i] for k in _ORDER]
    return tuple(result)
```

```python
import functools
import math

import jax
import jax.numpy as jnp
from jax import lax
from jax.experimental import pallas as pl
from jax.experimental.pallas import tpu as pltpu

F32 = jnp.float32
_MXU = jnp.bfloat16
_WIRE = jnp.bfloat16

HEAD_DIM = 64
GQA = 4
ATTN_BLOCK = 128
ROPE_THETA = 10000.0
POOL_WINDOWS = (2, 4, 8, 16)
SSM_GROUP = 16
SSM_STATE = 64
CONV_WIDTH = 3
LN_EPS = 1e-5
ADAM_LR, ADAM_B1, ADAM_B2, ADAM_EPS, ADAM_WD, ADAM_STEP = 0.001, 0.9, 0.999, 1e-08, 0.01, 10

N_DEV = 8
V7X_LANES = 128
V7X_VMEM_LIMIT = 56 * 1024 * 1024
SCAN_T = 64
SCAN_LANES = 256
MESH = pl.DeviceIdType.MESH
ANY = pl.BlockSpec(memory_space=pl.ANY)


def _cp():
    return pltpu.CompilerParams(vmem_limit_bytes=V7X_VMEM_LIMIT)


def _resident(block, index_map):
    return pl.BlockSpec(block, index_map, pipeline_mode=pl.Buffered(1))


def _sds(shape, dtype):
    return jax.ShapeDtypeStruct(tuple(shape), dtype)


def _mm(name, a, b, dims, grid, a_spec, b_spec, o_spec, out_shape, out_dtype, add=None, add_spec=None, add_scale=1.0, deps=()):
    nk = grid[1]
    n_in = 2 + (add is not None) + len(deps)
    oblk = tuple(d for d in o_spec.block_shape if d is not None)
    scratch = nk > 1 and out_dtype != F32

    def body(*refs):
        a_ref, b_ref = refs[:2]
        add_ref = None if add is None else refs[2]
        o_ref = refs[n_in]
        acc_ref = refs[-1] if scratch else None

        def finish(r):
            if add_ref is not None:
                r = r + add_scale * add_ref[...]
            o_ref[...] = r.astype(o_ref.dtype)

        part = lax.dot_general(a_ref[...], b_ref[...], (dims, ((), ())), preferred_element_type=F32)
        if nk == 1:
            finish(part)
        elif not scratch:
            k = pl.program_id(1)

            @pl.when(k == 0)
            def _():
                o_ref[...] = part

            @pl.when(k > 0)
            def _():
                o_ref[...] += part

            if add_ref is not None:
                @pl.when(k == nk - 1)
                def _():
                    o_ref[...] += add_scale * add_ref[...]
        else:
            k = pl.program_id(1)

            @pl.when(k == 0)
            def _():
                acc_ref[...] = part

            @pl.when(k > 0)
            def _():
                acc_ref[...] += part

            @pl.when(k == nk - 1)
            def _():
                finish(acc_ref[...])

    ins = [a, b] + ([] if add is None else [add]) + list(deps)
    in_specs = [a_spec, b_spec] + ([] if add is None else [add_spec]) + [ANY] * len(deps)
    return pl.pallas_call(
        body, grid=grid, in_specs=in_specs, out_specs=o_spec, out_shape=_sds(out_shape, out_dtype),
        scratch_shapes=[pltpu.VMEM(oblk, F32)] if scratch else [], compiler_params=_cp(), name=name,
    )(*ins)


NN = ((1,), (0,))
NT = ((1,), (1,))
TN = ((0,), (0,))


def _col_tile(n, cap=512):
    if n % V7X_LANES:
        return n
    t = min(cap, n)
    t -= t % V7X_LANES
    while n % t:
        t -= V7X_LANES
    return t


def _mm_nn(name, a, b, out_dtype=F32, cap=512, add=None, add_scale=1.0, deps=()):
    m, k = a.shape
    n = b.shape[1]
    tn = _col_tile(n, cap)
    o_spec = pl.BlockSpec((m, tn), lambda j, kk: (0, j))
    return _mm(name, a, b, NN, (n // tn, 1), _resident((m, k), lambda j, kk: (0, 0)),
               pl.BlockSpec((k, tn), lambda j, kk: (0, j)), o_spec, (m, n), out_dtype,
               add=add, add_spec=None if add is None else o_spec, add_scale=add_scale, deps=deps)


def _mm_nt(name, a, b, out_dtype=F32, add=None, add_scale=1.0, cap=512, deps=()):
    m, k = a.shape
    n = b.shape[0]
    tn = _col_tile(n, cap)
    o_spec = pl.BlockSpec((m, tn), lambda j, kk: (0, j))
    return _mm(name, a, b, NT, (n // tn, 1), _resident((m, k), lambda j, kk: (0, 0)),
               pl.BlockSpec((tn, k), lambda j, kk: (j, 0)), o_spec, (m, n), out_dtype,
               add=add, add_spec=None if add is None else o_spec, add_scale=add_scale, deps=deps)


def _mm_tn_bcols(name, a, b, out_dtype, cap=512):
    s, k = a.shape
    n = b.shape[1]
    tn = _col_tile(n, cap)
    return _mm(name, a, b, TN, (n // tn, 1), _resident((s, k), lambda j, kk: (0, 0)),
               pl.BlockSpec((s, tn), lambda j, kk: (0, j)), pl.BlockSpec((k, tn), lambda j, kk: (0, j)), (k, n), out_dtype)


def _mm_tn_acols(name, a, b, out_dtype, cap=512):
    s, k = a.shape
    n = b.shape[1]
    tk = _col_tile(k, cap)
    return _mm(name, a, b, TN, (k // tk, 1), pl.BlockSpec((s, tk), lambda i, kk: (0, i)),
               _resident((s, n), lambda i, kk: (0, 0)), pl.BlockSpec((tk, n), lambda i, kk: (i, 0)), (k, n), out_dtype)


def _ln_fwd(name, x, a, g, b, alpha):
    s, d = x.shape
    tr = min(256, s)

    def body(x_ref, a_ref, g_ref, b_ref, y_ref, yb_ref, xh_ref, rs_ref):
        r = alpha * x_ref[...] + a_ref[...]
        mu = jnp.mean(r, -1, keepdims=True)
        c = r - mu
        var = jnp.mean(c * c, -1, keepdims=True)
        rstd = lax.rsqrt(var + LN_EPS)
        xh = c * rstd
        y = xh * g_ref[...] + b_ref[...]
        y_ref[...] = y
        yb_ref[...] = y.astype(_MXU)
        xh_ref[...] = xh
        rs_ref[...] = rstd

    row = pl.BlockSpec((tr, d), lambda i: (i, 0))
    vec = pl.BlockSpec((1, d), lambda i: (0, 0))
    return pl.pallas_call(
        body, grid=(s // tr,), in_specs=[row, row, vec, vec],
        out_specs=[row, row, row, pl.BlockSpec((tr, 1), lambda i: (i, 0))],
        out_shape=[_sds((s, d), F32), _sds((s, d), _MXU), _sds((s, d), F32), _sds((s, 1), F32)],
        compiler_params=_cp(), name=name,
    )(x, a, g, b)


def _ln_bwd(name, dy, xh, rstd, g, deps=()):
    s, d = dy.shape
    tr = min(256, s)
    nd = len(deps)

    def body(dy_ref, xh_ref, rs_ref, g_ref, *rest):
        dr_ref, drb_ref, dg_ref, db_ref = rest[nd:]
        i = pl.program_id(0)
        dy_ = dy_ref[...]
        xh_ = xh_ref[...]
        dxh = dy_ * g_ref[...]
        m1 = jnp.mean(dxh, -1, keepdims=True)
        m2 = jnp.mean(dxh * xh_, -1, keepdims=True)
        dr = rs_ref[...] * (dxh - m1 - xh_ * m2)
        dr_ref[...] = dr
        drb_ref[...] = dr.astype(_MXU)
        pg = jnp.sum(dy_ * xh_, 0, keepdims=True)
        pb = jnp.sum(dy_, 0, keepdims=True)

        @pl.when(i == 0)
        def _():
            dg_ref[...] = pg
            db_ref[...] = pb

        @pl.when(i > 0)
        def _():
            dg_ref[...] += pg
            db_ref[...] += pb

    row = pl.BlockSpec((tr, d), lambda i: (i, 0))
    vec = pl.BlockSpec((1, d), lambda i: (0, 0))
    return pl.pallas_call(
        body, grid=(s // tr,), in_specs=[row, row, pl.BlockSpec((tr, 1), lambda i: (i, 0)), vec] + [ANY] * nd,
        out_specs=[row, row, vec, vec],
        out_shape=[_sds((s, d), F32), _sds((s, d), _MXU), _sds((1, d), F32), _sds((1, d), F32)],
        compiler_params=_cp(), name=name,
    )(dy, xh, rstd, g, *deps)


def _loss_head(name, y, target):
    s, d = y.shape
    tr = min(256, s)

    def body(y_ref, t_ref, dy_ref, l_ref):
        i = pl.program_id(0)
        e = y_ref[...] - t_ref[...]
        dy_ref[...] = e * (1.0 / d)
        part = 0.5 * jnp.sum(jnp.mean(e * e, -1, keepdims=True), 0, keepdims=True)

        @pl.when(i == 0)
        def _():
            l_ref[...] = part

        @pl.when(i > 0)
        def _():
            l_ref[...] += part

    row = pl.BlockSpec((tr, d), lambda i: (i, 0))
    return pl.pallas_call(
        body, grid=(s // tr,), in_specs=[row, row], out_specs=[row, pl.BlockSpec((1, 1), lambda i: (0, 0))],
        out_shape=[_sds((s, d), F32), _sds((1, 1), F32)], compiler_params=_cp(), name=name,
    )(y, target)


def _rope(name, t, width, cos, sin, out_dtype, splits):
    s = t.shape[0]
    tr = min(256, s)
    assert width % V7X_LANES == 0

    def body(t_ref, c_ref, s_ref, *o_refs):
        lane = lax.broadcasted_iota(jnp.int32, (tr, V7X_LANES), 1)
        first = (lane % HEAD_DIM) < (HEAD_DIM // 2)
        cs, sn = c_ref[...], s_ref[...]
        for (lo, hi), o_ref in zip(splits, o_refs):
            for c0 in range(lo, hi, V7X_LANES):
                v = t_ref[:, c0:c0 + V7X_LANES].astype(F32)
                partner = jnp.where(first, pltpu.roll(v, V7X_LANES - HEAD_DIM // 2, 1), pltpu.roll(v, HEAD_DIM // 2, 1))
                o_ref[:, c0 - lo:c0 - lo + V7X_LANES] = (v * cs + partner * sn).astype(o_ref.dtype)

    tab = pl.BlockSpec((tr, V7X_LANES), lambda i: (i, 0))
    return pl.pallas_call(
        body, grid=(s // tr,), in_specs=[pl.BlockSpec((tr, width), lambda i: (i, 0)), tab, tab],
        out_specs=[pl.BlockSpec((tr, hi - lo), lambda i: (i, 0)) for lo, hi in splits],
        out_shape=[_sds((s, hi - lo), out_dtype) for lo, hi in splits], compiler_params=_cp(), name=name,
    )(t, cos, sin)


def _attn_masks():
    i = lax.broadcasted_iota(jnp.int32, (GQA * ATTN_BLOCK, 2 * ATTN_BLOCK), 0) % ATTN_BLOCK
    j = lax.broadcasted_iota(jnp.int32, (GQA * ATTN_BLOCK, 2 * ATTN_BLOCK), 1)
    cur_ok = jnp.logical_and(j >= ATTN_BLOCK, j - ATTN_BLOCK <= i)
    prev_ok = jnp.logical_and(j < ATTN_BLOCK, j > i)
    return cur_ok, prev_ok


def _attn_scores(q4, kcat, n, cur_ok, prev_ok):
    sc = lax.dot_general(q4, kcat, (NT, ((), ())), preferred_element_type=F32) * (HEAD_DIM ** -0.5)
    return jnp.where(jnp.logical_or(cur_ok, jnp.logical_and(prev_ok, n > 0)), sc, -1e30)


def _stack_heads(ref, rows):
    return jnp.concatenate([ref[rows, g * HEAD_DIM:(g + 1) * HEAD_DIM] for g in range(GQA)], 0)


def _per_head_column(values):
    r = lax.broadcasted_iota(jnp.int32, (GQA * ATTN_BLOCK, 1), 0) // ATTN_BLOCK
    col = jnp.zeros((GQA * ATTN_BLOCK, 1), F32)
    for g, val in enumerate(values):
        col = jnp.where(r == g, val, col)
    return col


def _attn_fwd(name, q, k, v, sinks):
    s = q.shape[0]
    nkv = k.shape[0]
    gw = GQA * HEAD_DIM
    nb = s // ATTN_BLOCK

    def body(sk_ref, q_ref, k_ref, v_ref, o_ref, lse_ref):
        h = pl.program_id(0)
        cur_ok, prev_ok = _attn_masks()
        sink = _per_head_column([sk_ref[h, g] for g in range(GQA)])

        def blk(n, carry):
            rows = pl.ds(pl.multiple_of(n * ATTN_BLOCK, ATTN_BLOCK), ATTN_BLOCK)
            prows = pl.ds(pl.multiple_of(jnp.maximum(n - 1, 0) * ATTN_BLOCK, ATTN_BLOCK), ATTN_BLOCK)
            kcat = jnp.concatenate([k_ref[prows, :], k_ref[rows, :]], 0)
            vcat = jnp.concatenate([v_ref[prows, :], v_ref[rows, :]], 0)
            sc = _attn_scores(_stack_heads(q_ref, rows), kcat, n, cur_ok, prev_ok)
            m = jnp.maximum(sc.max(-1, keepdims=True), sink)
            p = jnp.exp(sc - m)
            den = p.sum(-1, keepdims=True) + jnp.exp(sink - m)
            o = jnp.dot((p / den).astype(_MXU), vcat, preferred_element_type=F32)
            lse = m + jnp.log(den)
            for g in range(GQA):
                mine = slice(g * ATTN_BLOCK, (g + 1) * ATTN_BLOCK)
                o_ref[rows, g * HEAD_DIM:(g + 1) * HEAD_DIM] = o[mine, :]
                lse_ref[rows, g:g + 1] = lse[mine, :]
            return carry

        lax.fori_loop(0, nb, blk, 0)

    kv_spec = pl.BlockSpec((None, s, HEAD_DIM), lambda h: (h, 0, 0))
    return pl.pallas_call(
        body, grid=(nkv,),
        in_specs=[pl.BlockSpec(memory_space=pltpu.SMEM), pl.BlockSpec((s, gw), lambda h: (0, h)), kv_spec, kv_spec],
        out_specs=[pl.BlockSpec((s, gw), lambda h: (0, h)), pl.BlockSpec((None, s, GQA), lambda h: (h, 0, 0))],
        out_shape=[_sds((s, nkv * gw), F32), _sds((nkv, s, GQA), F32)], compiler_params=_cp(), name=name,
    )(sinks, q, k, v)


def _attn_bwd(name, q, k, v, o, dmix, lse, sinks):
    s = q.shape[0]
    nkv = k.shape[0]
    gw = GQA * HEAD_DIM
    nb = s // ATTN_BLOCK
    scale = HEAD_DIM ** -0.5

    def body(sk_ref, q_ref, k_ref, v_ref, o_ref, do_ref, lse_ref, dq_ref, dk_ref, dv_ref, dsk_ref):
        h = pl.program_id(0)
        cur_ok, prev_ok = _attn_masks()
        dk_ref[...] = jnp.zeros_like(dk_ref)
        dv_ref[...] = jnp.zeros_like(dv_ref)

        sink = _per_head_column([sk_ref[h, g] for g in range(GQA)])

        def blk(n, acc):
            rows = pl.ds(pl.multiple_of(n * ATTN_BLOCK, ATTN_BLOCK), ATTN_BLOCK)
            prows = pl.ds(pl.multiple_of(jnp.maximum(n - 1, 0) * ATTN_BLOCK, ATTN_BLOCK), ATTN_BLOCK)
            kcat = jnp.concatenate([k_ref[prows, :], k_ref[rows, :]], 0)
            vcat = jnp.concatenate([v_ref[prows, :], v_ref[rows, :]], 0)
            q4 = _stack_heads(q_ref, rows)
            do4 = _stack_heads(do_ref, rows)
            delta = jnp.sum(do4 * _stack_heads(o_ref, rows), -1, keepdims=True)
            dob = do4.astype(_MXU)
            lse = jnp.concatenate([lse_ref[rows, g:g + 1] for g in range(GQA)], 0)
            p = jnp.exp(_attn_scores(q4, kcat, n, cur_ok, prev_ok) - lse)
            dp = lax.dot_general(dob, vcat, (NT, ((), ())), preferred_element_type=F32)
            ds = (p * (dp - delta) * scale).astype(_MXU)
            dq = jnp.dot(ds, kcat, preferred_element_type=F32)
            for g in range(GQA):
                dq_ref[rows, g * HEAD_DIM:(g + 1) * HEAD_DIM] = dq[g * ATTN_BLOCK:(g + 1) * ATTN_BLOCK, :]
            dk = lax.dot_general(ds, q4, (TN, ((), ())), preferred_element_type=F32)
            dv = lax.dot_general(p.astype(_MXU), dob, (TN, ((), ())), preferred_element_type=F32)
            dk_ref[prows, :] += dk[:ATTN_BLOCK, :]
            dv_ref[prows, :] += dv[:ATTN_BLOCK, :]
            dk_ref[rows, :] += dk[ATTN_BLOCK:, :]
            dv_ref[rows, :] += dv[ATTN_BLOCK:, :]
            return acc - jnp.exp(sink - lse) * delta

        acc = lax.fori_loop(0, nb, blk, jnp.zeros((GQA * ATTN_BLOCK, 1), F32))
        for g in range(GQA):
            dsk_ref[:, g:g + 1] = jnp.sum(acc[g * ATTN_BLOCK:(g + 1) * ATTN_BLOCK, :], 0, keepdims=True)

    kv_spec = pl.BlockSpec((None, s, HEAD_DIM), lambda h: (h, 0, 0))
    qcols = pl.BlockSpec((s, gw), lambda h: (0, h))
    return pl.pallas_call(
        body, grid=(nkv,),
        in_specs=[pl.BlockSpec(memory_space=pltpu.SMEM), qcols, kv_spec, kv_spec, qcols, qcols,
                  pl.BlockSpec((None, s, GQA), lambda h: (h, 0, 0))],
        out_specs=[qcols, kv_spec, kv_spec, pl.BlockSpec((None, 1, GQA), lambda h: (h, 0, 0))],
        out_shape=[_sds((s, nkv * gw), F32), _sds((nkv, s, HEAD_DIM), F32), _sds((nkv, s, HEAD_DIM), F32),
                   _sds((nkv, 1, GQA), F32)],
        compiler_params=_cp(), name=name,
    )(sinks, q, k, v, o, dmix, lse)


def _shift_down(a, k, t):
    return jnp.where(t >= k, pltpu.roll(a, k, 0), 0.0)


def _shift_up(a, k, t):
    n = a.shape[0]
    return jnp.where(t < n - k, pltpu.roll(a, n - k, 0), 0.0)


def _pool_fwd(name, h, col_block, pool_w, pool_scale):
    s = h.shape[0]
    ng, pg = pool_w.shape[0], pool_w.shape[1]
    pw_ = ng * pg

    def body(u_ref, w_ref, sc_ref, y_ref, pre_ref):
        t = lax.broadcasted_iota(jnp.int32, (s, pg), 0)
        for gi, win in enumerate(POOL_WINDOWS):
            cols = slice(gi * pg, (gi + 1) * pg)
            u = u_ref[:, cols]
            a = u
            k = 1
            while k < win:
                a = a + _shift_down(a, k, t)
                k *= 2
            div = jnp.minimum(t + 1, win).astype(F32)
            pre = (a / div - u).astype(_MXU)
            pre_ref[:, cols] = pre
            out = jnp.dot(pre, w_ref[gi], preferred_element_type=F32)
            y_ref[:, cols] = (out * sc_ref[:, cols]).astype(y_ref.dtype)

    blk = pl.BlockSpec((s, pw_), lambda i: (0, 0))
    return pl.pallas_call(
        body, grid=(1,),
        in_specs=[pl.BlockSpec((s, pw_), lambda i: (0, col_block)), pl.BlockSpec((ng, pg, pg), lambda i: (0, 0, 0)),
                  pl.BlockSpec((1, pw_), lambda i: (0, 0))],
        out_specs=[blk, blk], out_shape=[_sds((s, pw_), _MXU), _sds((s, pw_), _MXU)], compiler_params=_cp(), name=name,
    )(h, pool_w, pool_scale)


def _pool_bwd(name, dmix, col_block, pre, pool_w, pool_scale):
    s = pre.shape[0]
    ng, pg = pool_w.shape[0], pool_w.shape[1]
    pw_ = ng * pg

    def body(dy_ref, pre_ref, w_ref, sc_ref, du_ref, dw_ref, dsc_ref):
        t = lax.broadcasted_iota(jnp.int32, (s, pg), 0)
        for gi, win in enumerate(POOL_WINDOWS):
            cols = slice(gi * pg, (gi + 1) * pg)
            pre_g = pre_ref[:, cols]
            dy = dy_ref[:, cols]
            out = jnp.dot(pre_g, w_ref[gi], preferred_element_type=F32)
            dsc_ref[:, cols] = jnp.sum(dy * out, 0, keepdims=True)
            dout = (dy * sc_ref[:, cols]).astype(_MXU)
            dw_ref[gi] = lax.dot_general(pre_g, dout, (TN, ((), ())), preferred_element_type=F32)
            dpre = lax.dot_general(dout, w_ref[gi], (NT, ((), ())), preferred_element_type=F32)
            div = jnp.minimum(t + 1, win).astype(F32)
            a = dpre / div
            k = 1
            while k < win:
                a = a + _shift_up(a, k, t)
                k *= 2
            du_ref[:, cols] = (a - dpre).astype(du_ref.dtype)

    blk = pl.BlockSpec((s, pw_), lambda i: (0, 0))
    wspec = pl.BlockSpec((ng, pg, pg), lambda i: (0, 0, 0))
    vec = pl.BlockSpec((1, pw_), lambda i: (0, 0))
    return pl.pallas_call(
        body, grid=(1,), in_specs=[pl.BlockSpec((s, pw_), lambda i: (0, col_block)), blk, wspec, vec],
        out_specs=[blk, wspec, vec], out_shape=[_sds((s, pw_), _MXU), _sds((ng, pg, pg), F32), _sds((1, pw_), F32)],
        compiler_params=_cp(), name=name,
    )(dmix, pre, pool_w, pool_scale)


def _scan_chunks(xr_ref, xi_ref, sr_ref, si_ref, ar, ai, reverse):
    n, c = xr_ref.shape
    tt = min(SCAN_T, n)
    lw = min(SCAN_LANES, c)
    nchunk = n // tt
    t = lax.broadcasted_iota(jnp.int32, (tt, lw), 0)

    for l0 in range(0, c, lw):
        lanes = slice(l0, l0 + lw)
        a_r, a_i = ar[:, lanes], ai[:, lanes]

        def local(vr, vi, a_r=a_r, a_i=a_i):
            pr, pi = a_r, a_i
            k = 1
            while k < tt:
                if reverse:
                    hr, hi = _shift_up(vr, k, t), _shift_up(vi, k, t)
                else:
                    hr, hi = _shift_down(vr, k, t), _shift_down(vi, k, t)
                vr, vi = vr + pr * hr - pi * hi, vi + pr * hi + pi * hr
                pr, pi = pr * pr - pi * pi, 2.0 * pr * pi
                k *= 2
            return vr, vi

        edge = tt - 1 if reverse else 0
        pw_r, pw_i = local(jnp.where(t == edge, a_r, 0.0), jnp.where(t == edge, a_i, 0.0))
        last = 0 if reverse else tt - 1

        def body(i, carry, lanes=lanes, local=local, pw_r=pw_r, pw_i=pw_i):
            cr, ci = carry
            ch = nchunk - 1 - i if reverse else i
            rows = pl.ds(pl.multiple_of(ch * tt, tt), tt)
            vr, vi = local(xr_ref[rows, lanes], xi_ref[rows, lanes])
            vr2 = vr + pw_r * cr - pw_i * ci
            vi2 = vi + pw_r * ci + pw_i * cr
            sr_ref[rows, lanes] = vr2
            si_ref[rows, lanes] = vi2
            return vr2[last:last + 1, :], vi2[last:last + 1, :]

        lax.fori_loop(0, nchunk, body, (jnp.zeros((1, lw), F32), jnp.zeros((1, lw), F32)))


_GELU_K = math.sqrt(2.0 / math.pi)


def _gelu_grad(y):
    inner = _GELU_K * (y + 0.044715 * y * y * y)
    th = jnp.tanh(inner)
    return 0.5 * (1.0 + th) + 0.5 * y * (1.0 - th * th) * _GELU_K * (1.0 + 3.0 * 0.044715 * y * y)


def _ssm_fwd(name, h, u_block0, bdr, bdi, cdr, cdi, dvec, ar, ai):
    s = h.shape[0]
    nt, cw, lw = bdr.shape
    rc = min(256, s)

    def body(u_ref, bdr_ref, bdi_ref, cdr_ref, cdi_ref, d_ref, ar_ref, ai_ref, sr_ref, si_ref, y_ref, yg_ref):
        def mm_in(c, _):
            rows = pl.ds(pl.multiple_of(c * rc, rc), rc)
            ub = u_ref[rows, :].astype(_MXU)
            sr_ref[rows, :] = jnp.dot(ub, bdr_ref[...], preferred_element_type=F32)
            si_ref[rows, :] = jnp.dot(ub, bdi_ref[...], preferred_element_type=F32)
            return 0

        lax.fori_loop(0, s // rc, mm_in, 0)
        _scan_chunks(sr_ref, si_ref, sr_ref, si_ref, ar_ref[...], ai_ref[...], reverse=False)

        def mm_out(c, _):
            rows = pl.ds(pl.multiple_of(c * rc, rc), rc)
            y = (jnp.dot(sr_ref[rows, :].astype(_MXU), cdr_ref[...], preferred_element_type=F32)
                 - jnp.dot(si_ref[rows, :].astype(_MXU), cdi_ref[...], preferred_element_type=F32)
                 + d_ref[...] * u_ref[rows, :])
            y_ref[rows, :] = y
            yg_ref[rows, :] = jax.nn.gelu(y).astype(yg_ref.dtype)
            return 0

        lax.fori_loop(0, s // rc, mm_out, 0)

    st = pl.BlockSpec((s, lw), lambda j: (0, j))
    ch = pl.BlockSpec((s, cw), lambda j: (0, j))
    bspec = pl.BlockSpec((None, cw, lw), lambda j: (j, 0, 0))
    cspec = pl.BlockSpec((None, lw, cw), lambda j: (j, 0, 0))
    return pl.pallas_call(
        body, grid=(nt,),
        in_specs=[pl.BlockSpec((s, cw), lambda j: (0, u_block0 + j)), bspec, bspec, cspec, cspec,
                  pl.BlockSpec((1, cw), lambda j: (0, j)), pl.BlockSpec((1, lw), lambda j: (0, j)),
                  pl.BlockSpec((1, lw), lambda j: (0, j))],
        out_specs=[st, st, ch, ch],
        out_shape=[_sds((s, nt * lw), F32), _sds((s, nt * lw), F32), _sds((s, nt * cw), F32), _sds((s, nt * cw), _MXU)],
        compiler_params=_cp(), name=name,
    )(h, bdr, bdi, cdr, cdi, dvec, ar, ai)


def _ssm_bwd(name, dyg, ypre, h, u_block0, sr, si, bdr, bdi, cdr, cdi, dvec, ar, ai):
    s = h.shape[0]
    nt, cw, lw = bdr.shape
    rc = min(256, s)

    def body(dyg_ref, yp_ref, u_ref, sr_ref, si_ref, bdr_ref, bdi_ref, cdr_ref, cdi_ref, d_ref, ar_ref, ai_ref,
             du_ref, dd_ref, dcr_ref, dci_ref, dbr_ref, dbi_ref, dar_ref, dai_ref, lr_scr, li_scr, dy_scr):
        for ref in (dd_ref, dcr_ref, dci_ref, dbr_ref, dbi_ref, dar_ref, dai_ref):
            ref[...] = jnp.zeros_like(ref)

        def p1(c, _):
            rows = pl.ds(pl.multiple_of(c * rc, rc), rc)
            dy = dyg_ref[rows, :] * _gelu_grad(yp_ref[rows, :])
            dy_scr[rows, :] = dy
            dd_ref[...] += jnp.sum(dy * u_ref[rows, :], 0, keepdims=True)
            dyb = dy.astype(_MXU)
            lr_scr[rows, :] = lax.dot_general(dyb, cdr_ref[...], (NT, ((), ())), preferred_element_type=F32)
            li_scr[rows, :] = -lax.dot_general(dyb, cdi_ref[...], (NT, ((), ())), preferred_element_type=F32)
            dcr_ref[...] += lax.dot_general(sr_ref[rows, :].astype(_MXU), dyb, (TN, ((), ())), preferred_element_type=F32)
            dci_ref[...] -= lax.dot_general(si_ref[rows, :].astype(_MXU), dyb, (TN, ((), ())), preferred_element_type=F32)
            return 0

        lax.fori_loop(0, s // rc, p1, 0)
        _scan_chunks(lr_scr, li_scr, lr_scr, li_scr, ar_ref[...], -ai_ref[...], reverse=True)
        t = lax.broadcasted_iota(jnp.int32, (rc, lw), 0)

        def p2(c, _):
            r0 = pl.multiple_of(c * rc, rc)
            rows = pl.ds(r0, rc)
            before = pl.ds(pl.multiple_of(jnp.maximum(r0 - 8, 0), 8), 8)
            have = (c > 0).astype(F32)
            lr, li = lr_scr[rows, :], li_scr[rows, :]
            spr = jnp.where(t == 0, sr_ref[before, :][7:8, :] * have, pltpu.roll(sr_ref[rows, :], 1, 0))
            spi = jnp.where(t == 0, si_ref[before, :][7:8, :] * have, pltpu.roll(si_ref[rows, :], 1, 0))
            dar_ref[...] += jnp.sum(lr * spr + li * spi, 0, keepdims=True)
            dai_ref[...] += jnp.sum(li * spr - lr * spi, 0, keepdims=True)
            lrb, lib = lr.astype(_MXU), li.astype(_MXU)
            du = (dy_scr[rows, :] * d_ref[...]
                  + lax.dot_general(lrb, bdr_ref[...], (NT, ((), ())), preferred_element_type=F32)
                  + lax.dot_general(lib, bdi_ref[...], (NT, ((), ())), preferred_element_type=F32))
            du_ref[rows, :] = du.astype(du_ref.dtype)
            ub = u_ref[rows, :].astype(_MXU)
            dbr_ref[...] += lax.dot_general(ub, lrb, (TN, ((), ())), preferred_element_type=F32)
            dbi_ref[...] += lax.dot_general(ub, lib, (TN, ((), ())), preferred_element_type=F32)
            return 0

        lax.fori_loop(0, s // rc, p2, 0)

    st = pl.BlockSpec((s, lw), lambda j: (0, j))
    ch = pl.BlockSpec((s, cw), lambda j: (0, j))
    bspec = pl.BlockSpec((None, cw, lw), lambda j: (j, 0, 0))
    cspec = pl.BlockSpec((None, lw, cw), lambda j: (j, 0, 0))
    cvec = pl.BlockSpec((1, cw), lambda j: (0, j))
    svec = pl.BlockSpec((1, lw), lambda j: (0, j))
    return pl.pallas_call(
        body, grid=(nt,),
        in_specs=[ch, ch, pl.BlockSpec((s, cw), lambda j: (0, u_block0 + j)), st, st, bspec, bspec, cspec, cspec, cvec, svec, svec],
        out_specs=[ch, cvec, cspec, cspec, bspec, bspec, svec, svec],
        out_shape=[_sds((s, nt * cw), _MXU), _sds((1, nt * cw), F32), _sds((nt, lw, cw), F32), _sds((nt, lw, cw), F32),
                   _sds((nt, cw, lw), F32), _sds((nt, cw, lw), F32), _sds((1, nt * lw), F32), _sds((1, nt * lw), F32)],
        scratch_shapes=[pltpu.VMEM((s, lw), F32), pltpu.VMEM((s, lw), F32), pltpu.VMEM((s, cw), F32)],
        compiler_params=_cp(), name=name,
    )(dyg, ypre, h, sr, si, bdr, bdi, cdr, cdi, dvec, ar, ai)


def _glu_fwd(name, yg, gw):
    s, w = yg.shape
    tr = min(512, s)

    def body(y_ref, w_ref, o_ref, ab_ref):
        ab = jnp.dot(y_ref[...], w_ref[...], preferred_element_type=F32)
        ab_ref[...] = ab
        o_ref[...] = (ab[:, :w] * jax.nn.sigmoid(ab[:, w:])).astype(o_ref.dtype)

    return pl.pallas_call(
        body, grid=(s // tr,), in_specs=[pl.BlockSpec((tr, w), lambda i: (i, 0)), _resident((w, 2 * w), lambda i: (0, 0))],
        out_specs=[pl.BlockSpec((tr, w), lambda i: (i, 0)), pl.BlockSpec((tr, 2 * w), lambda i: (i, 0))],
        out_shape=[_sds((s, w), _MXU), _sds((s, 2 * w), F32)], compiler_params=_cp(), name=name,
    )(yg, gw)


def _glu_bwd(name, dmix, col_block, ab, gw):
    s = ab.shape[0]
    w = ab.shape[1] // 2
    tr = min(512, s)

    def body(do_ref, ab_ref, w_ref, dab_ref, dy_ref):
        do = do_ref[...]
        a, b = ab_ref[:, :w], ab_ref[:, w:]
        sg = jax.nn.sigmoid(b)
        da = (do * sg).astype(_MXU)
        db = (do * a * sg * (1.0 - sg)).astype(_MXU)
        dab_ref[:, :w] = da
        dab_ref[:, w:] = db
        dy_ref[...] = (lax.dot_general(da, w_ref[:, :w], (NT, ((), ())), preferred_element_type=F32)
                       + lax.dot_general(db, w_ref[:, w:], (NT, ((), ())), preferred_element_type=F32))

    return pl.pallas_call(
        body, grid=(s // tr,),
        in_specs=[pl.BlockSpec((tr, w), lambda i: (i, col_block)), pl.BlockSpec((tr, 2 * w), lambda i: (i, 0)),
                  _resident((w, 2 * w), lambda i: (0, 0))],
        out_specs=[pl.BlockSpec((tr, 2 * w), lambda i: (i, 0)), pl.BlockSpec((tr, w), lambda i: (i, 0))],
        out_shape=[_sds((s, 2 * w), _MXU), _sds((s, w), F32)], compiler_params=_cp(), name=name,
    )(dmix, ab, gw)


CONV_ROWS = 64


def _conv_chunk(ref, w_ref, b_ref, c, tt):
    r0 = pl.multiple_of(c * tt, tt)
    before = ref[pl.ds(pl.multiple_of(jnp.maximum(r0 - 8, 0), 8), 8), :]
    before = jnp.where(c > 0, before, 0.0)
    main = ref[pl.ds(r0, tt), :]
    ext = jnp.concatenate([before, main], 0)
    d1 = pltpu.roll(ext, 1, 0)[8:, :]
    d2 = pltpu.roll(ext, 2, 0)[8:, :]
    hc = b_ref[...] + d2 * w_ref[0:1, :]
    hc = hc + d1 * w_ref[1:2, :]
    return hc + main * w_ref[2:3, :], main, d1, d2


def _conv_act_fwd(name, hu, cw, cb):
    s, f2 = hu.shape
    f = f2 // 2
    tw = _col_tile(f, 256)
    nt = f // tw

    tt = min(CONV_ROWS, s)

    def body(v_ref, g_ref, wv_ref, wg_ref, bv_ref, bg_ref, act_ref):
        def chunk(c, _):
            val = _conv_chunk(v_ref, wv_ref, bv_ref, c, tt)[0]
            gate = _conv_chunk(g_ref, wg_ref, bg_ref, c, tt)[0]
            act_ref[pl.ds(pl.multiple_of(c * tt, tt), tt), :] = (jax.nn.silu(gate) * val).astype(act_ref.dtype)
            return 0

        lax.fori_loop(0, s // tt, chunk, 0)

    cv = lambda rows: pl.BlockSpec((rows, tw), lambda i: (0, i))
    cg = lambda rows: pl.BlockSpec((rows, tw), lambda i: (0, nt + i))
    return pl.pallas_call(
        body, grid=(nt,), in_specs=[cv(s), cg(s), cv(CONV_WIDTH), cg(CONV_WIDTH), cv(1), cg(1)],
        out_specs=cv(s), out_shape=_sds((s, f), _MXU), compiler_params=_cp(), name=name,
    )(hu, hu, cw, cw, cb, cb)


def _conv_act_bwd(name, dact, hu, cw, cb):
    s, f2 = hu.shape
    f = f2 // 2
    tw = _col_tile(f, 256)
    nt = f // tw

    tt = min(CONV_ROWS, s)
    nchunk = s // tt

    def body(da_ref, v_ref, g_ref, wv_ref, wg_ref, bv_ref, bg_ref, dh_ref, dwv_ref, dwg_ref, dbv_ref, dbg_ref):
        def chunk(i, carry):
            c = nchunk - 1 - i
            rows = pl.ds(pl.multiple_of(c * tt, tt), tt)
            val, hv, hv1, hv2 = _conv_chunk(v_ref, wv_ref, bv_ref, c, tt)
            gate, hg, hg1, hg2 = _conv_chunk(g_ref, wg_ref, bg_ref, c, tt)
            sg = jax.nn.sigmoid(gate)
            da = da_ref[rows, :]
            dval = da * (gate * sg)
            dgate = da * val * sg * (1.0 + gate * (1.0 - sg))
            out = []
            for part, dhc, taps, w_ref, (after, acc) in ((0, dval, (hv2, hv1, hv), wv_ref, carry[0]),
                                                        (1, dgate, (hg2, hg1, hg), wg_ref, carry[1])):
                ext = jnp.concatenate([dhc, after], 0)
                u1 = pltpu.roll(ext, tt + 8 - 1, 0)[:tt, :]
                u2 = pltpu.roll(ext, tt + 8 - 2, 0)[:tt, :]
                dh = dhc * w_ref[2:3, :] + u1 * w_ref[1:2, :] + u2 * w_ref[0:1, :]
                dh_ref[part, rows, :] = dh.astype(dh_ref.dtype)
                sums = [jnp.sum(dhc * tap, 0, keepdims=True) for tap in taps] + [jnp.sum(dhc, 0, keepdims=True)]
                out.append((dhc[0:8, :], tuple(a + b for a, b in zip(acc, sums))))
            return tuple(out)

        zero = (jnp.zeros((8, tw), F32), tuple(jnp.zeros((1, tw), F32) for _ in range(CONV_WIDTH + 1)))
        (_, acc_v), (_, acc_g) = lax.fori_loop(0, nchunk, chunk, (zero, zero))
        for acc, dw_ref, db_ref in ((acc_v, dwv_ref, dbv_ref), (acc_g, dwg_ref, dbg_ref)):
            for tap in range(CONV_WIDTH):
                dw_ref[tap:tap + 1, :] = acc[tap]
            db_ref[...] = acc[CONV_WIDTH]

    cv = lambda rows: pl.BlockSpec((rows, tw), lambda i: (0, i))
    cg = lambda rows: pl.BlockSpec((rows, tw), lambda i: (0, nt + i))
    both = pl.BlockSpec((2, s, tw), lambda i: (0, 0, i))
    dh, dwv, dwg, dbv, dbg = pl.pallas_call(
        body, grid=(nt,), in_specs=[cv(s), cv(s), cg(s), cv(CONV_WIDTH), cg(CONV_WIDTH), cv(1), cg(1)],
        out_specs=[both, cv(CONV_WIDTH), cv(CONV_WIDTH), cv(1), cv(1)],
        out_shape=[_sds((2, s, f), _MXU), _sds((CONV_WIDTH, f), F32), _sds((CONV_WIDTH, f), F32),
                   _sds((1, f), F32), _sds((1, f), F32)],
        compiler_params=_cp(), name=name,
    )(dact, hu, hu, cw, cw, cb, cb)
    return dh, jnp.concatenate([dwv, dwg], 1), jnp.concatenate([dbv, dbg], 1)


ELEM_BLOCK = 512 * 1024


def _elem_tiles(r, c, budget=ELEM_BLOCK):
    rows = [t for t in range(8, r + 1, 8) if r % t == 0] or [r]
    cols = [t for t in range(V7X_LANES, c + 1, V7X_LANES) if c % t == 0] or [c]
    fits = [(tr * tc, tc, tr) for tr in rows for tc in cols if tr * tc <= budget]
    if not fits:
        return min(rows), min(cols)
    _, tc, tr = max(fits)
    return tr, tc


def _reduce_adamw(name, parts, w, m, v):
    n, nl, r, c = parts.shape
    tr, tc = _elem_tiles(r, c, ELEM_BLOCK // n)
    c1 = 1.0 - ADAM_B1 ** ADAM_STEP
    c2 = 1.0 - ADAM_B2 ** ADAM_STEP

    def body(p_ref, w_ref, m_ref, v_ref, g_ref, d_ref, nm_ref, nv_ref):
        g = p_ref[0].astype(F32)
        for i in range(1, n):
            g = g + p_ref[i].astype(F32)
        nm = ADAM_B1 * m_ref[...] + (1.0 - ADAM_B1) * g
        nv = ADAM_B2 * v_ref[...] + (1.0 - ADAM_B2) * (g * g)
        m_hat = nm / c1
        v_hat = nv / c2
        g_ref[...] = g
        nm_ref[...] = nm
        nv_ref[...] = nv
        d_ref[...] = -ADAM_LR * (m_hat / (jnp.sqrt(v_hat) + ADAM_EPS) + ADAM_WD * w_ref[...])

    blk = pl.BlockSpec((None, tr, tc), lambda l, i, j: (l, i, j))
    out = _sds((nl, r, c), F32)
    return pl.pallas_call(
        body, grid=(nl, r // tr, c // tc),
        in_specs=[pl.BlockSpec((n, None, tr, tc), lambda l, i, j: (0, l, i, j)), blk, blk, blk],
        out_specs=[blk, blk, blk, blk], out_shape=[out, out, out, out], compiler_params=_cp(), name=name,
    )(parts, w, m, v)


def _pair_sum(name, mine, theirs, c_idx):
    _, _, r, c = mine.shape
    tr, tc = _elem_tiles(r, c)

    def body(c_ref, a_ref, b_ref, o_ref):
        o_ref[...] = (a_ref[...].astype(F32) + b_ref[...].astype(F32)).astype(o_ref.dtype)

    return pl.pallas_call(
        body,
        grid_spec=pltpu.PrefetchScalarGridSpec(
            num_scalar_prefetch=1, grid=(4, r // tr, c // tc),
            in_specs=[pl.BlockSpec((None, None, tr, tc), lambda p, i, j, cref: (p, cref[0], i, j)),
                      pl.BlockSpec((None, tr, tc), lambda p, i, j, cref: (p, i, j))],
            out_specs=pl.BlockSpec((None, tr, tc), lambda p, i, j, cref: (p, i, j))),
        out_shape=_sds((4, r, c), _WIRE), compiler_params=_cp(), name=name,
    )(c_idx, mine, theirs)


def _place():
    return lax.axis_index("x"), lax.axis_index("y"), lax.axis_index("c")


def _all_gather(name, xs):
    n = len(xs)

    def body(*refs):
        x_refs, o_refs = refs[:n], refs[n:2 * n]
        send_sems, recv_sems, local_sems = refs[2 * n:]
        x, y, c = _place()
        me, sibling = (x, y, c), (x, y, 1 - c)
        chips = [(1 - x, y), (x, 1 - y), (1 - x, 1 - y)]

        def copy(a, k, block, to, src=None):
            px, py, pc = block
            rows = o_refs[a].at[4 * px + 2 * py + pc]
            return pltpu.make_async_remote_copy(
                src_ref=rows if src is None else src, dst_ref=rows, send_sem=send_sems.at[a, k], recv_sem=recv_sems.at[a, k],
                device_id=to, device_id_type=MESH)

        sent = []
        mine = []
        for a in range(n):
            mx, my, mc = me
            cp = pltpu.make_async_copy(x_refs[a], o_refs[a].at[4 * mx + 2 * my + mc], local_sems.at[a])
            cp.start()
            mine.append(cp)
            first = [copy(a, 0, me, sibling, src=x_refs[a])]
            first += [copy(a, 1 + j, me, (*chip, c), src=x_refs[a]) for j, chip in enumerate(chips)]
            for cp in first:
                cp.start()
            sent += first
        for a in range(n):
            for j, chip in enumerate(chips):
                copy(a, 1 + j, (*chip, c), me).wait_recv()
                fwd = copy(a, 4 + j, (*chip, c), sibling)
                fwd.start()
                sent.append(fwd)
        for a in range(n):
            copy(a, 0, sibling, me).wait_recv()
            for j, chip in enumerate(chips):
                copy(a, 4 + j, (*chip, 1 - c), me).wait_recv()
        for cp in sent:
            cp.wait_send()
        for cp in mine:
            cp.wait()

    return pl.pallas_call(
        body, in_specs=[ANY] * n, out_specs=[ANY] * n,
        out_shape=[_sds((N_DEV,) + a.shape, a.dtype) for a in xs],
        scratch_shapes=[pltpu.SemaphoreType.DMA((n, 7)), pltpu.SemaphoreType.DMA((n, 7)), pltpu.SemaphoreType.DMA((n,))],
        name=name,
    )(*xs)


HBM = pl.BlockSpec(memory_space=pltpu.HBM)
SEM = pl.BlockSpec(memory_space=pltpu.SEMAPHORE)
DATAFLOW = pltpu.SideEffectType.DATAFLOW_SIDE_EFFECTING


def _in_hbm(a):
    return pltpu.with_memory_space_constraint(a, pltpu.HBM)


def _split_copy_start(name, srcs, lands, copies, deps):
    n, nd = len(srcs), len(deps)
    per = len(copies([None] * n, [None] * n, probe=True)) // n

    def body(*refs):
        s_refs, l_refs = refs[:n], refs[n:2 * n]
        send_sems, recv_sems = refs[2 * n + nd], refs[2 * n + nd + 1]
        token = refs[-1]
        for a, k, src, dst, to in copies(s_refs, l_refs):
            pltpu.make_async_remote_copy(src_ref=src, dst_ref=dst, send_sem=send_sems.at[a * per + k],
                                         recv_sem=recv_sems.at[a * per + k], device_id=to, device_id_type=MESH).start()
        token[...] = jnp.zeros_like(token)

    both = list(srcs) + list(lands)
    outs = pl.pallas_call(
        body, name=name,
        out_shape=(pltpu.SemaphoreType.DMA((n * per,)), pltpu.SemaphoreType.DMA((n * per,)),
                   *[pltpu.HBM(a.shape, a.dtype) for a in both], _sds((8, V7X_LANES), F32)),
        in_specs=[HBM] * (2 * n) + [ANY] * nd,
        out_specs=(SEM, SEM, *[HBM] * (2 * n), pl.BlockSpec(memory_space=pltpu.VMEM)),
        input_output_aliases={i: 2 + i for i in range(2 * n)},
        compiler_params=pltpu.CompilerParams(has_side_effects=DATAFLOW),
    )(*[_in_hbm(a) for a in both], *deps)
    return outs[0], outs[1], list(outs[2:2 + n]), list(outs[2 + n:2 + 2 * n]), outs[-1]


def _split_copy_wait(name, send_sems, recv_sems, srcs, lands, arrivals, after):
    n = len(srcs)
    per = len(arrivals([None] * n, [None] * n, probe=True)) // n

    def body(*refs):
        s_refs, l_refs = refs[:n], refs[n:2 * n]
        send_sems_, recv_sems_ = refs[2 * n], refs[2 * n + 1]
        for a, k, src, dst, frm in arrivals(s_refs, l_refs):
            cp = pltpu.make_async_remote_copy(src_ref=src, dst_ref=dst, send_sem=send_sems_.at[a * per + k],
                                              recv_sem=recv_sems_.at[a * per + k], device_id=frm, device_id_type=MESH)
            cp.wait_send()
            cp.wait_recv()

    both = list(srcs) + list(lands)
    outs = pl.pallas_call(
        body, name=name, out_shape=tuple(pltpu.HBM(a.shape, a.dtype) for a in both),
        in_specs=[HBM] * (2 * n) + [SEM, SEM, ANY], out_specs=tuple([HBM] * (2 * n)),
        input_output_aliases={i: i for i in range(2 * n)},
        compiler_params=pltpu.CompilerParams(has_side_effects=DATAFLOW),
    )(*both, send_sems, recv_sems, after)
    return list(outs[:n]), list(outs[n:])


def _gather_copies(arriving):
    def copies(s_refs, l_refs, probe=False):
        if probe:
            return [None] * (4 * len(s_refs))
        x, y, c = _place()
        out = []
        for a in range(len(s_refs)):
            for k, (px, py, pc) in enumerate([(x, y, 1 - c), (1 - x, y, c), (x, 1 - y, c), (1 - x, 1 - y, c)]):
                slot = 4 * px + 2 * py + pc if arriving else 4 * x + 2 * y + c
                out.append((a, k, s_refs[a], l_refs[a].at[slot], (px, py, pc)))
        return out
    return copies


def _chip_copies(arriving):
    def copies(s_refs, l_refs, probe=False):
        if probe:
            return [None] * (3 * len(s_refs))
        x, y, c = _place()
        out = []
        for a in range(len(s_refs)):
            for j, (px, py) in enumerate([(1 - x, y), (x, 1 - y), (1 - x, 1 - y)]):
                src = s_refs[a].at[2 * x + y] if arriving else s_refs[a].at[2 * px + py]
                out.append((a, j, src, l_refs[a].at[j], (px, py, c)))
        return out
    return copies


def _sibling_copies(s_refs, l_refs, probe=False):
    if probe:
        return [None] * (4 * len(s_refs))
    x, y, c = _place()
    return [(a, p, s_refs[a].at[p, 1 - c], l_refs[a].at[p], (x, y, 1 - c)) for a in range(len(s_refs)) for p in range(4)]


def _sibling_begin(name, by_owner, deps):
    lands = [lax.empty((4,) + a.shape[2:], a.dtype) for a in by_owner]
    return _split_copy_start(name + "_start", by_owner, lands, _sibling_copies, deps)


def _sibling_end(name, handle, after):
    send_sems, recv_sems, srcs, lands, _ = handle
    return _split_copy_wait(name + "_wait", send_sems, recv_sems, srcs, lands, _sibling_copies, after)


def _gather_begin(name, shards, deps):
    x, y, c = _place()
    lands = [lax.dynamic_update_slice_in_dim(lax.empty((N_DEV,) + a.shape, a.dtype), a[None], 4 * x + 2 * y + c, 0)
             for a in shards]
    return _split_copy_start(name + "_start", shards, lands, _gather_copies(False), deps)


def _gather_end(name, handle, after):
    send_sems, recv_sems, srcs, lands, _ = handle
    _, lands = _split_copy_wait(name + "_wait", send_sems, recv_sems, srcs, lands, _gather_copies(True), after)
    return _gather_forward(name + "_forward", lands)


def _gather_forward(name, lands):
    n = len(lands)

    def body(*refs):
        o_refs = refs[n:2 * n]
        send_sems, recv_sems = refs[2 * n:]
        x, y, c = _place()
        sibling = (x, y, 1 - c)
        chips = [(1 - x, y), (x, 1 - y), (1 - x, 1 - y)]
        sent = []
        for a in range(n):
            for j, (px, py) in enumerate(chips):
                rows = o_refs[a].at[4 * px + 2 * py + c]
                cp = pltpu.make_async_remote_copy(src_ref=rows, dst_ref=rows, send_sem=send_sems.at[a, j],
                                                  recv_sem=recv_sems.at[a, j], device_id=sibling, device_id_type=MESH)
                cp.start()
                sent.append(cp)
        for a in range(n):
            for j, (px, py) in enumerate(chips):
                rows = o_refs[a].at[4 * px + 2 * py + 1 - c]
                pltpu.make_async_remote_copy(src_ref=rows, dst_ref=rows, send_sem=send_sems.at[a, j],
                                             recv_sem=recv_sems.at[a, j], device_id=sibling, device_id_type=MESH).wait_recv()
        for cp in sent:
            cp.wait_send()

    return pl.pallas_call(
        body, in_specs=[ANY] * n, out_specs=[ANY] * n, out_shape=[_sds(a.shape, a.dtype) for a in lands],
        input_output_aliases={i: i for i in range(n)},
        scratch_shapes=[pltpu.SemaphoreType.DMA((n, 3)), pltpu.SemaphoreType.DMA((n, 3))], name=name,
    )(*lands)


def _chips_begin(name, pairs, deps):
    lands = [lax.empty((3,) + a.shape[1:], a.dtype) for a in pairs]
    return _split_copy_start(name + "_start", pairs, lands, _chip_copies(False), deps)


def _chips_end(name, handle, after):
    send_sems, recv_sems, srcs, lands, _ = handle
    return _split_copy_wait(name + "_wait", send_sems, recv_sems, srcs, lands, _chip_copies(True), after)


def _adamw_layer(name, l, own, lands, w, m, v, prev):
    nl, ng, r, c = w.shape
    tr, tc = _elem_tiles(r, c)
    c1 = 1.0 - ADAM_B1 ** ADAM_STEP
    c2 = 1.0 - ADAM_B2 ** ADAM_STEP

    def body(own_ref, lands_ref, w_ref, m_ref, v_ref, *rest):
        g_ref, d_ref, nm_ref, nv_ref = rest[-4:]
        g = own_ref[...].astype(F32) + lands_ref[0].astype(F32) + lands_ref[1].astype(F32) + lands_ref[2].astype(F32)
        nm = ADAM_B1 * m_ref[...] + (1.0 - ADAM_B1) * g
        nv = ADAM_B2 * v_ref[...] + (1.0 - ADAM_B2) * (g * g)
        m_hat = nm / c1
        v_hat = nv / c2
        g_ref[...] = g
        nm_ref[...] = nm
        nv_ref[...] = nv
        d_ref[...] = -ADAM_LR * (m_hat / (jnp.sqrt(v_hat) + ADAM_EPS) + ADAM_WD * w_ref[...])

    lay = pl.BlockSpec((None, None, tr, tc), lambda g, i, j: (l, g, i, j))
    out = _sds((nl, ng, r, c), F32)
    prev = [] if prev is None else list(prev)
    return pl.pallas_call(
        body, grid=(ng, r // tr, c // tc),
        in_specs=[pl.BlockSpec((None, tr, tc), lambda g, i, j: (g, i, j)),
                  pl.BlockSpec((3, None, tr, tc), lambda g, i, j: (0, g, i, j)), lay, lay, lay] + [ANY] * len(prev),
        out_specs=[lay] * 4, out_shape=[out] * 4, input_output_aliases={5 + i: i for i in range(len(prev))},
        compiler_params=_cp(), name=name,
    )(own, lands, w, m, v, *prev)


def _pad_pairs(a, axis, half, half_pad):
    shp = a.shape
    a = a.reshape(shp[:axis] + (2, half) + shp[axis + 1:])
    pad = [(0, 0)] * a.ndim
    pad[axis + 1] = (0, half_pad - half)
    a = jnp.pad(a, pad)
    return a.reshape(shp[:axis] + (2 * half_pad,) + shp[axis + 1:])


def _unpad_pairs(a, axis, half, half_pad):
    shp = a.shape
    a = a.reshape(shp[:axis] + (2, half_pad) + shp[axis + 1:])
    a = lax.slice_in_dim(a, 0, half, axis=axis + 1)
    return a.reshape(shp[:axis] + (2 * half,) + shp[axis + 1:])


def _blockdiag(w, nt):
    g, a, b = w.shape
    gl = g // nt
    e = jnp.eye(gl, dtype=w.dtype).reshape(1, gl, 1, gl, 1)
    return (w.reshape(nt, gl, a, 1, b) * e).reshape(nt, gl * a, gl * b)


def _diagblocks(m, g, a, b):
    nt = m.shape[0]
    gl = g // nt
    d = jnp.diagonal(m.reshape(nt, gl, a, gl, b), axis1=1, axis2=3)
    return jnp.moveaxis(d, -1, 1).reshape(g, a, b)


def _ssm_discretise(lam_re, lam_im, log_dt, b_re, b_im):
    dt = jnp.exp(log_dt)[..., None]
    mag = jnp.exp(lam_re * dt)
    ab_re, ab_im = mag * jnp.cos(lam_im * dt), mag * jnp.sin(lam_im * dt)
    nr, ni = ab_re - 1.0, ab_im
    den = lam_re * lam_re + lam_im * lam_im
    zr = (nr * lam_re + ni * lam_im) / den
    zi = (ni * lam_re - nr * lam_im) / den
    bbr = zr[..., None] * b_re - zi[..., None] * b_im
    bbi = zr[..., None] * b_im + zi[..., None] * b_re
    return ab_re, ab_im, bbr, bbi


def _rope_tables(s):
    half = HEAD_DIM // 2
    inv = ROPE_THETA ** (-jnp.arange(half, dtype=F32) / half)
    ang = jnp.arange(s).astype(F32)[:, None] * inv[None, :]
    cos, sin = jnp.cos(ang), jnp.sin(ang)
    reps = V7X_LANES // HEAD_DIM
    return jnp.tile(jnp.concatenate([cos, cos], -1), (1, reps)), jnp.tile(jnp.concatenate([-sin, sin], -1), (1, reps))


_SMALL = ("attn_sinks", "pool_w", "pool_scale", "ssm_lam_re", "ssm_lam_im", "ssm_log_dt", "ssm_b_re", "ssm_b_im",
          "ssm_c_re", "ssm_c_im", "ssm_d", "ln1_g", "ln1_b", "ffn_conv_b", "ln2_g", "ln2_b")
_BIG = ("w_in", "ssm_glu_w", "w_out", "ffn_w_up", "ffn_w_down")
_ORDER = ("w_in", "attn_sinks", "pool_w", "pool_scale", "ssm_lam_re", "ssm_lam_im", "ssm_log_dt", "ssm_b_re", "ssm_b_im",
          "ssm_c_re", "ssm_c_im", "ssm_d", "ssm_glu_w", "w_out", "ln1_g", "ln1_b", "ffn_w_up", "ffn_conv_w", "ffn_conv_b",
          "ffn_w_down", "ln2_g", "ln2_b")


def kernel(x, w_in, attn_sinks, pool_w, pool_scale, ssm_lam_re, ssm_lam_im, ssm_log_dt, ssm_b_re, ssm_b_im, ssm_c_re, ssm_c_im, ssm_d, ssm_glu_w, w_out, ln1_g, ln1_b, ffn_w_up, ffn_conv_w, ffn_conv_b, ffn_w_down, ln2_g, ln2_b, loss_target, m_w_in, m_attn_sinks, m_pool_w, m_pool_scale, m_ssm_lam_re, m_ssm_lam_im, m_ssm_log_dt, m_ssm_b_re, m_ssm_b_im, m_ssm_c_re, m_ssm_c_im, m_ssm_d, m_ssm_glu_w, m_w_out, m_ln1_g, m_ln1_b, m_ffn_w_up, m_ffn_conv_w, m_ffn_conv_b, m_ffn_w_down, m_ln2_g, m_ln2_b, v_w_in, v_attn_sinks, v_pool_w, v_pool_scale, v_ssm_lam_re, v_ssm_lam_im, v_ssm_log_dt, v_ssm_b_re, v_ssm_b_im, v_ssm_c_re, v_ssm_c_im, v_ssm_d, v_ssm_glu_w, v_w_out, v_ln1_g, v_ln1_b, v_ffn_w_up, v_ffn_conv_w, v_ffn_conv_b, v_ffn_w_down, v_ln2_g, v_ln2_b):
    W = dict(w_in=w_in, attn_sinks=attn_sinks, pool_w=pool_w, pool_scale=pool_scale, ssm_lam_re=ssm_lam_re, ssm_lam_im=ssm_lam_im, ssm_log_dt=ssm_log_dt, ssm_b_re=ssm_b_re, ssm_b_im=ssm_b_im, ssm_c_re=ssm_c_re, ssm_c_im=ssm_c_im, ssm_d=ssm_d, ssm_glu_w=ssm_glu_w, w_out=w_out, ln1_g=ln1_g, ln1_b=ln1_b, ffn_w_up=ffn_w_up, ffn_conv_w=ffn_conv_w, ffn_conv_b=ffn_conv_b, ffn_w_down=ffn_w_down, ln2_g=ln2_g, ln2_b=ln2_b)
    M = dict(w_in=m_w_in, attn_sinks=m_attn_sinks, pool_w=m_pool_w, pool_scale=m_pool_scale, ssm_lam_re=m_ssm_lam_re, ssm_lam_im=m_ssm_lam_im, ssm_log_dt=m_ssm_log_dt, ssm_b_re=m_ssm_b_re, ssm_b_im=m_ssm_b_im, ssm_c_re=m_ssm_c_re, ssm_c_im=m_ssm_c_im, ssm_d=m_ssm_d, ssm_glu_w=m_ssm_glu_w, w_out=m_w_out, ln1_g=m_ln1_g, ln1_b=m_ln1_b, ffn_w_up=m_ffn_w_up, ffn_conv_w=m_ffn_conv_w, ffn_conv_b=m_ffn_conv_b, ffn_w_down=m_ffn_w_down, ln2_g=m_ln2_g, ln2_b=m_ln2_b)
    V = dict(w_in=v_w_in, attn_sinks=v_attn_sinks, pool_w=v_pool_w, pool_scale=v_pool_scale, ssm_lam_re=v_ssm_lam_re, ssm_lam_im=v_ssm_lam_im, ssm_log_dt=v_ssm_log_dt, ssm_b_re=v_ssm_b_re, ssm_b_im=v_ssm_b_im, ssm_c_re=v_ssm_c_re, ssm_c_im=v_ssm_c_im, ssm_d=v_ssm_d, ssm_glu_w=v_ssm_glu_w, w_out=v_w_out, ln1_g=v_ln1_g, ln1_b=v_ln1_b, ffn_w_up=v_ffn_w_up, ffn_conv_w=v_ffn_conv_w, ffn_conv_b=v_ffn_conv_b, ffn_w_down=v_ffn_w_down, ln2_g=v_ln2_g, ln2_b=v_ln2_b)

    depth = w_in.shape[0]
    s, d = x.shape[1], x.shape[2]
    alpha = (2 * depth) ** 0.25
    attn_w = d // 2
    kv_w = attn_w // GQA
    nkv = kv_w // HEAD_DIM
    pool_wd = d // 4
    ssm_wd = d // 4
    n_groups = ssm_wd // SSM_GROUP
    state_w = n_groups * SSM_STATE
    nt_ssm = max(1, state_w // 512)
    o_k, o_v, o_p, o_s = attn_w, attn_w + kv_w, attn_w + 2 * kv_w, attn_w + 2 * kv_w + pool_wd
    in_w = o_s + ssm_wd
    half = ffn_w_down.shape[1]
    half_pad = -(-half // 64) * 64
    ffp = 4 * 2 * half_pad
    xi, yi, ci = _place()
    me = 4 * xi + 2 * yi + ci
    c_idx = jnp.reshape(ci, (1,)).astype(jnp.int32)

    cos_t, sin_t = _rope_tables(s)

    def layer_shards(l):
        return [
            jnp.transpose(w_in[l]).astype(_WIRE), ssm_glu_w[l].astype(_WIRE), w_out[l].astype(_WIRE),
            _pad_pairs(jnp.transpose(ffn_w_up[l]).astype(_WIRE), 0, half, half_pad),
            jnp.pad(ffn_w_down[l].astype(_WIRE), ((0, half_pad - half), (0, 0))),
        ]

    (g_cw,) = _all_gather("gather_conv_w", [_pad_pairs(ffn_conv_w, 2, half, half_pad)])

    def mixer_weights(l, gathered):
        g_in, g_glu, g_out = gathered
        return dict(
            win_t=g_in.reshape(in_w, d),
            glu=jnp.transpose(g_glu, (1, 0, 2)).reshape(ssm_wd, 2 * ssm_wd),
            wout=g_out.reshape(d, d),
            cw=jnp.transpose(g_cw[:, l], (1, 0, 2)).reshape(CONV_WIDTH, 2 * ffp),
            cb=_pad_pairs(ffn_conv_b[l].reshape(N_DEV, 2 * half), 1, half, half_pad).reshape(1, 2 * ffp),
        )

    def ffn_weights(gathered):
        g_up, g_down = gathered
        return dict(wup_t=g_up.reshape(2 * ffp, d), wdown=g_down.reshape(ffp, d))

    shards = [layer_shards(l) for l in range(depth)]
    full = [None] * depth
    handle = _gather_begin("gather_mixer_weights_0", shards[0][:3], [g_cw])
    handle_ffn0 = _gather_begin("gather_ffn_weights_0", shards[0][3:], [handle[-1]])
    full[0] = mixer_weights(0, _gather_end("gather_mixer_weights_0", handle, g_cw))

    ssm_params = (ssm_lam_re, ssm_lam_im, ssm_log_dt, ssm_b_re, ssm_b_im)
    ab_re_all, ab_im_all, bbr_all, bbi_all = _ssm_discretise(*ssm_params)

    def ssm_maps(w):
        return jax.vmap(lambda t: _blockdiag(jnp.transpose(t, (0, 2, 1)), nt_ssm))(w).astype(_MXU)

    bdr_all, bdi_all, cdr_all, cdi_all = ssm_maps(bbr_all), ssm_maps(bbi_all), ssm_maps(ssm_c_re), ssm_maps(ssm_c_im)

    saved = []
    xf = x[0]
    xb = xf.astype(_MXU)
    for l in range(depth):
        fw = full[l]
        deps = ()
        if l + 1 < depth:
            handle = _gather_begin(f"gather_weights_{l + 1}", shards[l + 1], [fw["win_t"], handle_ffn0[-1]])
            deps = (handle[-1],)
        h = _mm_nt(f"in_proj_{l}", xb, fw["win_t"], deps=deps)
        q_rot, k_rot = _rope(f"rope_{l}", h, o_v, cos_t, sin_t, _MXU, ((0, o_k), (o_k, o_v)))
        k_hm = jnp.transpose(k_rot.reshape(s, nkv, HEAD_DIM), (1, 0, 2))
        v_hm = jnp.transpose(h[:, o_v:o_p].astype(_MXU).reshape(s, nkv, HEAD_DIM), (1, 0, 2))
        sinks = attn_sinks[l].reshape(nkv, GQA)
        o_attn, lse = _attn_fwd(f"attn_{l}", q_rot, k_hm, v_hm, sinks)
        pw_b = pool_w[l].astype(_MXU)
        psc = pool_scale[l].reshape(1, pool_wd)
        y_pool, pre = _pool_fwd(f"pool_{l}", h, o_p // pool_wd, pw_b, psc)
        bdr, bdi, cdr, cdi = bdr_all[l], bdi_all[l], cdr_all[l], cdi_all[l]
        dvec = ssm_d[l].reshape(1, ssm_wd)
        ar, ai = ab_re_all[l].reshape(1, state_w), ab_im_all[l].reshape(1, state_w)
        cw_ssm = ssm_wd // nt_ssm
        sr, si, ypre, yg = _ssm_fwd(f"ssm_{l}", h, o_s // cw_ssm, bdr, bdi, cdr, cdi, dvec, ar, ai)
        y_ssm, ab2 = _glu_fwd(f"glu_{l}", yg, fw["glu"])
        mix = jnp.concatenate([o_attn.astype(_MXU), y_pool, y_ssm], -1)
        a1 = _mm_nn(f"out_proj_{l}", mix, fw["wout"])
        g1, b1 = ln1_g[l].reshape(1, d), ln1_b[l].reshape(1, d)
        x1, x1b, xh1, rs1 = _ln_fwd(f"ln1_{l}", xf, a1, g1, b1, alpha)
        if l == 0:
            fw.update(ffn_weights(_gather_end("gather_ffn_weights_0", handle_ffn0, x1b)))
        hu = _mm_nt(f"ffn_up_{l}", x1b, fw["wup_t"], cap=2 * half_pad)
        act = _conv_act_fwd(f"ffn_act_{l}", hu, fw["cw"], fw["cb"])
        tkd = 2 * half_pad
        tnd = _col_tile(d, 1024)
        f_out = _mm(f"ffn_down_{l}", act, fw["wdown"], NN, (d // tnd, ffp // tkd),
                    pl.BlockSpec((s, tkd), lambda j, kk: (0, kk)), pl.BlockSpec((tkd, tnd), lambda j, kk: (kk, j)),
                    pl.BlockSpec((s, tnd), lambda j, kk: (0, j)), (s, d), F32)
        g2, b2 = ln2_g[l].reshape(1, d), ln2_b[l].reshape(1, d)
        x2, x2b, xh2, rs2 = _ln_fwd(f"ln2_{l}", x1, f_out, g2, b2, alpha)
        saved.append(dict(xb=xb, h=h, q_rot=q_rot, k_hm=k_hm, v_hm=v_hm, sinks=sinks, o_attn=o_attn, lse=lse, pw_b=pw_b, psc=psc,
                          pre=pre, bdr=bdr, bdi=bdi, cdr=cdr, cdi=cdi, dvec=dvec, ar=ar, ai=ai, sr=sr, si=si, ypre=ypre, yg=yg,
                          ab2=ab2, mix=mix, g1=g1, xh1=xh1, rs1=rs1, x1b=x1b, hu=hu, act=act, g2=g2, xh2=xh2, rs2=rs2))
        xf, xb = x2, x2b
        if l + 1 < depth:
            gathered = _gather_end(f"gather_weights_{l + 1}", handle, x2b)
            full[l + 1] = {**mixer_weights(l + 1, gathered[:3]), **ffn_weights(gathered[3:])}

    dy, loss_part = _loss_head("loss_head", xf, loss_target[0])
    loss = lax.psum(loss_part[0, 0], ("x", "y", "c"))

    small_g = {k: [None] * depth for k in _SMALL}
    cw_g = [None] * depth
    outs = {}
    big_res = {k: None for k in _BIG}
    ssm_raw = [None] * depth
    my_chip = 2 * xi + yi
    pending = []

    transposed = ("w_in", "ffn_w_up")

    def row_groups(name_, t):
        g = 2 if name_ == "ffn_w_up" else 1
        return t.reshape(t.shape[:-2] + (g, t.shape[-2] // g, t.shape[-1]))

    def as_groups(name_, t):
        return row_groups(name_, jnp.transpose(t, (0, 2, 1)) if name_ in transposed else t)

    def from_groups(name_, t):
        t = t.reshape(t.shape[0], t.shape[1] * t.shape[2], t.shape[3])
        return jnp.transpose(t, (0, 2, 1)) if name_ in transposed else t

    grouped = {name_: tuple(as_groups(name_, t[name_]) for t in (W, M, V)) for name_ in _BIG}

    def finish_exchanges(after):
        while pending:
            lay, part, names, handle = pending.pop(0)
            pairs, lands = _chips_end(f"grads_between_chips_{part}_{lay}", handle, after)
            for name_, p, ld in zip(names, pairs, lands):
                own = row_groups(name_, lax.dynamic_index_in_dim(p, my_chip, 0, keepdims=False))
                big_res[name_] = _adamw_layer(f"adamw_{name_}_{lay}", lay, own, row_groups(name_, ld), *grouped[name_],
                                              big_res[name_])

    def begin_swap(lay, part, names, grads):
        by_owner = [a.reshape((4, 2) + a.shape[1:]) for a in grads]
        return lay, part, names, _sibling_begin(f"grads_to_sibling_{part}_{lay}", by_owner, [])

    def begin_exchange(swap, after):
        lay, part, names, handle = swap
        by_owner, theirs = _sibling_end(f"grads_to_sibling_{part}_{lay}", handle, after)
        pair = [_pair_sum(f"pair_sum_{name_}_{lay}", a, b, c_idx) for name_, a, b in zip(names, by_owner, theirs)]
        finish_exchanges(after)
        handle = _chips_begin(f"grads_between_chips_{part}_{lay}", pair, [])
        pending.append((lay, part, names, handle))
        return handle[-1]

    token = None
    small_handle = None
    for l in reversed(range(depth)):
        fw, sv = full[l], saved[l]
        deps = () if token is None else (token,)
        dr2, dr2b, dg2, db2 = _ln_bwd(f"ln2_bwd_{l}", dy, sv["xh2"], sv["rs2"], sv["g2"], deps=deps)
        d_wdown = _mm_tn_acols(f"ffn_down_dw_{l}", sv["act"], dr2b, _WIRE, cap=2 * half_pad)
        dact = _mm_nt(f"ffn_down_dx_{l}", dr2b, fw["wdown"], cap=2 * half_pad)
        dhu, dcw, dcb = _conv_act_bwd(f"ffn_act_bwd_{l}", dact, sv["hu"], fw["cw"], fw["cb"])
        d_wup = _mm(f"ffn_up_dw_{l}", dhu, sv["x1b"], TN, (N_DEV, 1),
                    pl.BlockSpec((None, s, 2 * half_pad), lambda j, kk: (j // 4, 0, j % 4)),
                    _resident((s, d), lambda j, kk: (0, 0)),
                    pl.BlockSpec((2 * half_pad, d), lambda j, kk: (j, 0)), (2 * ffp, d), _WIRE)
        swap = begin_swap(l, "ffn", ("ffn_w_up", "ffn_w_down"),
                          [d_wup.reshape(N_DEV, 2 * half_pad, d), d_wdown.reshape(N_DEV, half_pad, d)])
        tnd = _col_tile(d, 512)
        dy1 = _mm(f"ffn_up_dx_{l}", dhu, fw["wup_t"], NN, (d // tnd, N_DEV),
                  pl.BlockSpec((None, s, 2 * half_pad), lambda j, kk: (kk // 4, 0, kk % 4)),
                  pl.BlockSpec((2 * half_pad, tnd), lambda j, kk: (kk, j)),
                  pl.BlockSpec((s, tnd), lambda j, kk: (0, j)), (s, d), F32,
                  add=dr2, add_spec=pl.BlockSpec((s, tnd), lambda j, kk: (0, j)), add_scale=alpha, deps=(swap[3][-1],))
        token = begin_exchange(swap, dy1)
        dr1, dr1b, dg1, db1 = _ln_bwd(f"ln1_bwd_{l}", dy1, sv["xh1"], sv["rs1"], sv["g1"], deps=(token,))
        d_wout = _mm_tn_acols(f"out_proj_dw_{l}", sv["mix"], dr1b, _WIRE, cap=d // N_DEV)
        dmix = _mm_nt(f"out_proj_dx_{l}", dr1b, fw["wout"])
        dq_rot, dk_hm, dv_hm, dsk = _attn_bwd(f"attn_bwd_{l}", sv["q_rot"], sv["k_hm"], sv["v_hm"], sv["o_attn"], dmix,
                                             sv["lse"], sv["sinks"])
        dqk = jnp.concatenate([dq_rot, jnp.transpose(dk_hm, (1, 0, 2)).reshape(s, kv_w)], -1)
        dhq, dhk = _rope(f"rope_bwd_{l}", dqk, o_v, cos_t, -sin_t, _MXU, ((0, o_k), (o_k, o_v)))
        dhv = jnp.transpose(dv_hm, (1, 0, 2)).reshape(s, kv_w).astype(_MXU)
        dhp, dpw, dpsc = _pool_bwd(f"pool_bwd_{l}", dmix, attn_w // pool_wd, sv["pre"], sv["pw_b"], sv["psc"])
        dab2, dyg = _glu_bwd(f"glu_bwd_{l}", dmix, (attn_w + pool_wd) // ssm_wd, sv["ab2"], fw["glu"])
        d_glu = _mm_tn_bcols(f"glu_dw_{l}", sv["yg"], dab2, _WIRE)
        cw_ssm = ssm_wd // nt_ssm
        dhs, dd, dcdr, dcdi, dbdr, dbdi, dar, dai = _ssm_bwd(
            f"ssm_bwd_{l}", dyg, sv["ypre"], sv["h"], o_s // cw_ssm, sv["sr"], sv["si"], sv["bdr"], sv["bdi"], sv["cdr"],
            sv["cdi"], sv["dvec"], sv["ar"], sv["ai"])
        dh = jnp.concatenate([dhq, dhk, dhv, dhp, dhs], -1)
        d_win = _mm_tn_acols(f"in_proj_dw_{l}", dh, sv["xb"], _WIRE)
        swap = begin_swap(l, "mixer", ("w_in", "ssm_glu_w", "w_out"),
                          [d_win.reshape(N_DEV, in_w // N_DEV, d),
                           jnp.transpose(d_glu.reshape(ssm_wd, N_DEV, 2 * ssm_wd // N_DEV), (1, 0, 2)),
                           d_wout.reshape(N_DEV, d // N_DEV, d)])
        dy = _mm_nn(f"in_proj_dx_{l}", dh, fw["win_t"], add=dr1, add_scale=alpha, deps=(swap[3][-1],))

        ssm_raw[l] = (dar, dai, dbdr, dbdi, dcdr, dcdi)
        small_g["attn_sinks"][l] = dsk.reshape(-1)
        small_g["pool_w"][l] = dpw
        small_g["pool_scale"][l] = dpsc.reshape(-1)
        small_g["ssm_d"][l] = dd.reshape(n_groups, SSM_GROUP)
        small_g["ln1_g"][l], small_g["ln1_b"][l] = dg1.reshape(-1), db1.reshape(-1)
        small_g["ln2_g"][l], small_g["ln2_b"][l] = dg2.reshape(-1), db2.reshape(-1)
        small_g["ffn_conv_b"][l] = _unpad_pairs(dcb.reshape(N_DEV, 2 * half_pad), 1, half, half_pad).reshape(-1)
        cw_g[l] = _unpad_pairs(dcw.reshape(CONV_WIDTH, N_DEV, 2 * half_pad), 2, half, half_pad)

        token = begin_exchange(swap, dy)

    dar, dai, dbdr, dbdi, dcdr, dcdi = (jnp.stack(t) for t in zip(*ssm_raw))

    def from_maps(m, a, b):
        return jnp.transpose(jax.vmap(lambda t: _diagblocks(t, n_groups, a, b))(m), (0, 1, 3, 2))

    _, vjp = jax.vjp(_ssm_discretise, *ssm_params)
    dlr, dli, dldt, dbr, dbi = vjp((dar.reshape(depth, n_groups, SSM_STATE), dai.reshape(depth, n_groups, SSM_STATE),
                                    from_maps(dbdr, SSM_GROUP, SSM_STATE), from_maps(dbdi, SSM_GROUP, SSM_STATE)))
    small = {k: jnp.stack(v_) for k, v_ in small_g.items() if v_[0] is not None}
    small.update(ssm_lam_re=dlr, ssm_lam_im=dli, ssm_log_dt=dldt, ssm_b_re=dbr, ssm_b_im=dbi,
                 ssm_c_re=from_maps(dcdr, SSM_STATE, SSM_GROUP), ssm_c_im=from_maps(dcdi, SSM_STATE, SSM_GROUP))

    flat_parts = [small[k].reshape(-1) for k in _SMALL] + [jnp.stack(cw_g).reshape(-1)]
    sizes = [a.shape[0] for a in flat_parts]
    total = sum(sizes)
    rows = -(-total // (512 * V7X_LANES)) * 512
    flat = jnp.pad(jnp.concatenate(flat_parts), (0, rows * V7X_LANES - total)).reshape(rows, V7X_LANES)
    small_handle = _gather_begin("gather_small_grads", [flat], [token])
    (gathered,) = _gather_end("gather_small_grads", small_handle, token)
    gathered = gathered.reshape(N_DEV, -1)
    n_rep = sum(sizes[:-1])
    cw_all = gathered[:, n_rep:total].reshape(N_DEV, depth, CONV_WIDTH, N_DEV, 2 * half)
    cw_mine = lax.dynamic_index_in_dim(cw_all, me, axis=3, keepdims=False)
    n_small = n_rep + cw_mine[0].size
    rows2 = -(-n_small // (512 * V7X_LANES)) * 512

    def flatten(parts_):
        return jnp.pad(jnp.concatenate(parts_, -1), [(0, 0)] * (parts_[0].ndim - 1) + [(0, rows2 * V7X_LANES - n_small)])

    p_small = flatten([gathered[:, :n_rep], cw_mine.reshape(N_DEV, -1)]).reshape(N_DEV, 1, rows2, V7X_LANES)
    w_small, m_small, v_small = (
        flatten([jnp.concatenate([t[k].reshape(-1) for k in _SMALL]), t["ffn_conv_w"].reshape(-1)]).reshape(1, rows2, V7X_LANES)
        for t in (W, M, V))
    res_small = [a.reshape(-1) for a in _reduce_adamw("adamw_small", p_small, w_small, m_small, v_small)]
    off = 0
    for k in _SMALL + ("ffn_conv_w",):
        n = W[k].size
        outs[k] = tuple(a[off:off + n].reshape(W[k].shape) for a in res_small)
        off += n

    finish_exchanges(res_small[0])
    for name_ in _BIG:
        outs[name_] = tuple(from_groups(name_, t) for t in big_res[name_])

    grad_x = dy[None]
    result = [loss, grad_x]
    for i in range(4):
        result += [outs[k][i] for k in _ORDER]
    return tuple(result)
```

```python
import functools
import math

import jax
import jax.numpy as jnp
from jax import lax
from jax.experimental import pallas as pl
from jax.experimental.pallas import tpu as pltpu

F32 = jnp.float32
_MXU = jnp.bfloat16
_WIRE = jnp.bfloat16

HEAD_DIM = 64
GQA = 4
ATTN_BLOCK = 128
ROPE_THETA = 10000.0
POOL_WINDOWS = (2, 4, 8, 16)
SSM_GROUP = 16
SSM_STATE = 64
CONV_WIDTH = 3
LN_EPS = 1e-5
ADAM_LR, ADAM_B1, ADAM_B2, ADAM_EPS, ADAM_WD, ADAM_STEP = 0.001, 0.9, 0.999, 1e-08, 0.01, 10

N_DEV = 8
V7X_LANES = 128
V7X_VMEM_LIMIT = 56 * 1024 * 1024
SCAN_T = 64
SCAN_LANES = 256
MESH = pl.DeviceIdType.MESH
ANY = pl.BlockSpec(memory_space=pl.ANY)


def _cp():
    return pltpu.CompilerParams(vmem_limit_bytes=V7X_VMEM_LIMIT)


def _resident(block, index_map):
    return pl.BlockSpec(block, index_map, pipeline_mode=pl.Buffered(1))


def _sds(shape, dtype):
    return jax.ShapeDtypeStruct(tuple(shape), dtype)


def _mm(name, a, b, dims, grid, a_spec, b_spec, o_spec, out_shape, out_dtype, add=None, add_spec=None, add_scale=1.0, deps=()):
    nk = grid[1]
    n_in = 2 + (add is not None) + len(deps)
    oblk = tuple(d for d in o_spec.block_shape if d is not None)
    scratch = nk > 1 and out_dtype != F32

    def body(*refs):
        a_ref, b_ref = refs[:2]
        add_ref = None if add is None else refs[2]
        o_ref = refs[n_in]
        acc_ref = refs[-1] if scratch else None

        def finish(r):
            if add_ref is not None:
                r = r + add_scale * add_ref[...]
            o_ref[...] = r.astype(o_ref.dtype)

        part = lax.dot_general(a_ref[...], b_ref[...], (dims, ((), ())), preferred_element_type=F32)
        if nk == 1:
            finish(part)
        elif not scratch:
            k = pl.program_id(1)

            @pl.when(k == 0)
            def _():
                o_ref[...] = part

            @pl.when(k > 0)
            def _():
                o_ref[...] += part

            if add_ref is not None:
                @pl.when(k == nk - 1)
                def _():
                    o_ref[...] += add_scale * add_ref[...]
        else:
            k = pl.program_id(1)

            @pl.when(k == 0)
            def _():
                acc_ref[...] = part

            @pl.when(k > 0)
            def _():
                acc_ref[...] += part

            @pl.when(k == nk - 1)
            def _():
                finish(acc_ref[...])

    ins = [a, b] + ([] if add is None else [add]) + list(deps)
    in_specs = [a_spec, b_spec] + ([] if add is None else [add_spec]) + [ANY] * len(deps)
    return pl.pallas_call(
        body, grid=grid, in_specs=in_specs, out_specs=o_spec, out_shape=_sds(out_shape, out_dtype),
        scratch_shapes=[pltpu.VMEM(oblk, F32)] if scratch else [], compiler_params=_cp(), name=name,
    )(*ins)


NN = ((1,), (0,))
NT = ((1,), (1,))
TN = ((0,), (0,))


def _col_tile(n, cap=512):
    if n % V7X_LANES:
        return n
    t = min(cap, n)
    t -= t % V7X_LANES
    while n % t:
        t -= V7X_LANES
    return t


def _mm_nn(name, a, b, out_dtype=F32, cap=512, add=None, add_scale=1.0, deps=()):
    m, k = a.shape
    n = b.shape[1]
    tn = _col_tile(n, cap)
    o_spec = pl.BlockSpec((m, tn), lambda j, kk: (0, j))
    return _mm(name, a, b, NN, (n // tn, 1), _resident((m, k), lambda j, kk: (0, 0)),
               pl.BlockSpec((k, tn), lambda j, kk: (0, j)), o_spec, (m, n), out_dtype,
               add=add, add_spec=None if add is None else o_spec, add_scale=add_scale, deps=deps)


def _mm_nt(name, a, b, out_dtype=F32, add=None, add_scale=1.0, cap=512, deps=()):
    m, k = a.shape
    n = b.shape[0]
    tn = _col_tile(n, cap)
    o_spec = pl.BlockSpec((m, tn), lambda j, kk: (0, j))
    return _mm(name, a, b, NT, (n // tn, 1), _resident((m, k), lambda j, kk: (0, 0)),
               pl.BlockSpec((tn, k), lambda j, kk: (j, 0)), o_spec, (m, n), out_dtype,
               add=add, add_spec=None if add is None else o_spec, add_scale=add_scale, deps=deps)


def _mm_tn_bcols(name, a, b, out_dtype, cap=512):
    s, k = a.shape
    n = b.shape[1]
    tn = _col_tile(n, cap)
    return _mm(name, a, b, TN, (n // tn, 1), _resident((s, k), lambda j, kk: (0, 0)),
               pl.BlockSpec((s, tn), lambda j, kk: (0, j)), pl.BlockSpec((k, tn), lambda j, kk: (0, j)), (k, n), out_dtype)


def _mm_tn_acols(name, a, b, out_dtype, cap=512):
    s, k = a.shape
    n = b.shape[1]
    tk = _col_tile(k, cap)
    return _mm(name, a, b, TN, (k // tk, 1), pl.BlockSpec((s, tk), lambda i, kk: (0, i)),
               _resident((s, n), lambda i, kk: (0, 0)), pl.BlockSpec((tk, n), lambda i, kk: (i, 0)), (k, n), out_dtype)


def _ln_fwd(name, x, a, g, b, alpha):
    s, d = x.shape
    tr = min(256, s)

    def body(x_ref, a_ref, g_ref, b_ref, y_ref, yb_ref, xh_ref, rs_ref):
        r = alpha * x_ref[...] + a_ref[...]
        mu = jnp.mean(r, -1, keepdims=True)
        c = r - mu
        var = jnp.mean(c * c, -1, keepdims=True)
        rstd = lax.rsqrt(var + LN_EPS)
        xh = c * rstd
        y = xh * g_ref[...] + b_ref[...]
        y_ref[...] = y
        yb_ref[...] = y.astype(_MXU)
        xh_ref[...] = xh
        rs_ref[...] = rstd

    row = pl.BlockSpec((tr, d), lambda i: (i, 0))
    vec = pl.BlockSpec((1, d), lambda i: (0, 0))
    return pl.pallas_call(
        body, grid=(s // tr,), in_specs=[row, row, vec, vec],
        out_specs=[row, row, row, pl.BlockSpec((tr, 1), lambda i: (i, 0))],
        out_shape=[_sds((s, d), F32), _sds((s, d), _MXU), _sds((s, d), F32), _sds((s, 1), F32)],
        compiler_params=_cp(), name=name,
    )(x, a, g, b)


def _ln_bwd(name, dy, xh, rstd, g, deps=()):
    s, d = dy.shape
    tr = min(256, s)
    nd = len(deps)

    def body(dy_ref, xh_ref, rs_ref, g_ref, *rest):
        dr_ref, drb_ref, dg_ref, db_ref = rest[nd:]
        i = pl.program_id(0)
        dy_ = dy_ref[...]
        xh_ = xh_ref[...]
        dxh = dy_ * g_ref[...]
        m1 = jnp.mean(dxh, -1, keepdims=True)
        m2 = jnp.mean(dxh * xh_, -1, keepdims=True)
        dr = rs_ref[...] * (dxh - m1 - xh_ * m2)
        dr_ref[...] = dr
        drb_ref[...] = dr.astype(_MXU)
        pg = jnp.sum(dy_ * xh_, 0, keepdims=True)
        pb = jnp.sum(dy_, 0, keepdims=True)

        @pl.when(i == 0)
        def _():
            dg_ref[...] = pg
            db_ref[...] = pb

        @pl.when(i > 0)
        def _():
            dg_ref[...] += pg
            db_ref[...] += pb

    row = pl.BlockSpec((tr, d), lambda i: (i, 0))
    vec = pl.BlockSpec((1, d), lambda i: (0, 0))
    return pl.pallas_call(
        body, grid=(s // tr,), in_specs=[row, row, pl.BlockSpec((tr, 1), lambda i: (i, 0)), vec] + [ANY] * nd,
        out_specs=[row, row, vec, vec],
        out_shape=[_sds((s, d), F32), _sds((s, d), _MXU), _sds((1, d), F32), _sds((1, d), F32)],
        compiler_params=_cp(), name=name,
    )(dy, xh, rstd, g, *deps)


def _loss_head(name, y, target):
    s, d = y.shape
    tr = min(256, s)

    def body(y_ref, t_ref, dy_ref, l_ref):
        i = pl.program_id(0)
        e = y_ref[...] - t_ref[...]
        dy_ref[...] = e * (1.0 / d)
        part = 0.5 * jnp.sum(jnp.mean(e * e, -1, keepdims=True), 0, keepdims=True)

        @pl.when(i == 0)
        def _():
            l_ref[...] = part

        @pl.when(i > 0)
        def _():
            l_ref[...] += part

    row = pl.BlockSpec((tr, d), lambda i: (i, 0))
    return pl.pallas_call(
        body, grid=(s // tr,), in_specs=[row, row], out_specs=[row, pl.BlockSpec((1, 1), lambda i: (0, 0))],
        out_shape=[_sds((s, d), F32), _sds((1, 1), F32)], compiler_params=_cp(), name=name,
    )(y, target)


def _rope(name, t, width, cos, sin, out_dtype, splits):
    s = t.shape[0]
    tr = min(256, s)
    assert width % V7X_LANES == 0

    def body(t_ref, c_ref, s_ref, *o_refs):
        lane = lax.broadcasted_iota(jnp.int32, (tr, V7X_LANES), 1)
        first = (lane % HEAD_DIM) < (HEAD_DIM // 2)
        cs, sn = c_ref[...], s_ref[...]
        for (lo, hi), o_ref in zip(splits, o_refs):
            for c0 in range(lo, hi, V7X_LANES):
                v = t_ref[:, c0:c0 + V7X_LANES].astype(F32)
                partner = jnp.where(first, pltpu.roll(v, V7X_LANES - HEAD_DIM // 2, 1), pltpu.roll(v, HEAD_DIM // 2, 1))
                o_ref[:, c0 - lo:c0 - lo + V7X_LANES] = (v * cs + partner * sn).astype(o_ref.dtype)

    tab = pl.BlockSpec((tr, V7X_LANES), lambda i: (i, 0))
    return pl.pallas_call(
        body, grid=(s // tr,), in_specs=[pl.BlockSpec((tr, width), lambda i: (i, 0)), tab, tab],
        out_specs=[pl.BlockSpec((tr, hi - lo), lambda i: (i, 0)) for lo, hi in splits],
        out_shape=[_sds((s, hi - lo), out_dtype) for lo, hi in splits], compiler_params=_cp(), name=name,
    )(t, cos, sin)


def _attn_masks():
    i = lax.broadcasted_iota(jnp.int32, (GQA * ATTN_BLOCK, 2 * ATTN_BLOCK), 0) % ATTN_BLOCK
    j = lax.broadcasted_iota(jnp.int32, (GQA * ATTN_BLOCK, 2 * ATTN_BLOCK), 1)
    cur_ok = jnp.logical_and(j >= ATTN_BLOCK, j - ATTN_BLOCK <= i)
    prev_ok = jnp.logical_and(j < ATTN_BLOCK, j > i)
    return cur_ok, prev_ok


def _attn_scores(q4, kcat, n, cur_ok, prev_ok):
    sc = lax.dot_general(q4, kcat, (NT, ((), ())), preferred_element_type=F32) * (HEAD_DIM ** -0.5)
    return jnp.where(jnp.logical_or(cur_ok, jnp.logical_and(prev_ok, n > 0)), sc, -1e30)


def _stack_heads(ref, rows):
    return jnp.concatenate([ref[rows, g * HEAD_DIM:(g + 1) * HEAD_DIM] for g in range(GQA)], 0)


def _per_head_column(values):
    r = lax.broadcasted_iota(jnp.int32, (GQA * ATTN_BLOCK, 1), 0) // ATTN_BLOCK
    col = jnp.zeros((GQA * ATTN_BLOCK, 1), F32)
    for g, val in enumerate(values):
        col = jnp.where(r == g, val, col)
    return col


def _attn_fwd(name, q, k, v, sinks):
    s = q.shape[0]
    nkv = k.shape[0]
    gw = GQA * HEAD_DIM
    nb = s // ATTN_BLOCK

    def body(sk_ref, q_ref, k_ref, v_ref, o_ref, lse_ref):
        h = pl.program_id(0)
        cur_ok, prev_ok = _attn_masks()
        sink = _per_head_column([sk_ref[h, g] for g in range(GQA)])

        def blk(n, carry):
            rows = pl.ds(pl.multiple_of(n * ATTN_BLOCK, ATTN_BLOCK), ATTN_BLOCK)
            prows = pl.ds(pl.multiple_of(jnp.maximum(n - 1, 0) * ATTN_BLOCK, ATTN_BLOCK), ATTN_BLOCK)
            kcat = jnp.concatenate([k_ref[prows, :], k_ref[rows, :]], 0)
            vcat = jnp.concatenate([v_ref[prows, :], v_ref[rows, :]], 0)
            sc = _attn_scores(_stack_heads(q_ref, rows), kcat, n, cur_ok, prev_ok)
            m = jnp.maximum(sc.max(-1, keepdims=True), sink)
            p = jnp.exp(sc - m)
            den = p.sum(-1, keepdims=True) + jnp.exp(sink - m)
            o = jnp.dot((p / den).astype(_MXU), vcat, preferred_element_type=F32)
            lse = m + jnp.log(den)
            for g in range(GQA):
                mine = slice(g * ATTN_BLOCK, (g + 1) * ATTN_BLOCK)
                o_ref[rows, g * HEAD_DIM:(g + 1) * HEAD_DIM] = o[mine, :]
                lse_ref[rows, g:g + 1] = lse[mine, :]
            return carry

        lax.fori_loop(0, nb, blk, 0)

    kv_spec = pl.BlockSpec((None, s, HEAD_DIM), lambda h: (h, 0, 0))
    return pl.pallas_call(
        body, grid=(nkv,),
        in_specs=[pl.BlockSpec(memory_space=pltpu.SMEM), pl.BlockSpec((s, gw), lambda h: (0, h)), kv_spec, kv_spec],
        out_specs=[pl.BlockSpec((s, gw), lambda h: (0, h)), pl.BlockSpec((None, s, GQA), lambda h: (h, 0, 0))],
        out_shape=[_sds((s, nkv * gw), F32), _sds((nkv, s, GQA), F32)], compiler_params=_cp(), name=name,
    )(sinks, q, k, v)


def _attn_bwd(name, q, k, v, o, dmix, lse, sinks):
    s = q.shape[0]
    nkv = k.shape[0]
    gw = GQA * HEAD_DIM
    nb = s // ATTN_BLOCK
    scale = HEAD_DIM ** -0.5

    def body(sk_ref, q_ref, k_ref, v_ref, o_ref, do_ref, lse_ref, dq_ref, dk_ref, dv_ref, dsk_ref):
        h = pl.program_id(0)
        cur_ok, prev_ok = _attn_masks()
        dk_ref[...] = jnp.zeros_like(dk_ref)
        dv_ref[...] = jnp.zeros_like(dv_ref)

        sink = _per_head_column([sk_ref[h, g] for g in range(GQA)])

        def blk(n, acc):
            rows = pl.ds(pl.multiple_of(n * ATTN_BLOCK, ATTN_BLOCK), ATTN_BLOCK)
            prows = pl.ds(pl.multiple_of(jnp.maximum(n - 1, 0) * ATTN_BLOCK, ATTN_BLOCK), ATTN_BLOCK)
            kcat = jnp.concatenate([k_ref[prows, :], k_ref[rows, :]], 0)
            vcat = jnp.concatenate([v_ref[prows, :], v_ref[rows, :]], 0)
            q4 = _stack_heads(q_ref, rows)
            do4 = _stack_heads(do_ref, rows)
            delta = jnp.sum(do4 * _stack_heads(o_ref, rows), -1, keepdims=True)
            dob = do4.astype(_MXU)
            lse = jnp.concatenate([lse_ref[rows, g:g + 1] for g in range(GQA)], 0)
            p = jnp.exp(_attn_scores(q4, kcat, n, cur_ok, prev_ok) - lse)
            dp = lax.dot_general(dob, vcat, (NT, ((), ())), preferred_element_type=F32)
            ds = (p * (dp - delta) * scale).astype(_MXU)
            dq = jnp.dot(ds, kcat, preferred_element_type=F32)
            for g in range(GQA):
                dq_ref[rows, g * HEAD_DIM:(g + 1) * HEAD_DIM] = dq[g * ATTN_BLOCK:(g + 1) * ATTN_BLOCK, :]
            dk = lax.dot_general(ds, q4, (TN, ((), ())), preferred_element_type=F32)
            dv = lax.dot_general(p.astype(_MXU), dob, (TN, ((), ())), preferred_element_type=F32)
            dk_ref[prows, :] += dk[:ATTN_BLOCK, :]
            dv_ref[prows, :] += dv[:ATTN_BLOCK, :]
            dk_ref[rows, :] += dk[ATTN_BLOCK:, :]
            dv_ref[rows, :] += dv[ATTN_BLOCK:, :]
            return acc - jnp.exp(sink - lse) * delta

        acc = lax.fori_loop(0, nb, blk, jnp.zeros((GQA * ATTN_BLOCK, 1), F32))
        for g in range(GQA):
            dsk_ref[:, g:g + 1] = jnp.sum(acc[g * ATTN_BLOCK:(g + 1) * ATTN_BLOCK, :], 0, keepdims=True)

    kv_spec = pl.BlockSpec((None, s, HEAD_DIM), lambda h: (h, 0, 0))
    qcols = pl.BlockSpec((s, gw), lambda h: (0, h))
    return pl.pallas_call(
        body, grid=(nkv,),
        in_specs=[pl.BlockSpec(memory_space=pltpu.SMEM), qcols, kv_spec, kv_spec, qcols, qcols,
                  pl.BlockSpec((None, s, GQA), lambda h: (h, 0, 0))],
        out_specs=[qcols, kv_spec, kv_spec, pl.BlockSpec((None, 1, GQA), lambda h: (h, 0, 0))],
        out_shape=[_sds((s, nkv * gw), F32), _sds((nkv, s, HEAD_DIM), F32), _sds((nkv, s, HEAD_DIM), F32),
                   _sds((nkv, 1, GQA), F32)],
        compiler_params=_cp(), name=name,
    )(sinks, q, k, v, o, dmix, lse)


def _shift_down(a, k, t):
    return jnp.where(t >= k, pltpu.roll(a, k, 0), 0.0)


def _shift_up(a, k, t):
    n = a.shape[0]
    return jnp.where(t < n - k, pltpu.roll(a, n - k, 0), 0.0)


def _pool_fwd(name, h, col_block, pool_w, pool_scale):
    s = h.shape[0]
    ng, pg = pool_w.shape[0], pool_w.shape[1]
    pw_ = ng * pg

    def body(u_ref, w_ref, sc_ref, y_ref, pre_ref):
        t = lax.broadcasted_iota(jnp.int32, (s, pg), 0)
        for gi, win in enumerate(POOL_WINDOWS):
            cols = slice(gi * pg, (gi + 1) * pg)
            u = u_ref[:, cols]
            a = u
            k = 1
            while k < win:
                a = a + _shift_down(a, k, t)
                k *= 2
            div = jnp.minimum(t + 1, win).astype(F32)
            pre = (a / div - u).astype(_MXU)
            pre_ref[:, cols] = pre
            out = jnp.dot(pre, w_ref[gi], preferred_element_type=F32)
            y_ref[:, cols] = (out * sc_ref[:, cols]).astype(y_ref.dtype)

    blk = pl.BlockSpec((s, pw_), lambda i: (0, 0))
    return pl.pallas_call(
        body, grid=(1,),
        in_specs=[pl.BlockSpec((s, pw_), lambda i: (0, col_block)), pl.BlockSpec((ng, pg, pg), lambda i: (0, 0, 0)),
                  pl.BlockSpec((1, pw_), lambda i: (0, 0))],
        out_specs=[blk, blk], out_shape=[_sds((s, pw_), _MXU), _sds((s, pw_), _MXU)], compiler_params=_cp(), name=name,
    )(h, pool_w, pool_scale)


def _pool_bwd(name, dmix, col_block, pre, pool_w, pool_scale):
    s = pre.shape[0]
    ng, pg = pool_w.shape[0], pool_w.shape[1]
    pw_ = ng * pg

    def body(dy_ref, pre_ref, w_ref, sc_ref, du_ref, dw_ref, dsc_ref):
        t = lax.broadcasted_iota(jnp.int32, (s, pg), 0)
        for gi, win in enumerate(POOL_WINDOWS):
            cols = slice(gi * pg, (gi + 1) * pg)
            pre_g = pre_ref[:, cols]
            dy = dy_ref[:, cols]
            out = jnp.dot(pre_g, w_ref[gi], preferred_element_type=F32)
            dsc_ref[:, cols] = jnp.sum(dy * out, 0, keepdims=True)
            dout = (dy * sc_ref[:, cols]).astype(_MXU)
            dw_ref[gi] = lax.dot_general(pre_g, dout, (TN, ((), ())), preferred_element_type=F32)
            dpre = lax.dot_general(dout, w_ref[gi], (NT, ((), ())), preferred_element_type=F32)
            div = jnp.minimum(t + 1, win).astype(F32)
            a = dpre / div
            k = 1
            while k < win:
                a = a + _shift_up(a, k, t)
                k *= 2
            du_ref[:, cols] = (a - dpre).astype(du_ref.dtype)

    blk = pl.BlockSpec((s, pw_), lambda i: (0, 0))
    wspec = pl.BlockSpec((ng, pg, pg), lambda i: (0, 0, 0))
    vec = pl.BlockSpec((1, pw_), lambda i: (0, 0))
    return pl.pallas_call(
        body, grid=(1,), in_specs=[pl.BlockSpec((s, pw_), lambda i: (0, col_block)), blk, wspec, vec],
        out_specs=[blk, wspec, vec], out_shape=[_sds((s, pw_), _MXU), _sds((ng, pg, pg), F32), _sds((1, pw_), F32)],
        compiler_params=_cp(), name=name,
    )(dmix, pre, pool_w, pool_scale)


def _scan_chunks(xr_ref, xi_ref, sr_ref, si_ref, ar, ai, reverse):
    n, c = xr_ref.shape
    tt = min(SCAN_T, n)
    lw = min(SCAN_LANES, c)
    nchunk = n // tt
    t = lax.broadcasted_iota(jnp.int32, (tt, lw), 0)

    for l0 in range(0, c, lw):
        lanes = slice(l0, l0 + lw)
        a_r, a_i = ar[:, lanes], ai[:, lanes]

        def local(vr, vi, a_r=a_r, a_i=a_i):
            pr, pi = a_r, a_i
            k = 1
            while k < tt:
                if reverse:
                    hr, hi = _shift_up(vr, k, t), _shift_up(vi, k, t)
                else:
                    hr, hi = _shift_down(vr, k, t), _shift_down(vi, k, t)
                vr, vi = vr + pr * hr - pi * hi, vi + pr * hi + pi * hr
                pr, pi = pr * pr - pi * pi, 2.0 * pr * pi
                k *= 2
            return vr, vi

        edge = tt - 1 if reverse else 0
        pw_r, pw_i = local(jnp.where(t == edge, a_r, 0.0), jnp.where(t == edge, a_i, 0.0))
        last = 0 if reverse else tt - 1

        def body(i, carry, lanes=lanes, local=local, pw_r=pw_r, pw_i=pw_i):
            cr, ci = carry
            ch = nchunk - 1 - i if reverse else i
            rows = pl.ds(pl.multiple_of(ch * tt, tt), tt)
            vr, vi = local(xr_ref[rows, lanes], xi_ref[rows, lanes])
            vr2 = vr + pw_r * cr - pw_i * ci
            vi2 = vi + pw_r * ci + pw_i * cr
            sr_ref[rows, lanes] = vr2
            si_ref[rows, lanes] = vi2
            return vr2[last:last + 1, :], vi2[last:last + 1, :]

        lax.fori_loop(0, nchunk, body, (jnp.zeros((1, lw), F32), jnp.zeros((1, lw), F32)))


_GELU_K = math.sqrt(2.0 / math.pi)


def _gelu_grad(y):
    inner = _GELU_K * (y + 0.044715 * y * y * y)
    th = jnp.tanh(inner)
    return 0.5 * (1.0 + th) + 0.5 * y * (1.0 - th * th) * _GELU_K * (1.0 + 3.0 * 0.044715 * y * y)


def _ssm_fwd(name, h, u_block0, bdr, bdi, cdr, cdi, dvec, ar, ai):
    s = h.shape[0]
    nt, cw, lw = bdr.shape
    rc = min(256, s)

    def body(u_ref, bdr_ref, bdi_ref, cdr_ref, cdi_ref, d_ref, ar_ref, ai_ref, sr_ref, si_ref, y_ref, yg_ref):
        def mm_in(c, _):
            rows = pl.ds(pl.multiple_of(c * rc, rc), rc)
            ub = u_ref[rows, :].astype(_MXU)
            sr_ref[rows, :] = jnp.dot(ub, bdr_ref[...], preferred_element_type=F32)
            si_ref[rows, :] = jnp.dot(ub, bdi_ref[...], preferred_element_type=F32)
            return 0

        lax.fori_loop(0, s // rc, mm_in, 0)
        _scan_chunks(sr_ref, si_ref, sr_ref, si_ref, ar_ref[...], ai_ref[...], reverse=False)

        def mm_out(c, _):
            rows = pl.ds(pl.multiple_of(c * rc, rc), rc)
            y = (jnp.dot(sr_ref[rows, :].astype(_MXU), cdr_ref[...], preferred_element_type=F32)
                 - jnp.dot(si_ref[rows, :].astype(_MXU), cdi_ref[...], preferred_element_type=F32)
                 + d_ref[...] * u_ref[rows, :])
            y_ref[rows, :] = y
            yg_ref[rows, :] = jax.nn.gelu(y).astype(yg_ref.dtype)
            return 0

        lax.fori_loop(0, s // rc, mm_out, 0)

    st = pl.BlockSpec((s, lw), lambda j: (0, j))
    ch = pl.BlockSpec((s, cw), lambda j: (0, j))
    bspec = pl.BlockSpec((None, cw, lw), lambda j: (j, 0, 0))
    cspec = pl.BlockSpec((None, lw, cw), lambda j: (j, 0, 0))
    return pl.pallas_call(
        body, grid=(nt,),
        in_specs=[pl.BlockSpec((s, cw), lambda j: (0, u_block0 + j)), bspec, bspec, cspec, cspec,
                  pl.BlockSpec((1, cw), lambda j: (0, j)), pl.BlockSpec((1, lw), lambda j: (0, j)),
                  pl.BlockSpec((1, lw), lambda j: (0, j))],
        out_specs=[st, st, ch, ch],
        out_shape=[_sds((s, nt * lw), F32), _sds((s, nt * lw), F32), _sds((s, nt * cw), F32), _sds((s, nt * cw), _MXU)],
        compiler_params=_cp(), name=name,
    )(h, bdr, bdi, cdr, cdi, dvec, ar, ai)


def _ssm_bwd(name, dyg, ypre, h, u_block0, sr, si, bdr, bdi, cdr, cdi, dvec, ar, ai):
    s = h.shape[0]
    nt, cw, lw = bdr.shape
    rc = min(256, s)

    def body(dyg_ref, yp_ref, u_ref, sr_ref, si_ref, bdr_ref, bdi_ref, cdr_ref, cdi_ref, d_ref, ar_ref, ai_ref,
             du_ref, dd_ref, dcr_ref, dci_ref, dbr_ref, dbi_ref, dar_ref, dai_ref, lr_scr, li_scr, dy_scr):
        for ref in (dd_ref, dcr_ref, dci_ref, dbr_ref, dbi_ref, dar_ref, dai_ref):
            ref[...] = jnp.zeros_like(ref)

        def p1(c, _):
            rows = pl.ds(pl.multiple_of(c * rc, rc), rc)
            dy = dyg_ref[rows, :] * _gelu_grad(yp_ref[rows, :])
            dy_scr[rows, :] = dy
            dd_ref[...] += jnp.sum(dy * u_ref[rows, :], 0, keepdims=True)
            dyb = dy.astype(_MXU)
            lr_scr[rows, :] = lax.dot_general(dyb, cdr_ref[...], (NT, ((), ())), preferred_element_type=F32)
            li_scr[rows, :] = -lax.dot_general(dyb, cdi_ref[...], (NT, ((), ())), preferred_element_type=F32)
            dcr_ref[...] += lax.dot_general(sr_ref[rows, :].astype(_MXU), dyb, (TN, ((), ())), preferred_element_type=F32)
            dci_ref[...] -= lax.dot_general(si_ref[rows, :].astype(_MXU), dyb, (TN, ((), ())), preferred_element_type=F32)
            return 0

        lax.fori_loop(0, s // rc, p1, 0)
        _scan_chunks(lr_scr, li_scr, lr_scr, li_scr, ar_ref[...], -ai_ref[...], reverse=True)
        t = lax.broadcasted_iota(jnp.int32, (rc, lw), 0)

        def p2(c, _):
            r0 = pl.multiple_of(c * rc, rc)
            rows = pl.ds(r0, rc)
            before = pl.ds(pl.multiple_of(jnp.maximum(r0 - 8, 0), 8), 8)
            have = (c > 0).astype(F32)
            lr, li = lr_scr[rows, :], li_scr[rows, :]
            spr = jnp.where(t == 0, sr_ref[before, :][7:8, :] * have, pltpu.roll(sr_ref[rows, :], 1, 0))
            spi = jnp.where(t == 0, si_ref[before, :][7:8, :] * have, pltpu.roll(si_ref[rows, :], 1, 0))
            dar_ref[...] += jnp.sum(lr * spr + li * spi, 0, keepdims=True)
            dai_ref[...] += jnp.sum(li * spr - lr * spi, 0, keepdims=True)
            lrb, lib = lr.astype(_MXU), li.astype(_MXU)
            du = (dy_scr[rows, :] * d_ref[...]
                  + lax.dot_general(lrb, bdr_ref[...], (NT, ((), ())), preferred_element_type=F32)
                  + lax.dot_general(lib, bdi_ref[...], (NT, ((), ())), preferred_element_type=F32))
            du_ref[rows, :] = du.astype(du_ref.dtype)
            ub = u_ref[rows, :].astype(_MXU)
            dbr_ref[...] += lax.dot_general(ub, lrb, (TN, ((), ())), preferred_element_type=F32)
            dbi_ref[...] += lax.dot_general(ub, lib, (TN, ((), ())), preferred_element_type=F32)
            return 0

        lax.fori_loop(0, s // rc, p2, 0)

    st = pl.BlockSpec((s, lw), lambda j: (0, j))
    ch = pl.BlockSpec((s, cw), lambda j: (0, j))
    bspec = pl.BlockSpec((None, cw, lw), lambda j: (j, 0, 0))
    cspec = pl.BlockSpec((None, lw, cw), lambda j: (j, 0, 0))
    cvec = pl.BlockSpec((1, cw), lambda j: (0, j))
    svec = pl.BlockSpec((1, lw), lambda j: (0, j))
    return pl.pallas_call(
        body, grid=(nt,),
        in_specs=[ch, ch, pl.BlockSpec((s, cw), lambda j: (0, u_block0 + j)), st, st, bspec, bspec, cspec, cspec, cvec, svec, svec],
        out_specs=[ch, cvec, cspec, cspec, bspec, bspec, svec, svec],
        out_shape=[_sds((s, nt * cw), _MXU), _sds((1, nt * cw), F32), _sds((nt, lw, cw), F32), _sds((nt, lw, cw), F32),
                   _sds((nt, cw, lw), F32), _sds((nt, cw, lw), F32), _sds((1, nt * lw), F32), _sds((1, nt * lw), F32)],
        scratch_shapes=[pltpu.VMEM((s, lw), F32), pltpu.VMEM((s, lw), F32), pltpu.VMEM((s, cw), F32)],
        compiler_params=_cp(), name=name,
    )(dyg, ypre, h, sr, si, bdr, bdi, cdr, cdi, dvec, ar, ai)


def _glu_fwd(name, yg, gw):
    s, w = yg.shape
    tr = min(512, s)

    def body(y_ref, w_ref, o_ref, ab_ref):
        ab = jnp.dot(y_ref[...], w_ref[...], preferred_element_type=F32)
        ab_ref[...] = ab
        o_ref[...] = (ab[:, :w] * jax.nn.sigmoid(ab[:, w:])).astype(o_ref.dtype)

    return pl.pallas_call(
        body, grid=(s // tr,), in_specs=[pl.BlockSpec((tr, w), lambda i: (i, 0)), _resident((w, 2 * w), lambda i: (0, 0))],
        out_specs=[pl.BlockSpec((tr, w), lambda i: (i, 0)), pl.BlockSpec((tr, 2 * w), lambda i: (i, 0))],
        out_shape=[_sds((s, w), _MXU), _sds((s, 2 * w), F32)], compiler_params=_cp(), name=name,
    )(yg, gw)


def _glu_bwd(name, dmix, col_block, ab, gw):
    s = ab.shape[0]
    w = ab.shape[1] // 2
    tr = min(512, s)

    def body(do_ref, ab_ref, w_ref, dab_ref, dy_ref):
        do = do_ref[...]
        a, b = ab_ref[:, :w], ab_ref[:, w:]
        sg = jax.nn.sigmoid(b)
        da = (do * sg).astype(_MXU)
        db = (do * a * sg * (1.0 - sg)).astype(_MXU)
        dab_ref[:, :w] = da
        dab_ref[:, w:] = db
        dy_ref[...] = (lax.dot_general(da, w_ref[:, :w], (NT, ((), ())), preferred_element_type=F32)
                       + lax.dot_general(db, w_ref[:, w:], (NT, ((), ())), preferred_element_type=F32))

    return pl.pallas_call(
        body, grid=(s // tr,),
        in_specs=[pl.BlockSpec((tr, w), lambda i: (i, col_block)), pl.BlockSpec((tr, 2 * w), lambda i: (i, 0)),
                  _resident((w, 2 * w), lambda i: (0, 0))],
        out_specs=[pl.BlockSpec((tr, 2 * w), lambda i: (i, 0)), pl.BlockSpec((tr, w), lambda i: (i, 0))],
        out_shape=[_sds((s, 2 * w), _MXU), _sds((s, w), F32)], compiler_params=_cp(), name=name,
    )(dmix, ab, gw)


CONV_ROWS = 64


def _conv_chunk(ref, w_ref, b_ref, c, tt):
    r0 = pl.multiple_of(c * tt, tt)
    before = ref[pl.ds(pl.multiple_of(jnp.maximum(r0 - 8, 0), 8), 8), :]
    before = jnp.where(c > 0, before, 0.0)
    main = ref[pl.ds(r0, tt), :]
    ext = jnp.concatenate([before, main], 0)
    d1 = pltpu.roll(ext, 1, 0)[8:, :]
    d2 = pltpu.roll(ext, 2, 0)[8:, :]
    hc = b_ref[...] + d2 * w_ref[0:1, :]
    hc = hc + d1 * w_ref[1:2, :]
    return hc + main * w_ref[2:3, :], main, d1, d2


def _conv_act_fwd(name, hu, cw, cb):
    s, f2 = hu.shape
    f = f2 // 2
    tw = _col_tile(f, 256)
    nt = f // tw

    tt = min(CONV_ROWS, s)

    def body(v_ref, g_ref, wv_ref, wg_ref, bv_ref, bg_ref, act_ref):
        def chunk(c, _):
            val = _conv_chunk(v_ref, wv_ref, bv_ref, c, tt)[0]
            gate = _conv_chunk(g_ref, wg_ref, bg_ref, c, tt)[0]
            act_ref[pl.ds(pl.multiple_of(c * tt, tt), tt), :] = (jax.nn.silu(gate) * val).astype(act_ref.dtype)
            return 0

        lax.fori_loop(0, s // tt, chunk, 0)

    cv = lambda rows: pl.BlockSpec((rows, tw), lambda i: (0, i))
    cg = lambda rows: pl.BlockSpec((rows, tw), lambda i: (0, nt + i))
    return pl.pallas_call(
        body, grid=(nt,), in_specs=[cv(s), cg(s), cv(CONV_WIDTH), cg(CONV_WIDTH), cv(1), cg(1)],
        out_specs=cv(s), out_shape=_sds((s, f), _MXU), compiler_params=_cp(), name=name,
    )(hu, hu, cw, cw, cb, cb)


def _conv_act_bwd(name, dact, hu, cw, cb):
    s, f2 = hu.shape
    f = f2 // 2
    tw = _col_tile(f, 256)
    nt = f // tw

    tt = min(CONV_ROWS, s)
    nchunk = s // tt

    def body(da_ref, v_ref, g_ref, wv_ref, wg_ref, bv_ref, bg_ref, dh_ref, dwv_ref, dwg_ref, dbv_ref, dbg_ref):
        def chunk(i, carry):
            c = nchunk - 1 - i
            rows = pl.ds(pl.multiple_of(c * tt, tt), tt)
            val, hv, hv1, hv2 = _conv_chunk(v_ref, wv_ref, bv_ref, c, tt)
            gate, hg, hg1, hg2 = _conv_chunk(g_ref, wg_ref, bg_ref, c, tt)
            sg = jax.nn.sigmoid(gate)
            da = da_ref[rows, :]
            dval = da * (gate * sg)
            dgate = da * val * sg * (1.0 + gate * (1.0 - sg))
            out = []
            for part, dhc, taps, w_ref, (after, acc) in ((0, dval, (hv2, hv1, hv), wv_ref, carry[0]),
                                                        (1, dgate, (hg2, hg1, hg), wg_ref, carry[1])):
                ext = jnp.concatenate([dhc, after], 0)
                u1 = pltpu.roll(ext, tt + 8 - 1, 0)[:tt, :]
                u2 = pltpu.roll(ext, tt + 8 - 2, 0)[:tt, :]
                dh = dhc * w_ref[2:3, :] + u1 * w_ref[1:2, :] + u2 * w_ref[0:1, :]
                dh_ref[part, rows, :] = dh.astype(dh_ref.dtype)
                sums = [jnp.sum(dhc * tap, 0, keepdims=True) for tap in taps] + [jnp.sum(dhc, 0, keepdims=True)]
                out.append((dhc[0:8, :], tuple(a + b for a, b in zip(acc, sums))))
            return tuple(out)

        zero = (jnp.zeros((8, tw), F32), tuple(jnp.zeros((1, tw), F32) for _ in range(CONV_WIDTH + 1)))
        (_, acc_v), (_, acc_g) = lax.fori_loop(0, nchunk, chunk, (zero, zero))
        for acc, dw_ref, db_ref in ((acc_v, dwv_ref, dbv_ref), (acc_g, dwg_ref, dbg_ref)):
            for tap in range(CONV_WIDTH):
                dw_ref[tap:tap + 1, :] = acc[tap]
            db_ref[...] = acc[CONV_WIDTH]

    cv = lambda rows: pl.BlockSpec((rows, tw), lambda i: (0, i))
    cg = lambda rows: pl.BlockSpec((rows, tw), lambda i: (0, nt + i))
    both = pl.BlockSpec((2, s, tw), lambda i: (0, 0, i))
    dh, dwv, dwg, dbv, dbg = pl.pallas_call(
        body, grid=(nt,), in_specs=[cv(s), cv(s), cg(s), cv(CONV_WIDTH), cg(CONV_WIDTH), cv(1), cg(1)],
        out_specs=[both, cv(CONV_WIDTH), cv(CONV_WIDTH), cv(1), cv(1)],
        out_shape=[_sds((2, s, f), _MXU), _sds((CONV_WIDTH, f), F32), _sds((CONV_WIDTH, f), F32),
                   _sds((1, f), F32), _sds((1, f), F32)],
        compiler_params=_cp(), name=name,
    )(dact, hu, hu, cw, cw, cb, cb)
    return dh, jnp.concatenate([dwv, dwg], 1), jnp.concatenate([dbv, dbg], 1)


ELEM_BLOCK = 512 * 1024


def _elem_tiles(r, c, budget=ELEM_BLOCK):
    rows = [t for t in range(8, r + 1, 8) if r % t == 0] or [r]
    cols = [t for t in range(V7X_LANES, c + 1, V7X_LANES) if c % t == 0] or [c]
    fits = [(tr * tc, tc, tr) for tr in rows for tc in cols if tr * tc <= budget]
    if not fits:
        return min(rows), min(cols)
    _, tc, tr = max(fits)
    return tr, tc


def _sum_parts(name, parts):
    n, r, c = parts.shape
    tr, tc = _elem_tiles(r, c, ELEM_BLOCK // n)

    def body(p_ref, g_ref):
        g = p_ref[0]
        for i in range(1, n):
            g = g + p_ref[i]
        g_ref[...] = g

    return pl.pallas_call(
        body, grid=(r // tr, c // tc), in_specs=[pl.BlockSpec((n, tr, tc), lambda i, j: (0, i, j))],
        out_specs=pl.BlockSpec((tr, tc), lambda i, j: (i, j)), out_shape=_sds((r, c), F32), compiler_params=_cp(), name=name,
    )(parts)


def _adamw_2d(name, g, w, m, v):
    r, c = w.shape
    tc = c if c % V7X_LANES else _col_tile(c, 2048)
    rows = [t for t in range(8, r + 1, 8) if r % t == 0 and t * max(tc, V7X_LANES) <= ELEM_BLOCK // 4] or [r]
    tr = max(rows)
    c1 = 1.0 - ADAM_B1 ** ADAM_STEP
    c2 = 1.0 - ADAM_B2 ** ADAM_STEP

    def body(g_ref, w_ref, m_ref, v_ref, d_ref, nm_ref, nv_ref):
        g = g_ref[...]
        nm = ADAM_B1 * m_ref[...] + (1.0 - ADAM_B1) * g
        nv = ADAM_B2 * v_ref[...] + (1.0 - ADAM_B2) * (g * g)
        m_hat = nm / c1
        v_hat = nv / c2
        nm_ref[...] = nm
        nv_ref[...] = nv
        d_ref[...] = -ADAM_LR * (m_hat / (jnp.sqrt(v_hat) + ADAM_EPS) + ADAM_WD * w_ref[...])

    blk = pl.BlockSpec((tr, tc), lambda i, j: (i, j))
    out = _sds((r, c), F32)
    return pl.pallas_call(
        body, grid=(r // tr, c // tc), in_specs=[blk] * 4, out_specs=[blk] * 3, out_shape=[out] * 3,
        compiler_params=_cp(), name=name,
    )(g, w, m, v)


def _pair_sum(name, mine, theirs, c_idx):
    _, _, r, c = mine.shape
    tr, tc = _elem_tiles(r, c)

    def body(c_ref, a_ref, b_ref, o_ref):
        o_ref[...] = (a_ref[...].astype(F32) + b_ref[...].astype(F32)).astype(o_ref.dtype)

    return pl.pallas_call(
        body,
        grid_spec=pltpu.PrefetchScalarGridSpec(
            num_scalar_prefetch=1, grid=(4, r // tr, c // tc),
            in_specs=[pl.BlockSpec((None, None, tr, tc), lambda p, i, j, cref: (p, cref[0], i, j)),
                      pl.BlockSpec((None, tr, tc), lambda p, i, j, cref: (p, i, j))],
            out_specs=pl.BlockSpec((None, tr, tc), lambda p, i, j, cref: (p, i, j))),
        out_shape=_sds((4, r, c), _WIRE), compiler_params=_cp(), name=name,
    )(c_idx, mine, theirs)


def _place():
    return lax.axis_index("x"), lax.axis_index("y"), lax.axis_index("c")


def _all_gather(name, xs):
    n = len(xs)

    def body(*refs):
        x_refs, o_refs = refs[:n], refs[n:2 * n]
        send_sems, recv_sems, local_sems = refs[2 * n:]
        x, y, c = _place()
        me, sibling = (x, y, c), (x, y, 1 - c)
        chips = [(1 - x, y), (x, 1 - y), (1 - x, 1 - y)]

        def copy(a, k, block, to, src=None):
            px, py, pc = block
            rows = o_refs[a].at[4 * px + 2 * py + pc]
            return pltpu.make_async_remote_copy(
                src_ref=rows if src is None else src, dst_ref=rows, send_sem=send_sems.at[a, k], recv_sem=recv_sems.at[a, k],
                device_id=to, device_id_type=MESH)

        sent = []
        mine = []
        for a in range(n):
            mx, my, mc = me
            cp = pltpu.make_async_copy(x_refs[a], o_refs[a].at[4 * mx + 2 * my + mc], local_sems.at[a])
            cp.start()
            mine.append(cp)
            first = [copy(a, 0, me, sibling, src=x_refs[a])]
            first += [copy(a, 1 + j, me, (*chip, c), src=x_refs[a]) for j, chip in enumerate(chips)]
            for cp in first:
                cp.start()
            sent += first
        for a in range(n):
            for j, chip in enumerate(chips):
                copy(a, 1 + j, (*chip, c), me).wait_recv()
                fwd = copy(a, 4 + j, (*chip, c), sibling)
                fwd.start()
                sent.append(fwd)
        for a in range(n):
            copy(a, 0, sibling, me).wait_recv()
            for j, chip in enumerate(chips):
                copy(a, 4 + j, (*chip, 1 - c), me).wait_recv()
        for cp in sent:
            cp.wait_send()
        for cp in mine:
            cp.wait()

    return pl.pallas_call(
        body, in_specs=[ANY] * n, out_specs=[ANY] * n,
        out_shape=[_sds((N_DEV,) + a.shape, a.dtype) for a in xs],
        scratch_shapes=[pltpu.SemaphoreType.DMA((n, 7)), pltpu.SemaphoreType.DMA((n, 7)), pltpu.SemaphoreType.DMA((n,))],
        name=name,
    )(*xs)


HBM = pl.BlockSpec(memory_space=pltpu.HBM)
SEM = pl.BlockSpec(memory_space=pltpu.SEMAPHORE)
DATAFLOW = pltpu.SideEffectType.DATAFLOW_SIDE_EFFECTING


def _in_hbm(a):
    return pltpu.with_memory_space_constraint(a, pltpu.HBM)


def _split_copy_start(name, srcs, lands, copies, deps):
    n, nd = len(srcs), len(deps)
    per = len(copies([None] * n, [None] * n, probe=True)) // n

    def body(*refs):
        s_refs, l_refs = refs[:n], refs[n:2 * n]
        send_sems, recv_sems = refs[2 * n + nd], refs[2 * n + nd + 1]
        token = refs[-1]
        for a, k, src, dst, to in copies(s_refs, l_refs):
            pltpu.make_async_remote_copy(src_ref=src, dst_ref=dst, send_sem=send_sems.at[a * per + k],
                                         recv_sem=recv_sems.at[a * per + k], device_id=to, device_id_type=MESH).start()
        token[...] = jnp.zeros_like(token)

    both = list(srcs) + list(lands)
    outs = pl.pallas_call(
        body, name=name,
        out_shape=(pltpu.SemaphoreType.DMA((n * per,)), pltpu.SemaphoreType.DMA((n * per,)),
                   *[pltpu.HBM(a.shape, a.dtype) for a in both], _sds((8, V7X_LANES), F32)),
        in_specs=[HBM] * (2 * n) + [ANY] * nd,
        out_specs=(SEM, SEM, *[HBM] * (2 * n), pl.BlockSpec(memory_space=pltpu.VMEM)),
        input_output_aliases={i: 2 + i for i in range(2 * n)},
        compiler_params=pltpu.CompilerParams(has_side_effects=DATAFLOW),
    )(*[_in_hbm(a) for a in both], *deps)
    return outs[0], outs[1], list(outs[2:2 + n]), list(outs[2 + n:2 + 2 * n]), outs[-1]


def _split_copy_wait(name, send_sems, recv_sems, srcs, lands, arrivals, after):
    n = len(srcs)
    per = len(arrivals([None] * n, [None] * n, probe=True)) // n

    def body(*refs):
        s_refs, l_refs = refs[:n], refs[n:2 * n]
        send_sems_, recv_sems_ = refs[2 * n], refs[2 * n + 1]
        for a, k, src, dst, frm in arrivals(s_refs, l_refs):
            cp = pltpu.make_async_remote_copy(src_ref=src, dst_ref=dst, send_sem=send_sems_.at[a * per + k],
                                              recv_sem=recv_sems_.at[a * per + k], device_id=frm, device_id_type=MESH)
            cp.wait_send()
            cp.wait_recv()

    both = list(srcs) + list(lands)
    outs = pl.pallas_call(
        body, name=name, out_shape=tuple(pltpu.HBM(a.shape, a.dtype) for a in both),
        in_specs=[HBM] * (2 * n) + [SEM, SEM, ANY], out_specs=tuple([HBM] * (2 * n)),
        input_output_aliases={i: i for i in range(2 * n)},
        compiler_params=pltpu.CompilerParams(has_side_effects=DATAFLOW),
    )(*both, send_sems, recv_sems, after)
    return list(outs[:n]), list(outs[n:])


def _gather_copies(arriving):
    def copies(s_refs, l_refs, probe=False):
        if probe:
            return [None] * (4 * len(s_refs))
        x, y, c = _place()
        out = []
        for a in range(len(s_refs)):
            for k, (px, py, pc) in enumerate([(x, y, 1 - c), (1 - x, y, c), (x, 1 - y, c), (1 - x, 1 - y, c)]):
                slot = 4 * px + 2 * py + pc if arriving else 4 * x + 2 * y + c
                out.append((a, k, s_refs[a], l_refs[a].at[slot], (px, py, pc)))
        return out
    return copies


def _chip_copies(arriving):
    def copies(s_refs, l_refs, probe=False):
        if probe:
            return [None] * (3 * len(s_refs))
        x, y, c = _place()
        out = []
        for a in range(len(s_refs)):
            for j, (px, py) in enumerate([(1 - x, y), (x, 1 - y), (1 - x, 1 - y)]):
                src = s_refs[a].at[2 * x + y] if arriving else s_refs[a].at[2 * px + py]
                out.append((a, j, src, l_refs[a].at[j], (px, py, c)))
        return out
    return copies


def _sibling_copies(s_refs, l_refs, probe=False):
    if probe:
        return [None] * (4 * len(s_refs))
    x, y, c = _place()
    return [(a, p, s_refs[a].at[p, 1 - c], l_refs[a].at[p], (x, y, 1 - c)) for a in range(len(s_refs)) for p in range(4)]


def _sibling_begin(name, by_owner, deps):
    lands = [lax.empty((4,) + a.shape[2:], a.dtype) for a in by_owner]
    return _split_copy_start(name + "_start", by_owner, lands, _sibling_copies, deps)


def _sibling_end(name, handle, after):
    send_sems, recv_sems, srcs, lands, _ = handle
    return _split_copy_wait(name + "_wait", send_sems, recv_sems, srcs, lands, _sibling_copies, after)


def _gather_begin(name, shards, deps):
    x, y, c = _place()
    lands = [lax.dynamic_update_slice_in_dim(lax.empty((N_DEV,) + a.shape, a.dtype), a[None], 4 * x + 2 * y + c, 0)
             for a in shards]
    return _split_copy_start(name + "_start", shards, lands, _gather_copies(False), deps)


def _gather_end(name, handle, after):
    send_sems, recv_sems, srcs, lands, _ = handle
    _, lands = _split_copy_wait(name + "_wait", send_sems, recv_sems, srcs, lands, _gather_copies(True), after)
    return _gather_forward(name + "_forward", lands)


def _gather_forward(name, lands):
    n = len(lands)

    def body(*refs):
        o_refs = refs[n:2 * n]
        send_sems, recv_sems = refs[2 * n:]
        x, y, c = _place()
        sibling = (x, y, 1 - c)
        chips = [(1 - x, y), (x, 1 - y), (1 - x, 1 - y)]
        sent = []
        for a in range(n):
            for j, (px, py) in enumerate(chips):
                rows = o_refs[a].at[4 * px + 2 * py + c]
                cp = pltpu.make_async_remote_copy(src_ref=rows, dst_ref=rows, send_sem=send_sems.at[a, j],
                                                  recv_sem=recv_sems.at[a, j], device_id=sibling, device_id_type=MESH)
                cp.start()
                sent.append(cp)
        for a in range(n):
            for j, (px, py) in enumerate(chips):
                rows = o_refs[a].at[4 * px + 2 * py + 1 - c]
                pltpu.make_async_remote_copy(src_ref=rows, dst_ref=rows, send_sem=send_sems.at[a, j],
                                             recv_sem=recv_sems.at[a, j], device_id=sibling, device_id_type=MESH).wait_recv()
        for cp in sent:
            cp.wait_send()

    return pl.pallas_call(
        body, in_specs=[ANY] * n, out_specs=[ANY] * n, out_shape=[_sds(a.shape, a.dtype) for a in lands],
        input_output_aliases={i: i for i in range(n)},
        scratch_shapes=[pltpu.SemaphoreType.DMA((n, 3)), pltpu.SemaphoreType.DMA((n, 3))], name=name,
    )(*lands)


def _chips_begin(name, pairs, deps):
    lands = [lax.empty((3,) + a.shape[1:], a.dtype) for a in pairs]
    return _split_copy_start(name + "_start", pairs, lands, _chip_copies(False), deps)


def _chips_end(name, handle, after):
    send_sems, recv_sems, srcs, lands, _ = handle
    return _split_copy_wait(name + "_wait", send_sems, recv_sems, srcs, lands, _chip_copies(True), after)


def _adamw_layer(name, l, own, lands, w, m, v, prev):
    nl, ng, r, c = w.shape
    tr, tc = _elem_tiles(r, c)
    c1 = 1.0 - ADAM_B1 ** ADAM_STEP
    c2 = 1.0 - ADAM_B2 ** ADAM_STEP

    def body(own_ref, lands_ref, w_ref, m_ref, v_ref, *rest):
        g_ref, d_ref, nm_ref, nv_ref = rest[-4:]
        g = own_ref[...].astype(F32) + lands_ref[0].astype(F32) + lands_ref[1].astype(F32) + lands_ref[2].astype(F32)
        nm = ADAM_B1 * m_ref[...] + (1.0 - ADAM_B1) * g
        nv = ADAM_B2 * v_ref[...] + (1.0 - ADAM_B2) * (g * g)
        m_hat = nm / c1
        v_hat = nv / c2
        g_ref[...] = g
        nm_ref[...] = nm
        nv_ref[...] = nv
        d_ref[...] = -ADAM_LR * (m_hat / (jnp.sqrt(v_hat) + ADAM_EPS) + ADAM_WD * w_ref[...])

    lay = pl.BlockSpec((None, None, tr, tc), lambda g, i, j: (l, g, i, j))
    out = _sds((nl, ng, r, c), F32)
    prev = [] if prev is None else list(prev)
    return pl.pallas_call(
        body, grid=(ng, r // tr, c // tc),
        in_specs=[pl.BlockSpec((None, tr, tc), lambda g, i, j: (g, i, j)),
                  pl.BlockSpec((3, None, tr, tc), lambda g, i, j: (0, g, i, j)), lay, lay, lay] + [ANY] * len(prev),
        out_specs=[lay] * 4, out_shape=[out] * 4, input_output_aliases={5 + i: i for i in range(len(prev))},
        compiler_params=_cp(), name=name,
    )(own, lands, w, m, v, *prev)


def _pad_pairs(a, axis, half, half_pad):
    shp = a.shape
    a = a.reshape(shp[:axis] + (2, half) + shp[axis + 1:])
    pad = [(0, 0)] * a.ndim
    pad[axis + 1] = (0, half_pad - half)
    a = jnp.pad(a, pad)
    return a.reshape(shp[:axis] + (2 * half_pad,) + shp[axis + 1:])


def _unpad_pairs(a, axis, half, half_pad):
    shp = a.shape
    a = a.reshape(shp[:axis] + (2, half_pad) + shp[axis + 1:])
    a = lax.slice_in_dim(a, 0, half, axis=axis + 1)
    return a.reshape(shp[:axis] + (2 * half,) + shp[axis + 1:])


def _blockdiag(w, nt):
    g, a, b = w.shape
    gl = g // nt
    e = jnp.eye(gl, dtype=w.dtype).reshape(1, gl, 1, gl, 1)
    return (w.reshape(nt, gl, a, 1, b) * e).reshape(nt, gl * a, gl * b)


def _diagblocks(m, g, a, b):
    nt = m.shape[0]
    gl = g // nt
    d = jnp.diagonal(m.reshape(nt, gl, a, gl, b), axis1=1, axis2=3)
    return jnp.moveaxis(d, -1, 1).reshape(g, a, b)


_PIECE = 8 * V7X_LANES


def _pack(pieces, row_multiple=512):
    rows = []
    for p in pieces:
        flat = p.reshape(-1).astype(F32)
        rows.append(jnp.pad(flat, (0, -flat.shape[0] % _PIECE)).reshape(-1, V7X_LANES))
    fill = -sum(r.shape[0] for r in rows) % row_multiple
    if fill:
        rows.append(jnp.zeros((fill, V7X_LANES), F32))
    return jnp.concatenate(rows, 0)


def _unpack(packed, shapes):
    lead = packed.shape[:-2]
    out, off = [], 0
    for shp in shapes:
        n = math.prod(shp)
        r = -(-n // _PIECE) * 8
        piece = packed[..., off:off + r, :].reshape(lead + (r * V7X_LANES,))[..., :n]
        out.append(piece.reshape(lead + tuple(shp)))
        off += r
    return out


def _ssm_discretise(lam_re, lam_im, log_dt, b_re, b_im):
    dt = jnp.exp(log_dt)[..., None]
    mag = jnp.exp(lam_re * dt)
    ab_re, ab_im = mag * jnp.cos(lam_im * dt), mag * jnp.sin(lam_im * dt)
    nr, ni = ab_re - 1.0, ab_im
    den = lam_re * lam_re + lam_im * lam_im
    zr = (nr * lam_re + ni * lam_im) / den
    zi = (ni * lam_re - nr * lam_im) / den
    bbr = zr[..., None] * b_re - zi[..., None] * b_im
    bbi = zr[..., None] * b_im + zi[..., None] * b_re
    return ab_re, ab_im, bbr, bbi


def _rope_tables(s):
    half = HEAD_DIM // 2
    inv = ROPE_THETA ** (-jnp.arange(half, dtype=F32) / half)
    ang = jnp.arange(s).astype(F32)[:, None] * inv[None, :]
    cos, sin = jnp.cos(ang), jnp.sin(ang)
    reps = V7X_LANES // HEAD_DIM
    return jnp.tile(jnp.concatenate([cos, cos], -1), (1, reps)), jnp.tile(jnp.concatenate([-sin, sin], -1), (1, reps))


_SMALL = ("attn_sinks", "pool_w", "pool_scale", "ssm_lam_re", "ssm_lam_im", "ssm_log_dt", "ssm_b_re", "ssm_b_im",
          "ssm_c_re", "ssm_c_im", "ssm_d", "ln1_g", "ln1_b", "ffn_conv_b", "ln2_g", "ln2_b")
_BIG = ("w_in", "ssm_glu_w", "w_out", "ffn_w_up", "ffn_w_down")
_RAW = ("attn_sinks", "pool_w", "pool_scale", "ssm_lam_re", "ssm_lam_im", "ssm_b_re", "ssm_b_im", "ssm_c_re", "ssm_c_im",
        "ssm_d", "ln1_g", "ln1_b", "ffn_conv_b", "ln2_g", "ln2_b", "ffn_conv_w")
_ORDER = ("w_in", "attn_sinks", "pool_w", "pool_scale", "ssm_lam_re", "ssm_lam_im", "ssm_log_dt", "ssm_b_re", "ssm_b_im",
          "ssm_c_re", "ssm_c_im", "ssm_d", "ssm_glu_w", "w_out", "ln1_g", "ln1_b", "ffn_w_up", "ffn_conv_w", "ffn_conv_b",
          "ffn_w_down", "ln2_g", "ln2_b")


def kernel(x, w_in, attn_sinks, pool_w, pool_scale, ssm_lam_re, ssm_lam_im, ssm_log_dt, ssm_b_re, ssm_b_im, ssm_c_re, ssm_c_im, ssm_d, ssm_glu_w, w_out, ln1_g, ln1_b, ffn_w_up, ffn_conv_w, ffn_conv_b, ffn_w_down, ln2_g, ln2_b, loss_target, m_w_in, m_attn_sinks, m_pool_w, m_pool_scale, m_ssm_lam_re, m_ssm_lam_im, m_ssm_log_dt, m_ssm_b_re, m_ssm_b_im, m_ssm_c_re, m_ssm_c_im, m_ssm_d, m_ssm_glu_w, m_w_out, m_ln1_g, m_ln1_b, m_ffn_w_up, m_ffn_conv_w, m_ffn_conv_b, m_ffn_w_down, m_ln2_g, m_ln2_b, v_w_in, v_attn_sinks, v_pool_w, v_pool_scale, v_ssm_lam_re, v_ssm_lam_im, v_ssm_log_dt, v_ssm_b_re, v_ssm_b_im, v_ssm_c_re, v_ssm_c_im, v_ssm_d, v_ssm_glu_w, v_w_out, v_ln1_g, v_ln1_b, v_ffn_w_up, v_ffn_conv_w, v_ffn_conv_b, v_ffn_w_down, v_ln2_g, v_ln2_b):
    W = dict(w_in=w_in, attn_sinks=attn_sinks, pool_w=pool_w, pool_scale=pool_scale, ssm_lam_re=ssm_lam_re, ssm_lam_im=ssm_lam_im, ssm_log_dt=ssm_log_dt, ssm_b_re=ssm_b_re, ssm_b_im=ssm_b_im, ssm_c_re=ssm_c_re, ssm_c_im=ssm_c_im, ssm_d=ssm_d, ssm_glu_w=ssm_glu_w, w_out=w_out, ln1_g=ln1_g, ln1_b=ln1_b, ffn_w_up=ffn_w_up, ffn_conv_w=ffn_conv_w, ffn_conv_b=ffn_conv_b, ffn_w_down=ffn_w_down, ln2_g=ln2_g, ln2_b=ln2_b)
    M = dict(w_in=m_w_in, attn_sinks=m_attn_sinks, pool_w=m_pool_w, pool_scale=m_pool_scale, ssm_lam_re=m_ssm_lam_re, ssm_lam_im=m_ssm_lam_im, ssm_log_dt=m_ssm_log_dt, ssm_b_re=m_ssm_b_re, ssm_b_im=m_ssm_b_im, ssm_c_re=m_ssm_c_re, ssm_c_im=m_ssm_c_im, ssm_d=m_ssm_d, ssm_glu_w=m_ssm_glu_w, w_out=m_w_out, ln1_g=m_ln1_g, ln1_b=m_ln1_b, ffn_w_up=m_ffn_w_up, ffn_conv_w=m_ffn_conv_w, ffn_conv_b=m_ffn_conv_b, ffn_w_down=m_ffn_w_down, ln2_g=m_ln2_g, ln2_b=m_ln2_b)
    V = dict(w_in=v_w_in, attn_sinks=v_attn_sinks, pool_w=v_pool_w, pool_scale=v_pool_scale, ssm_lam_re=v_ssm_lam_re, ssm_lam_im=v_ssm_lam_im, ssm_log_dt=v_ssm_log_dt, ssm_b_re=v_ssm_b_re, ssm_b_im=v_ssm_b_im, ssm_c_re=v_ssm_c_re, ssm_c_im=v_ssm_c_im, ssm_d=v_ssm_d, ssm_glu_w=v_ssm_glu_w, w_out=v_w_out, ln1_g=v_ln1_g, ln1_b=v_ln1_b, ffn_w_up=v_ffn_w_up, ffn_conv_w=v_ffn_conv_w, ffn_conv_b=v_ffn_conv_b, ffn_w_down=v_ffn_w_down, ln2_g=v_ln2_g, ln2_b=v_ln2_b)

    depth = w_in.shape[0]
    s, d = x.shape[1], x.shape[2]
    alpha = (2 * depth) ** 0.25
    attn_w = d // 2
    kv_w = attn_w // GQA
    nkv = kv_w // HEAD_DIM
    pool_wd = d // 4
    ssm_wd = d // 4
    n_groups = ssm_wd // SSM_GROUP
    state_w = n_groups * SSM_STATE
    nt_ssm = max(1, state_w // 512)
    o_k, o_v, o_p, o_s = attn_w, attn_w + kv_w, attn_w + 2 * kv_w, attn_w + 2 * kv_w + pool_wd
    in_w = o_s + ssm_wd
    half = ffn_w_down.shape[1]
    half_pad = -(-half // 64) * 64
    ffp = 4 * 2 * half_pad
    xi, yi, ci = _place()
    me = 4 * xi + 2 * yi + ci
    c_idx = jnp.reshape(ci, (1,)).astype(jnp.int32)

    cos_t, sin_t = _rope_tables(s)

    def layer_shards(l):
        return [
            jnp.transpose(w_in[l]).astype(_WIRE), ssm_glu_w[l].astype(_WIRE), w_out[l].astype(_WIRE),
            _pad_pairs(jnp.transpose(ffn_w_up[l]).astype(_WIRE), 0, half, half_pad),
            jnp.pad(ffn_w_down[l].astype(_WIRE), ((0, half_pad - half), (0, 0))),
        ]

    (g_cw,) = _all_gather("gather_conv_w", [_pad_pairs(ffn_conv_w, 2, half, half_pad)])

    def mixer_weights(l, gathered):
        g_in, g_glu, g_out = gathered
        return dict(
            win_t=g_in.reshape(in_w, d),
            glu=jnp.transpose(g_glu, (1, 0, 2)).reshape(ssm_wd, 2 * ssm_wd),
            wout=g_out.reshape(d, d),
            cw=jnp.transpose(g_cw[:, l], (1, 0, 2)).reshape(CONV_WIDTH, 2 * ffp),
            cb=_pad_pairs(ffn_conv_b[l].reshape(N_DEV, 2 * half), 1, half, half_pad).reshape(1, 2 * ffp),
        )

    def ffn_weights(gathered):
        g_up, g_down = gathered
        return dict(wup_t=g_up.reshape(2 * ffp, d), wdown=g_down.reshape(ffp, d))

    shards = [layer_shards(l) for l in range(depth)]
    full = [None] * depth
    gathers = {}

    def begin_gather(l, part, deps):
        arrays = shards[l][:3] if part == "mixer" else shards[l][3:]
        gathers[l, part] = _gather_begin(f"gather_{part}_weights_{l}", arrays, deps)
        return gathers[l, part][-1]

    def end_gather(l, part, after):
        return _gather_end(f"gather_{part}_weights_{l}", gathers.pop((l, part)), after)

    issued = begin_gather(0, "ffn", [begin_gather(0, "mixer", [g_cw])])
    full[0] = mixer_weights(0, end_gather(0, "mixer", g_cw))

    ssm_params = (ssm_lam_re, ssm_lam_im, ssm_log_dt, ssm_b_re, ssm_b_im)
    ab_re_all, ab_im_all, bbr_all, bbi_all = _ssm_discretise(*ssm_params)

    def ssm_maps(w):
        return jax.vmap(lambda t: _blockdiag(jnp.transpose(t, (0, 2, 1)), nt_ssm))(w).astype(_MXU)

    bdr_all, bdi_all, cdr_all, cdi_all = ssm_maps(bbr_all), ssm_maps(bbi_all), ssm_maps(ssm_c_re), ssm_maps(ssm_c_im)

    saved = []
    xf = x[0]
    xb = xf.astype(_MXU)
    for l in range(depth):
        fw = full[l]
        deps = ()
        if l + 1 < depth:
            issued = begin_gather(l + 1, "ffn", [begin_gather(l + 1, "mixer", [fw["win_t"], issued])])
            deps = (issued,)
        h = _mm_nt(f"in_proj_{l}", xb, fw["win_t"], deps=deps)
        q_rot, k_rot = _rope(f"rope_{l}", h, o_v, cos_t, sin_t, _MXU, ((0, o_k), (o_k, o_v)))
        k_hm = jnp.transpose(k_rot.reshape(s, nkv, HEAD_DIM), (1, 0, 2))
        v_hm = jnp.transpose(h[:, o_v:o_p].astype(_MXU).reshape(s, nkv, HEAD_DIM), (1, 0, 2))
        sinks = attn_sinks[l].reshape(nkv, GQA)
        o_attn, lse = _attn_fwd(f"attn_{l}", q_rot, k_hm, v_hm, sinks)
        pw_b = pool_w[l].astype(_MXU)
        psc = pool_scale[l].reshape(1, pool_wd)
        y_pool, pre = _pool_fwd(f"pool_{l}", h, o_p // pool_wd, pw_b, psc)
        bdr, bdi, cdr, cdi = bdr_all[l], bdi_all[l], cdr_all[l], cdi_all[l]
        dvec = ssm_d[l].reshape(1, ssm_wd)
        ar, ai = ab_re_all[l].reshape(1, state_w), ab_im_all[l].reshape(1, state_w)
        cw_ssm = ssm_wd // nt_ssm
        sr, si, ypre, yg = _ssm_fwd(f"ssm_{l}", h, o_s // cw_ssm, bdr, bdi, cdr, cdi, dvec, ar, ai)
        y_ssm, ab2 = _glu_fwd(f"glu_{l}", yg, fw["glu"])
        mix = jnp.concatenate([o_attn.astype(_MXU), y_pool, y_ssm], -1)
        a1 = _mm_nn(f"out_proj_{l}", mix, fw["wout"])
        g1, b1 = ln1_g[l].reshape(1, d), ln1_b[l].reshape(1, d)
        x1, x1b, xh1, rs1 = _ln_fwd(f"ln1_{l}", xf, a1, g1, b1, alpha)
        fw.update(ffn_weights(end_gather(l, "ffn", x1b)))
        hu = _mm_nt(f"ffn_up_{l}", x1b, fw["wup_t"], cap=2 * half_pad)
        act = _conv_act_fwd(f"ffn_act_{l}", hu, fw["cw"], fw["cb"])
        tkd = 2 * half_pad
        tnd = _col_tile(d, 1024)
        f_out = _mm(f"ffn_down_{l}", act, fw["wdown"], NN, (d // tnd, ffp // tkd),
                    pl.BlockSpec((s, tkd), lambda j, kk: (0, kk)), pl.BlockSpec((tkd, tnd), lambda j, kk: (kk, j)),
                    pl.BlockSpec((s, tnd), lambda j, kk: (0, j)), (s, d), F32)
        g2, b2 = ln2_g[l].reshape(1, d), ln2_b[l].reshape(1, d)
        x2, x2b, xh2, rs2 = _ln_fwd(f"ln2_{l}", x1, f_out, g2, b2, alpha)
        saved.append(dict(xb=xb, h=h, q_rot=q_rot, k_hm=k_hm, v_hm=v_hm, sinks=sinks, o_attn=o_attn, lse=lse, pw_b=pw_b, psc=psc,
                          pre=pre, bdr=bdr, bdi=bdi, cdr=cdr, cdi=cdi, dvec=dvec, ar=ar, ai=ai, sr=sr, si=si, ypre=ypre, yg=yg,
                          ab2=ab2, mix=mix, g1=g1, xh1=xh1, rs1=rs1, x1b=x1b, hu=hu, act=act, g2=g2, xh2=xh2, rs2=rs2))
        xf, xb = x2, x2b
        if l + 1 < depth:
            full[l + 1] = mixer_weights(l + 1, end_gather(l + 1, "mixer", x2b))

    dy, loss_part = _loss_head("loss_head", xf, loss_target[0])
    loss = lax.psum(loss_part[0, 0], ("x", "y", "c"))

    small_handles = [None] * depth
    small_parts = [None] * depth
    outs = {}
    big_res = {k: None for k in _BIG}
    my_chip = 2 * xi + yi
    pending = []

    transposed = ("w_in", "ffn_w_up")

    def row_groups(name_, t):
        g = 2 if name_ == "ffn_w_up" else 1
        return t.reshape(t.shape[:-2] + (g, t.shape[-2] // g, t.shape[-1]))

    def as_groups(name_, t):
        return row_groups(name_, jnp.transpose(t, (0, 2, 1)) if name_ in transposed else t)

    def from_groups(name_, t):
        t = t.reshape(t.shape[0], t.shape[1] * t.shape[2], t.shape[3])
        return jnp.transpose(t, (0, 2, 1)) if name_ in transposed else t

    grouped = {name_: tuple(as_groups(name_, t[name_]) for t in (W, M, V)) for name_ in _BIG}

    def finish_exchanges(after):
        while pending:
            lay, part, names, handle = pending.pop(0)
            pairs, lands = _chips_end(f"grads_between_chips_{part}_{lay}", handle, after)
            for name_, p, ld in zip(names, pairs, lands):
                own = row_groups(name_, lax.dynamic_index_in_dim(p, my_chip, 0, keepdims=False))
                big_res[name_] = _adamw_layer(f"adamw_{name_}_{lay}", lay, own, row_groups(name_, ld), *grouped[name_],
                                              big_res[name_])

    def begin_swap(lay, part, names, grads):
        by_owner = [a.reshape((4, 2) + a.shape[1:]) for a in grads]
        return lay, part, names, _sibling_begin(f"grads_to_sibling_{part}_{lay}", by_owner, [])

    def begin_exchange(swap, after):
        lay, part, names, handle = swap
        by_owner, theirs = _sibling_end(f"grads_to_sibling_{part}_{lay}", handle, after)
        pair = [_pair_sum(f"pair_sum_{name_}_{lay}", a, b, c_idx) for name_, a, b in zip(names, by_owner, theirs)]
        finish_exchanges(after)
        handle = _chips_begin(f"grads_between_chips_{part}_{lay}", pair, [])
        pending.append((lay, part, names, handle))
        return handle[-1]

    token = None
    small_handle = None
    for l in reversed(range(depth)):
        fw, sv = full[l], saved[l]
        deps = () if token is None else (token,)
        dr2, dr2b, dg2, db2 = _ln_bwd(f"ln2_bwd_{l}", dy, sv["xh2"], sv["rs2"], sv["g2"], deps=deps)
        d_wdown = _mm_tn_acols(f"ffn_down_dw_{l}", sv["act"], dr2b, _WIRE, cap=2 * half_pad)
        dact = _mm_nt(f"ffn_down_dx_{l}", dr2b, fw["wdown"], cap=2 * half_pad)
        dhu, dcw, dcb = _conv_act_bwd(f"ffn_act_bwd_{l}", dact, sv["hu"], fw["cw"], fw["cb"])
        d_wup = _mm(f"ffn_up_dw_{l}", dhu, sv["x1b"], TN, (N_DEV, 1),
                    pl.BlockSpec((None, s, 2 * half_pad), lambda j, kk: (j // 4, 0, j % 4)),
                    _resident((s, d), lambda j, kk: (0, 0)),
                    pl.BlockSpec((2 * half_pad, d), lambda j, kk: (j, 0)), (2 * ffp, d), _WIRE)
        swap = begin_swap(l, "ffn", ("ffn_w_up", "ffn_w_down"),
                          [d_wup.reshape(N_DEV, 2 * half_pad, d), d_wdown.reshape(N_DEV, half_pad, d)])
        tnd = _col_tile(d, 512)
        dy1 = _mm(f"ffn_up_dx_{l}", dhu, fw["wup_t"], NN, (d // tnd, N_DEV),
                  pl.BlockSpec((None, s, 2 * half_pad), lambda j, kk: (kk // 4, 0, kk % 4)),
                  pl.BlockSpec((2 * half_pad, tnd), lambda j, kk: (kk, j)),
                  pl.BlockSpec((s, tnd), lambda j, kk: (0, j)), (s, d), F32,
                  add=dr2, add_spec=pl.BlockSpec((s, tnd), lambda j, kk: (0, j)), add_scale=alpha, deps=(swap[3][-1],))
        token = begin_exchange(swap, dy1)
        dr1, dr1b, dg1, db1 = _ln_bwd(f"ln1_bwd_{l}", dy1, sv["xh1"], sv["rs1"], sv["g1"], deps=(token,))
        d_wout = _mm_tn_acols(f"out_proj_dw_{l}", sv["mix"], dr1b, _WIRE, cap=d // N_DEV)
        dmix = _mm_nt(f"out_proj_dx_{l}", dr1b, fw["wout"])
        dq_rot, dk_hm, dv_hm, dsk = _attn_bwd(f"attn_bwd_{l}", sv["q_rot"], sv["k_hm"], sv["v_hm"], sv["o_attn"], dmix,
                                             sv["lse"], sv["sinks"])
        dqk = jnp.concatenate([dq_rot, jnp.transpose(dk_hm, (1, 0, 2)).reshape(s, kv_w)], -1)
        dhq, dhk = _rope(f"rope_bwd_{l}", dqk, o_v, cos_t, -sin_t, _MXU, ((0, o_k), (o_k, o_v)))
        dhv = jnp.transpose(dv_hm, (1, 0, 2)).reshape(s, kv_w).astype(_MXU)
        dhp, dpw, dpsc = _pool_bwd(f"pool_bwd_{l}", dmix, attn_w // pool_wd, sv["pre"], sv["pw_b"], sv["psc"])
        dab2, dyg = _glu_bwd(f"glu_bwd_{l}", dmix, (attn_w + pool_wd) // ssm_wd, sv["ab2"], fw["glu"])
        d_glu = _mm_tn_bcols(f"glu_dw_{l}", sv["yg"], dab2, _WIRE)
        cw_ssm = ssm_wd // nt_ssm
        dhs, dd, dcdr, dcdi, dbdr, dbdi, dar, dai = _ssm_bwd(
            f"ssm_bwd_{l}", dyg, sv["ypre"], sv["h"], o_s // cw_ssm, sv["sr"], sv["si"], sv["bdr"], sv["bdi"], sv["cdr"],
            sv["cdi"], sv["dvec"], sv["ar"], sv["ai"])
        dh = jnp.concatenate([dhq, dhk, dhv, dhp, dhs], -1)
        d_win = _mm_tn_acols(f"in_proj_dw_{l}", dh, sv["xb"], _WIRE)
        swap = begin_swap(l, "mixer", ("w_in", "ssm_glu_w", "w_out"),
                          [d_win.reshape(N_DEV, in_w // N_DEV, d),
                           jnp.transpose(d_glu.reshape(ssm_wd, N_DEV, 2 * ssm_wd // N_DEV), (1, 0, 2)),
                           d_wout.reshape(N_DEV, d // N_DEV, d)])
        dy = _mm_nn(f"in_proj_dx_{l}", dh, fw["win_t"], add=dr1, add_scale=alpha, deps=(swap[3][-1],))

        raw = dict(attn_sinks=dsk, pool_w=dpw, pool_scale=dpsc, ssm_lam_re=dar, ssm_lam_im=dai,
                   ssm_b_re=_diagblocks(dbdr, n_groups, SSM_GROUP, SSM_STATE),
                   ssm_b_im=_diagblocks(dbdi, n_groups, SSM_GROUP, SSM_STATE),
                   ssm_c_re=_diagblocks(dcdr, n_groups, SSM_STATE, SSM_GROUP),
                   ssm_c_im=_diagblocks(dcdi, n_groups, SSM_STATE, SSM_GROUP), ssm_d=dd, ln1_g=dg1, ln1_b=db1,
                   ffn_conv_b=_unpad_pairs(dcb.reshape(N_DEV, 2 * half_pad), 1, half, half_pad), ln2_g=dg2, ln2_b=db2,
                   ffn_conv_w=_unpad_pairs(dcw.reshape(CONV_WIDTH, N_DEV, 2 * half_pad), 2, half, half_pad))
        raw_shapes = {k: raw[k].shape for k in _RAW}
        token = begin_exchange(swap, dy)
        small_handles[l] = _gather_begin(f"gather_small_grads_{l}", [_pack([raw[k] for k in _RAW])], [token])
        if l + 1 < depth:
            small_parts[l + 1] = _gather_end(f"gather_small_grads_{l + 1}", small_handles[l + 1], dy)[0]

    small_parts[0] = _gather_end("gather_small_grads_0", small_handles[0], token)[0]
    rows_l = small_parts[0].shape[1]
    summed = _sum_parts("sum_small_grads", jnp.concatenate(small_parts, 1)).reshape(depth, rows_l, V7X_LANES)
    g_small = dict(zip(_RAW, _unpack(summed, [raw_shapes[k] for k in _RAW])))
    swap_last = lambda t: jnp.transpose(t, (0, 1, 3, 2))
    _, vjp = jax.vjp(_ssm_discretise, *ssm_params)
    dlr, dli, dldt, dbr, dbi = vjp((g_small["ssm_lam_re"].reshape(depth, n_groups, SSM_STATE),
                                    g_small["ssm_lam_im"].reshape(depth, n_groups, SSM_STATE),
                                    swap_last(g_small["ssm_b_re"]), swap_last(g_small["ssm_b_im"])))
    g_small.update(ssm_lam_re=dlr, ssm_lam_im=dli, ssm_log_dt=dldt, ssm_b_re=dbr, ssm_b_im=dbi,
                   ssm_c_re=swap_last(g_small["ssm_c_re"]), ssm_c_im=swap_last(g_small["ssm_c_im"]),
                   ffn_conv_w=lax.dynamic_index_in_dim(g_small["ffn_conv_w"], me, axis=2, keepdims=False))
    for k in _SMALL + ("ffn_conv_w",):
        shp = W[k].shape
        two_d = (math.prod(shp[:-1]), shp[-1])
        res = _adamw_2d(f"adamw_{k}", g_small[k].reshape(two_d), *(t[k].reshape(two_d) for t in (W, M, V)))
        outs[k] = (g_small[k].reshape(shp),) + tuple(a.reshape(shp) for a in res)

    finish_exchanges(outs["ln2_b"][1])
    for name_ in _BIG:
        outs[name_] = tuple(from_groups(name_, t) for t in big_res[name_])

    grad_x = dy[None]
    result = [loss, grad_x]
    for i in range(4):
        result += [outs[k][i] for k in _ORDER]
    return tuple(result)
```

```python
import functools
import math

import jax
import jax.numpy as jnp
from jax import lax
from jax.experimental import pallas as pl
from jax.experimental.pallas import tpu as pltpu

F32 = jnp.float32
_MXU = jnp.bfloat16
_WIRE = jnp.bfloat16

HEAD_DIM = 64
GQA = 4
ATTN_BLOCK = 128
ROPE_THETA = 10000.0
POOL_WINDOWS = (2, 4, 8, 16)
SSM_GROUP = 16
SSM_STATE = 64
CONV_WIDTH = 3
LN_EPS = 1e-5
ADAM_LR, ADAM_B1, ADAM_B2, ADAM_EPS, ADAM_WD, ADAM_STEP = 0.001, 0.9, 0.999, 1e-08, 0.01, 10

N_DEV = 8
V7X_LANES = 128
V7X_VMEM_LIMIT = 56 * 1024 * 1024
SCAN_T = 64
SCAN_LANES = 256
MESH = pl.DeviceIdType.MESH
ANY = pl.BlockSpec(memory_space=pl.ANY)


def _cp():
    return pltpu.CompilerParams(vmem_limit_bytes=V7X_VMEM_LIMIT)


def _resident(block, index_map):
    return pl.BlockSpec(block, index_map, pipeline_mode=pl.Buffered(1))


def _sds(shape, dtype):
    return jax.ShapeDtypeStruct(tuple(shape), dtype)


def _mm(name, a, b, dims, grid, a_spec, b_spec, o_spec, out_shape, out_dtype, add=None, add_spec=None, add_scale=1.0, deps=()):
    nk = grid[1]
    n_in = 2 + (add is not None) + len(deps)
    oblk = tuple(d for d in o_spec.block_shape if d is not None)
    scratch = nk > 1 and out_dtype != F32

    def body(*refs):
        a_ref, b_ref = refs[:2]
        add_ref = None if add is None else refs[2]
        o_ref = refs[n_in]
        acc_ref = refs[-1] if scratch else None

        def finish(r):
            if add_ref is not None:
                r = r + add_scale * add_ref[...]
            o_ref[...] = r.astype(o_ref.dtype)

        part = lax.dot_general(a_ref[...], b_ref[...], (dims, ((), ())), preferred_element_type=F32)
        if nk == 1:
            finish(part)
        elif not scratch:
            k = pl.program_id(1)

            @pl.when(k == 0)
            def _():
                o_ref[...] = part

            @pl.when(k > 0)
            def _():
                o_ref[...] += part

            if add_ref is not None:
                @pl.when(k == nk - 1)
                def _():
                    o_ref[...] += add_scale * add_ref[...]
        else:
            k = pl.program_id(1)

            @pl.when(k == 0)
            def _():
                acc_ref[...] = part

            @pl.when(k > 0)
            def _():
                acc_ref[...] += part

            @pl.when(k == nk - 1)
            def _():
                finish(acc_ref[...])

    ins = [a, b] + ([] if add is None else [add]) + list(deps)
    in_specs = [a_spec, b_spec] + ([] if add is None else [add_spec]) + [ANY] * len(deps)
    return pl.pallas_call(
        body, grid=grid, in_specs=in_specs, out_specs=o_spec, out_shape=_sds(out_shape, out_dtype),
        scratch_shapes=[pltpu.VMEM(oblk, F32)] if scratch else [], compiler_params=_cp(), name=name,
    )(*ins)


NN = ((1,), (0,))
NT = ((1,), (1,))
TN = ((0,), (0,))


def _col_tile(n, cap=512):
    if n % V7X_LANES:
        return n
    t = min(cap, n)
    t -= t % V7X_LANES
    while n % t:
        t -= V7X_LANES
    return t


def _mm_nn(name, a, b, out_dtype=F32, cap=512, add=None, add_scale=1.0, deps=()):
    m, k = a.shape
    n = b.shape[1]
    tn = _col_tile(n, cap)
    o_spec = pl.BlockSpec((m, tn), lambda j, kk: (0, j))
    return _mm(name, a, b, NN, (n // tn, 1), _resident((m, k), lambda j, kk: (0, 0)),
               pl.BlockSpec((k, tn), lambda j, kk: (0, j)), o_spec, (m, n), out_dtype,
               add=add, add_spec=None if add is None else o_spec, add_scale=add_scale, deps=deps)


def _mm_nt(name, a, b, out_dtype=F32, add=None, add_scale=1.0, cap=512, deps=()):
    m, k = a.shape
    n = b.shape[0]
    tn = _col_tile(n, cap)
    o_spec = pl.BlockSpec((m, tn), lambda j, kk: (0, j))
    return _mm(name, a, b, NT, (n // tn, 1), _resident((m, k), lambda j, kk: (0, 0)),
               pl.BlockSpec((tn, k), lambda j, kk: (j, 0)), o_spec, (m, n), out_dtype,
               add=add, add_spec=None if add is None else o_spec, add_scale=add_scale, deps=deps)


def _mm_tn_bcols(name, a, b, out_dtype, cap=512):
    s, k = a.shape
    n = b.shape[1]
    tn = _col_tile(n, cap)
    return _mm(name, a, b, TN, (n // tn, 1), _resident((s, k), lambda j, kk: (0, 0)),
               pl.BlockSpec((s, tn), lambda j, kk: (0, j)), pl.BlockSpec((k, tn), lambda j, kk: (0, j)), (k, n), out_dtype)


def _mm_tn_acols(name, a, b, out_dtype, cap=512):
    s, k = a.shape
    n = b.shape[1]
    tk = _col_tile(k, cap)
    return _mm(name, a, b, TN, (k // tk, 1), pl.BlockSpec((s, tk), lambda i, kk: (0, i)),
               _resident((s, n), lambda i, kk: (0, 0)), pl.BlockSpec((tk, n), lambda i, kk: (i, 0)), (k, n), out_dtype)


def _ln_fwd(name, x, a, g, b, alpha):
    s, d = x.shape
    tr = min(256, s)

    def body(x_ref, a_ref, g_ref, b_ref, y_ref, yb_ref, xh_ref, rs_ref):
        r = alpha * x_ref[...] + a_ref[...]
        mu = jnp.mean(r, -1, keepdims=True)
        c = r - mu
        var = jnp.mean(c * c, -1, keepdims=True)
        rstd = lax.rsqrt(var + LN_EPS)
        xh = c * rstd
        y = xh * g_ref[...] + b_ref[...]
        y_ref[...] = y
        yb_ref[...] = y.astype(_MXU)
        xh_ref[...] = xh
        rs_ref[...] = rstd

    row = pl.BlockSpec((tr, d), lambda i: (i, 0))
    vec = pl.BlockSpec((1, d), lambda i: (0, 0))
    return pl.pallas_call(
        body, grid=(s // tr,), in_specs=[row, row, vec, vec],
        out_specs=[row, row, row, pl.BlockSpec((tr, 1), lambda i: (i, 0))],
        out_shape=[_sds((s, d), F32), _sds((s, d), _MXU), _sds((s, d), F32), _sds((s, 1), F32)],
        compiler_params=_cp(), name=name,
    )(x, a, g, b)


def _ln_bwd(name, dy, xh, rstd, g, deps=()):
    s, d = dy.shape
    tr = min(256, s)
    nd = len(deps)

    def body(dy_ref, xh_ref, rs_ref, g_ref, *rest):
        dr_ref, drb_ref, dg_ref, db_ref = rest[nd:]
        i = pl.program_id(0)
        dy_ = dy_ref[...]
        xh_ = xh_ref[...]
        dxh = dy_ * g_ref[...]
        m1 = jnp.mean(dxh, -1, keepdims=True)
        m2 = jnp.mean(dxh * xh_, -1, keepdims=True)
        dr = rs_ref[...] * (dxh - m1 - xh_ * m2)
        dr_ref[...] = dr
        drb_ref[...] = dr.astype(_MXU)
        pg = jnp.sum(dy_ * xh_, 0, keepdims=True)
        pb = jnp.sum(dy_, 0, keepdims=True)

        @pl.when(i == 0)
        def _():
            dg_ref[...] = pg
            db_ref[...] = pb

        @pl.when(i > 0)
        def _():
            dg_ref[...] += pg
            db_ref[...] += pb

    row = pl.BlockSpec((tr, d), lambda i: (i, 0))
    vec = pl.BlockSpec((1, d), lambda i: (0, 0))
    return pl.pallas_call(
        body, grid=(s // tr,), in_specs=[row, row, pl.BlockSpec((tr, 1), lambda i: (i, 0)), vec] + [ANY] * nd,
        out_specs=[row, row, vec, vec],
        out_shape=[_sds((s, d), F32), _sds((s, d), _MXU), _sds((1, d), F32), _sds((1, d), F32)],
        compiler_params=_cp(), name=name,
    )(dy, xh, rstd, g, *deps)


def _loss_head(name, y, target):
    s, d = y.shape
    tr = min(256, s)

    def body(y_ref, t_ref, dy_ref, l_ref):
        i = pl.program_id(0)
        e = y_ref[...] - t_ref[...]
        dy_ref[...] = e * (1.0 / d)
        part = 0.5 * jnp.sum(jnp.mean(e * e, -1, keepdims=True), 0, keepdims=True)

        @pl.when(i == 0)
        def _():
            l_ref[...] = part

        @pl.when(i > 0)
        def _():
            l_ref[...] += part

    row = pl.BlockSpec((tr, d), lambda i: (i, 0))
    return pl.pallas_call(
        body, grid=(s // tr,), in_specs=[row, row], out_specs=[row, pl.BlockSpec((1, 1), lambda i: (0, 0))],
        out_shape=[_sds((s, d), F32), _sds((1, 1), F32)], compiler_params=_cp(), name=name,
    )(y, target)


def _rope(name, t, width, cos, sin, out_dtype, splits):
    s = t.shape[0]
    tr = min(256, s)
    assert width % V7X_LANES == 0

    def body(t_ref, c_ref, s_ref, *o_refs):
        lane = lax.broadcasted_iota(jnp.int32, (tr, V7X_LANES), 1)
        first = (lane % HEAD_DIM) < (HEAD_DIM // 2)
        cs, sn = c_ref[...], s_ref[...]
        for (lo, hi), o_ref in zip(splits, o_refs):
            for c0 in range(lo, hi, V7X_LANES):
                v = t_ref[:, c0:c0 + V7X_LANES].astype(F32)
                partner = jnp.where(first, pltpu.roll(v, V7X_LANES - HEAD_DIM // 2, 1), pltpu.roll(v, HEAD_DIM // 2, 1))
                o_ref[:, c0 - lo:c0 - lo + V7X_LANES] = (v * cs + partner * sn).astype(o_ref.dtype)

    tab = pl.BlockSpec((tr, V7X_LANES), lambda i: (i, 0))
    return pl.pallas_call(
        body, grid=(s // tr,), in_specs=[pl.BlockSpec((tr, width), lambda i: (i, 0)), tab, tab],
        out_specs=[pl.BlockSpec((tr, hi - lo), lambda i: (i, 0)) for lo, hi in splits],
        out_shape=[_sds((s, hi - lo), out_dtype) for lo, hi in splits], compiler_params=_cp(), name=name,
    )(t, cos, sin)


def _attn_masks():
    i = lax.broadcasted_iota(jnp.int32, (GQA * ATTN_BLOCK, 2 * ATTN_BLOCK), 0) % ATTN_BLOCK
    j = lax.broadcasted_iota(jnp.int32, (GQA * ATTN_BLOCK, 2 * ATTN_BLOCK), 1)
    cur_ok = jnp.logical_and(j >= ATTN_BLOCK, j - ATTN_BLOCK <= i)
    prev_ok = jnp.logical_and(j < ATTN_BLOCK, j > i)
    return cur_ok, prev_ok


def _attn_scores(q4, kcat, n, cur_ok, prev_ok):
    sc = lax.dot_general(q4, kcat, (NT, ((), ())), preferred_element_type=F32) * (HEAD_DIM ** -0.5)
    return jnp.where(jnp.logical_or(cur_ok, jnp.logical_and(prev_ok, n > 0)), sc, -1e30)


def _stack_heads(ref, rows):
    return jnp.concatenate([ref[rows, g * HEAD_DIM:(g + 1) * HEAD_DIM] for g in range(GQA)], 0)


def _per_head_column(values):
    r = lax.broadcasted_iota(jnp.int32, (GQA * ATTN_BLOCK, 1), 0) // ATTN_BLOCK
    col = jnp.zeros((GQA * ATTN_BLOCK, 1), F32)
    for g, val in enumerate(values):
        col = jnp.where(r == g, val, col)
    return col


def _attn_fwd(name, q, k, v, sinks):
    s = q.shape[0]
    nkv = k.shape[0]
    gw = GQA * HEAD_DIM
    nb = s // ATTN_BLOCK

    def body(sk_ref, q_ref, k_ref, v_ref, o_ref, lse_ref):
        h = pl.program_id(0)
        cur_ok, prev_ok = _attn_masks()
        sink = _per_head_column([sk_ref[h, g] for g in range(GQA)])

        def blk(n, carry):
            rows = pl.ds(pl.multiple_of(n * ATTN_BLOCK, ATTN_BLOCK), ATTN_BLOCK)
            prows = pl.ds(pl.multiple_of(jnp.maximum(n - 1, 0) * ATTN_BLOCK, ATTN_BLOCK), ATTN_BLOCK)
            kcat = jnp.concatenate([k_ref[prows, :], k_ref[rows, :]], 0)
            vcat = jnp.concatenate([v_ref[prows, :], v_ref[rows, :]], 0)
            sc = _attn_scores(_stack_heads(q_ref, rows), kcat, n, cur_ok, prev_ok)
            m = jnp.maximum(sc.max(-1, keepdims=True), sink)
            p = jnp.exp(sc - m)
            den = p.sum(-1, keepdims=True) + jnp.exp(sink - m)
            o = jnp.dot((p / den).astype(_MXU), vcat, preferred_element_type=F32)
            lse = m + jnp.log(den)
            for g in range(GQA):
                mine = slice(g * ATTN_BLOCK, (g + 1) * ATTN_BLOCK)
                o_ref[rows, g * HEAD_DIM:(g + 1) * HEAD_DIM] = o[mine, :]
                lse_ref[rows, g:g + 1] = lse[mine, :]
            return carry

        lax.fori_loop(0, nb, blk, 0)

    kv_spec = pl.BlockSpec((None, s, HEAD_DIM), lambda h: (h, 0, 0))
    return pl.pallas_call(
        body, grid=(nkv,),
        in_specs=[pl.BlockSpec(memory_space=pltpu.SMEM), pl.BlockSpec((s, gw), lambda h: (0, h)), kv_spec, kv_spec],
        out_specs=[pl.BlockSpec((s, gw), lambda h: (0, h)), pl.BlockSpec((None, s, GQA), lambda h: (h, 0, 0))],
        out_shape=[_sds((s, nkv * gw), F32), _sds((nkv, s, GQA), F32)], compiler_params=_cp(), name=name,
    )(sinks, q, k, v)


def _attn_bwd(name, q, k, v, o, dmix, lse, sinks):
    s = q.shape[0]
    nkv = k.shape[0]
    gw = GQA * HEAD_DIM
    nb = s // ATTN_BLOCK
    scale = HEAD_DIM ** -0.5

    def body(sk_ref, q_ref, k_ref, v_ref, o_ref, do_ref, lse_ref, dq_ref, dk_ref, dv_ref, dsk_ref):
        h = pl.program_id(0)
        cur_ok, prev_ok = _attn_masks()
        dk_ref[...] = jnp.zeros_like(dk_ref)
        dv_ref[...] = jnp.zeros_like(dv_ref)

        sink = _per_head_column([sk_ref[h, g] for g in range(GQA)])

        def blk(n, acc):
            rows = pl.ds(pl.multiple_of(n * ATTN_BLOCK, ATTN_BLOCK), ATTN_BLOCK)
            prows = pl.ds(pl.multiple_of(jnp.maximum(n - 1, 0) * ATTN_BLOCK, ATTN_BLOCK), ATTN_BLOCK)
            kcat = jnp.concatenate([k_ref[prows, :], k_ref[rows, :]], 0)
            vcat = jnp.concatenate([v_ref[prows, :], v_ref[rows, :]], 0)
            q4 = _stack_heads(q_ref, rows)
            do4 = _stack_heads(do_ref, rows)
            delta = jnp.sum(do4 * _stack_heads(o_ref, rows), -1, keepdims=True)
            dob = do4.astype(_MXU)
            lse = jnp.concatenate([lse_ref[rows, g:g + 1] for g in range(GQA)], 0)
            p = jnp.exp(_attn_scores(q4, kcat, n, cur_ok, prev_ok) - lse)
            dp = lax.dot_general(dob, vcat, (NT, ((), ())), preferred_element_type=F32)
            ds = (p * (dp - delta) * scale).astype(_MXU)
            dq = jnp.dot(ds, kcat, preferred_element_type=F32)
            for g in range(GQA):
                dq_ref[rows, g * HEAD_DIM:(g + 1) * HEAD_DIM] = dq[g * ATTN_BLOCK:(g + 1) * ATTN_BLOCK, :]
            dk = lax.dot_general(ds, q4, (TN, ((), ())), preferred_element_type=F32)
            dv = lax.dot_general(p.astype(_MXU), dob, (TN, ((), ())), preferred_element_type=F32)
            dk_ref[prows, :] += dk[:ATTN_BLOCK, :]
            dv_ref[prows, :] += dv[:ATTN_BLOCK, :]
            dk_ref[rows, :] += dk[ATTN_BLOCK:, :]
            dv_ref[rows, :] += dv[ATTN_BLOCK:, :]
            return acc - jnp.exp(sink - lse) * delta

        acc = lax.fori_loop(0, nb, blk, jnp.zeros((GQA * ATTN_BLOCK, 1), F32))
        for g in range(GQA):
            dsk_ref[:, g:g + 1] = jnp.sum(acc[g * ATTN_BLOCK:(g + 1) * ATTN_BLOCK, :], 0, keepdims=True)

    kv_spec = pl.BlockSpec((None, s, HEAD_DIM), lambda h: (h, 0, 0))
    qcols = pl.BlockSpec((s, gw), lambda h: (0, h))
    return pl.pallas_call(
        body, grid=(nkv,),
        in_specs=[pl.BlockSpec(memory_space=pltpu.SMEM), qcols, kv_spec, kv_spec, qcols, qcols,
                  pl.BlockSpec((None, s, GQA), lambda h: (h, 0, 0))],
        out_specs=[qcols, kv_spec, kv_spec, pl.BlockSpec((None, 1, GQA), lambda h: (h, 0, 0))],
        out_shape=[_sds((s, nkv * gw), F32), _sds((nkv, s, HEAD_DIM), F32), _sds((nkv, s, HEAD_DIM), F32),
                   _sds((nkv, 1, GQA), F32)],
        compiler_params=_cp(), name=name,
    )(sinks, q, k, v, o, dmix, lse)


def _shift_down(a, k, t):
    return jnp.where(t >= k, pltpu.roll(a, k, 0), 0.0)


def _shift_up(a, k, t):
    n = a.shape[0]
    return jnp.where(t < n - k, pltpu.roll(a, n - k, 0), 0.0)


def _pool_fwd(name, h, col_block, pool_w, pool_scale):
    s = h.shape[0]
    ng, pg = pool_w.shape[0], pool_w.shape[1]
    pw_ = ng * pg

    def body(u_ref, w_ref, sc_ref, y_ref, pre_ref):
        t = lax.broadcasted_iota(jnp.int32, (s, pg), 0)
        for gi, win in enumerate(POOL_WINDOWS):
            cols = slice(gi * pg, (gi + 1) * pg)
            u = u_ref[:, cols]
            a = u
            k = 1
            while k < win:
                a = a + _shift_down(a, k, t)
                k *= 2
            div = jnp.minimum(t + 1, win).astype(F32)
            pre = (a / div - u).astype(_MXU)
            pre_ref[:, cols] = pre
            out = jnp.dot(pre, w_ref[gi], preferred_element_type=F32)
            y_ref[:, cols] = (out * sc_ref[:, cols]).astype(y_ref.dtype)

    blk = pl.BlockSpec((s, pw_), lambda i: (0, 0))
    return pl.pallas_call(
        body, grid=(1,),
        in_specs=[pl.BlockSpec((s, pw_), lambda i: (0, col_block)), pl.BlockSpec((ng, pg, pg), lambda i: (0, 0, 0)),
                  pl.BlockSpec((1, pw_), lambda i: (0, 0))],
        out_specs=[blk, blk], out_shape=[_sds((s, pw_), _MXU), _sds((s, pw_), _MXU)], compiler_params=_cp(), name=name,
    )(h, pool_w, pool_scale)


def _pool_bwd(name, dmix, col_block, pre, pool_w, pool_scale):
    s = pre.shape[0]
    ng, pg = pool_w.shape[0], pool_w.shape[1]
    pw_ = ng * pg

    def body(dy_ref, pre_ref, w_ref, sc_ref, du_ref, dw_ref, dsc_ref):
        t = lax.broadcasted_iota(jnp.int32, (s, pg), 0)
        for gi, win in enumerate(POOL_WINDOWS):
            cols = slice(gi * pg, (gi + 1) * pg)
            pre_g = pre_ref[:, cols]
            dy = dy_ref[:, cols]
            out = jnp.dot(pre_g, w_ref[gi], preferred_element_type=F32)
            dsc_ref[:, cols] = jnp.sum(dy * out, 0, keepdims=True)
            dout = (dy * sc_ref[:, cols]).astype(_MXU)
            dw_ref[gi] = lax.dot_general(pre_g, dout, (TN, ((), ())), preferred_element_type=F32)
            dpre = lax.dot_general(dout, w_ref[gi], (NT, ((), ())), preferred_element_type=F32)
            div = jnp.minimum(t + 1, win).astype(F32)
            a = dpre / div
            k = 1
            while k < win:
                a = a + _shift_up(a, k, t)
                k *= 2
            du_ref[:, cols] = (a - dpre).astype(du_ref.dtype)

    blk = pl.BlockSpec((s, pw_), lambda i: (0, 0))
    wspec = pl.BlockSpec((ng, pg, pg), lambda i: (0, 0, 0))
    vec = pl.BlockSpec((1, pw_), lambda i: (0, 0))
    return pl.pallas_call(
        body, grid=(1,), in_specs=[pl.BlockSpec((s, pw_), lambda i: (0, col_block)), blk, wspec, vec],
        out_specs=[blk, wspec, vec], out_shape=[_sds((s, pw_), _MXU), _sds((ng, pg, pg), F32), _sds((1, pw_), F32)],
        compiler_params=_cp(), name=name,
    )(dmix, pre, pool_w, pool_scale)


def _scan_chunks(xr_ref, xi_ref, sr_ref, si_ref, ar, ai, reverse):
    n, c = xr_ref.shape
    tt = min(SCAN_T, n)
    lw = min(SCAN_LANES, c)
    nchunk = n // tt
    t = lax.broadcasted_iota(jnp.int32, (tt, lw), 0)

    for l0 in range(0, c, lw):
        lanes = slice(l0, l0 + lw)
        a_r, a_i = ar[:, lanes], ai[:, lanes]

        def local(vr, vi, a_r=a_r, a_i=a_i):
            pr, pi = a_r, a_i
            k = 1
            while k < tt:
                if reverse:
                    hr, hi = _shift_up(vr, k, t), _shift_up(vi, k, t)
                else:
                    hr, hi = _shift_down(vr, k, t), _shift_down(vi, k, t)
                vr, vi = vr + pr * hr - pi * hi, vi + pr * hi + pi * hr
                pr, pi = pr * pr - pi * pi, 2.0 * pr * pi
                k *= 2
            return vr, vi

        edge = tt - 1 if reverse else 0
        pw_r, pw_i = local(jnp.where(t == edge, a_r, 0.0), jnp.where(t == edge, a_i, 0.0))
        last = 0 if reverse else tt - 1

        def body(i, carry, lanes=lanes, local=local, pw_r=pw_r, pw_i=pw_i):
            cr, ci = carry
            ch = nchunk - 1 - i if reverse else i
            rows = pl.ds(pl.multiple_of(ch * tt, tt), tt)
            vr, vi = local(xr_ref[rows, lanes], xi_ref[rows, lanes])
            vr2 = vr + pw_r * cr - pw_i * ci
            vi2 = vi + pw_r * ci + pw_i * cr
            sr_ref[rows, lanes] = vr2
            si_ref[rows, lanes] = vi2
            return vr2[last:last + 1, :], vi2[last:last + 1, :]

        lax.fori_loop(0, nchunk, body, (jnp.zeros((1, lw), F32), jnp.zeros((1, lw), F32)))


_GELU_K = math.sqrt(2.0 / math.pi)


def _gelu_grad(y):
    inner = _GELU_K * (y + 0.044715 * y * y * y)
    th = jnp.tanh(inner)
    return 0.5 * (1.0 + th) + 0.5 * y * (1.0 - th * th) * _GELU_K * (1.0 + 3.0 * 0.044715 * y * y)


def _ssm_fwd(name, h, u_block0, bdr, bdi, cdr, cdi, dvec, ar, ai):
    s = h.shape[0]
    nt, cw, lw = bdr.shape
    rc = min(256, s)

    def body(u_ref, bdr_ref, bdi_ref, cdr_ref, cdi_ref, d_ref, ar_ref, ai_ref, sr_ref, si_ref, y_ref, yg_ref):
        def mm_in(c, _):
            rows = pl.ds(pl.multiple_of(c * rc, rc), rc)
            ub = u_ref[rows, :].astype(_MXU)
            sr_ref[rows, :] = jnp.dot(ub, bdr_ref[...], preferred_element_type=F32)
            si_ref[rows, :] = jnp.dot(ub, bdi_ref[...], preferred_element_type=F32)
            return 0

        lax.fori_loop(0, s // rc, mm_in, 0)
        _scan_chunks(sr_ref, si_ref, sr_ref, si_ref, ar_ref[...], ai_ref[...], reverse=False)

        def mm_out(c, _):
            rows = pl.ds(pl.multiple_of(c * rc, rc), rc)
            y = (jnp.dot(sr_ref[rows, :].astype(_MXU), cdr_ref[...], preferred_element_type=F32)
                 - jnp.dot(si_ref[rows, :].astype(_MXU), cdi_ref[...], preferred_element_type=F32)
                 + d_ref[...] * u_ref[rows, :])
            y_ref[rows, :] = y
            yg_ref[rows, :] = jax.nn.gelu(y).astype(yg_ref.dtype)
            return 0

        lax.fori_loop(0, s // rc, mm_out, 0)

    st = pl.BlockSpec((s, lw), lambda j: (0, j))
    ch = pl.BlockSpec((s, cw), lambda j: (0, j))
    bspec = pl.BlockSpec((None, cw, lw), lambda j: (j, 0, 0))
    cspec = pl.BlockSpec((None, lw, cw), lambda j: (j, 0, 0))
    return pl.pallas_call(
        body, grid=(nt,),
        in_specs=[pl.BlockSpec((s, cw), lambda j: (0, u_block0 + j)), bspec, bspec, cspec, cspec,
                  pl.BlockSpec((1, cw), lambda j: (0, j)), pl.BlockSpec((1, lw), lambda j: (0, j)),
                  pl.BlockSpec((1, lw), lambda j: (0, j))],
        out_specs=[st, st, ch, ch],
        out_shape=[_sds((s, nt * lw), F32), _sds((s, nt * lw), F32), _sds((s, nt * cw), F32), _sds((s, nt * cw), _MXU)],
        compiler_params=_cp(), name=name,
    )(h, bdr, bdi, cdr, cdi, dvec, ar, ai)


def _ssm_bwd(name, dyg, ypre, h, u_block0, sr, si, bdr, bdi, cdr, cdi, dvec, ar, ai):
    s = h.shape[0]
    nt, cw, lw = bdr.shape
    rc = min(256, s)

    def body(dyg_ref, yp_ref, u_ref, sr_ref, si_ref, bdr_ref, bdi_ref, cdr_ref, cdi_ref, d_ref, ar_ref, ai_ref,
             du_ref, dd_ref, dcr_ref, dci_ref, dbr_ref, dbi_ref, dar_ref, dai_ref, lr_scr, li_scr, dy_scr):
        for ref in (dd_ref, dcr_ref, dci_ref, dbr_ref, dbi_ref, dar_ref, dai_ref):
            ref[...] = jnp.zeros_like(ref)

        def p1(c, _):
            rows = pl.ds(pl.multiple_of(c * rc, rc), rc)
            dy = dyg_ref[rows, :] * _gelu_grad(yp_ref[rows, :])
            dy_scr[rows, :] = dy
            dd_ref[...] += jnp.sum(dy * u_ref[rows, :], 0, keepdims=True)
            dyb = dy.astype(_MXU)
            lr_scr[rows, :] = lax.dot_general(dyb, cdr_ref[...], (NT, ((), ())), preferred_element_type=F32)
            li_scr[rows, :] = -lax.dot_general(dyb, cdi_ref[...], (NT, ((), ())), preferred_element_type=F32)
            dcr_ref[...] += lax.dot_general(sr_ref[rows, :].astype(_MXU), dyb, (TN, ((), ())), preferred_element_type=F32)
            dci_ref[...] -= lax.dot_general(si_ref[rows, :].astype(_MXU), dyb, (TN, ((), ())), preferred_element_type=F32)
            return 0

        lax.fori_loop(0, s // rc, p1, 0)
        _scan_chunks(lr_scr, li_scr, lr_scr, li_scr, ar_ref[...], -ai_ref[...], reverse=True)
        t = lax.broadcasted_iota(jnp.int32, (rc, lw), 0)

        def p2(c, _):
            r0 = pl.multiple_of(c * rc, rc)
            rows = pl.ds(r0, rc)
            before = pl.ds(pl.multiple_of(jnp.maximum(r0 - 8, 0), 8), 8)
            have = (c > 0).astype(F32)
            lr, li = lr_scr[rows, :], li_scr[rows, :]
            spr = jnp.where(t == 0, sr_ref[before, :][7:8, :] * have, pltpu.roll(sr_ref[rows, :], 1, 0))
            spi = jnp.where(t == 0, si_ref[before, :][7:8, :] * have, pltpu.roll(si_ref[rows, :], 1, 0))
            dar_ref[...] += jnp.sum(lr * spr + li * spi, 0, keepdims=True)
            dai_ref[...] += jnp.sum(li * spr - lr * spi, 0, keepdims=True)
            lrb, lib = lr.astype(_MXU), li.astype(_MXU)
            du = (dy_scr[rows, :] * d_ref[...]
                  + lax.dot_general(lrb, bdr_ref[...], (NT, ((), ())), preferred_element_type=F32)
                  + lax.dot_general(lib, bdi_ref[...], (NT, ((), ())), preferred_element_type=F32))
            du_ref[rows, :] = du.astype(du_ref.dtype)
            ub = u_ref[rows, :].astype(_MXU)
            dbr_ref[...] += lax.dot_general(ub, lrb, (TN, ((), ())), preferred_element_type=F32)
            dbi_ref[...] += lax.dot_general(ub, lib, (TN, ((), ())), preferred_element_type=F32)
            return 0

        lax.fori_loop(0, s // rc, p2, 0)

    st = pl.BlockSpec((s, lw), lambda j: (0, j))
    ch = pl.BlockSpec((s, cw), lambda j: (0, j))
    bspec = pl.BlockSpec((None, cw, lw), lambda j: (j, 0, 0))
    cspec = pl.BlockSpec((None, lw, cw), lambda j: (j, 0, 0))
    cvec = pl.BlockSpec((1, cw), lambda j: (0, j))
    svec = pl.BlockSpec((1, lw), lambda j: (0, j))
    return pl.pallas_call(
        body, grid=(nt,),
        in_specs=[ch, ch, pl.BlockSpec((s, cw), lambda j: (0, u_block0 + j)), st, st, bspec, bspec, cspec, cspec, cvec, svec, svec],
        out_specs=[ch, cvec, cspec, cspec, bspec, bspec, svec, svec],
        out_shape=[_sds((s, nt * cw), _MXU), _sds((1, nt * cw), F32), _sds((nt, lw, cw), F32), _sds((nt, lw, cw), F32),
                   _sds((nt, cw, lw), F32), _sds((nt, cw, lw), F32), _sds((1, nt * lw), F32), _sds((1, nt * lw), F32)],
        scratch_shapes=[pltpu.VMEM((s, lw), F32), pltpu.VMEM((s, lw), F32), pltpu.VMEM((s, cw), F32)],
        compiler_params=_cp(), name=name,
    )(dyg, ypre, h, sr, si, bdr, bdi, cdr, cdi, dvec, ar, ai)


def _glu_fwd(name, yg, gw):
    s, w = yg.shape
    tr = min(512, s)

    def body(y_ref, w_ref, o_ref, ab_ref):
        ab = jnp.dot(y_ref[...], w_ref[...], preferred_element_type=F32)
        ab_ref[...] = ab
        o_ref[...] = (ab[:, :w] * jax.nn.sigmoid(ab[:, w:])).astype(o_ref.dtype)

    return pl.pallas_call(
        body, grid=(s // tr,), in_specs=[pl.BlockSpec((tr, w), lambda i: (i, 0)), _resident((w, 2 * w), lambda i: (0, 0))],
        out_specs=[pl.BlockSpec((tr, w), lambda i: (i, 0)), pl.BlockSpec((tr, 2 * w), lambda i: (i, 0))],
        out_shape=[_sds((s, w), _MXU), _sds((s, 2 * w), F32)], compiler_params=_cp(), name=name,
    )(yg, gw)


def _glu_bwd(name, dmix, col_block, ab, gw):
    s = ab.shape[0]
    w = ab.shape[1] // 2
    tr = min(512, s)

    def body(do_ref, ab_ref, w_ref, dab_ref, dy_ref):
        do = do_ref[...]
        a, b = ab_ref[:, :w], ab_ref[:, w:]
        sg = jax.nn.sigmoid(b)
        da = (do * sg).astype(_MXU)
        db = (do * a * sg * (1.0 - sg)).astype(_MXU)
        dab_ref[:, :w] = da
        dab_ref[:, w:] = db
        dy_ref[...] = (lax.dot_general(da, w_ref[:, :w], (NT, ((), ())), preferred_element_type=F32)
                       + lax.dot_general(db, w_ref[:, w:], (NT, ((), ())), preferred_element_type=F32))

    return pl.pallas_call(
        body, grid=(s // tr,),
        in_specs=[pl.BlockSpec((tr, w), lambda i: (i, col_block)), pl.BlockSpec((tr, 2 * w), lambda i: (i, 0)),
                  _resident((w, 2 * w), lambda i: (0, 0))],
        out_specs=[pl.BlockSpec((tr, 2 * w), lambda i: (i, 0)), pl.BlockSpec((tr, w), lambda i: (i, 0))],
        out_shape=[_sds((s, 2 * w), _MXU), _sds((s, w), F32)], compiler_params=_cp(), name=name,
    )(dmix, ab, gw)


CONV_ROWS = 64


def _conv_chunk(ref, w_ref, b_ref, c, tt):
    r0 = pl.multiple_of(c * tt, tt)
    before = ref[pl.ds(pl.multiple_of(jnp.maximum(r0 - 8, 0), 8), 8), :]
    before = jnp.where(c > 0, before, 0.0)
    main = ref[pl.ds(r0, tt), :]
    ext = jnp.concatenate([before, main], 0)
    d1 = pltpu.roll(ext, 1, 0)[8:, :]
    d2 = pltpu.roll(ext, 2, 0)[8:, :]
    hc = b_ref[...] + d2 * w_ref[0:1, :]
    hc = hc + d1 * w_ref[1:2, :]
    return hc + main * w_ref[2:3, :], main, d1, d2


def _conv_act_fwd(name, hu, cw, cb):
    s, f2 = hu.shape
    f = f2 // 2
    tw = _col_tile(f, 256)
    nt = f // tw

    tt = min(CONV_ROWS, s)

    def body(v_ref, g_ref, wv_ref, wg_ref, bv_ref, bg_ref, act_ref):
        def chunk(c, _):
            val = _conv_chunk(v_ref, wv_ref, bv_ref, c, tt)[0]
            gate = _conv_chunk(g_ref, wg_ref, bg_ref, c, tt)[0]
            act_ref[pl.ds(pl.multiple_of(c * tt, tt), tt), :] = (jax.nn.silu(gate) * val).astype(act_ref.dtype)
            return 0

        lax.fori_loop(0, s // tt, chunk, 0)

    cv = lambda rows: pl.BlockSpec((rows, tw), lambda i: (0, i))
    cg = lambda rows: pl.BlockSpec((rows, tw), lambda i: (0, nt + i))
    return pl.pallas_call(
        body, grid=(nt,), in_specs=[cv(s), cg(s), cv(CONV_WIDTH), cg(CONV_WIDTH), cv(1), cg(1)],
        out_specs=cv(s), out_shape=_sds((s, f), _MXU), compiler_params=_cp(), name=name,
    )(hu, hu, cw, cw, cb, cb)


def _conv_act_bwd(name, dact, hu, cw, cb):
    s, f2 = hu.shape
    f = f2 // 2
    tw = _col_tile(f, 256)
    nt = f // tw

    tt = min(CONV_ROWS, s)
    nchunk = s // tt

    def body(da_ref, v_ref, g_ref, wv_ref, wg_ref, bv_ref, bg_ref, dh_ref, dwv_ref, dwg_ref, dbv_ref, dbg_ref):
        def chunk(i, carry):
            c = nchunk - 1 - i
            rows = pl.ds(pl.multiple_of(c * tt, tt), tt)
            val, hv, hv1, hv2 = _conv_chunk(v_ref, wv_ref, bv_ref, c, tt)
            gate, hg, hg1, hg2 = _conv_chunk(g_ref, wg_ref, bg_ref, c, tt)
            sg = jax.nn.sigmoid(gate)
            da = da_ref[rows, :]
            dval = da * (gate * sg)
            dgate = da * val * sg * (1.0 + gate * (1.0 - sg))
            out = []
            for part, dhc, taps, w_ref, (after, acc) in ((0, dval, (hv2, hv1, hv), wv_ref, carry[0]),
                                                        (1, dgate, (hg2, hg1, hg), wg_ref, carry[1])):
                ext = jnp.concatenate([dhc, after], 0)
                u1 = pltpu.roll(ext, tt + 8 - 1, 0)[:tt, :]
                u2 = pltpu.roll(ext, tt + 8 - 2, 0)[:tt, :]
                dh = dhc * w_ref[2:3, :] + u1 * w_ref[1:2, :] + u2 * w_ref[0:1, :]
                dh_ref[part, rows, :] = dh.astype(dh_ref.dtype)
                sums = [jnp.sum(dhc * tap, 0, keepdims=True) for tap in taps] + [jnp.sum(dhc, 0, keepdims=True)]
                out.append((dhc[0:8, :], tuple(a + b for a, b in zip(acc, sums))))
            return tuple(out)

        zero = (jnp.zeros((8, tw), F32), tuple(jnp.zeros((1, tw), F32) for _ in range(CONV_WIDTH + 1)))
        (_, acc_v), (_, acc_g) = lax.fori_loop(0, nchunk, chunk, (zero, zero))
        for acc, dw_ref, db_ref in ((acc_v, dwv_ref, dbv_ref), (acc_g, dwg_ref, dbg_ref)):
            for tap in range(CONV_WIDTH):
                dw_ref[tap:tap + 1, :] = acc[tap]
            db_ref[...] = acc[CONV_WIDTH]

    cv = lambda rows: pl.BlockSpec((rows, tw), lambda i: (0, i))
    cg = lambda rows: pl.BlockSpec((rows, tw), lambda i: (0, nt + i))
    both = pl.BlockSpec((2, s, tw), lambda i: (0, 0, i))
    dh, dwv, dwg, dbv, dbg = pl.pallas_call(
        body, grid=(nt,), in_specs=[cv(s), cv(s), cg(s), cv(CONV_WIDTH), cg(CONV_WIDTH), cv(1), cg(1)],
        out_specs=[both, cv(CONV_WIDTH), cv(CONV_WIDTH), cv(1), cv(1)],
        out_shape=[_sds((2, s, f), _MXU), _sds((CONV_WIDTH, f), F32), _sds((CONV_WIDTH, f), F32),
                   _sds((1, f), F32), _sds((1, f), F32)],
        compiler_params=_cp(), name=name,
    )(dact, hu, hu, cw, cw, cb, cb)
    return dh, jnp.concatenate([dwv, dwg], 1), jnp.concatenate([dbv, dbg], 1)


ELEM_BLOCK = 512 * 1024


def _elem_tiles(r, c, budget=ELEM_BLOCK):
    rows = [t for t in range(8, r + 1, 8) if r % t == 0] or [r]
    cols = [t for t in range(V7X_LANES, c + 1, V7X_LANES) if c % t == 0] or [c]
    fits = [(tr * tc, tc, tr) for tr in rows for tc in cols if tr * tc <= budget]
    if not fits:
        return min(rows), min(cols)
    _, tc, tr = max(fits)
    return tr, tc


def _sum_parts(name, parts):
    n, r, c = parts.shape
    tr, tc = _elem_tiles(r, c, ELEM_BLOCK // n)

    def body(p_ref, g_ref):
        g = p_ref[0]
        for i in range(1, n):
            g = g + p_ref[i]
        g_ref[...] = g

    return pl.pallas_call(
        body, grid=(r // tr, c // tc), in_specs=[pl.BlockSpec((n, tr, tc), lambda i, j: (0, i, j))],
        out_specs=pl.BlockSpec((tr, tc), lambda i, j: (i, j)), out_shape=_sds((r, c), F32), compiler_params=_cp(), name=name,
    )(parts)


def _adamw_2d(name, g, w, m, v):
    r, c = w.shape
    tc = c if c % V7X_LANES else _col_tile(c, 2048)
    rows = [t for t in range(8, r + 1, 8) if r % t == 0 and t * max(tc, V7X_LANES) <= ELEM_BLOCK // 4] or [r]
    tr = max(rows)
    c1 = 1.0 - ADAM_B1 ** ADAM_STEP
    c2 = 1.0 - ADAM_B2 ** ADAM_STEP

    def body(g_ref, w_ref, m_ref, v_ref, d_ref, nm_ref, nv_ref):
        g = g_ref[...]
        nm = ADAM_B1 * m_ref[...] + (1.0 - ADAM_B1) * g
        nv = ADAM_B2 * v_ref[...] + (1.0 - ADAM_B2) * (g * g)
        m_hat = nm / c1
        v_hat = nv / c2
        nm_ref[...] = nm
        nv_ref[...] = nv
        d_ref[...] = -ADAM_LR * (m_hat / (jnp.sqrt(v_hat) + ADAM_EPS) + ADAM_WD * w_ref[...])

    blk = pl.BlockSpec((tr, tc), lambda i, j: (i, j))
    out = _sds((r, c), F32)
    return pl.pallas_call(
        body, grid=(r // tr, c // tc), in_specs=[blk] * 4, out_specs=[blk] * 3, out_shape=[out] * 3,
        compiler_params=_cp(), name=name,
    )(g, w, m, v)


def _pair_sum(name, mine, theirs, c_idx):
    _, _, r, c = mine.shape
    tr, tc = _elem_tiles(r, c)

    def body(c_ref, a_ref, b_ref, o_ref):
        o_ref[...] = (a_ref[...].astype(F32) + b_ref[...].astype(F32)).astype(o_ref.dtype)

    return pl.pallas_call(
        body,
        grid_spec=pltpu.PrefetchScalarGridSpec(
            num_scalar_prefetch=1, grid=(4, r // tr, c // tc),
            in_specs=[pl.BlockSpec((None, None, tr, tc), lambda p, i, j, cref: (p, cref[0], i, j)),
                      pl.BlockSpec((None, tr, tc), lambda p, i, j, cref: (p, i, j))],
            out_specs=pl.BlockSpec((None, tr, tc), lambda p, i, j, cref: (p, i, j))),
        out_shape=_sds((4, r, c), _WIRE), compiler_params=_cp(), name=name,
    )(c_idx, mine, theirs)


def _place():
    return lax.axis_index("x"), lax.axis_index("y"), lax.axis_index("c")


def _all_gather(name, xs):
    n = len(xs)

    def body(*refs):
        x_refs, o_refs = refs[:n], refs[n:2 * n]
        send_sems, recv_sems, local_sems = refs[2 * n:]
        x, y, c = _place()
        me, sibling = (x, y, c), (x, y, 1 - c)
        chips = [(1 - x, y), (x, 1 - y), (1 - x, 1 - y)]

        def copy(a, k, block, to, src=None):
            px, py, pc = block
            rows = o_refs[a].at[4 * px + 2 * py + pc]
            return pltpu.make_async_remote_copy(
                src_ref=rows if src is None else src, dst_ref=rows, send_sem=send_sems.at[a, k], recv_sem=recv_sems.at[a, k],
                device_id=to, device_id_type=MESH)

        sent = []
        mine = []
        for a in range(n):
            mx, my, mc = me
            cp = pltpu.make_async_copy(x_refs[a], o_refs[a].at[4 * mx + 2 * my + mc], local_sems.at[a])
            cp.start()
            mine.append(cp)
            first = [copy(a, 0, me, sibling, src=x_refs[a])]
            first += [copy(a, 1 + j, me, (*chip, c), src=x_refs[a]) for j, chip in enumerate(chips)]
            for cp in first:
                cp.start()
            sent += first
        for a in range(n):
            for j, chip in enumerate(chips):
                copy(a, 1 + j, (*chip, c), me).wait_recv()
                fwd = copy(a, 4 + j, (*chip, c), sibling)
                fwd.start()
                sent.append(fwd)
        for a in range(n):
            copy(a, 0, sibling, me).wait_recv()
            for j, chip in enumerate(chips):
                copy(a, 4 + j, (*chip, 1 - c), me).wait_recv()
        for cp in sent:
            cp.wait_send()
        for cp in mine:
            cp.wait()

    return pl.pallas_call(
        body, in_specs=[ANY] * n, out_specs=[ANY] * n,
        out_shape=[_sds((N_DEV,) + a.shape, a.dtype) for a in xs],
        scratch_shapes=[pltpu.SemaphoreType.DMA((n, 7)), pltpu.SemaphoreType.DMA((n, 7)), pltpu.SemaphoreType.DMA((n,))],
        name=name,
    )(*xs)


HBM = pl.BlockSpec(memory_space=pltpu.HBM)
SEM = pl.BlockSpec(memory_space=pltpu.SEMAPHORE)
DATAFLOW = pltpu.SideEffectType.DATAFLOW_SIDE_EFFECTING


def _in_hbm(a):
    return pltpu.with_memory_space_constraint(a, pltpu.HBM)


def _split_copy_start(name, srcs, lands, copies, deps):
    n, nd = len(srcs), len(deps)
    per = len(copies([None] * n, [None] * n, probe=True)) // n

    def body(*refs):
        s_refs, l_refs = refs[:n], refs[n:2 * n]
        send_sems, recv_sems = refs[2 * n + nd], refs[2 * n + nd + 1]
        token = refs[-1]
        for a, k, src, dst, to in copies(s_refs, l_refs):
            pltpu.make_async_remote_copy(src_ref=src, dst_ref=dst, send_sem=send_sems.at[a * per + k],
                                         recv_sem=recv_sems.at[a * per + k], device_id=to, device_id_type=MESH).start()
        token[...] = jnp.zeros_like(token)

    both = list(srcs) + list(lands)
    outs = pl.pallas_call(
        body, name=name,
        out_shape=(pltpu.SemaphoreType.DMA((n * per,)), pltpu.SemaphoreType.DMA((n * per,)),
                   *[pltpu.HBM(a.shape, a.dtype) for a in both], _sds((8, V7X_LANES), F32)),
        in_specs=[HBM] * (2 * n) + [ANY] * nd,
        out_specs=(SEM, SEM, *[HBM] * (2 * n), pl.BlockSpec(memory_space=pltpu.VMEM)),
        input_output_aliases={i: 2 + i for i in range(2 * n)},
        compiler_params=pltpu.CompilerParams(has_side_effects=DATAFLOW),
    )(*[_in_hbm(a) for a in both], *deps)
    return outs[0], outs[1], list(outs[2:2 + n]), list(outs[2 + n:2 + 2 * n]), outs[-1]


def _split_copy_wait(name, send_sems, recv_sems, srcs, lands, arrivals, after):
    n = len(srcs)
    per = len(arrivals([None] * n, [None] * n, probe=True)) // n

    def body(*refs):
        s_refs, l_refs = refs[:n], refs[n:2 * n]
        send_sems_, recv_sems_ = refs[2 * n], refs[2 * n + 1]
        for a, k, src, dst, frm in arrivals(s_refs, l_refs):
            cp = pltpu.make_async_remote_copy(src_ref=src, dst_ref=dst, send_sem=send_sems_.at[a * per + k],
                                              recv_sem=recv_sems_.at[a * per + k], device_id=frm, device_id_type=MESH)
            cp.wait_send()
            cp.wait_recv()

    both = list(srcs) + list(lands)
    outs = pl.pallas_call(
        body, name=name, out_shape=tuple(pltpu.HBM(a.shape, a.dtype) for a in both),
        in_specs=[HBM] * (2 * n) + [SEM, SEM, ANY], out_specs=tuple([HBM] * (2 * n)),
        input_output_aliases={i: i for i in range(2 * n)},
        compiler_params=pltpu.CompilerParams(has_side_effects=DATAFLOW),
    )(*both, send_sems, recv_sems, after)
    return list(outs[:n]), list(outs[n:])


def _gather_copies(arriving):
    def copies(s_refs, l_refs, probe=False):
        if probe:
            return [None] * (4 * len(s_refs))
        x, y, c = _place()
        out = []
        for a in range(len(s_refs)):
            for k, (px, py, pc) in enumerate([(x, y, 1 - c), (1 - x, y, c), (x, 1 - y, c), (1 - x, 1 - y, c)]):
                slot = 4 * px + 2 * py + pc if arriving else 4 * x + 2 * y + c
                out.append((a, k, s_refs[a], l_refs[a].at[slot], (px, py, pc)))
        return out
    return copies


def _chip_copies(arriving):
    def copies(s_refs, l_refs, probe=False):
        if probe:
            return [None] * (3 * len(s_refs))
        x, y, c = _place()
        out = []
        for a in range(len(s_refs)):
            for j, (px, py) in enumerate([(1 - x, y), (x, 1 - y), (1 - x, 1 - y)]):
                src = s_refs[a].at[2 * x + y] if arriving else s_refs[a].at[2 * px + py]
                out.append((a, j, src, l_refs[a].at[j], (px, py, c)))
        return out
    return copies


def _sibling_copies(s_refs, l_refs, probe=False):
    if probe:
        return [None] * (4 * len(s_refs))
    x, y, c = _place()
    return [(a, p, s_refs[a].at[p, 1 - c], l_refs[a].at[p], (x, y, 1 - c)) for a in range(len(s_refs)) for p in range(4)]


def _sibling_begin(name, by_owner, deps):
    lands = [lax.empty((4,) + a.shape[2:], a.dtype) for a in by_owner]
    return _split_copy_start(name + "_start", by_owner, lands, _sibling_copies, deps)


def _sibling_end(name, handle, after):
    send_sems, recv_sems, srcs, lands, _ = handle
    return _split_copy_wait(name + "_wait", send_sems, recv_sems, srcs, lands, _sibling_copies, after)


def _gather_begin(name, shards, deps):
    x, y, c = _place()
    lands = [lax.dynamic_update_slice_in_dim(lax.empty((N_DEV,) + a.shape, a.dtype), a[None], 4 * x + 2 * y + c, 0)
             for a in shards]
    return _split_copy_start(name + "_start", shards, lands, _gather_copies(False), deps)


def _gather_end(name, handle, after):
    send_sems, recv_sems, srcs, lands, _ = handle
    _, lands = _split_copy_wait(name + "_wait", send_sems, recv_sems, srcs, lands, _gather_copies(True), after)
    return _gather_forward(name + "_forward", lands)


def _forward_copies(arriving):
    def copies(s_refs, l_refs, probe=False):
        if probe:
            return [None] * (3 * len(l_refs))
        x, y, c = _place()
        out = []
        for a in range(len(l_refs)):
            for j, (px, py) in enumerate([(1 - x, y), (x, 1 - y), (1 - x, 1 - y)]):
                mine, theirs = l_refs[a].at[4 * px + 2 * py + c], l_refs[a].at[4 * px + 2 * py + 1 - c]
                out.append((a, j, mine, theirs if arriving else mine, (x, y, 1 - c)))
        return out
    return copies


def _gather_arrived(name, handle, after):
    send_sems, recv_sems, srcs, lands, _ = handle
    srcs, lands = _split_copy_wait(name + "_wait", send_sems, recv_sems, srcs, lands, _gather_copies(True), after)
    return _split_copy_start(name + "_forward_start", srcs, lands, _forward_copies(False), [])


def _gather_done(name, handle, after):
    send_sems, recv_sems, srcs, lands, _ = handle
    return _split_copy_wait(name + "_forward_wait", send_sems, recv_sems, srcs, lands, _forward_copies(True), after)[1]


def _gather_forward(name, lands):
    n = len(lands)

    def body(*refs):
        o_refs = refs[n:2 * n]
        send_sems, recv_sems = refs[2 * n:]
        x, y, c = _place()
        sibling = (x, y, 1 - c)
        chips = [(1 - x, y), (x, 1 - y), (1 - x, 1 - y)]
        sent = []
        for a in range(n):
            for j, (px, py) in enumerate(chips):
                rows = o_refs[a].at[4 * px + 2 * py + c]
                cp = pltpu.make_async_remote_copy(src_ref=rows, dst_ref=rows, send_sem=send_sems.at[a, j],
                                                  recv_sem=recv_sems.at[a, j], device_id=sibling, device_id_type=MESH)
                cp.start()
                sent.append(cp)
        for a in range(n):
            for j, (px, py) in enumerate(chips):
                rows = o_refs[a].at[4 * px + 2 * py + 1 - c]
                pltpu.make_async_remote_copy(src_ref=rows, dst_ref=rows, send_sem=send_sems.at[a, j],
                                             recv_sem=recv_sems.at[a, j], device_id=sibling, device_id_type=MESH).wait_recv()
        for cp in sent:
            cp.wait_send()

    return pl.pallas_call(
        body, in_specs=[ANY] * n, out_specs=[ANY] * n, out_shape=[_sds(a.shape, a.dtype) for a in lands],
        input_output_aliases={i: i for i in range(n)},
        scratch_shapes=[pltpu.SemaphoreType.DMA((n, 3)), pltpu.SemaphoreType.DMA((n, 3))], name=name,
    )(*lands)


def _chips_begin(name, pairs, deps):
    lands = [lax.empty((3,) + a.shape[1:], a.dtype) for a in pairs]
    return _split_copy_start(name + "_start", pairs, lands, _chip_copies(False), deps)


def _chips_end(name, handle, after):
    send_sems, recv_sems, srcs, lands, _ = handle
    return _split_copy_wait(name + "_wait", send_sems, recv_sems, srcs, lands, _chip_copies(True), after)


def _adamw_layer(name, l, own, lands, w, m, v, prev):
    nl, ng, r, c = w.shape
    tr, tc = _elem_tiles(r, c)
    c1 = 1.0 - ADAM_B1 ** ADAM_STEP
    c2 = 1.0 - ADAM_B2 ** ADAM_STEP

    def body(own_ref, lands_ref, w_ref, m_ref, v_ref, *rest):
        g_ref, d_ref, nm_ref, nv_ref = rest[-4:]
        g = own_ref[...].astype(F32) + lands_ref[0].astype(F32) + lands_ref[1].astype(F32) + lands_ref[2].astype(F32)
        nm = ADAM_B1 * m_ref[...] + (1.0 - ADAM_B1) * g
        nv = ADAM_B2 * v_ref[...] + (1.0 - ADAM_B2) * (g * g)
        m_hat = nm / c1
        v_hat = nv / c2
        g_ref[...] = g
        nm_ref[...] = nm
        nv_ref[...] = nv
        d_ref[...] = -ADAM_LR * (m_hat / (jnp.sqrt(v_hat) + ADAM_EPS) + ADAM_WD * w_ref[...])

    lay = pl.BlockSpec((None, None, tr, tc), lambda g, i, j: (l, g, i, j))
    out = _sds((nl, ng, r, c), F32)
    prev = [] if prev is None else list(prev)
    return pl.pallas_call(
        body, grid=(ng, r // tr, c // tc),
        in_specs=[pl.BlockSpec((None, tr, tc), lambda g, i, j: (g, i, j)),
                  pl.BlockSpec((3, None, tr, tc), lambda g, i, j: (0, g, i, j)), lay, lay, lay] + [ANY] * len(prev),
        out_specs=[lay] * 4, out_shape=[out] * 4, input_output_aliases={5 + i: i for i in range(len(prev))},
        compiler_params=_cp(), name=name,
    )(own, lands, w, m, v, *prev)


def _pad_pairs(a, axis, half, half_pad):
    shp = a.shape
    a = a.reshape(shp[:axis] + (2, half) + shp[axis + 1:])
    pad = [(0, 0)] * a.ndim
    pad[axis + 1] = (0, half_pad - half)
    a = jnp.pad(a, pad)
    return a.reshape(shp[:axis] + (2 * half_pad,) + shp[axis + 1:])


def _unpad_pairs(a, axis, half, half_pad):
    shp = a.shape
    a = a.reshape(shp[:axis] + (2, half_pad) + shp[axis + 1:])
    a = lax.slice_in_dim(a, 0, half, axis=axis + 1)
    return a.reshape(shp[:axis] + (2 * half,) + shp[axis + 1:])


def _blockdiag(w, nt):
    g, a, b = w.shape
    gl = g // nt
    e = jnp.eye(gl, dtype=w.dtype).reshape(1, gl, 1, gl, 1)
    return (w.reshape(nt, gl, a, 1, b) * e).reshape(nt, gl * a, gl * b)


def _diagblocks(m, g, a, b):
    nt = m.shape[0]
    gl = g // nt
    d = jnp.diagonal(m.reshape(nt, gl, a, gl, b), axis1=1, axis2=3)
    return jnp.moveaxis(d, -1, 1).reshape(g, a, b)


_PIECE = 8 * V7X_LANES


def _pack(pieces, row_multiple=512):
    rows = []
    for p in pieces:
        flat = p.reshape(-1).astype(F32)
        rows.append(jnp.pad(flat, (0, -flat.shape[0] % _PIECE)).reshape(-1, V7X_LANES))
    fill = -sum(r.shape[0] for r in rows) % row_multiple
    if fill:
        rows.append(jnp.zeros((fill, V7X_LANES), F32))
    return jnp.concatenate(rows, 0)


def _unpack(packed, shapes):
    lead = packed.shape[:-2]
    out, off = [], 0
    for shp in shapes:
        n = math.prod(shp)
        r = -(-n // _PIECE) * 8
        piece = packed[..., off:off + r, :].reshape(lead + (r * V7X_LANES,))[..., :n]
        out.append(piece.reshape(lead + tuple(shp)))
        off += r
    return out


def _ssm_discretise(lam_re, lam_im, log_dt, b_re, b_im):
    dt = jnp.exp(log_dt)[..., None]
    mag = jnp.exp(lam_re * dt)
    ab_re, ab_im = mag * jnp.cos(lam_im * dt), mag * jnp.sin(lam_im * dt)
    nr, ni = ab_re - 1.0, ab_im
    den = lam_re * lam_re + lam_im * lam_im
    zr = (nr * lam_re + ni * lam_im) / den
    zi = (ni * lam_re - nr * lam_im) / den
    bbr = zr[..., None] * b_re - zi[..., None] * b_im
    bbi = zr[..., None] * b_im + zi[..., None] * b_re
    return ab_re, ab_im, bbr, bbi


def _rope_tables(s):
    half = HEAD_DIM // 2
    inv = ROPE_THETA ** (-jnp.arange(half, dtype=F32) / half)
    ang = jnp.arange(s).astype(F32)[:, None] * inv[None, :]
    cos, sin = jnp.cos(ang), jnp.sin(ang)
    reps = V7X_LANES // HEAD_DIM
    return jnp.tile(jnp.concatenate([cos, cos], -1), (1, reps)), jnp.tile(jnp.concatenate([-sin, sin], -1), (1, reps))


_SMALL = ("attn_sinks", "pool_w", "pool_scale", "ssm_lam_re", "ssm_lam_im", "ssm_log_dt", "ssm_b_re", "ssm_b_im",
          "ssm_c_re", "ssm_c_im", "ssm_d", "ln1_g", "ln1_b", "ffn_conv_b", "ln2_g", "ln2_b")
_BIG = ("w_in", "ssm_glu_w", "w_out", "ffn_w_up", "ffn_w_down")
_RAW = ("attn_sinks", "pool_w", "pool_scale", "ssm_lam_re", "ssm_lam_im", "ssm_b_re", "ssm_b_im", "ssm_c_re", "ssm_c_im",
        "ssm_d", "ln1_g", "ln1_b", "ffn_conv_b", "ln2_g", "ln2_b", "ffn_conv_w")
_ORDER = ("w_in", "attn_sinks", "pool_w", "pool_scale", "ssm_lam_re", "ssm_lam_im", "ssm_log_dt", "ssm_b_re", "ssm_b_im",
          "ssm_c_re", "ssm_c_im", "ssm_d", "ssm_glu_w", "w_out", "ln1_g", "ln1_b", "ffn_w_up", "ffn_conv_w", "ffn_conv_b",
          "ffn_w_down", "ln2_g", "ln2_b")


def kernel(x, w_in, attn_sinks, pool_w, pool_scale, ssm_lam_re, ssm_lam_im, ssm_log_dt, ssm_b_re, ssm_b_im, ssm_c_re, ssm_c_im, ssm_d, ssm_glu_w, w_out, ln1_g, ln1_b, ffn_w_up, ffn_conv_w, ffn_conv_b, ffn_w_down, ln2_g, ln2_b, loss_target, m_w_in, m_attn_sinks, m_pool_w, m_pool_scale, m_ssm_lam_re, m_ssm_lam_im, m_ssm_log_dt, m_ssm_b_re, m_ssm_b_im, m_ssm_c_re, m_ssm_c_im, m_ssm_d, m_ssm_glu_w, m_w_out, m_ln1_g, m_ln1_b, m_ffn_w_up, m_ffn_conv_w, m_ffn_conv_b, m_ffn_w_down, m_ln2_g, m_ln2_b, v_w_in, v_attn_sinks, v_pool_w, v_pool_scale, v_ssm_lam_re, v_ssm_lam_im, v_ssm_log_dt, v_ssm_b_re, v_ssm_b_im, v_ssm_c_re, v_ssm_c_im, v_ssm_d, v_ssm_glu_w, v_w_out, v_ln1_g, v_ln1_b, v_ffn_w_up, v_ffn_conv_w, v_ffn_conv_b, v_ffn_w_down, v_ln2_g, v_ln2_b):
    W = dict(w_in=w_in, attn_sinks=attn_sinks, pool_w=pool_w, pool_scale=pool_scale, ssm_lam_re=ssm_lam_re, ssm_lam_im=ssm_lam_im, ssm_log_dt=ssm_log_dt, ssm_b_re=ssm_b_re, ssm_b_im=ssm_b_im, ssm_c_re=ssm_c_re, ssm_c_im=ssm_c_im, ssm_d=ssm_d, ssm_glu_w=ssm_glu_w, w_out=w_out, ln1_g=ln1_g, ln1_b=ln1_b, ffn_w_up=ffn_w_up, ffn_conv_w=ffn_conv_w, ffn_conv_b=ffn_conv_b, ffn_w_down=ffn_w_down, ln2_g=ln2_g, ln2_b=ln2_b)
    M = dict(w_in=m_w_in, attn_sinks=m_attn_sinks, pool_w=m_pool_w, pool_scale=m_pool_scale, ssm_lam_re=m_ssm_lam_re, ssm_lam_im=m_ssm_lam_im, ssm_log_dt=m_ssm_log_dt, ssm_b_re=m_ssm_b_re, ssm_b_im=m_ssm_b_im, ssm_c_re=m_ssm_c_re, ssm_c_im=m_ssm_c_im, ssm_d=m_ssm_d, ssm_glu_w=m_ssm_glu_w, w_out=m_w_out, ln1_g=m_ln1_g, ln1_b=m_ln1_b, ffn_w_up=m_ffn_w_up, ffn_conv_w=m_ffn_conv_w, ffn_conv_b=m_ffn_conv_b, ffn_w_down=m_ffn_w_down, ln2_g=m_ln2_g, ln2_b=m_ln2_b)
    V = dict(w_in=v_w_in, attn_sinks=v_attn_sinks, pool_w=v_pool_w, pool_scale=v_pool_scale, ssm_lam_re=v_ssm_lam_re, ssm_lam_im=v_ssm_lam_im, ssm_log_dt=v_ssm_log_dt, ssm_b_re=v_ssm_b_re, ssm_b_im=v_ssm_b_im, ssm_c_re=v_ssm_c_re, ssm_c_im=v_ssm_c_im, ssm_d=v_ssm_d, ssm_glu_w=v_ssm_glu_w, w_out=v_w_out, ln1_g=v_ln1_g, ln1_b=v_ln1_b, ffn_w_up=v_ffn_w_up, ffn_conv_w=v_ffn_conv_w, ffn_conv_b=v_ffn_conv_b, ffn_w_down=v_ffn_w_down, ln2_g=v_ln2_g, ln2_b=v_ln2_b)

    depth = w_in.shape[0]
    s, d = x.shape[1], x.shape[2]
    alpha = (2 * depth) ** 0.25
    attn_w = d // 2
    kv_w = attn_w // GQA
    nkv = kv_w // HEAD_DIM
    pool_wd = d // 4
    ssm_wd = d // 4
    n_groups = ssm_wd // SSM_GROUP
    state_w = n_groups * SSM_STATE
    nt_ssm = max(1, state_w // 512)
    o_k, o_v, o_p, o_s = attn_w, attn_w + kv_w, attn_w + 2 * kv_w, attn_w + 2 * kv_w + pool_wd
    in_w = o_s + ssm_wd
    half = ffn_w_down.shape[1]
    half_pad = -(-half // 64) * 64
    ffp = 4 * 2 * half_pad
    xi, yi, ci = _place()
    me = 4 * xi + 2 * yi + ci
    c_idx = jnp.reshape(ci, (1,)).astype(jnp.int32)

    cos_t, sin_t = _rope_tables(s)

    def layer_shards(l):
        return [
            jnp.transpose(w_in[l]).astype(_WIRE), ssm_glu_w[l].astype(_WIRE), w_out[l].astype(_WIRE),
            _pad_pairs(jnp.transpose(ffn_w_up[l]).astype(_WIRE), 0, half, half_pad),
            jnp.pad(ffn_w_down[l].astype(_WIRE), ((0, half_pad - half), (0, 0))),
        ]

    (g_cw,) = _all_gather("gather_conv_w", [_pad_pairs(ffn_conv_w, 2, half, half_pad)])

    def mixer_weights(l, gathered):
        g_in, g_glu, g_out = gathered
        return dict(
            win_t=g_in.reshape(in_w, d),
            glu=jnp.transpose(g_glu, (1, 0, 2)).reshape(ssm_wd, 2 * ssm_wd),
            wout=g_out.reshape(d, d),
            cw=jnp.transpose(g_cw[:, l], (1, 0, 2)).reshape(CONV_WIDTH, 2 * ffp),
            cb=_pad_pairs(ffn_conv_b[l].reshape(N_DEV, 2 * half), 1, half, half_pad).reshape(1, 2 * ffp),
        )

    def ffn_weights(gathered):
        g_up, g_down = gathered
        return dict(wup_t=g_up.reshape(2 * ffp, d), wdown=g_down.reshape(ffp, d))

    shards = [layer_shards(l) for l in range(depth)]
    full = [None] * depth
    gathers = {}

    def begin_gather(l, part, deps):
        arrays = shards[l][:3] if part == "mixer" else shards[l][3:]
        gathers[l, part] = ("sent", _gather_begin(f"gather_{part}_weights_{l}", arrays, deps))
        return gathers[l, part][1][-1]

    def arrive_gather(l, part, after):
        gathers[l, part] = ("forwarding", _gather_arrived(f"gather_{part}_weights_{l}", gathers[l, part][1], after))
        return gathers[l, part][1][-1]

    def end_gather(l, part, after):
        stage, handle = gathers.pop((l, part))
        return (_gather_done if stage == "forwarding" else _gather_end)(f"gather_{part}_weights_{l}", handle, after)

    issued = begin_gather(0, "ffn", [begin_gather(0, "mixer", [g_cw])])
    full[0] = mixer_weights(0, end_gather(0, "mixer", g_cw))

    ssm_params = (ssm_lam_re, ssm_lam_im, ssm_log_dt, ssm_b_re, ssm_b_im)
    ab_re_all, ab_im_all, bbr_all, bbi_all = _ssm_discretise(*ssm_params)

    def ssm_maps(w):
        return jax.vmap(lambda t: _blockdiag(jnp.transpose(t, (0, 2, 1)), nt_ssm))(w).astype(_MXU)

    bdr_all, bdi_all, cdr_all, cdi_all = ssm_maps(bbr_all), ssm_maps(bbi_all), ssm_maps(ssm_c_re), ssm_maps(ssm_c_im)

    saved = []
    xf = x[0]
    xb = xf.astype(_MXU)
    for l in range(depth):
        fw = full[l]
        deps = [arrive_gather(l, "ffn", xb)] if l >= 2 else []
        if l + 1 < depth:
            issued = begin_gather(l + 1, "ffn", [begin_gather(l + 1, "mixer", [fw["win_t"], issued])])
            deps.append(issued)
        h = _mm_nt(f"in_proj_{l}", xb, fw["win_t"], deps=tuple(deps))
        q_rot, k_rot = _rope(f"rope_{l}", h, o_v, cos_t, sin_t, _MXU, ((0, o_k), (o_k, o_v)))
        k_hm = jnp.transpose(k_rot.reshape(s, nkv, HEAD_DIM), (1, 0, 2))
        v_hm = jnp.transpose(h[:, o_v:o_p].astype(_MXU).reshape(s, nkv, HEAD_DIM), (1, 0, 2))
        sinks = attn_sinks[l].reshape(nkv, GQA)
        o_attn, lse = _attn_fwd(f"attn_{l}", q_rot, k_hm, v_hm, sinks)
        pw_b = pool_w[l].astype(_MXU)
        psc = pool_scale[l].reshape(1, pool_wd)
        y_pool, pre = _pool_fwd(f"pool_{l}", h, o_p // pool_wd, pw_b, psc)
        bdr, bdi, cdr, cdi = bdr_all[l], bdi_all[l], cdr_all[l], cdi_all[l]
        dvec = ssm_d[l].reshape(1, ssm_wd)
        ar, ai = ab_re_all[l].reshape(1, state_w), ab_im_all[l].reshape(1, state_w)
        cw_ssm = ssm_wd // nt_ssm
        sr, si, ypre, yg = _ssm_fwd(f"ssm_{l}", h, o_s // cw_ssm, bdr, bdi, cdr, cdi, dvec, ar, ai)
        y_ssm, ab2 = _glu_fwd(f"glu_{l}", yg, fw["glu"])
        mix = jnp.concatenate([o_attn.astype(_MXU), y_pool, y_ssm], -1)
        a1 = _mm_nn(f"out_proj_{l}", mix, fw["wout"])
        g1, b1 = ln1_g[l].reshape(1, d), ln1_b[l].reshape(1, d)
        x1, x1b, xh1, rs1 = _ln_fwd(f"ln1_{l}", xf, a1, g1, b1, alpha)
        fw.update(ffn_weights(end_gather(l, "ffn", x1b)))
        deps = (arrive_gather(l + 1, "mixer", x1b),) if 1 <= l < depth - 1 else ()
        hu = _mm_nt(f"ffn_up_{l}", x1b, fw["wup_t"], cap=2 * half_pad, deps=deps)
        act = _conv_act_fwd(f"ffn_act_{l}", hu, fw["cw"], fw["cb"])
        tkd = 2 * half_pad
        tnd = _col_tile(d, 1024)
        f_out = _mm(f"ffn_down_{l}", act, fw["wdown"], NN, (d // tnd, ffp // tkd),
                    pl.BlockSpec((s, tkd), lambda j, kk: (0, kk)), pl.BlockSpec((tkd, tnd), lambda j, kk: (kk, j)),
                    pl.BlockSpec((s, tnd), lambda j, kk: (0, j)), (s, d), F32)
        g2, b2 = ln2_g[l].reshape(1, d), ln2_b[l].reshape(1, d)
        x2, x2b, xh2, rs2 = _ln_fwd(f"ln2_{l}", x1, f_out, g2, b2, alpha)
        saved.append(dict(xb=xb, h=h, q_rot=q_rot, k_hm=k_hm, v_hm=v_hm, sinks=sinks, o_attn=o_attn, lse=lse, pw_b=pw_b, psc=psc,
                          pre=pre, bdr=bdr, bdi=bdi, cdr=cdr, cdi=cdi, dvec=dvec, ar=ar, ai=ai, sr=sr, si=si, ypre=ypre, yg=yg,
                          ab2=ab2, mix=mix, g1=g1, xh1=xh1, rs1=rs1, x1b=x1b, hu=hu, act=act, g2=g2, xh2=xh2, rs2=rs2))
        xf, xb = x2, x2b
        if l + 1 < depth:
            full[l + 1] = mixer_weights(l + 1, end_gather(l + 1, "mixer", x2b))

    dy, loss_part = _loss_head("loss_head", xf, loss_target[0])
    loss = lax.psum(loss_part[0, 0], ("x", "y", "c"))

    small_handles = [None] * depth
    small_parts = [None] * depth
    outs = {}
    big_res = {k: None for k in _BIG}
    my_chip = 2 * xi + yi
    pending = []

    transposed = ("w_in", "ffn_w_up")

    def row_groups(name_, t):
        g = 2 if name_ == "ffn_w_up" else 1
        return t.reshape(t.shape[:-2] + (g, t.shape[-2] // g, t.shape[-1]))

    def as_groups(name_, t):
        return row_groups(name_, jnp.transpose(t, (0, 2, 1)) if name_ in transposed else t)

    def from_groups(name_, t):
        t = t.reshape(t.shape[0], t.shape[1] * t.shape[2], t.shape[3])
        return jnp.transpose(t, (0, 2, 1)) if name_ in transposed else t

    grouped = {name_: tuple(as_groups(name_, t[name_]) for t in (W, M, V)) for name_ in _BIG}

    def finish_exchanges(after):
        while pending:
            lay, part, names, handle = pending.pop(0)
            pairs, lands = _chips_end(f"grads_between_chips_{part}_{lay}", handle, after)
            for name_, p, ld in zip(names, pairs, lands):
                own = row_groups(name_, lax.dynamic_index_in_dim(p, my_chip, 0, keepdims=False))
                big_res[name_] = _adamw_layer(f"adamw_{name_}_{lay}", lay, own, row_groups(name_, ld), *grouped[name_],
                                              big_res[name_])

    def begin_swap(lay, part, names, grads):
        by_owner = [a.reshape((4, 2) + a.shape[1:]) for a in grads]
        return lay, part, names, _sibling_begin(f"grads_to_sibling_{part}_{lay}", by_owner, [])

    def begin_exchange(swap, after):
        lay, part, names, handle = swap
        by_owner, theirs = _sibling_end(f"grads_to_sibling_{part}_{lay}", handle, after)
        pair = [_pair_sum(f"pair_sum_{name_}_{lay}", a, b, c_idx) for name_, a, b in zip(names, by_owner, theirs)]
        finish_exchanges(after)
        handle = _chips_begin(f"grads_between_chips_{part}_{lay}", pair, [])
        pending.append((lay, part, names, handle))
        return handle[-1]

    token = None
    small_handle = None
    for l in reversed(range(depth)):
        fw, sv = full[l], saved[l]
        deps = () if token is None else (token, small_handles[l + 1][-1])
        dr2, dr2b, dg2, db2 = _ln_bwd(f"ln2_bwd_{l}", dy, sv["xh2"], sv["rs2"], sv["g2"], deps=deps)
        d_wdown = _mm_tn_acols(f"ffn_down_dw_{l}", sv["act"], dr2b, _WIRE, cap=2 * half_pad)
        dact = _mm_nt(f"ffn_down_dx_{l}", dr2b, fw["wdown"], cap=2 * half_pad)
        dhu, dcw, dcb = _conv_act_bwd(f"ffn_act_bwd_{l}", dact, sv["hu"], fw["cw"], fw["cb"])
        d_wup = _mm(f"ffn_up_dw_{l}", dhu, sv["x1b"], TN, (N_DEV, 1),
                    pl.BlockSpec((None, s, 2 * half_pad), lambda j, kk: (j // 4, 0, j % 4)),
                    _resident((s, d), lambda j, kk: (0, 0)),
                    pl.BlockSpec((2 * half_pad, d), lambda j, kk: (j, 0)), (2 * ffp, d), _WIRE)
        swap = begin_swap(l, "ffn", ("ffn_w_up", "ffn_w_down"),
                          [d_wup.reshape(N_DEV, 2 * half_pad, d), d_wdown.reshape(N_DEV, half_pad, d)])
        tnd = _col_tile(d, 512)
        dy1 = _mm(f"ffn_up_dx_{l}", dhu, fw["wup_t"], NN, (d // tnd, N_DEV),
                  pl.BlockSpec((None, s, 2 * half_pad), lambda j, kk: (kk // 4, 0, kk % 4)),
                  pl.BlockSpec((2 * half_pad, tnd), lambda j, kk: (kk, j)),
                  pl.BlockSpec((s, tnd), lambda j, kk: (0, j)), (s, d), F32,
                  add=dr2, add_spec=pl.BlockSpec((s, tnd), lambda j, kk: (0, j)), add_scale=alpha, deps=(swap[3][-1],))
        token = begin_exchange(swap, dy1)
        dr1, dr1b, dg1, db1 = _ln_bwd(f"ln1_bwd_{l}", dy1, sv["xh1"], sv["rs1"], sv["g1"], deps=(token,))
        d_wout = _mm_tn_acols(f"out_proj_dw_{l}", sv["mix"], dr1b, _WIRE, cap=d // N_DEV)
        dmix = _mm_nt(f"out_proj_dx_{l}", dr1b, fw["wout"])
        dq_rot, dk_hm, dv_hm, dsk = _attn_bwd(f"attn_bwd_{l}", sv["q_rot"], sv["k_hm"], sv["v_hm"], sv["o_attn"], dmix,
                                             sv["lse"], sv["sinks"])
        dqk = jnp.concatenate([dq_rot, jnp.transpose(dk_hm, (1, 0, 2)).reshape(s, kv_w)], -1)
        dhq, dhk = _rope(f"rope_bwd_{l}", dqk, o_v, cos_t, -sin_t, _MXU, ((0, o_k), (o_k, o_v)))
        dhv = jnp.transpose(dv_hm, (1, 0, 2)).reshape(s, kv_w).astype(_MXU)
        dhp, dpw, dpsc = _pool_bwd(f"pool_bwd_{l}", dmix, attn_w // pool_wd, sv["pre"], sv["pw_b"], sv["psc"])
        dab2, dyg = _glu_bwd(f"glu_bwd_{l}", dmix, (attn_w + pool_wd) // ssm_wd, sv["ab2"], fw["glu"])
        d_glu = _mm_tn_bcols(f"glu_dw_{l}", sv["yg"], dab2, _WIRE)
        cw_ssm = ssm_wd // nt_ssm
        dhs, dd, dcdr, dcdi, dbdr, dbdi, dar, dai = _ssm_bwd(
            f"ssm_bwd_{l}", dyg, sv["ypre"], sv["h"], o_s // cw_ssm, sv["sr"], sv["si"], sv["bdr"], sv["bdi"], sv["cdr"],
            sv["cdi"], sv["dvec"], sv["ar"], sv["ai"])
        dh = jnp.concatenate([dhq, dhk, dhv, dhp, dhs], -1)
        d_win = _mm_tn_acols(f"in_proj_dw_{l}", dh, sv["xb"], _WIRE)
        swap = begin_swap(l, "mixer", ("w_in", "ssm_glu_w", "w_out"),
                          [d_win.reshape(N_DEV, in_w // N_DEV, d),
                           jnp.transpose(d_glu.reshape(ssm_wd, N_DEV, 2 * ssm_wd // N_DEV), (1, 0, 2)),
                           d_wout.reshape(N_DEV, d // N_DEV, d)])
        dy = _mm_nn(f"in_proj_dx_{l}", dh, fw["win_t"], add=dr1, add_scale=alpha, deps=(swap[3][-1],))

        raw = dict(attn_sinks=dsk, pool_w=dpw, pool_scale=dpsc, ssm_lam_re=dar, ssm_lam_im=dai,
                   ssm_b_re=_diagblocks(dbdr, n_groups, SSM_GROUP, SSM_STATE),
                   ssm_b_im=_diagblocks(dbdi, n_groups, SSM_GROUP, SSM_STATE),
                   ssm_c_re=_diagblocks(dcdr, n_groups, SSM_STATE, SSM_GROUP),
                   ssm_c_im=_diagblocks(dcdi, n_groups, SSM_STATE, SSM_GROUP), ssm_d=dd, ln1_g=dg1, ln1_b=db1,
                   ffn_conv_b=_unpad_pairs(dcb.reshape(N_DEV, 2 * half_pad), 1, half, half_pad), ln2_g=dg2, ln2_b=db2,
                   ffn_conv_w=_unpad_pairs(dcw.reshape(CONV_WIDTH, N_DEV, 2 * half_pad), 2, half, half_pad))
        raw_shapes = {k: raw[k].shape for k in _RAW}
        token = begin_exchange(swap, dy)
        small_handles[l] = _gather_begin(f"gather_small_grads_{l}", [_pack([raw[k] for k in _RAW])], [token])
        if l + 1 < depth:
            small_parts[l + 1] = _gather_end(f"gather_small_grads_{l + 1}", small_handles[l + 1], dy)[0]

    small_parts[0] = _gather_end("gather_small_grads_0", small_handles[0], token)[0]
    rows_l = small_parts[0].shape[1]
    summed = _sum_parts("sum_small_grads", jnp.concatenate(small_parts, 1)).reshape(depth, rows_l, V7X_LANES)
    g_small = dict(zip(_RAW, _unpack(summed, [raw_shapes[k] for k in _RAW])))
    swap_last = lambda t: jnp.transpose(t, (0, 1, 3, 2))
    _, vjp = jax.vjp(_ssm_discretise, *ssm_params)
    dlr, dli, dldt, dbr, dbi = vjp((g_small["ssm_lam_re"].reshape(depth, n_groups, SSM_STATE),
                                    g_small["ssm_lam_im"].reshape(depth, n_groups, SSM_STATE),
                                    swap_last(g_small["ssm_b_re"]), swap_last(g_small["ssm_b_im"])))
    g_small.update(ssm_lam_re=dlr, ssm_lam_im=dli, ssm_log_dt=dldt, ssm_b_re=dbr, ssm_b_im=dbi,
                   ssm_c_re=swap_last(g_small["ssm_c_re"]), ssm_c_im=swap_last(g_small["ssm_c_im"]),
                   ffn_conv_w=lax.dynamic_index_in_dim(g_small["ffn_conv_w"], me, axis=2, keepdims=False))
    for k in _SMALL + ("ffn_conv_w",):
        shp = W[k].shape
        two_d = (math.prod(shp[:-1]), shp[-1])
        res = _adamw_2d(f"adamw_{k}", g_small[k].reshape(two_d), *(t[k].reshape(two_d) for t in (W, M, V)))
        outs[k] = (g_small[k].reshape(shp),) + tuple(a.reshape(shp) for a in res)

    finish_exchanges(outs["ln2_b"][1])
    for name_ in _BIG:
        outs[name_] = tuple(from_groups(name_, t) for t in big_res[name_])

    grad_x = dy[None]
    result = [loss, grad_x]
    for i in range(4):
        result += [outs[k][i] for k in _ORDER]
    return tuple(result)
```

```python
import functools
import math

import jax
import jax.numpy as jnp
from jax import lax
from jax.experimental import pallas as pl
from jax.experimental.pallas import tpu as pltpu

F32 = jnp.float32
_MXU = jnp.bfloat16
_WIRE = jnp.bfloat16

HEAD_DIM = 64
GQA = 4
ATTN_BLOCK = 128
ROPE_THETA = 10000.0
POOL_WINDOWS = (2, 4, 8, 16)
SSM_GROUP = 16
SSM_STATE = 64
CONV_WIDTH = 3
LN_EPS = 1e-5
ADAM_LR, ADAM_B1, ADAM_B2, ADAM_EPS, ADAM_WD, ADAM_STEP = 0.001, 0.9, 0.999, 1e-08, 0.01, 10

N_DEV = 8
V7X_LANES = 128
V7X_VMEM_LIMIT = 56 * 1024 * 1024
SCAN_T = 64
SCAN_LANES = 256
MESH = pl.DeviceIdType.MESH
ANY = pl.BlockSpec(memory_space=pl.ANY)


def _cp():
    return pltpu.CompilerParams(vmem_limit_bytes=V7X_VMEM_LIMIT)


def _resident(block, index_map):
    return pl.BlockSpec(block, index_map, pipeline_mode=pl.Buffered(1))


def _sds(shape, dtype):
    return jax.ShapeDtypeStruct(tuple(shape), dtype)


def _mm(name, a, b, dims, grid, a_spec, b_spec, o_spec, out_shape, out_dtype, add=None, add_spec=None, add_scale=1.0, deps=()):
    nk = grid[1]
    n_in = 2 + (add is not None) + len(deps)
    oblk = tuple(d for d in o_spec.block_shape if d is not None)
    scratch = nk > 1 and out_dtype != F32

    def body(*refs):
        a_ref, b_ref = refs[:2]
        add_ref = None if add is None else refs[2]
        o_ref = refs[n_in]
        acc_ref = refs[-1] if scratch else None

        def finish(r):
            if add_ref is not None:
                r = r + add_scale * add_ref[...]
            o_ref[...] = r.astype(o_ref.dtype)

        part = lax.dot_general(a_ref[...], b_ref[...], (dims, ((), ())), preferred_element_type=F32)
        if nk == 1:
            finish(part)
        elif not scratch:
            k = pl.program_id(1)

            @pl.when(k == 0)
            def _():
                o_ref[...] = part

            @pl.when(k > 0)
            def _():
                o_ref[...] += part

            if add_ref is not None:
                @pl.when(k == nk - 1)
                def _():
                    o_ref[...] += add_scale * add_ref[...]
        else:
            k = pl.program_id(1)

            @pl.when(k == 0)
            def _():
                acc_ref[...] = part

            @pl.when(k > 0)
            def _():
                acc_ref[...] += part

            @pl.when(k == nk - 1)
            def _():
                finish(acc_ref[...])

    ins = [a, b] + ([] if add is None else [add]) + list(deps)
    in_specs = [a_spec, b_spec] + ([] if add is None else [add_spec]) + [ANY] * len(deps)
    return pl.pallas_call(
        body, grid=grid, in_specs=in_specs, out_specs=o_spec, out_shape=_sds(out_shape, out_dtype),
        scratch_shapes=[pltpu.VMEM(oblk, F32)] if scratch else [], compiler_params=_cp(), name=name,
    )(*ins)


NN = ((1,), (0,))
NT = ((1,), (1,))
TN = ((0,), (0,))


def _mm_split_k(name, a2, b, add, add_scale, deps=()):
    _, m, f = a2.shape
    n = b.shape[1]
    tm, tn = m // 2, _col_tile(n, 256)

    def body(a_ref, b_ref, add_ref, *rest):
        o_ref = rest[-1]
        o_ref[...] = (jnp.dot(a_ref[0], b_ref[:f, :], preferred_element_type=F32)
                      + jnp.dot(a_ref[1], b_ref[f:, :], preferred_element_type=F32) + add_scale * add_ref[...])

    tile = pl.BlockSpec((tm, tn), lambda i, j: (i, j))
    return pl.pallas_call(
        body, grid=(m // tm, n // tn),
        in_specs=[_resident((2, tm, f), lambda i, j: (0, i, 0)), pl.BlockSpec((2 * f, tn), lambda i, j: (0, j)), tile]
        + [ANY] * len(deps),
        out_specs=tile, out_shape=_sds((m, n), F32), compiler_params=_cp(), name=name,
    )(a2, b, add, *deps)


def _col_tile(n, cap=512):
    if n % V7X_LANES:
        return n
    t = min(cap, n)
    t -= t % V7X_LANES
    while n % t:
        t -= V7X_LANES
    return t


def _mm_nn(name, a, b, out_dtype=F32, cap=512, add=None, add_scale=1.0, deps=()):
    m, k = a.shape
    n = b.shape[1]
    tn = _col_tile(n, cap)
    o_spec = pl.BlockSpec((m, tn), lambda j, kk: (0, j))
    return _mm(name, a, b, NN, (n // tn, 1), _resident((m, k), lambda j, kk: (0, 0)),
               pl.BlockSpec((k, tn), lambda j, kk: (0, j)), o_spec, (m, n), out_dtype,
               add=add, add_spec=None if add is None else o_spec, add_scale=add_scale, deps=deps)


def _mm_nt(name, a, b, out_dtype=F32, add=None, add_scale=1.0, cap=512, deps=()):
    m, k = a.shape
    n = b.shape[0]
    tn = _col_tile(n, cap)
    o_spec = pl.BlockSpec((m, tn), lambda j, kk: (0, j))
    return _mm(name, a, b, NT, (n // tn, 1), _resident((m, k), lambda j, kk: (0, 0)),
               pl.BlockSpec((tn, k), lambda j, kk: (j, 0)), o_spec, (m, n), out_dtype,
               add=add, add_spec=None if add is None else o_spec, add_scale=add_scale, deps=deps)


def _mm_tn_bcols(name, a, b, out_dtype, cap=512):
    s, k = a.shape
    n = b.shape[1]
    tn = _col_tile(n, cap)
    return _mm(name, a, b, TN, (n // tn, 1), _resident((s, k), lambda j, kk: (0, 0)),
               pl.BlockSpec((s, tn), lambda j, kk: (0, j)), pl.BlockSpec((k, tn), lambda j, kk: (0, j)), (k, n), out_dtype)


def _mm_tn_acols(name, a, b, out_dtype, cap=512):
    s, k = a.shape
    n = b.shape[1]
    tk = _col_tile(k, cap)
    return _mm(name, a, b, TN, (k // tk, 1), pl.BlockSpec((s, tk), lambda i, kk: (0, i)),
               _resident((s, n), lambda i, kk: (0, 0)), pl.BlockSpec((tk, n), lambda i, kk: (i, 0)), (k, n), out_dtype)


def _ln_fwd(name, x, a, g, b, alpha):
    s, d = x.shape
    tr = min(256, s)

    def body(x_ref, a_ref, g_ref, b_ref, y_ref, yb_ref, xh_ref, rs_ref):
        r = alpha * x_ref[...] + a_ref[...]
        mu = jnp.mean(r, -1, keepdims=True)
        c = r - mu
        var = jnp.mean(c * c, -1, keepdims=True)
        rstd = lax.rsqrt(var + LN_EPS)
        xh = c * rstd
        y = xh * g_ref[...] + b_ref[...]
        y_ref[...] = y
        yb_ref[...] = y.astype(_MXU)
        xh_ref[...] = xh
        rs_ref[...] = rstd

    row = pl.BlockSpec((tr, d), lambda i: (i, 0))
    vec = pl.BlockSpec((1, d), lambda i: (0, 0))
    return pl.pallas_call(
        body, grid=(s // tr,), in_specs=[row, row, vec, vec],
        out_specs=[row, row, row, pl.BlockSpec((tr, 1), lambda i: (i, 0))],
        out_shape=[_sds((s, d), F32), _sds((s, d), _MXU), _sds((s, d), F32), _sds((s, 1), F32)],
        compiler_params=_cp(), name=name,
    )(x, a, g, b)


def _ln_bwd(name, dy, xh, rstd, g, deps=()):
    s, d = dy.shape
    tr = min(256, s)
    nd = len(deps)

    def body(dy_ref, xh_ref, rs_ref, g_ref, *rest):
        dr_ref, drb_ref, dg_ref, db_ref = rest[nd:]
        i = pl.program_id(0)
        dy_ = dy_ref[...]
        xh_ = xh_ref[...]
        dxh = dy_ * g_ref[...]
        m1 = jnp.mean(dxh, -1, keepdims=True)
        m2 = jnp.mean(dxh * xh_, -1, keepdims=True)
        dr = rs_ref[...] * (dxh - m1 - xh_ * m2)
        dr_ref[...] = dr
        drb_ref[...] = dr.astype(_MXU)
        pg = jnp.sum(dy_ * xh_, 0, keepdims=True)
        pb = jnp.sum(dy_, 0, keepdims=True)

        @pl.when(i == 0)
        def _():
            dg_ref[...] = pg
            db_ref[...] = pb

        @pl.when(i > 0)
        def _():
            dg_ref[...] += pg
            db_ref[...] += pb

    row = pl.BlockSpec((tr, d), lambda i: (i, 0))
    vec = pl.BlockSpec((1, d), lambda i: (0, 0))
    return pl.pallas_call(
        body, grid=(s // tr,), in_specs=[row, row, pl.BlockSpec((tr, 1), lambda i: (i, 0)), vec] + [ANY] * nd,
        out_specs=[row, row, vec, vec],
        out_shape=[_sds((s, d), F32), _sds((s, d), _MXU), _sds((1, d), F32), _sds((1, d), F32)],
        compiler_params=_cp(), name=name,
    )(dy, xh, rstd, g, *deps)


def _loss_head(name, y, target):
    s, d = y.shape
    tr = min(256, s)

    def body(y_ref, t_ref, dy_ref, l_ref):
        i = pl.program_id(0)
        e = y_ref[...] - t_ref[...]
        dy_ref[...] = e * (1.0 / d)
        part = 0.5 * jnp.sum(jnp.mean(e * e, -1, keepdims=True), 0, keepdims=True)

        @pl.when(i == 0)
        def _():
            l_ref[...] = part

        @pl.when(i > 0)
        def _():
            l_ref[...] += part

    row = pl.BlockSpec((tr, d), lambda i: (i, 0))
    return pl.pallas_call(
        body, grid=(s // tr,), in_specs=[row, row], out_specs=[row, pl.BlockSpec((1, 1), lambda i: (0, 0))],
        out_shape=[_sds((s, d), F32), _sds((1, 1), F32)], compiler_params=_cp(), name=name,
    )(y, target)


def _rope(name, t, width, cos, sin, out_dtype, splits):
    s = t.shape[0]
    tr = min(256, s)
    assert width % V7X_LANES == 0

    def body(t_ref, c_ref, s_ref, *o_refs):
        lane = lax.broadcasted_iota(jnp.int32, (tr, V7X_LANES), 1)
        first = (lane % HEAD_DIM) < (HEAD_DIM // 2)
        cs, sn = c_ref[...], s_ref[...]
        for (lo, hi), o_ref in zip(splits, o_refs):
            for c0 in range(lo, hi, V7X_LANES):
                v = t_ref[:, c0:c0 + V7X_LANES].astype(F32)
                partner = jnp.where(first, pltpu.roll(v, V7X_LANES - HEAD_DIM // 2, 1), pltpu.roll(v, HEAD_DIM // 2, 1))
                o_ref[:, c0 - lo:c0 - lo + V7X_LANES] = (v * cs + partner * sn).astype(o_ref.dtype)

    tab = pl.BlockSpec((tr, V7X_LANES), lambda i: (i, 0))
    return pl.pallas_call(
        body, grid=(s // tr,), in_specs=[pl.BlockSpec((tr, width), lambda i: (i, 0)), tab, tab],
        out_specs=[pl.BlockSpec((tr, hi - lo), lambda i: (i, 0)) for lo, hi in splits],
        out_shape=[_sds((s, hi - lo), out_dtype) for lo, hi in splits], compiler_params=_cp(), name=name,
    )(t, cos, sin)


def _attn_masks():
    i = lax.broadcasted_iota(jnp.int32, (GQA * ATTN_BLOCK, 2 * ATTN_BLOCK), 0) % ATTN_BLOCK
    j = lax.broadcasted_iota(jnp.int32, (GQA * ATTN_BLOCK, 2 * ATTN_BLOCK), 1)
    cur_ok = jnp.logical_and(j >= ATTN_BLOCK, j - ATTN_BLOCK <= i)
    prev_ok = jnp.logical_and(j < ATTN_BLOCK, j > i)
    return cur_ok, prev_ok


def _attn_scores(q4, kcat, n, cur_ok, prev_ok):
    sc = lax.dot_general(q4, kcat, (NT, ((), ())), preferred_element_type=F32) * (HEAD_DIM ** -0.5)
    return jnp.where(jnp.logical_or(cur_ok, jnp.logical_and(prev_ok, n > 0)), sc, -1e30)


def _stack_heads(ref, rows):
    return jnp.concatenate([ref[rows, g * HEAD_DIM:(g + 1) * HEAD_DIM] for g in range(GQA)], 0)


def _per_head_column(values):
    r = lax.broadcasted_iota(jnp.int32, (GQA * ATTN_BLOCK, 1), 0) // ATTN_BLOCK
    col = jnp.zeros((GQA * ATTN_BLOCK, 1), F32)
    for g, val in enumerate(values):
        col = jnp.where(r == g, val, col)
    return col


def _attn_fwd(name, q, k, v, sinks):
    s = q.shape[0]
    nkv = k.shape[0]
    gw = GQA * HEAD_DIM
    nb = s // ATTN_BLOCK

    def body(sk_ref, q_ref, k_ref, v_ref, o_ref, lse_ref):
        h = pl.program_id(0)
        cur_ok, prev_ok = _attn_masks()
        sink = _per_head_column([sk_ref[h, g] for g in range(GQA)])

        def blk(n, carry):
            rows = pl.ds(pl.multiple_of(n * ATTN_BLOCK, ATTN_BLOCK), ATTN_BLOCK)
            prows = pl.ds(pl.multiple_of(jnp.maximum(n - 1, 0) * ATTN_BLOCK, ATTN_BLOCK), ATTN_BLOCK)
            kcat = jnp.concatenate([k_ref[prows, :], k_ref[rows, :]], 0)
            vcat = jnp.concatenate([v_ref[prows, :], v_ref[rows, :]], 0)
            sc = _attn_scores(_stack_heads(q_ref, rows), kcat, n, cur_ok, prev_ok)
            m = jnp.maximum(sc.max(-1, keepdims=True), sink)
            p = jnp.exp(sc - m)
            den = p.sum(-1, keepdims=True) + jnp.exp(sink - m)
            o = jnp.dot((p / den).astype(_MXU), vcat, preferred_element_type=F32)
            lse = m + jnp.log(den)
            for g in range(GQA):
                mine = slice(g * ATTN_BLOCK, (g + 1) * ATTN_BLOCK)
                o_ref[rows, g * HEAD_DIM:(g + 1) * HEAD_DIM] = o[mine, :]
                lse_ref[rows, g:g + 1] = lse[mine, :]
            return carry

        lax.fori_loop(0, nb, blk, 0)

    kv_spec = pl.BlockSpec((None, s, HEAD_DIM), lambda h: (h, 0, 0))
    return pl.pallas_call(
        body, grid=(nkv,),
        in_specs=[pl.BlockSpec(memory_space=pltpu.SMEM), pl.BlockSpec((s, gw), lambda h: (0, h)), kv_spec, kv_spec],
        out_specs=[pl.BlockSpec((s, gw), lambda h: (0, h)), pl.BlockSpec((None, s, GQA), lambda h: (h, 0, 0))],
        out_shape=[_sds((s, nkv * gw), F32), _sds((nkv, s, GQA), F32)], compiler_params=_cp(), name=name,
    )(sinks, q, k, v)


def _attn_bwd(name, q, k, v, o, dmix, lse, sinks):
    s = q.shape[0]
    nkv = k.shape[0]
    gw = GQA * HEAD_DIM
    nb = s // ATTN_BLOCK
    scale = HEAD_DIM ** -0.5

    def body(sk_ref, q_ref, k_ref, v_ref, o_ref, do_ref, lse_ref, dq_ref, dk_ref, dv_ref, dsk_ref):
        h = pl.program_id(0)
        cur_ok, prev_ok = _attn_masks()
        dk_ref[...] = jnp.zeros_like(dk_ref)
        dv_ref[...] = jnp.zeros_like(dv_ref)

        sink = _per_head_column([sk_ref[h, g] for g in range(GQA)])

        def blk(n, acc):
            rows = pl.ds(pl.multiple_of(n * ATTN_BLOCK, ATTN_BLOCK), ATTN_BLOCK)
            prows = pl.ds(pl.multiple_of(jnp.maximum(n - 1, 0) * ATTN_BLOCK, ATTN_BLOCK), ATTN_BLOCK)
            kcat = jnp.concatenate([k_ref[prows, :], k_ref[rows, :]], 0)
            vcat = jnp.concatenate([v_ref[prows, :], v_ref[rows, :]], 0)
            q4 = _stack_heads(q_ref, rows)
            do4 = _stack_heads(do_ref, rows)
            delta = jnp.sum(do4 * _stack_heads(o_ref, rows), -1, keepdims=True)
            dob = do4.astype(_MXU)
            lse = jnp.concatenate([lse_ref[rows, g:g + 1] for g in range(GQA)], 0)
            p = jnp.exp(_attn_scores(q4, kcat, n, cur_ok, prev_ok) - lse)
            dp = lax.dot_general(dob, vcat, (NT, ((), ())), preferred_element_type=F32)
            ds = (p * (dp - delta) * scale).astype(_MXU)
            dq = jnp.dot(ds, kcat, preferred_element_type=F32)
            for g in range(GQA):
                dq_ref[rows, g * HEAD_DIM:(g + 1) * HEAD_DIM] = dq[g * ATTN_BLOCK:(g + 1) * ATTN_BLOCK, :]
            dk = lax.dot_general(ds, q4, (TN, ((), ())), preferred_element_type=F32)
            dv = lax.dot_general(p.astype(_MXU), dob, (TN, ((), ())), preferred_element_type=F32)
            dk_ref[prows, :] += dk[:ATTN_BLOCK, :]
            dv_ref[prows, :] += dv[:ATTN_BLOCK, :]
            dk_ref[rows, :] += dk[ATTN_BLOCK:, :]
            dv_ref[rows, :] += dv[ATTN_BLOCK:, :]
            return acc - jnp.exp(sink - lse) * delta

        acc = lax.fori_loop(0, nb, blk, jnp.zeros((GQA * ATTN_BLOCK, 1), F32))
        for g in range(GQA):
            dsk_ref[:, g:g + 1] = jnp.sum(acc[g * ATTN_BLOCK:(g + 1) * ATTN_BLOCK, :], 0, keepdims=True)

    kv_spec = pl.BlockSpec((None, s, HEAD_DIM), lambda h: (h, 0, 0))
    qcols = pl.BlockSpec((s, gw), lambda h: (0, h))
    return pl.pallas_call(
        body, grid=(nkv,),
        in_specs=[pl.BlockSpec(memory_space=pltpu.SMEM), qcols, kv_spec, kv_spec, qcols, qcols,
                  pl.BlockSpec((None, s, GQA), lambda h: (h, 0, 0))],
        out_specs=[qcols, kv_spec, kv_spec, pl.BlockSpec((None, 1, GQA), lambda h: (h, 0, 0))],
        out_shape=[_sds((s, nkv * gw), F32), _sds((nkv, s, HEAD_DIM), F32), _sds((nkv, s, HEAD_DIM), F32),
                   _sds((nkv, 1, GQA), F32)],
        compiler_params=_cp(), name=name,
    )(sinks, q, k, v, o, dmix, lse)


def _shift_down(a, k, t):
    return jnp.where(t >= k, pltpu.roll(a, k, 0), 0.0)


def _shift_up(a, k, t):
    n = a.shape[0]
    return jnp.where(t < n - k, pltpu.roll(a, n - k, 0), 0.0)


def _pool_fwd(name, h, col_block, pool_w, pool_scale):
    s = h.shape[0]
    ng, pg = pool_w.shape[0], pool_w.shape[1]
    pw_ = ng * pg

    def body(u_ref, w_ref, sc_ref, y_ref, pre_ref):
        t = lax.broadcasted_iota(jnp.int32, (s, pg), 0)
        for gi, win in enumerate(POOL_WINDOWS):
            cols = slice(gi * pg, (gi + 1) * pg)
            u = u_ref[:, cols]
            a = u
            k = 1
            while k < win:
                a = a + _shift_down(a, k, t)
                k *= 2
            div = jnp.minimum(t + 1, win).astype(F32)
            pre = (a / div - u).astype(_MXU)
            pre_ref[:, cols] = pre
            out = jnp.dot(pre, w_ref[gi], preferred_element_type=F32)
            y_ref[:, cols] = (out * sc_ref[:, cols]).astype(y_ref.dtype)

    blk = pl.BlockSpec((s, pw_), lambda i: (0, 0))
    return pl.pallas_call(
        body, grid=(1,),
        in_specs=[pl.BlockSpec((s, pw_), lambda i: (0, col_block)), pl.BlockSpec((ng, pg, pg), lambda i: (0, 0, 0)),
                  pl.BlockSpec((1, pw_), lambda i: (0, 0))],
        out_specs=[blk, blk], out_shape=[_sds((s, pw_), _MXU), _sds((s, pw_), _MXU)], compiler_params=_cp(), name=name,
    )(h, pool_w, pool_scale)


def _pool_bwd(name, dmix, col_block, pre, pool_w, pool_scale):
    s = pre.shape[0]
    ng, pg = pool_w.shape[0], pool_w.shape[1]
    pw_ = ng * pg

    def body(dy_ref, pre_ref, w_ref, sc_ref, du_ref, dw_ref, dsc_ref):
        t = lax.broadcasted_iota(jnp.int32, (s, pg), 0)
        for gi, win in enumerate(POOL_WINDOWS):
            cols = slice(gi * pg, (gi + 1) * pg)
            pre_g = pre_ref[:, cols]
            dy = dy_ref[:, cols]
            out = jnp.dot(pre_g, w_ref[gi], preferred_element_type=F32)
            dsc_ref[:, cols] = jnp.sum(dy * out, 0, keepdims=True)
            dout = (dy * sc_ref[:, cols]).astype(_MXU)
            dw_ref[gi] = lax.dot_general(pre_g, dout, (TN, ((), ())), preferred_element_type=F32)
            dpre = lax.dot_general(dout, w_ref[gi], (NT, ((), ())), preferred_element_type=F32)
            div = jnp.minimum(t + 1, win).astype(F32)
            a = dpre / div
            k = 1
            while k < win:
                a = a + _shift_up(a, k, t)
                k *= 2
            du_ref[:, cols] = (a - dpre).astype(du_ref.dtype)

    blk = pl.BlockSpec((s, pw_), lambda i: (0, 0))
    wspec = pl.BlockSpec((ng, pg, pg), lambda i: (0, 0, 0))
    vec = pl.BlockSpec((1, pw_), lambda i: (0, 0))
    return pl.pallas_call(
        body, grid=(1,), in_specs=[pl.BlockSpec((s, pw_), lambda i: (0, col_block)), blk, wspec, vec],
        out_specs=[blk, wspec, vec], out_shape=[_sds((s, pw_), _MXU), _sds((ng, pg, pg), F32), _sds((1, pw_), F32)],
        compiler_params=_cp(), name=name,
    )(dmix, pre, pool_w, pool_scale)


def _scan_chunks(xr_ref, xi_ref, sr_ref, si_ref, ar, ai, reverse):
    n, c = xr_ref.shape
    tt = min(SCAN_T, n)
    lw = min(SCAN_LANES, c)
    nchunk, ngroup = n // tt, tt // 8
    t8 = lax.broadcasted_iota(jnp.int32, (tt, lw), 0) % 8

    for l0 in range(0, c, lw):
        lanes = slice(l0, l0 + lw)
        a_r, a_i = ar[:, lanes], ai[:, lanes]

        def within8(vr, vi, a_r=a_r, a_i=a_i):
            rows = vr.shape[0]
            tq = t8[:rows, :]
            pr, pi = a_r, a_i
            k = 1
            while k < 8:
                if reverse:
                    hr = jnp.where(tq < 8 - k, pltpu.roll(vr, rows - k, 0), 0.0)
                    hi = jnp.where(tq < 8 - k, pltpu.roll(vi, rows - k, 0), 0.0)
                else:
                    hr = jnp.where(tq >= k, pltpu.roll(vr, k, 0), 0.0)
                    hi = jnp.where(tq >= k, pltpu.roll(vi, k, 0), 0.0)
                vr, vi = vr + pr * hr - pi * hi, vi + pr * hi + pi * hr
                pr, pi = pr * pr - pi * pi, 2.0 * pr * pi
                k *= 2
            return vr, vi

        at_edge = t8[:8, :] == (7 if reverse else 0)
        pw_r, pw_i = within8(jnp.where(at_edge, a_r, 0.0), jnp.where(at_edge, a_i, 0.0))
        last = 0 if reverse else 7

        def body(i, carry, lanes=lanes, within8=within8, pw_r=pw_r, pw_i=pw_i):
            cr, ci = carry
            ch = nchunk - 1 - i if reverse else i
            rows = pl.ds(pl.multiple_of(ch * tt, tt), tt)
            vr, vi = within8(xr_ref[rows, lanes], xi_ref[rows, lanes])
            out_r, out_i = [None] * ngroup, [None] * ngroup
            for g in (reversed(range(ngroup)) if reverse else range(ngroup)):
                br = vr[8 * g:8 * g + 8, :] + pw_r * cr - pw_i * ci
                bi = vi[8 * g:8 * g + 8, :] + pw_r * ci + pw_i * cr
                out_r[g], out_i[g] = br, bi
                cr, ci = br[last:last + 1, :], bi[last:last + 1, :]
            sr_ref[rows, lanes] = jnp.concatenate(out_r, 0)
            si_ref[rows, lanes] = jnp.concatenate(out_i, 0)
            return cr, ci

        lax.fori_loop(0, nchunk, body, (jnp.zeros((1, lw), F32), jnp.zeros((1, lw), F32)))


_GELU_K = math.sqrt(2.0 / math.pi)


def _gelu_grad(y):
    inner = _GELU_K * (y + 0.044715 * y * y * y)
    th = jnp.tanh(inner)
    return 0.5 * (1.0 + th) + 0.5 * y * (1.0 - th * th) * _GELU_K * (1.0 + 3.0 * 0.044715 * y * y)


def _ssm_fwd(name, h, u_block0, bdr, bdi, cdr, cdi, dvec, ar, ai):
    s = h.shape[0]
    nt, cw, lw = bdr.shape
    rc = min(256, s)

    def body(u_ref, bdr_ref, bdi_ref, cdr_ref, cdi_ref, d_ref, ar_ref, ai_ref, sr_ref, si_ref, y_ref, yg_ref):
        def mm_in(c, _):
            rows = pl.ds(pl.multiple_of(c * rc, rc), rc)
            ub = u_ref[rows, :].astype(_MXU)
            sr_ref[rows, :] = jnp.dot(ub, bdr_ref[...], preferred_element_type=F32)
            si_ref[rows, :] = jnp.dot(ub, bdi_ref[...], preferred_element_type=F32)
            return 0

        lax.fori_loop(0, s // rc, mm_in, 0)
        _scan_chunks(sr_ref, si_ref, sr_ref, si_ref, ar_ref[...], ai_ref[...], reverse=False)

        def mm_out(c, _):
            rows = pl.ds(pl.multiple_of(c * rc, rc), rc)
            y = (jnp.dot(sr_ref[rows, :].astype(_MXU), cdr_ref[...], preferred_element_type=F32)
                 - jnp.dot(si_ref[rows, :].astype(_MXU), cdi_ref[...], preferred_element_type=F32)
                 + d_ref[...] * u_ref[rows, :])
            y_ref[rows, :] = y
            yg_ref[rows, :] = jax.nn.gelu(y).astype(yg_ref.dtype)
            return 0

        lax.fori_loop(0, s // rc, mm_out, 0)

    st = pl.BlockSpec((s, lw), lambda j: (0, j))
    ch = pl.BlockSpec((s, cw), lambda j: (0, j))
    bspec = pl.BlockSpec((None, cw, lw), lambda j: (j, 0, 0))
    cspec = pl.BlockSpec((None, lw, cw), lambda j: (j, 0, 0))
    return pl.pallas_call(
        body, grid=(nt,),
        in_specs=[pl.BlockSpec((s, cw), lambda j: (0, u_block0 + j)), bspec, bspec, cspec, cspec,
                  pl.BlockSpec((1, cw), lambda j: (0, j)), pl.BlockSpec((1, lw), lambda j: (0, j)),
                  pl.BlockSpec((1, lw), lambda j: (0, j))],
        out_specs=[st, st, ch, ch],
        out_shape=[_sds((s, nt * lw), F32), _sds((s, nt * lw), F32), _sds((s, nt * cw), F32), _sds((s, nt * cw), _MXU)],
        compiler_params=_cp(), name=name,
    )(h, bdr, bdi, cdr, cdi, dvec, ar, ai)


def _ssm_bwd(name, dyg, ypre, h, u_block0, sr, si, bdr, bdi, cdr, cdi, dvec, ar, ai):
    s = h.shape[0]
    nt, cw, lw = bdr.shape
    rc = min(256, s)

    def body(dyg_ref, yp_ref, u_ref, sr_ref, si_ref, bdr_ref, bdi_ref, cdr_ref, cdi_ref, d_ref, ar_ref, ai_ref,
             du_ref, dd_ref, dcr_ref, dci_ref, dbr_ref, dbi_ref, dar_ref, dai_ref, lr_scr, li_scr, dy_scr):
        for ref in (dd_ref, dcr_ref, dci_ref, dbr_ref, dbi_ref, dar_ref, dai_ref):
            ref[...] = jnp.zeros_like(ref)

        def p1(c, _):
            rows = pl.ds(pl.multiple_of(c * rc, rc), rc)
            dy = dyg_ref[rows, :] * _gelu_grad(yp_ref[rows, :])
            dy_scr[rows, :] = dy
            dd_ref[...] += jnp.sum(dy * u_ref[rows, :], 0, keepdims=True)
            dyb = dy.astype(_MXU)
            lr_scr[rows, :] = lax.dot_general(dyb, cdr_ref[...], (NT, ((), ())), preferred_element_type=F32)
            li_scr[rows, :] = -lax.dot_general(dyb, cdi_ref[...], (NT, ((), ())), preferred_element_type=F32)
            dcr_ref[...] += lax.dot_general(sr_ref[rows, :].astype(_MXU), dyb, (TN, ((), ())), preferred_element_type=F32)
            dci_ref[...] -= lax.dot_general(si_ref[rows, :].astype(_MXU), dyb, (TN, ((), ())), preferred_element_type=F32)
            return 0

        lax.fori_loop(0, s // rc, p1, 0)
        _scan_chunks(lr_scr, li_scr, lr_scr, li_scr, ar_ref[...], -ai_ref[...], reverse=True)
        t = lax.broadcasted_iota(jnp.int32, (rc, lw), 0)

        def p2(c, _):
            r0 = pl.multiple_of(c * rc, rc)
            rows = pl.ds(r0, rc)
            before = pl.ds(pl.multiple_of(jnp.maximum(r0 - 8, 0), 8), 8)
            have = (c > 0).astype(F32)
            lr, li = lr_scr[rows, :], li_scr[rows, :]
            spr = jnp.where(t == 0, sr_ref[before, :][7:8, :] * have, pltpu.roll(sr_ref[rows, :], 1, 0))
            spi = jnp.where(t == 0, si_ref[before, :][7:8, :] * have, pltpu.roll(si_ref[rows, :], 1, 0))
            dar_ref[...] += jnp.sum(lr * spr + li * spi, 0, keepdims=True)
            dai_ref[...] += jnp.sum(li * spr - lr * spi, 0, keepdims=True)
            lrb, lib = lr.astype(_MXU), li.astype(_MXU)
            du = (dy_scr[rows, :] * d_ref[...]
                  + lax.dot_general(lrb, bdr_ref[...], (NT, ((), ())), preferred_element_type=F32)
                  + lax.dot_general(lib, bdi_ref[...], (NT, ((), ())), preferred_element_type=F32))
            du_ref[rows, :] = du.astype(du_ref.dtype)
            ub = u_ref[rows, :].astype(_MXU)
            dbr_ref[...] += lax.dot_general(ub, lrb, (TN, ((), ())), preferred_element_type=F32)
            dbi_ref[...] += lax.dot_general(ub, lib, (TN, ((), ())), preferred_element_type=F32)
            return 0

        lax.fori_loop(0, s // rc, p2, 0)

    st = pl.BlockSpec((s, lw), lambda j: (0, j))
    ch = pl.BlockSpec((s, cw), lambda j: (0, j))
    bspec = pl.BlockSpec((None, cw, lw), lambda j: (j, 0, 0))
    cspec = pl.BlockSpec((None, lw, cw), lambda j: (j, 0, 0))
    cvec = pl.BlockSpec((1, cw), lambda j: (0, j))
    svec = pl.BlockSpec((1, lw), lambda j: (0, j))
    return pl.pallas_call(
        body, grid=(nt,),
        in_specs=[ch, ch, pl.BlockSpec((s, cw), lambda j: (0, u_block0 + j)), st, st, bspec, bspec, cspec, cspec, cvec, svec, svec],
        out_specs=[ch, cvec, cspec, cspec, bspec, bspec, svec, svec],
        out_shape=[_sds((s, nt * cw), _MXU), _sds((1, nt * cw), F32), _sds((nt, lw, cw), F32), _sds((nt, lw, cw), F32),
                   _sds((nt, cw, lw), F32), _sds((nt, cw, lw), F32), _sds((1, nt * lw), F32), _sds((1, nt * lw), F32)],
        scratch_shapes=[pltpu.VMEM((s, lw), F32), pltpu.VMEM((s, lw), F32), pltpu.VMEM((s, cw), F32)],
        compiler_params=_cp(), name=name,
    )(dyg, ypre, h, sr, si, bdr, bdi, cdr, cdi, dvec, ar, ai)


def _glu_fwd(name, yg, gw):
    s, w = yg.shape
    tr = min(512, s)

    def body(y_ref, w_ref, o_ref, ab_ref):
        ab = jnp.dot(y_ref[...], w_ref[...], preferred_element_type=F32)
        ab_ref[...] = ab
        o_ref[...] = (ab[:, :w] * jax.nn.sigmoid(ab[:, w:])).astype(o_ref.dtype)

    return pl.pallas_call(
        body, grid=(s // tr,), in_specs=[pl.BlockSpec((tr, w), lambda i: (i, 0)), _resident((w, 2 * w), lambda i: (0, 0))],
        out_specs=[pl.BlockSpec((tr, w), lambda i: (i, 0)), pl.BlockSpec((tr, 2 * w), lambda i: (i, 0))],
        out_shape=[_sds((s, w), _MXU), _sds((s, 2 * w), F32)], compiler_params=_cp(), name=name,
    )(yg, gw)


def _glu_bwd(name, dmix, col_block, ab, gw):
    s = ab.shape[0]
    w = ab.shape[1] // 2
    tr = min(512, s)

    def body(do_ref, ab_ref, w_ref, dab_ref, dy_ref):
        do = do_ref[...]
        a, b = ab_ref[:, :w], ab_ref[:, w:]
        sg = jax.nn.sigmoid(b)
        da = (do * sg).astype(_MXU)
        db = (do * a * sg * (1.0 - sg)).astype(_MXU)
        dab_ref[:, :w] = da
        dab_ref[:, w:] = db
        dy_ref[...] = (lax.dot_general(da, w_ref[:, :w], (NT, ((), ())), preferred_element_type=F32)
                       + lax.dot_general(db, w_ref[:, w:], (NT, ((), ())), preferred_element_type=F32))

    return pl.pallas_call(
        body, grid=(s // tr,),
        in_specs=[pl.BlockSpec((tr, w), lambda i: (i, col_block)), pl.BlockSpec((tr, 2 * w), lambda i: (i, 0)),
                  _resident((w, 2 * w), lambda i: (0, 0))],
        out_specs=[pl.BlockSpec((tr, 2 * w), lambda i: (i, 0)), pl.BlockSpec((tr, w), lambda i: (i, 0))],
        out_shape=[_sds((s, 2 * w), _MXU), _sds((s, w), F32)], compiler_params=_cp(), name=name,
    )(dmix, ab, gw)


CONV_ROWS = 64


def _conv_chunk(ref, w_ref, b_ref, c, tt):
    r0 = pl.multiple_of(c * tt, tt)
    before = ref[pl.ds(pl.multiple_of(jnp.maximum(r0 - 8, 0), 8), 8), :]
    before = jnp.where(c > 0, before, 0.0)
    main = ref[pl.ds(r0, tt), :]
    ext = jnp.concatenate([before, main], 0)
    d1 = pltpu.roll(ext, 1, 0)[8:, :]
    d2 = pltpu.roll(ext, 2, 0)[8:, :]
    hc = b_ref[...] + d2 * w_ref[0:1, :]
    hc = hc + d1 * w_ref[1:2, :]
    return hc + main * w_ref[2:3, :], main, d1, d2


def _conv_act_fwd(name, hu, cw, cb):
    s, f2 = hu.shape
    f = f2 // 2
    tw = _col_tile(f, 256)
    nt = f // tw

    tt = min(CONV_ROWS, s)

    def body(v_ref, g_ref, wv_ref, wg_ref, bv_ref, bg_ref, act_ref):
        def chunk(c, _):
            val = _conv_chunk(v_ref, wv_ref, bv_ref, c, tt)[0]
            gate = _conv_chunk(g_ref, wg_ref, bg_ref, c, tt)[0]
            act_ref[pl.ds(pl.multiple_of(c * tt, tt), tt), :] = (jax.nn.silu(gate) * val).astype(act_ref.dtype)
            return 0

        lax.fori_loop(0, s // tt, chunk, 0)

    cv = lambda rows: pl.BlockSpec((rows, tw), lambda i: (0, i))
    cg = lambda rows: pl.BlockSpec((rows, tw), lambda i: (0, nt + i))
    return pl.pallas_call(
        body, grid=(nt,), in_specs=[cv(s), cg(s), cv(CONV_WIDTH), cg(CONV_WIDTH), cv(1), cg(1)],
        out_specs=cv(s), out_shape=_sds((s, f), _MXU), compiler_params=_cp(), name=name,
    )(hu, hu, cw, cw, cb, cb)


def _conv_act_bwd(name, dact, hu, cw, cb):
    s, f2 = hu.shape
    f = f2 // 2
    tw = _col_tile(f, 256)
    nt = f // tw

    tt = min(CONV_ROWS, s)
    nchunk = s // tt

    def body(da_ref, v_ref, g_ref, wv_ref, wg_ref, bv_ref, bg_ref, dh_ref, dwv_ref, dwg_ref, dbv_ref, dbg_ref):
        def chunk(i, carry):
            c = nchunk - 1 - i
            rows = pl.ds(pl.multiple_of(c * tt, tt), tt)
            val, hv, hv1, hv2 = _conv_chunk(v_ref, wv_ref, bv_ref, c, tt)
            gate, hg, hg1, hg2 = _conv_chunk(g_ref, wg_ref, bg_ref, c, tt)
            sg = jax.nn.sigmoid(gate)
            da = da_ref[rows, :]
            dval = da * (gate * sg)
            dgate = da * val * sg * (1.0 + gate * (1.0 - sg))
            out = []
            for part, dhc, taps, w_ref, (after, acc) in ((0, dval, (hv2, hv1, hv), wv_ref, carry[0]),
                                                        (1, dgate, (hg2, hg1, hg), wg_ref, carry[1])):
                ext = jnp.concatenate([dhc, after], 0)
                u1 = pltpu.roll(ext, tt + 8 - 1, 0)[:tt, :]
                u2 = pltpu.roll(ext, tt + 8 - 2, 0)[:tt, :]
                dh = dhc * w_ref[2:3, :] + u1 * w_ref[1:2, :] + u2 * w_ref[0:1, :]
                dh_ref[part, rows, :] = dh.astype(dh_ref.dtype)
                sums = [jnp.sum(dhc * tap, 0, keepdims=True) for tap in taps] + [jnp.sum(dhc, 0, keepdims=True)]
                out.append((dhc[0:8, :], tuple(a + b for a, b in zip(acc, sums))))
            return tuple(out)

        zero = (jnp.zeros((8, tw), F32), tuple(jnp.zeros((1, tw), F32) for _ in range(CONV_WIDTH + 1)))
        (_, acc_v), (_, acc_g) = lax.fori_loop(0, nchunk, chunk, (zero, zero))
        for acc, dw_ref, db_ref in ((acc_v, dwv_ref, dbv_ref), (acc_g, dwg_ref, dbg_ref)):
            for tap in range(CONV_WIDTH):
                dw_ref[tap:tap + 1, :] = acc[tap]
            db_ref[...] = acc[CONV_WIDTH]

    cv = lambda rows: pl.BlockSpec((rows, tw), lambda i: (0, i))
    cg = lambda rows: pl.BlockSpec((rows, tw), lambda i: (0, nt + i))
    both = pl.BlockSpec((2, s, tw), lambda i: (0, 0, i))
    dh, dwv, dwg, dbv, dbg = pl.pallas_call(
        body, grid=(nt,), in_specs=[cv(s), cv(s), cg(s), cv(CONV_WIDTH), cg(CONV_WIDTH), cv(1), cg(1)],
        out_specs=[both, cv(CONV_WIDTH), cv(CONV_WIDTH), cv(1), cv(1)],
        out_shape=[_sds((2, s, f), _MXU), _sds((CONV_WIDTH, f), F32), _sds((CONV_WIDTH, f), F32),
                   _sds((1, f), F32), _sds((1, f), F32)],
        compiler_params=_cp(), name=name,
    )(dact, hu, hu, cw, cw, cb, cb)
    return dh, jnp.concatenate([dwv, dwg], 1), jnp.concatenate([dbv, dbg], 1)


ELEM_BLOCK = 512 * 1024


def _elem_tiles(r, c, budget=ELEM_BLOCK):
    rows = [t for t in range(8, r + 1, 8) if r % t == 0] or [r]
    cols = [t for t in range(V7X_LANES, c + 1, V7X_LANES) if c % t == 0] or [c]
    fits = [(tr * tc, tc, tr) for tr in rows for tc in cols if tr * tc <= budget]
    if not fits:
        return min(rows), min(cols)
    _, tc, tr = max(fits)
    return tr, tc


def _sum_parts(name, parts):
    n, r, c = parts.shape
    tr, tc = _elem_tiles(r, c, ELEM_BLOCK // n)

    def body(p_ref, g_ref):
        g = p_ref[0]
        for i in range(1, n):
            g = g + p_ref[i]
        g_ref[...] = g

    return pl.pallas_call(
        body, grid=(r // tr, c // tc), in_specs=[pl.BlockSpec((n, tr, tc), lambda i, j: (0, i, j))],
        out_specs=pl.BlockSpec((tr, tc), lambda i, j: (i, j)), out_shape=_sds((r, c), F32), compiler_params=_cp(), name=name,
    )(parts)


def _adamw_2d(name, g, w, m, v):
    r, c = w.shape
    tc = c if c % V7X_LANES else _col_tile(c, 2048)
    rows = [t for t in range(8, r + 1, 8) if r % t == 0 and t * max(tc, V7X_LANES) <= ELEM_BLOCK // 4] or [r]
    tr = max(rows)
    c1 = 1.0 - ADAM_B1 ** ADAM_STEP
    c2 = 1.0 - ADAM_B2 ** ADAM_STEP

    def body(g_ref, w_ref, m_ref, v_ref, d_ref, nm_ref, nv_ref):
        g = g_ref[...]
        nm = ADAM_B1 * m_ref[...] + (1.0 - ADAM_B1) * g
        nv = ADAM_B2 * v_ref[...] + (1.0 - ADAM_B2) * (g * g)
        m_hat = nm / c1
        v_hat = nv / c2
        nm_ref[...] = nm
        nv_ref[...] = nv
        d_ref[...] = -ADAM_LR * (m_hat / (jnp.sqrt(v_hat) + ADAM_EPS) + ADAM_WD * w_ref[...])

    blk = pl.BlockSpec((tr, tc), lambda i, j: (i, j))
    out = _sds((r, c), F32)
    return pl.pallas_call(
        body, grid=(r // tr, c // tc), in_specs=[blk] * 4, out_specs=[blk] * 3, out_shape=[out] * 3,
        compiler_params=_cp(), name=name,
    )(g, w, m, v)


def _pair_sum(name, mine, theirs, c_idx):
    _, _, r, c = mine.shape
    tr, tc = _elem_tiles(r, c)

    def body(c_ref, a_ref, b_ref, o_ref):
        o_ref[...] = (a_ref[...].astype(F32) + b_ref[...].astype(F32)).astype(o_ref.dtype)

    return pl.pallas_call(
        body,
        grid_spec=pltpu.PrefetchScalarGridSpec(
            num_scalar_prefetch=1, grid=(4, r // tr, c // tc),
            in_specs=[pl.BlockSpec((None, None, tr, tc), lambda p, i, j, cref: (p, cref[0], i, j)),
                      pl.BlockSpec((None, tr, tc), lambda p, i, j, cref: (p, i, j))],
            out_specs=pl.BlockSpec((None, tr, tc), lambda p, i, j, cref: (p, i, j))),
        out_shape=_sds((4, r, c), _WIRE), compiler_params=_cp(), name=name,
    )(c_idx, mine, theirs)


def _place():
    return lax.axis_index("x"), lax.axis_index("y"), lax.axis_index("c")


def _all_gather(name, xs):
    n = len(xs)

    def body(*refs):
        x_refs, o_refs = refs[:n], refs[n:2 * n]
        send_sems, recv_sems, local_sems = refs[2 * n:]
        x, y, c = _place()
        me, sibling = (x, y, c), (x, y, 1 - c)
        chips = [(1 - x, y), (x, 1 - y), (1 - x, 1 - y)]

        def copy(a, k, block, to, src=None):
            px, py, pc = block
            rows = o_refs[a].at[4 * px + 2 * py + pc]
            return pltpu.make_async_remote_copy(
                src_ref=rows if src is None else src, dst_ref=rows, send_sem=send_sems.at[a, k], recv_sem=recv_sems.at[a, k],
                device_id=to, device_id_type=MESH)

        sent = []
        mine = []
        for a in range(n):
            mx, my, mc = me
            cp = pltpu.make_async_copy(x_refs[a], o_refs[a].at[4 * mx + 2 * my + mc], local_sems.at[a])
            cp.start()
            mine.append(cp)
            first = [copy(a, 0, me, sibling, src=x_refs[a])]
            first += [copy(a, 1 + j, me, (*chip, c), src=x_refs[a]) for j, chip in enumerate(chips)]
            for cp in first:
                cp.start()
            sent += first
        for a in range(n):
            for j, chip in enumerate(chips):
                copy(a, 1 + j, (*chip, c), me).wait_recv()
                fwd = copy(a, 4 + j, (*chip, c), sibling)
                fwd.start()
                sent.append(fwd)
        for a in range(n):
            copy(a, 0, sibling, me).wait_recv()
            for j, chip in enumerate(chips):
                copy(a, 4 + j, (*chip, 1 - c), me).wait_recv()
        for cp in sent:
            cp.wait_send()
        for cp in mine:
            cp.wait()

    return pl.pallas_call(
        body, in_specs=[ANY] * n, out_specs=[ANY] * n,
        out_shape=[_sds((N_DEV,) + a.shape, a.dtype) for a in xs],
        scratch_shapes=[pltpu.SemaphoreType.DMA((n, 7)), pltpu.SemaphoreType.DMA((n, 7)), pltpu.SemaphoreType.DMA((n,))],
        name=name,
    )(*xs)


HBM = pl.BlockSpec(memory_space=pltpu.HBM)
SEM = pl.BlockSpec(memory_space=pltpu.SEMAPHORE)
DATAFLOW = pltpu.SideEffectType.DATAFLOW_SIDE_EFFECTING


def _in_hbm(a):
    return pltpu.with_memory_space_constraint(a, pltpu.HBM)


def _split_copy_start(name, srcs, lands, copies, deps):
    n, nd = len(srcs), len(deps)
    per = len(copies([None] * n, [None] * n, probe=True)) // n

    def body(*refs):
        s_refs, l_refs = refs[:n], refs[n:2 * n]
        send_sems, recv_sems = refs[2 * n + nd], refs[2 * n + nd + 1]
        token = refs[-1]
        for a, k, src, dst, to in copies(s_refs, l_refs):
            pltpu.make_async_remote_copy(src_ref=src, dst_ref=dst, send_sem=send_sems.at[a * per + k],
                                         recv_sem=recv_sems.at[a * per + k], device_id=to, device_id_type=MESH).start()
        token[...] = jnp.zeros_like(token)

    both = list(srcs) + list(lands)
    outs = pl.pallas_call(
        body, name=name,
        out_shape=(pltpu.SemaphoreType.DMA((n * per,)), pltpu.SemaphoreType.DMA((n * per,)),
                   *[pltpu.HBM(a.shape, a.dtype) for a in both], _sds((8, V7X_LANES), F32)),
        in_specs=[HBM] * (2 * n) + [ANY] * nd,
        out_specs=(SEM, SEM, *[HBM] * (2 * n), pl.BlockSpec(memory_space=pltpu.VMEM)),
        input_output_aliases={i: 2 + i for i in range(2 * n)},
        compiler_params=pltpu.CompilerParams(has_side_effects=DATAFLOW),
    )(*[_in_hbm(a) for a in both], *deps)
    return outs[0], outs[1], list(outs[2:2 + n]), list(outs[2 + n:2 + 2 * n]), outs[-1]


def _split_copy_wait(name, send_sems, recv_sems, srcs, lands, arrivals, after):
    n = len(srcs)
    per = len(arrivals([None] * n, [None] * n, probe=True)) // n

    def body(*refs):
        s_refs, l_refs = refs[:n], refs[n:2 * n]
        send_sems_, recv_sems_ = refs[2 * n], refs[2 * n + 1]
        for a, k, src, dst, frm in arrivals(s_refs, l_refs):
            cp = pltpu.make_async_remote_copy(src_ref=src, dst_ref=dst, send_sem=send_sems_.at[a * per + k],
                                              recv_sem=recv_sems_.at[a * per + k], device_id=frm, device_id_type=MESH)
            cp.wait_send()
            cp.wait_recv()

    both = list(srcs) + list(lands)
    outs = pl.pallas_call(
        body, name=name, out_shape=tuple(pltpu.HBM(a.shape, a.dtype) for a in both),
        in_specs=[HBM] * (2 * n) + [SEM, SEM, ANY], out_specs=tuple([HBM] * (2 * n)),
        input_output_aliases={i: i for i in range(2 * n)},
        compiler_params=pltpu.CompilerParams(has_side_effects=DATAFLOW),
    )(*both, send_sems, recv_sems, after)
    return list(outs[:n]), list(outs[n:])


def _gather_copies(arriving):
    def copies(s_refs, l_refs, probe=False):
        if probe:
            return [None] * (4 * len(s_refs))
        x, y, c = _place()
        out = []
        for a in range(len(s_refs)):
            for k, (px, py, pc) in enumerate([(x, y, 1 - c), (1 - x, y, c), (x, 1 - y, c), (1 - x, 1 - y, c)]):
                slot = 4 * px + 2 * py + pc if arriving else 4 * x + 2 * y + c
                out.append((a, k, s_refs[a], l_refs[a].at[slot], (px, py, pc)))
        return out
    return copies


def _chip_copies(arriving):
    def copies(s_refs, l_refs, probe=False):
        if probe:
            return [None] * (3 * len(s_refs))
        x, y, c = _place()
        out = []
        for a in range(len(s_refs)):
            for j, (px, py) in enumerate([(1 - x, y), (x, 1 - y), (1 - x, 1 - y)]):
                src = s_refs[a].at[2 * x + y] if arriving else s_refs[a].at[2 * px + py]
                out.append((a, j, src, l_refs[a].at[j], (px, py, c)))
        return out
    return copies


def _sibling_copies(s_refs, l_refs, probe=False):
    if probe:
        return [None] * (4 * len(s_refs))
    x, y, c = _place()
    return [(a, p, s_refs[a].at[p, 1 - c], l_refs[a].at[p], (x, y, 1 - c)) for a in range(len(s_refs)) for p in range(4)]


def _sibling_begin(name, by_owner, deps):
    lands = [lax.empty((4,) + a.shape[2:], a.dtype) for a in by_owner]
    return _split_copy_start(name + "_start", by_owner, lands, _sibling_copies, deps)


def _sibling_end(name, handle, after):
    send_sems, recv_sems, srcs, lands, _ = handle
    return _split_copy_wait(name + "_wait", send_sems, recv_sems, srcs, lands, _sibling_copies, after)


def _gather_begin(name, shards, deps):
    x, y, c = _place()
    lands = [lax.dynamic_update_slice_in_dim(lax.empty((N_DEV,) + a.shape, a.dtype), a[None], 4 * x + 2 * y + c, 0)
             for a in shards]
    return _split_copy_start(name + "_start", shards, lands, _gather_copies(False), deps)


def _gather_end(name, handle, after):
    send_sems, recv_sems, srcs, lands, _ = handle
    _, lands = _split_copy_wait(name + "_wait", send_sems, recv_sems, srcs, lands, _gather_copies(True), after)
    return _gather_forward(name + "_forward", lands)


def _forward_copies(arriving):
    def copies(s_refs, l_refs, probe=False):
        if probe:
            return [None] * (3 * len(l_refs))
        x, y, c = _place()
        out = []
        for a in range(len(l_refs)):
            for j, (px, py) in enumerate([(1 - x, y), (x, 1 - y), (1 - x, 1 - y)]):
                mine, theirs = l_refs[a].at[4 * px + 2 * py + c], l_refs[a].at[4 * px + 2 * py + 1 - c]
                out.append((a, j, mine, theirs if arriving else mine, (x, y, 1 - c)))
        return out
    return copies


def _gather_arrived(name, handle, after):
    send_sems, recv_sems, srcs, lands, _ = handle
    srcs, lands = _split_copy_wait(name + "_wait", send_sems, recv_sems, srcs, lands, _gather_copies(True), after)
    return _split_copy_start(name + "_forward_start", srcs, lands, _forward_copies(False), [])


def _gather_done(name, handle, after):
    send_sems, recv_sems, srcs, lands, _ = handle
    return _split_copy_wait(name + "_forward_wait", send_sems, recv_sems, srcs, lands, _forward_copies(True), after)[1]


def _gather_forward(name, lands):
    n = len(lands)

    def body(*refs):
        o_refs = refs[n:2 * n]
        send_sems, recv_sems = refs[2 * n:]
        x, y, c = _place()
        sibling = (x, y, 1 - c)
        chips = [(1 - x, y), (x, 1 - y), (1 - x, 1 - y)]
        sent = []
        for a in range(n):
            for j, (px, py) in enumerate(chips):
                rows = o_refs[a].at[4 * px + 2 * py + c]
                cp = pltpu.make_async_remote_copy(src_ref=rows, dst_ref=rows, send_sem=send_sems.at[a, j],
                                                  recv_sem=recv_sems.at[a, j], device_id=sibling, device_id_type=MESH)
                cp.start()
                sent.append(cp)
        for a in range(n):
            for j, (px, py) in enumerate(chips):
                rows = o_refs[a].at[4 * px + 2 * py + 1 - c]
                pltpu.make_async_remote_copy(src_ref=rows, dst_ref=rows, send_sem=send_sems.at[a, j],
                                             recv_sem=recv_sems.at[a, j], device_id=sibling, device_id_type=MESH).wait_recv()
        for cp in sent:
            cp.wait_send()

    return pl.pallas_call(
        body, in_specs=[ANY] * n, out_specs=[ANY] * n, out_shape=[_sds(a.shape, a.dtype) for a in lands],
        input_output_aliases={i: i for i in range(n)},
        scratch_shapes=[pltpu.SemaphoreType.DMA((n, 3)), pltpu.SemaphoreType.DMA((n, 3))], name=name,
    )(*lands)


def _chips_begin(name, pairs, deps):
    lands = [lax.empty((3,) + a.shape[1:], a.dtype) for a in pairs]
    return _split_copy_start(name + "_start", pairs, lands, _chip_copies(False), deps)


def _chips_end(name, handle, after):
    send_sems, recv_sems, srcs, lands, _ = handle
    return _split_copy_wait(name + "_wait", send_sems, recv_sems, srcs, lands, _chip_copies(True), after)


def _adamw_layer(name, l, own, lands, w, m, v, prev):
    nl, ng, r, c = w.shape
    tr, tc = _elem_tiles(r, c)
    c1 = 1.0 - ADAM_B1 ** ADAM_STEP
    c2 = 1.0 - ADAM_B2 ** ADAM_STEP

    def body(own_ref, lands_ref, w_ref, m_ref, v_ref, *rest):
        g_ref, d_ref, nm_ref, nv_ref = rest[-4:]
        g = own_ref[...].astype(F32) + lands_ref[0].astype(F32) + lands_ref[1].astype(F32) + lands_ref[2].astype(F32)
        nm = ADAM_B1 * m_ref[...] + (1.0 - ADAM_B1) * g
        nv = ADAM_B2 * v_ref[...] + (1.0 - ADAM_B2) * (g * g)
        m_hat = nm / c1
        v_hat = nv / c2
        g_ref[...] = g
        nm_ref[...] = nm
        nv_ref[...] = nv
        d_ref[...] = -ADAM_LR * (m_hat / (jnp.sqrt(v_hat) + ADAM_EPS) + ADAM_WD * w_ref[...])

    lay = pl.BlockSpec((None, None, tr, tc), lambda g, i, j: (l, g, i, j))
    out = _sds((nl, ng, r, c), F32)
    prev = [] if prev is None else list(prev)
    return pl.pallas_call(
        body, grid=(ng, r // tr, c // tc),
        in_specs=[pl.BlockSpec((None, tr, tc), lambda g, i, j: (g, i, j)),
                  pl.BlockSpec((3, None, tr, tc), lambda g, i, j: (0, g, i, j)), lay, lay, lay] + [ANY] * len(prev),
        out_specs=[lay] * 4, out_shape=[out] * 4, input_output_aliases={5 + i: i for i in range(len(prev))},
        compiler_params=_cp(), name=name,
    )(own, lands, w, m, v, *prev)


def _pad_pairs(a, axis, half, half_pad):
    shp = a.shape
    a = a.reshape(shp[:axis] + (2, half) + shp[axis + 1:])
    pad = [(0, 0)] * a.ndim
    pad[axis + 1] = (0, half_pad - half)
    a = jnp.pad(a, pad)
    return a.reshape(shp[:axis] + (2 * half_pad,) + shp[axis + 1:])


def _unpad_pairs(a, axis, half, half_pad):
    shp = a.shape
    a = a.reshape(shp[:axis] + (2, half_pad) + shp[axis + 1:])
    a = lax.slice_in_dim(a, 0, half, axis=axis + 1)
    return a.reshape(shp[:axis] + (2 * half,) + shp[axis + 1:])


def _blockdiag(w, nt):
    g, a, b = w.shape
    gl = g // nt
    e = jnp.eye(gl, dtype=w.dtype).reshape(1, gl, 1, gl, 1)
    return (w.reshape(nt, gl, a, 1, b) * e).reshape(nt, gl * a, gl * b)


def _diagblocks(m, g, a, b):
    nt = m.shape[0]
    gl = g // nt
    d = jnp.diagonal(m.reshape(nt, gl, a, gl, b), axis1=1, axis2=3)
    return jnp.moveaxis(d, -1, 1).reshape(g, a, b)


_PIECE = 8 * V7X_LANES


def _pack(pieces, row_multiple=512):
    rows = []
    for p in pieces:
        flat = p.reshape(-1).astype(F32)
        rows.append(jnp.pad(flat, (0, -flat.shape[0] % _PIECE)).reshape(-1, V7X_LANES))
    fill = -sum(r.shape[0] for r in rows) % row_multiple
    if fill:
        rows.append(jnp.zeros((fill, V7X_LANES), F32))
    return jnp.concatenate(rows, 0)


def _unpack(packed, shapes):
    lead = packed.shape[:-2]
    out, off = [], 0
    for shp in shapes:
        n = math.prod(shp)
        r = -(-n // _PIECE) * 8
        piece = packed[..., off:off + r, :].reshape(lead + (r * V7X_LANES,))[..., :n]
        out.append(piece.reshape(lead + tuple(shp)))
        off += r
    return out


def _ssm_discretise(lam_re, lam_im, log_dt, b_re, b_im):
    dt = jnp.exp(log_dt)[..., None]
    mag = jnp.exp(lam_re * dt)
    ab_re, ab_im = mag * jnp.cos(lam_im * dt), mag * jnp.sin(lam_im * dt)
    nr, ni = ab_re - 1.0, ab_im
    den = lam_re * lam_re + lam_im * lam_im
    zr = (nr * lam_re + ni * lam_im) / den
    zi = (ni * lam_re - nr * lam_im) / den
    bbr = zr[..., None] * b_re - zi[..., None] * b_im
    bbi = zr[..., None] * b_im + zi[..., None] * b_re
    return ab_re, ab_im, bbr, bbi


def _rope_tables(s):
    half = HEAD_DIM // 2
    inv = ROPE_THETA ** (-jnp.arange(half, dtype=F32) / half)
    ang = jnp.arange(s).astype(F32)[:, None] * inv[None, :]
    cos, sin = jnp.cos(ang), jnp.sin(ang)
    reps = V7X_LANES // HEAD_DIM
    return jnp.tile(jnp.concatenate([cos, cos], -1), (1, reps)), jnp.tile(jnp.concatenate([-sin, sin], -1), (1, reps))


_SMALL = ("attn_sinks", "pool_w", "pool_scale", "ssm_lam_re", "ssm_lam_im", "ssm_log_dt", "ssm_b_re", "ssm_b_im",
          "ssm_c_re", "ssm_c_im", "ssm_d", "ln1_g", "ln1_b", "ffn_conv_b", "ln2_g", "ln2_b")
_BIG = ("w_in", "ssm_glu_w", "w_out", "ffn_w_up", "ffn_w_down")
_RAW = ("attn_sinks", "pool_w", "pool_scale", "ssm_lam_re", "ssm_lam_im", "ssm_b_re", "ssm_b_im", "ssm_c_re", "ssm_c_im",
        "ssm_d", "ln1_g", "ln1_b", "ffn_conv_b", "ln2_g", "ln2_b", "ffn_conv_w")
_ORDER = ("w_in", "attn_sinks", "pool_w", "pool_scale", "ssm_lam_re", "ssm_lam_im", "ssm_log_dt", "ssm_b_re", "ssm_b_im",
          "ssm_c_re", "ssm_c_im", "ssm_d", "ssm_glu_w", "w_out", "ln1_g", "ln1_b", "ffn_w_up", "ffn_conv_w", "ffn_conv_b",
          "ffn_w_down", "ln2_g", "ln2_b")


def kernel(x, w_in, attn_sinks, pool_w, pool_scale, ssm_lam_re, ssm_lam_im, ssm_log_dt, ssm_b_re, ssm_b_im, ssm_c_re, ssm_c_im, ssm_d, ssm_glu_w, w_out, ln1_g, ln1_b, ffn_w_up, ffn_conv_w, ffn_conv_b, ffn_w_down, ln2_g, ln2_b, loss_target, m_w_in, m_attn_sinks, m_pool_w, m_pool_scale, m_ssm_lam_re, m_ssm_lam_im, m_ssm_log_dt, m_ssm_b_re, m_ssm_b_im, m_ssm_c_re, m_ssm_c_im, m_ssm_d, m_ssm_glu_w, m_w_out, m_ln1_g, m_ln1_b, m_ffn_w_up, m_ffn_conv_w, m_ffn_conv_b, m_ffn_w_down, m_ln2_g, m_ln2_b, v_w_in, v_attn_sinks, v_pool_w, v_pool_scale, v_ssm_lam_re, v_ssm_lam_im, v_ssm_log_dt, v_ssm_b_re, v_ssm_b_im, v_ssm_c_re, v_ssm_c_im, v_ssm_d, v_ssm_glu_w, v_w_out, v_ln1_g, v_ln1_b, v_ffn_w_up, v_ffn_conv_w, v_ffn_conv_b, v_ffn_w_down, v_ln2_g, v_ln2_b):
    W = dict(w_in=w_in, attn_sinks=attn_sinks, pool_w=pool_w, pool_scale=pool_scale, ssm_lam_re=ssm_lam_re, ssm_lam_im=ssm_lam_im, ssm_log_dt=ssm_log_dt, ssm_b_re=ssm_b_re, ssm_b_im=ssm_b_im, ssm_c_re=ssm_c_re, ssm_c_im=ssm_c_im, ssm_d=ssm_d, ssm_glu_w=ssm_glu_w, w_out=w_out, ln1_g=ln1_g, ln1_b=ln1_b, ffn_w_up=ffn_w_up, ffn_conv_w=ffn_conv_w, ffn_conv_b=ffn_conv_b, ffn_w_down=ffn_w_down, ln2_g=ln2_g, ln2_b=ln2_b)
    M = dict(w_in=m_w_in, attn_sinks=m_attn_sinks, pool_w=m_pool_w, pool_scale=m_pool_scale, ssm_lam_re=m_ssm_lam_re, ssm_lam_im=m_ssm_lam_im, ssm_log_dt=m_ssm_log_dt, ssm_b_re=m_ssm_b_re, ssm_b_im=m_ssm_b_im, ssm_c_re=m_ssm_c_re, ssm_c_im=m_ssm_c_im, ssm_d=m_ssm_d, ssm_glu_w=m_ssm_glu_w, w_out=m_w_out, ln1_g=m_ln1_g, ln1_b=m_ln1_b, ffn_w_up=m_ffn_w_up, ffn_conv_w=m_ffn_conv_w, ffn_conv_b=m_ffn_conv_b, ffn_w_down=m_ffn_w_down, ln2_g=m_ln2_g, ln2_b=m_ln2_b)
    V = dict(w_in=v_w_in, attn_sinks=v_attn_sinks, pool_w=v_pool_w, pool_scale=v_pool_scale, ssm_lam_re=v_ssm_lam_re, ssm_lam_im=v_ssm_lam_im, ssm_log_dt=v_ssm_log_dt, ssm_b_re=v_ssm_b_re, ssm_b_im=v_ssm_b_im, ssm_c_re=v_ssm_c_re, ssm_c_im=v_ssm_c_im, ssm_d=v_ssm_d, ssm_glu_w=v_ssm_glu_w, w_out=v_w_out, ln1_g=v_ln1_g, ln1_b=v_ln1_b, ffn_w_up=v_ffn_w_up, ffn_conv_w=v_ffn_conv_w, ffn_conv_b=v_ffn_conv_b, ffn_w_down=v_ffn_w_down, ln2_g=v_ln2_g, ln2_b=v_ln2_b)

    depth = w_in.shape[0]
    s, d = x.shape[1], x.shape[2]
    alpha = (2 * depth) ** 0.25
    attn_w = d // 2
    kv_w = attn_w // GQA
    nkv = kv_w // HEAD_DIM
    pool_wd = d // 4
    ssm_wd = d // 4
    n_groups = ssm_wd // SSM_GROUP
    state_w = n_groups * SSM_STATE
    nt_ssm = max(1, state_w // 512)
    o_k, o_v, o_p, o_s = attn_w, attn_w + kv_w, attn_w + 2 * kv_w, attn_w + 2 * kv_w + pool_wd
    in_w = o_s + ssm_wd
    half = ffn_w_down.shape[1]
    half_pad = -(-half // 64) * 64
    ffp = 4 * 2 * half_pad
    xi, yi, ci = _place()
    me = 4 * xi + 2 * yi + ci
    c_idx = jnp.reshape(ci, (1,)).astype(jnp.int32)

    cos_t, sin_t = _rope_tables(s)

    def layer_shards(l):
        return [
            jnp.transpose(w_in[l]).astype(_WIRE), ssm_glu_w[l].astype(_WIRE), w_out[l].astype(_WIRE),
            _pad_pairs(jnp.transpose(ffn_w_up[l]).astype(_WIRE), 0, half, half_pad),
            jnp.pad(ffn_w_down[l].astype(_WIRE), ((0, half_pad - half), (0, 0))),
        ]

    (g_cw,) = _all_gather("gather_conv_w", [_pad_pairs(ffn_conv_w, 2, half, half_pad)])

    def mixer_weights(l, gathered):
        g_in, g_glu, g_out = gathered
        return dict(
            win_t=g_in.reshape(in_w, d),
            glu=jnp.transpose(g_glu, (1, 0, 2)).reshape(ssm_wd, 2 * ssm_wd),
            wout=g_out.reshape(d, d),
            cw=jnp.transpose(g_cw[:, l], (1, 0, 2)).reshape(CONV_WIDTH, 2 * ffp),
            cb=_pad_pairs(ffn_conv_b[l].reshape(N_DEV, 2 * half), 1, half, half_pad).reshape(1, 2 * ffp),
        )

    def ffn_weights(gathered):
        g_up, g_down = gathered
        return dict(wup_t=g_up.reshape(2 * ffp, d), wdown=g_down.reshape(ffp, d))

    shards = [layer_shards(l) for l in range(depth)]
    full = [None] * depth
    gathers = {}

    def begin_gather(l, part, deps):
        arrays = shards[l][:3] if part == "mixer" else shards[l][3:]
        gathers[l, part] = ("sent", _gather_begin(f"gather_{part}_weights_{l}", arrays, deps))
        return gathers[l, part][1][-1]

    def arrive_gather(l, part, after):
        gathers[l, part] = ("forwarding", _gather_arrived(f"gather_{part}_weights_{l}", gathers[l, part][1], after))
        return gathers[l, part][1][-1]

    def end_gather(l, part, after):
        stage, handle = gathers.pop((l, part))
        return (_gather_done if stage == "forwarding" else _gather_end)(f"gather_{part}_weights_{l}", handle, after)

    issued = begin_gather(0, "ffn", [begin_gather(0, "mixer", [g_cw])])
    full[0] = mixer_weights(0, end_gather(0, "mixer", g_cw))

    ssm_params = (ssm_lam_re, ssm_lam_im, ssm_log_dt, ssm_b_re, ssm_b_im)
    ab_re_all, ab_im_all, bbr_all, bbi_all = _ssm_discretise(*ssm_params)

    def ssm_maps(w):
        return jax.vmap(lambda t: _blockdiag(jnp.transpose(t, (0, 2, 1)), nt_ssm))(w).astype(_MXU)

    bdr_all, bdi_all, cdr_all, cdi_all = ssm_maps(bbr_all), ssm_maps(bbi_all), ssm_maps(ssm_c_re), ssm_maps(ssm_c_im)

    saved = []
    xf = x[0]
    xb = xf.astype(_MXU)
    for l in range(depth):
        fw = full[l]
        deps = [arrive_gather(l, "ffn", xb)] if l >= 2 else []
        if l + 1 < depth:
            issued = begin_gather(l + 1, "ffn", [begin_gather(l + 1, "mixer", [fw["win_t"], issued])])
            deps.append(issued)
        h = _mm_nt(f"in_proj_{l}", xb, fw["win_t"], deps=tuple(deps))
        q_rot, k_rot = _rope(f"rope_{l}", h, o_v, cos_t, sin_t, _MXU, ((0, o_k), (o_k, o_v)))
        k_hm = jnp.transpose(k_rot.reshape(s, nkv, HEAD_DIM), (1, 0, 2))
        v_hm = jnp.transpose(h[:, o_v:o_p].astype(_MXU).reshape(s, nkv, HEAD_DIM), (1, 0, 2))
        sinks = attn_sinks[l].reshape(nkv, GQA)
        o_attn, lse = _attn_fwd(f"attn_{l}", q_rot, k_hm, v_hm, sinks)
        pw_b = pool_w[l].astype(_MXU)
        psc = pool_scale[l].reshape(1, pool_wd)
        y_pool, pre = _pool_fwd(f"pool_{l}", h, o_p // pool_wd, pw_b, psc)
        bdr, bdi, cdr, cdi = bdr_all[l], bdi_all[l], cdr_all[l], cdi_all[l]
        dvec = ssm_d[l].reshape(1, ssm_wd)
        ar, ai = ab_re_all[l].reshape(1, state_w), ab_im_all[l].reshape(1, state_w)
        cw_ssm = ssm_wd // nt_ssm
        sr, si, ypre, yg = _ssm_fwd(f"ssm_{l}", h, o_s // cw_ssm, bdr, bdi, cdr, cdi, dvec, ar, ai)
        y_ssm, ab2 = _glu_fwd(f"glu_{l}", yg, fw["glu"])
        mix = jnp.concatenate([o_attn.astype(_MXU), y_pool, y_ssm], -1)
        a1 = _mm_nn(f"out_proj_{l}", mix, fw["wout"])
        g1, b1 = ln1_g[l].reshape(1, d), ln1_b[l].reshape(1, d)
        x1, x1b, xh1, rs1 = _ln_fwd(f"ln1_{l}", xf, a1, g1, b1, alpha)
        fw.update(ffn_weights(end_gather(l, "ffn", x1b)))
        deps = (arrive_gather(l + 1, "mixer", x1b),) if 1 <= l < depth - 1 else ()
        hu = _mm_nt(f"ffn_up_{l}", x1b, fw["wup_t"], cap=2 * half_pad, deps=deps)
        act = _conv_act_fwd(f"ffn_act_{l}", hu, fw["cw"], fw["cb"])
        f_out = _mm_nn(f"ffn_down_{l}", act, fw["wdown"], cap=256)
        g2, b2 = ln2_g[l].reshape(1, d), ln2_b[l].reshape(1, d)
        x2, x2b, xh2, rs2 = _ln_fwd(f"ln2_{l}", x1, f_out, g2, b2, alpha)
        saved.append(dict(xb=xb, h=h, q_rot=q_rot, k_hm=k_hm, v_hm=v_hm, sinks=sinks, o_attn=o_attn, lse=lse, pw_b=pw_b, psc=psc,
                          pre=pre, bdr=bdr, bdi=bdi, cdr=cdr, cdi=cdi, dvec=dvec, ar=ar, ai=ai, sr=sr, si=si, ypre=ypre, yg=yg,
                          ab2=ab2, mix=mix, g1=g1, xh1=xh1, rs1=rs1, x1b=x1b, hu=hu, act=act, g2=g2, xh2=xh2, rs2=rs2))
        xf, xb = x2, x2b
        if l + 1 < depth:
            full[l + 1] = mixer_weights(l + 1, end_gather(l + 1, "mixer", x2b))

    dy, loss_part = _loss_head("loss_head", xf, loss_target[0])
    loss = lax.psum(loss_part[0, 0], ("x", "y", "c"))

    small_handles = [None] * depth
    small_parts = [None] * depth
    outs = {}
    big_res = {k: None for k in _BIG}
    my_chip = 2 * xi + yi
    pending = []

    transposed = ("w_in", "ffn_w_up")

    def row_groups(name_, t):
        g = 2 if name_ == "ffn_w_up" else 1
        return t.reshape(t.shape[:-2] + (g, t.shape[-2] // g, t.shape[-1]))

    def as_groups(name_, t):
        return row_groups(name_, jnp.transpose(t, (0, 2, 1)) if name_ in transposed else t)

    def from_groups(name_, t):
        t = t.reshape(t.shape[0], t.shape[1] * t.shape[2], t.shape[3])
        return jnp.transpose(t, (0, 2, 1)) if name_ in transposed else t

    grouped = {name_: tuple(as_groups(name_, t[name_]) for t in (W, M, V)) for name_ in _BIG}

    def finish_exchanges(after):
        while pending:
            lay, part, names, handle = pending.pop(0)
            pairs, lands = _chips_end(f"grads_between_chips_{part}_{lay}", handle, after)
            for name_, p, ld in zip(names, pairs, lands):
                own = row_groups(name_, lax.dynamic_index_in_dim(p, my_chip, 0, keepdims=False))
                big_res[name_] = _adamw_layer(f"adamw_{name_}_{lay}", lay, own, row_groups(name_, ld), *grouped[name_],
                                              big_res[name_])

    def begin_swap(lay, part, names, grads):
        by_owner = [a.reshape((4, 2) + a.shape[1:]) for a in grads]
        return lay, part, names, _sibling_begin(f"grads_to_sibling_{part}_{lay}", by_owner, [])

    def begin_exchange(swap, after):
        lay, part, names, handle = swap
        by_owner, theirs = _sibling_end(f"grads_to_sibling_{part}_{lay}", handle, after)
        pair = [_pair_sum(f"pair_sum_{name_}_{lay}", a, b, c_idx) for name_, a, b in zip(names, by_owner, theirs)]
        finish_exchanges(after)
        handle = _chips_begin(f"grads_between_chips_{part}_{lay}", pair, [])
        pending.append((lay, part, names, handle))
        return handle[-1]

    token = None
    small_handle = None
    for l in reversed(range(depth)):
        fw, sv = full[l], saved[l]
        deps = () if token is None else (token, small_handles[l + 1][-1])
        dr2, dr2b, dg2, db2 = _ln_bwd(f"ln2_bwd_{l}", dy, sv["xh2"], sv["rs2"], sv["g2"], deps=deps)
        d_wdown = _mm_tn_acols(f"ffn_down_dw_{l}", sv["act"], dr2b, _WIRE, cap=2 * half_pad)
        dact = _mm_nt(f"ffn_down_dx_{l}", dr2b, fw["wdown"], cap=2 * half_pad)
        dhu, dcw, dcb = _conv_act_bwd(f"ffn_act_bwd_{l}", dact, sv["hu"], fw["cw"], fw["cb"])
        d_wup = _mm(f"ffn_up_dw_{l}", dhu, sv["x1b"], TN, (N_DEV, 1),
                    pl.BlockSpec((None, s, 2 * half_pad), lambda j, kk: (j // 4, 0, j % 4)),
                    _resident((s, d), lambda j, kk: (0, 0)),
                    pl.BlockSpec((2 * half_pad, d), lambda j, kk: (j, 0)), (2 * ffp, d), _WIRE)
        swap = begin_swap(l, "ffn", ("ffn_w_up", "ffn_w_down"),
                          [d_wup.reshape(N_DEV, 2 * half_pad, d), d_wdown.reshape(N_DEV, half_pad, d)])
        dy1 = _mm_split_k(f"ffn_up_dx_{l}", dhu, fw["wup_t"], dr2, alpha, deps=(swap[3][-1],))
        token = begin_exchange(swap, dy1)
        dr1, dr1b, dg1, db1 = _ln_bwd(f"ln1_bwd_{l}", dy1, sv["xh1"], sv["rs1"], sv["g1"], deps=(token,))
        d_wout = _mm_tn_acols(f"out_proj_dw_{l}", sv["mix"], dr1b, _WIRE, cap=d // N_DEV)
        dmix = _mm_nt(f"out_proj_dx_{l}", dr1b, fw["wout"])
        dq_rot, dk_hm, dv_hm, dsk = _attn_bwd(f"attn_bwd_{l}", sv["q_rot"], sv["k_hm"], sv["v_hm"], sv["o_attn"], dmix,
                                             sv["lse"], sv["sinks"])
        dqk = jnp.concatenate([dq_rot, jnp.transpose(dk_hm, (1, 0, 2)).reshape(s, kv_w)], -1)
        dhq, dhk = _rope(f"rope_bwd_{l}", dqk, o_v, cos_t, -sin_t, _MXU, ((0, o_k), (o_k, o_v)))
        dhv = jnp.transpose(dv_hm, (1, 0, 2)).reshape(s, kv_w).astype(_MXU)
        dhp, dpw, dpsc = _pool_bwd(f"pool_bwd_{l}", dmix, attn_w // pool_wd, sv["pre"], sv["pw_b"], sv["psc"])
        dab2, dyg = _glu_bwd(f"glu_bwd_{l}", dmix, (attn_w + pool_wd) // ssm_wd, sv["ab2"], fw["glu"])
        d_glu = _mm_tn_bcols(f"glu_dw_{l}", sv["yg"], dab2, _WIRE)
        cw_ssm = ssm_wd // nt_ssm
        dhs, dd, dcdr, dcdi, dbdr, dbdi, dar, dai = _ssm_bwd(
            f"ssm_bwd_{l}", dyg, sv["ypre"], sv["h"], o_s // cw_ssm, sv["sr"], sv["si"], sv["bdr"], sv["bdi"], sv["cdr"],
            sv["cdi"], sv["dvec"], sv["ar"], sv["ai"])
        dh = jnp.concatenate([dhq, dhk, dhv, dhp, dhs], -1)
        d_win = _mm_tn_acols(f"in_proj_dw_{l}", dh, sv["xb"], _WIRE)
        swap = begin_swap(l, "mixer", ("w_in", "ssm_glu_w", "w_out"),
                          [d_win.reshape(N_DEV, in_w // N_DEV, d),
                           jnp.transpose(d_glu.reshape(ssm_wd, N_DEV, 2 * ssm_wd // N_DEV), (1, 0, 2)),
                           d_wout.reshape(N_DEV, d // N_DEV, d)])
        dy = _mm_nn(f"in_proj_dx_{l}", dh, fw["win_t"], add=dr1, add_scale=alpha, deps=(swap[3][-1],))

        raw = dict(attn_sinks=dsk, pool_w=dpw, pool_scale=dpsc, ssm_lam_re=dar, ssm_lam_im=dai,
                   ssm_b_re=_diagblocks(dbdr, n_groups, SSM_GROUP, SSM_STATE),
                   ssm_b_im=_diagblocks(dbdi, n_groups, SSM_GROUP, SSM_STATE),
                   ssm_c_re=_diagblocks(dcdr, n_groups, SSM_STATE, SSM_GROUP),
                   ssm_c_im=_diagblocks(dcdi, n_groups, SSM_STATE, SSM_GROUP), ssm_d=dd, ln1_g=dg1, ln1_b=db1,
                   ffn_conv_b=_unpad_pairs(dcb.reshape(N_DEV, 2 * half_pad), 1, half, half_pad), ln2_g=dg2, ln2_b=db2,
                   ffn_conv_w=_unpad_pairs(dcw.reshape(CONV_WIDTH, N_DEV, 2 * half_pad), 2, half, half_pad))
        raw_shapes = {k: raw[k].shape for k in _RAW}
        token = begin_exchange(swap, dy)
        small_handles[l] = _gather_begin(f"gather_small_grads_{l}", [_pack([raw[k] for k in _RAW])], [token])
        if l + 1 < depth:
            small_parts[l + 1] = _gather_end(f"gather_small_grads_{l + 1}", small_handles[l + 1], dy)[0]

    small_parts[0] = _gather_end("gather_small_grads_0", small_handles[0], token)[0]
    rows_l = small_parts[0].shape[1]
    summed = _sum_parts("sum_small_grads", jnp.concatenate(small_parts, 1)).reshape(depth, rows_l, V7X_LANES)
    g_small = dict(zip(_RAW, _unpack(summed, [raw_shapes[k] for k in _RAW])))
    swap_last = lambda t: jnp.transpose(t, (0, 1, 3, 2))
    _, vjp = jax.vjp(_ssm_discretise, *ssm_params)
    dlr, dli, dldt, dbr, dbi = vjp((g_small["ssm_lam_re"].reshape(depth, n_groups, SSM_STATE),
                                    g_small["ssm_lam_im"].reshape(depth, n_groups, SSM_STATE),
                                    swap_last(g_small["ssm_b_re"]), swap_last(g_small["ssm_b_im"])))
    g_small.update(ssm_lam_re=dlr, ssm_lam_im=dli, ssm_log_dt=dldt, ssm_b_re=dbr, ssm_b_im=dbi,
                   ssm_c_re=swap_last(g_small["ssm_c_re"]), ssm_c_im=swap_last(g_small["ssm_c_im"]),
                   ffn_conv_w=lax.dynamic_index_in_dim(g_small["ffn_conv_w"], me, axis=2, keepdims=False))
    for k in _SMALL + ("ffn_conv_w",):
        shp = W[k].shape
        two_d = (math.prod(shp[:-1]), shp[-1])
        res = _adamw_2d(f"adamw_{k}", g_small[k].reshape(two_d), *(t[k].reshape(two_d) for t in (W, M, V)))
        outs[k] = (g_small[k].reshape(shp),) + tuple(a.reshape(shp) for a in res)

    finish_exchanges(outs["ln2_b"][1])
    for name_ in _BIG:
        outs[name_] = tuple(from_groups(name_, t) for t in big_res[name_])

    grad_x = dy[None]
    result = [loss, grad_x]
    for i in range(4):
        result += [outs[k][i] for k in _ORDER]
    return tuple(result)
```

```python
import functools
import math

import jax
import jax.numpy as jnp
from jax import lax
from jax.experimental import pallas as pl
from jax.experimental.pallas import tpu as pltpu

F32 = jnp.float32
_MXU = jnp.bfloat16
_WIRE = jnp.bfloat16

HEAD_DIM = 64
GQA = 4
ATTN_BLOCK = 128
ROPE_THETA = 10000.0
POOL_WINDOWS = (2, 4, 8, 16)
SSM_GROUP = 16
SSM_STATE = 64
CONV_WIDTH = 3
LN_EPS = 1e-5
ADAM_LR, ADAM_B1, ADAM_B2, ADAM_EPS, ADAM_WD, ADAM_STEP = 0.001, 0.9, 0.999, 1e-08, 0.01, 10

N_DEV = 8
V7X_LANES = 128
V7X_VMEM_LIMIT = 56 * 1024 * 1024
SCAN_T = 64
SCAN_LANES = 256
MESH = pl.DeviceIdType.MESH
ANY = pl.BlockSpec(memory_space=pl.ANY)


def _cp():
    return pltpu.CompilerParams(vmem_limit_bytes=V7X_VMEM_LIMIT)


def _resident(block, index_map):
    return pl.BlockSpec(block, index_map, pipeline_mode=pl.Buffered(1))


def _sds(shape, dtype):
    return jax.ShapeDtypeStruct(tuple(shape), dtype)


def _mm(name, a, b, dims, grid, a_spec, b_spec, o_spec, out_shape, out_dtype, add=None, add_spec=None, add_scale=1.0, deps=()):
    nk = grid[1]
    n_in = 2 + (add is not None) + len(deps)
    oblk = tuple(d for d in o_spec.block_shape if d is not None)
    scratch = nk > 1 and out_dtype != F32

    def body(*refs):
        a_ref, b_ref = refs[:2]
        add_ref = None if add is None else refs[2]
        o_ref = refs[n_in]
        acc_ref = refs[-1] if scratch else None

        def finish(r):
            if add_ref is not None:
                r = r + add_scale * add_ref[...]
            o_ref[...] = r.astype(o_ref.dtype)

        part = lax.dot_general(a_ref[...], b_ref[...], (dims, ((), ())), preferred_element_type=F32)
        if nk == 1:
            finish(part)
        elif not scratch:
            k = pl.program_id(1)

            @pl.when(k == 0)
            def _():
                o_ref[...] = part

            @pl.when(k > 0)
            def _():
                o_ref[...] += part

            if add_ref is not None:
                @pl.when(k == nk - 1)
                def _():
                    o_ref[...] += add_scale * add_ref[...]
        else:
            k = pl.program_id(1)

            @pl.when(k == 0)
            def _():
                acc_ref[...] = part

            @pl.when(k > 0)
            def _():
                acc_ref[...] += part

            @pl.when(k == nk - 1)
            def _():
                finish(acc_ref[...])

    ins = [a, b] + ([] if add is None else [add]) + list(deps)
    in_specs = [a_spec, b_spec] + ([] if add is None else [add_spec]) + [ANY] * len(deps)
    return pl.pallas_call(
        body, grid=grid, in_specs=in_specs, out_specs=o_spec, out_shape=_sds(out_shape, out_dtype),
        scratch_shapes=[pltpu.VMEM(oblk, F32)] if scratch else [], compiler_params=_cp(), name=name,
    )(*ins)


NN = ((1,), (0,))
NT = ((1,), (1,))
TN = ((0,), (0,))


def _mm_split_k(name, a2, b, add, add_scale, deps=()):
    _, m, f = a2.shape
    n = b.shape[1]
    tm, tn = m // 2, _col_tile(n, 256)

    def body(a_ref, b_ref, add_ref, *rest):
        o_ref = rest[-1]
        o_ref[...] = (jnp.dot(a_ref[0], b_ref[:f, :], preferred_element_type=F32)
                      + jnp.dot(a_ref[1], b_ref[f:, :], preferred_element_type=F32) + add_scale * add_ref[...])

    tile = pl.BlockSpec((tm, tn), lambda i, j: (i, j))
    return pl.pallas_call(
        body, grid=(m // tm, n // tn),
        in_specs=[_resident((2, tm, f), lambda i, j: (0, i, 0)), pl.BlockSpec((2 * f, tn), lambda i, j: (0, j)), tile]
        + [ANY] * len(deps),
        out_specs=tile, out_shape=_sds((m, n), F32), compiler_params=_cp(), name=name,
    )(a2, b, add, *deps)


def _col_tile(n, cap=512):
    if n % V7X_LANES:
        return n
    t = min(cap, n)
    t -= t % V7X_LANES
    while n % t:
        t -= V7X_LANES
    return t


def _mm_nn(name, a, b, out_dtype=F32, cap=512, add=None, add_scale=1.0, deps=()):
    m, k = a.shape
    n = b.shape[1]
    tn = _col_tile(n, cap)
    o_spec = pl.BlockSpec((m, tn), lambda j, kk: (0, j))
    return _mm(name, a, b, NN, (n // tn, 1), _resident((m, k), lambda j, kk: (0, 0)),
               pl.BlockSpec((k, tn), lambda j, kk: (0, j)), o_spec, (m, n), out_dtype,
               add=add, add_spec=None if add is None else o_spec, add_scale=add_scale, deps=deps)


def _mm_nt(name, a, b, out_dtype=F32, add=None, add_scale=1.0, cap=512, deps=()):
    m, k = a.shape
    n = b.shape[0]
    tn = _col_tile(n, cap)
    o_spec = pl.BlockSpec((m, tn), lambda j, kk: (0, j))
    return _mm(name, a, b, NT, (n // tn, 1), _resident((m, k), lambda j, kk: (0, 0)),
               pl.BlockSpec((tn, k), lambda j, kk: (j, 0)), o_spec, (m, n), out_dtype,
               add=add, add_spec=None if add is None else o_spec, add_scale=add_scale, deps=deps)


def _mm_tn_bcols(name, a, b, out_dtype, cap=512):
    s, k = a.shape
    n = b.shape[1]
    tn = _col_tile(n, cap)
    return _mm(name, a, b, TN, (n // tn, 1), _resident((s, k), lambda j, kk: (0, 0)),
               pl.BlockSpec((s, tn), lambda j, kk: (0, j)), pl.BlockSpec((k, tn), lambda j, kk: (0, j)), (k, n), out_dtype)


def _mm_tn_acols(name, a, b, out_dtype, cap=512):
    s, k = a.shape
    n = b.shape[1]
    tk = _col_tile(k, cap)
    return _mm(name, a, b, TN, (k // tk, 1), pl.BlockSpec((s, tk), lambda i, kk: (0, i)),
               _resident((s, n), lambda i, kk: (0, 0)), pl.BlockSpec((tk, n), lambda i, kk: (i, 0)), (k, n), out_dtype)


def _ln_fwd(name, x, a, g, b, alpha):
    s, d = x.shape
    tr = min(256, s)

    def body(x_ref, a_ref, g_ref, b_ref, y_ref, yb_ref, xh_ref, rs_ref):
        r = alpha * x_ref[...] + a_ref[...]
        mu = jnp.mean(r, -1, keepdims=True)
        c = r - mu
        var = jnp.mean(c * c, -1, keepdims=True)
        rstd = lax.rsqrt(var + LN_EPS)
        xh = c * rstd
        y = xh * g_ref[...] + b_ref[...]
        y_ref[...] = y
        yb_ref[...] = y.astype(_MXU)
        xh_ref[...] = xh
        rs_ref[...] = rstd

    row = pl.BlockSpec((tr, d), lambda i: (i, 0))
    vec = pl.BlockSpec((1, d), lambda i: (0, 0))
    return pl.pallas_call(
        body, grid=(s // tr,), in_specs=[row, row, vec, vec],
        out_specs=[row, row, row, pl.BlockSpec((tr, 1), lambda i: (i, 0))],
        out_shape=[_sds((s, d), F32), _sds((s, d), _MXU), _sds((s, d), F32), _sds((s, 1), F32)],
        compiler_params=_cp(), name=name,
    )(x, a, g, b)


def _ln_bwd(name, dy, xh, rstd, g, deps=()):
    s, d = dy.shape
    tr = min(256, s)
    nd = len(deps)

    def body(dy_ref, xh_ref, rs_ref, g_ref, *rest):
        dr_ref, drb_ref, dg_ref, db_ref = rest[nd:]
        i = pl.program_id(0)
        dy_ = dy_ref[...]
        xh_ = xh_ref[...]
        dxh = dy_ * g_ref[...]
        m1 = jnp.mean(dxh, -1, keepdims=True)
        m2 = jnp.mean(dxh * xh_, -1, keepdims=True)
        dr = rs_ref[...] * (dxh - m1 - xh_ * m2)
        dr_ref[...] = dr
        drb_ref[...] = dr.astype(_MXU)
        pg = jnp.sum(dy_ * xh_, 0, keepdims=True)
        pb = jnp.sum(dy_, 0, keepdims=True)

        @pl.when(i == 0)
        def _():
            dg_ref[...] = pg
            db_ref[...] = pb

        @pl.when(i > 0)
        def _():
            dg_ref[...] += pg
            db_ref[...] += pb

    row = pl.BlockSpec((tr, d), lambda i: (i, 0))
    vec = pl.BlockSpec((1, d), lambda i: (0, 0))
    return pl.pallas_call(
        body, grid=(s // tr,), in_specs=[row, row, pl.BlockSpec((tr, 1), lambda i: (i, 0)), vec] + [ANY] * nd,
        out_specs=[row, row, vec, vec],
        out_shape=[_sds((s, d), F32), _sds((s, d), _MXU), _sds((1, d), F32), _sds((1, d), F32)],
        compiler_params=_cp(), name=name,
    )(dy, xh, rstd, g, *deps)


def _loss_head(name, y, target):
    s, d = y.shape
    tr = min(256, s)

    def body(y_ref, t_ref, dy_ref, l_ref):
        i = pl.program_id(0)
        e = y_ref[...] - t_ref[...]
        dy_ref[...] = e * (1.0 / d)
        part = 0.5 * jnp.sum(jnp.mean(e * e, -1, keepdims=True), 0, keepdims=True)

        @pl.when(i == 0)
        def _():
            l_ref[...] = part

        @pl.when(i > 0)
        def _():
            l_ref[...] += part

    row = pl.BlockSpec((tr, d), lambda i: (i, 0))
    return pl.pallas_call(
        body, grid=(s // tr,), in_specs=[row, row], out_specs=[row, pl.BlockSpec((1, 1), lambda i: (0, 0))],
        out_shape=[_sds((s, d), F32), _sds((1, 1), F32)], compiler_params=_cp(), name=name,
    )(y, target)


def _rope(name, t, width, cos, sin, out_dtype, splits):
    s = t.shape[0]
    tr = min(256, s)
    assert width % V7X_LANES == 0

    def body(t_ref, c_ref, s_ref, *o_refs):
        lane = lax.broadcasted_iota(jnp.int32, (tr, V7X_LANES), 1)
        first = (lane % HEAD_DIM) < (HEAD_DIM // 2)
        cs, sn = c_ref[...], s_ref[...]
        for (lo, hi), o_ref in zip(splits, o_refs):
            for c0 in range(lo, hi, V7X_LANES):
                v = t_ref[:, c0:c0 + V7X_LANES].astype(F32)
                partner = jnp.where(first, pltpu.roll(v, V7X_LANES - HEAD_DIM // 2, 1), pltpu.roll(v, HEAD_DIM // 2, 1))
                o_ref[:, c0 - lo:c0 - lo + V7X_LANES] = (v * cs + partner * sn).astype(o_ref.dtype)

    tab = pl.BlockSpec((tr, V7X_LANES), lambda i: (i, 0))
    return pl.pallas_call(
        body, grid=(s // tr,), in_specs=[pl.BlockSpec((tr, width), lambda i: (i, 0)), tab, tab],
        out_specs=[pl.BlockSpec((tr, hi - lo), lambda i: (i, 0)) for lo, hi in splits],
        out_shape=[_sds((s, hi - lo), out_dtype) for lo, hi in splits], compiler_params=_cp(), name=name,
    )(t, cos, sin)


def _attn_masks():
    i = lax.broadcasted_iota(jnp.int32, (GQA * ATTN_BLOCK, 2 * ATTN_BLOCK), 0) % ATTN_BLOCK
    j = lax.broadcasted_iota(jnp.int32, (GQA * ATTN_BLOCK, 2 * ATTN_BLOCK), 1)
    cur_ok = jnp.logical_and(j >= ATTN_BLOCK, j - ATTN_BLOCK <= i)
    prev_ok = jnp.logical_and(j < ATTN_BLOCK, j > i)
    return cur_ok, prev_ok


def _attn_scores(q4, kcat, n, cur_ok, prev_ok):
    sc = lax.dot_general(q4, kcat, (NT, ((), ())), preferred_element_type=F32) * (HEAD_DIM ** -0.5)
    return jnp.where(jnp.logical_or(cur_ok, jnp.logical_and(prev_ok, n > 0)), sc, -1e30)


def _stack_heads(ref, rows):
    return jnp.concatenate([ref[rows, g * HEAD_DIM:(g + 1) * HEAD_DIM] for g in range(GQA)], 0)


def _per_head_column(values):
    r = lax.broadcasted_iota(jnp.int32, (GQA * ATTN_BLOCK, 1), 0) // ATTN_BLOCK
    col = jnp.zeros((GQA * ATTN_BLOCK, 1), F32)
    for g, val in enumerate(values):
        col = jnp.where(r == g, val, col)
    return col


def _attn_fwd(name, q, k, v, sinks):
    s = q.shape[0]
    nkv = k.shape[0]
    gw = GQA * HEAD_DIM
    nb = s // ATTN_BLOCK

    def body(sk_ref, q_ref, k_ref, v_ref, o_ref, lse_ref):
        h = pl.program_id(0)
        cur_ok, prev_ok = _attn_masks()
        sink = _per_head_column([sk_ref[h, g] for g in range(GQA)])

        def blk(n, carry):
            rows = pl.ds(pl.multiple_of(n * ATTN_BLOCK, ATTN_BLOCK), ATTN_BLOCK)
            prows = pl.ds(pl.multiple_of(jnp.maximum(n - 1, 0) * ATTN_BLOCK, ATTN_BLOCK), ATTN_BLOCK)
            kcat = jnp.concatenate([k_ref[prows, :], k_ref[rows, :]], 0)
            vcat = jnp.concatenate([v_ref[prows, :], v_ref[rows, :]], 0)
            sc = _attn_scores(_stack_heads(q_ref, rows), kcat, n, cur_ok, prev_ok)
            m = jnp.maximum(sc.max(-1, keepdims=True), sink)
            p = jnp.exp(sc - m)
            den = p.sum(-1, keepdims=True) + jnp.exp(sink - m)
            o = jnp.dot((p / den).astype(_MXU), vcat, preferred_element_type=F32)
            lse = m + jnp.log(den)
            for g in range(GQA):
                mine = slice(g * ATTN_BLOCK, (g + 1) * ATTN_BLOCK)
                o_ref[rows, g * HEAD_DIM:(g + 1) * HEAD_DIM] = o[mine, :]
                lse_ref[rows, g:g + 1] = lse[mine, :]
            return carry

        lax.fori_loop(0, nb, blk, 0)

    kv_spec = pl.BlockSpec((None, s, HEAD_DIM), lambda h: (h, 0, 0))
    return pl.pallas_call(
        body, grid=(nkv,),
        in_specs=[pl.BlockSpec(memory_space=pltpu.SMEM), pl.BlockSpec((s, gw), lambda h: (0, h)), kv_spec, kv_spec],
        out_specs=[pl.BlockSpec((s, gw), lambda h: (0, h)), pl.BlockSpec((None, s, GQA), lambda h: (h, 0, 0))],
        out_shape=[_sds((s, nkv * gw), F32), _sds((nkv, s, GQA), F32)], compiler_params=_cp(), name=name,
    )(sinks, q, k, v)


def _attn_bwd(name, q, k, v, o, dmix, lse, sinks):
    s = q.shape[0]
    nkv = k.shape[0]
    gw = GQA * HEAD_DIM
    nb = s // ATTN_BLOCK
    scale = HEAD_DIM ** -0.5

    def body(sk_ref, q_ref, k_ref, v_ref, o_ref, do_ref, lse_ref, dq_ref, dk_ref, dv_ref, dsk_ref):
        h = pl.program_id(0)
        cur_ok, prev_ok = _attn_masks()
        dk_ref[...] = jnp.zeros_like(dk_ref)
        dv_ref[...] = jnp.zeros_like(dv_ref)

        sink = _per_head_column([sk_ref[h, g] for g in range(GQA)])

        def blk(n, acc):
            rows = pl.ds(pl.multiple_of(n * ATTN_BLOCK, ATTN_BLOCK), ATTN_BLOCK)
            prows = pl.ds(pl.multiple_of(jnp.maximum(n - 1, 0) * ATTN_BLOCK, ATTN_BLOCK), ATTN_BLOCK)
            kcat = jnp.concatenate([k_ref[prows, :], k_ref[rows, :]], 0)
            vcat = jnp.concatenate([v_ref[prows, :], v_ref[rows, :]], 0)
            q4 = _stack_heads(q_ref, rows)
            do4 = _stack_heads(do_ref, rows)
            delta = jnp.sum(do4 * _stack_heads(o_ref, rows), -1, keepdims=True)
            dob = do4.astype(_MXU)
            lse = jnp.concatenate([lse_ref[rows, g:g + 1] for g in range(GQA)], 0)
            p = jnp.exp(_attn_scores(q4, kcat, n, cur_ok, prev_ok) - lse)
            dp = lax.dot_general(dob, vcat, (NT, ((), ())), preferred_element_type=F32)
            ds = (p * (dp - delta) * scale).astype(_MXU)
            dq = jnp.dot(ds, kcat, preferred_element_type=F32)
            for g in range(GQA):
                dq_ref[rows, g * HEAD_DIM:(g + 1) * HEAD_DIM] = dq[g * ATTN_BLOCK:(g + 1) * ATTN_BLOCK, :]
            dk = lax.dot_general(ds, q4, (TN, ((), ())), preferred_element_type=F32)
            dv = lax.dot_general(p.astype(_MXU), dob, (TN, ((), ())), preferred_element_type=F32)
            dk_ref[prows, :] += dk[:ATTN_BLOCK, :]
            dv_ref[prows, :] += dv[:ATTN_BLOCK, :]
            dk_ref[rows, :] += dk[ATTN_BLOCK:, :]
            dv_ref[rows, :] += dv[ATTN_BLOCK:, :]
            return acc - jnp.exp(sink - lse) * delta

        acc = lax.fori_loop(0, nb, blk, jnp.zeros((GQA * ATTN_BLOCK, 1), F32))
        for g in range(GQA):
            dsk_ref[:, g:g + 1] = jnp.sum(acc[g * ATTN_BLOCK:(g + 1) * ATTN_BLOCK, :], 0, keepdims=True)

    kv_spec = pl.BlockSpec((None, s, HEAD_DIM), lambda h: (h, 0, 0))
    qcols = pl.BlockSpec((s, gw), lambda h: (0, h))
    return pl.pallas_call(
        body, grid=(nkv,),
        in_specs=[pl.BlockSpec(memory_space=pltpu.SMEM), qcols, kv_spec, kv_spec, qcols, qcols,
                  pl.BlockSpec((None, s, GQA), lambda h: (h, 0, 0))],
        out_specs=[qcols, kv_spec, kv_spec, pl.BlockSpec((None, 1, GQA), lambda h: (h, 0, 0))],
        out_shape=[_sds((s, nkv * gw), F32), _sds((nkv, s, HEAD_DIM), F32), _sds((nkv, s, HEAD_DIM), F32),
                   _sds((nkv, 1, GQA), F32)],
        compiler_params=_cp(), name=name,
    )(sinks, q, k, v, o, dmix, lse)


def _shift_down(a, k, t):
    return jnp.where(t >= k, pltpu.roll(a, k, 0), 0.0)


def _shift_up(a, k, t):
    n = a.shape[0]
    return jnp.where(t < n - k, pltpu.roll(a, n - k, 0), 0.0)


def _pool_fwd(name, h, col_block, pool_w, pool_scale):
    s = h.shape[0]
    ng, pg = pool_w.shape[0], pool_w.shape[1]
    pw_ = ng * pg

    def body(u_ref, w_ref, sc_ref, y_ref, pre_ref):
        t = lax.broadcasted_iota(jnp.int32, (s, pg), 0)
        for gi, win in enumerate(POOL_WINDOWS):
            cols = slice(gi * pg, (gi + 1) * pg)
            u = u_ref[:, cols]
            a = u
            k = 1
            while k < win:
                a = a + _shift_down(a, k, t)
                k *= 2
            div = jnp.minimum(t + 1, win).astype(F32)
            pre = (a / div - u).astype(_MXU)
            pre_ref[:, cols] = pre
            out = jnp.dot(pre, w_ref[gi], preferred_element_type=F32)
            y_ref[:, cols] = (out * sc_ref[:, cols]).astype(y_ref.dtype)

    blk = pl.BlockSpec((s, pw_), lambda i: (0, 0))
    return pl.pallas_call(
        body, grid=(1,),
        in_specs=[pl.BlockSpec((s, pw_), lambda i: (0, col_block)), pl.BlockSpec((ng, pg, pg), lambda i: (0, 0, 0)),
                  pl.BlockSpec((1, pw_), lambda i: (0, 0))],
        out_specs=[blk, blk], out_shape=[_sds((s, pw_), _MXU), _sds((s, pw_), _MXU)], compiler_params=_cp(), name=name,
    )(h, pool_w, pool_scale)


def _pool_bwd(name, dmix, col_block, pre, pool_w, pool_scale):
    s = pre.shape[0]
    ng, pg = pool_w.shape[0], pool_w.shape[1]
    pw_ = ng * pg

    def body(dy_ref, pre_ref, w_ref, sc_ref, du_ref, dw_ref, dsc_ref):
        t = lax.broadcasted_iota(jnp.int32, (s, pg), 0)
        for gi, win in enumerate(POOL_WINDOWS):
            cols = slice(gi * pg, (gi + 1) * pg)
            pre_g = pre_ref[:, cols]
            dy = dy_ref[:, cols]
            out = jnp.dot(pre_g, w_ref[gi], preferred_element_type=F32)
            dsc_ref[:, cols] = jnp.sum(dy * out, 0, keepdims=True)
            dout = (dy * sc_ref[:, cols]).astype(_MXU)
            dw_ref[gi] = lax.dot_general(pre_g, dout, (TN, ((), ())), preferred_element_type=F32)
            dpre = lax.dot_general(dout, w_ref[gi], (NT, ((), ())), preferred_element_type=F32)
            div = jnp.minimum(t + 1, win).astype(F32)
            a = dpre / div
            k = 1
            while k < win:
                a = a + _shift_up(a, k, t)
                k *= 2
            du_ref[:, cols] = (a - dpre).astype(du_ref.dtype)

    blk = pl.BlockSpec((s, pw_), lambda i: (0, 0))
    wspec = pl.BlockSpec((ng, pg, pg), lambda i: (0, 0, 0))
    vec = pl.BlockSpec((1, pw_), lambda i: (0, 0))
    return pl.pallas_call(
        body, grid=(1,), in_specs=[pl.BlockSpec((s, pw_), lambda i: (0, col_block)), blk, wspec, vec],
        out_specs=[blk, wspec, vec], out_shape=[_sds((s, pw_), _MXU), _sds((ng, pg, pg), F32), _sds((1, pw_), F32)],
        compiler_params=_cp(), name=name,
    )(dmix, pre, pool_w, pool_scale)


def _scan_chunks(xr_ref, xi_ref, sr_ref, si_ref, ar, ai, reverse):
    n, c = xr_ref.shape
    tt = min(SCAN_T, n)
    lw = min(SCAN_LANES, c)
    nchunk, ngroup = n // tt, tt // 8
    t8 = lax.broadcasted_iota(jnp.int32, (tt, lw), 0) % 8

    for l0 in range(0, c, lw):
        lanes = slice(l0, l0 + lw)
        a_r, a_i = ar[:, lanes], ai[:, lanes]

        def within8(vr, vi, a_r=a_r, a_i=a_i):
            rows = vr.shape[0]
            tq = t8[:rows, :]
            pr, pi = a_r, a_i
            k = 1
            while k < 8:
                if reverse:
                    hr = jnp.where(tq < 8 - k, pltpu.roll(vr, rows - k, 0), 0.0)
                    hi = jnp.where(tq < 8 - k, pltpu.roll(vi, rows - k, 0), 0.0)
                else:
                    hr = jnp.where(tq >= k, pltpu.roll(vr, k, 0), 0.0)
                    hi = jnp.where(tq >= k, pltpu.roll(vi, k, 0), 0.0)
                vr, vi = vr + pr * hr - pi * hi, vi + pr * hi + pi * hr
                pr, pi = pr * pr - pi * pi, 2.0 * pr * pi
                k *= 2
            return vr, vi

        at_edge = t8[:8, :] == (7 if reverse else 0)
        pw_r, pw_i = within8(jnp.where(at_edge, a_r, 0.0), jnp.where(at_edge, a_i, 0.0))
        last = 0 if reverse else 7

        def body(i, carry, lanes=lanes, within8=within8, pw_r=pw_r, pw_i=pw_i):
            cr, ci = carry
            ch = nchunk - 1 - i if reverse else i
            rows = pl.ds(pl.multiple_of(ch * tt, tt), tt)
            vr, vi = within8(xr_ref[rows, lanes], xi_ref[rows, lanes])
            out_r, out_i = [None] * ngroup, [None] * ngroup
            for g in (reversed(range(ngroup)) if reverse else range(ngroup)):
                br = vr[8 * g:8 * g + 8, :] + pw_r * cr - pw_i * ci
                bi = vi[8 * g:8 * g + 8, :] + pw_r * ci + pw_i * cr
                out_r[g], out_i[g] = br, bi
                cr, ci = br[last:last + 1, :], bi[last:last + 1, :]
            sr_ref[rows, lanes] = jnp.concatenate(out_r, 0)
            si_ref[rows, lanes] = jnp.concatenate(out_i, 0)
            return cr, ci

        lax.fori_loop(0, nchunk, body, (jnp.zeros((1, lw), F32), jnp.zeros((1, lw), F32)))


_GELU_K = math.sqrt(2.0 / math.pi)


def _gelu_grad(y):
    inner = _GELU_K * (y + 0.044715 * y * y * y)
    th = jnp.tanh(inner)
    return 0.5 * (1.0 + th) + 0.5 * y * (1.0 - th * th) * _GELU_K * (1.0 + 3.0 * 0.044715 * y * y)


def _ssm_fwd(name, h, u_block0, bdr, bdi, cdr, cdi, dvec, ar, ai):
    s = h.shape[0]
    nt, cw, lw = bdr.shape
    rc = min(256, s)

    def body(u_ref, bdr_ref, bdi_ref, cdr_ref, cdi_ref, d_ref, ar_ref, ai_ref, sr_ref, si_ref, y_ref, yg_ref):
        def mm_in(c, _):
            rows = pl.ds(pl.multiple_of(c * rc, rc), rc)
            ub = u_ref[rows, :].astype(_MXU)
            sr_ref[rows, :] = jnp.dot(ub, bdr_ref[...], preferred_element_type=F32)
            si_ref[rows, :] = jnp.dot(ub, bdi_ref[...], preferred_element_type=F32)
            return 0

        lax.fori_loop(0, s // rc, mm_in, 0)
        _scan_chunks(sr_ref, si_ref, sr_ref, si_ref, ar_ref[...], ai_ref[...], reverse=False)

        def mm_out(c, _):
            rows = pl.ds(pl.multiple_of(c * rc, rc), rc)
            y = (jnp.dot(sr_ref[rows, :].astype(_MXU), cdr_ref[...], preferred_element_type=F32)
                 - jnp.dot(si_ref[rows, :].astype(_MXU), cdi_ref[...], preferred_element_type=F32)
                 + d_ref[...] * u_ref[rows, :])
            y_ref[rows, :] = y
            yg_ref[rows, :] = jax.nn.gelu(y).astype(yg_ref.dtype)
            return 0

        lax.fori_loop(0, s // rc, mm_out, 0)

    st = pl.BlockSpec((s, lw), lambda j: (0, j))
    ch = pl.BlockSpec((s, cw), lambda j: (0, j))
    bspec = pl.BlockSpec((None, cw, lw), lambda j: (j, 0, 0))
    cspec = pl.BlockSpec((None, lw, cw), lambda j: (j, 0, 0))
    return pl.pallas_call(
        body, grid=(nt,),
        in_specs=[pl.BlockSpec((s, cw), lambda j: (0, u_block0 + j)), bspec, bspec, cspec, cspec,
                  pl.BlockSpec((1, cw), lambda j: (0, j)), pl.BlockSpec((1, lw), lambda j: (0, j)),
                  pl.BlockSpec((1, lw), lambda j: (0, j))],
        out_specs=[st, st, ch, ch],
        out_shape=[_sds((s, nt * lw), F32), _sds((s, nt * lw), F32), _sds((s, nt * cw), F32), _sds((s, nt * cw), _MXU)],
        compiler_params=_cp(), name=name,
    )(h, bdr, bdi, cdr, cdi, dvec, ar, ai)


def _ssm_bwd(name, dyg, ypre, h, u_block0, sr, si, bdr, bdi, cdr, cdi, dvec, ar, ai):
    s = h.shape[0]
    nt, cw, lw = bdr.shape
    rc = min(256, s)

    def body(dyg_ref, yp_ref, u_ref, sr_ref, si_ref, bdr_ref, bdi_ref, cdr_ref, cdi_ref, d_ref, ar_ref, ai_ref,
             du_ref, dd_ref, dcr_ref, dci_ref, dbr_ref, dbi_ref, dar_ref, dai_ref, lr_scr, li_scr, dy_scr):
        for ref in (dd_ref, dcr_ref, dci_ref, dbr_ref, dbi_ref, dar_ref, dai_ref):
            ref[...] = jnp.zeros_like(ref)

        def p1(c, _):
            rows = pl.ds(pl.multiple_of(c * rc, rc), rc)
            dy = dyg_ref[rows, :] * _gelu_grad(yp_ref[rows, :])
            dy_scr[rows, :] = dy
            dd_ref[...] += jnp.sum(dy * u_ref[rows, :], 0, keepdims=True)
            dyb = dy.astype(_MXU)
            lr_scr[rows, :] = lax.dot_general(dyb, cdr_ref[...], (NT, ((), ())), preferred_element_type=F32)
            li_scr[rows, :] = -lax.dot_general(dyb, cdi_ref[...], (NT, ((), ())), preferred_element_type=F32)
            dcr_ref[...] += lax.dot_general(sr_ref[rows, :].astype(_MXU), dyb, (TN, ((), ())), preferred_element_type=F32)
            dci_ref[...] -= lax.dot_general(si_ref[rows, :].astype(_MXU), dyb, (TN, ((), ())), preferred_element_type=F32)
            return 0

        lax.fori_loop(0, s // rc, p1, 0)
        _scan_chunks(lr_scr, li_scr, lr_scr, li_scr, ar_ref[...], -ai_ref[...], reverse=True)
        t = lax.broadcasted_iota(jnp.int32, (rc, lw), 0)

        def p2(c, _):
            r0 = pl.multiple_of(c * rc, rc)
            rows = pl.ds(r0, rc)
            before = pl.ds(pl.multiple_of(jnp.maximum(r0 - 8, 0), 8), 8)
            have = (c > 0).astype(F32)
            lr, li = lr_scr[rows, :], li_scr[rows, :]
            spr = jnp.where(t == 0, sr_ref[before, :][7:8, :] * have, pltpu.roll(sr_ref[rows, :], 1, 0))
            spi = jnp.where(t == 0, si_ref[before, :][7:8, :] * have, pltpu.roll(si_ref[rows, :], 1, 0))
            dar_ref[...] += jnp.sum(lr * spr + li * spi, 0, keepdims=True)
            dai_ref[...] += jnp.sum(li * spr - lr * spi, 0, keepdims=True)
            lrb, lib = lr.astype(_MXU), li.astype(_MXU)
            du = (dy_scr[rows, :] * d_ref[...]
                  + lax.dot_general(lrb, bdr_ref[...], (NT, ((), ())), preferred_element_type=F32)
                  + lax.dot_general(lib, bdi_ref[...], (NT, ((), ())), preferred_element_type=F32))
            du_ref[rows, :] = du.astype(du_ref.dtype)
            ub = u_ref[rows, :].astype(_MXU)
            dbr_ref[...] += lax.dot_general(ub, lrb, (TN, ((), ())), preferred_element_type=F32)
            dbi_ref[...] += lax.dot_general(ub, lib, (TN, ((), ())), preferred_element_type=F32)
            return 0

        lax.fori_loop(0, s // rc, p2, 0)

    st = pl.BlockSpec((s, lw), lambda j: (0, j))
    ch = pl.BlockSpec((s, cw), lambda j: (0, j))
    bspec = pl.BlockSpec((None, cw, lw), lambda j: (j, 0, 0))
    cspec = pl.BlockSpec((None, lw, cw), lambda j: (j, 0, 0))
    cvec = pl.BlockSpec((1, cw), lambda j: (0, j))
    svec = pl.BlockSpec((1, lw), lambda j: (0, j))
    return pl.pallas_call(
        body, grid=(nt,),
        in_specs=[ch, ch, pl.BlockSpec((s, cw), lambda j: (0, u_block0 + j)), st, st, bspec, bspec, cspec, cspec, cvec, svec, svec],
        out_specs=[ch, cvec, cspec, cspec, bspec, bspec, svec, svec],
        out_shape=[_sds((s, nt * cw), _MXU), _sds((1, nt * cw), F32), _sds((nt, lw, cw), F32), _sds((nt, lw, cw), F32),
                   _sds((nt, cw, lw), F32), _sds((nt, cw, lw), F32), _sds((1, nt * lw), F32), _sds((1, nt * lw), F32)],
        scratch_shapes=[pltpu.VMEM((s, lw), F32), pltpu.VMEM((s, lw), F32), pltpu.VMEM((s, cw), F32)],
        compiler_params=_cp(), name=name,
    )(dyg, ypre, h, sr, si, bdr, bdi, cdr, cdi, dvec, ar, ai)


def _glu_fwd(name, yg, gw):
    s, w = yg.shape
    tr = min(512, s)

    def body(y_ref, w_ref, o_ref, ab_ref):
        ab = jnp.dot(y_ref[...], w_ref[...], preferred_element_type=F32)
        ab_ref[...] = ab
        o_ref[...] = (ab[:, :w] * jax.nn.sigmoid(ab[:, w:])).astype(o_ref.dtype)

    return pl.pallas_call(
        body, grid=(s // tr,), in_specs=[pl.BlockSpec((tr, w), lambda i: (i, 0)), _resident((w, 2 * w), lambda i: (0, 0))],
        out_specs=[pl.BlockSpec((tr, w), lambda i: (i, 0)), pl.BlockSpec((tr, 2 * w), lambda i: (i, 0))],
        out_shape=[_sds((s, w), _MXU), _sds((s, 2 * w), F32)], compiler_params=_cp(), name=name,
    )(yg, gw)


def _glu_bwd(name, dmix, col_block, ab, gw):
    s = ab.shape[0]
    w = ab.shape[1] // 2
    tr = min(512, s)

    def body(do_ref, ab_ref, w_ref, dab_ref, dy_ref):
        do = do_ref[...]
        a, b = ab_ref[:, :w], ab_ref[:, w:]
        sg = jax.nn.sigmoid(b)
        da = (do * sg).astype(_MXU)
        db = (do * a * sg * (1.0 - sg)).astype(_MXU)
        dab_ref[:, :w] = da
        dab_ref[:, w:] = db
        dy_ref[...] = (lax.dot_general(da, w_ref[:, :w], (NT, ((), ())), preferred_element_type=F32)
                       + lax.dot_general(db, w_ref[:, w:], (NT, ((), ())), preferred_element_type=F32))

    return pl.pallas_call(
        body, grid=(s // tr,),
        in_specs=[pl.BlockSpec((tr, w), lambda i: (i, col_block)), pl.BlockSpec((tr, 2 * w), lambda i: (i, 0)),
                  _resident((w, 2 * w), lambda i: (0, 0))],
        out_specs=[pl.BlockSpec((tr, 2 * w), lambda i: (i, 0)), pl.BlockSpec((tr, w), lambda i: (i, 0))],
        out_shape=[_sds((s, 2 * w), _MXU), _sds((s, w), F32)], compiler_params=_cp(), name=name,
    )(dmix, ab, gw)


CONV_ROWS = 64


def _conv_chunk(ref, w_ref, b_ref, c, tt):
    r0 = pl.multiple_of(c * tt, tt)
    before = ref[pl.ds(pl.multiple_of(jnp.maximum(r0 - 8, 0), 8), 8), :]
    before = jnp.where(c > 0, before, 0.0)
    main = ref[pl.ds(r0, tt), :]
    ext = jnp.concatenate([before, main], 0)
    d1 = pltpu.roll(ext, 1, 0)[8:, :]
    d2 = pltpu.roll(ext, 2, 0)[8:, :]
    hc = b_ref[...] + d2 * w_ref[0:1, :]
    hc = hc + d1 * w_ref[1:2, :]
    return hc + main * w_ref[2:3, :], main, d1, d2


def _conv_act_fwd(name, hu, cw, cb):
    s, f2 = hu.shape
    f = f2 // 2
    tw = _col_tile(f, 256)
    nt = f // tw

    tt = min(CONV_ROWS, s)

    def body(v_ref, g_ref, wv_ref, wg_ref, bv_ref, bg_ref, act_ref):
        def chunk(c, _):
            val = _conv_chunk(v_ref, wv_ref, bv_ref, c, tt)[0]
            gate = _conv_chunk(g_ref, wg_ref, bg_ref, c, tt)[0]
            act_ref[pl.ds(pl.multiple_of(c * tt, tt), tt), :] = (jax.nn.silu(gate) * val).astype(act_ref.dtype)
            return 0

        lax.fori_loop(0, s // tt, chunk, 0)

    cv = lambda rows: pl.BlockSpec((rows, tw), lambda i: (0, i))
    cg = lambda rows: pl.BlockSpec((rows, tw), lambda i: (0, nt + i))
    return pl.pallas_call(
        body, grid=(nt,), in_specs=[cv(s), cg(s), cv(CONV_WIDTH), cg(CONV_WIDTH), cv(1), cg(1)],
        out_specs=cv(s), out_shape=_sds((s, f), _MXU), compiler_params=_cp(), name=name,
    )(hu, hu, cw, cw, cb, cb)


def _conv_act_bwd(name, dact, hu, cw, cb):
    s, f2 = hu.shape
    f = f2 // 2
    tw = _col_tile(f, 256)
    nt = f // tw

    tt = min(CONV_ROWS, s)
    nchunk = s // tt

    def body(da_ref, v_ref, g_ref, wv_ref, wg_ref, bv_ref, bg_ref, dh_ref, dwv_ref, dwg_ref, dbv_ref, dbg_ref):
        def chunk(i, carry):
            c = nchunk - 1 - i
            rows = pl.ds(pl.multiple_of(c * tt, tt), tt)
            val, hv, hv1, hv2 = _conv_chunk(v_ref, wv_ref, bv_ref, c, tt)
            gate, hg, hg1, hg2 = _conv_chunk(g_ref, wg_ref, bg_ref, c, tt)
            sg = jax.nn.sigmoid(gate)
            da = da_ref[rows, :]
            dval = da * (gate * sg)
            dgate = da * val * sg * (1.0 + gate * (1.0 - sg))
            out = []
            for part, dhc, taps, w_ref, (after, acc) in ((0, dval, (hv2, hv1, hv), wv_ref, carry[0]),
                                                        (1, dgate, (hg2, hg1, hg), wg_ref, carry[1])):
                ext = jnp.concatenate([dhc, after], 0)
                u1 = pltpu.roll(ext, tt + 8 - 1, 0)[:tt, :]
                u2 = pltpu.roll(ext, tt + 8 - 2, 0)[:tt, :]
                dh = dhc * w_ref[2:3, :] + u1 * w_ref[1:2, :] + u2 * w_ref[0:1, :]
                dh_ref[part, rows, :] = dh.astype(dh_ref.dtype)
                sums = [jnp.sum((dhc * tap).reshape(tt // 8, 8, tw), 0) for tap in taps]
                sums.append(jnp.sum(dhc.reshape(tt // 8, 8, tw), 0))
                out.append((dhc[0:8, :], tuple(a + b for a, b in zip(acc, sums))))
            return tuple(out)

        zero = (jnp.zeros((8, tw), F32), tuple(jnp.zeros((8, tw), F32) for _ in range(CONV_WIDTH + 1)))
        (_, acc_v), (_, acc_g) = lax.fori_loop(0, nchunk, chunk, (zero, zero))
        for acc, dw_ref, db_ref in ((acc_v, dwv_ref, dbv_ref), (acc_g, dwg_ref, dbg_ref)):
            for tap in range(CONV_WIDTH):
                dw_ref[tap:tap + 1, :] = jnp.sum(acc[tap], 0, keepdims=True)
            db_ref[...] = jnp.sum(acc[CONV_WIDTH], 0, keepdims=True)

    cv = lambda rows: pl.BlockSpec((rows, tw), lambda i: (0, i))
    cg = lambda rows: pl.BlockSpec((rows, tw), lambda i: (0, nt + i))
    both = pl.BlockSpec((2, s, tw), lambda i: (0, 0, i))
    dh, dwv, dwg, dbv, dbg = pl.pallas_call(
        body, grid=(nt,), in_specs=[cv(s), cv(s), cg(s), cv(CONV_WIDTH), cg(CONV_WIDTH), cv(1), cg(1)],
        out_specs=[both, cv(CONV_WIDTH), cv(CONV_WIDTH), cv(1), cv(1)],
        out_shape=[_sds((2, s, f), _MXU), _sds((CONV_WIDTH, f), F32), _sds((CONV_WIDTH, f), F32),
                   _sds((1, f), F32), _sds((1, f), F32)],
        compiler_params=_cp(), name=name,
    )(dact, hu, hu, cw, cw, cb, cb)
    return dh, jnp.concatenate([dwv, dwg], 1), jnp.concatenate([dbv, dbg], 1)


ELEM_BLOCK = 512 * 1024


def _elem_tiles(r, c, budget=ELEM_BLOCK):
    rows = [t for t in range(8, r + 1, 8) if r % t == 0] or [r]
    cols = [t for t in range(V7X_LANES, c + 1, V7X_LANES) if c % t == 0] or [c]
    fits = [(tr * tc, tc, tr) for tr in rows for tc in cols if tr * tc <= budget]
    if not fits:
        return min(rows), min(cols)
    _, tc, tr = max(fits)
    return tr, tc


def _sum_parts(name, parts):
    n, r, c = parts.shape
    tr, tc = _elem_tiles(r, c, ELEM_BLOCK // n)

    def body(p_ref, g_ref):
        g = p_ref[0]
        for i in range(1, n):
            g = g + p_ref[i]
        g_ref[...] = g

    return pl.pallas_call(
        body, grid=(r // tr, c // tc), in_specs=[pl.BlockSpec((n, tr, tc), lambda i, j: (0, i, j))],
        out_specs=pl.BlockSpec((tr, tc), lambda i, j: (i, j)), out_shape=_sds((r, c), F32), compiler_params=_cp(), name=name,
    )(parts)


def _adamw_2d(name, g, w, m, v):
    r, c = w.shape
    tc = c if c % V7X_LANES else _col_tile(c, 2048)
    rows = [t for t in range(8, r + 1, 8) if r % t == 0 and t * max(tc, V7X_LANES) <= ELEM_BLOCK // 4] or [r]
    tr = max(rows)
    c1 = 1.0 - ADAM_B1 ** ADAM_STEP
    c2 = 1.0 - ADAM_B2 ** ADAM_STEP

    def body(g_ref, w_ref, m_ref, v_ref, d_ref, nm_ref, nv_ref):
        g = g_ref[...]
        nm = ADAM_B1 * m_ref[...] + (1.0 - ADAM_B1) * g
        nv = ADAM_B2 * v_ref[...] + (1.0 - ADAM_B2) * (g * g)
        m_hat = nm / c1
        v_hat = nv / c2
        nm_ref[...] = nm
        nv_ref[...] = nv
        d_ref[...] = -ADAM_LR * (m_hat / (jnp.sqrt(v_hat) + ADAM_EPS) + ADAM_WD * w_ref[...])

    blk = pl.BlockSpec((tr, tc), lambda i, j: (i, j))
    out = _sds((r, c), F32)
    return pl.pallas_call(
        body, grid=(r // tr, c // tc), in_specs=[blk] * 4, out_specs=[blk] * 3, out_shape=[out] * 3,
        compiler_params=_cp(), name=name,
    )(g, w, m, v)


def _pair_sum(name, mine, theirs, c_idx):
    _, _, r, c = mine.shape
    tr, tc = _elem_tiles(r, c)

    def body(c_ref, a_ref, b_ref, o_ref):
        o_ref[...] = (a_ref[...].astype(F32) + b_ref[...].astype(F32)).astype(o_ref.dtype)

    return pl.pallas_call(
        body,
        grid_spec=pltpu.PrefetchScalarGridSpec(
            num_scalar_prefetch=1, grid=(4, r // tr, c // tc),
            in_specs=[pl.BlockSpec((None, None, tr, tc), lambda p, i, j, cref: (p, cref[0], i, j)),
                      pl.BlockSpec((None, tr, tc), lambda p, i, j, cref: (p, i, j))],
            out_specs=pl.BlockSpec((None, tr, tc), lambda p, i, j, cref: (p, i, j))),
        out_shape=_sds((4, r, c), _WIRE), compiler_params=_cp(), name=name,
    )(c_idx, mine, theirs)


def _place():
    return lax.axis_index("x"), lax.axis_index("y"), lax.axis_index("c")


def _all_gather(name, xs):
    n = len(xs)

    def body(*refs):
        x_refs, o_refs = refs[:n], refs[n:2 * n]
        send_sems, recv_sems, local_sems = refs[2 * n:]
        x, y, c = _place()
        me, sibling = (x, y, c), (x, y, 1 - c)
        chips = [(1 - x, y), (x, 1 - y), (1 - x, 1 - y)]

        def copy(a, k, block, to, src=None):
            px, py, pc = block
            rows = o_refs[a].at[4 * px + 2 * py + pc]
            return pltpu.make_async_remote_copy(
                src_ref=rows if src is None else src, dst_ref=rows, send_sem=send_sems.at[a, k], recv_sem=recv_sems.at[a, k],
                device_id=to, device_id_type=MESH)

        sent = []
        mine = []
        for a in range(n):
            mx, my, mc = me
            cp = pltpu.make_async_copy(x_refs[a], o_refs[a].at[4 * mx + 2 * my + mc], local_sems.at[a])
            cp.start()
            mine.append(cp)
            first = [copy(a, 0, me, sibling, src=x_refs[a])]
            first += [copy(a, 1 + j, me, (*chip, c), src=x_refs[a]) for j, chip in enumerate(chips)]
            for cp in first:
                cp.start()
            sent += first
        for a in range(n):
            for j, chip in enumerate(chips):
                copy(a, 1 + j, (*chip, c), me).wait_recv()
                fwd = copy(a, 4 + j, (*chip, c), sibling)
                fwd.start()
                sent.append(fwd)
        for a in range(n):
            copy(a, 0, sibling, me).wait_recv()
            for j, chip in enumerate(chips):
                copy(a, 4 + j, (*chip, 1 - c), me).wait_recv()
        for cp in sent:
            cp.wait_send()
        for cp in mine:
            cp.wait()

    return pl.pallas_call(
        body, in_specs=[ANY] * n, out_specs=[ANY] * n,
        out_shape=[_sds((N_DEV,) + a.shape, a.dtype) for a in xs],
        scratch_shapes=[pltpu.SemaphoreType.DMA((n, 7)), pltpu.SemaphoreType.DMA((n, 7)), pltpu.SemaphoreType.DMA((n,))],
        name=name,
    )(*xs)


HBM = pl.BlockSpec(memory_space=pltpu.HBM)
SEM = pl.BlockSpec(memory_space=pltpu.SEMAPHORE)
DATAFLOW = pltpu.SideEffectType.DATAFLOW_SIDE_EFFECTING


def _in_hbm(a):
    return pltpu.with_memory_space_constraint(a, pltpu.HBM)


def _split_copy_start(name, srcs, lands, copies, deps):
    n, nd = len(srcs), len(deps)
    per = len(copies([None] * n, [None] * n, probe=True)) // n

    def body(*refs):
        s_refs, l_refs = refs[:n], refs[n:2 * n]
        send_sems, recv_sems = refs[2 * n + nd], refs[2 * n + nd + 1]
        token = refs[-1]
        for a, k, src, dst, to in copies(s_refs, l_refs):
            pltpu.make_async_remote_copy(src_ref=src, dst_ref=dst, send_sem=send_sems.at[a * per + k],
                                         recv_sem=recv_sems.at[a * per + k], device_id=to, device_id_type=MESH).start()
        token[...] = jnp.zeros_like(token)

    both = list(srcs) + list(lands)
    outs = pl.pallas_call(
        body, name=name,
        out_shape=(pltpu.SemaphoreType.DMA((n * per,)), pltpu.SemaphoreType.DMA((n * per,)),
                   *[pltpu.HBM(a.shape, a.dtype) for a in both], _sds((8, V7X_LANES), F32)),
        in_specs=[HBM] * (2 * n) + [ANY] * nd,
        out_specs=(SEM, SEM, *[HBM] * (2 * n), pl.BlockSpec(memory_space=pltpu.VMEM)),
        input_output_aliases={i: 2 + i for i in range(2 * n)},
        compiler_params=pltpu.CompilerParams(has_side_effects=DATAFLOW),
    )(*[_in_hbm(a) for a in both], *deps)
    return outs[0], outs[1], list(outs[2:2 + n]), list(outs[2 + n:2 + 2 * n]), outs[-1]


def _split_copy_wait(name, send_sems, recv_sems, srcs, lands, arrivals, after):
    n = len(srcs)
    per = len(arrivals([None] * n, [None] * n, probe=True)) // n

    def body(*refs):
        s_refs, l_refs = refs[:n], refs[n:2 * n]
        send_sems_, recv_sems_ = refs[2 * n], refs[2 * n + 1]
        for a, k, src, dst, frm in arrivals(s_refs, l_refs):
            cp = pltpu.make_async_remote_copy(src_ref=src, dst_ref=dst, send_sem=send_sems_.at[a * per + k],
                                              recv_sem=recv_sems_.at[a * per + k], device_id=frm, device_id_type=MESH)
            cp.wait_send()
            cp.wait_recv()

    both = list(srcs) + list(lands)
    outs = pl.pallas_call(
        body, name=name, out_shape=tuple(pltpu.HBM(a.shape, a.dtype) for a in both),
        in_specs=[HBM] * (2 * n) + [SEM, SEM, ANY], out_specs=tuple([HBM] * (2 * n)),
        input_output_aliases={i: i for i in range(2 * n)},
        compiler_params=pltpu.CompilerParams(has_side_effects=DATAFLOW),
    )(*both, send_sems, recv_sems, after)
    return list(outs[:n]), list(outs[n:])


def _gather_copies(arriving):
    def copies(s_refs, l_refs, probe=False):
        if probe:
            return [None] * (4 * len(s_refs))
        x, y, c = _place()
        out = []
        for a in range(len(s_refs)):
            for k, (px, py, pc) in enumerate([(x, y, 1 - c), (1 - x, y, c), (x, 1 - y, c), (1 - x, 1 - y, c)]):
                slot = 4 * px + 2 * py + pc if arriving else 4 * x + 2 * y + c
                out.append((a, k, s_refs[a], l_refs[a].at[slot], (px, py, pc)))
        return out
    return copies


def _chip_copies(arriving):
    def copies(s_refs, l_refs, probe=False):
        if probe:
            return [None] * (3 * len(s_refs))
        x, y, c = _place()
        out = []
        for a in range(len(s_refs)):
            for j, (px, py) in enumerate([(1 - x, y), (x, 1 - y), (1 - x, 1 - y)]):
                src = s_refs[a].at[2 * x + y] if arriving else s_refs[a].at[2 * px + py]
                out.append((a, j, src, l_refs[a].at[j], (px, py, c)))
        return out
    return copies


def _sibling_copies(s_refs, l_refs, probe=False):
    if probe:
        return [None] * (4 * len(s_refs))
    x, y, c = _place()
    return [(a, p, s_refs[a].at[p, 1 - c], l_refs[a].at[p], (x, y, 1 - c)) for a in range(len(s_refs)) for p in range(4)]


def _sibling_begin(name, by_owner, deps):
    lands = [lax.empty((4,) + a.shape[2:], a.dtype) for a in by_owner]
    return _split_copy_start(name + "_start", by_owner, lands, _sibling_copies, deps)


def _sibling_end(name, handle, after):
    send_sems, recv_sems, srcs, lands, _ = handle
    return _split_copy_wait(name + "_wait", send_sems, recv_sems, srcs, lands, _sibling_copies, after)


def _gather_begin(name, shards, deps):
    x, y, c = _place()
    lands = [lax.dynamic_update_slice_in_dim(lax.empty((N_DEV,) + a.shape, a.dtype), a[None], 4 * x + 2 * y + c, 0)
             for a in shards]
    return _split_copy_start(name + "_start", shards, lands, _gather_copies(False), deps)


def _gather_end(name, handle, after):
    send_sems, recv_sems, srcs, lands, _ = handle
    _, lands = _split_copy_wait(name + "_wait", send_sems, recv_sems, srcs, lands, _gather_copies(True), after)
    return _gather_forward(name + "_forward", lands)


def _forward_copies(arriving):
    def copies(s_refs, l_refs, probe=False):
        if probe:
            return [None] * (3 * len(l_refs))
        x, y, c = _place()
        out = []
        for a in range(len(l_refs)):
            for j, (px, py) in enumerate([(1 - x, y), (x, 1 - y), (1 - x, 1 - y)]):
                mine, theirs = l_refs[a].at[4 * px + 2 * py + c], l_refs[a].at[4 * px + 2 * py + 1 - c]
                out.append((a, j, mine, theirs if arriving else mine, (x, y, 1 - c)))
        return out
    return copies


def _gather_arrived(name, handle, after):
    send_sems, recv_sems, srcs, lands, _ = handle
    srcs, lands = _split_copy_wait(name + "_wait", send_sems, recv_sems, srcs, lands, _gather_copies(True), after)
    return _split_copy_start(name + "_forward_start", srcs, lands, _forward_copies(False), [])


def _gather_done(name, handle, after):
    send_sems, recv_sems, srcs, lands, _ = handle
    return _split_copy_wait(name + "_forward_wait", send_sems, recv_sems, srcs, lands, _forward_copies(True), after)[1]


def _gather_forward(name, lands):
    n = len(lands)

    def body(*refs):
        o_refs = refs[n:2 * n]
        send_sems, recv_sems = refs[2 * n:]
        x, y, c = _place()
        sibling = (x, y, 1 - c)
        chips = [(1 - x, y), (x, 1 - y), (1 - x, 1 - y)]
        sent = []
        for a in range(n):
            for j, (px, py) in enumerate(chips):
                rows = o_refs[a].at[4 * px + 2 * py + c]
                cp = pltpu.make_async_remote_copy(src_ref=rows, dst_ref=rows, send_sem=send_sems.at[a, j],
                                                  recv_sem=recv_sems.at[a, j], device_id=sibling, device_id_type=MESH)
                cp.start()
                sent.append(cp)
        for a in range(n):
            for j, (px, py) in enumerate(chips):
                rows = o_refs[a].at[4 * px + 2 * py + 1 - c]
                pltpu.make_async_remote_copy(src_ref=rows, dst_ref=rows, send_sem=send_sems.at[a, j],
                                             recv_sem=recv_sems.at[a, j], device_id=sibling, device_id_type=MESH).wait_recv()
        for cp in sent:
            cp.wait_send()

    return pl.pallas_call(
        body, in_specs=[ANY] * n, out_specs=[ANY] * n, out_shape=[_sds(a.shape, a.dtype) for a in lands],
        input_output_aliases={i: i for i in range(n)},
        scratch_shapes=[pltpu.SemaphoreType.DMA((n, 3)), pltpu.SemaphoreType.DMA((n, 3))], name=name,
    )(*lands)


def _chips_begin(name, pairs, deps):
    lands = [lax.empty((3,) + a.shape[1:], a.dtype) for a in pairs]
    return _split_copy_start(name + "_start", pairs, lands, _chip_copies(False), deps)


def _chips_end(name, handle, after):
    send_sems, recv_sems, srcs, lands, _ = handle
    return _split_copy_wait(name + "_wait", send_sems, recv_sems, srcs, lands, _chip_copies(True), after)


def _adamw_layer(name, l, own, lands, w, m, v, prev):
    nl, ng, r, c = w.shape
    tr, tc = _elem_tiles(r, c)
    c1 = 1.0 - ADAM_B1 ** ADAM_STEP
    c2 = 1.0 - ADAM_B2 ** ADAM_STEP

    def body(own_ref, lands_ref, w_ref, m_ref, v_ref, *rest):
        g_ref, d_ref, nm_ref, nv_ref = rest[-4:]
        g = own_ref[...].astype(F32) + lands_ref[0].astype(F32) + lands_ref[1].astype(F32) + lands_ref[2].astype(F32)
        nm = ADAM_B1 * m_ref[...] + (1.0 - ADAM_B1) * g
        nv = ADAM_B2 * v_ref[...] + (1.0 - ADAM_B2) * (g * g)
        m_hat = nm / c1
        v_hat = nv / c2
        g_ref[...] = g
        nm_ref[...] = nm
        nv_ref[...] = nv
        d_ref[...] = -ADAM_LR * (m_hat / (jnp.sqrt(v_hat) + ADAM_EPS) + ADAM_WD * w_ref[...])

    lay = pl.BlockSpec((None, None, tr, tc), lambda g, i, j: (l, g, i, j))
    out = _sds((nl, ng, r, c), F32)
    prev = [] if prev is None else list(prev)
    return pl.pallas_call(
        body, grid=(ng, r // tr, c // tc),
        in_specs=[pl.BlockSpec((None, tr, tc), lambda g, i, j: (g, i, j)),
                  pl.BlockSpec((3, None, tr, tc), lambda g, i, j: (0, g, i, j)), lay, lay, lay] + [ANY] * len(prev),
        out_specs=[lay] * 4, out_shape=[out] * 4, input_output_aliases={5 + i: i for i in range(len(prev))},
        compiler_params=_cp(), name=name,
    )(own, lands, w, m, v, *prev)


def _pad_pairs(a, axis, half, half_pad):
    shp = a.shape
    a = a.reshape(shp[:axis] + (2, half) + shp[axis + 1:])
    pad = [(0, 0)] * a.ndim
    pad[axis + 1] = (0, half_pad - half)
    a = jnp.pad(a, pad)
    return a.reshape(shp[:axis] + (2 * half_pad,) + shp[axis + 1:])


def _unpad_pairs(a, axis, half, half_pad):
    shp = a.shape
    a = a.reshape(shp[:axis] + (2, half_pad) + shp[axis + 1:])
    a = lax.slice_in_dim(a, 0, half, axis=axis + 1)
    return a.reshape(shp[:axis] + (2 * half,) + shp[axis + 1:])


def _blockdiag(w, nt):
    g, a, b = w.shape
    gl = g // nt
    e = jnp.eye(gl, dtype=w.dtype).reshape(1, gl, 1, gl, 1)
    return (w.reshape(nt, gl, a, 1, b) * e).reshape(nt, gl * a, gl * b)


def _diagblocks(m, g, a, b):
    nt = m.shape[0]
    gl = g // nt
    d = jnp.diagonal(m.reshape(nt, gl, a, gl, b), axis1=1, axis2=3)
    return jnp.moveaxis(d, -1, 1).reshape(g, a, b)


_PIECE = 8 * V7X_LANES


def _pack(pieces, row_multiple=512):
    rows = []
    for p in pieces:
        flat = p.reshape(-1).astype(F32)
        rows.append(jnp.pad(flat, (0, -flat.shape[0] % _PIECE)).reshape(-1, V7X_LANES))
    fill = -sum(r.shape[0] for r in rows) % row_multiple
    if fill:
        rows.append(jnp.zeros((fill, V7X_LANES), F32))
    return jnp.concatenate(rows, 0)


def _unpack(packed, shapes):
    lead = packed.shape[:-2]
    out, off = [], 0
    for shp in shapes:
        n = math.prod(shp)
        r = -(-n // _PIECE) * 8
        piece = packed[..., off:off + r, :].reshape(lead + (r * V7X_LANES,))[..., :n]
        out.append(piece.reshape(lead + tuple(shp)))
        off += r
    return out


def _ssm_discretise(lam_re, lam_im, log_dt, b_re, b_im):
    dt = jnp.exp(log_dt)[..., None]
    mag = jnp.exp(lam_re * dt)
    ab_re, ab_im = mag * jnp.cos(lam_im * dt), mag * jnp.sin(lam_im * dt)
    nr, ni = ab_re - 1.0, ab_im
    den = lam_re * lam_re + lam_im * lam_im
    zr = (nr * lam_re + ni * lam_im) / den
    zi = (ni * lam_re - nr * lam_im) / den
    bbr = zr[..., None] * b_re - zi[..., None] * b_im
    bbi = zr[..., None] * b_im + zi[..., None] * b_re
    return ab_re, ab_im, bbr, bbi


def _rope_tables(s):
    half = HEAD_DIM // 2
    inv = ROPE_THETA ** (-jnp.arange(half, dtype=F32) / half)
    ang = jnp.arange(s).astype(F32)[:, None] * inv[None, :]
    cos, sin = jnp.cos(ang), jnp.sin(ang)
    reps = V7X_LANES // HEAD_DIM
    return jnp.tile(jnp.concatenate([cos, cos], -1), (1, reps)), jnp.tile(jnp.concatenate([-sin, sin], -1), (1, reps))


_SMALL = ("attn_sinks", "pool_w", "pool_scale", "ssm_lam_re", "ssm_lam_im", "ssm_log_dt", "ssm_b_re", "ssm_b_im",
          "ssm_c_re", "ssm_c_im", "ssm_d", "ln1_g", "ln1_b", "ffn_conv_b", "ln2_g", "ln2_b")
_BIG = ("w_in", "ssm_glu_w", "w_out", "ffn_w_up", "ffn_w_down")
_RAW = ("attn_sinks", "pool_w", "pool_scale", "ssm_lam_re", "ssm_lam_im", "ssm_b_re", "ssm_b_im", "ssm_c_re", "ssm_c_im",
        "ssm_d", "ln1_g", "ln1_b", "ffn_conv_b", "ln2_g", "ln2_b", "ffn_conv_w")
_ORDER = ("w_in", "attn_sinks", "pool_w", "pool_scale", "ssm_lam_re", "ssm_lam_im", "ssm_log_dt", "ssm_b_re", "ssm_b_im",
          "ssm_c_re", "ssm_c_im", "ssm_d", "ssm_glu_w", "w_out", "ln1_g", "ln1_b", "ffn_w_up", "ffn_conv_w", "ffn_conv_b",
          "ffn_w_down", "ln2_g", "ln2_b")


def kernel(x, w_in, attn_sinks, pool_w, pool_scale, ssm_lam_re, ssm_lam_im, ssm_log_dt, ssm_b_re, ssm_b_im, ssm_c_re, ssm_c_im, ssm_d, ssm_glu_w, w_out, ln1_g, ln1_b, ffn_w_up, ffn_conv_w, ffn_conv_b, ffn_w_down, ln2_g, ln2_b, loss_target, m_w_in, m_attn_sinks, m_pool_w, m_pool_scale, m_ssm_lam_re, m_ssm_lam_im, m_ssm_log_dt, m_ssm_b_re, m_ssm_b_im, m_ssm_c_re, m_ssm_c_im, m_ssm_d, m_ssm_glu_w, m_w_out, m_ln1_g, m_ln1_b, m_ffn_w_up, m_ffn_conv_w, m_ffn_conv_b, m_ffn_w_down, m_ln2_g, m_ln2_b, v_w_in, v_attn_sinks, v_pool_w, v_pool_scale, v_ssm_lam_re, v_ssm_lam_im, v_ssm_log_dt, v_ssm_b_re, v_ssm_b_im, v_ssm_c_re, v_ssm_c_im, v_ssm_d, v_ssm_glu_w, v_w_out, v_ln1_g, v_ln1_b, v_ffn_w_up, v_ffn_conv_w, v_ffn_conv_b, v_ffn_w_down, v_ln2_g, v_ln2_b):
    W = dict(w_in=w_in, attn_sinks=attn_sinks, pool_w=pool_w, pool_scale=pool_scale, ssm_lam_re=ssm_lam_re, ssm_lam_im=ssm_lam_im, ssm_log_dt=ssm_log_dt, ssm_b_re=ssm_b_re, ssm_b_im=ssm_b_im, ssm_c_re=ssm_c_re, ssm_c_im=ssm_c_im, ssm_d=ssm_d, ssm_glu_w=ssm_glu_w, w_out=w_out, ln1_g=ln1_g, ln1_b=ln1_b, ffn_w_up=ffn_w_up, ffn_conv_w=ffn_conv_w, ffn_conv_b=ffn_conv_b, ffn_w_down=ffn_w_down, ln2_g=ln2_g, ln2_b=ln2_b)
    M = dict(w_in=m_w_in, attn_sinks=m_attn_sinks, pool_w=m_pool_w, pool_scale=m_pool_scale, ssm_lam_re=m_ssm_lam_re, ssm_lam_im=m_ssm_lam_im, ssm_log_dt=m_ssm_log_dt, ssm_b_re=m_ssm_b_re, ssm_b_im=m_ssm_b_im, ssm_c_re=m_ssm_c_re, ssm_c_im=m_ssm_c_im, ssm_d=m_ssm_d, ssm_glu_w=m_ssm_glu_w, w_out=m_w_out, ln1_g=m_ln1_g, ln1_b=m_ln1_b, ffn_w_up=m_ffn_w_up, ffn_conv_w=m_ffn_conv_w, ffn_conv_b=m_ffn_conv_b, ffn_w_down=m_ffn_w_down, ln2_g=m_ln2_g, ln2_b=m_ln2_b)
    V = dict(w_in=v_w_in, attn_sinks=v_attn_sinks, pool_w=v_pool_w, pool_scale=v_pool_scale, ssm_lam_re=v_ssm_lam_re, ssm_lam_im=v_ssm_lam_im, ssm_log_dt=v_ssm_log_dt, ssm_b_re=v_ssm_b_re, ssm_b_im=v_ssm_b_im, ssm_c_re=v_ssm_c_re, ssm_c_im=v_ssm_c_im, ssm_d=v_ssm_d, ssm_glu_w=v_ssm_glu_w, w_out=v_w_out, ln1_g=v_ln1_g, ln1_b=v_ln1_b, ffn_w_up=v_ffn_w_up, ffn_conv_w=v_ffn_conv_w, ffn_conv_b=v_ffn_conv_b, ffn_w_down=v_ffn_w_down, ln2_g=v_ln2_g, ln2_b=v_ln2_b)

    depth = w_in.shape[0]
    s, d = x.shape[1], x.shape[2]
    alpha = (2 * depth) ** 0.25
    attn_w = d // 2
    kv_w = attn_w // GQA
    nkv = kv_w // HEAD_DIM
    pool_wd = d // 4
    ssm_wd = d // 4
    n_groups = ssm_wd // SSM_GROUP
    state_w = n_groups * SSM_STATE
    nt_ssm = max(1, state_w // 512)
    o_k, o_v, o_p, o_s = attn_w, attn_w + kv_w, attn_w + 2 * kv_w, attn_w + 2 * kv_w + pool_wd
    in_w = o_s + ssm_wd
    half = ffn_w_down.shape[1]
    half_pad = -(-half // 64) * 64
    ffp = 4 * 2 * half_pad
    xi, yi, ci = _place()
    me = 4 * xi + 2 * yi + ci
    c_idx = jnp.reshape(ci, (1,)).astype(jnp.int32)

    cos_t, sin_t = _rope_tables(s)

    def layer_shards(l):
        return [
            jnp.transpose(w_in[l]).astype(_WIRE), ssm_glu_w[l].astype(_WIRE), w_out[l].astype(_WIRE),
            _pad_pairs(jnp.transpose(ffn_w_up[l]).astype(_WIRE), 0, half, half_pad),
            jnp.pad(ffn_w_down[l].astype(_WIRE), ((0, half_pad - half), (0, 0))),
        ]

    (g_cw,) = _all_gather("gather_conv_w", [_pad_pairs(ffn_conv_w, 2, half, half_pad)])

    def in_weights(l, gathered):
        (g_in,) = gathered
        return dict(
            win_t=g_in.reshape(in_w, d),
            cw=jnp.transpose(g_cw[:, l], (1, 0, 2)).reshape(CONV_WIDTH, 2 * ffp),
            cb=_pad_pairs(ffn_conv_b[l].reshape(N_DEV, 2 * half), 1, half, half_pad).reshape(1, 2 * ffp),
        )

    def out_weights(gathered):
        g_glu, g_out = gathered
        return dict(glu=jnp.transpose(g_glu, (1, 0, 2)).reshape(ssm_wd, 2 * ssm_wd), wout=g_out.reshape(d, d))

    def mixer_weights(l, gathered):
        return {**in_weights(l, gathered[:1]), **out_weights(gathered[1:])}

    def ffn_weights(gathered):
        g_up, g_down = gathered
        return dict(wup_t=g_up.reshape(2 * ffp, d), wdown=g_down.reshape(ffp, d))

    shards = [layer_shards(l) for l in range(depth)]
    full = [None] * depth
    gathers = {}

    def begin_gather(l, part, deps):
        arrays = shards[l][{"in": slice(0, 1), "out": slice(1, 3), "mixer": slice(0, 3), "ffn": slice(3, 5)}[part]]
        gathers[l, part] = ("sent", _gather_begin(f"gather_{part}_weights_{l}", arrays, deps))
        return gathers[l, part][1][-1]

    def arrive_gather(l, part, after):
        gathers[l, part] = ("forwarding", _gather_arrived(f"gather_{part}_weights_{l}", gathers[l, part][1], after))
        return gathers[l, part][1][-1]

    def end_gather(l, part, after):
        stage, handle = gathers.pop((l, part))
        return (_gather_done if stage == "forwarding" else _gather_end)(f"gather_{part}_weights_{l}", handle, after)

    issued = begin_gather(0, "ffn", [begin_gather(0, "out", [begin_gather(0, "in", [g_cw])])])
    full[0] = in_weights(0, end_gather(0, "in", g_cw))

    ssm_params = (ssm_lam_re, ssm_lam_im, ssm_log_dt, ssm_b_re, ssm_b_im)
    ab_re_all, ab_im_all, bbr_all, bbi_all = _ssm_discretise(*ssm_params)

    def ssm_maps(w):
        return jax.vmap(lambda t: _blockdiag(jnp.transpose(t, (0, 2, 1)), nt_ssm))(w).astype(_MXU)

    bdr_all, bdi_all, cdr_all, cdi_all = ssm_maps(bbr_all), ssm_maps(bbi_all), ssm_maps(ssm_c_re), ssm_maps(ssm_c_im)

    saved = []
    xf = x[0]
    xb = xf.astype(_MXU)
    for l in range(depth):
        fw = full[l]
        deps = [arrive_gather(l, "ffn", xb)] if l >= 2 else []
        if l + 1 < depth:
            issued = begin_gather(l + 1, "ffn", [begin_gather(l + 1, "mixer", [fw["win_t"], issued])])
            deps.append(issued)
        h = _mm_nt(f"in_proj_{l}", xb, fw["win_t"], deps=tuple(deps))
        q_rot, k_rot = _rope(f"rope_{l}", h, o_v, cos_t, sin_t, _MXU, ((0, o_k), (o_k, o_v)))
        k_hm = jnp.transpose(k_rot.reshape(s, nkv, HEAD_DIM), (1, 0, 2))
        v_hm = jnp.transpose(h[:, o_v:o_p].astype(_MXU).reshape(s, nkv, HEAD_DIM), (1, 0, 2))
        sinks = attn_sinks[l].reshape(nkv, GQA)
        o_attn, lse = _attn_fwd(f"attn_{l}", q_rot, k_hm, v_hm, sinks)
        pw_b = pool_w[l].astype(_MXU)
        psc = pool_scale[l].reshape(1, pool_wd)
        y_pool, pre = _pool_fwd(f"pool_{l}", h, o_p // pool_wd, pw_b, psc)
        bdr, bdi, cdr, cdi = bdr_all[l], bdi_all[l], cdr_all[l], cdi_all[l]
        dvec = ssm_d[l].reshape(1, ssm_wd)
        ar, ai = ab_re_all[l].reshape(1, state_w), ab_im_all[l].reshape(1, state_w)
        cw_ssm = ssm_wd // nt_ssm
        sr, si, ypre, yg = _ssm_fwd(f"ssm_{l}", h, o_s // cw_ssm, bdr, bdi, cdr, cdi, dvec, ar, ai)
        if l == 0:
            fw.update(out_weights(end_gather(0, "out", yg)))
        y_ssm, ab2 = _glu_fwd(f"glu_{l}", yg, fw["glu"])
        mix = jnp.concatenate([o_attn.astype(_MXU), y_pool, y_ssm], -1)
        a1 = _mm_nn(f"out_proj_{l}", mix, fw["wout"])
        g1, b1 = ln1_g[l].reshape(1, d), ln1_b[l].reshape(1, d)
        x1, x1b, xh1, rs1 = _ln_fwd(f"ln1_{l}", xf, a1, g1, b1, alpha)
        fw.update(ffn_weights(end_gather(l, "ffn", x1b)))
        deps = (arrive_gather(l + 1, "mixer", x1b),) if 1 <= l < depth - 1 else ()
        hu = _mm_nt(f"ffn_up_{l}", x1b, fw["wup_t"], cap=2 * half_pad, deps=deps)
        act = _conv_act_fwd(f"ffn_act_{l}", hu, fw["cw"], fw["cb"])
        f_out = _mm_nn(f"ffn_down_{l}", act, fw["wdown"], cap=256)
        g2, b2 = ln2_g[l].reshape(1, d), ln2_b[l].reshape(1, d)
        x2, x2b, xh2, rs2 = _ln_fwd(f"ln2_{l}", x1, f_out, g2, b2, alpha)
        saved.append(dict(xb=xb, h=h, q_rot=q_rot, k_hm=k_hm, v_hm=v_hm, sinks=sinks, o_attn=o_attn, lse=lse, pw_b=pw_b, psc=psc,
                          pre=pre, bdr=bdr, bdi=bdi, cdr=cdr, cdi=cdi, dvec=dvec, ar=ar, ai=ai, sr=sr, si=si, ypre=ypre, yg=yg,
                          ab2=ab2, mix=mix, g1=g1, xh1=xh1, rs1=rs1, x1b=x1b, hu=hu, act=act, g2=g2, xh2=xh2, rs2=rs2))
        xf, xb = x2, x2b
        if l + 1 < depth:
            full[l + 1] = mixer_weights(l + 1, end_gather(l + 1, "mixer", x2b))

    dy, loss_part = _loss_head("loss_head", xf, loss_target[0])
    loss = lax.psum(loss_part[0, 0], ("x", "y", "c"))

    small_handles = [None] * depth
    small_parts = [None] * depth
    outs = {}
    big_res = {k: None for k in _BIG}
    my_chip = 2 * xi + yi
    pending = []

    transposed = ("w_in", "ffn_w_up")

    def row_groups(name_, t):
        g = 2 if name_ == "ffn_w_up" else 1
        return t.reshape(t.shape[:-2] + (g, t.shape[-2] // g, t.shape[-1]))

    def as_groups(name_, t):
        return row_groups(name_, jnp.transpose(t, (0, 2, 1)) if name_ in transposed else t)

    def from_groups(name_, t):
        t = t.reshape(t.shape[0], t.shape[1] * t.shape[2], t.shape[3])
        return jnp.transpose(t, (0, 2, 1)) if name_ in transposed else t

    grouped = {name_: tuple(as_groups(name_, t[name_]) for t in (W, M, V)) for name_ in _BIG}

    def finish_exchanges(after):
        while pending:
            lay, part, names, handle = pending.pop(0)
            pairs, lands = _chips_end(f"grads_between_chips_{part}_{lay}", handle, after)
            for name_, p, ld in zip(names, pairs, lands):
                own = row_groups(name_, lax.dynamic_index_in_dim(p, my_chip, 0, keepdims=False))
                big_res[name_] = _adamw_layer(f"adamw_{name_}_{lay}", lay, own, row_groups(name_, ld), *grouped[name_],
                                              big_res[name_])

    def begin_swap(lay, part, names, grads):
        by_owner = [a.reshape((4, 2) + a.shape[1:]) for a in grads]
        return lay, part, names, _sibling_begin(f"grads_to_sibling_{part}_{lay}", by_owner, [])

    def begin_exchange(swap, after):
        lay, part, names, handle = swap
        by_owner, theirs = _sibling_end(f"grads_to_sibling_{part}_{lay}", handle, after)
        pair = [_pair_sum(f"pair_sum_{name_}_{lay}", a, b, c_idx) for name_, a, b in zip(names, by_owner, theirs)]
        finish_exchanges(after)
        handle = _chips_begin(f"grads_between_chips_{part}_{lay}", pair, [])
        pending.append((lay, part, names, handle))
        return handle[-1]

    token = None
    small_handle = None
    for l in reversed(range(depth)):
        fw, sv = full[l], saved[l]
        deps = () if token is None else (token, small_handles[l + 1][-1])
        dr2, dr2b, dg2, db2 = _ln_bwd(f"ln2_bwd_{l}", dy, sv["xh2"], sv["rs2"], sv["g2"], deps=deps)
        d_wdown = _mm_tn_acols(f"ffn_down_dw_{l}", sv["act"], dr2b, _WIRE, cap=2 * half_pad)
        dact = _mm_nt(f"ffn_down_dx_{l}", dr2b, fw["wdown"], cap=2 * half_pad)
        dhu, dcw, dcb = _conv_act_bwd(f"ffn_act_bwd_{l}", dact, sv["hu"], fw["cw"], fw["cb"])
        d_wup = _mm(f"ffn_up_dw_{l}", dhu, sv["x1b"], TN, (N_DEV, 1),
                    pl.BlockSpec((None, s, 2 * half_pad), lambda j, kk: (j // 4, 0, j % 4)),
                    _resident((s, d), lambda j, kk: (0, 0)),
                    pl.BlockSpec((2 * half_pad, d), lambda j, kk: (j, 0)), (2 * ffp, d), _WIRE)
        swap = begin_swap(l, "ffn", ("ffn_w_up", "ffn_w_down"),
                          [d_wup.reshape(N_DEV, 2 * half_pad, d), d_wdown.reshape(N_DEV, half_pad, d)])
        dy1 = _mm_split_k(f"ffn_up_dx_{l}", dhu, fw["wup_t"], dr2, alpha, deps=(swap[3][-1],))
        token = begin_exchange(swap, dy1)
        dr1, dr1b, dg1, db1 = _ln_bwd(f"ln1_bwd_{l}", dy1, sv["xh1"], sv["rs1"], sv["g1"], deps=(token,))
        d_wout = _mm_tn_acols(f"out_proj_dw_{l}", sv["mix"], dr1b, _WIRE, cap=d // N_DEV)
        dmix = _mm_nt(f"out_proj_dx_{l}", dr1b, fw["wout"])
        dq_rot, dk_hm, dv_hm, dsk = _attn_bwd(f"attn_bwd_{l}", sv["q_rot"], sv["k_hm"], sv["v_hm"], sv["o_attn"], dmix,
                                             sv["lse"], sv["sinks"])
        dqk = jnp.concatenate([dq_rot, jnp.transpose(dk_hm, (1, 0, 2)).reshape(s, kv_w)], -1)
        dhq, dhk = _rope(f"rope_bwd_{l}", dqk, o_v, cos_t, -sin_t, _MXU, ((0, o_k), (o_k, o_v)))
        dhv = jnp.transpose(dv_hm, (1, 0, 2)).reshape(s, kv_w).astype(_MXU)
        dhp, dpw, dpsc = _pool_bwd(f"pool_bwd_{l}", dmix, attn_w // pool_wd, sv["pre"], sv["pw_b"], sv["psc"])
        dab2, dyg = _glu_bwd(f"glu_bwd_{l}", dmix, (attn_w + pool_wd) // ssm_wd, sv["ab2"], fw["glu"])
        d_glu = _mm_tn_bcols(f"glu_dw_{l}", sv["yg"], dab2, _WIRE)
        cw_ssm = ssm_wd // nt_ssm
        dhs, dd, dcdr, dcdi, dbdr, dbdi, dar, dai = _ssm_bwd(
            f"ssm_bwd_{l}", dyg, sv["ypre"], sv["h"], o_s // cw_ssm, sv["sr"], sv["si"], sv["bdr"], sv["bdi"], sv["cdr"],
            sv["cdi"], sv["dvec"], sv["ar"], sv["ai"])
        dh = jnp.concatenate([dhq, dhk, dhv, dhp, dhs], -1)
        d_win = _mm_tn_acols(f"in_proj_dw_{l}", dh, sv["xb"], _WIRE)
        swap = begin_swap(l, "mixer", ("w_in", "ssm_glu_w", "w_out"),
                          [d_win.reshape(N_DEV, in_w // N_DEV, d),
                           jnp.transpose(d_glu.reshape(ssm_wd, N_DEV, 2 * ssm_wd // N_DEV), (1, 0, 2)),
                           d_wout.reshape(N_DEV, d // N_DEV, d)])
        dy = _mm_nn(f"in_proj_dx_{l}", dh, fw["win_t"], add=dr1, add_scale=alpha, deps=(swap[3][-1],))

        raw = dict(attn_sinks=dsk, pool_w=dpw, pool_scale=dpsc, ssm_lam_re=dar, ssm_lam_im=dai,
                   ssm_b_re=_diagblocks(dbdr, n_groups, SSM_GROUP, SSM_STATE),
                   ssm_b_im=_diagblocks(dbdi, n_groups, SSM_GROUP, SSM_STATE),
                   ssm_c_re=_diagblocks(dcdr, n_groups, SSM_STATE, SSM_GROUP),
                   ssm_c_im=_diagblocks(dcdi, n_groups, SSM_STATE, SSM_GROUP), ssm_d=dd, ln1_g=dg1, ln1_b=db1,
                   ffn_conv_b=_unpad_pairs(dcb.reshape(N_DEV, 2 * half_pad), 1, half, half_pad), ln2_g=dg2, ln2_b=db2,
                   ffn_conv_w=_unpad_pairs(dcw.reshape(CONV_WIDTH, N_DEV, 2 * half_pad), 2, half, half_pad))
        raw_shapes = {k: raw[k].shape for k in _RAW}
        small_handles[l] = _gather_begin(f"gather_small_grads_{l}", [_pack([raw[k] for k in _RAW])], [dy])
        token = begin_exchange(swap, dy)
        if l + 1 < depth:
            small_parts[l + 1] = _gather_end(f"gather_small_grads_{l + 1}", small_handles[l + 1], dy)[0]

    small_parts[0] = _gather_end("gather_small_grads_0", small_handles[0], token)[0]
    rows_l = small_parts[0].shape[1]
    summed = _sum_parts("sum_small_grads", jnp.concatenate(small_parts, 1)).reshape(depth, rows_l, V7X_LANES)
    g_small = dict(zip(_RAW, _unpack(summed, [raw_shapes[k] for k in _RAW])))
    swap_last = lambda t: jnp.transpose(t, (0, 1, 3, 2))
    _, vjp = jax.vjp(_ssm_discretise, *ssm_params)
    dlr, dli, dldt, dbr, dbi = vjp((g_small["ssm_lam_re"].reshape(depth, n_groups, SSM_STATE),
                                    g_small["ssm_lam_im"].reshape(depth, n_groups, SSM_STATE),
                                    swap_last(g_small["ssm_b_re"]), swap_last(g_small["ssm_b_im"])))
    g_small.update(ssm_lam_re=dlr, ssm_lam_im=dli, ssm_log_dt=dldt, ssm_b_re=dbr, ssm_b_im=dbi,
                   ssm_c_re=swap_last(g_small["ssm_c_re"]), ssm_c_im=swap_last(g_small["ssm_c_im"]),
                   ffn_conv_w=lax.dynamic_index_in_dim(g_small["ffn_conv_w"], me, axis=2, keepdims=False))
    for k in _SMALL + ("ffn_conv_w",):
        shp = W[k].shape
        two_d = (math.prod(shp[:-1]), shp[-1])
        res = _adamw_2d(f"adamw_{k}", g_small[k].reshape(two_d), *(t[k].reshape(two_d) for t in (W, M, V)))
        outs[k] = (g_small[k].reshape(shp),) + tuple(a.reshape(shp) for a in res)

    finish_exchanges(outs["ln2_b"][1])
    for name_ in _BIG:
        outs[name_] = tuple(from_groups(name_, t) for t in big_res[name_])

    grad_x = dy[None]
    result = [loss, grad_x]
    for i in range(4):
        result += [outs[k][i] for k in _ORDER]
    return tuple(result)
```

```python
import functools
import math

import jax
import jax.numpy as jnp
from jax import lax
from jax.experimental import pallas as pl
from jax.experimental.pallas import tpu as pltpu

F32 = jnp.float32
_MXU = jnp.bfloat16
_WIRE = jnp.bfloat16

HEAD_DIM = 64
GQA = 4
ATTN_BLOCK = 128
ROPE_THETA = 10000.0
POOL_WINDOWS = (2, 4, 8, 16)
SSM_GROUP = 16
SSM_STATE = 64
CONV_WIDTH = 3
LN_EPS = 1e-5
ADAM_LR, ADAM_B1, ADAM_B2, ADAM_EPS, ADAM_WD, ADAM_STEP = 0.001, 0.9, 0.999, 1e-08, 0.01, 10

N_DEV = 8
V7X_LANES = 128
V7X_VMEM_LIMIT = 56 * 1024 * 1024
SCAN_T = 64
SCAN_LANES = 256
MESH = pl.DeviceIdType.MESH
ANY = pl.BlockSpec(memory_space=pl.ANY)


def _cp():
    return pltpu.CompilerParams(vmem_limit_bytes=V7X_VMEM_LIMIT)


def _resident(block, index_map):
    return pl.BlockSpec(block, index_map, pipeline_mode=pl.Buffered(1))


def _sds(shape, dtype):
    return jax.ShapeDtypeStruct(tuple(shape), dtype)


def _mm(name, a, b, dims, grid, a_spec, b_spec, o_spec, out_shape, out_dtype, add=None, add_spec=None, add_scale=1.0, deps=()):
    nk = grid[1]
    n_in = 2 + (add is not None) + len(deps)
    oblk = tuple(d for d in o_spec.block_shape if d is not None)
    scratch = nk > 1 and out_dtype != F32

    def body(*refs):
        a_ref, b_ref = refs[:2]
        add_ref = None if add is None else refs[2]
        o_ref = refs[n_in]
        acc_ref = refs[-1] if scratch else None

        def finish(r):
            if add_ref is not None:
                r = r + add_scale * add_ref[...]
            o_ref[...] = r.astype(o_ref.dtype)

        part = lax.dot_general(a_ref[...], b_ref[...], (dims, ((), ())), preferred_element_type=F32)
        if nk == 1:
            finish(part)
        elif not scratch:
            k = pl.program_id(1)

            @pl.when(k == 0)
            def _():
                o_ref[...] = part

            @pl.when(k > 0)
            def _():
                o_ref[...] += part

            if add_ref is not None:
                @pl.when(k == nk - 1)
                def _():
                    o_ref[...] += add_scale * add_ref[...]
        else:
            k = pl.program_id(1)

            @pl.when(k == 0)
            def _():
                acc_ref[...] = part

            @pl.when(k > 0)
            def _():
                acc_ref[...] += part

            @pl.when(k == nk - 1)
            def _():
                finish(acc_ref[...])

    ins = [a, b] + ([] if add is None else [add]) + list(deps)
    in_specs = [a_spec, b_spec] + ([] if add is None else [add_spec]) + [ANY] * len(deps)
    return pl.pallas_call(
        body, grid=grid, in_specs=in_specs, out_specs=o_spec, out_shape=_sds(out_shape, out_dtype),
        scratch_shapes=[pltpu.VMEM(oblk, F32)] if scratch else [], compiler_params=_cp(), name=name,
    )(*ins)


NN = ((1,), (0,))
NT = ((1,), (1,))
TN = ((0,), (0,))


def _mm_split_k(name, a2, b, add, add_scale, deps=()):
    _, m, f = a2.shape
    n = b.shape[1]
    tm, tn = m // 2, _col_tile(n, 256)

    def body(a_ref, b_ref, add_ref, *rest):
        o_ref = rest[-1]
        o_ref[...] = (jnp.dot(a_ref[0], b_ref[:f, :], preferred_element_type=F32)
                      + jnp.dot(a_ref[1], b_ref[f:, :], preferred_element_type=F32) + add_scale * add_ref[...])

    tile = pl.BlockSpec((tm, tn), lambda i, j: (i, j))
    return pl.pallas_call(
        body, grid=(m // tm, n // tn),
        in_specs=[_resident((2, tm, f), lambda i, j: (0, i, 0)), pl.BlockSpec((2 * f, tn), lambda i, j: (0, j)), tile]
        + [ANY] * len(deps),
        out_specs=tile, out_shape=_sds((m, n), F32), compiler_params=_cp(), name=name,
    )(a2, b, add, *deps)


def _col_tile(n, cap=512):
    if n % V7X_LANES:
        return n
    t = min(cap, n)
    t -= t % V7X_LANES
    while n % t:
        t -= V7X_LANES
    return t


def _mm_nn(name, a, b, out_dtype=F32, cap=512, add=None, add_scale=1.0, deps=()):
    m, k = a.shape
    n = b.shape[1]
    tn = _col_tile(n, cap)
    o_spec = pl.BlockSpec((m, tn), lambda j, kk: (0, j))
    return _mm(name, a, b, NN, (n // tn, 1), _resident((m, k), lambda j, kk: (0, 0)),
               pl.BlockSpec((k, tn), lambda j, kk: (0, j)), o_spec, (m, n), out_dtype,
               add=add, add_spec=None if add is None else o_spec, add_scale=add_scale, deps=deps)


def _mm_nt(name, a, b, out_dtype=F32, add=None, add_scale=1.0, cap=512, deps=()):
    m, k = a.shape
    n = b.shape[0]
    tn = _col_tile(n, cap)
    o_spec = pl.BlockSpec((m, tn), lambda j, kk: (0, j))
    return _mm(name, a, b, NT, (n // tn, 1), _resident((m, k), lambda j, kk: (0, 0)),
               pl.BlockSpec((tn, k), lambda j, kk: (j, 0)), o_spec, (m, n), out_dtype,
               add=add, add_spec=None if add is None else o_spec, add_scale=add_scale, deps=deps)


def _mm_tn_bcols(name, a, b, out_dtype, cap=512):
    s, k = a.shape
    n = b.shape[1]
    tn = _col_tile(n, cap)
    return _mm(name, a, b, TN, (n // tn, 1), _resident((s, k), lambda j, kk: (0, 0)),
               pl.BlockSpec((s, tn), lambda j, kk: (0, j)), pl.BlockSpec((k, tn), lambda j, kk: (0, j)), (k, n), out_dtype)


def _mm_tn_acols(name, a, b, out_dtype, cap=512):
    s, k = a.shape
    n = b.shape[1]
    tk = _col_tile(k, cap)
    return _mm(name, a, b, TN, (k // tk, 1), pl.BlockSpec((s, tk), lambda i, kk: (0, i)),
               _resident((s, n), lambda i, kk: (0, 0)), pl.BlockSpec((tk, n), lambda i, kk: (i, 0)), (k, n), out_dtype)


def _ln_fwd(name, x, a, g, b, alpha):
    s, d = x.shape
    tr = min(256, s)

    def body(x_ref, a_ref, g_ref, b_ref, y_ref, yb_ref, xh_ref, rs_ref):
        r = alpha * x_ref[...] + a_ref[...]
        mu = jnp.mean(r, -1, keepdims=True)
        c = r - mu
        var = jnp.mean(c * c, -1, keepdims=True)
        rstd = lax.rsqrt(var + LN_EPS)
        xh = c * rstd
        y = xh * g_ref[...] + b_ref[...]
        y_ref[...] = y
        yb_ref[...] = y.astype(_MXU)
        xh_ref[...] = xh
        rs_ref[...] = rstd

    row = pl.BlockSpec((tr, d), lambda i: (i, 0))
    vec = pl.BlockSpec((1, d), lambda i: (0, 0))
    return pl.pallas_call(
        body, grid=(s // tr,), in_specs=[row, row, vec, vec],
        out_specs=[row, row, row, pl.BlockSpec((tr, 1), lambda i: (i, 0))],
        out_shape=[_sds((s, d), F32), _sds((s, d), _MXU), _sds((s, d), F32), _sds((s, 1), F32)],
        compiler_params=_cp(), name=name,
    )(x, a, g, b)


def _ln_bwd(name, dy, xh, rstd, g, deps=()):
    s, d = dy.shape
    tr = min(256, s)
    nd = len(deps)

    def body(dy_ref, xh_ref, rs_ref, g_ref, *rest):
        dr_ref, drb_ref, dg_ref, db_ref = rest[nd:]
        i = pl.program_id(0)
        dy_ = dy_ref[...]
        xh_ = xh_ref[...]
        dxh = dy_ * g_ref[...]
        m1 = jnp.mean(dxh, -1, keepdims=True)
        m2 = jnp.mean(dxh * xh_, -1, keepdims=True)
        dr = rs_ref[...] * (dxh - m1 - xh_ * m2)
        dr_ref[...] = dr
        drb_ref[...] = dr.astype(_MXU)
        pg = jnp.sum(dy_ * xh_, 0, keepdims=True)
        pb = jnp.sum(dy_, 0, keepdims=True)

        @pl.when(i == 0)
        def _():
            dg_ref[...] = pg
            db_ref[...] = pb

        @pl.when(i > 0)
        def _():
            dg_ref[...] += pg
            db_ref[...] += pb

    row = pl.BlockSpec((tr, d), lambda i: (i, 0))
    vec = pl.BlockSpec((1, d), lambda i: (0, 0))
    return pl.pallas_call(
        body, grid=(s // tr,), in_specs=[row, row, pl.BlockSpec((tr, 1), lambda i: (i, 0)), vec] + [ANY] * nd,
        out_specs=[row, row, vec, vec],
        out_shape=[_sds((s, d), F32), _sds((s, d), _MXU), _sds((1, d), F32), _sds((1, d), F32)],
        compiler_params=_cp(), name=name,
    )(dy, xh, rstd, g, *deps)


def _loss_head(name, y, target):
    s, d = y.shape
    tr = min(256, s)

    def body(y_ref, t_ref, dy_ref, l_ref):
        i = pl.program_id(0)
        e = y_ref[...] - t_ref[...]
        dy_ref[...] = e * (1.0 / d)
        part = 0.5 * jnp.sum(jnp.mean(e * e, -1, keepdims=True), 0, keepdims=True)

        @pl.when(i == 0)
        def _():
            l_ref[...] = part

        @pl.when(i > 0)
        def _():
            l_ref[...] += part

    row = pl.BlockSpec((tr, d), lambda i: (i, 0))
    return pl.pallas_call(
        body, grid=(s // tr,), in_specs=[row, row], out_specs=[row, pl.BlockSpec((1, 1), lambda i: (0, 0))],
        out_shape=[_sds((s, d), F32), _sds((1, 1), F32)], compiler_params=_cp(), name=name,
    )(y, target)


def _rope(name, t, width, cos, sin, out_dtype, splits):
    s = t.shape[0]
    tr = min(256, s)
    assert width % V7X_LANES == 0

    def body(t_ref, c_ref, s_ref, *o_refs):
        lane = lax.broadcasted_iota(jnp.int32, (tr, V7X_LANES), 1)
        first = (lane % HEAD_DIM) < (HEAD_DIM // 2)
        cs, sn = c_ref[...], s_ref[...]
        for (lo, hi), o_ref in zip(splits, o_refs):
            for c0 in range(lo, hi, V7X_LANES):
                v = t_ref[:, c0:c0 + V7X_LANES].astype(F32)
                partner = jnp.where(first, pltpu.roll(v, V7X_LANES - HEAD_DIM // 2, 1), pltpu.roll(v, HEAD_DIM // 2, 1))
                o_ref[:, c0 - lo:c0 - lo + V7X_LANES] = (v * cs + partner * sn).astype(o_ref.dtype)

    tab = pl.BlockSpec((tr, V7X_LANES), lambda i: (i, 0))
    return pl.pallas_call(
        body, grid=(s // tr,), in_specs=[pl.BlockSpec((tr, width), lambda i: (i, 0)), tab, tab],
        out_specs=[pl.BlockSpec((tr, hi - lo), lambda i: (i, 0)) for lo, hi in splits],
        out_shape=[_sds((s, hi - lo), out_dtype) for lo, hi in splits], compiler_params=_cp(), name=name,
    )(t, cos, sin)


def _attn_masks():
    i = lax.broadcasted_iota(jnp.int32, (GQA * ATTN_BLOCK, 2 * ATTN_BLOCK), 0) % ATTN_BLOCK
    j = lax.broadcasted_iota(jnp.int32, (GQA * ATTN_BLOCK, 2 * ATTN_BLOCK), 1)
    cur_ok = jnp.logical_and(j >= ATTN_BLOCK, j - ATTN_BLOCK <= i)
    prev_ok = jnp.logical_and(j < ATTN_BLOCK, j > i)
    return cur_ok, prev_ok


def _attn_scores(q4, kcat, n, cur_ok, prev_ok):
    sc = lax.dot_general(q4, kcat, (NT, ((), ())), preferred_element_type=F32) * (HEAD_DIM ** -0.5)
    return jnp.where(jnp.logical_or(cur_ok, jnp.logical_and(prev_ok, n > 0)), sc, -1e30)


def _stack_heads(ref, rows):
    return jnp.concatenate([ref[rows, g * HEAD_DIM:(g + 1) * HEAD_DIM] for g in range(GQA)], 0)


def _per_head_column(values):
    r = lax.broadcasted_iota(jnp.int32, (GQA * ATTN_BLOCK, 1), 0) // ATTN_BLOCK
    col = jnp.zeros((GQA * ATTN_BLOCK, 1), F32)
    for g, val in enumerate(values):
        col = jnp.where(r == g, val, col)
    return col


def _attn_fwd(name, q, k, v, sinks):
    s = q.shape[0]
    nkv = k.shape[0]
    gw = GQA * HEAD_DIM
    nb = s // ATTN_BLOCK

    def body(sk_ref, q_ref, k_ref, v_ref, o_ref, lse_ref):
        h = pl.program_id(0)
        cur_ok, prev_ok = _attn_masks()
        sink = _per_head_column([sk_ref[h, g] for g in range(GQA)])

        def blk(n, carry):
            rows = pl.ds(pl.multiple_of(n * ATTN_BLOCK, ATTN_BLOCK), ATTN_BLOCK)
            prows = pl.ds(pl.multiple_of(jnp.maximum(n - 1, 0) * ATTN_BLOCK, ATTN_BLOCK), ATTN_BLOCK)
            kcat = jnp.concatenate([k_ref[prows, :], k_ref[rows, :]], 0)
            vcat = jnp.concatenate([v_ref[prows, :], v_ref[rows, :]], 0)
            sc = _attn_scores(_stack_heads(q_ref, rows), kcat, n, cur_ok, prev_ok)
            m = jnp.maximum(sc.max(-1, keepdims=True), sink)
            p = jnp.exp(sc - m)
            den = p.sum(-1, keepdims=True) + jnp.exp(sink - m)
            o = jnp.dot((p / den).astype(_MXU), vcat, preferred_element_type=F32)
            lse = m + jnp.log(den)
            for g in range(GQA):
                mine = slice(g * ATTN_BLOCK, (g + 1) * ATTN_BLOCK)
                o_ref[rows, g * HEAD_DIM:(g + 1) * HEAD_DIM] = o[mine, :]
                lse_ref[rows, g:g + 1] = lse[mine, :]
            return carry

        lax.fori_loop(0, nb, blk, 0)

    kv_spec = pl.BlockSpec((None, s, HEAD_DIM), lambda h: (h, 0, 0))
    return pl.pallas_call(
        body, grid=(nkv,),
        in_specs=[pl.BlockSpec(memory_space=pltpu.SMEM), pl.BlockSpec((s, gw), lambda h: (0, h)), kv_spec, kv_spec],
        out_specs=[pl.BlockSpec((s, gw), lambda h: (0, h)), pl.BlockSpec((None, s, GQA), lambda h: (h, 0, 0))],
        out_shape=[_sds((s, nkv * gw), F32), _sds((nkv, s, GQA), F32)], compiler_params=_cp(), name=name,
    )(sinks, q, k, v)


def _attn_bwd(name, q, k, v, o, dmix, lse, sinks):
    s = q.shape[0]
    nkv = k.shape[0]
    gw = GQA * HEAD_DIM
    nb = s // ATTN_BLOCK
    scale = HEAD_DIM ** -0.5

    def body(sk_ref, q_ref, k_ref, v_ref, o_ref, do_ref, lse_ref, dq_ref, dk_ref, dv_ref, dsk_ref):
        h = pl.program_id(0)
        cur_ok, prev_ok = _attn_masks()
        dk_ref[...] = jnp.zeros_like(dk_ref)
        dv_ref[...] = jnp.zeros_like(dv_ref)

        sink = _per_head_column([sk_ref[h, g] for g in range(GQA)])

        def blk(n, acc):
            rows = pl.ds(pl.multiple_of(n * ATTN_BLOCK, ATTN_BLOCK), ATTN_BLOCK)
            prows = pl.ds(pl.multiple_of(jnp.maximum(n - 1, 0) * ATTN_BLOCK, ATTN_BLOCK), ATTN_BLOCK)
            kcat = jnp.concatenate([k_ref[prows, :], k_ref[rows, :]], 0)
            vcat = jnp.concatenate([v_ref[prows, :], v_ref[rows, :]], 0)
            q4 = _stack_heads(q_ref, rows)
            do4 = _stack_heads(do_ref, rows)
            delta = jnp.sum(do4 * _stack_heads(o_ref, rows), -1, keepdims=True)
            dob = do4.astype(_MXU)
            lse = jnp.concatenate([lse_ref[rows, g:g + 1] for g in range(GQA)], 0)
            p = jnp.exp(_attn_scores(q4, kcat, n, cur_ok, prev_ok) - lse)
            dp = lax.dot_general(dob, vcat, (NT, ((), ())), preferred_element_type=F32)
            ds = (p * (dp - delta) * scale).astype(_MXU)
            dq = jnp.dot(ds, kcat, preferred_element_type=F32)
            for g in range(GQA):
                dq_ref[rows, g * HEAD_DIM:(g + 1) * HEAD_DIM] = dq[g * ATTN_BLOCK:(g + 1) * ATTN_BLOCK, :]
            dk = lax.dot_general(ds, q4, (TN, ((), ())), preferred_element_type=F32)
            dv = lax.dot_general(p.astype(_MXU), dob, (TN, ((), ())), preferred_element_type=F32)
            dk_ref[prows, :] += dk[:ATTN_BLOCK, :]
            dv_ref[prows, :] += dv[:ATTN_BLOCK, :]
            dk_ref[rows, :] += dk[ATTN_BLOCK:, :]
            dv_ref[rows, :] += dv[ATTN_BLOCK:, :]
            return acc - jnp.exp(sink - lse) * delta

        acc = lax.fori_loop(0, nb, blk, jnp.zeros((GQA * ATTN_BLOCK, 1), F32))
        for g in range(GQA):
            dsk_ref[:, g:g + 1] = jnp.sum(acc[g * ATTN_BLOCK:(g + 1) * ATTN_BLOCK, :], 0, keepdims=True)

    kv_spec = pl.BlockSpec((None, s, HEAD_DIM), lambda h: (h, 0, 0))
    qcols = pl.BlockSpec((s, gw), lambda h: (0, h))
    return pl.pallas_call(
        body, grid=(nkv,),
        in_specs=[pl.BlockSpec(memory_space=pltpu.SMEM), qcols, kv_spec, kv_spec, qcols, qcols,
                  pl.BlockSpec((None, s, GQA), lambda h: (h, 0, 0))],
        out_specs=[qcols, kv_spec, kv_spec, pl.BlockSpec((None, 1, GQA), lambda h: (h, 0, 0))],
        out_shape=[_sds((s, nkv * gw), F32), _sds((nkv, s, HEAD_DIM), F32), _sds((nkv, s, HEAD_DIM), F32),
                   _sds((nkv, 1, GQA), F32)],
        compiler_params=_cp(), name=name,
    )(sinks, q, k, v, o, dmix, lse)


def _shift_down(a, k, t):
    return jnp.where(t >= k, pltpu.roll(a, k, 0), 0.0)


def _shift_up(a, k, t):
    n = a.shape[0]
    return jnp.where(t < n - k, pltpu.roll(a, n - k, 0), 0.0)


def _pool_fwd(name, h, col_block, pool_w, pool_scale):
    s = h.shape[0]
    ng, pg = pool_w.shape[0], pool_w.shape[1]
    pw_ = ng * pg

    def body(u_ref, w_ref, sc_ref, y_ref, pre_ref):
        t = lax.broadcasted_iota(jnp.int32, (s, pg), 0)
        for gi, win in enumerate(POOL_WINDOWS):
            cols = slice(gi * pg, (gi + 1) * pg)
            u = u_ref[:, cols]
            a = u
            k = 1
            while k < win:
                a = a + _shift_down(a, k, t)
                k *= 2
            div = jnp.minimum(t + 1, win).astype(F32)
            pre = (a / div - u).astype(_MXU)
            pre_ref[:, cols] = pre
            out = jnp.dot(pre, w_ref[gi], preferred_element_type=F32)
            y_ref[:, cols] = (out * sc_ref[:, cols]).astype(y_ref.dtype)

    blk = pl.BlockSpec((s, pw_), lambda i: (0, 0))
    return pl.pallas_call(
        body, grid=(1,),
        in_specs=[pl.BlockSpec((s, pw_), lambda i: (0, col_block)), pl.BlockSpec((ng, pg, pg), lambda i: (0, 0, 0)),
                  pl.BlockSpec((1, pw_), lambda i: (0, 0))],
        out_specs=[blk, blk], out_shape=[_sds((s, pw_), _MXU), _sds((s, pw_), _MXU)], compiler_params=_cp(), name=name,
    )(h, pool_w, pool_scale)


def _pool_bwd(name, dmix, col_block, pre, pool_w, pool_scale):
    s = pre.shape[0]
    ng, pg = pool_w.shape[0], pool_w.shape[1]
    pw_ = ng * pg

    def body(dy_ref, pre_ref, w_ref, sc_ref, du_ref, dw_ref, dsc_ref):
        t = lax.broadcasted_iota(jnp.int32, (s, pg), 0)
        for gi, win in enumerate(POOL_WINDOWS):
            cols = slice(gi * pg, (gi + 1) * pg)
            pre_g = pre_ref[:, cols]
            dy = dy_ref[:, cols]
            out = jnp.dot(pre_g, w_ref[gi], preferred_element_type=F32)
            dsc_ref[:, cols] = jnp.sum(dy * out, 0, keepdims=True)
            dout = (dy * sc_ref[:, cols]).astype(_MXU)
            dw_ref[gi] = lax.dot_general(pre_g, dout, (TN, ((), ())), preferred_element_type=F32)
            dpre = lax.dot_general(dout, w_ref[gi], (NT, ((), ())), preferred_element_type=F32)
            div = jnp.minimum(t + 1, win).astype(F32)
            a = dpre / div
            k = 1
            while k < win:
                a = a + _shift_up(a, k, t)
                k *= 2
            du_ref[:, cols] = (a - dpre).astype(du_ref.dtype)

    blk = pl.BlockSpec((s, pw_), lambda i: (0, 0))
    wspec = pl.BlockSpec((ng, pg, pg), lambda i: (0, 0, 0))
    vec = pl.BlockSpec((1, pw_), lambda i: (0, 0))
    return pl.pallas_call(
        body, grid=(1,), in_specs=[pl.BlockSpec((s, pw_), lambda i: (0, col_block)), blk, wspec, vec],
        out_specs=[blk, wspec, vec], out_shape=[_sds((s, pw_), _MXU), _sds((ng, pg, pg), F32), _sds((1, pw_), F32)],
        compiler_params=_cp(), name=name,
    )(dmix, pre, pool_w, pool_scale)


def _scan_chunks(xr_ref, xi_ref, sr_ref, si_ref, ar, ai, reverse):
    n, c = xr_ref.shape
    tt = min(SCAN_T, n)
    lw = min(SCAN_LANES, c)
    nchunk, ngroup = n // tt, tt // 8
    t8 = lax.broadcasted_iota(jnp.int32, (tt, lw), 0) % 8

    for l0 in range(0, c, lw):
        lanes = slice(l0, l0 + lw)
        a_r, a_i = ar[:, lanes], ai[:, lanes]

        def within8(vr, vi, a_r=a_r, a_i=a_i):
            rows = vr.shape[0]
            tq = t8[:rows, :]
            pr, pi = a_r, a_i
            k = 1
            while k < 8:
                if reverse:
                    hr = jnp.where(tq < 8 - k, pltpu.roll(vr, rows - k, 0), 0.0)
                    hi = jnp.where(tq < 8 - k, pltpu.roll(vi, rows - k, 0), 0.0)
                else:
                    hr = jnp.where(tq >= k, pltpu.roll(vr, k, 0), 0.0)
                    hi = jnp.where(tq >= k, pltpu.roll(vi, k, 0), 0.0)
                vr, vi = vr + pr * hr - pi * hi, vi + pr * hi + pi * hr
                pr, pi = pr * pr - pi * pi, 2.0 * pr * pi
                k *= 2
            return vr, vi

        at_edge = t8[:8, :] == (7 if reverse else 0)
        pw_r, pw_i = within8(jnp.where(at_edge, a_r, 0.0), jnp.where(at_edge, a_i, 0.0))
        last = 0 if reverse else 7

        def body(i, carry, lanes=lanes, within8=within8, pw_r=pw_r, pw_i=pw_i):
            cr, ci = carry
            ch = nchunk - 1 - i if reverse else i
            rows = pl.ds(pl.multiple_of(ch * tt, tt), tt)
            vr, vi = within8(xr_ref[rows, lanes], xi_ref[rows, lanes])
            out_r, out_i = [None] * ngroup, [None] * ngroup
            for g in (reversed(range(ngroup)) if reverse else range(ngroup)):
                br = vr[8 * g:8 * g + 8, :] + pw_r * cr - pw_i * ci
                bi = vi[8 * g:8 * g + 8, :] + pw_r * ci + pw_i * cr
                out_r[g], out_i[g] = br, bi
                cr, ci = br[last:last + 1, :], bi[last:last + 1, :]
            sr_ref[rows, lanes] = jnp.concatenate(out_r, 0)
            si_ref[rows, lanes] = jnp.concatenate(out_i, 0)
            return cr, ci

        lax.fori_loop(0, nchunk, body, (jnp.zeros((1, lw), F32), jnp.zeros((1, lw), F32)))


_GELU_K = math.sqrt(2.0 / math.pi)


def _gelu_grad(y):
    inner = _GELU_K * (y + 0.044715 * y * y * y)
    th = jnp.tanh(inner)
    return 0.5 * (1.0 + th) + 0.5 * y * (1.0 - th * th) * _GELU_K * (1.0 + 3.0 * 0.044715 * y * y)


def _ssm_fwd(name, h, u_block0, bdr, bdi, cdr, cdi, dvec, ar, ai):
    s = h.shape[0]
    nt, cw, lw = bdr.shape
    rc = min(256, s)

    def body(u_ref, bdr_ref, bdi_ref, cdr_ref, cdi_ref, d_ref, ar_ref, ai_ref, sr_ref, si_ref, y_ref, yg_ref):
        def mm_in(c, _):
            rows = pl.ds(pl.multiple_of(c * rc, rc), rc)
            ub = u_ref[rows, :].astype(_MXU)
            sr_ref[rows, :] = jnp.dot(ub, bdr_ref[...], preferred_element_type=F32)
            si_ref[rows, :] = jnp.dot(ub, bdi_ref[...], preferred_element_type=F32)
            return 0

        lax.fori_loop(0, s // rc, mm_in, 0)
        _scan_chunks(sr_ref, si_ref, sr_ref, si_ref, ar_ref[...], ai_ref[...], reverse=False)

        def mm_out(c, _):
            rows = pl.ds(pl.multiple_of(c * rc, rc), rc)
            y = (jnp.dot(sr_ref[rows, :].astype(_MXU), cdr_ref[...], preferred_element_type=F32)
                 - jnp.dot(si_ref[rows, :].astype(_MXU), cdi_ref[...], preferred_element_type=F32)
                 + d_ref[...] * u_ref[rows, :])
            y_ref[rows, :] = y
            yg_ref[rows, :] = jax.nn.gelu(y).astype(yg_ref.dtype)
            return 0

        lax.fori_loop(0, s // rc, mm_out, 0)

    st = pl.BlockSpec((s, lw), lambda j: (0, j))
    ch = pl.BlockSpec((s, cw), lambda j: (0, j))
    bspec = pl.BlockSpec((None, cw, lw), lambda j: (j, 0, 0))
    cspec = pl.BlockSpec((None, lw, cw), lambda j: (j, 0, 0))
    return pl.pallas_call(
        body, grid=(nt,),
        in_specs=[pl.BlockSpec((s, cw), lambda j: (0, u_block0 + j)), bspec, bspec, cspec, cspec,
                  pl.BlockSpec((1, cw), lambda j: (0, j)), pl.BlockSpec((1, lw), lambda j: (0, j)),
                  pl.BlockSpec((1, lw), lambda j: (0, j))],
        out_specs=[st, st, ch, ch],
        out_shape=[_sds((s, nt * lw), F32), _sds((s, nt * lw), F32), _sds((s, nt * cw), F32), _sds((s, nt * cw), _MXU)],
        compiler_params=_cp(), name=name,
    )(h, bdr, bdi, cdr, cdi, dvec, ar, ai)


def _ssm_bwd(name, dyg, ypre, h, u_block0, sr, si, bdr, bdi, cdr, cdi, dvec, ar, ai):
    s = h.shape[0]
    nt, cw, lw = bdr.shape
    rc = min(256, s)

    def body(dyg_ref, yp_ref, u_ref, sr_ref, si_ref, bdr_ref, bdi_ref, cdr_ref, cdi_ref, d_ref, ar_ref, ai_ref,
             du_ref, dd_ref, dcr_ref, dci_ref, dbr_ref, dbi_ref, dar_ref, dai_ref, lr_scr, li_scr, dy_scr):
        for ref in (dd_ref, dcr_ref, dci_ref, dbr_ref, dbi_ref, dar_ref, dai_ref):
            ref[...] = jnp.zeros_like(ref)

        def p1(c, _):
            rows = pl.ds(pl.multiple_of(c * rc, rc), rc)
            dy = dyg_ref[rows, :] * _gelu_grad(yp_ref[rows, :])
            dy_scr[rows, :] = dy
            dd_ref[...] += jnp.sum(dy * u_ref[rows, :], 0, keepdims=True)
            dyb = dy.astype(_MXU)
            lr_scr[rows, :] = lax.dot_general(dyb, cdr_ref[...], (NT, ((), ())), preferred_element_type=F32)
            li_scr[rows, :] = -lax.dot_general(dyb, cdi_ref[...], (NT, ((), ())), preferred_element_type=F32)
            dcr_ref[...] += lax.dot_general(sr_ref[rows, :].astype(_MXU), dyb, (TN, ((), ())), preferred_element_type=F32)
            dci_ref[...] -= lax.dot_general(si_ref[rows, :].astype(_MXU), dyb, (TN, ((), ())), preferred_element_type=F32)
            return 0

        lax.fori_loop(0, s // rc, p1, 0)
        _scan_chunks(lr_scr, li_scr, lr_scr, li_scr, ar_ref[...], -ai_ref[...], reverse=True)
        t = lax.broadcasted_iota(jnp.int32, (rc, lw), 0)

        def p2(c, _):
            r0 = pl.multiple_of(c * rc, rc)
            rows = pl.ds(r0, rc)
            before = pl.ds(pl.multiple_of(jnp.maximum(r0 - 8, 0), 8), 8)
            have = (c > 0).astype(F32)
            lr, li = lr_scr[rows, :], li_scr[rows, :]
            spr = jnp.where(t == 0, sr_ref[before, :][7:8, :] * have, pltpu.roll(sr_ref[rows, :], 1, 0))
            spi = jnp.where(t == 0, si_ref[before, :][7:8, :] * have, pltpu.roll(si_ref[rows, :], 1, 0))
            dar_ref[...] += jnp.sum(lr * spr + li * spi, 0, keepdims=True)
            dai_ref[...] += jnp.sum(li * spr - lr * spi, 0, keepdims=True)
            lrb, lib = lr.astype(_MXU), li.astype(_MXU)
            du = (dy_scr[rows, :] * d_ref[...]
                  + lax.dot_general(lrb, bdr_ref[...], (NT, ((), ())), preferred_element_type=F32)
                  + lax.dot_general(lib, bdi_ref[...], (NT, ((), ())), preferred_element_type=F32))
            du_ref[rows, :] = du.astype(du_ref.dtype)
            ub = u_ref[rows, :].astype(_MXU)
            dbr_ref[...] += lax.dot_general(ub, lrb, (TN, ((), ())), preferred_element_type=F32)
            dbi_ref[...] += lax.dot_general(ub, lib, (TN, ((), ())), preferred_element_type=F32)
            return 0

        lax.fori_loop(0, s // rc, p2, 0)

    st = pl.BlockSpec((s, lw), lambda j: (0, j))
    ch = pl.BlockSpec((s, cw), lambda j: (0, j))
    bspec = pl.BlockSpec((None, cw, lw), lambda j: (j, 0, 0))
    cspec = pl.BlockSpec((None, lw, cw), lambda j: (j, 0, 0))
    cvec = pl.BlockSpec((1, cw), lambda j: (0, j))
    svec = pl.BlockSpec((1, lw), lambda j: (0, j))
    return pl.pallas_call(
        body, grid=(nt,),
        in_specs=[ch, ch, pl.BlockSpec((s, cw), lambda j: (0, u_block0 + j)), st, st, bspec, bspec, cspec, cspec, cvec, svec, svec],
        out_specs=[ch, cvec, cspec, cspec, bspec, bspec, svec, svec],
        out_shape=[_sds((s, nt * cw), _MXU), _sds((1, nt * cw), F32), _sds((nt, lw, cw), F32), _sds((nt, lw, cw), F32),
                   _sds((nt, cw, lw), F32), _sds((nt, cw, lw), F32), _sds((1, nt * lw), F32), _sds((1, nt * lw), F32)],
        scratch_shapes=[pltpu.VMEM((s, lw), F32), pltpu.VMEM((s, lw), F32), pltpu.VMEM((s, cw), F32)],
        compiler_params=_cp(), name=name,
    )(dyg, ypre, h, sr, si, bdr, bdi, cdr, cdi, dvec, ar, ai)


def _glu_fwd(name, yg, gw):
    s, w = yg.shape
    tr = min(512, s)

    def body(y_ref, w_ref, o_ref, ab_ref):
        ab = jnp.dot(y_ref[...], w_ref[...], preferred_element_type=F32)
        ab_ref[...] = ab
        o_ref[...] = (ab[:, :w] * jax.nn.sigmoid(ab[:, w:])).astype(o_ref.dtype)

    return pl.pallas_call(
        body, grid=(s // tr,), in_specs=[pl.BlockSpec((tr, w), lambda i: (i, 0)), _resident((w, 2 * w), lambda i: (0, 0))],
        out_specs=[pl.BlockSpec((tr, w), lambda i: (i, 0)), pl.BlockSpec((tr, 2 * w), lambda i: (i, 0))],
        out_shape=[_sds((s, w), _MXU), _sds((s, 2 * w), F32)], compiler_params=_cp(), name=name,
    )(yg, gw)


def _glu_bwd(name, dmix, col_block, ab, gw):
    s = ab.shape[0]
    w = ab.shape[1] // 2
    tr = min(512, s)

    def body(do_ref, ab_ref, w_ref, dab_ref, dy_ref):
        do = do_ref[...]
        a, b = ab_ref[:, :w], ab_ref[:, w:]
        sg = jax.nn.sigmoid(b)
        da = (do * sg).astype(_MXU)
        db = (do * a * sg * (1.0 - sg)).astype(_MXU)
        dab_ref[:, :w] = da
        dab_ref[:, w:] = db
        dy_ref[...] = (lax.dot_general(da, w_ref[:, :w], (NT, ((), ())), preferred_element_type=F32)
                       + lax.dot_general(db, w_ref[:, w:], (NT, ((), ())), preferred_element_type=F32))

    return pl.pallas_call(
        body, grid=(s // tr,),
        in_specs=[pl.BlockSpec((tr, w), lambda i: (i, col_block)), pl.BlockSpec((tr, 2 * w), lambda i: (i, 0)),
                  _resident((w, 2 * w), lambda i: (0, 0))],
        out_specs=[pl.BlockSpec((tr, 2 * w), lambda i: (i, 0)), pl.BlockSpec((tr, w), lambda i: (i, 0))],
        out_shape=[_sds((s, 2 * w), _MXU), _sds((s, w), F32)], compiler_params=_cp(), name=name,
    )(dmix, ab, gw)


CONV_ROWS = 64


def _conv_chunk(ref, w_ref, b_ref, c, tt):
    r0 = pl.multiple_of(c * tt, tt)
    before = ref[pl.ds(pl.multiple_of(jnp.maximum(r0 - 8, 0), 8), 8), :]
    before = jnp.where(c > 0, before, 0.0)
    main = ref[pl.ds(r0, tt), :]
    ext = jnp.concatenate([before, main], 0)
    d1 = pltpu.roll(ext, 1, 0)[8:, :]
    d2 = pltpu.roll(ext, 2, 0)[8:, :]
    hc = b_ref[...] + d2 * w_ref[0:1, :]
    hc = hc + d1 * w_ref[1:2, :]
    return hc + main * w_ref[2:3, :], main, d1, d2


def _conv_act_fwd(name, hu, cw, cb):
    s, f2 = hu.shape
    f = f2 // 2
    tw = _col_tile(f, 256)
    nt = f // tw

    tt = min(CONV_ROWS, s)

    def body(v_ref, g_ref, wv_ref, wg_ref, bv_ref, bg_ref, act_ref):
        def chunk(c, _):
            val = _conv_chunk(v_ref, wv_ref, bv_ref, c, tt)[0]
            gate = _conv_chunk(g_ref, wg_ref, bg_ref, c, tt)[0]
            act_ref[pl.ds(pl.multiple_of(c * tt, tt), tt), :] = (jax.nn.silu(gate) * val).astype(act_ref.dtype)
            return 0

        lax.fori_loop(0, s // tt, chunk, 0)

    cv = lambda rows: pl.BlockSpec((rows, tw), lambda i: (0, i))
    cg = lambda rows: pl.BlockSpec((rows, tw), lambda i: (0, nt + i))
    return pl.pallas_call(
        body, grid=(nt,), in_specs=[cv(s), cg(s), cv(CONV_WIDTH), cg(CONV_WIDTH), cv(1), cg(1)],
        out_specs=cv(s), out_shape=_sds((s, f), _MXU), compiler_params=_cp(), name=name,
    )(hu, hu, cw, cw, cb, cb)


def _conv_act_bwd(name, dact, hu, cw, cb):
    s, f2 = hu.shape
    f = f2 // 2
    tw = _col_tile(f, 256)
    nt = f // tw

    tt = min(CONV_ROWS, s)
    nchunk = s // tt

    def body(da_ref, v_ref, g_ref, wv_ref, wg_ref, bv_ref, bg_ref, dh_ref, dwv_ref, dwg_ref, dbv_ref, dbg_ref):
        def chunk(i, carry):
            c = nchunk - 1 - i
            rows = pl.ds(pl.multiple_of(c * tt, tt), tt)
            val, hv, hv1, hv2 = _conv_chunk(v_ref, wv_ref, bv_ref, c, tt)
            gate, hg, hg1, hg2 = _conv_chunk(g_ref, wg_ref, bg_ref, c, tt)
            sg = jax.nn.sigmoid(gate)
            da = da_ref[rows, :]
            dval = da * (gate * sg)
            dgate = da * val * sg * (1.0 + gate * (1.0 - sg))
            out = []
            for part, dhc, taps, w_ref, (after, acc) in ((0, dval, (hv2, hv1, hv), wv_ref, carry[0]),
                                                        (1, dgate, (hg2, hg1, hg), wg_ref, carry[1])):
                ext = jnp.concatenate([dhc, after], 0)
                u1 = pltpu.roll(ext, tt + 8 - 1, 0)[:tt, :]
                u2 = pltpu.roll(ext, tt + 8 - 2, 0)[:tt, :]
                dh = dhc * w_ref[2:3, :] + u1 * w_ref[1:2, :] + u2 * w_ref[0:1, :]
                dh_ref[part, rows, :] = dh.astype(dh_ref.dtype)
                sums = [jnp.sum((dhc * tap).reshape(tt // 8, 8, tw), 0) for tap in taps]
                sums.append(jnp.sum(dhc.reshape(tt // 8, 8, tw), 0))
                out.append((dhc[0:8, :], tuple(a + b for a, b in zip(acc, sums))))
            return tuple(out)

        zero = (jnp.zeros((8, tw), F32), tuple(jnp.zeros((8, tw), F32) for _ in range(CONV_WIDTH + 1)))
        (_, acc_v), (_, acc_g) = lax.fori_loop(0, nchunk, chunk, (zero, zero))
        for acc, dw_ref, db_ref in ((acc_v, dwv_ref, dbv_ref), (acc_g, dwg_ref, dbg_ref)):
            for tap in range(CONV_WIDTH):
                dw_ref[tap:tap + 1, :] = jnp.sum(acc[tap], 0, keepdims=True)
            db_ref[...] = jnp.sum(acc[CONV_WIDTH], 0, keepdims=True)

    cv = lambda rows: pl.BlockSpec((rows, tw), lambda i: (0, i))
    cg = lambda rows: pl.BlockSpec((rows, tw), lambda i: (0, nt + i))
    both = pl.BlockSpec((2, s, tw), lambda i: (0, 0, i))
    dh, dwv, dwg, dbv, dbg = pl.pallas_call(
        body, grid=(nt,), in_specs=[cv(s), cv(s), cg(s), cv(CONV_WIDTH), cg(CONV_WIDTH), cv(1), cg(1)],
        out_specs=[both, cv(CONV_WIDTH), cv(CONV_WIDTH), cv(1), cv(1)],
        out_shape=[_sds((2, s, f), _MXU), _sds((CONV_WIDTH, f), F32), _sds((CONV_WIDTH, f), F32),
                   _sds((1, f), F32), _sds((1, f), F32)],
        compiler_params=_cp(), name=name,
    )(dact, hu, hu, cw, cw, cb, cb)
    return dh, jnp.concatenate([dwv, dwg], 1), jnp.concatenate([dbv, dbg], 1)


ELEM_BLOCK = 512 * 1024


def _elem_tiles(r, c, budget=ELEM_BLOCK):
    rows = [t for t in range(8, r + 1, 8) if r % t == 0] or [r]
    cols = [t for t in range(V7X_LANES, c + 1, V7X_LANES) if c % t == 0] or [c]
    fits = [(tr * tc, tc, tr) for tr in rows for tc in cols if tr * tc <= budget]
    if not fits:
        return min(rows), min(cols)
    _, tc, tr = max(fits)
    return tr, tc


def _sum_parts(name, parts):
    n, r, c = parts.shape
    tr, tc = _elem_tiles(r, c, ELEM_BLOCK // n)

    def body(p_ref, g_ref):
        g = p_ref[0]
        for i in range(1, n):
            g = g + p_ref[i]
        g_ref[...] = g

    return pl.pallas_call(
        body, grid=(r // tr, c // tc), in_specs=[pl.BlockSpec((n, tr, tc), lambda i, j: (0, i, j))],
        out_specs=pl.BlockSpec((tr, tc), lambda i, j: (i, j)), out_shape=_sds((r, c), F32), compiler_params=_cp(), name=name,
    )(parts)


def _adamw_2d(name, g, w, m, v):
    r, c = w.shape
    tc = c if c % V7X_LANES else _col_tile(c, 2048)
    rows = [t for t in range(8, r + 1, 8) if r % t == 0 and t * max(tc, V7X_LANES) <= ELEM_BLOCK // 4] or [r]
    tr = max(rows)
    c1 = 1.0 - ADAM_B1 ** ADAM_STEP
    c2 = 1.0 - ADAM_B2 ** ADAM_STEP

    def body(g_ref, w_ref, m_ref, v_ref, d_ref, nm_ref, nv_ref):
        g = g_ref[...]
        nm = ADAM_B1 * m_ref[...] + (1.0 - ADAM_B1) * g
        nv = ADAM_B2 * v_ref[...] + (1.0 - ADAM_B2) * (g * g)
        m_hat = nm / c1
        v_hat = nv / c2
        nm_ref[...] = nm
        nv_ref[...] = nv
        d_ref[...] = -ADAM_LR * (m_hat / (jnp.sqrt(v_hat) + ADAM_EPS) + ADAM_WD * w_ref[...])

    blk = pl.BlockSpec((tr, tc), lambda i, j: (i, j))
    out = _sds((r, c), F32)
    return pl.pallas_call(
        body, grid=(r // tr, c // tc), in_specs=[blk] * 4, out_specs=[blk] * 3, out_shape=[out] * 3,
        compiler_params=_cp(), name=name,
    )(g, w, m, v)


def _pair_sum(name, mine, theirs, c_idx):
    _, _, r, c = mine.shape
    tr, tc = _elem_tiles(r, c)

    def body(c_ref, a_ref, b_ref, o_ref):
        o_ref[...] = (a_ref[...].astype(F32) + b_ref[...].astype(F32)).astype(o_ref.dtype)

    return pl.pallas_call(
        body,
        grid_spec=pltpu.PrefetchScalarGridSpec(
            num_scalar_prefetch=1, grid=(4, r // tr, c // tc),
            in_specs=[pl.BlockSpec((None, None, tr, tc), lambda p, i, j, cref: (p, cref[0], i, j)),
                      pl.BlockSpec((None, tr, tc), lambda p, i, j, cref: (p, i, j))],
            out_specs=pl.BlockSpec((None, tr, tc), lambda p, i, j, cref: (p, i, j))),
        out_shape=_sds((4, r, c), _WIRE), compiler_params=_cp(), name=name,
    )(c_idx, mine, theirs)


def _place():
    return lax.axis_index("x"), lax.axis_index("y"), lax.axis_index("c")


def _all_gather(name, xs):
    n = len(xs)

    def body(*refs):
        x_refs, o_refs = refs[:n], refs[n:2 * n]
        send_sems, recv_sems, local_sems = refs[2 * n:]
        x, y, c = _place()
        me, sibling = (x, y, c), (x, y, 1 - c)
        chips = [(1 - x, y), (x, 1 - y), (1 - x, 1 - y)]

        def copy(a, k, block, to, src=None):
            px, py, pc = block
            rows = o_refs[a].at[4 * px + 2 * py + pc]
            return pltpu.make_async_remote_copy(
                src_ref=rows if src is None else src, dst_ref=rows, send_sem=send_sems.at[a, k], recv_sem=recv_sems.at[a, k],
                device_id=to, device_id_type=MESH)

        sent = []
        mine = []
        for a in range(n):
            mx, my, mc = me
            cp = pltpu.make_async_copy(x_refs[a], o_refs[a].at[4 * mx + 2 * my + mc], local_sems.at[a])
            cp.start()
            mine.append(cp)
            first = [copy(a, 0, me, sibling, src=x_refs[a])]
            first += [copy(a, 1 + j, me, (*chip, c), src=x_refs[a]) for j, chip in enumerate(chips)]
            for cp in first:
                cp.start()
            sent += first
        for a in range(n):
            for j, chip in enumerate(chips):
                copy(a, 1 + j, (*chip, c), me).wait_recv()
                fwd = copy(a, 4 + j, (*chip, c), sibling)
                fwd.start()
                sent.append(fwd)
        for a in range(n):
            copy(a, 0, sibling, me).wait_recv()
            for j, chip in enumerate(chips):
                copy(a, 4 + j, (*chip, 1 - c), me).wait_recv()
        for cp in sent:
            cp.wait_send()
        for cp in mine:
            cp.wait()

    return pl.pallas_call(
        body, in_specs=[ANY] * n, out_specs=[ANY] * n,
        out_shape=[_sds((N_DEV,) + a.shape, a.dtype) for a in xs],
        scratch_shapes=[pltpu.SemaphoreType.DMA((n, 7)), pltpu.SemaphoreType.DMA((n, 7)), pltpu.SemaphoreType.DMA((n,))],
        name=name,
    )(*xs)


HBM = pl.BlockSpec(memory_space=pltpu.HBM)
SEM = pl.BlockSpec(memory_space=pltpu.SEMAPHORE)
DATAFLOW = pltpu.SideEffectType.DATAFLOW_SIDE_EFFECTING


def _in_hbm(a):
    return pltpu.with_memory_space_constraint(a, pltpu.HBM)


def _split_copy_start(name, srcs, lands, copies, deps):
    n, nd = len(srcs), len(deps)
    per = len(copies([None] * n, [None] * n, probe=True)) // n

    def body(*refs):
        s_refs, l_refs = refs[:n], refs[n:2 * n]
        send_sems, recv_sems = refs[2 * n + nd], refs[2 * n + nd + 1]
        token = refs[-1]
        for a, k, src, dst, to in copies(s_refs, l_refs):
            pltpu.make_async_remote_copy(src_ref=src, dst_ref=dst, send_sem=send_sems.at[a * per + k],
                                         recv_sem=recv_sems.at[a * per + k], device_id=to, device_id_type=MESH).start()
        token[...] = jnp.zeros_like(token)

    both = list(srcs) + list(lands)
    outs = pl.pallas_call(
        body, name=name,
        out_shape=(pltpu.SemaphoreType.DMA((n * per,)), pltpu.SemaphoreType.DMA((n * per,)),
                   *[pltpu.HBM(a.shape, a.dtype) for a in both], _sds((8, V7X_LANES), F32)),
        in_specs=[HBM] * (2 * n) + [ANY] * nd,
        out_specs=(SEM, SEM, *[HBM] * (2 * n), pl.BlockSpec(memory_space=pltpu.VMEM)),
        input_output_aliases={i: 2 + i for i in range(2 * n)},
        compiler_params=pltpu.CompilerParams(has_side_effects=DATAFLOW),
    )(*[_in_hbm(a) for a in both], *deps)
    return outs[0], outs[1], list(outs[2:2 + n]), list(outs[2 + n:2 + 2 * n]), outs[-1]


def _split_copy_wait(name, send_sems, recv_sems, srcs, lands, arrivals, after):
    n = len(srcs)
    per = len(arrivals([None] * n, [None] * n, probe=True)) // n

    def body(*refs):
        s_refs, l_refs = refs[:n], refs[n:2 * n]
        send_sems_, recv_sems_ = refs[2 * n], refs[2 * n + 1]
        for a, k, src, dst, frm in arrivals(s_refs, l_refs):
            cp = pltpu.make_async_remote_copy(src_ref=src, dst_ref=dst, send_sem=send_sems_.at[a * per + k],
                                              recv_sem=recv_sems_.at[a * per + k], device_id=frm, device_id_type=MESH)
            cp.wait_send()
            cp.wait_recv()

    both = list(srcs) + list(lands)
    outs = pl.pallas_call(
        body, name=name, out_shape=tuple(pltpu.HBM(a.shape, a.dtype) for a in both),
        in_specs=[HBM] * (2 * n) + [SEM, SEM, ANY], out_specs=tuple([HBM] * (2 * n)),
        input_output_aliases={i: i for i in range(2 * n)},
        compiler_params=pltpu.CompilerParams(has_side_effects=DATAFLOW),
    )(*both, send_sems, recv_sems, after)
    return list(outs[:n]), list(outs[n:])


def _gather_copies(arriving):
    def copies(s_refs, l_refs, probe=False):
        if probe:
            return [None] * (4 * len(s_refs))
        x, y, c = _place()
        out = []
        for a in range(len(s_refs)):
            for k, (px, py, pc) in enumerate([(x, y, 1 - c), (1 - x, y, c), (x, 1 - y, c), (1 - x, 1 - y, c)]):
                slot = 4 * px + 2 * py + pc if arriving else 4 * x + 2 * y + c
                out.append((a, k, s_refs[a], l_refs[a].at[slot], (px, py, pc)))
        return out
    return copies


def _chip_copies(arriving):
    def copies(s_refs, l_refs, probe=False):
        if probe:
            return [None] * (3 * len(s_refs))
        x, y, c = _place()
        out = []
        for a in range(len(s_refs)):
            for j, (px, py) in enumerate([(1 - x, y), (x, 1 - y), (1 - x, 1 - y)]):
                src = s_refs[a].at[2 * x + y] if arriving else s_refs[a].at[2 * px + py]
                out.append((a, j, src, l_refs[a].at[j], (px, py, c)))
        return out
    return copies


def _sibling_copies(s_refs, l_refs, probe=False):
    if probe:
        return [None] * (4 * len(s_refs))
    x, y, c = _place()
    return [(a, p, s_refs[a].at[p, 1 - c], l_refs[a].at[p], (x, y, 1 - c)) for a in range(len(s_refs)) for p in range(4)]


def _sibling_begin(name, by_owner, deps):
    lands = [lax.empty((4,) + a.shape[2:], a.dtype) for a in by_owner]
    return _split_copy_start(name + "_start", by_owner, lands, _sibling_copies, deps)


def _sibling_end(name, handle, after):
    send_sems, recv_sems, srcs, lands, _ = handle
    return _split_copy_wait(name + "_wait", send_sems, recv_sems, srcs, lands, _sibling_copies, after)


def _gather_begin(name, shards, deps):
    x, y, c = _place()
    lands = [lax.dynamic_update_slice_in_dim(lax.empty((N_DEV,) + a.shape, a.dtype), a[None], 4 * x + 2 * y + c, 0)
             for a in shards]
    return _split_copy_start(name + "_start", shards, lands, _gather_copies(False), deps)


def _gather_end(name, handle, after):
    send_sems, recv_sems, srcs, lands, _ = handle
    _, lands = _split_copy_wait(name + "_wait", send_sems, recv_sems, srcs, lands, _gather_copies(True), after)
    return _gather_forward(name + "_forward", lands)


def _forward_copies(arriving):
    def copies(s_refs, l_refs, probe=False):
        if probe:
            return [None] * (3 * len(l_refs))
        x, y, c = _place()
        out = []
        for a in range(len(l_refs)):
            for j, (px, py) in enumerate([(1 - x, y), (x, 1 - y), (1 - x, 1 - y)]):
                mine, theirs = l_refs[a].at[4 * px + 2 * py + c], l_refs[a].at[4 * px + 2 * py + 1 - c]
                out.append((a, j, mine, theirs if arriving else mine, (x, y, 1 - c)))
        return out
    return copies


def _gather_arrived(name, handle, after):
    send_sems, recv_sems, srcs, lands, _ = handle
    srcs, lands = _split_copy_wait(name + "_wait", send_sems, recv_sems, srcs, lands, _gather_copies(True), after)
    return _split_copy_start(name + "_forward_start", srcs, lands, _forward_copies(False), [])


def _gather_done(name, handle, after):
    send_sems, recv_sems, srcs, lands, _ = handle
    return _split_copy_wait(name + "_forward_wait", send_sems, recv_sems, srcs, lands, _forward_copies(True), after)[1]


def _gather_forward(name, lands):
    n = len(lands)

    def body(*refs):
        o_refs = refs[n:2 * n]
        send_sems, recv_sems = refs[2 * n:]
        x, y, c = _place()
        sibling = (x, y, 1 - c)
        chips = [(1 - x, y), (x, 1 - y), (1 - x, 1 - y)]
        sent = []
        for a in range(n):
            for j, (px, py) in enumerate(chips):
                rows = o_refs[a].at[4 * px + 2 * py + c]
                cp = pltpu.make_async_remote_copy(src_ref=rows, dst_ref=rows, send_sem=send_sems.at[a, j],
                                                  recv_sem=recv_sems.at[a, j], device_id=sibling, device_id_type=MESH)
                cp.start()
                sent.append(cp)
        for a in range(n):
            for j, (px, py) in enumerate(chips):
                rows = o_refs[a].at[4 * px + 2 * py + 1 - c]
                pltpu.make_async_remote_copy(src_ref=rows, dst_ref=rows, send_sem=send_sems.at[a, j],
                                             recv_sem=recv_sems.at[a, j], device_id=sibling, device_id_type=MESH).wait_recv()
        for cp in sent:
            cp.wait_send()

    return pl.pallas_call(
        body, in_specs=[ANY] * n, out_specs=[ANY] * n, out_shape=[_sds(a.shape, a.dtype) for a in lands],
        input_output_aliases={i: i for i in range(n)},
        scratch_shapes=[pltpu.SemaphoreType.DMA((n, 3)), pltpu.SemaphoreType.DMA((n, 3))], name=name,
    )(*lands)


def _chips_begin(name, pairs, deps):
    lands = [lax.empty((3,) + a.shape[1:], a.dtype) for a in pairs]
    return _split_copy_start(name + "_start", pairs, lands, _chip_copies(False), deps)


def _chips_end(name, handle, after):
    send_sems, recv_sems, srcs, lands, _ = handle
    return _split_copy_wait(name + "_wait", send_sems, recv_sems, srcs, lands, _chip_copies(True), after)


def _adamw_layer(name, l, own, lands, w, m, v, prev):
    nl, ng, r, c = w.shape
    tr, tc = _elem_tiles(r, c)
    c1 = 1.0 - ADAM_B1 ** ADAM_STEP
    c2 = 1.0 - ADAM_B2 ** ADAM_STEP

    def body(own_ref, lands_ref, w_ref, m_ref, v_ref, *rest):
        g_ref, d_ref, nm_ref, nv_ref = rest[-4:]
        g = own_ref[...].astype(F32) + lands_ref[0].astype(F32) + lands_ref[1].astype(F32) + lands_ref[2].astype(F32)
        nm = ADAM_B1 * m_ref[...] + (1.0 - ADAM_B1) * g
        nv = ADAM_B2 * v_ref[...] + (1.0 - ADAM_B2) * (g * g)
        m_hat = nm / c1
        v_hat = nv / c2
        g_ref[...] = g
        nm_ref[...] = nm
        nv_ref[...] = nv
        d_ref[...] = -ADAM_LR * (m_hat / (jnp.sqrt(v_hat) + ADAM_EPS) + ADAM_WD * w_ref[...])

    lay = pl.BlockSpec((None, None, tr, tc), lambda g, i, j: (l, g, i, j))
    out = _sds((nl, ng, r, c), F32)
    prev = [] if prev is None else list(prev)
    return pl.pallas_call(
        body, grid=(ng, r // tr, c // tc),
        in_specs=[pl.BlockSpec((None, tr, tc), lambda g, i, j: (g, i, j)),
                  pl.BlockSpec((3, None, tr, tc), lambda g, i, j: (0, g, i, j)), lay, lay, lay] + [ANY] * len(prev),
        out_specs=[lay] * 4, out_shape=[out] * 4, input_output_aliases={5 + i: i for i in range(len(prev))},
        compiler_params=_cp(), name=name,
    )(own, lands, w, m, v, *prev)


def _pad_pairs(a, axis, half, half_pad):
    shp = a.shape
    a = a.reshape(shp[:axis] + (2, half) + shp[axis + 1:])
    pad = [(0, 0)] * a.ndim
    pad[axis + 1] = (0, half_pad - half)
    a = jnp.pad(a, pad)
    return a.reshape(shp[:axis] + (2 * half_pad,) + shp[axis + 1:])


def _unpad_pairs(a, axis, half, half_pad):
    shp = a.shape
    a = a.reshape(shp[:axis] + (2, half_pad) + shp[axis + 1:])
    a = lax.slice_in_dim(a, 0, half, axis=axis + 1)
    return a.reshape(shp[:axis] + (2 * half,) + shp[axis + 1:])


def _blockdiag(w, nt):
    g, a, b = w.shape
    gl = g // nt
    e = jnp.eye(gl, dtype=w.dtype).reshape(1, gl, 1, gl, 1)
    return (w.reshape(nt, gl, a, 1, b) * e).reshape(nt, gl * a, gl * b)


def _diagblocks(m, g, a, b):
    nt = m.shape[0]
    gl = g // nt
    d = jnp.diagonal(m.reshape(nt, gl, a, gl, b), axis1=1, axis2=3)
    return jnp.moveaxis(d, -1, 1).reshape(g, a, b)


_PIECE = 8 * V7X_LANES


def _pack(pieces, row_multiple=512):
    rows = []
    for p in pieces:
        flat = p.reshape(-1).astype(F32)
        rows.append(jnp.pad(flat, (0, -flat.shape[0] % _PIECE)).reshape(-1, V7X_LANES))
    fill = -sum(r.shape[0] for r in rows) % row_multiple
    if fill:
        rows.append(jnp.zeros((fill, V7X_LANES), F32))
    return jnp.concatenate(rows, 0)


def _unpack(packed, shapes):
    lead = packed.shape[:-2]
    out, off = [], 0
    for shp in shapes:
        n = math.prod(shp)
        r = -(-n // _PIECE) * 8
        piece = packed[..., off:off + r, :].reshape(lead + (r * V7X_LANES,))[..., :n]
        out.append(piece.reshape(lead + tuple(shp)))
        off += r
    return out


def _ssm_discretise(lam_re, lam_im, log_dt, b_re, b_im):
    dt = jnp.exp(log_dt)[..., None]
    mag = jnp.exp(lam_re * dt)
    ab_re, ab_im = mag * jnp.cos(lam_im * dt), mag * jnp.sin(lam_im * dt)
    nr, ni = ab_re - 1.0, ab_im
    den = lam_re * lam_re + lam_im * lam_im
    zr = (nr * lam_re + ni * lam_im) / den
    zi = (ni * lam_re - nr * lam_im) / den
    bbr = zr[..., None] * b_re - zi[..., None] * b_im
    bbi = zr[..., None] * b_im + zi[..., None] * b_re
    return ab_re, ab_im, bbr, bbi


def _rope_tables(s):
    half = HEAD_DIM // 2
    inv = ROPE_THETA ** (-jnp.arange(half, dtype=F32) / half)
    ang = jnp.arange(s).astype(F32)[:, None] * inv[None, :]
    cos, sin = jnp.cos(ang), jnp.sin(ang)
    reps = V7X_LANES // HEAD_DIM
    return jnp.tile(jnp.concatenate([cos, cos], -1), (1, reps)), jnp.tile(jnp.concatenate([-sin, sin], -1), (1, reps))


_SMALL = ("attn_sinks", "pool_w", "pool_scale", "ssm_lam_re", "ssm_lam_im", "ssm_log_dt", "ssm_b_re", "ssm_b_im",
          "ssm_c_re", "ssm_c_im", "ssm_d", "ln1_g", "ln1_b", "ffn_conv_b", "ln2_g", "ln2_b")
_BIG = ("w_in", "ssm_glu_w", "w_out", "ffn_w_up", "ffn_w_down")
_RAW = ("attn_sinks", "pool_w", "pool_scale", "ssm_lam_re", "ssm_lam_im", "ssm_b_re", "ssm_b_im", "ssm_c_re", "ssm_c_im",
        "ssm_d", "ln1_g", "ln1_b", "ffn_conv_b", "ln2_g", "ln2_b", "ffn_conv_w")
_ORDER = ("w_in", "attn_sinks", "pool_w", "pool_scale", "ssm_lam_re", "ssm_lam_im", "ssm_log_dt", "ssm_b_re", "ssm_b_im",
          "ssm_c_re", "ssm_c_im", "ssm_d", "ssm_glu_w", "w_out", "ln1_g", "ln1_b", "ffn_w_up", "ffn_conv_w", "ffn_conv_b",
          "ffn_w_down", "ln2_g", "ln2_b")


def kernel(x, w_in, attn_sinks, pool_w, pool_scale, ssm_lam_re, ssm_lam_im, ssm_log_dt, ssm_b_re, ssm_b_im, ssm_c_re, ssm_c_im, ssm_d, ssm_glu_w, w_out, ln1_g, ln1_b, ffn_w_up, ffn_conv_w, ffn_conv_b, ffn_w_down, ln2_g, ln2_b, loss_target, m_w_in, m_attn_sinks, m_pool_w, m_pool_scale, m_ssm_lam_re, m_ssm_lam_im, m_ssm_log_dt, m_ssm_b_re, m_ssm_b_im, m_ssm_c_re, m_ssm_c_im, m_ssm_d, m_ssm_glu_w, m_w_out, m_ln1_g, m_ln1_b, m_ffn_w_up, m_ffn_conv_w, m_ffn_conv_b, m_ffn_w_down, m_ln2_g, m_ln2_b, v_w_in, v_attn_sinks, v_pool_w, v_pool_scale, v_ssm_lam_re, v_ssm_lam_im, v_ssm_log_dt, v_ssm_b_re, v_ssm_b_im, v_ssm_c_re, v_ssm_c_im, v_ssm_d, v_ssm_glu_w, v_w_out, v_ln1_g, v_ln1_b, v_ffn_w_up, v_ffn_conv_w, v_ffn_conv_b, v_ffn_w_down, v_ln2_g, v_ln2_b):
    W = dict(w_in=w_in, attn_sinks=attn_sinks, pool_w=pool_w, pool_scale=pool_scale, ssm_lam_re=ssm_lam_re, ssm_lam_im=ssm_lam_im, ssm_log_dt=ssm_log_dt, ssm_b_re=ssm_b_re, ssm_b_im=ssm_b_im, ssm_c_re=ssm_c_re, ssm_c_im=ssm_c_im, ssm_d=ssm_d, ssm_glu_w=ssm_glu_w, w_out=w_out, ln1_g=ln1_g, ln1_b=ln1_b, ffn_w_up=ffn_w_up, ffn_conv_w=ffn_conv_w, ffn_conv_b=ffn_conv_b, ffn_w_down=ffn_w_down, ln2_g=ln2_g, ln2_b=ln2_b)
    M = dict(w_in=m_w_in, attn_sinks=m_attn_sinks, pool_w=m_pool_w, pool_scale=m_pool_scale, ssm_lam_re=m_ssm_lam_re, ssm_lam_im=m_ssm_lam_im, ssm_log_dt=m_ssm_log_dt, ssm_b_re=m_ssm_b_re, ssm_b_im=m_ssm_b_im, ssm_c_re=m_ssm_c_re, ssm_c_im=m_ssm_c_im, ssm_d=m_ssm_d, ssm_glu_w=m_ssm_glu_w, w_out=m_w_out, ln1_g=m_ln1_g, ln1_b=m_ln1_b, ffn_w_up=m_ffn_w_up, ffn_conv_w=m_ffn_conv_w, ffn_conv_b=m_ffn_conv_b, ffn_w_down=m_ffn_w_down, ln2_g=m_ln2_g, ln2_b=m_ln2_b)
    V = dict(w_in=v_w_in, attn_sinks=v_attn_sinks, pool_w=v_pool_w, pool_scale=v_pool_scale, ssm_lam_re=v_ssm_lam_re, ssm_lam_im=v_ssm_lam_im, ssm_log_dt=v_ssm_log_dt, ssm_b_re=v_ssm_b_re, ssm_b_im=v_ssm_b_im, ssm_c_re=v_ssm_c_re, ssm_c_im=v_ssm_c_im, ssm_d=v_ssm_d, ssm_glu_w=v_ssm_glu_w, w_out=v_w_out, ln1_g=v_ln1_g, ln1_b=v_ln1_b, ffn_w_up=v_ffn_w_up, ffn_conv_w=v_ffn_conv_w, ffn_conv_b=v_ffn_conv_b, ffn_w_down=v_ffn_w_down, ln2_g=v_ln2_g, ln2_b=v_ln2_b)

    depth = w_in.shape[0]
    s, d = x.shape[1], x.shape[2]
    alpha = (2 * depth) ** 0.25
    attn_w = d // 2
    kv_w = attn_w // GQA
    nkv = kv_w // HEAD_DIM
    pool_wd = d // 4
    ssm_wd = d // 4
    n_groups = ssm_wd // SSM_GROUP
    state_w = n_groups * SSM_STATE
    nt_ssm = max(1, state_w // 512)
    o_k, o_v, o_p, o_s = attn_w, attn_w + kv_w, attn_w + 2 * kv_w, attn_w + 2 * kv_w + pool_wd
    in_w = o_s + ssm_wd
    half = ffn_w_down.shape[1]
    half_pad = -(-half // 64) * 64
    ffp = 4 * 2 * half_pad
    xi, yi, ci = _place()
    me = 4 * xi + 2 * yi + ci
    c_idx = jnp.reshape(ci, (1,)).astype(jnp.int32)

    cos_t, sin_t = _rope_tables(s)

    def layer_shards(l):
        return [
            jnp.transpose(w_in[l]).astype(_WIRE), ssm_glu_w[l].astype(_WIRE), w_out[l].astype(_WIRE),
            _pad_pairs(jnp.transpose(ffn_w_up[l]).astype(_WIRE), 0, half, half_pad),
            jnp.pad(ffn_w_down[l].astype(_WIRE), ((0, half_pad - half), (0, 0))),
        ]

    (g_cw,) = _all_gather("gather_conv_w", [_pad_pairs(ffn_conv_w, 2, half, half_pad)])

    def in_weights(l, gathered):
        (g_in,) = gathered
        return dict(
            win_t=g_in.reshape(in_w, d),
            cw=jnp.transpose(g_cw[:, l], (1, 0, 2)).reshape(CONV_WIDTH, 2 * ffp),
            cb=_pad_pairs(ffn_conv_b[l].reshape(N_DEV, 2 * half), 1, half, half_pad).reshape(1, 2 * ffp),
        )

    def out_weights(gathered):
        g_glu, g_out = gathered
        return dict(glu=jnp.transpose(g_glu, (1, 0, 2)).reshape(ssm_wd, 2 * ssm_wd), wout=g_out.reshape(d, d))

    def mixer_weights(l, gathered):
        return {**in_weights(l, gathered[:1]), **out_weights(gathered[1:])}

    def up_weights(gathered):
        (g_up,) = gathered
        return dict(wup_t=g_up.reshape(2 * ffp, d))

    def down_weights(gathered):
        (g_down,) = gathered
        return dict(wdown=g_down.reshape(ffp, d))

    def ffn_weights(gathered):
        return {**up_weights(gathered[:1]), **down_weights(gathered[1:])}

    shards = [layer_shards(l) for l in range(depth)]
    full = [None] * depth
    gathers = {}

    def begin_gather(l, part, deps):
        arrays = shards[l][{"in": slice(0, 1), "out": slice(1, 3), "mixer": slice(0, 3), "up": slice(3, 4),
                            "down": slice(4, 5), "ffn": slice(3, 5)}[part]]
        gathers[l, part] = ("sent", _gather_begin(f"gather_{part}_weights_{l}", arrays, deps))
        return gathers[l, part][1][-1]

    def arrive_gather(l, part, after):
        gathers[l, part] = ("forwarding", _gather_arrived(f"gather_{part}_weights_{l}", gathers[l, part][1], after))
        return gathers[l, part][1][-1]

    def end_gather(l, part, after):
        stage, handle = gathers.pop((l, part))
        return (_gather_done if stage == "forwarding" else _gather_end)(f"gather_{part}_weights_{l}", handle, after)

    issued = begin_gather(0, "in", [g_cw])
    for part in ("out", "up", "down"):
        issued = begin_gather(0, part, [issued])
    full[0] = in_weights(0, end_gather(0, "in", g_cw))

    ssm_params = (ssm_lam_re, ssm_lam_im, ssm_log_dt, ssm_b_re, ssm_b_im)
    ab_re_all, ab_im_all, bbr_all, bbi_all = _ssm_discretise(*ssm_params)

    def ssm_maps(w):
        return jax.vmap(lambda t: _blockdiag(jnp.transpose(t, (0, 2, 1)), nt_ssm))(w).astype(_MXU)

    bdr_all, bdi_all, cdr_all, cdi_all = ssm_maps(bbr_all), ssm_maps(bbi_all), ssm_maps(ssm_c_re), ssm_maps(ssm_c_im)

    saved = []
    xf = x[0]
    xb = xf.astype(_MXU)
    for l in range(depth):
        fw = full[l]
        deps = [arrive_gather(l, "ffn", xb)] if l >= 2 else []
        if l + 1 < depth:
            issued = begin_gather(l + 1, "ffn", [begin_gather(l + 1, "mixer", [fw["win_t"], issued])])
            deps.append(issued)
        h = _mm_nt(f"in_proj_{l}", xb, fw["win_t"], deps=tuple(deps))
        q_rot, k_rot = _rope(f"rope_{l}", h, o_v, cos_t, sin_t, _MXU, ((0, o_k), (o_k, o_v)))
        k_hm = jnp.transpose(k_rot.reshape(s, nkv, HEAD_DIM), (1, 0, 2))
        v_hm = jnp.transpose(h[:, o_v:o_p].astype(_MXU).reshape(s, nkv, HEAD_DIM), (1, 0, 2))
        sinks = attn_sinks[l].reshape(nkv, GQA)
        o_attn, lse = _attn_fwd(f"attn_{l}", q_rot, k_hm, v_hm, sinks)
        pw_b = pool_w[l].astype(_MXU)
        psc = pool_scale[l].reshape(1, pool_wd)
        y_pool, pre = _pool_fwd(f"pool_{l}", h, o_p // pool_wd, pw_b, psc)
        bdr, bdi, cdr, cdi = bdr_all[l], bdi_all[l], cdr_all[l], cdi_all[l]
        dvec = ssm_d[l].reshape(1, ssm_wd)
        ar, ai = ab_re_all[l].reshape(1, state_w), ab_im_all[l].reshape(1, state_w)
        cw_ssm = ssm_wd // nt_ssm
        sr, si, ypre, yg = _ssm_fwd(f"ssm_{l}", h, o_s // cw_ssm, bdr, bdi, cdr, cdi, dvec, ar, ai)
        if l == 0:
            fw.update(out_weights(end_gather(0, "out", yg)))
        y_ssm, ab2 = _glu_fwd(f"glu_{l}", yg, fw["glu"])
        mix = jnp.concatenate([o_attn.astype(_MXU), y_pool, y_ssm], -1)
        a1 = _mm_nn(f"out_proj_{l}", mix, fw["wout"])
        g1, b1 = ln1_g[l].reshape(1, d), ln1_b[l].reshape(1, d)
        x1, x1b, xh1, rs1 = _ln_fwd(f"ln1_{l}", xf, a1, g1, b1, alpha)
        fw.update(up_weights(end_gather(0, "up", x1b)) if l == 0 else ffn_weights(end_gather(l, "ffn", x1b)))
        deps = (arrive_gather(l + 1, "mixer", x1b),) if 1 <= l < depth - 1 else ()
        hu = _mm_nt(f"ffn_up_{l}", x1b, fw["wup_t"], cap=2 * half_pad, deps=deps)
        act = _conv_act_fwd(f"ffn_act_{l}", hu, fw["cw"], fw["cb"])
        if l == 0:
            fw.update(down_weights(end_gather(0, "down", act)))
        f_out = _mm_nn(f"ffn_down_{l}", act, fw["wdown"], cap=256)
        g2, b2 = ln2_g[l].reshape(1, d), ln2_b[l].reshape(1, d)
        x2, x2b, xh2, rs2 = _ln_fwd(f"ln2_{l}", x1, f_out, g2, b2, alpha)
        saved.append(dict(xb=xb, h=h, q_rot=q_rot, k_hm=k_hm, v_hm=v_hm, sinks=sinks, o_attn=o_attn, lse=lse, pw_b=pw_b, psc=psc,
                          pre=pre, bdr=bdr, bdi=bdi, cdr=cdr, cdi=cdi, dvec=dvec, ar=ar, ai=ai, sr=sr, si=si, ypre=ypre, yg=yg,
                          ab2=ab2, mix=mix, g1=g1, xh1=xh1, rs1=rs1, x1b=x1b, hu=hu, act=act, g2=g2, xh2=xh2, rs2=rs2))
        xf, xb = x2, x2b
        if l + 1 < depth:
            full[l + 1] = mixer_weights(l + 1, end_gather(l + 1, "mixer", x2b))

    dy, loss_part = _loss_head("loss_head", xf, loss_target[0])
    loss = lax.psum(loss_part[0, 0], ("x", "y", "c"))

    small_handles = [None] * depth
    small_parts = [None] * depth
    outs = {}
    big_res = {k: None for k in _BIG}
    my_chip = 2 * xi + yi
    pending = []

    transposed = ("w_in", "ffn_w_up")

    def row_groups(name_, t):
        g = 2 if name_ == "ffn_w_up" else 1
        return t.reshape(t.shape[:-2] + (g, t.shape[-2] // g, t.shape[-1]))

    def as_groups(name_, t):
        return row_groups(name_, jnp.transpose(t, (0, 2, 1)) if name_ in transposed else t)

    def from_groups(name_, t):
        t = t.reshape(t.shape[0], t.shape[1] * t.shape[2], t.shape[3])
        return jnp.transpose(t, (0, 2, 1)) if name_ in transposed else t

    grouped = {name_: tuple(as_groups(name_, t[name_]) for t in (W, M, V)) for name_ in _BIG}

    def finish_exchanges(after):
        while pending:
            lay, part, names, handle = pending.pop(0)
            pairs, lands = _chips_end(f"grads_between_chips_{part}_{lay}", handle, after)
            for name_, p, ld in zip(names, pairs, lands):
                own = row_groups(name_, lax.dynamic_index_in_dim(p, my_chip, 0, keepdims=False))
                big_res[name_] = _adamw_layer(f"adamw_{name_}_{lay}", lay, own, row_groups(name_, ld), *grouped[name_],
                                              big_res[name_])

    def begin_swap(lay, part, names, grads):
        by_owner = [a.reshape((4, 2) + a.shape[1:]) for a in grads]
        return lay, part, names, _sibling_begin(f"grads_to_sibling_{part}_{lay}", by_owner, [])

    def begin_exchange(swap, after):
        lay, part, names, handle = swap
        by_owner, theirs = _sibling_end(f"grads_to_sibling_{part}_{lay}", handle, after)
        pair = [_pair_sum(f"pair_sum_{name_}_{lay}", a, b, c_idx) for name_, a, b in zip(names, by_owner, theirs)]
        finish_exchanges(after)
        handle = _chips_begin(f"grads_between_chips_{part}_{lay}", pair, [])
        pending.append((lay, part, names, handle))
        return handle[-1]

    token = None
    small_handle = None
    for l in reversed(range(depth)):
        fw, sv = full[l], saved[l]
        deps = () if token is None else (token, small_handles[l + 1][-1])
        dr2, dr2b, dg2, db2 = _ln_bwd(f"ln2_bwd_{l}", dy, sv["xh2"], sv["rs2"], sv["g2"], deps=deps)
        d_wdown = _mm_tn_acols(f"ffn_down_dw_{l}", sv["act"], dr2b, _WIRE, cap=2 * half_pad)
        dact = _mm_nt(f"ffn_down_dx_{l}", dr2b, fw["wdown"], cap=2 * half_pad)
        dhu, dcw, dcb = _conv_act_bwd(f"ffn_act_bwd_{l}", dact, sv["hu"], fw["cw"], fw["cb"])
        d_wup = _mm(f"ffn_up_dw_{l}", dhu, sv["x1b"], TN, (N_DEV, 1),
                    pl.BlockSpec((None, s, 2 * half_pad), lambda j, kk: (j // 4, 0, j % 4)),
                    _resident((s, d), lambda j, kk: (0, 0)),
                    pl.BlockSpec((2 * half_pad, d), lambda j, kk: (j, 0)), (2 * ffp, d), _WIRE)
        swap = begin_swap(l, "ffn", ("ffn_w_up", "ffn_w_down"),
                          [d_wup.reshape(N_DEV, 2 * half_pad, d), d_wdown.reshape(N_DEV, half_pad, d)])
        dy1 = _mm_split_k(f"ffn_up_dx_{l}", dhu, fw["wup_t"], dr2, alpha, deps=(swap[3][-1],))
        token = begin_exchange(swap, dy1)
        dr1, dr1b, dg1, db1 = _ln_bwd(f"ln1_bwd_{l}", dy1, sv["xh1"], sv["rs1"], sv["g1"], deps=(token,))
        d_wout = _mm_tn_acols(f"out_proj_dw_{l}", sv["mix"], dr1b, _WIRE, cap=d // N_DEV)
        dmix = _mm_nt(f"out_proj_dx_{l}", dr1b, fw["wout"])
        dq_rot, dk_hm, dv_hm, dsk = _attn_bwd(f"attn_bwd_{l}", sv["q_rot"], sv["k_hm"], sv["v_hm"], sv["o_attn"], dmix,
                                             sv["lse"], sv["sinks"])
        dqk = jnp.concatenate([dq_rot, jnp.transpose(dk_hm, (1, 0, 2)).reshape(s, kv_w)], -1)
        dhq, dhk = _rope(f"rope_bwd_{l}", dqk, o_v, cos_t, -sin_t, _MXU, ((0, o_k), (o_k, o_v)))
        dhv = jnp.transpose(dv_hm, (1, 0, 2)).reshape(s, kv_w).astype(_MXU)
        dhp, dpw, dpsc = _pool_bwd(f"pool_bwd_{l}", dmix, attn_w // pool_wd, sv["pre"], sv["pw_b"], sv["psc"])
        dab2, dyg = _glu_bwd(f"glu_bwd_{l}", dmix, (attn_w + pool_wd) // ssm_wd, sv["ab2"], fw["glu"])
        d_glu = _mm_tn_bcols(f"glu_dw_{l}", sv["yg"], dab2, _WIRE)
        cw_ssm = ssm_wd // nt_ssm
        dhs, dd, dcdr, dcdi, dbdr, dbdi, dar, dai = _ssm_bwd(
            f"ssm_bwd_{l}", dyg, sv["ypre"], sv["h"], o_s // cw_ssm, sv["sr"], sv["si"], sv["bdr"], sv["bdi"], sv["cdr"],
            sv["cdi"], sv["dvec"], sv["ar"], sv["ai"])
        dh = jnp.concatenate([dhq, dhk, dhv, dhp, dhs], -1)
        d_win = _mm_tn_acols(f"in_proj_dw_{l}", dh, sv["xb"], _WIRE)
        swap = begin_swap(l, "mixer", ("w_in", "ssm_glu_w", "w_out"),
                          [d_win.reshape(N_DEV, in_w // N_DEV, d),
                           jnp.transpose(d_glu.reshape(ssm_wd, N_DEV, 2 * ssm_wd // N_DEV), (1, 0, 2)),
                           d_wout.reshape(N_DEV, d // N_DEV, d)])
        dy = _mm_nn(f"in_proj_dx_{l}", dh, fw["win_t"], add=dr1, add_scale=alpha, deps=(swap[3][-1],))

        raw = dict(attn_sinks=dsk, pool_w=dpw, pool_scale=dpsc, ssm_lam_re=dar, ssm_lam_im=dai,
                   ssm_b_re=_diagblocks(dbdr, n_groups, SSM_GROUP, SSM_STATE),
                   ssm_b_im=_diagblocks(dbdi, n_groups, SSM_GROUP, SSM_STATE),
                   ssm_c_re=_diagblocks(dcdr, n_groups, SSM_STATE, SSM_GROUP),
                   ssm_c_im=_diagblocks(dcdi, n_groups, SSM_STATE, SSM_GROUP), ssm_d=dd, ln1_g=dg1, ln1_b=db1,
                   ffn_conv_b=_unpad_pairs(dcb.reshape(N_DEV, 2 * half_pad), 1, half, half_pad), ln2_g=dg2, ln2_b=db2,
                   ffn_conv_w=_unpad_pairs(dcw.reshape(CONV_WIDTH, N_DEV, 2 * half_pad), 2, half, half_pad))
        raw_shapes = {k: raw[k].shape for k in _RAW}
        small_handles[l] = _gather_begin(f"gather_small_grads_{l}", [_pack([raw[k] for k in _RAW])], [dy])
        token = begin_exchange(swap, small_handles[l][-1])
        if l + 1 < depth:
            small_parts[l + 1] = _sum_parts(f"sum_small_grads_{l + 1}",
                                            _gather_end(f"gather_small_grads_{l + 1}", small_handles[l + 1], dy)[0])

    small_parts[0] = _sum_parts("sum_small_grads_0", _gather_end("gather_small_grads_0", small_handles[0], token)[0])
    summed = jnp.stack(small_parts)
    g_small = dict(zip(_RAW, _unpack(summed, [raw_shapes[k] for k in _RAW])))
    swap_last = lambda t: jnp.transpose(t, (0, 1, 3, 2))
    _, vjp = jax.vjp(_ssm_discretise, *ssm_params)
    dlr, dli, dldt, dbr, dbi = vjp((g_small["ssm_lam_re"].reshape(depth, n_groups, SSM_STATE),
                                    g_small["ssm_lam_im"].reshape(depth, n_groups, SSM_STATE),
                                    swap_last(g_small["ssm_b_re"]), swap_last(g_small["ssm_b_im"])))
    g_small.update(ssm_lam_re=dlr, ssm_lam_im=dli, ssm_log_dt=dldt, ssm_b_re=dbr, ssm_b_im=dbi,
                   ssm_c_re=swap_last(g_small["ssm_c_re"]), ssm_c_im=swap_last(g_small["ssm_c_im"]),
                   ffn_conv_w=lax.dynamic_index_in_dim(g_small["ffn_conv_w"], me, axis=2, keepdims=False))
    for k in _SMALL + ("ffn_conv_w",):
        shp = W[k].shape
        two_d = (math.prod(shp[:-1]), shp[-1])
        res = _adamw_2d(f"adamw_{k}", g_small[k].reshape(two_d), *(t[k].reshape(two_d) for t in (W, M, V)))
        outs[k] = (g_small[k].reshape(shp),) + tuple(a.reshape(shp) for a in res)

    finish_exchanges(outs["ln2_b"][1])
    for name_ in _BIG:
        outs[name_] = tuple(from_groups(name_, t) for t in big_res[name_])

    grad_x = dy[None]
    result = [loss, grad_x]
    for i in range(4):
        result += [outs[k][i] for k in _ORDER]
    return tuple(result)
```

```python
import functools
import math

import jax
import jax.numpy as jnp
from jax import lax
from jax.experimental import pallas as pl
from jax.experimental.pallas import tpu as pltpu

F32 = jnp.float32
_MXU = jnp.bfloat16
_WIRE = jnp.bfloat16

HEAD_DIM = 64
GQA = 4
ATTN_BLOCK = 128
ROPE_THETA = 10000.0
POOL_WINDOWS = (2, 4, 8, 16)
SSM_GROUP = 16
SSM_STATE = 64
CONV_WIDTH = 3
LN_EPS = 1e-5
ADAM_LR, ADAM_B1, ADAM_B2, ADAM_EPS, ADAM_WD, ADAM_STEP = 0.001, 0.9, 0.999, 1e-08, 0.01, 10

N_DEV = 8
V7X_LANES = 128
V7X_VMEM_LIMIT = 56 * 1024 * 1024
SCAN_T = 64
SCAN_LANES = 256
MESH = pl.DeviceIdType.MESH
ANY = pl.BlockSpec(memory_space=pl.ANY)


def _cp():
    return pltpu.CompilerParams(vmem_limit_bytes=V7X_VMEM_LIMIT)


def _resident(block, index_map):
    return pl.BlockSpec(block, index_map, pipeline_mode=pl.Buffered(1))


def _sds(shape, dtype):
    return jax.ShapeDtypeStruct(tuple(shape), dtype)


def _mm(name, a, b, dims, grid, a_spec, b_spec, o_spec, out_shape, out_dtype, add=None, add_spec=None, add_scale=1.0, deps=()):
    nk = grid[1]
    n_in = 2 + (add is not None) + len(deps)
    oblk = tuple(d for d in o_spec.block_shape if d is not None)
    scratch = nk > 1 and out_dtype != F32

    def body(*refs):
        a_ref, b_ref = refs[:2]
        add_ref = None if add is None else refs[2]
        o_ref = refs[n_in]
        acc_ref = refs[-1] if scratch else None

        def finish(r):
            if add_ref is not None:
                r = r + add_scale * add_ref[...]
            o_ref[...] = r.astype(o_ref.dtype)

        part = lax.dot_general(a_ref[...], b_ref[...], (dims, ((), ())), preferred_element_type=F32)
        if nk == 1:
            finish(part)
        elif not scratch:
            k = pl.program_id(1)

            @pl.when(k == 0)
            def _():
                o_ref[...] = part

            @pl.when(k > 0)
            def _():
                o_ref[...] += part

            if add_ref is not None:
                @pl.when(k == nk - 1)
                def _():
                    o_ref[...] += add_scale * add_ref[...]
        else:
            k = pl.program_id(1)

            @pl.when(k == 0)
            def _():
                acc_ref[...] = part

            @pl.when(k > 0)
            def _():
                acc_ref[...] += part

            @pl.when(k == nk - 1)
            def _():
                finish(acc_ref[...])

    ins = [a, b] + ([] if add is None else [add]) + list(deps)
    in_specs = [a_spec, b_spec] + ([] if add is None else [add_spec]) + [ANY] * len(deps)
    return pl.pallas_call(
        body, grid=grid, in_specs=in_specs, out_specs=o_spec, out_shape=_sds(out_shape, out_dtype),
        scratch_shapes=[pltpu.VMEM(oblk, F32)] if scratch else [], compiler_params=_cp(), name=name,
    )(*ins)


NN = ((1,), (0,))
NT = ((1,), (1,))
TN = ((0,), (0,))


def _mm_split_k(name, a2, b, add, add_scale, deps=()):
    _, m, f = a2.shape
    n = b.shape[1]
    tm, tn = m // 2, _col_tile(n, 256)

    def body(a_ref, b_ref, add_ref, *rest):
        o_ref = rest[-1]
        o_ref[...] = (jnp.dot(a_ref[0], b_ref[:f, :], preferred_element_type=F32)
                      + jnp.dot(a_ref[1], b_ref[f:, :], preferred_element_type=F32) + add_scale * add_ref[...])

    tile = pl.BlockSpec((tm, tn), lambda i, j: (i, j))
    return pl.pallas_call(
        body, grid=(m // tm, n // tn),
        in_specs=[_resident((2, tm, f), lambda i, j: (0, i, 0)), pl.BlockSpec((2 * f, tn), lambda i, j: (0, j)), tile]
        + [ANY] * len(deps),
        out_specs=tile, out_shape=_sds((m, n), F32), compiler_params=_cp(), name=name,
    )(a2, b, add, *deps)


def _col_tile(n, cap=512):
    if n % V7X_LANES:
        return n
    t = min(cap, n)
    t -= t % V7X_LANES
    while n % t:
        t -= V7X_LANES
    return t


def _mm_nn(name, a, b, out_dtype=F32, cap=512, add=None, add_scale=1.0, deps=()):
    m, k = a.shape
    n = b.shape[1]
    tn = _col_tile(n, cap)
    o_spec = pl.BlockSpec((m, tn), lambda j, kk: (0, j))
    return _mm(name, a, b, NN, (n // tn, 1), _resident((m, k), lambda j, kk: (0, 0)),
               pl.BlockSpec((k, tn), lambda j, kk: (0, j)), o_spec, (m, n), out_dtype,
               add=add, add_spec=None if add is None else o_spec, add_scale=add_scale, deps=deps)


def _mm_nt(name, a, b, out_dtype=F32, add=None, add_scale=1.0, cap=512, deps=()):
    m, k = a.shape
    n = b.shape[0]
    tn = _col_tile(n, cap)
    o_spec = pl.BlockSpec((m, tn), lambda j, kk: (0, j))
    return _mm(name, a, b, NT, (n // tn, 1), _resident((m, k), lambda j, kk: (0, 0)),
               pl.BlockSpec((tn, k), lambda j, kk: (j, 0)), o_spec, (m, n), out_dtype,
               add=add, add_spec=None if add is None else o_spec, add_scale=add_scale, deps=deps)


def _mm_tn_bcols(name, a, b, out_dtype, cap=512):
    s, k = a.shape
    n = b.shape[1]
    tn = _col_tile(n, cap)
    return _mm(name, a, b, TN, (n // tn, 1), _resident((s, k), lambda j, kk: (0, 0)),
               pl.BlockSpec((s, tn), lambda j, kk: (0, j)), pl.BlockSpec((k, tn), lambda j, kk: (0, j)), (k, n), out_dtype)


def _mm_tn_acols(name, a, b, out_dtype, cap=512):
    s, k = a.shape
    n = b.shape[1]
    tk = _col_tile(k, cap)
    return _mm(name, a, b, TN, (k // tk, 1), pl.BlockSpec((s, tk), lambda i, kk: (0, i)),
               _resident((s, n), lambda i, kk: (0, 0)), pl.BlockSpec((tk, n), lambda i, kk: (i, 0)), (k, n), out_dtype)


def _ln_fwd(name, x, a, g, b, alpha):
    s, d = x.shape
    tr = min(256, s)

    def body(x_ref, a_ref, g_ref, b_ref, y_ref, yb_ref, xh_ref, rs_ref):
        r = alpha * x_ref[...] + a_ref[...]
        mu = jnp.mean(r, -1, keepdims=True)
        c = r - mu
        var = jnp.mean(c * c, -1, keepdims=True)
        rstd = lax.rsqrt(var + LN_EPS)
        xh = c * rstd
        y = xh * g_ref[...] + b_ref[...]
        y_ref[...] = y
        yb_ref[...] = y.astype(_MXU)
        xh_ref[...] = xh
        rs_ref[...] = rstd

    row = pl.BlockSpec((tr, d), lambda i: (i, 0))
    vec = pl.BlockSpec((1, d), lambda i: (0, 0))
    return pl.pallas_call(
        body, grid=(s // tr,), in_specs=[row, row, vec, vec],
        out_specs=[row, row, row, pl.BlockSpec((tr, 1), lambda i: (i, 0))],
        out_shape=[_sds((s, d), F32), _sds((s, d), _MXU), _sds((s, d), F32), _sds((s, 1), F32)],
        compiler_params=_cp(), name=name,
    )(x, a, g, b)


def _ln_bwd(name, dy, xh, rstd, g, deps=()):
    s, d = dy.shape
    tr = min(256, s)
    nd = len(deps)

    def body(dy_ref, xh_ref, rs_ref, g_ref, *rest):
        dr_ref, drb_ref, dg_ref, db_ref = rest[nd:]
        i = pl.program_id(0)
        dy_ = dy_ref[...]
        xh_ = xh_ref[...]
        dxh = dy_ * g_ref[...]
        m1 = jnp.mean(dxh, -1, keepdims=True)
        m2 = jnp.mean(dxh * xh_, -1, keepdims=True)
        dr = rs_ref[...] * (dxh - m1 - xh_ * m2)
        dr_ref[...] = dr
        drb_ref[...] = dr.astype(_MXU)
        pg = jnp.sum(dy_ * xh_, 0, keepdims=True)
        pb = jnp.sum(dy_, 0, keepdims=True)

        @pl.when(i == 0)
        def _():
            dg_ref[...] = pg
            db_ref[...] = pb

        @pl.when(i > 0)
        def _():
            dg_ref[...] += pg
            db_ref[...] += pb

    row = pl.BlockSpec((tr, d), lambda i: (i, 0))
    vec = pl.BlockSpec((1, d), lambda i: (0, 0))
    return pl.pallas_call(
        body, grid=(s // tr,), in_specs=[row, row, pl.BlockSpec((tr, 1), lambda i: (i, 0)), vec] + [ANY] * nd,
        out_specs=[row, row, vec, vec],
        out_shape=[_sds((s, d), F32), _sds((s, d), _MXU), _sds((1, d), F32), _sds((1, d), F32)],
        compiler_params=_cp(), name=name,
    )(dy, xh, rstd, g, *deps)


def _loss_head(name, y, target):
    s, d = y.shape
    tr = min(256, s)

    def body(y_ref, t_ref, dy_ref, l_ref):
        i = pl.program_id(0)
        e = y_ref[...] - t_ref[...]
        dy_ref[...] = e * (1.0 / d)
        part = 0.5 * jnp.sum(jnp.mean(e * e, -1, keepdims=True), 0, keepdims=True)

        @pl.when(i == 0)
        def _():
            l_ref[...] = part

        @pl.when(i > 0)
        def _():
            l_ref[...] += part

    row = pl.BlockSpec((tr, d), lambda i: (i, 0))
    return pl.pallas_call(
        body, grid=(s // tr,), in_specs=[row, row], out_specs=[row, pl.BlockSpec((1, 1), lambda i: (0, 0))],
        out_shape=[_sds((s, d), F32), _sds((1, 1), F32)], compiler_params=_cp(), name=name,
    )(y, target)


def _rope(name, t, width, cos, sin, out_dtype, splits):
    s = t.shape[0]
    tr = min(256, s)
    assert width % V7X_LANES == 0

    def body(t_ref, c_ref, s_ref, *o_refs):
        lane = lax.broadcasted_iota(jnp.int32, (tr, V7X_LANES), 1)
        first = (lane % HEAD_DIM) < (HEAD_DIM // 2)
        cs, sn = c_ref[...], s_ref[...]
        for (lo, hi), o_ref in zip(splits, o_refs):
            for c0 in range(lo, hi, V7X_LANES):
                v = t_ref[:, c0:c0 + V7X_LANES].astype(F32)
                partner = jnp.where(first, pltpu.roll(v, V7X_LANES - HEAD_DIM // 2, 1), pltpu.roll(v, HEAD_DIM // 2, 1))
                o_ref[:, c0 - lo:c0 - lo + V7X_LANES] = (v * cs + partner * sn).astype(o_ref.dtype)

    tab = pl.BlockSpec((tr, V7X_LANES), lambda i: (i, 0))
    return pl.pallas_call(
        body, grid=(s // tr,), in_specs=[pl.BlockSpec((tr, width), lambda i: (i, 0)), tab, tab],
        out_specs=[pl.BlockSpec((tr, hi - lo), lambda i: (i, 0)) for lo, hi in splits],
        out_shape=[_sds((s, hi - lo), out_dtype) for lo, hi in splits], compiler_params=_cp(), name=name,
    )(t, cos, sin)


def _attn_masks():
    i = lax.broadcasted_iota(jnp.int32, (GQA * ATTN_BLOCK, 2 * ATTN_BLOCK), 0) % ATTN_BLOCK
    j = lax.broadcasted_iota(jnp.int32, (GQA * ATTN_BLOCK, 2 * ATTN_BLOCK), 1)
    cur_ok = jnp.logical_and(j >= ATTN_BLOCK, j - ATTN_BLOCK <= i)
    prev_ok = jnp.logical_and(j < ATTN_BLOCK, j > i)
    return cur_ok, prev_ok


def _attn_scores(q4, kcat, n, cur_ok, prev_ok):
    sc = lax.dot_general(q4, kcat, (NT, ((), ())), preferred_element_type=F32) * (HEAD_DIM ** -0.5)
    return jnp.where(jnp.logical_or(cur_ok, jnp.logical_and(prev_ok, n > 0)), sc, -1e30)


def _stack_heads(ref, rows):
    return jnp.concatenate([ref[rows, g * HEAD_DIM:(g + 1) * HEAD_DIM] for g in range(GQA)], 0)


def _per_head_column(values):
    r = lax.broadcasted_iota(jnp.int32, (GQA * ATTN_BLOCK, 1), 0) // ATTN_BLOCK
    col = jnp.zeros((GQA * ATTN_BLOCK, 1), F32)
    for g, val in enumerate(values):
        col = jnp.where(r == g, val, col)
    return col


def _attn_fwd(name, q, k, v, sinks):
    s = q.shape[0]
    nkv = k.shape[0]
    gw = GQA * HEAD_DIM
    nb = s // ATTN_BLOCK

    def body(sk_ref, q_ref, k_ref, v_ref, o_ref, lse_ref):
        h = pl.program_id(0)
        cur_ok, prev_ok = _attn_masks()
        sink = _per_head_column([sk_ref[h, g] for g in range(GQA)])

        def blk(n, carry):
            rows = pl.ds(pl.multiple_of(n * ATTN_BLOCK, ATTN_BLOCK), ATTN_BLOCK)
            prows = pl.ds(pl.multiple_of(jnp.maximum(n - 1, 0) * ATTN_BLOCK, ATTN_BLOCK), ATTN_BLOCK)
            kcat = jnp.concatenate([k_ref[prows, :], k_ref[rows, :]], 0)
            vcat = jnp.concatenate([v_ref[prows, :], v_ref[rows, :]], 0)
            sc = _attn_scores(_stack_heads(q_ref, rows), kcat, n, cur_ok, prev_ok)
            m = jnp.maximum(sc.max(-1, keepdims=True), sink)
            p = jnp.exp(sc - m)
            den = p.sum(-1, keepdims=True) + jnp.exp(sink - m)
            o = jnp.dot((p / den).astype(_MXU), vcat, preferred_element_type=F32)
            lse = m + jnp.log(den)
            for g in range(GQA):
                mine = slice(g * ATTN_BLOCK, (g + 1) * ATTN_BLOCK)
                o_ref[rows, g * HEAD_DIM:(g + 1) * HEAD_DIM] = o[mine, :]
                lse_ref[rows, g:g + 1] = lse[mine, :]
            return carry

        lax.fori_loop(0, nb, blk, 0)

    kv_spec = pl.BlockSpec((None, s, HEAD_DIM), lambda h: (h, 0, 0))
    return pl.pallas_call(
        body, grid=(nkv,),
        in_specs=[pl.BlockSpec(memory_space=pltpu.SMEM), pl.BlockSpec((s, gw), lambda h: (0, h)), kv_spec, kv_spec],
        out_specs=[pl.BlockSpec((s, gw), lambda h: (0, h)), pl.BlockSpec((None, s, GQA), lambda h: (h, 0, 0))],
        out_shape=[_sds((s, nkv * gw), F32), _sds((nkv, s, GQA), F32)], compiler_params=_cp(), name=name,
    )(sinks, q, k, v)


def _attn_bwd(name, q, k, v, o, dmix, lse, sinks):
    s = q.shape[0]
    nkv = k.shape[0]
    gw = GQA * HEAD_DIM
    nb = s // ATTN_BLOCK
    scale = HEAD_DIM ** -0.5

    def body(sk_ref, q_ref, k_ref, v_ref, o_ref, do_ref, lse_ref, dq_ref, dk_ref, dv_ref, dsk_ref):
        h = pl.program_id(0)
        cur_ok, prev_ok = _attn_masks()
        dk_ref[...] = jnp.zeros_like(dk_ref)
        dv_ref[...] = jnp.zeros_like(dv_ref)

        sink = _per_head_column([sk_ref[h, g] for g in range(GQA)])

        def blk(n, acc):
            rows = pl.ds(pl.multiple_of(n * ATTN_BLOCK, ATTN_BLOCK), ATTN_BLOCK)
            prows = pl.ds(pl.multiple_of(jnp.maximum(n - 1, 0) * ATTN_BLOCK, ATTN_BLOCK), ATTN_BLOCK)
            kcat = jnp.concatenate([k_ref[prows, :], k_ref[rows, :]], 0)
            vcat = jnp.concatenate([v_ref[prows, :], v_ref[rows, :]], 0)
            q4 = _stack_heads(q_ref, rows)
            do4 = _stack_heads(do_ref, rows)
            delta = jnp.sum(do4 * _stack_heads(o_ref, rows), -1, keepdims=True)
            dob = do4.astype(_MXU)
            lse = jnp.concatenate([lse_ref[rows, g:g + 1] for g in range(GQA)], 0)
            p = jnp.exp(_attn_scores(q4, kcat, n, cur_ok, prev_ok) - lse)
            dp = lax.dot_general(dob, vcat, (NT, ((), ())), preferred_element_type=F32)
            ds = (p * (dp - delta) * scale).astype(_MXU)
            dq = jnp.dot(ds, kcat, preferred_element_type=F32)
            for g in range(GQA):
                dq_ref[rows, g * HEAD_DIM:(g + 1) * HEAD_DIM] = dq[g * ATTN_BLOCK:(g + 1) * ATTN_BLOCK, :]
            dk = jnp.dot(q4.T, ds, preferred_element_type=F32).T
            dv = jnp.dot(dob.T, p.astype(_MXU), preferred_element_type=F32).T
            dk_ref[prows, :] += dk[:ATTN_BLOCK, :]
            dv_ref[prows, :] += dv[:ATTN_BLOCK, :]
            dk_ref[rows, :] += dk[ATTN_BLOCK:, :]
            dv_ref[rows, :] += dv[ATTN_BLOCK:, :]
            return acc - jnp.exp(sink - lse) * delta

        acc = lax.fori_loop(0, nb, blk, jnp.zeros((GQA * ATTN_BLOCK, 1), F32))
        for g in range(GQA):
            dsk_ref[:, g:g + 1] = jnp.sum(acc[g * ATTN_BLOCK:(g + 1) * ATTN_BLOCK, :], 0, keepdims=True)

    kv_spec = pl.BlockSpec((None, s, HEAD_DIM), lambda h: (h, 0, 0))
    qcols = pl.BlockSpec((s, gw), lambda h: (0, h))
    return pl.pallas_call(
        body, grid=(nkv,),
        in_specs=[pl.BlockSpec(memory_space=pltpu.SMEM), qcols, kv_spec, kv_spec, qcols, qcols,
                  pl.BlockSpec((None, s, GQA), lambda h: (h, 0, 0))],
        out_specs=[qcols, kv_spec, kv_spec, pl.BlockSpec((None, 1, GQA), lambda h: (h, 0, 0))],
        out_shape=[_sds((s, nkv * gw), F32), _sds((nkv, s, HEAD_DIM), F32), _sds((nkv, s, HEAD_DIM), F32),
                   _sds((nkv, 1, GQA), F32)],
        compiler_params=_cp(), name=name,
    )(sinks, q, k, v, o, dmix, lse)


def _shift_down(a, k, t):
    return jnp.where(t >= k, pltpu.roll(a, k, 0), 0.0)


def _shift_up(a, k, t):
    n = a.shape[0]
    return jnp.where(t < n - k, pltpu.roll(a, n - k, 0), 0.0)


def _pool_fwd(name, h, col_block, pool_w, pool_scale):
    s = h.shape[0]
    ng, pg = pool_w.shape[0], pool_w.shape[1]
    pw_ = ng * pg

    def body(u_ref, w_ref, sc_ref, y_ref, pre_ref):
        t = lax.broadcasted_iota(jnp.int32, (s, pg), 0)
        for gi, win in enumerate(POOL_WINDOWS):
            cols = slice(gi * pg, (gi + 1) * pg)
            u = u_ref[:, cols]
            a = u
            k = 1
            while k < win:
                a = a + _shift_down(a, k, t)
                k *= 2
            div = jnp.minimum(t + 1, win).astype(F32)
            pre = (a / div - u).astype(_MXU)
            pre_ref[:, cols] = pre
            out = jnp.dot(pre, w_ref[gi], preferred_element_type=F32)
            y_ref[:, cols] = (out * sc_ref[:, cols]).astype(y_ref.dtype)

    blk = pl.BlockSpec((s, pw_), lambda i: (0, 0))
    return pl.pallas_call(
        body, grid=(1,),
        in_specs=[pl.BlockSpec((s, pw_), lambda i: (0, col_block)), pl.BlockSpec((ng, pg, pg), lambda i: (0, 0, 0)),
                  pl.BlockSpec((1, pw_), lambda i: (0, 0))],
        out_specs=[blk, blk], out_shape=[_sds((s, pw_), _MXU), _sds((s, pw_), _MXU)], compiler_params=_cp(), name=name,
    )(h, pool_w, pool_scale)


def _pool_bwd(name, dmix, col_block, pre, pool_w, pool_scale):
    s = pre.shape[0]
    ng, pg = pool_w.shape[0], pool_w.shape[1]
    pw_ = ng * pg

    def body(dy_ref, pre_ref, w_ref, sc_ref, du_ref, dw_ref, dsc_ref):
        t = lax.broadcasted_iota(jnp.int32, (s, pg), 0)
        for gi, win in enumerate(POOL_WINDOWS):
            cols = slice(gi * pg, (gi + 1) * pg)
            pre_g = pre_ref[:, cols]
            dy = dy_ref[:, cols]
            out = jnp.dot(pre_g, w_ref[gi], preferred_element_type=F32)
            dsc_ref[:, cols] = jnp.sum(dy * out, 0, keepdims=True)
            dout = (dy * sc_ref[:, cols]).astype(_MXU)
            dw_ref[gi] = lax.dot_general(pre_g, dout, (TN, ((), ())), preferred_element_type=F32)
            dpre = lax.dot_general(dout, w_ref[gi], (NT, ((), ())), preferred_element_type=F32)
            div = jnp.minimum(t + 1, win).astype(F32)
            a = dpre / div
            k = 1
            while k < win:
                a = a + _shift_up(a, k, t)
                k *= 2
            du_ref[:, cols] = (a - dpre).astype(du_ref.dtype)

    blk = pl.BlockSpec((s, pw_), lambda i: (0, 0))
    wspec = pl.BlockSpec((ng, pg, pg), lambda i: (0, 0, 0))
    vec = pl.BlockSpec((1, pw_), lambda i: (0, 0))
    return pl.pallas_call(
        body, grid=(1,), in_specs=[pl.BlockSpec((s, pw_), lambda i: (0, col_block)), blk, wspec, vec],
        out_specs=[blk, wspec, vec], out_shape=[_sds((s, pw_), _MXU), _sds((ng, pg, pg), F32), _sds((1, pw_), F32)],
        compiler_params=_cp(), name=name,
    )(dmix, pre, pool_w, pool_scale)


def _scan_chunks(xr_ref, xi_ref, sr_ref, si_ref, ar, ai, reverse):
    n, c = xr_ref.shape
    tt = min(SCAN_T, n)
    lw = min(SCAN_LANES, c)
    nchunk, ngroup = n // tt, tt // 8
    t8 = lax.broadcasted_iota(jnp.int32, (tt, lw), 0) % 8

    for l0 in range(0, c, lw):
        lanes = slice(l0, l0 + lw)
        a_r, a_i = ar[:, lanes], ai[:, lanes]

        def within8(vr, vi, a_r=a_r, a_i=a_i):
            rows = vr.shape[0]
            tq = t8[:rows, :]
            pr, pi = a_r, a_i
            k = 1
            while k < 8:
                if reverse:
                    hr = jnp.where(tq < 8 - k, pltpu.roll(vr, rows - k, 0), 0.0)
                    hi = jnp.where(tq < 8 - k, pltpu.roll(vi, rows - k, 0), 0.0)
                else:
                    hr = jnp.where(tq >= k, pltpu.roll(vr, k, 0), 0.0)
                    hi = jnp.where(tq >= k, pltpu.roll(vi, k, 0), 0.0)
                vr, vi = vr + pr * hr - pi * hi, vi + pr * hi + pi * hr
                pr, pi = pr * pr - pi * pi, 2.0 * pr * pi
                k *= 2
            return vr, vi

        at_edge = t8[:8, :] == (7 if reverse else 0)
        pw_r, pw_i = within8(jnp.where(at_edge, a_r, 0.0), jnp.where(at_edge, a_i, 0.0))
        last = 0 if reverse else 7

        def body(i, carry, lanes=lanes, within8=within8, pw_r=pw_r, pw_i=pw_i):
            cr, ci = carry
            ch = nchunk - 1 - i if reverse else i
            rows = pl.ds(pl.multiple_of(ch * tt, tt), tt)
            vr, vi = within8(xr_ref[rows, lanes], xi_ref[rows, lanes])
            out_r, out_i = [None] * ngroup, [None] * ngroup
            for g in (reversed(range(ngroup)) if reverse else range(ngroup)):
                br = vr[8 * g:8 * g + 8, :] + pw_r * cr - pw_i * ci
                bi = vi[8 * g:8 * g + 8, :] + pw_r * ci + pw_i * cr
                out_r[g], out_i[g] = br, bi
                cr, ci = br[last:last + 1, :], bi[last:last + 1, :]
            sr_ref[rows, lanes] = jnp.concatenate(out_r, 0)
            si_ref[rows, lanes] = jnp.concatenate(out_i, 0)
            return cr, ci

        lax.fori_loop(0, nchunk, body, (jnp.zeros((1, lw), F32), jnp.zeros((1, lw), F32)))


_GELU_K = math.sqrt(2.0 / math.pi)


def _gelu_grad(y):
    inner = _GELU_K * (y + 0.044715 * y * y * y)
    th = jnp.tanh(inner)
    return 0.5 * (1.0 + th) + 0.5 * y * (1.0 - th * th) * _GELU_K * (1.0 + 3.0 * 0.044715 * y * y)


def _ssm_fwd(name, h, u_block0, bdr, bdi, cdr, cdi, dvec, ar, ai):
    s = h.shape[0]
    nt, cw, lw = bdr.shape
    rc = min(256, s)

    def body(u_ref, bdr_ref, bdi_ref, cdr_ref, cdi_ref, d_ref, ar_ref, ai_ref, sr_ref, si_ref, y_ref, yg_ref):
        def mm_in(c, _):
            rows = pl.ds(pl.multiple_of(c * rc, rc), rc)
            ub = u_ref[rows, :].astype(_MXU)
            sr_ref[rows, :] = jnp.dot(ub, bdr_ref[...], preferred_element_type=F32)
            si_ref[rows, :] = jnp.dot(ub, bdi_ref[...], preferred_element_type=F32)
            return 0

        lax.fori_loop(0, s // rc, mm_in, 0)
        _scan_chunks(sr_ref, si_ref, sr_ref, si_ref, ar_ref[...], ai_ref[...], reverse=False)

        def mm_out(c, _):
            rows = pl.ds(pl.multiple_of(c * rc, rc), rc)
            y = (jnp.dot(sr_ref[rows, :].astype(_MXU), cdr_ref[...], preferred_element_type=F32)
                 - jnp.dot(si_ref[rows, :].astype(_MXU), cdi_ref[...], preferred_element_type=F32)
                 + d_ref[...] * u_ref[rows, :])
            y_ref[rows, :] = y
            yg_ref[rows, :] = jax.nn.gelu(y).astype(yg_ref.dtype)
            return 0

        lax.fori_loop(0, s // rc, mm_out, 0)

    st = pl.BlockSpec((s, lw), lambda j: (0, j))
    ch = pl.BlockSpec((s, cw), lambda j: (0, j))
    bspec = pl.BlockSpec((None, cw, lw), lambda j: (j, 0, 0))
    cspec = pl.BlockSpec((None, lw, cw), lambda j: (j, 0, 0))
    return pl.pallas_call(
        body, grid=(nt,),
        in_specs=[pl.BlockSpec((s, cw), lambda j: (0, u_block0 + j)), bspec, bspec, cspec, cspec,
                  pl.BlockSpec((1, cw), lambda j: (0, j)), pl.BlockSpec((1, lw), lambda j: (0, j)),
                  pl.BlockSpec((1, lw), lambda j: (0, j))],
        out_specs=[st, st, ch, ch],
        out_shape=[_sds((s, nt * lw), F32), _sds((s, nt * lw), F32), _sds((s, nt * cw), F32), _sds((s, nt * cw), _MXU)],
        compiler_params=_cp(), name=name,
    )(h, bdr, bdi, cdr, cdi, dvec, ar, ai)


def _ssm_bwd(name, dyg, ypre, h, u_block0, sr, si, bdr, bdi, cdr, cdi, dvec, ar, ai):
    s = h.shape[0]
    nt, cw, lw = bdr.shape
    rc = min(256, s)

    def body(dyg_ref, yp_ref, u_ref, sr_ref, si_ref, bdr_ref, bdi_ref, cdr_ref, cdi_ref, d_ref, ar_ref, ai_ref,
             du_ref, dd_ref, dcr_ref, dci_ref, dbr_ref, dbi_ref, dar_ref, dai_ref, lr_scr, li_scr, dy_scr):
        for ref in (dd_ref, dcr_ref, dci_ref, dbr_ref, dbi_ref, dar_ref, dai_ref):
            ref[...] = jnp.zeros_like(ref)

        def p1(c, _):
            rows = pl.ds(pl.multiple_of(c * rc, rc), rc)
            dy = dyg_ref[rows, :] * _gelu_grad(yp_ref[rows, :])
            dy_scr[rows, :] = dy
            dd_ref[...] += jnp.sum(dy * u_ref[rows, :], 0, keepdims=True)
            dyb = dy.astype(_MXU)
            lr_scr[rows, :] = lax.dot_general(dyb, cdr_ref[...], (NT, ((), ())), preferred_element_type=F32)
            li_scr[rows, :] = -lax.dot_general(dyb, cdi_ref[...], (NT, ((), ())), preferred_element_type=F32)
            dcr_ref[...] += lax.dot_general(sr_ref[rows, :].astype(_MXU), dyb, (TN, ((), ())), preferred_element_type=F32)
            dci_ref[...] -= lax.dot_general(si_ref[rows, :].astype(_MXU), dyb, (TN, ((), ())), preferred_element_type=F32)
            return 0

        lax.fori_loop(0, s // rc, p1, 0)
        _scan_chunks(lr_scr, li_scr, lr_scr, li_scr, ar_ref[...], -ai_ref[...], reverse=True)
        t = lax.broadcasted_iota(jnp.int32, (rc, lw), 0)

        def p2(c, _):
            r0 = pl.multiple_of(c * rc, rc)
            rows = pl.ds(r0, rc)
            before = pl.ds(pl.multiple_of(jnp.maximum(r0 - 8, 0), 8), 8)
            have = (c > 0).astype(F32)
            lr, li = lr_scr[rows, :], li_scr[rows, :]
            spr = jnp.where(t == 0, sr_ref[before, :][7:8, :] * have, pltpu.roll(sr_ref[rows, :], 1, 0))
            spi = jnp.where(t == 0, si_ref[before, :][7:8, :] * have, pltpu.roll(si_ref[rows, :], 1, 0))
            dar_ref[...] += jnp.sum(lr * spr + li * spi, 0, keepdims=True)
            dai_ref[...] += jnp.sum(li * spr - lr * spi, 0, keepdims=True)
            lrb, lib = lr.astype(_MXU), li.astype(_MXU)
            du = (dy_scr[rows, :] * d_ref[...]
                  + lax.dot_general(lrb, bdr_ref[...], (NT, ((), ())), preferred_element_type=F32)
                  + lax.dot_general(lib, bdi_ref[...], (NT, ((), ())), preferred_element_type=F32))
            du_ref[rows, :] = du.astype(du_ref.dtype)
            ub = u_ref[rows, :].astype(_MXU)
            dbr_ref[...] += lax.dot_general(ub, lrb, (TN, ((), ())), preferred_element_type=F32)
            dbi_ref[...] += lax.dot_general(ub, lib, (TN, ((), ())), preferred_element_type=F32)
            return 0

        lax.fori_loop(0, s // rc, p2, 0)

    st = pl.BlockSpec((s, lw), lambda j: (0, j))
    ch = pl.BlockSpec((s, cw), lambda j: (0, j))
    bspec = pl.BlockSpec((None, cw, lw), lambda j: (j, 0, 0))
    cspec = pl.BlockSpec((None, lw, cw), lambda j: (j, 0, 0))
    cvec = pl.BlockSpec((1, cw), lambda j: (0, j))
    svec = pl.BlockSpec((1, lw), lambda j: (0, j))
    return pl.pallas_call(
        body, grid=(nt,),
        in_specs=[ch, ch, pl.BlockSpec((s, cw), lambda j: (0, u_block0 + j)), st, st, bspec, bspec, cspec, cspec, cvec, svec, svec],
        out_specs=[ch, cvec, cspec, cspec, bspec, bspec, svec, svec],
        out_shape=[_sds((s, nt * cw), _MXU), _sds((1, nt * cw), F32), _sds((nt, lw, cw), F32), _sds((nt, lw, cw), F32),
                   _sds((nt, cw, lw), F32), _sds((nt, cw, lw), F32), _sds((1, nt * lw), F32), _sds((1, nt * lw), F32)],
        scratch_shapes=[pltpu.VMEM((s, lw), F32), pltpu.VMEM((s, lw), F32), pltpu.VMEM((s, cw), F32)],
        compiler_params=_cp(), name=name,
    )(dyg, ypre, h, sr, si, bdr, bdi, cdr, cdi, dvec, ar, ai)


def _glu_fwd(name, yg, gw):
    s, w = yg.shape
    tr = min(512, s)

    def body(y_ref, w_ref, o_ref, ab_ref):
        ab = jnp.dot(y_ref[...], w_ref[...], preferred_element_type=F32)
        ab_ref[...] = ab
        o_ref[...] = (ab[:, :w] * jax.nn.sigmoid(ab[:, w:])).astype(o_ref.dtype)

    return pl.pallas_call(
        body, grid=(s // tr,), in_specs=[pl.BlockSpec((tr, w), lambda i: (i, 0)), _resident((w, 2 * w), lambda i: (0, 0))],
        out_specs=[pl.BlockSpec((tr, w), lambda i: (i, 0)), pl.BlockSpec((tr, 2 * w), lambda i: (i, 0))],
        out_shape=[_sds((s, w), _MXU), _sds((s, 2 * w), F32)], compiler_params=_cp(), name=name,
    )(yg, gw)


def _glu_bwd(name, dmix, col_block, ab, gw):
    s = ab.shape[0]
    w = ab.shape[1] // 2
    tr = min(512, s)

    def body(do_ref, ab_ref, w_ref, dab_ref, dy_ref):
        do = do_ref[...]
        a, b = ab_ref[:, :w], ab_ref[:, w:]
        sg = jax.nn.sigmoid(b)
        da = (do * sg).astype(_MXU)
        db = (do * a * sg * (1.0 - sg)).astype(_MXU)
        dab_ref[:, :w] = da
        dab_ref[:, w:] = db
        dy_ref[...] = (lax.dot_general(da, w_ref[:, :w], (NT, ((), ())), preferred_element_type=F32)
                       + lax.dot_general(db, w_ref[:, w:], (NT, ((), ())), preferred_element_type=F32))

    return pl.pallas_call(
        body, grid=(s // tr,),
        in_specs=[pl.BlockSpec((tr, w), lambda i: (i, col_block)), pl.BlockSpec((tr, 2 * w), lambda i: (i, 0)),
                  _resident((w, 2 * w), lambda i: (0, 0))],
        out_specs=[pl.BlockSpec((tr, 2 * w), lambda i: (i, 0)), pl.BlockSpec((tr, w), lambda i: (i, 0))],
        out_shape=[_sds((s, 2 * w), _MXU), _sds((s, w), F32)], compiler_params=_cp(), name=name,
    )(dmix, ab, gw)


CONV_ROWS = 64


def _conv_chunk(ref, w_ref, b_ref, c, tt):
    r0 = pl.multiple_of(c * tt, tt)
    before = ref[pl.ds(pl.multiple_of(jnp.maximum(r0 - 8, 0), 8), 8), :]
    before = jnp.where(c > 0, before, 0.0)
    main = ref[pl.ds(r0, tt), :]
    ext = jnp.concatenate([before, main], 0)
    d1 = pltpu.roll(ext, 1, 0)[8:, :]
    d2 = pltpu.roll(ext, 2, 0)[8:, :]
    hc = b_ref[...] + d2 * w_ref[0:1, :]
    hc = hc + d1 * w_ref[1:2, :]
    return hc + main * w_ref[2:3, :], main, d1, d2


def _conv_act_fwd(name, hu, cw, cb):
    s, f2 = hu.shape
    f = f2 // 2
    tw = _col_tile(f, 256)
    nt = f // tw

    tt = min(CONV_ROWS, s)

    def body(v_ref, g_ref, wv_ref, wg_ref, bv_ref, bg_ref, act_ref):
        def chunk(c, _):
            val = _conv_chunk(v_ref, wv_ref, bv_ref, c, tt)[0]
            gate = _conv_chunk(g_ref, wg_ref, bg_ref, c, tt)[0]
            act_ref[pl.ds(pl.multiple_of(c * tt, tt), tt), :] = (jax.nn.silu(gate) * val).astype(act_ref.dtype)
            return 0

        lax.fori_loop(0, s // tt, chunk, 0)

    cv = lambda rows: pl.BlockSpec((rows, tw), lambda i: (0, i))
    cg = lambda rows: pl.BlockSpec((rows, tw), lambda i: (0, nt + i))
    return pl.pallas_call(
        body, grid=(nt,), in_specs=[cv(s), cg(s), cv(CONV_WIDTH), cg(CONV_WIDTH), cv(1), cg(1)],
        out_specs=cv(s), out_shape=_sds((s, f), _MXU), compiler_params=_cp(), name=name,
    )(hu, hu, cw, cw, cb, cb)


def _conv_act_bwd(name, dact, hu, cw, cb):
    s, f2 = hu.shape
    f = f2 // 2
    tw = _col_tile(f, 256)
    nt = f // tw

    tt = min(CONV_ROWS, s)
    nchunk = s // tt

    def body(da_ref, v_ref, g_ref, wv_ref, wg_ref, bv_ref, bg_ref, dh_ref, dwv_ref, dwg_ref, dbv_ref, dbg_ref):
        def chunk(i, carry):
            c = nchunk - 1 - i
            rows = pl.ds(pl.multiple_of(c * tt, tt), tt)
            val, hv, hv1, hv2 = _conv_chunk(v_ref, wv_ref, bv_ref, c, tt)
            gate, hg, hg1, hg2 = _conv_chunk(g_ref, wg_ref, bg_ref, c, tt)
            sg = jax.nn.sigmoid(gate)
            da = da_ref[rows, :]
            dval = da * (gate * sg)
            dgate = da * val * sg * (1.0 + gate * (1.0 - sg))
            out = []
            for part, dhc, taps, w_ref, (after, acc) in ((0, dval, (hv2, hv1, hv), wv_ref, carry[0]),
                                                        (1, dgate, (hg2, hg1, hg), wg_ref, carry[1])):
                ext = jnp.concatenate([dhc, after], 0)
                u1 = pltpu.roll(ext, tt + 8 - 1, 0)[:tt, :]
                u2 = pltpu.roll(ext, tt + 8 - 2, 0)[:tt, :]
                dh = dhc * w_ref[2:3, :] + u1 * w_ref[1:2, :] + u2 * w_ref[0:1, :]
                dh_ref[part, rows, :] = dh.astype(dh_ref.dtype)
                sums = [jnp.sum((dhc * tap).reshape(tt // 8, 8, tw), 0) for tap in taps]
                sums.append(jnp.sum(dhc.reshape(tt // 8, 8, tw), 0))
                out.append((dhc[0:8, :], tuple(a + b for a, b in zip(acc, sums))))
            return tuple(out)

        zero = (jnp.zeros((8, tw), F32), tuple(jnp.zeros((8, tw), F32) for _ in range(CONV_WIDTH + 1)))
        (_, acc_v), (_, acc_g) = lax.fori_loop(0, nchunk, chunk, (zero, zero))
        for acc, dw_ref, db_ref in ((acc_v, dwv_ref, dbv_ref), (acc_g, dwg_ref, dbg_ref)):
            for tap in range(CONV_WIDTH):
                dw_ref[tap:tap + 1, :] = jnp.sum(acc[tap], 0, keepdims=True)
            db_ref[...] = jnp.sum(acc[CONV_WIDTH], 0, keepdims=True)

    cv = lambda rows: pl.BlockSpec((rows, tw), lambda i: (0, i))
    cg = lambda rows: pl.BlockSpec((rows, tw), lambda i: (0, nt + i))
    both = pl.BlockSpec((2, s, tw), lambda i: (0, 0, i))
    dh, dwv, dwg, dbv, dbg = pl.pallas_call(
        body, grid=(nt,), in_specs=[cv(s), cv(s), cg(s), cv(CONV_WIDTH), cg(CONV_WIDTH), cv(1), cg(1)],
        out_specs=[both, cv(CONV_WIDTH), cv(CONV_WIDTH), cv(1), cv(1)],
        out_shape=[_sds((2, s, f), _MXU), _sds((CONV_WIDTH, f), F32), _sds((CONV_WIDTH, f), F32),
                   _sds((1, f), F32), _sds((1, f), F32)],
        compiler_params=_cp(), name=name,
    )(dact, hu, hu, cw, cw, cb, cb)
    return dh, jnp.concatenate([dwv, dwg], 1), jnp.concatenate([dbv, dbg], 1)


ELEM_BLOCK = 512 * 1024


def _elem_tiles(r, c, budget=ELEM_BLOCK):
    rows = [t for t in range(8, r + 1, 8) if r % t == 0] or [r]
    cols = [t for t in range(V7X_LANES, c + 1, V7X_LANES) if c % t == 0] or [c]
    fits = [(tr * tc, tc, tr) for tr in rows for tc in cols if tr * tc <= budget]
    if not fits:
        return min(rows), min(cols)
    _, tc, tr = max(fits)
    return tr, tc


def _sum_parts(name, parts):
    n, r, c = parts.shape
    tr, tc = _elem_tiles(r, c, ELEM_BLOCK // n)

    def body(p_ref, g_ref):
        g = p_ref[0]
        for i in range(1, n):
            g = g + p_ref[i]
        g_ref[...] = g

    return pl.pallas_call(
        body, grid=(r // tr, c // tc), in_specs=[pl.BlockSpec((n, tr, tc), lambda i, j: (0, i, j))],
        out_specs=pl.BlockSpec((tr, tc), lambda i, j: (i, j)), out_shape=_sds((r, c), F32), compiler_params=_cp(), name=name,
    )(parts)


def _adamw_2d(name, g, w, m, v):
    r, c = w.shape
    tc = c if c % V7X_LANES else _col_tile(c, 2048)
    rows = [t for t in range(8, r + 1, 8) if r % t == 0 and t * max(tc, V7X_LANES) <= ELEM_BLOCK // 4] or [r]
    tr = max(rows)
    c1 = 1.0 - ADAM_B1 ** ADAM_STEP
    c2 = 1.0 - ADAM_B2 ** ADAM_STEP

    def body(g_ref, w_ref, m_ref, v_ref, d_ref, nm_ref, nv_ref):
        g = g_ref[...]
        nm = ADAM_B1 * m_ref[...] + (1.0 - ADAM_B1) * g
        nv = ADAM_B2 * v_ref[...] + (1.0 - ADAM_B2) * (g * g)
        m_hat = nm / c1
        v_hat = nv / c2
        nm_ref[...] = nm
        nv_ref[...] = nv
        d_ref[...] = -ADAM_LR * (m_hat / (jnp.sqrt(v_hat) + ADAM_EPS) + ADAM_WD * w_ref[...])

    blk = pl.BlockSpec((tr, tc), lambda i, j: (i, j))
    out = _sds((r, c), F32)
    return pl.pallas_call(
        body, grid=(r // tr, c // tc), in_specs=[blk] * 4, out_specs=[blk] * 3, out_shape=[out] * 3,
        compiler_params=_cp(), name=name,
    )(g, w, m, v)


def _pair_sum(name, mine, theirs, c_idx):
    _, _, r, c = mine.shape
    tr, tc = _elem_tiles(r, c)

    def body(c_ref, a_ref, b_ref, o_ref):
        o_ref[...] = (a_ref[...].astype(F32) + b_ref[...].astype(F32)).astype(o_ref.dtype)

    return pl.pallas_call(
        body,
        grid_spec=pltpu.PrefetchScalarGridSpec(
            num_scalar_prefetch=1, grid=(4, r // tr, c // tc),
            in_specs=[pl.BlockSpec((None, None, tr, tc), lambda p, i, j, cref: (p, cref[0], i, j)),
                      pl.BlockSpec((None, tr, tc), lambda p, i, j, cref: (p, i, j))],
            out_specs=pl.BlockSpec((None, tr, tc), lambda p, i, j, cref: (p, i, j))),
        out_shape=_sds((4, r, c), _WIRE), compiler_params=_cp(), name=name,
    )(c_idx, mine, theirs)


def _place():
    return lax.axis_index("x"), lax.axis_index("y"), lax.axis_index("c")


def _all_gather(name, xs):
    n = len(xs)

    def body(*refs):
        x_refs, o_refs = refs[:n], refs[n:2 * n]
        send_sems, recv_sems, local_sems = refs[2 * n:]
        x, y, c = _place()
        me, sibling = (x, y, c), (x, y, 1 - c)
        chips = [(1 - x, y), (x, 1 - y), (1 - x, 1 - y)]

        def copy(a, k, block, to, src=None):
            px, py, pc = block
            rows = o_refs[a].at[4 * px + 2 * py + pc]
            return pltpu.make_async_remote_copy(
                src_ref=rows if src is None else src, dst_ref=rows, send_sem=send_sems.at[a, k], recv_sem=recv_sems.at[a, k],
                device_id=to, device_id_type=MESH)

        sent = []
        mine = []
        for a in range(n):
            mx, my, mc = me
            cp = pltpu.make_async_copy(x_refs[a], o_refs[a].at[4 * mx + 2 * my + mc], local_sems.at[a])
            cp.start()
            mine.append(cp)
            first = [copy(a, 0, me, sibling, src=x_refs[a])]
            first += [copy(a, 1 + j, me, (*chip, c), src=x_refs[a]) for j, chip in enumerate(chips)]
            for cp in first:
                cp.start()
            sent += first
        for a in range(n):
            for j, chip in enumerate(chips):
                copy(a, 1 + j, (*chip, c), me).wait_recv()
                fwd = copy(a, 4 + j, (*chip, c), sibling)
                fwd.start()
                sent.append(fwd)
        for a in range(n):
            copy(a, 0, sibling, me).wait_recv()
            for j, chip in enumerate(chips):
                copy(a, 4 + j, (*chip, 1 - c), me).wait_recv()
        for cp in sent:
            cp.wait_send()
        for cp in mine:
            cp.wait()

    return pl.pallas_call(
        body, in_specs=[ANY] * n, out_specs=[ANY] * n,
        out_shape=[_sds((N_DEV,) + a.shape, a.dtype) for a in xs],
        scratch_shapes=[pltpu.SemaphoreType.DMA((n, 7)), pltpu.SemaphoreType.DMA((n, 7)), pltpu.SemaphoreType.DMA((n,))],
        name=name,
    )(*xs)


HBM = pl.BlockSpec(memory_space=pltpu.HBM)
SEM = pl.BlockSpec(memory_space=pltpu.SEMAPHORE)
DATAFLOW = pltpu.SideEffectType.DATAFLOW_SIDE_EFFECTING


def _in_hbm(a):
    return pltpu.with_memory_space_constraint(a, pltpu.HBM)


def _split_copy_start(name, srcs, lands, copies, deps):
    n, nd = len(srcs), len(deps)
    per = len(copies([None] * n, [None] * n, probe=True)) // n

    def body(*refs):
        s_refs, l_refs = refs[:n], refs[n:2 * n]
        send_sems, recv_sems = refs[2 * n + nd], refs[2 * n + nd + 1]
        token = refs[-1]
        for a, k, src, dst, to in copies(s_refs, l_refs):
            pltpu.make_async_remote_copy(src_ref=src, dst_ref=dst, send_sem=send_sems.at[a * per + k],
                                         recv_sem=recv_sems.at[a * per + k], device_id=to, device_id_type=MESH).start()
        token[...] = jnp.zeros_like(token)

    both = list(srcs) + list(lands)
    outs = pl.pallas_call(
        body, name=name,
        out_shape=(pltpu.SemaphoreType.DMA((n * per,)), pltpu.SemaphoreType.DMA((n * per,)),
                   *[pltpu.HBM(a.shape, a.dtype) for a in both], _sds((8, V7X_LANES), F32)),
        in_specs=[HBM] * (2 * n) + [ANY] * nd,
        out_specs=(SEM, SEM, *[HBM] * (2 * n), pl.BlockSpec(memory_space=pltpu.VMEM)),
        input_output_aliases={i: 2 + i for i in range(2 * n)},
        compiler_params=pltpu.CompilerParams(has_side_effects=DATAFLOW),
    )(*[_in_hbm(a) for a in both], *deps)
    return outs[0], outs[1], list(outs[2:2 + n]), list(outs[2 + n:2 + 2 * n]), outs[-1]


def _split_copy_wait(name, send_sems, recv_sems, srcs, lands, arrivals, after):
    n = len(srcs)
    per = len(arrivals([None] * n, [None] * n, probe=True)) // n

    def body(*refs):
        s_refs, l_refs = refs[:n], refs[n:2 * n]
        send_sems_, recv_sems_ = refs[2 * n], refs[2 * n + 1]
        for a, k, src, dst, frm in arrivals(s_refs, l_refs):
            cp = pltpu.make_async_remote_copy(src_ref=src, dst_ref=dst, send_sem=send_sems_.at[a * per + k],
                                              recv_sem=recv_sems_.at[a * per + k], device_id=frm, device_id_type=MESH)
            cp.wait_send()
            cp.wait_recv()

    both = list(srcs) + list(lands)
    outs = pl.pallas_call(
        body, name=name, out_shape=tuple(pltpu.HBM(a.shape, a.dtype) for a in both),
        in_specs=[HBM] * (2 * n) + [SEM, SEM, ANY], out_specs=tuple([HBM] * (2 * n)),
        input_output_aliases={i: i for i in range(2 * n)},
        compiler_params=pltpu.CompilerParams(has_side_effects=DATAFLOW),
    )(*both, send_sems, recv_sems, after)
    return list(outs[:n]), list(outs[n:])


def _gather_copies(arriving):
    def copies(s_refs, l_refs, probe=False):
        if probe:
            return [None] * (4 * len(s_refs))
        x, y, c = _place()
        out = []
        for a in range(len(s_refs)):
            for k, (px, py, pc) in enumerate([(x, y, 1 - c), (1 - x, y, c), (x, 1 - y, c), (1 - x, 1 - y, c)]):
                slot = 4 * px + 2 * py + pc if arriving else 4 * x + 2 * y + c
                out.append((a, k, s_refs[a], l_refs[a].at[slot], (px, py, pc)))
        return out
    return copies


def _chip_copies(arriving):
    def copies(s_refs, l_refs, probe=False):
        if probe:
            return [None] * (3 * len(s_refs))
        x, y, c = _place()
        out = []
        for a in range(len(s_refs)):
            for j, (px, py) in enumerate([(1 - x, y), (x, 1 - y), (1 - x, 1 - y)]):
                src = s_refs[a].at[2 * x + y] if arriving else s_refs[a].at[2 * px + py]
                out.append((a, j, src, l_refs[a].at[j], (px, py, c)))
        return out
    return copies


def _sibling_copies(s_refs, l_refs, probe=False):
    if probe:
        return [None] * (4 * len(s_refs))
    x, y, c = _place()
    return [(a, p, s_refs[a].at[p, 1 - c], l_refs[a].at[p], (x, y, 1 - c)) for a in range(len(s_refs)) for p in range(4)]


def _sibling_begin(name, by_owner, deps):
    lands = [lax.empty((4,) + a.shape[2:], a.dtype) for a in by_owner]
    return _split_copy_start(name + "_start", by_owner, lands, _sibling_copies, deps)


def _sibling_end(name, handle, after):
    send_sems, recv_sems, srcs, lands, _ = handle
    return _split_copy_wait(name + "_wait", send_sems, recv_sems, srcs, lands, _sibling_copies, after)


def _gather_begin(name, shards, deps):
    x, y, c = _place()
    lands = [lax.dynamic_update_slice_in_dim(lax.empty((N_DEV,) + a.shape, a.dtype), a[None], 4 * x + 2 * y + c, 0)
             for a in shards]
    return _split_copy_start(name + "_start", shards, lands, _gather_copies(False), deps)


def _gather_end(name, handle, after):
    send_sems, recv_sems, srcs, lands, _ = handle
    _, lands = _split_copy_wait(name + "_wait", send_sems, recv_sems, srcs, lands, _gather_copies(True), after)
    return _gather_forward(name + "_forward", lands)


def _forward_copies(arriving):
    def copies(s_refs, l_refs, probe=False):
        if probe:
            return [None] * (3 * len(l_refs))
        x, y, c = _place()
        out = []
        for a in range(len(l_refs)):
            for j, (px, py) in enumerate([(1 - x, y), (x, 1 - y), (1 - x, 1 - y)]):
                mine, theirs = l_refs[a].at[4 * px + 2 * py + c], l_refs[a].at[4 * px + 2 * py + 1 - c]
                out.append((a, j, mine, theirs if arriving else mine, (x, y, 1 - c)))
        return out
    return copies


def _gather_arrived(name, handle, after):
    send_sems, recv_sems, srcs, lands, _ = handle
    srcs, lands = _split_copy_wait(name + "_wait", send_sems, recv_sems, srcs, lands, _gather_copies(True), after)
    return _split_copy_start(name + "_forward_start", srcs, lands, _forward_copies(False), [])


def _gather_done(name, handle, after):
    send_sems, recv_sems, srcs, lands, _ = handle
    return _split_copy_wait(name + "_forward_wait", send_sems, recv_sems, srcs, lands, _forward_copies(True), after)[1]


def _gather_forward(name, lands):
    n = len(lands)

    def body(*refs):
        o_refs = refs[n:2 * n]
        send_sems, recv_sems = refs[2 * n:]
        x, y, c = _place()
        sibling = (x, y, 1 - c)
        chips = [(1 - x, y), (x, 1 - y), (1 - x, 1 - y)]
        sent = []
        for a in range(n):
            for j, (px, py) in enumerate(chips):
                rows = o_refs[a].at[4 * px + 2 * py + c]
                cp = pltpu.make_async_remote_copy(src_ref=rows, dst_ref=rows, send_sem=send_sems.at[a, j],
                                                  recv_sem=recv_sems.at[a, j], device_id=sibling, device_id_type=MESH)
                cp.start()
                sent.append(cp)
        for a in range(n):
            for j, (px, py) in enumerate(chips):
                rows = o_refs[a].at[4 * px + 2 * py + 1 - c]
                pltpu.make_async_remote_copy(src_ref=rows, dst_ref=rows, send_sem=send_sems.at[a, j],
                                             recv_sem=recv_sems.at[a, j], device_id=sibling, device_id_type=MESH).wait_recv()
        for cp in sent:
            cp.wait_send()

    return pl.pallas_call(
        body, in_specs=[ANY] * n, out_specs=[ANY] * n, out_shape=[_sds(a.shape, a.dtype) for a in lands],
        input_output_aliases={i: i for i in range(n)},
        scratch_shapes=[pltpu.SemaphoreType.DMA((n, 3)), pltpu.SemaphoreType.DMA((n, 3))], name=name,
    )(*lands)


def _chips_begin(name, pairs, deps):
    lands = [lax.empty((3,) + a.shape[1:], a.dtype) for a in pairs]
    return _split_copy_start(name + "_start", pairs, lands, _chip_copies(False), deps)


def _chips_end(name, handle, after):
    send_sems, recv_sems, srcs, lands, _ = handle
    return _split_copy_wait(name + "_wait", send_sems, recv_sems, srcs, lands, _chip_copies(True), after)


def _adamw_layer(name, l, own, lands, w, m, v, prev):
    nl, ng, r, c = w.shape
    tr, tc = _elem_tiles(r, c)
    c1 = 1.0 - ADAM_B1 ** ADAM_STEP
    c2 = 1.0 - ADAM_B2 ** ADAM_STEP

    def body(own_ref, lands_ref, w_ref, m_ref, v_ref, *rest):
        g_ref, d_ref, nm_ref, nv_ref = rest[-4:]
        g = own_ref[...].astype(F32) + lands_ref[0].astype(F32) + lands_ref[1].astype(F32) + lands_ref[2].astype(F32)
        nm = ADAM_B1 * m_ref[...] + (1.0 - ADAM_B1) * g
        nv = ADAM_B2 * v_ref[...] + (1.0 - ADAM_B2) * (g * g)
        m_hat = nm / c1
        v_hat = nv / c2
        g_ref[...] = g
        nm_ref[...] = nm
        nv_ref[...] = nv
        d_ref[...] = -ADAM_LR * (m_hat / (jnp.sqrt(v_hat) + ADAM_EPS) + ADAM_WD * w_ref[...])

    lay = pl.BlockSpec((None, None, tr, tc), lambda g, i, j: (l, g, i, j))
    out = _sds((nl, ng, r, c), F32)
    prev = [] if prev is None else list(prev)
    return pl.pallas_call(
        body, grid=(ng, r // tr, c // tc),
        in_specs=[pl.BlockSpec((None, tr, tc), lambda g, i, j: (g, i, j)),
                  pl.BlockSpec((3, None, tr, tc), lambda g, i, j: (0, g, i, j)), lay, lay, lay] + [ANY] * len(prev),
        out_specs=[lay] * 4, out_shape=[out] * 4, input_output_aliases={5 + i: i for i in range(len(prev))},
        compiler_params=_cp(), name=name,
    )(own, lands, w, m, v, *prev)


def _pad_pairs(a, axis, half, half_pad):
    shp = a.shape
    a = a.reshape(shp[:axis] + (2, half) + shp[axis + 1:])
    pad = [(0, 0)] * a.ndim
    pad[axis + 1] = (0, half_pad - half)
    a = jnp.pad(a, pad)
    return a.reshape(shp[:axis] + (2 * half_pad,) + shp[axis + 1:])


def _unpad_pairs(a, axis, half, half_pad):
    shp = a.shape
    a = a.reshape(shp[:axis] + (2, half_pad) + shp[axis + 1:])
    a = lax.slice_in_dim(a, 0, half, axis=axis + 1)
    return a.reshape(shp[:axis] + (2 * half,) + shp[axis + 1:])


def _blockdiag(w, nt):
    g, a, b = w.shape
    gl = g // nt
    e = jnp.eye(gl, dtype=w.dtype).reshape(1, gl, 1, gl, 1)
    return (w.reshape(nt, gl, a, 1, b) * e).reshape(nt, gl * a, gl * b)


def _diagblocks(m, g, a, b):
    nt = m.shape[0]
    gl = g // nt
    d = jnp.diagonal(m.reshape(nt, gl, a, gl, b), axis1=1, axis2=3)
    return jnp.moveaxis(d, -1, 1).reshape(g, a, b)


_PIECE = 8 * V7X_LANES


def _pack(pieces, row_multiple=512):
    rows = []
    for p in pieces:
        flat = p.reshape(-1).astype(F32)
        rows.append(jnp.pad(flat, (0, -flat.shape[0] % _PIECE)).reshape(-1, V7X_LANES))
    fill = -sum(r.shape[0] for r in rows) % row_multiple
    if fill:
        rows.append(jnp.zeros((fill, V7X_LANES), F32))
    return jnp.concatenate(rows, 0)


def _unpack(packed, shapes):
    lead = packed.shape[:-2]
    out, off = [], 0
    for shp in shapes:
        n = math.prod(shp)
        r = -(-n // _PIECE) * 8
        piece = packed[..., off:off + r, :].reshape(lead + (r * V7X_LANES,))[..., :n]
        out.append(piece.reshape(lead + tuple(shp)))
        off += r
    return out


def _ssm_discretise(lam_re, lam_im, log_dt, b_re, b_im):
    dt = jnp.exp(log_dt)[..., None]
    mag = jnp.exp(lam_re * dt)
    ab_re, ab_im = mag * jnp.cos(lam_im * dt), mag * jnp.sin(lam_im * dt)
    nr, ni = ab_re - 1.0, ab_im
    den = lam_re * lam_re + lam_im * lam_im
    zr = (nr * lam_re + ni * lam_im) / den
    zi = (ni * lam_re - nr * lam_im) / den
    bbr = zr[..., None] * b_re - zi[..., None] * b_im
    bbi = zr[..., None] * b_im + zi[..., None] * b_re
    return ab_re, ab_im, bbr, bbi


def _rope_tables(s):
    half = HEAD_DIM // 2
    inv = ROPE_THETA ** (-jnp.arange(half, dtype=F32) / half)
    ang = jnp.arange(s).astype(F32)[:, None] * inv[None, :]
    cos, sin = jnp.cos(ang), jnp.sin(ang)
    reps = V7X_LANES // HEAD_DIM
    return jnp.tile(jnp.concatenate([cos, cos], -1), (1, reps)), jnp.tile(jnp.concatenate([-sin, sin], -1), (1, reps))


_SMALL = ("attn_sinks", "pool_w", "pool_scale", "ssm_lam_re", "ssm_lam_im", "ssm_log_dt", "ssm_b_re", "ssm_b_im",
          "ssm_c_re", "ssm_c_im", "ssm_d", "ln1_g", "ln1_b", "ffn_conv_b", "ln2_g", "ln2_b")
_BIG = ("w_in", "ssm_glu_w", "w_out", "ffn_w_up", "ffn_w_down")
_RAW = ("attn_sinks", "pool_w", "pool_scale", "ssm_lam_re", "ssm_lam_im", "ssm_b_re", "ssm_b_im", "ssm_c_re", "ssm_c_im",
        "ssm_d", "ln1_g", "ln1_b", "ffn_conv_b", "ln2_g", "ln2_b", "ffn_conv_w")
_ORDER = ("w_in", "attn_sinks", "pool_w", "pool_scale", "ssm_lam_re", "ssm_lam_im", "ssm_log_dt", "ssm_b_re", "ssm_b_im",
          "ssm_c_re", "ssm_c_im", "ssm_d", "ssm_glu_w", "w_out", "ln1_g", "ln1_b", "ffn_w_up", "ffn_conv_w", "ffn_conv_b",
          "ffn_w_down", "ln2_g", "ln2_b")


def kernel(x, w_in, attn_sinks, pool_w, pool_scale, ssm_lam_re, ssm_lam_im, ssm_log_dt, ssm_b_re, ssm_b_im, ssm_c_re, ssm_c_im, ssm_d, ssm_glu_w, w_out, ln1_g, ln1_b, ffn_w_up, ffn_conv_w, ffn_conv_b, ffn_w_down, ln2_g, ln2_b, loss_target, m_w_in, m_attn_sinks, m_pool_w, m_pool_scale, m_ssm_lam_re, m_ssm_lam_im, m_ssm_log_dt, m_ssm_b_re, m_ssm_b_im, m_ssm_c_re, m_ssm_c_im, m_ssm_d, m_ssm_glu_w, m_w_out, m_ln1_g, m_ln1_b, m_ffn_w_up, m_ffn_conv_w, m_ffn_conv_b, m_ffn_w_down, m_ln2_g, m_ln2_b, v_w_in, v_attn_sinks, v_pool_w, v_pool_scale, v_ssm_lam_re, v_ssm_lam_im, v_ssm_log_dt, v_ssm_b_re, v_ssm_b_im, v_ssm_c_re, v_ssm_c_im, v_ssm_d, v_ssm_glu_w, v_w_out, v_ln1_g, v_ln1_b, v_ffn_w_up, v_ffn_conv_w, v_ffn_conv_b, v_ffn_w_down, v_ln2_g, v_ln2_b):
    W = dict(w_in=w_in, attn_sinks=attn_sinks, pool_w=pool_w, pool_scale=pool_scale, ssm_lam_re=ssm_lam_re, ssm_lam_im=ssm_lam_im, ssm_log_dt=ssm_log_dt, ssm_b_re=ssm_b_re, ssm_b_im=ssm_b_im, ssm_c_re=ssm_c_re, ssm_c_im=ssm_c_im, ssm_d=ssm_d, ssm_glu_w=ssm_glu_w, w_out=w_out, ln1_g=ln1_g, ln1_b=ln1_b, ffn_w_up=ffn_w_up, ffn_conv_w=ffn_conv_w, ffn_conv_b=ffn_conv_b, ffn_w_down=ffn_w_down, ln2_g=ln2_g, ln2_b=ln2_b)
    M = dict(w_in=m_w_in, attn_sinks=m_attn_sinks, pool_w=m_pool_w, pool_scale=m_pool_scale, ssm_lam_re=m_ssm_lam_re, ssm_lam_im=m_ssm_lam_im, ssm_log_dt=m_ssm_log_dt, ssm_b_re=m_ssm_b_re, ssm_b_im=m_ssm_b_im, ssm_c_re=m_ssm_c_re, ssm_c_im=m_ssm_c_im, ssm_d=m_ssm_d, ssm_glu_w=m_ssm_glu_w, w_out=m_w_out, ln1_g=m_ln1_g, ln1_b=m_ln1_b, ffn_w_up=m_ffn_w_up, ffn_conv_w=m_ffn_conv_w, ffn_conv_b=m_ffn_conv_b, ffn_w_down=m_ffn_w_down, ln2_g=m_ln2_g, ln2_b=m_ln2_b)
    V = dict(w_in=v_w_in, attn_sinks=v_attn_sinks, pool_w=v_pool_w, pool_scale=v_pool_scale, ssm_lam_re=v_ssm_lam_re, ssm_lam_im=v_ssm_lam_im, ssm_log_dt=v_ssm_log_dt, ssm_b_re=v_ssm_b_re, ssm_b_im=v_ssm_b_im, ssm_c_re=v_ssm_c_re, ssm_c_im=v_ssm_c_im, ssm_d=v_ssm_d, ssm_glu_w=v_ssm_glu_w, w_out=v_w_out, ln1_g=v_ln1_g, ln1_b=v_ln1_b, ffn_w_up=v_ffn_w_up, ffn_conv_w=v_ffn_conv_w, ffn_conv_b=v_ffn_conv_b, ffn_w_down=v_ffn_w_down, ln2_g=v_ln2_g, ln2_b=v_ln2_b)

    depth = w_in.shape[0]
    s, d = x.shape[1], x.shape[2]
    alpha = (2 * depth) ** 0.25
    attn_w = d // 2
    kv_w = attn_w // GQA
    nkv = kv_w // HEAD_DIM
    pool_wd = d // 4
    ssm_wd = d // 4
    n_groups = ssm_wd // SSM_GROUP
    state_w = n_groups * SSM_STATE
    nt_ssm = max(1, state_w // 512)
    o_k, o_v, o_p, o_s = attn_w, attn_w + kv_w, attn_w + 2 * kv_w, attn_w + 2 * kv_w + pool_wd
    in_w = o_s + ssm_wd
    half = ffn_w_down.shape[1]
    half_pad = -(-half // 64) * 64
    ffp = 4 * 2 * half_pad
    xi, yi, ci = _place()
    me = 4 * xi + 2 * yi + ci
    c_idx = jnp.reshape(ci, (1,)).astype(jnp.int32)

    cos_t, sin_t = _rope_tables(s)

    def layer_shards(l):
        return [
            jnp.transpose(w_in[l]).astype(_WIRE), ssm_glu_w[l].astype(_WIRE), w_out[l].astype(_WIRE),
            _pad_pairs(jnp.transpose(ffn_w_up[l]).astype(_WIRE), 0, half, half_pad),
            jnp.pad(ffn_w_down[l].astype(_WIRE), ((0, half_pad - half), (0, 0))),
        ]

    (g_cw,) = _all_gather("gather_conv_w", [_pad_pairs(ffn_conv_w, 2, half, half_pad)])

    def in_weights(l, gathered):
        (g_in,) = gathered
        return dict(
            win_t=g_in.reshape(in_w, d),
            cw=jnp.transpose(g_cw[:, l], (1, 0, 2)).reshape(CONV_WIDTH, 2 * ffp),
            cb=_pad_pairs(ffn_conv_b[l].reshape(N_DEV, 2 * half), 1, half, half_pad).reshape(1, 2 * ffp),
        )

    def out_weights(gathered):
        g_glu, g_out = gathered
        return dict(glu=jnp.transpose(g_glu, (1, 0, 2)).reshape(ssm_wd, 2 * ssm_wd), wout=g_out.reshape(d, d))

    def mixer_weights(l, gathered):
        return {**in_weights(l, gathered[:1]), **out_weights(gathered[1:])}

    def up_weights(gathered):
        (g_up,) = gathered
        return dict(wup_t=g_up.reshape(2 * ffp, d))

    def down_weights(gathered):
        (g_down,) = gathered
        return dict(wdown=g_down.reshape(ffp, d))

    def ffn_weights(gathered):
        return {**up_weights(gathered[:1]), **down_weights(gathered[1:])}

    shards = [layer_shards(l) for l in range(depth)]
    full = [None] * depth
    gathers = {}

    def begin_gather(l, part, deps):
        arrays = shards[l][{"in": slice(0, 1), "out": slice(1, 3), "mixer": slice(0, 3), "up": slice(3, 4),
                            "down": slice(4, 5), "ffn": slice(3, 5)}[part]]
        gathers[l, part] = ("sent", _gather_begin(f"gather_{part}_weights_{l}", arrays, deps))
        return gathers[l, part][1][-1]

    def arrive_gather(l, part, after):
        gathers[l, part] = ("forwarding", _gather_arrived(f"gather_{part}_weights_{l}", gathers[l, part][1], after))
        return gathers[l, part][1][-1]

    def end_gather(l, part, after):
        stage, handle = gathers.pop((l, part))
        return (_gather_done if stage == "forwarding" else _gather_end)(f"gather_{part}_weights_{l}", handle, after)

    issued = begin_gather(0, "in", [g_cw])
    for part in ("out", "up", "down"):
        issued = begin_gather(0, part, [issued])
    full[0] = in_weights(0, end_gather(0, "in", g_cw))

    ssm_params = (ssm_lam_re, ssm_lam_im, ssm_log_dt, ssm_b_re, ssm_b_im)
    ab_re_all, ab_im_all, bbr_all, bbi_all = _ssm_discretise(*ssm_params)

    def ssm_maps(w):
        return jax.vmap(lambda t: _blockdiag(jnp.transpose(t, (0, 2, 1)), nt_ssm))(w).astype(_MXU)

    bdr_all, bdi_all, cdr_all, cdi_all = ssm_maps(bbr_all), ssm_maps(bbi_all), ssm_maps(ssm_c_re), ssm_maps(ssm_c_im)

    saved = []
    xf = x[0]
    xb = xf.astype(_MXU)
    for l in range(depth):
        fw = full[l]
        deps = [arrive_gather(l, "ffn", xb)] if l >= 2 else []
        if l + 1 < depth:
            issued = begin_gather(l + 1, "ffn", [begin_gather(l + 1, "mixer", [fw["win_t"], issued])])
            deps.append(issued)
        h = _mm_nt(f"in_proj_{l}", xb, fw["win_t"], deps=tuple(deps))
        q_rot, k_rot = _rope(f"rope_{l}", h, o_v, cos_t, sin_t, _MXU, ((0, o_k), (o_k, o_v)))
        k_hm = jnp.transpose(k_rot.reshape(s, nkv, HEAD_DIM), (1, 0, 2))
        v_hm = jnp.transpose(h[:, o_v:o_p].astype(_MXU).reshape(s, nkv, HEAD_DIM), (1, 0, 2))
        sinks = attn_sinks[l].reshape(nkv, GQA)
        o_attn, lse = _attn_fwd(f"attn_{l}", q_rot, k_hm, v_hm, sinks)
        pw_b = pool_w[l].astype(_MXU)
        psc = pool_scale[l].reshape(1, pool_wd)
        y_pool, pre = _pool_fwd(f"pool_{l}", h, o_p // pool_wd, pw_b, psc)
        bdr, bdi, cdr, cdi = bdr_all[l], bdi_all[l], cdr_all[l], cdi_all[l]
        dvec = ssm_d[l].reshape(1, ssm_wd)
        ar, ai = ab_re_all[l].reshape(1, state_w), ab_im_all[l].reshape(1, state_w)
        cw_ssm = ssm_wd // nt_ssm
        sr, si, ypre, yg = _ssm_fwd(f"ssm_{l}", h, o_s // cw_ssm, bdr, bdi, cdr, cdi, dvec, ar, ai)
        if l == 0:
            fw.update(out_weights(end_gather(0, "out", yg)))
        y_ssm, ab2 = _glu_fwd(f"glu_{l}", yg, fw["glu"])
        mix = jnp.concatenate([o_attn.astype(_MXU), y_pool, y_ssm], -1)
        a1 = _mm_nn(f"out_proj_{l}", mix, fw["wout"])
        g1, b1 = ln1_g[l].reshape(1, d), ln1_b[l].reshape(1, d)
        x1, x1b, xh1, rs1 = _ln_fwd(f"ln1_{l}", xf, a1, g1, b1, alpha)
        fw.update(up_weights(end_gather(0, "up", x1b)) if l == 0 else ffn_weights(end_gather(l, "ffn", x1b)))
        deps = (arrive_gather(l + 1, "mixer", x1b),) if 1 <= l < depth - 1 else ()
        hu = _mm_nt(f"ffn_up_{l}", x1b, fw["wup_t"], cap=2 * half_pad, deps=deps)
        act = _conv_act_fwd(f"ffn_act_{l}", hu, fw["cw"], fw["cb"])
        if l == 0:
            fw.update(down_weights(end_gather(0, "down", act)))
        f_out = _mm_nn(f"ffn_down_{l}", act, fw["wdown"], cap=256)
        g2, b2 = ln2_g[l].reshape(1, d), ln2_b[l].reshape(1, d)
        x2, x2b, xh2, rs2 = _ln_fwd(f"ln2_{l}", x1, f_out, g2, b2, alpha)
        saved.append(dict(xb=xb, h=h, q_rot=q_rot, k_hm=k_hm, v_hm=v_hm, sinks=sinks, o_attn=o_attn, lse=lse, pw_b=pw_b, psc=psc,
                          pre=pre, bdr=bdr, bdi=bdi, cdr=cdr, cdi=cdi, dvec=dvec, ar=ar, ai=ai, sr=sr, si=si, ypre=ypre, yg=yg,
                          ab2=ab2, mix=mix, g1=g1, xh1=xh1, rs1=rs1, x1b=x1b, hu=hu, act=act, g2=g2, xh2=xh2, rs2=rs2))
        xf, xb = x2, x2b
        if l + 1 < depth:
            full[l + 1] = mixer_weights(l + 1, end_gather(l + 1, "mixer", x2b))

    dy, loss_part = _loss_head("loss_head", xf, loss_target[0])
    loss = lax.psum(loss_part[0, 0], ("x", "y", "c"))

    small_handles = [None] * depth
    small_parts = [None] * depth
    outs = {}
    big_res = {k: None for k in _BIG}
    my_chip = 2 * xi + yi
    pending = []

    transposed = ("w_in", "ffn_w_up")

    def row_groups(name_, t):
        g = 2 if name_ == "ffn_w_up" else 1
        return t.reshape(t.shape[:-2] + (g, t.shape[-2] // g, t.shape[-1]))

    def as_groups(name_, t):
        return row_groups(name_, jnp.transpose(t, (0, 2, 1)) if name_ in transposed else t)

    def from_groups(name_, t):
        t = t.reshape(t.shape[0], t.shape[1] * t.shape[2], t.shape[3])
        return jnp.transpose(t, (0, 2, 1)) if name_ in transposed else t

    grouped = {name_: tuple(as_groups(name_, t[name_]) for t in (W, M, V)) for name_ in _BIG}

    def finish_exchanges(after):
        while pending:
            lay, part, names, handle = pending.pop(0)
            pairs, lands = _chips_end(f"grads_between_chips_{part}_{lay}", handle, after)
            for name_, p, ld in zip(names, pairs, lands):
                own = row_groups(name_, lax.dynamic_index_in_dim(p, my_chip, 0, keepdims=False))
                big_res[name_] = _adamw_layer(f"adamw_{name_}_{lay}", lay, own, row_groups(name_, ld), *grouped[name_],
                                              big_res[name_])

    def begin_swap(lay, part, names, grads):
        by_owner = [a.reshape((4, 2) + a.shape[1:]) for a in grads]
        return lay, part, names, _sibling_begin(f"grads_to_sibling_{part}_{lay}", by_owner, [])

    def begin_exchange(swap, after):
        lay, part, names, handle = swap
        by_owner, theirs = _sibling_end(f"grads_to_sibling_{part}_{lay}", handle, after)
        pair = [_pair_sum(f"pair_sum_{name_}_{lay}", a, b, c_idx) for name_, a, b in zip(names, by_owner, theirs)]
        finish_exchanges(after)
        handle = _chips_begin(f"grads_between_chips_{part}_{lay}", pair, [])
        pending.append((lay, part, names, handle))
        return handle[-1]

    token = None
    small_handle = None
    for l in reversed(range(depth)):
        fw, sv = full[l], saved[l]
        deps = () if token is None else (token, small_handles[l + 1][-1])
        dr2, dr2b, dg2, db2 = _ln_bwd(f"ln2_bwd_{l}", dy, sv["xh2"], sv["rs2"], sv["g2"], deps=deps)
        d_wdown = _mm_tn_acols(f"ffn_down_dw_{l}", sv["act"], dr2b, _WIRE, cap=2 * half_pad)
        dact = _mm_nt(f"ffn_down_dx_{l}", dr2b, fw["wdown"], cap=2 * half_pad)
        dhu, dcw, dcb = _conv_act_bwd(f"ffn_act_bwd_{l}", dact, sv["hu"], fw["cw"], fw["cb"])
        d_wup = _mm(f"ffn_up_dw_{l}", dhu, sv["x1b"], TN, (N_DEV, 1),
                    pl.BlockSpec((None, s, 2 * half_pad), lambda j, kk: (j // 4, 0, j % 4)),
                    _resident((s, d), lambda j, kk: (0, 0)),
                    pl.BlockSpec((2 * half_pad, d), lambda j, kk: (j, 0)), (2 * ffp, d), _WIRE)
        swap = begin_swap(l, "ffn", ("ffn_w_up", "ffn_w_down"),
                          [d_wup.reshape(N_DEV, 2 * half_pad, d), d_wdown.reshape(N_DEV, half_pad, d)])
        dy1 = _mm_split_k(f"ffn_up_dx_{l}", dhu, fw["wup_t"], dr2, alpha, deps=(swap[3][-1],))
        token = begin_exchange(swap, dy1)
        dr1, dr1b, dg1, db1 = _ln_bwd(f"ln1_bwd_{l}", dy1, sv["xh1"], sv["rs1"], sv["g1"], deps=(token,))
        d_wout = _mm_tn_acols(f"out_proj_dw_{l}", sv["mix"], dr1b, _WIRE, cap=d // N_DEV)
        dmix = _mm_nt(f"out_proj_dx_{l}", dr1b, fw["wout"])
        dq_rot, dk_hm, dv_hm, dsk = _attn_bwd(f"attn_bwd_{l}", sv["q_rot"], sv["k_hm"], sv["v_hm"], sv["o_attn"], dmix,
                                             sv["lse"], sv["sinks"])
        dqk = jnp.concatenate([dq_rot, jnp.transpose(dk_hm, (1, 0, 2)).reshape(s, kv_w)], -1)
        dhq, dhk = _rope(f"rope_bwd_{l}", dqk, o_v, cos_t, -sin_t, _MXU, ((0, o_k), (o_k, o_v)))
        dhv = jnp.transpose(dv_hm, (1, 0, 2)).reshape(s, kv_w).astype(_MXU)
        dhp, dpw, dpsc = _pool_bwd(f"pool_bwd_{l}", dmix, attn_w // pool_wd, sv["pre"], sv["pw_b"], sv["psc"])
        dab2, dyg = _glu_bwd(f"glu_bwd_{l}", dmix, (attn_w + pool_wd) // ssm_wd, sv["ab2"], fw["glu"])
        d_glu = _mm_tn_bcols(f"glu_dw_{l}", sv["yg"], dab2, _WIRE)
        cw_ssm = ssm_wd // nt_ssm
        dhs, dd, dcdr, dcdi, dbdr, dbdi, dar, dai = _ssm_bwd(
            f"ssm_bwd_{l}", dyg, sv["ypre"], sv["h"], o_s // cw_ssm, sv["sr"], sv["si"], sv["bdr"], sv["bdi"], sv["cdr"],
            sv["cdi"], sv["dvec"], sv["ar"], sv["ai"])
        dh = jnp.concatenate([dhq, dhk, dhv, dhp, dhs], -1)
        d_win = _mm_tn_acols(f"in_proj_dw_{l}", dh, sv["xb"], _WIRE)
        swap = begin_swap(l, "mixer", ("w_in", "ssm_glu_w", "w_out"),
                          [d_win.reshape(N_DEV, in_w // N_DEV, d),
                           jnp.transpose(d_glu.reshape(ssm_wd, N_DEV, 2 * ssm_wd // N_DEV), (1, 0, 2)),
                           d_wout.reshape(N_DEV, d // N_DEV, d)])
        dy = _mm_nn(f"in_proj_dx_{l}", dh, fw["win_t"], add=dr1, add_scale=alpha, deps=(swap[3][-1],))

        raw = dict(attn_sinks=dsk, pool_w=dpw, pool_scale=dpsc, ssm_lam_re=dar, ssm_lam_im=dai,
                   ssm_b_re=_diagblocks(dbdr, n_groups, SSM_GROUP, SSM_STATE),
                   ssm_b_im=_diagblocks(dbdi, n_groups, SSM_GROUP, SSM_STATE),
                   ssm_c_re=_diagblocks(dcdr, n_groups, SSM_STATE, SSM_GROUP),
                   ssm_c_im=_diagblocks(dcdi, n_groups, SSM_STATE, SSM_GROUP), ssm_d=dd, ln1_g=dg1, ln1_b=db1,
                   ffn_conv_b=_unpad_pairs(dcb.reshape(N_DEV, 2 * half_pad), 1, half, half_pad), ln2_g=dg2, ln2_b=db2,
                   ffn_conv_w=_unpad_pairs(dcw.reshape(CONV_WIDTH, N_DEV, 2 * half_pad), 2, half, half_pad))
        raw_shapes = {k: raw[k].shape for k in _RAW}
        small_handles[l] = _gather_begin(f"gather_small_grads_{l}", [_pack([raw[k] for k in _RAW])], [dy])
        token = begin_exchange(swap, small_handles[l][-1])
        if l + 1 < depth:
            small_parts[l + 1] = _sum_parts(f"sum_small_grads_{l + 1}",
                                            _gather_end(f"gather_small_grads_{l + 1}", small_handles[l + 1], dy)[0])

    small_parts[0] = _sum_parts("sum_small_grads_0", _gather_end("gather_small_grads_0", small_handles[0], token)[0])
    summed = jnp.stack(small_parts)
    g_small = dict(zip(_RAW, _unpack(summed, [raw_shapes[k] for k in _RAW])))
    swap_last = lambda t: jnp.transpose(t, (0, 1, 3, 2))
    _, vjp = jax.vjp(_ssm_discretise, *ssm_params)
    dlr, dli, dldt, dbr, dbi = vjp((g_small["ssm_lam_re"].reshape(depth, n_groups, SSM_STATE),
                                    g_small["ssm_lam_im"].reshape(depth, n_groups, SSM_STATE),
                                    swap_last(g_small["ssm_b_re"]), swap_last(g_small["ssm_b_im"])))
    g_small.update(ssm_lam_re=dlr, ssm_lam_im=dli, ssm_log_dt=dldt, ssm_b_re=dbr, ssm_b_im=dbi,
                   ssm_c_re=swap_last(g_small["ssm_c_re"]), ssm_c_im=swap_last(g_small["ssm_c_im"]),
                   ffn_conv_w=lax.dynamic_index_in_dim(g_small["ffn_conv_w"], me, axis=2, keepdims=False))
    for k in _SMALL + ("ffn_conv_w",):
        shp = W[k].shape
        two_d = (math.prod(shp[:-1]), shp[-1])
        res = _adamw_2d(f"adamw_{k}", g_small[k].reshape(two_d), *(t[k].reshape(two_d) for t in (W, M, V)))
        outs[k] = (g_small[k].reshape(shp),) + tuple(a.reshape(shp) for a in res)

    finish_exchanges(outs["ln2_b"][1])
    for name_ in _BIG:
        outs[name_] = tuple(from_groups(name_, t) for t in big_res[name_])

    grad_x = dy[None]
    result = [loss, grad_x]
    for i in range(4):
        result += [outs[k][i] for k in _ORDER]
    return tuple(result)
```

```python
import math

import jax
import jax.numpy as jnp
from jax import lax
from jax.experimental import pallas as pl
from jax.experimental.pallas import tpu as pltpu

F32 = jnp.float32
_MXU = jnp.bfloat16
_WIRE = jnp.bfloat16

HEAD_DIM = 64
GQA = 4
ATTN_BLOCK = 128
ROPE_THETA = 10000.0
POOL_WINDOWS = (2, 4, 8, 16)
SSM_GROUP = 16
SSM_STATE = 64
CONV_WIDTH = 3
LN_EPS = 1e-5
ADAM_LR, ADAM_B1, ADAM_B2, ADAM_EPS, ADAM_WD, ADAM_STEP = 0.001, 0.9, 0.999, 1e-08, 0.01, 10

N_DEV = 8
V7X_LANES = 128
V7X_VMEM_LIMIT = 56 * 1024 * 1024
SCAN_T = 64
SCAN_LANES = 256
MESH = pl.DeviceIdType.MESH
ANY = pl.BlockSpec(memory_space=pl.ANY)


def _cp():
    return pltpu.CompilerParams(vmem_limit_bytes=V7X_VMEM_LIMIT)


def _resident(block, index_map):
    return pl.BlockSpec(block, index_map, pipeline_mode=pl.Buffered(1))


def _sds(shape, dtype):
    return jax.ShapeDtypeStruct(tuple(shape), dtype)


def _mm(name, a, b, dims, grid, a_spec, b_spec, o_spec, out_shape, out_dtype, add=None, add_spec=None, add_scale=1.0, deps=()):
    nk = grid[1]
    n_in = 2 + (add is not None) + len(deps)
    oblk = tuple(d for d in o_spec.block_shape if d is not None)
    scratch = nk > 1 and out_dtype != F32

    def body(*refs):
        a_ref, b_ref = refs[:2]
        add_ref = None if add is None else refs[2]
        o_ref = refs[n_in]
        acc_ref = refs[-1] if scratch else None

        def finish(r):
            if add_ref is not None:
                r = r + add_scale * add_ref[...]
            o_ref[...] = r.astype(o_ref.dtype)

        part = lax.dot_general(a_ref[...], b_ref[...], (dims, ((), ())), preferred_element_type=F32)
        if nk == 1:
            finish(part)
        elif not scratch:
            k = pl.program_id(1)

            @pl.when(k == 0)
            def _():
                o_ref[...] = part

            @pl.when(k > 0)
            def _():
                o_ref[...] += part

            if add_ref is not None:
                @pl.when(k == nk - 1)
                def _():
                    o_ref[...] += add_scale * add_ref[...]
        else:
            k = pl.program_id(1)

            @pl.when(k == 0)
            def _():
                acc_ref[...] = part

            @pl.when(k > 0)
            def _():
                acc_ref[...] += part

            @pl.when(k == nk - 1)
            def _():
                finish(acc_ref[...])

    ins = [a, b] + ([] if add is None else [add]) + list(deps)
    in_specs = [a_spec, b_spec] + ([] if add is None else [add_spec]) + [ANY] * len(deps)
    return pl.pallas_call(
        body, grid=grid, in_specs=in_specs, out_specs=o_spec, out_shape=_sds(out_shape, out_dtype),
        scratch_shapes=[pltpu.VMEM(oblk, F32)] if scratch else [], compiler_params=_cp(), name=name,
    )(*ins)


NN = ((1,), (0,))
NT = ((1,), (1,))
TN = ((0,), (0,))


def _mm_split_k(name, a2, b, add, add_scale, deps=()):
    _, m, f = a2.shape
    n = b.shape[1]
    tm, tn = m // 2, _col_tile(n, 256)

    def body(a_ref, b_ref, add_ref, *rest):
        o_ref = rest[-1]
        o_ref[...] = (jnp.dot(a_ref[0], b_ref[:f, :], preferred_element_type=F32)
                      + jnp.dot(a_ref[1], b_ref[f:, :], preferred_element_type=F32) + add_scale * add_ref[...])

    tile = pl.BlockSpec((tm, tn), lambda i, j: (i, j))
    return pl.pallas_call(
        body, grid=(m // tm, n // tn),
        in_specs=[_resident((2, tm, f), lambda i, j: (0, i, 0)), pl.BlockSpec((2 * f, tn), lambda i, j: (0, j)), tile]
        + [ANY] * len(deps),
        out_specs=tile, out_shape=_sds((m, n), F32), compiler_params=_cp(), name=name,
    )(a2, b, add, *deps)


def _col_tile(n, cap=512):
    if n % V7X_LANES:
        return n
    t = min(cap, n)
    t -= t % V7X_LANES
    while n % t:
        t -= V7X_LANES
    return t


def _mm_nn(name, a, b, out_dtype=F32, cap=512, add=None, add_scale=1.0, deps=()):
    m, k = a.shape
    n = b.shape[1]
    tn = _col_tile(n, cap)
    o_spec = pl.BlockSpec((m, tn), lambda j, kk: (0, j))
    return _mm(name, a, b, NN, (n // tn, 1), _resident((m, k), lambda j, kk: (0, 0)),
               pl.BlockSpec((k, tn), lambda j, kk: (0, j)), o_spec, (m, n), out_dtype,
               add=add, add_spec=None if add is None else o_spec, add_scale=add_scale, deps=deps)


def _mm_nt(name, a, b, out_dtype=F32, add=None, add_scale=1.0, cap=512, deps=()):
    m, k = a.shape
    n = b.shape[0]
    tn = _col_tile(n, cap)
    o_spec = pl.BlockSpec((m, tn), lambda j, kk: (0, j))
    return _mm(name, a, b, NT, (n // tn, 1), _resident((m, k), lambda j, kk: (0, 0)),
               pl.BlockSpec((tn, k), lambda j, kk: (j, 0)), o_spec, (m, n), out_dtype,
               add=add, add_spec=None if add is None else o_spec, add_scale=add_scale, deps=deps)


def _mm_tn_bcols(name, a, b, out_dtype, cap=512):
    s, k = a.shape
    n = b.shape[1]
    tn = _col_tile(n, cap)
    return _mm(name, a, b, TN, (n // tn, 1), _resident((s, k), lambda j, kk: (0, 0)),
               pl.BlockSpec((s, tn), lambda j, kk: (0, j)), pl.BlockSpec((k, tn), lambda j, kk: (0, j)), (k, n), out_dtype)


def _mm_tn_acols(name, a, b, out_dtype, cap=512):
    s, k = a.shape
    n = b.shape[1]
    tk = _col_tile(k, cap)
    return _mm(name, a, b, TN, (k // tk, 1), pl.BlockSpec((s, tk), lambda i, kk: (0, i)),
               _resident((s, n), lambda i, kk: (0, 0)), pl.BlockSpec((tk, n), lambda i, kk: (i, 0)), (k, n), out_dtype)


def _ln_fwd(name, x, a, g, b, alpha):
    s, d = x.shape
    tr = min(256, s)

    def body(x_ref, a_ref, g_ref, b_ref, y_ref, yb_ref, xh_ref, rs_ref):
        r = alpha * x_ref[...] + a_ref[...]
        mu = jnp.mean(r, -1, keepdims=True)
        c = r - mu
        var = jnp.mean(c * c, -1, keepdims=True)
        rstd = lax.rsqrt(var + LN_EPS)
        xh = c * rstd
        y = xh * g_ref[...] + b_ref[...]
        y_ref[...] = y
        yb_ref[...] = y.astype(_MXU)
        xh_ref[...] = xh
        rs_ref[...] = rstd

    row = pl.BlockSpec((tr, d), lambda i: (i, 0))
    vec = pl.BlockSpec((1, d), lambda i: (0, 0))
    return pl.pallas_call(
        body, grid=(s // tr,), in_specs=[row, row, vec, vec],
        out_specs=[row, row, row, pl.BlockSpec((tr, 1), lambda i: (i, 0))],
        out_shape=[_sds((s, d), F32), _sds((s, d), _MXU), _sds((s, d), F32), _sds((s, 1), F32)],
        compiler_params=_cp(), name=name,
    )(x, a, g, b)


def _ln_bwd(name, dy, xh, rstd, g, deps=()):
    s, d = dy.shape
    tr = min(256, s)
    nd = len(deps)

    def body(dy_ref, xh_ref, rs_ref, g_ref, *rest):
        dr_ref, drb_ref, dg_ref, db_ref = rest[nd:]
        i = pl.program_id(0)
        dy_ = dy_ref[...]
        xh_ = xh_ref[...]
        dxh = dy_ * g_ref[...]
        m1 = jnp.mean(dxh, -1, keepdims=True)
        m2 = jnp.mean(dxh * xh_, -1, keepdims=True)
        dr = rs_ref[...] * (dxh - m1 - xh_ * m2)
        dr_ref[...] = dr
        drb_ref[...] = dr.astype(_MXU)
        pg = jnp.sum(dy_ * xh_, 0, keepdims=True)
        pb = jnp.sum(dy_, 0, keepdims=True)

        @pl.when(i == 0)
        def _():
            dg_ref[...] = pg
            db_ref[...] = pb

        @pl.when(i > 0)
        def _():
            dg_ref[...] += pg
            db_ref[...] += pb

    row = pl.BlockSpec((tr, d), lambda i: (i, 0))
    vec = pl.BlockSpec((1, d), lambda i: (0, 0))
    return pl.pallas_call(
        body, grid=(s // tr,), in_specs=[row, row, pl.BlockSpec((tr, 1), lambda i: (i, 0)), vec] + [ANY] * nd,
        out_specs=[row, row, vec, vec],
        out_shape=[_sds((s, d), F32), _sds((s, d), _MXU), _sds((1, d), F32), _sds((1, d), F32)],
        compiler_params=_cp(), name=name,
    )(dy, xh, rstd, g, *deps)


def _loss_head(name, y, target):
    s, d = y.shape
    tr = min(256, s)

    def body(y_ref, t_ref, dy_ref, l_ref):
        i = pl.program_id(0)
        e = y_ref[...] - t_ref[...]
        dy_ref[...] = e * (1.0 / d)
        part = 0.5 * jnp.sum(jnp.mean(e * e, -1, keepdims=True), 0, keepdims=True)

        @pl.when(i == 0)
        def _():
            l_ref[...] = part

        @pl.when(i > 0)
        def _():
            l_ref[...] += part

    row = pl.BlockSpec((tr, d), lambda i: (i, 0))
    return pl.pallas_call(
        body, grid=(s // tr,), in_specs=[row, row], out_specs=[row, pl.BlockSpec((1, 1), lambda i: (0, 0))],
        out_shape=[_sds((s, d), F32), _sds((1, 1), F32)], compiler_params=_cp(), name=name,
    )(y, target)


def _rope(name, t, width, cos, sin, out_dtype, splits):
    s = t.shape[0]
    tr = min(256, s)
    assert width % V7X_LANES == 0

    def body(t_ref, c_ref, s_ref, *o_refs):
        lane = lax.broadcasted_iota(jnp.int32, (tr, V7X_LANES), 1)
        first = (lane % HEAD_DIM) < (HEAD_DIM // 2)
        cs, sn = c_ref[...], s_ref[...]
        for (lo, hi), o_ref in zip(splits, o_refs):
            for c0 in range(lo, hi, V7X_LANES):
                v = t_ref[:, c0:c0 + V7X_LANES].astype(F32)
                partner = jnp.where(first, pltpu.roll(v, V7X_LANES - HEAD_DIM // 2, 1), pltpu.roll(v, HEAD_DIM // 2, 1))
                o_ref[:, c0 - lo:c0 - lo + V7X_LANES] = (v * cs + partner * sn).astype(o_ref.dtype)

    tab = pl.BlockSpec((tr, V7X_LANES), lambda i: (i, 0))
    return pl.pallas_call(
        body, grid=(s // tr,), in_specs=[pl.BlockSpec((tr, width), lambda i: (i, 0)), tab, tab],
        out_specs=[pl.BlockSpec((tr, hi - lo), lambda i: (i, 0)) for lo, hi in splits],
        out_shape=[_sds((s, hi - lo), out_dtype) for lo, hi in splits], compiler_params=_cp(), name=name,
    )(t, cos, sin)


def _attn_masks():
    i = lax.broadcasted_iota(jnp.int32, (GQA * ATTN_BLOCK, 2 * ATTN_BLOCK), 0) % ATTN_BLOCK
    j = lax.broadcasted_iota(jnp.int32, (GQA * ATTN_BLOCK, 2 * ATTN_BLOCK), 1)
    cur_ok = jnp.logical_and(j >= ATTN_BLOCK, j - ATTN_BLOCK <= i)
    prev_ok = jnp.logical_and(j < ATTN_BLOCK, j > i)
    return cur_ok, prev_ok


def _attn_scores(q4, kcat, n, cur_ok, prev_ok):
    sc = lax.dot_general(q4, kcat, (NT, ((), ())), preferred_element_type=F32) * (HEAD_DIM ** -0.5)
    return jnp.where(jnp.logical_or(cur_ok, jnp.logical_and(prev_ok, n > 0)), sc, -1e30)


def _stack_heads(ref, rows):
    return jnp.concatenate([ref[rows, g * HEAD_DIM:(g + 1) * HEAD_DIM] for g in range(GQA)], 0)


def _per_head_column(values):
    r = lax.broadcasted_iota(jnp.int32, (GQA * ATTN_BLOCK, 1), 0) // ATTN_BLOCK
    col = jnp.zeros((GQA * ATTN_BLOCK, 1), F32)
    for g, val in enumerate(values):
        col = jnp.where(r == g, val, col)
    return col


def _attn_fwd(name, q, k, v, sinks):
    s = q.shape[0]
    nkv = k.shape[0]
    gw = GQA * HEAD_DIM
    nb = s // ATTN_BLOCK

    def body(sk_ref, q_ref, k_ref, v_ref, o_ref, lse_ref):
        h = pl.program_id(0)
        cur_ok, prev_ok = _attn_masks()
        sink = _per_head_column([sk_ref[h, g] for g in range(GQA)])

        def blk(n, carry):
            rows = pl.ds(pl.multiple_of(n * ATTN_BLOCK, ATTN_BLOCK), ATTN_BLOCK)
            prows = pl.ds(pl.multiple_of(jnp.maximum(n - 1, 0) * ATTN_BLOCK, ATTN_BLOCK), ATTN_BLOCK)
            kcat = jnp.concatenate([k_ref[prows, :], k_ref[rows, :]], 0)
            vcat = jnp.concatenate([v_ref[prows, :], v_ref[rows, :]], 0)
            sc = _attn_scores(_stack_heads(q_ref, rows), kcat, n, cur_ok, prev_ok)
            m = jnp.maximum(sc.max(-1, keepdims=True), sink)
            p = jnp.exp(sc - m)
            den = p.sum(-1, keepdims=True) + jnp.exp(sink - m)
            o = jnp.dot((p / den).astype(_MXU), vcat, preferred_element_type=F32)
            lse = m + jnp.log(den)
            for g in range(GQA):
                mine = slice(g * ATTN_BLOCK, (g + 1) * ATTN_BLOCK)
                o_ref[rows, g * HEAD_DIM:(g + 1) * HEAD_DIM] = o[mine, :]
                lse_ref[rows, g:g + 1] = lse[mine, :]
            return carry

        lax.fori_loop(0, nb, blk, 0)

    kv_spec = pl.BlockSpec((None, s, HEAD_DIM), lambda h: (h, 0, 0))
    return pl.pallas_call(
        body, grid=(nkv,),
        in_specs=[pl.BlockSpec(memory_space=pltpu.SMEM), pl.BlockSpec((s, gw), lambda h: (0, h)), kv_spec, kv_spec],
        out_specs=[pl.BlockSpec((s, gw), lambda h: (0, h)), pl.BlockSpec((None, s, GQA), lambda h: (h, 0, 0))],
        out_shape=[_sds((s, nkv * gw), F32), _sds((nkv, s, GQA), F32)], compiler_params=_cp(), name=name,
    )(sinks, q, k, v)


def _attn_bwd(name, q, k, v, o, dmix, lse, sinks):
    s = q.shape[0]
    nkv = k.shape[0]
    gw = GQA * HEAD_DIM
    nb = s // ATTN_BLOCK
    scale = HEAD_DIM ** -0.5

    def body(sk_ref, q_ref, k_ref, v_ref, o_ref, do_ref, lse_ref, dq_ref, dk_ref, dv_ref, dsk_ref):
        h = pl.program_id(0)
        cur_ok, prev_ok = _attn_masks()
        dk_ref[...] = jnp.zeros_like(dk_ref)
        dv_ref[...] = jnp.zeros_like(dv_ref)

        sink = _per_head_column([sk_ref[h, g] for g in range(GQA)])

        def blk(n, acc):
            rows = pl.ds(pl.multiple_of(n * ATTN_BLOCK, ATTN_BLOCK), ATTN_BLOCK)
            prows = pl.ds(pl.multiple_of(jnp.maximum(n - 1, 0) * ATTN_BLOCK, ATTN_BLOCK), ATTN_BLOCK)
            kcat = jnp.concatenate([k_ref[prows, :], k_ref[rows, :]], 0)
            vcat = jnp.concatenate([v_ref[prows, :], v_ref[rows, :]], 0)
            q4 = _stack_heads(q_ref, rows)
            do4 = _stack_heads(do_ref, rows)
            delta = jnp.sum(do4 * _stack_heads(o_ref, rows), -1, keepdims=True)
            dob = do4.astype(_MXU)
            lse = jnp.concatenate([lse_ref[rows, g:g + 1] for g in range(GQA)], 0)
            p = jnp.exp(_attn_scores(q4, kcat, n, cur_ok, prev_ok) - lse)
            dp = lax.dot_general(dob, vcat, (NT, ((), ())), preferred_element_type=F32)
            ds = (p * (dp - delta) * scale).astype(_MXU)
            dq = jnp.dot(ds, kcat, preferred_element_type=F32)
            for g in range(GQA):
                dq_ref[rows, g * HEAD_DIM:(g + 1) * HEAD_DIM] = dq[g * ATTN_BLOCK:(g + 1) * ATTN_BLOCK, :]
            dk = jnp.dot(q4.T, ds, preferred_element_type=F32).T
            dv = jnp.dot(dob.T, p.astype(_MXU), preferred_element_type=F32).T
            dk_ref[prows, :] += dk[:ATTN_BLOCK, :]
            dv_ref[prows, :] += dv[:ATTN_BLOCK, :]
            dk_ref[rows, :] += dk[ATTN_BLOCK:, :]
            dv_ref[rows, :] += dv[ATTN_BLOCK:, :]
            return acc - jnp.exp(sink - lse) * delta

        acc = lax.fori_loop(0, nb, blk, jnp.zeros((GQA * ATTN_BLOCK, 1), F32))
        for g in range(GQA):
            dsk_ref[:, g:g + 1] = jnp.sum(acc[g * ATTN_BLOCK:(g + 1) * ATTN_BLOCK, :], 0, keepdims=True)

    kv_spec = pl.BlockSpec((None, s, HEAD_DIM), lambda h: (h, 0, 0))
    qcols = pl.BlockSpec((s, gw), lambda h: (0, h))
    return pl.pallas_call(
        body, grid=(nkv,),
        in_specs=[pl.BlockSpec(memory_space=pltpu.SMEM), qcols, kv_spec, kv_spec, qcols, qcols,
                  pl.BlockSpec((None, s, GQA), lambda h: (h, 0, 0))],
        out_specs=[qcols, kv_spec, kv_spec, pl.BlockSpec((None, 1, GQA), lambda h: (h, 0, 0))],
        out_shape=[_sds((s, nkv * gw), F32), _sds((nkv, s, HEAD_DIM), F32), _sds((nkv, s, HEAD_DIM), F32),
                   _sds((nkv, 1, GQA), F32)],
        compiler_params=_cp(), name=name,
    )(sinks, q, k, v, o, dmix, lse)


def _shift_down(a, k, t):
    return jnp.where(t >= k, pltpu.roll(a, k, 0), 0.0)


def _shift_up(a, k, t):
    n = a.shape[0]
    return jnp.where(t < n - k, pltpu.roll(a, n - k, 0), 0.0)


def _pool_fwd(name, h, col_block, pool_w, pool_scale):
    s = h.shape[0]
    ng, pg = pool_w.shape[0], pool_w.shape[1]
    pw_ = ng * pg

    def body(u_ref, w_ref, sc_ref, y_ref, pre_ref):
        t = lax.broadcasted_iota(jnp.int32, (s, pg), 0)
        for gi, win in enumerate(POOL_WINDOWS):
            cols = slice(gi * pg, (gi + 1) * pg)
            u = u_ref[:, cols]
            a = u
            k = 1
            while k < win:
                a = a + _shift_down(a, k, t)
                k *= 2
            div = jnp.minimum(t + 1, win).astype(F32)
            pre = (a / div - u).astype(_MXU)
            pre_ref[:, cols] = pre
            out = jnp.dot(pre, w_ref[gi], preferred_element_type=F32)
            y_ref[:, cols] = (out * sc_ref[:, cols]).astype(y_ref.dtype)

    blk = pl.BlockSpec((s, pw_), lambda i: (0, 0))
    return pl.pallas_call(
        body, grid=(1,),
        in_specs=[pl.BlockSpec((s, pw_), lambda i: (0, col_block)), pl.BlockSpec((ng, pg, pg), lambda i: (0, 0, 0)),
                  pl.BlockSpec((1, pw_), lambda i: (0, 0))],
        out_specs=[blk, blk], out_shape=[_sds((s, pw_), _MXU), _sds((s, pw_), _MXU)], compiler_params=_cp(), name=name,
    )(h, pool_w, pool_scale)


def _pool_bwd(name, dmix, col_block, pre, pool_w, pool_scale):
    s = pre.shape[0]
    ng, pg = pool_w.shape[0], pool_w.shape[1]
    pw_ = ng * pg

    def body(dy_ref, pre_ref, w_ref, sc_ref, du_ref, dw_ref, dsc_ref):
        t = lax.broadcasted_iota(jnp.int32, (s, pg), 0)
        for gi, win in enumerate(POOL_WINDOWS):
            cols = slice(gi * pg, (gi + 1) * pg)
            pre_g = pre_ref[:, cols]
            dy = dy_ref[:, cols]
            out = jnp.dot(pre_g, w_ref[gi], preferred_element_type=F32)
            dsc_ref[:, cols] = jnp.sum(dy * out, 0, keepdims=True)
            dout = (dy * sc_ref[:, cols]).astype(_MXU)
            dw_ref[gi] = lax.dot_general(pre_g, dout, (TN, ((), ())), preferred_element_type=F32)
            dpre = lax.dot_general(dout, w_ref[gi], (NT, ((), ())), preferred_element_type=F32)
            div = jnp.minimum(t + 1, win).astype(F32)
            a = dpre / div
            k = 1
            while k < win:
                a = a + _shift_up(a, k, t)
                k *= 2
            du_ref[:, cols] = (a - dpre).astype(du_ref.dtype)

    blk = pl.BlockSpec((s, pw_), lambda i: (0, 0))
    wspec = pl.BlockSpec((ng, pg, pg), lambda i: (0, 0, 0))
    vec = pl.BlockSpec((1, pw_), lambda i: (0, 0))
    return pl.pallas_call(
        body, grid=(1,), in_specs=[pl.BlockSpec((s, pw_), lambda i: (0, col_block)), blk, wspec, vec],
        out_specs=[blk, wspec, vec], out_shape=[_sds((s, pw_), _MXU), _sds((ng, pg, pg), F32), _sds((1, pw_), F32)],
        compiler_params=_cp(), name=name,
    )(dmix, pre, pool_w, pool_scale)


def _scan_chunks(xr_ref, xi_ref, sr_ref, si_ref, ar, ai, reverse):
    n, c = xr_ref.shape
    tt = min(SCAN_T, n)
    lw = min(SCAN_LANES, c)
    nchunk, ngroup = n // tt, tt // 8
    t8 = lax.broadcasted_iota(jnp.int32, (tt, lw), 0) % 8

    for l0 in range(0, c, lw):
        lanes = slice(l0, l0 + lw)
        a_r, a_i = ar[:, lanes], ai[:, lanes]

        def within8(vr, vi, a_r=a_r, a_i=a_i):
            rows = vr.shape[0]
            tq = t8[:rows, :]
            pr, pi = a_r, a_i
            k = 1
            while k < 8:
                if reverse:
                    hr = jnp.where(tq < 8 - k, pltpu.roll(vr, rows - k, 0), 0.0)
                    hi = jnp.where(tq < 8 - k, pltpu.roll(vi, rows - k, 0), 0.0)
                else:
                    hr = jnp.where(tq >= k, pltpu.roll(vr, k, 0), 0.0)
                    hi = jnp.where(tq >= k, pltpu.roll(vi, k, 0), 0.0)
                vr, vi = vr + pr * hr - pi * hi, vi + pr * hi + pi * hr
                pr, pi = pr * pr - pi * pi, 2.0 * pr * pi
                k *= 2
            return vr, vi

        at_edge = t8[:8, :] == (7 if reverse else 0)
        pw_r, pw_i = within8(jnp.where(at_edge, a_r, 0.0), jnp.where(at_edge, a_i, 0.0))
        last = 0 if reverse else 7

        def body(i, carry, lanes=lanes, within8=within8, pw_r=pw_r, pw_i=pw_i):
            cr, ci = carry
            ch = nchunk - 1 - i if reverse else i
            rows = pl.ds(pl.multiple_of(ch * tt, tt), tt)
            vr, vi = within8(xr_ref[rows, lanes], xi_ref[rows, lanes])
            out_r, out_i = [None] * ngroup, [None] * ngroup
            for g in (reversed(range(ngroup)) if reverse else range(ngroup)):
                br = vr[8 * g:8 * g + 8, :] + pw_r * cr - pw_i * ci
                bi = vi[8 * g:8 * g + 8, :] + pw_r * ci + pw_i * cr
                out_r[g], out_i[g] = br, bi
                cr, ci = br[last:last + 1, :], bi[last:last + 1, :]
            sr_ref[rows, lanes] = jnp.concatenate(out_r, 0)
            si_ref[rows, lanes] = jnp.concatenate(out_i, 0)
            return cr, ci

        lax.fori_loop(0, nchunk, body, (jnp.zeros((1, lw), F32), jnp.zeros((1, lw), F32)))


_GELU_K = math.sqrt(2.0 / math.pi)


def _gelu_grad(y):
    inner = _GELU_K * (y + 0.044715 * y * y * y)
    th = jnp.tanh(inner)
    return 0.5 * (1.0 + th) + 0.5 * y * (1.0 - th * th) * _GELU_K * (1.0 + 3.0 * 0.044715 * y * y)


def _ssm_fwd(name, h, u_block0, bdr, bdi, cdr, cdi, dvec, ar, ai):
    s = h.shape[0]
    nt, cw, lw = bdr.shape
    rc = min(256, s)

    def body(u_ref, bdr_ref, bdi_ref, cdr_ref, cdi_ref, d_ref, ar_ref, ai_ref, sr_ref, si_ref, y_ref, yg_ref):
        def mm_in(c, _):
            rows = pl.ds(pl.multiple_of(c * rc, rc), rc)
            ub = u_ref[rows, :].astype(_MXU)
            sr_ref[rows, :] = jnp.dot(ub, bdr_ref[...], preferred_element_type=F32)
            si_ref[rows, :] = jnp.dot(ub, bdi_ref[...], preferred_element_type=F32)
            return 0

        lax.fori_loop(0, s // rc, mm_in, 0)
        _scan_chunks(sr_ref, si_ref, sr_ref, si_ref, ar_ref[...], ai_ref[...], reverse=False)

        def mm_out(c, _):
            rows = pl.ds(pl.multiple_of(c * rc, rc), rc)
            y = (jnp.dot(sr_ref[rows, :].astype(_MXU), cdr_ref[...], preferred_element_type=F32)
                 - jnp.dot(si_ref[rows, :].astype(_MXU), cdi_ref[...], preferred_element_type=F32)
                 + d_ref[...] * u_ref[rows, :])
            y_ref[rows, :] = y
            yg_ref[rows, :] = jax.nn.gelu(y).astype(yg_ref.dtype)
            return 0

        lax.fori_loop(0, s // rc, mm_out, 0)

    st = pl.BlockSpec((s, lw), lambda j: (0, j))
    ch = pl.BlockSpec((s, cw), lambda j: (0, j))
    bspec = pl.BlockSpec((None, cw, lw), lambda j: (j, 0, 0))
    cspec = pl.BlockSpec((None, lw, cw), lambda j: (j, 0, 0))
    return pl.pallas_call(
        body, grid=(nt,),
        in_specs=[pl.BlockSpec((s, cw), lambda j: (0, u_block0 + j)), bspec, bspec, cspec, cspec,
                  pl.BlockSpec((1, cw), lambda j: (0, j)), pl.BlockSpec((1, lw), lambda j: (0, j)),
                  pl.BlockSpec((1, lw), lambda j: (0, j))],
        out_specs=[st, st, ch, ch],
        out_shape=[_sds((s, nt * lw), F32), _sds((s, nt * lw), F32), _sds((s, nt * cw), F32), _sds((s, nt * cw), _MXU)],
        compiler_params=_cp(), name=name,
    )(h, bdr, bdi, cdr, cdi, dvec, ar, ai)


def _ssm_bwd(name, dyg, ypre, h, u_block0, sr, si, bdr, bdi, cdr, cdi, dvec, ar, ai):
    s = h.shape[0]
    nt, cw, lw = bdr.shape
    rc = min(256, s)

    def body(dyg_ref, yp_ref, u_ref, sr_ref, si_ref, bdr_ref, bdi_ref, cdr_ref, cdi_ref, d_ref, ar_ref, ai_ref,
             du_ref, dd_ref, dcr_ref, dci_ref, dbr_ref, dbi_ref, dar_ref, dai_ref, lr_scr, li_scr, dy_scr):
        for ref in (dd_ref, dcr_ref, dci_ref, dbr_ref, dbi_ref, dar_ref, dai_ref):
            ref[...] = jnp.zeros_like(ref)

        def p1(c, _):
            rows = pl.ds(pl.multiple_of(c * rc, rc), rc)
            dy = dyg_ref[rows, :] * _gelu_grad(yp_ref[rows, :])
            dy_scr[rows, :] = dy
            dd_ref[...] += jnp.sum(dy * u_ref[rows, :], 0, keepdims=True)
            dyb = dy.astype(_MXU)
            lr_scr[rows, :] = lax.dot_general(dyb, cdr_ref[...], (NT, ((), ())), preferred_element_type=F32)
            li_scr[rows, :] = -lax.dot_general(dyb, cdi_ref[...], (NT, ((), ())), preferred_element_type=F32)
            dcr_ref[...] += lax.dot_general(sr_ref[rows, :].astype(_MXU), dyb, (TN, ((), ())), preferred_element_type=F32)
            dci_ref[...] -= lax.dot_general(si_ref[rows, :].astype(_MXU), dyb, (TN, ((), ())), preferred_element_type=F32)
            return 0

        lax.fori_loop(0, s // rc, p1, 0)
        _scan_chunks(lr_scr, li_scr, lr_scr, li_scr, ar_ref[...], -ai_ref[...], reverse=True)
        t = lax.broadcasted_iota(jnp.int32, (rc, lw), 0)

        def p2(c, _):
            r0 = pl.multiple_of(c * rc, rc)
            rows = pl.ds(r0, rc)
            before = pl.ds(pl.multiple_of(jnp.maximum(r0 - 8, 0), 8), 8)
            have = (c > 0).astype(F32)
            lr, li = lr_scr[rows, :], li_scr[rows, :]
            spr = jnp.where(t == 0, sr_ref[before, :][7:8, :] * have, pltpu.roll(sr_ref[rows, :], 1, 0))
            spi = jnp.where(t == 0, si_ref[before, :][7:8, :] * have, pltpu.roll(si_ref[rows, :], 1, 0))
            dar_ref[...] += jnp.sum(lr * spr + li * spi, 0, keepdims=True)
            dai_ref[...] += jnp.sum(li * spr - lr * spi, 0, keepdims=True)
            lrb, lib = lr.astype(_MXU), li.astype(_MXU)
            du = (dy_scr[rows, :] * d_ref[...]
                  + lax.dot_general(lrb, bdr_ref[...], (NT, ((), ())), preferred_element_type=F32)
                  + lax.dot_general(lib, bdi_ref[...], (NT, ((), ())), preferred_element_type=F32))
            du_ref[rows, :] = du.astype(du_ref.dtype)
            ub = u_ref[rows, :].astype(_MXU)
            dbr_ref[...] += lax.dot_general(ub, lrb, (TN, ((), ())), preferred_element_type=F32)
            dbi_ref[...] += lax.dot_general(ub, lib, (TN, ((), ())), preferred_element_type=F32)
            return 0

        lax.fori_loop(0, s // rc, p2, 0)

    st = pl.BlockSpec((s, lw), lambda j: (0, j))
    ch = pl.BlockSpec((s, cw), lambda j: (0, j))
    bspec = pl.BlockSpec((None, cw, lw), lambda j: (j, 0, 0))
    cspec = pl.BlockSpec((None, lw, cw), lambda j: (j, 0, 0))
    cvec = pl.BlockSpec((1, cw), lambda j: (0, j))
    svec = pl.BlockSpec((1, lw), lambda j: (0, j))
    return pl.pallas_call(
        body, grid=(nt,),
        in_specs=[ch, ch, pl.BlockSpec((s, cw), lambda j: (0, u_block0 + j)), st, st, bspec, bspec, cspec, cspec, cvec, svec, svec],
        out_specs=[ch, cvec, cspec, cspec, bspec, bspec, svec, svec],
        out_shape=[_sds((s, nt * cw), _MXU), _sds((1, nt * cw), F32), _sds((nt, lw, cw), F32), _sds((nt, lw, cw), F32),
                   _sds((nt, cw, lw), F32), _sds((nt, cw, lw), F32), _sds((1, nt * lw), F32), _sds((1, nt * lw), F32)],
        scratch_shapes=[pltpu.VMEM((s, lw), F32), pltpu.VMEM((s, lw), F32), pltpu.VMEM((s, cw), F32)],
        compiler_params=_cp(), name=name,
    )(dyg, ypre, h, sr, si, bdr, bdi, cdr, cdi, dvec, ar, ai)


def _glu_fwd(name, yg, gw):
    s, w = yg.shape
    tr = min(512, s)

    def body(y_ref, w_ref, o_ref, ab_ref):
        ab = jnp.dot(y_ref[...], w_ref[...], preferred_element_type=F32)
        ab_ref[...] = ab
        o_ref[...] = (ab[:, :w] * jax.nn.sigmoid(ab[:, w:])).astype(o_ref.dtype)

    return pl.pallas_call(
        body, grid=(s // tr,), in_specs=[pl.BlockSpec((tr, w), lambda i: (i, 0)), _resident((w, 2 * w), lambda i: (0, 0))],
        out_specs=[pl.BlockSpec((tr, w), lambda i: (i, 0)), pl.BlockSpec((tr, 2 * w), lambda i: (i, 0))],
        out_shape=[_sds((s, w), _MXU), _sds((s, 2 * w), F32)], compiler_params=_cp(), name=name,
    )(yg, gw)


def _glu_bwd(name, dmix, col_block, ab, gw):
    s = ab.shape[0]
    w = ab.shape[1] // 2
    tr = min(512, s)

    def body(do_ref, ab_ref, w_ref, dab_ref, dy_ref):
        do = do_ref[...]
        a, b = ab_ref[:, :w], ab_ref[:, w:]
        sg = jax.nn.sigmoid(b)
        da = (do * sg).astype(_MXU)
        db = (do * a * sg * (1.0 - sg)).astype(_MXU)
        dab_ref[:, :w] = da
        dab_ref[:, w:] = db
        dy_ref[...] = (lax.dot_general(da, w_ref[:, :w], (NT, ((), ())), preferred_element_type=F32)
                       + lax.dot_general(db, w_ref[:, w:], (NT, ((), ())), preferred_element_type=F32))

    return pl.pallas_call(
        body, grid=(s // tr,),
        in_specs=[pl.BlockSpec((tr, w), lambda i: (i, col_block)), pl.BlockSpec((tr, 2 * w), lambda i: (i, 0)),
                  _resident((w, 2 * w), lambda i: (0, 0))],
        out_specs=[pl.BlockSpec((tr, 2 * w), lambda i: (i, 0)), pl.BlockSpec((tr, w), lambda i: (i, 0))],
        out_shape=[_sds((s, 2 * w), _MXU), _sds((s, w), F32)], compiler_params=_cp(), name=name,
    )(dmix, ab, gw)


CONV_ROWS = 64


def _conv_chunk(ref, w_ref, b_ref, c, tt):
    r0 = pl.multiple_of(c * tt, tt)
    before = ref[pl.ds(pl.multiple_of(jnp.maximum(r0 - 8, 0), 8), 8), :]
    before = jnp.where(c > 0, before, 0.0)
    main = ref[pl.ds(r0, tt), :]
    ext = jnp.concatenate([before, main], 0)
    d1 = pltpu.roll(ext, 1, 0)[8:, :]
    d2 = pltpu.roll(ext, 2, 0)[8:, :]
    hc = b_ref[...] + d2 * w_ref[0:1, :]
    hc = hc + d1 * w_ref[1:2, :]
    return hc + main * w_ref[2:3, :], main, d1, d2


def _conv_act_fwd(name, hu, cw, cb):
    s, f2 = hu.shape
    f = f2 // 2
    tw = _col_tile(f, 256)
    nt = f // tw

    tt = min(CONV_ROWS, s)

    def body(v_ref, g_ref, wv_ref, wg_ref, bv_ref, bg_ref, act_ref):
        def chunk(c, _):
            val = _conv_chunk(v_ref, wv_ref, bv_ref, c, tt)[0]
            gate = _conv_chunk(g_ref, wg_ref, bg_ref, c, tt)[0]
            act_ref[pl.ds(pl.multiple_of(c * tt, tt), tt), :] = (jax.nn.silu(gate) * val).astype(act_ref.dtype)
            return 0

        lax.fori_loop(0, s // tt, chunk, 0)

    cv = lambda rows: pl.BlockSpec((rows, tw), lambda i: (0, i))
    cg = lambda rows: pl.BlockSpec((rows, tw), lambda i: (0, nt + i))
    return pl.pallas_call(
        body, grid=(nt,), in_specs=[cv(s), cg(s), cv(CONV_WIDTH), cg(CONV_WIDTH), cv(1), cg(1)],
        out_specs=cv(s), out_shape=_sds((s, f), _MXU), compiler_params=_cp(), name=name,
    )(hu, hu, cw, cw, cb, cb)


def _conv_act_bwd(name, dact, hu, cw, cb):
    s, f2 = hu.shape
    f = f2 // 2
    tw = _col_tile(f, 256)
    nt = f // tw

    tt = min(CONV_ROWS, s)
    nchunk = s // tt

    def body(da_ref, v_ref, g_ref, wv_ref, wg_ref, bv_ref, bg_ref, dh_ref, dwv_ref, dwg_ref, dbv_ref, dbg_ref):
        def chunk(i, carry):
            c = nchunk - 1 - i
            rows = pl.ds(pl.multiple_of(c * tt, tt), tt)
            val, hv, hv1, hv2 = _conv_chunk(v_ref, wv_ref, bv_ref, c, tt)
            gate, hg, hg1, hg2 = _conv_chunk(g_ref, wg_ref, bg_ref, c, tt)
            sg = jax.nn.sigmoid(gate)
            da = da_ref[rows, :]
            dval = da * (gate * sg)
            dgate = da * val * sg * (1.0 + gate * (1.0 - sg))
            out = []
            for part, dhc, taps, w_ref, (after, acc) in ((0, dval, (hv2, hv1, hv), wv_ref, carry[0]),
                                                        (1, dgate, (hg2, hg1, hg), wg_ref, carry[1])):
                ext = jnp.concatenate([dhc, after], 0)
                u1 = pltpu.roll(ext, tt + 8 - 1, 0)[:tt, :]
                u2 = pltpu.roll(ext, tt + 8 - 2, 0)[:tt, :]
                dh = dhc * w_ref[2:3, :] + u1 * w_ref[1:2, :] + u2 * w_ref[0:1, :]
                dh_ref[part, rows, :] = dh.astype(dh_ref.dtype)
                sums = [jnp.sum((dhc * tap).reshape(tt // 8, 8, tw), 0) for tap in taps]
                sums.append(jnp.sum(dhc.reshape(tt // 8, 8, tw), 0))
                out.append((dhc[0:8, :], tuple(a + b for a, b in zip(acc, sums))))
            return tuple(out)

        zero = (jnp.zeros((8, tw), F32), tuple(jnp.zeros((8, tw), F32) for _ in range(CONV_WIDTH + 1)))
        (_, acc_v), (_, acc_g) = lax.fori_loop(0, nchunk, chunk, (zero, zero))
        for acc, dw_ref, db_ref in ((acc_v, dwv_ref, dbv_ref), (acc_g, dwg_ref, dbg_ref)):
            for tap in range(CONV_WIDTH):
                dw_ref[tap:tap + 1, :] = jnp.sum(acc[tap], 0, keepdims=True)
            db_ref[...] = jnp.sum(acc[CONV_WIDTH], 0, keepdims=True)

    cv = lambda rows: pl.BlockSpec((rows, tw), lambda i: (0, i))
    cg = lambda rows: pl.BlockSpec((rows, tw), lambda i: (0, nt + i))
    both = pl.BlockSpec((2, s, tw), lambda i: (0, 0, i))
    dh, dwv, dwg, dbv, dbg = pl.pallas_call(
        body, grid=(nt,), in_specs=[cv(s), cv(s), cg(s), cv(CONV_WIDTH), cg(CONV_WIDTH), cv(1), cg(1)],
        out_specs=[both, cv(CONV_WIDTH), cv(CONV_WIDTH), cv(1), cv(1)],
        out_shape=[_sds((2, s, f), _MXU), _sds((CONV_WIDTH, f), F32), _sds((CONV_WIDTH, f), F32),
                   _sds((1, f), F32), _sds((1, f), F32)],
        compiler_params=_cp(), name=name,
    )(dact, hu, hu, cw, cw, cb, cb)
    return dh, jnp.concatenate([dwv, dwg], 1), jnp.concatenate([dbv, dbg], 1)


ELEM_BLOCK = 512 * 1024


def _elem_tiles(r, c, budget=ELEM_BLOCK):
    rows = [t for t in range(8, r + 1, 8) if r % t == 0] or [r]
    cols = [t for t in range(V7X_LANES, c + 1, V7X_LANES) if c % t == 0] or [c]
    fits = [(tr * tc, tc, tr) for tr in rows for tc in cols if tr * tc <= budget]
    if not fits:
        return min(rows), min(cols)
    _, tc, tr = max(fits)
    return tr, tc


def _sum_parts(name, parts):
    n, r, c = parts.shape
    tr, tc = _elem_tiles(r, c, ELEM_BLOCK // n)

    def body(p_ref, g_ref):
        g = p_ref[0]
        for i in range(1, n):
            g = g + p_ref[i]
        g_ref[...] = g

    return pl.pallas_call(
        body, grid=(r // tr, c // tc), in_specs=[pl.BlockSpec((n, tr, tc), lambda i, j: (0, i, j))],
        out_specs=pl.BlockSpec((tr, tc), lambda i, j: (i, j)), out_shape=_sds((r, c), F32), compiler_params=_cp(), name=name,
    )(parts)


def _adamw_2d(name, g, w, m, v):
    r, c = w.shape
    tc = c if c % V7X_LANES else _col_tile(c, 2048)
    rows = [t for t in range(8, r + 1, 8) if r % t == 0 and t * max(tc, V7X_LANES) <= ELEM_BLOCK // 4] or [r]
    tr = max(rows)
    c1 = 1.0 - ADAM_B1 ** ADAM_STEP
    c2 = 1.0 - ADAM_B2 ** ADAM_STEP

    def body(g_ref, w_ref, m_ref, v_ref, d_ref, nm_ref, nv_ref):
        g = g_ref[...]
        nm = ADAM_B1 * m_ref[...] + (1.0 - ADAM_B1) * g
        nv = ADAM_B2 * v_ref[...] + (1.0 - ADAM_B2) * (g * g)
        m_hat = nm / c1
        v_hat = nv / c2
        nm_ref[...] = nm
        nv_ref[...] = nv
        d_ref[...] = -ADAM_LR * (m_hat / (jnp.sqrt(v_hat) + ADAM_EPS) + ADAM_WD * w_ref[...])

    blk = pl.BlockSpec((tr, tc), lambda i, j: (i, j))
    out = _sds((r, c), F32)
    return pl.pallas_call(
        body, grid=(r // tr, c // tc), in_specs=[blk] * 4, out_specs=[blk] * 3, out_shape=[out] * 3,
        compiler_params=_cp(), name=name,
    )(g, w, m, v)


def _pair_sum(name, mine, theirs, c_idx):
    _, _, r, c = mine.shape
    tr, tc = _elem_tiles(r, c)

    def body(c_ref, a_ref, b_ref, o_ref):
        o_ref[...] = (a_ref[...].astype(F32) + b_ref[...].astype(F32)).astype(o_ref.dtype)

    return pl.pallas_call(
        body,
        grid_spec=pltpu.PrefetchScalarGridSpec(
            num_scalar_prefetch=1, grid=(4, r // tr, c // tc),
            in_specs=[pl.BlockSpec((None, None, tr, tc), lambda p, i, j, cref: (p, cref[0], i, j)),
                      pl.BlockSpec((None, tr, tc), lambda p, i, j, cref: (p, i, j))],
            out_specs=pl.BlockSpec((None, tr, tc), lambda p, i, j, cref: (p, i, j))),
        out_shape=_sds((4, r, c), _WIRE), compiler_params=_cp(), name=name,
    )(c_idx, mine, theirs)


def _place():
    return lax.axis_index("x"), lax.axis_index("y"), lax.axis_index("c")


def _all_gather(name, xs):
    n = len(xs)

    def body(*refs):
        x_refs, o_refs = refs[:n], refs[n:2 * n]
        send_sems, recv_sems, local_sems = refs[2 * n:]
        x, y, c = _place()
        me, sibling = (x, y, c), (x, y, 1 - c)
        chips = [(1 - x, y), (x, 1 - y), (1 - x, 1 - y)]

        def copy(a, k, block, to, src=None):
            px, py, pc = block
            rows = o_refs[a].at[4 * px + 2 * py + pc]
            return pltpu.make_async_remote_copy(
                src_ref=rows if src is None else src, dst_ref=rows, send_sem=send_sems.at[a, k], recv_sem=recv_sems.at[a, k],
                device_id=to, device_id_type=MESH)

        sent = []
        mine = []
        for a in range(n):
            mx, my, mc = me
            cp = pltpu.make_async_copy(x_refs[a], o_refs[a].at[4 * mx + 2 * my + mc], local_sems.at[a])
            cp.start()
            mine.append(cp)
            first = [copy(a, 0, me, sibling, src=x_refs[a])]
            first += [copy(a, 1 + j, me, (*chip, c), src=x_refs[a]) for j, chip in enumerate(chips)]
            for cp in first:
                cp.start()
            sent += first
        for a in range(n):
            for j, chip in enumerate(chips):
                copy(a, 1 + j, (*chip, c), me).wait_recv()
                fwd = copy(a, 4 + j, (*chip, c), sibling)
                fwd.start()
                sent.append(fwd)
        for a in range(n):
            copy(a, 0, sibling, me).wait_recv()
            for j, chip in enumerate(chips):
                copy(a, 4 + j, (*chip, 1 - c), me).wait_recv()
        for cp in sent:
            cp.wait_send()
        for cp in mine:
            cp.wait()

    return pl.pallas_call(
        body, in_specs=[ANY] * n, out_specs=[ANY] * n,
        out_shape=[_sds((N_DEV,) + a.shape, a.dtype) for a in xs],
        scratch_shapes=[pltpu.SemaphoreType.DMA((n, 7)), pltpu.SemaphoreType.DMA((n, 7)), pltpu.SemaphoreType.DMA((n,))],
        name=name,
    )(*xs)


HBM = pl.BlockSpec(memory_space=pltpu.HBM)
SEM = pl.BlockSpec(memory_space=pltpu.SEMAPHORE)
DATAFLOW = pltpu.SideEffectType.DATAFLOW_SIDE_EFFECTING


def _in_hbm(a):
    return pltpu.with_memory_space_constraint(a, pltpu.HBM)


def _split_copy_start(name, srcs, lands, copies, deps):
    n, nd = len(srcs), len(deps)
    per = len(copies([None] * n, [None] * n, probe=True)) // n

    def body(*refs):
        s_refs, l_refs = refs[:n], refs[n:2 * n]
        send_sems, recv_sems = refs[2 * n + nd], refs[2 * n + nd + 1]
        token = refs[-1]
        for a, k, src, dst, to in copies(s_refs, l_refs):
            pltpu.make_async_remote_copy(src_ref=src, dst_ref=dst, send_sem=send_sems.at[a * per + k],
                                         recv_sem=recv_sems.at[a * per + k], device_id=to, device_id_type=MESH).start()
        token[...] = jnp.zeros_like(token)

    both = list(srcs) + list(lands)
    outs = pl.pallas_call(
        body, name=name,
        out_shape=(pltpu.SemaphoreType.DMA((n * per,)), pltpu.SemaphoreType.DMA((n * per,)),
                   *[pltpu.HBM(a.shape, a.dtype) for a in both], _sds((8, V7X_LANES), F32)),
        in_specs=[HBM] * (2 * n) + [ANY] * nd,
        out_specs=(SEM, SEM, *[HBM] * (2 * n), pl.BlockSpec(memory_space=pltpu.VMEM)),
        input_output_aliases={i: 2 + i for i in range(2 * n)},
        compiler_params=pltpu.CompilerParams(has_side_effects=DATAFLOW),
    )(*[_in_hbm(a) for a in both], *deps)
    return outs[0], outs[1], list(outs[2:2 + n]), list(outs[2 + n:2 + 2 * n]), outs[-1]


def _split_copy_wait(name, send_sems, recv_sems, srcs, lands, arrivals, after):
    n = len(srcs)
    per = len(arrivals([None] * n, [None] * n, probe=True)) // n

    def body(*refs):
        s_refs, l_refs = refs[:n], refs[n:2 * n]
        send_sems_, recv_sems_ = refs[2 * n], refs[2 * n + 1]
        for a, k, src, dst, frm in arrivals(s_refs, l_refs):
            cp = pltpu.make_async_remote_copy(src_ref=src, dst_ref=dst, send_sem=send_sems_.at[a * per + k],
                                              recv_sem=recv_sems_.at[a * per + k], device_id=frm, device_id_type=MESH)
            cp.wait_send()
            cp.wait_recv()

    both = list(srcs) + list(lands)
    outs = pl.pallas_call(
        body, name=name, out_shape=tuple(pltpu.HBM(a.shape, a.dtype) for a in both),
        in_specs=[HBM] * (2 * n) + [SEM, SEM, ANY], out_specs=tuple([HBM] * (2 * n)),
        input_output_aliases={i: i for i in range(2 * n)},
        compiler_params=pltpu.CompilerParams(has_side_effects=DATAFLOW),
    )(*both, send_sems, recv_sems, after)
    return list(outs[:n]), list(outs[n:])


def _gather_copies(arriving):
    def copies(s_refs, l_refs, probe=False):
        if probe:
            return [None] * (4 * len(s_refs))
        x, y, c = _place()
        out = []
        for a in range(len(s_refs)):
            for k, (px, py, pc) in enumerate([(x, y, 1 - c), (1 - x, y, c), (x, 1 - y, c), (1 - x, 1 - y, c)]):
                slot = 4 * px + 2 * py + pc if arriving else 4 * x + 2 * y + c
                out.append((a, k, s_refs[a], l_refs[a].at[slot], (px, py, pc)))
        return out
    return copies


def _chip_copies(arriving):
    def copies(s_refs, l_refs, probe=False):
        if probe:
            return [None] * (3 * len(s_refs))
        x, y, c = _place()
        out = []
        for a in range(len(s_refs)):
            for j, (px, py) in enumerate([(1 - x, y), (x, 1 - y), (1 - x, 1 - y)]):
                src = s_refs[a].at[2 * x + y] if arriving else s_refs[a].at[2 * px + py]
                out.append((a, j, src, l_refs[a].at[j], (px, py, c)))
        return out
    return copies


def _sibling_copies(s_refs, l_refs, probe=False):
    if probe:
        return [None] * (4 * len(s_refs))
    x, y, c = _place()
    return [(a, p, s_refs[a].at[p, 1 - c], l_refs[a].at[p], (x, y, 1 - c)) for a in range(len(s_refs)) for p in range(4)]


def _sibling_begin(name, by_owner, deps):
    lands = [lax.empty((4,) + a.shape[2:], a.dtype) for a in by_owner]
    return _split_copy_start(name + "_start", by_owner, lands, _sibling_copies, deps)


def _sibling_end(name, handle, after):
    send_sems, recv_sems, srcs, lands, _ = handle
    return _split_copy_wait(name + "_wait", send_sems, recv_sems, srcs, lands, _sibling_copies, after)


def _gather_begin(name, shards, deps):
    x, y, c = _place()
    lands = [lax.dynamic_update_slice_in_dim(lax.empty((N_DEV,) + a.shape, a.dtype), a[None], 4 * x + 2 * y + c, 0)
             for a in shards]
    return _split_copy_start(name + "_start", shards, lands, _gather_copies(False), deps)


def _gather_end(name, handle, after):
    send_sems, recv_sems, srcs, lands, _ = handle
    _, lands = _split_copy_wait(name + "_wait", send_sems, recv_sems, srcs, lands, _gather_copies(True), after)
    return _gather_forward(name + "_forward", lands)


def _forward_copies(arriving):
    def copies(s_refs, l_refs, probe=False):
        if probe:
            return [None] * (3 * len(l_refs))
        x, y, c = _place()
        out = []
        for a in range(len(l_refs)):
            for j, (px, py) in enumerate([(1 - x, y), (x, 1 - y), (1 - x, 1 - y)]):
                mine, theirs = l_refs[a].at[4 * px + 2 * py + c], l_refs[a].at[4 * px + 2 * py + 1 - c]
                out.append((a, j, mine, theirs if arriving else mine, (x, y, 1 - c)))
        return out
    return copies


def _gather_arrived(name, handle, after):
    send_sems, recv_sems, srcs, lands, _ = handle
    srcs, lands = _split_copy_wait(name + "_wait", send_sems, recv_sems, srcs, lands, _gather_copies(True), after)
    return _split_copy_start(name + "_forward_start", srcs, lands, _forward_copies(False), [])


def _gather_done(name, handle, after):
    send_sems, recv_sems, srcs, lands, _ = handle
    return _split_copy_wait(name + "_forward_wait", send_sems, recv_sems, srcs, lands, _forward_copies(True), after)[1]


def _gather_forward(name, lands):
    n = len(lands)

    def body(*refs):
        o_refs = refs[n:2 * n]
        send_sems, recv_sems = refs[2 * n:]
        x, y, c = _place()
        sibling = (x, y, 1 - c)
        chips = [(1 - x, y), (x, 1 - y), (1 - x, 1 - y)]
        sent = []
        for a in range(n):
            for j, (px, py) in enumerate(chips):
                rows = o_refs[a].at[4 * px + 2 * py + c]
                cp = pltpu.make_async_remote_copy(src_ref=rows, dst_ref=rows, send_sem=send_sems.at[a, j],
                                                  recv_sem=recv_sems.at[a, j], device_id=sibling, device_id_type=MESH)
                cp.start()
                sent.append(cp)
        for a in range(n):
            for j, (px, py) in enumerate(chips):
                rows = o_refs[a].at[4 * px + 2 * py + 1 - c]
                pltpu.make_async_remote_copy(src_ref=rows, dst_ref=rows, send_sem=send_sems.at[a, j],
                                             recv_sem=recv_sems.at[a, j], device_id=sibling, device_id_type=MESH).wait_recv()
        for cp in sent:
            cp.wait_send()

    return pl.pallas_call(
        body, in_specs=[ANY] * n, out_specs=[ANY] * n, out_shape=[_sds(a.shape, a.dtype) for a in lands],
        input_output_aliases={i: i for i in range(n)},
        scratch_shapes=[pltpu.SemaphoreType.DMA((n, 3)), pltpu.SemaphoreType.DMA((n, 3))], name=name,
    )(*lands)


def _chips_begin(name, pairs, deps):
    lands = [lax.empty((3,) + a.shape[1:], a.dtype) for a in pairs]
    return _split_copy_start(name + "_start", pairs, lands, _chip_copies(False), deps)


def _chips_end(name, handle, after):
    send_sems, recv_sems, srcs, lands, _ = handle
    return _split_copy_wait(name + "_wait", send_sems, recv_sems, srcs, lands, _chip_copies(True), after)


def _adamw_layer(name, l, own, lands, w, m, v, prev):
    nl, ng, r, c = w.shape
    tr, tc = _elem_tiles(r, c)
    c1 = 1.0 - ADAM_B1 ** ADAM_STEP
    c2 = 1.0 - ADAM_B2 ** ADAM_STEP

    def body(own_ref, lands_ref, w_ref, m_ref, v_ref, *rest):
        g_ref, d_ref, nm_ref, nv_ref = rest[-4:]
        g = own_ref[...].astype(F32) + lands_ref[0].astype(F32) + lands_ref[1].astype(F32) + lands_ref[2].astype(F32)
        nm = ADAM_B1 * m_ref[...] + (1.0 - ADAM_B1) * g
        nv = ADAM_B2 * v_ref[...] + (1.0 - ADAM_B2) * (g * g)
        m_hat = nm / c1
        v_hat = nv / c2
        g_ref[...] = g
        nm_ref[...] = nm
        nv_ref[...] = nv
        d_ref[...] = -ADAM_LR * (m_hat / (jnp.sqrt(v_hat) + ADAM_EPS) + ADAM_WD * w_ref[...])

    lay = pl.BlockSpec((None, None, tr, tc), lambda g, i, j: (l, g, i, j))
    out = _sds((nl, ng, r, c), F32)
    prev = [] if prev is None else list(prev)
    return pl.pallas_call(
        body, grid=(ng, r // tr, c // tc),
        in_specs=[pl.BlockSpec((None, tr, tc), lambda g, i, j: (g, i, j)),
                  pl.BlockSpec((3, None, tr, tc), lambda g, i, j: (0, g, i, j)), lay, lay, lay] + [ANY] * len(prev),
        out_specs=[lay] * 4, out_shape=[out] * 4, input_output_aliases={5 + i: i for i in range(len(prev))},
        compiler_params=_cp(), name=name,
    )(own, lands, w, m, v, *prev)


def _pad_pairs(a, axis, half, half_pad):
    shp = a.shape
    a = a.reshape(shp[:axis] + (2, half) + shp[axis + 1:])
    pad = [(0, 0)] * a.ndim
    pad[axis + 1] = (0, half_pad - half)
    a = jnp.pad(a, pad)
    return a.reshape(shp[:axis] + (2 * half_pad,) + shp[axis + 1:])


def _unpad_pairs(a, axis, half, half_pad):
    shp = a.shape
    a = a.reshape(shp[:axis] + (2, half_pad) + shp[axis + 1:])
    a = lax.slice_in_dim(a, 0, half, axis=axis + 1)
    return a.reshape(shp[:axis] + (2 * half,) + shp[axis + 1:])


def _blockdiag(w, nt):
    g, a, b = w.shape
    gl = g // nt
    e = jnp.eye(gl, dtype=w.dtype).reshape(1, gl, 1, gl, 1)
    return (w.reshape(nt, gl, a, 1, b) * e).reshape(nt, gl * a, gl * b)


def _diagblocks(m, g, a, b):
    nt = m.shape[0]
    gl = g // nt
    d = jnp.diagonal(m.reshape(nt, gl, a, gl, b), axis1=1, axis2=3)
    return jnp.moveaxis(d, -1, 1).reshape(g, a, b)


_PIECE = 8 * V7X_LANES


def _pack(pieces, row_multiple=512):
    rows = []
    for p in pieces:
        flat = p.reshape(-1).astype(F32)
        rows.append(jnp.pad(flat, (0, -flat.shape[0] % _PIECE)).reshape(-1, V7X_LANES))
    fill = -sum(r.shape[0] for r in rows) % row_multiple
    if fill:
        rows.append(jnp.zeros((fill, V7X_LANES), F32))
    return jnp.concatenate(rows, 0)


def _unpack(packed, shapes):
    lead = packed.shape[:-2]
    out, off = [], 0
    for shp in shapes:
        n = math.prod(shp)
        r = -(-n // _PIECE) * 8
        piece = packed[..., off:off + r, :].reshape(lead + (r * V7X_LANES,))[..., :n]
        out.append(piece.reshape(lead + tuple(shp)))
        off += r
    return out


def _ssm_discretise(lam_re, lam_im, log_dt, b_re, b_im):
    dt = jnp.exp(log_dt)[..., None]
    mag = jnp.exp(lam_re * dt)
    ab_re, ab_im = mag * jnp.cos(lam_im * dt), mag * jnp.sin(lam_im * dt)
    nr, ni = ab_re - 1.0, ab_im
    den = lam_re * lam_re + lam_im * lam_im
    zr = (nr * lam_re + ni * lam_im) / den
    zi = (ni * lam_re - nr * lam_im) / den
    bbr = zr[..., None] * b_re - zi[..., None] * b_im
    bbi = zr[..., None] * b_im + zi[..., None] * b_re
    return ab_re, ab_im, bbr, bbi


def _rope_tables(s):
    half = HEAD_DIM // 2
    inv = ROPE_THETA ** (-jnp.arange(half, dtype=F32) / half)
    ang = jnp.arange(s).astype(F32)[:, None] * inv[None, :]
    cos, sin = jnp.cos(ang), jnp.sin(ang)
    reps = V7X_LANES // HEAD_DIM
    return jnp.tile(jnp.concatenate([cos, cos], -1), (1, reps)), jnp.tile(jnp.concatenate([-sin, sin], -1), (1, reps))


_SMALL = ("attn_sinks", "pool_w", "pool_scale", "ssm_lam_re", "ssm_lam_im", "ssm_log_dt", "ssm_b_re", "ssm_b_im",
          "ssm_c_re", "ssm_c_im", "ssm_d", "ln1_g", "ln1_b", "ffn_conv_b", "ln2_g", "ln2_b")
_BIG = ("w_in", "ssm_glu_w", "w_out", "ffn_w_up", "ffn_w_down")
_RAW = ("attn_sinks", "pool_w", "pool_scale", "ssm_lam_re", "ssm_lam_im", "ssm_b_re", "ssm_b_im", "ssm_c_re", "ssm_c_im",
        "ssm_d", "ln1_g", "ln1_b", "ffn_conv_b", "ln2_g", "ln2_b", "ffn_conv_w")
_ORDER = ("w_in", "attn_sinks", "pool_w", "pool_scale", "ssm_lam_re", "ssm_lam_im", "ssm_log_dt", "ssm_b_re", "ssm_b_im",
          "ssm_c_re", "ssm_c_im", "ssm_d", "ssm_glu_w", "w_out", "ln1_g", "ln1_b", "ffn_w_up", "ffn_conv_w", "ffn_conv_b",
          "ffn_w_down", "ln2_g", "ln2_b")


def kernel(x, w_in, attn_sinks, pool_w, pool_scale, ssm_lam_re, ssm_lam_im, ssm_log_dt, ssm_b_re, ssm_b_im, ssm_c_re, ssm_c_im, ssm_d, ssm_glu_w, w_out, ln1_g, ln1_b, ffn_w_up, ffn_conv_w, ffn_conv_b, ffn_w_down, ln2_g, ln2_b, loss_target, m_w_in, m_attn_sinks, m_pool_w, m_pool_scale, m_ssm_lam_re, m_ssm_lam_im, m_ssm_log_dt, m_ssm_b_re, m_ssm_b_im, m_ssm_c_re, m_ssm_c_im, m_ssm_d, m_ssm_glu_w, m_w_out, m_ln1_g, m_ln1_b, m_ffn_w_up, m_ffn_conv_w, m_ffn_conv_b, m_ffn_w_down, m_ln2_g, m_ln2_b, v_w_in, v_attn_sinks, v_pool_w, v_pool_scale, v_ssm_lam_re, v_ssm_lam_im, v_ssm_log_dt, v_ssm_b_re, v_ssm_b_im, v_ssm_c_re, v_ssm_c_im, v_ssm_d, v_ssm_glu_w, v_w_out, v_ln1_g, v_ln1_b, v_ffn_w_up, v_ffn_conv_w, v_ffn_conv_b, v_ffn_w_down, v_ln2_g, v_ln2_b):
    W = dict(w_in=w_in, attn_sinks=attn_sinks, pool_w=pool_w, pool_scale=pool_scale, ssm_lam_re=ssm_lam_re, ssm_lam_im=ssm_lam_im, ssm_log_dt=ssm_log_dt, ssm_b_re=ssm_b_re, ssm_b_im=ssm_b_im, ssm_c_re=ssm_c_re, ssm_c_im=ssm_c_im, ssm_d=ssm_d, ssm_glu_w=ssm_glu_w, w_out=w_out, ln1_g=ln1_g, ln1_b=ln1_b, ffn_w_up=ffn_w_up, ffn_conv_w=ffn_conv_w, ffn_conv_b=ffn_conv_b, ffn_w_down=ffn_w_down, ln2_g=ln2_g, ln2_b=ln2_b)
    M = dict(w_in=m_w_in, attn_sinks=m_attn_sinks, pool_w=m_pool_w, pool_scale=m_pool_scale, ssm_lam_re=m_ssm_lam_re, ssm_lam_im=m_ssm_lam_im, ssm_log_dt=m_ssm_log_dt, ssm_b_re=m_ssm_b_re, ssm_b_im=m_ssm_b_im, ssm_c_re=m_ssm_c_re, ssm_c_im=m_ssm_c_im, ssm_d=m_ssm_d, ssm_glu_w=m_ssm_glu_w, w_out=m_w_out, ln1_g=m_ln1_g, ln1_b=m_ln1_b, ffn_w_up=m_ffn_w_up, ffn_conv_w=m_ffn_conv_w, ffn_conv_b=m_ffn_conv_b, ffn_w_down=m_ffn_w_down, ln2_g=m_ln2_g, ln2_b=m_ln2_b)
    V = dict(w_in=v_w_in, attn_sinks=v_attn_sinks, pool_w=v_pool_w, pool_scale=v_pool_scale, ssm_lam_re=v_ssm_lam_re, ssm_lam_im=v_ssm_lam_im, ssm_log_dt=v_ssm_log_dt, ssm_b_re=v_ssm_b_re, ssm_b_im=v_ssm_b_im, ssm_c_re=v_ssm_c_re, ssm_c_im=v_ssm_c_im, ssm_d=v_ssm_d, ssm_glu_w=v_ssm_glu_w, w_out=v_w_out, ln1_g=v_ln1_g, ln1_b=v_ln1_b, ffn_w_up=v_ffn_w_up, ffn_conv_w=v_ffn_conv_w, ffn_conv_b=v_ffn_conv_b, ffn_w_down=v_ffn_w_down, ln2_g=v_ln2_g, ln2_b=v_ln2_b)

    depth = w_in.shape[0]
    s, d = x.shape[1], x.shape[2]
    alpha = (2 * depth) ** 0.25
    attn_w = d // 2
    kv_w = attn_w // GQA
    nkv = kv_w // HEAD_DIM
    pool_wd = d // 4
    ssm_wd = d // 4
    n_groups = ssm_wd // SSM_GROUP
    state_w = n_groups * SSM_STATE
    nt_ssm = max(1, state_w // 512)
    o_k, o_v, o_p, o_s = attn_w, attn_w + kv_w, attn_w + 2 * kv_w, attn_w + 2 * kv_w + pool_wd
    in_w = o_s + ssm_wd
    half = ffn_w_down.shape[1]
    half_pad = -(-half // 64) * 64
    ffp = 4 * 2 * half_pad
    xi, yi, ci = _place()
    me = 4 * xi + 2 * yi + ci
    c_idx = jnp.reshape(ci, (1,)).astype(jnp.int32)

    cos_t, sin_t = _rope_tables(s)

    def layer_shards(l):
        return [
            jnp.transpose(w_in[l]).astype(_WIRE), ssm_glu_w[l].astype(_WIRE), w_out[l].astype(_WIRE),
            _pad_pairs(jnp.transpose(ffn_w_up[l]).astype(_WIRE), 0, half, half_pad),
            jnp.pad(ffn_w_down[l].astype(_WIRE), ((0, half_pad - half), (0, 0))),
        ]

    (g_cw,) = _all_gather("gather_conv_w", [_pad_pairs(ffn_conv_w, 2, half, half_pad)])

    def in_weights(l, gathered):
        (g_in,) = gathered
        return dict(
            win_t=g_in.reshape(in_w, d),
            cw=jnp.transpose(g_cw[:, l], (1, 0, 2)).reshape(CONV_WIDTH, 2 * ffp),
            cb=_pad_pairs(ffn_conv_b[l].reshape(N_DEV, 2 * half), 1, half, half_pad).reshape(1, 2 * ffp),
        )

    def out_weights(gathered):
        g_glu, g_out = gathered
        return dict(glu=jnp.transpose(g_glu, (1, 0, 2)).reshape(ssm_wd, 2 * ssm_wd), wout=g_out.reshape(d, d))

    def mixer_weights(l, gathered):
        return {**in_weights(l, gathered[:1]), **out_weights(gathered[1:])}

    def up_weights(gathered):
        (g_up,) = gathered
        return dict(wup_t=g_up.reshape(2 * ffp, d))

    def down_weights(gathered):
        (g_down,) = gathered
        return dict(wdown=g_down.reshape(ffp, d))

    def ffn_weights(gathered):
        return {**up_weights(gathered[:1]), **down_weights(gathered[1:])}

    shards = [layer_shards(l) for l in range(depth)]
    full = [None] * depth
    gathers = {}

    def begin_gather(l, part, deps):
        arrays = shards[l][{"in": slice(0, 1), "out": slice(1, 3), "mixer": slice(0, 3), "up": slice(3, 4),
                            "down": slice(4, 5), "ffn": slice(3, 5)}[part]]
        gathers[l, part] = ("sent", _gather_begin(f"gather_{part}_weights_{l}", arrays, deps))
        return gathers[l, part][1][-1]

    def arrive_gather(l, part, after):
        gathers[l, part] = ("forwarding", _gather_arrived(f"gather_{part}_weights_{l}", gathers[l, part][1], after))
        return gathers[l, part][1][-1]

    def end_gather(l, part, after):
        stage, handle = gathers.pop((l, part))
        return (_gather_done if stage == "forwarding" else _gather_end)(f"gather_{part}_weights_{l}", handle, after)

    issued = begin_gather(0, "in", [g_cw])
    for part in ("out", "up", "down"):
        issued = begin_gather(0, part, [issued])
    full[0] = in_weights(0, end_gather(0, "in", g_cw))

    ssm_params = (ssm_lam_re, ssm_lam_im, ssm_log_dt, ssm_b_re, ssm_b_im)
    ab_re_all, ab_im_all, bbr_all, bbi_all = _ssm_discretise(*ssm_params)

    def ssm_maps(w):
        return jax.vmap(lambda t: _blockdiag(jnp.transpose(t, (0, 2, 1)), nt_ssm))(w).astype(_MXU)

    bdr_all, bdi_all, cdr_all, cdi_all = ssm_maps(bbr_all), ssm_maps(bbi_all), ssm_maps(ssm_c_re), ssm_maps(ssm_c_im)

    saved = []
    xf = x[0]
    xb = xf.astype(_MXU)
    for l in range(depth):
        fw = full[l]
        deps = [arrive_gather(l, "ffn", xb)] if l >= 2 else []
        if l + 1 < depth:
            issued = begin_gather(l + 1, "ffn", [begin_gather(l + 1, "mixer", [fw["win_t"], issued])])
            deps.append(issued)
        h = _mm_nt(f"in_proj_{l}", xb, fw["win_t"], deps=tuple(deps))
        q_rot, k_rot = _rope(f"rope_{l}", h, o_v, cos_t, sin_t, _MXU, ((0, o_k), (o_k, o_v)))
        k_hm = jnp.transpose(k_rot.reshape(s, nkv, HEAD_DIM), (1, 0, 2))
        v_hm = jnp.transpose(h[:, o_v:o_p].astype(_MXU).reshape(s, nkv, HEAD_DIM), (1, 0, 2))
        sinks = attn_sinks[l].reshape(nkv, GQA)
        o_attn, lse = _attn_fwd(f"attn_{l}", q_rot, k_hm, v_hm, sinks)
        pw_b = pool_w[l].astype(_MXU)
        psc = pool_scale[l].reshape(1, pool_wd)
        y_pool, pre = _pool_fwd(f"pool_{l}", h, o_p // pool_wd, pw_b, psc)
        bdr, bdi, cdr, cdi = bdr_all[l], bdi_all[l], cdr_all[l], cdi_all[l]
        dvec = ssm_d[l].reshape(1, ssm_wd)
        ar, ai = ab_re_all[l].reshape(1, state_w), ab_im_all[l].reshape(1, state_w)
        cw_ssm = ssm_wd // nt_ssm
        sr, si, ypre, yg = _ssm_fwd(f"ssm_{l}", h, o_s // cw_ssm, bdr, bdi, cdr, cdi, dvec, ar, ai)
        if l == 0:
            fw.update(out_weights(end_gather(0, "out", yg)))
        y_ssm, ab2 = _glu_fwd(f"glu_{l}", yg, fw["glu"])
        mix = jnp.concatenate([o_attn.astype(_MXU), y_pool, y_ssm], -1)
        a1 = _mm_nn(f"out_proj_{l}", mix, fw["wout"])
        g1, b1 = ln1_g[l].reshape(1, d), ln1_b[l].reshape(1, d)
        x1, x1b, xh1, rs1 = _ln_fwd(f"ln1_{l}", xf, a1, g1, b1, alpha)
        fw.update(up_weights(end_gather(0, "up", x1b)) if l == 0 else ffn_weights(end_gather(l, "ffn", x1b)))
        deps = (arrive_gather(l + 1, "mixer", x1b),) if 1 <= l < depth - 1 else ()
        hu = _mm_nt(f"ffn_up_{l}", x1b, fw["wup_t"], cap=2 * half_pad, deps=deps)
        act = _conv_act_fwd(f"ffn_act_{l}", hu, fw["cw"], fw["cb"])
        if l == 0:
            fw.update(down_weights(end_gather(0, "down", act)))
        f_out = _mm_nn(f"ffn_down_{l}", act, fw["wdown"], cap=256)
        g2, b2 = ln2_g[l].reshape(1, d), ln2_b[l].reshape(1, d)
        x2, x2b, xh2, rs2 = _ln_fwd(f"ln2_{l}", x1, f_out, g2, b2, alpha)
        saved.append(dict(xb=xb, h=h, q_rot=q_rot, k_hm=k_hm, v_hm=v_hm, sinks=sinks, o_attn=o_attn, lse=lse, pw_b=pw_b, psc=psc,
                          pre=pre, bdr=bdr, bdi=bdi, cdr=cdr, cdi=cdi, dvec=dvec, ar=ar, ai=ai, sr=sr, si=si, ypre=ypre, yg=yg,
                          ab2=ab2, mix=mix, g1=g1, xh1=xh1, rs1=rs1, x1b=x1b, hu=hu, act=act, g2=g2, xh2=xh2, rs2=rs2))
        xf, xb = x2, x2b
        if l + 1 < depth:
            full[l + 1] = mixer_weights(l + 1, end_gather(l + 1, "mixer", x2b))

    dy, loss_part = _loss_head("loss_head", xf, loss_target[0])
    loss = lax.psum(loss_part[0, 0], ("x", "y", "c"))

    small_handles = [None] * depth
    small_parts = [None] * depth
    outs = {}
    big_res = {k: None for k in _BIG}
    my_chip = 2 * xi + yi
    pending = []

    transposed = ("w_in", "ffn_w_up")

    def row_groups(name_, t):
        g = 2 if name_ == "ffn_w_up" else 1
        return t.reshape(t.shape[:-2] + (g, t.shape[-2] // g, t.shape[-1]))

    def as_groups(name_, t):
        return row_groups(name_, jnp.transpose(t, (0, 2, 1)) if name_ in transposed else t)

    def from_groups(name_, t):
        t = t.reshape(t.shape[0], t.shape[1] * t.shape[2], t.shape[3])
        return jnp.transpose(t, (0, 2, 1)) if name_ in transposed else t

    grouped = {name_: tuple(as_groups(name_, t[name_]) for t in (W, M, V)) for name_ in _BIG}

    def finish_exchanges(after):
        while pending:
            lay, part, names, handle = pending.pop(0)
            pairs, lands = _chips_end(f"grads_between_chips_{part}_{lay}", handle, after)
            for name_, p, ld in zip(names, pairs, lands):
                own = row_groups(name_, lax.dynamic_index_in_dim(p, my_chip, 0, keepdims=False))
                big_res[name_] = _adamw_layer(f"adamw_{name_}_{lay}", lay, own, row_groups(name_, ld), *grouped[name_],
                                              big_res[name_])

    def begin_swap(lay, part, names, grads):
        by_owner = [a.reshape((4, 2) + a.shape[1:]) for a in grads]
        return lay, part, names, _sibling_begin(f"grads_to_sibling_{part}_{lay}", by_owner, [])

    def begin_exchange(swap, after):
        lay, part, names, handle = swap
        by_owner, theirs = _sibling_end(f"grads_to_sibling_{part}_{lay}", handle, after)
        pair = [_pair_sum(f"pair_sum_{name_}_{lay}", a, b, c_idx) for name_, a, b in zip(names, by_owner, theirs)]
        finish_exchanges(after)
        handle = _chips_begin(f"grads_between_chips_{part}_{lay}", pair, [])
        pending.append((lay, part, names, handle))
        return handle[-1]

    token = None
    small_handle = None
    for l in reversed(range(depth)):
        fw, sv = full[l], saved[l]
        deps = () if token is None else (token, small_handles[l + 1][-1])
        dr2, dr2b, dg2, db2 = _ln_bwd(f"ln2_bwd_{l}", dy, sv["xh2"], sv["rs2"], sv["g2"], deps=deps)
        d_wdown = _mm_tn_acols(f"ffn_down_dw_{l}", sv["act"], dr2b, _WIRE, cap=2 * half_pad)
        dact = _mm_nt(f"ffn_down_dx_{l}", dr2b, fw["wdown"], cap=2 * half_pad)
        dhu, dcw, dcb = _conv_act_bwd(f"ffn_act_bwd_{l}", dact, sv["hu"], fw["cw"], fw["cb"])
        d_wup = _mm(f"ffn_up_dw_{l}", dhu, sv["x1b"], TN, (N_DEV, 1),
                    pl.BlockSpec((None, s, 2 * half_pad), lambda j, kk: (j // 4, 0, j % 4)),
                    _resident((s, d), lambda j, kk: (0, 0)),
                    pl.BlockSpec((2 * half_pad, d), lambda j, kk: (j, 0)), (2 * ffp, d), _WIRE)
        swap = begin_swap(l, "ffn", ("ffn_w_up", "ffn_w_down"),
                          [d_wup.reshape(N_DEV, 2 * half_pad, d), d_wdown.reshape(N_DEV, half_pad, d)])
        dy1 = _mm_split_k(f"ffn_up_dx_{l}", dhu, fw["wup_t"], dr2, alpha, deps=(swap[3][-1],))
        token = begin_exchange(swap, dy1)
        dr1, dr1b, dg1, db1 = _ln_bwd(f"ln1_bwd_{l}", dy1, sv["xh1"], sv["rs1"], sv["g1"], deps=(token,))
        d_wout = _mm_tn_acols(f"out_proj_dw_{l}", sv["mix"], dr1b, _WIRE, cap=d // N_DEV)
        dmix = _mm_nt(f"out_proj_dx_{l}", dr1b, fw["wout"])
        dq_rot, dk_hm, dv_hm, dsk = _attn_bwd(f"attn_bwd_{l}", sv["q_rot"], sv["k_hm"], sv["v_hm"], sv["o_attn"], dmix,
                                             sv["lse"], sv["sinks"])
        dqk = jnp.concatenate([dq_rot, jnp.transpose(dk_hm, (1, 0, 2)).reshape(s, kv_w)], -1)
        dhq, dhk = _rope(f"rope_bwd_{l}", dqk, o_v, cos_t, -sin_t, _MXU, ((0, o_k), (o_k, o_v)))
        dhv = jnp.transpose(dv_hm, (1, 0, 2)).reshape(s, kv_w).astype(_MXU)
        dhp, dpw, dpsc = _pool_bwd(f"pool_bwd_{l}", dmix, attn_w // pool_wd, sv["pre"], sv["pw_b"], sv["psc"])
        dab2, dyg = _glu_bwd(f"glu_bwd_{l}", dmix, (attn_w + pool_wd) // ssm_wd, sv["ab2"], fw["glu"])
        d_glu = _mm_tn_bcols(f"glu_dw_{l}", sv["yg"], dab2, _WIRE)
        cw_ssm = ssm_wd // nt_ssm
        dhs, dd, dcdr, dcdi, dbdr, dbdi, dar, dai = _ssm_bwd(
            f"ssm_bwd_{l}", dyg, sv["ypre"], sv["h"], o_s // cw_ssm, sv["sr"], sv["si"], sv["bdr"], sv["bdi"], sv["cdr"],
            sv["cdi"], sv["dvec"], sv["ar"], sv["ai"])
        dh = jnp.concatenate([dhq, dhk, dhv, dhp, dhs], -1)
        d_win = _mm_tn_acols(f"in_proj_dw_{l}", dh, sv["xb"], _WIRE)
        swap = begin_swap(l, "mixer", ("w_in", "ssm_glu_w", "w_out"),
                          [d_win.reshape(N_DEV, in_w // N_DEV, d),
                           jnp.transpose(d_glu.reshape(ssm_wd, N_DEV, 2 * ssm_wd // N_DEV), (1, 0, 2)),
                           d_wout.reshape(N_DEV, d // N_DEV, d)])
        dy = _mm_nn(f"in_proj_dx_{l}", dh, fw["win_t"], add=dr1, add_scale=alpha, deps=(swap[3][-1],))

        raw = dict(attn_sinks=dsk, pool_w=dpw, pool_scale=dpsc, ssm_lam_re=dar, ssm_lam_im=dai,
                   ssm_b_re=_diagblocks(dbdr, n_groups, SSM_GROUP, SSM_STATE),
                   ssm_b_im=_diagblocks(dbdi, n_groups, SSM_GROUP, SSM_STATE),
                   ssm_c_re=_diagblocks(dcdr, n_groups, SSM_STATE, SSM_GROUP),
                   ssm_c_im=_diagblocks(dcdi, n_groups, SSM_STATE, SSM_GROUP), ssm_d=dd, ln1_g=dg1, ln1_b=db1,
                   ffn_conv_b=_unpad_pairs(dcb.reshape(N_DEV, 2 * half_pad), 1, half, half_pad), ln2_g=dg2, ln2_b=db2,
                   ffn_conv_w=_unpad_pairs(dcw.reshape(CONV_WIDTH, N_DEV, 2 * half_pad), 2, half, half_pad))
        raw_shapes = {k: raw[k].shape for k in _RAW}
        small_handles[l] = _gather_begin(f"gather_small_grads_{l}", [_pack([raw[k] for k in _RAW])], [dy])
        token = begin_exchange(swap, small_handles[l][-1])
        if l + 1 < depth:
            small_parts[l + 1] = _sum_parts(f"sum_small_grads_{l + 1}",
                                            _gather_end(f"gather_small_grads_{l + 1}", small_handles[l + 1], dy)[0])

    small_parts[0] = _sum_parts("sum_small_grads_0", _gather_end("gather_small_grads_0", small_handles[0], token)[0])
    summed = jnp.stack(small_parts)
    g_small = dict(zip(_RAW, _unpack(summed, [raw_shapes[k] for k in _RAW])))
    swap_last = lambda t: jnp.transpose(t, (0, 1, 3, 2))
    _, vjp = jax.vjp(_ssm_discretise, *ssm_params)
    dlr, dli, dldt, dbr, dbi = vjp((g_small["ssm_lam_re"].reshape(depth, n_groups, SSM_STATE),
                                    g_small["ssm_lam_im"].reshape(depth, n_groups, SSM_STATE),
                                    swap_last(g_small["ssm_b_re"]), swap_last(g_small["ssm_b_im"])))
    g_small.update(ssm_lam_re=dlr, ssm_lam_im=dli, ssm_log_dt=dldt, ssm_b_re=dbr, ssm_b_im=dbi,
                   ssm_c_re=swap_last(g_small["ssm_c_re"]), ssm_c_im=swap_last(g_small["ssm_c_im"]),
                   ffn_conv_w=lax.dynamic_index_in_dim(g_small["ffn_conv_w"], me, axis=2, keepdims=False))
    for k in _SMALL + ("ffn_conv_w",):
        shp = W[k].shape
        flip = k in ("ssm_b_re", "ssm_b_im")
        view = (lambda t: jnp.swapaxes(t, -1, -2)) if flip else (lambda t: t)
        vshp = view(W[k]).shape
        two_d = (math.prod(vshp[:-1]), vshp[-1])
        res = _adamw_2d(f"adamw_{k}", view(g_small[k].reshape(shp)).reshape(two_d),
                        *(view(t[k]).reshape(two_d) for t in (W, M, V)))
        outs[k] = (g_small[k].reshape(shp),) + tuple(view(a.reshape(vshp)) for a in res)

    finish_exchanges(outs["ln2_b"][1])
    for name_ in _BIG:
        outs[name_] = tuple(from_groups(name_, t) for t in big_res[name_])

    grad_x = dy[None]
    result = [loss, grad_x]
    for i in range(4):
        result += [outs[k][i] for k in _ORDER]
    return tuple(result)
```

```python
import math

import jax
import jax.numpy as jnp
from jax import lax
from jax.experimental import pallas as pl
from jax.experimental.pallas import tpu as pltpu

F32 = jnp.float32
_MXU = jnp.bfloat16
_WIRE = jnp.bfloat16

HEAD_DIM = 64
GQA = 4
ATTN_BLOCK = 128
ROPE_THETA = 10000.0
POOL_WINDOWS = (2, 4, 8, 16)
SSM_GROUP = 16
SSM_STATE = 64
CONV_WIDTH = 3
LN_EPS = 1e-5
ADAM_LR, ADAM_B1, ADAM_B2, ADAM_EPS, ADAM_WD, ADAM_STEP = 0.001, 0.9, 0.999, 1e-08, 0.01, 10

N_DEV = 8
V7X_LANES = 128
V7X_VMEM_LIMIT = 56 * 1024 * 1024
SCAN_T = 64
SCAN_LANES = 256
MESH = pl.DeviceIdType.MESH
ANY = pl.BlockSpec(memory_space=pl.ANY)


def _cp():
    return pltpu.CompilerParams(vmem_limit_bytes=V7X_VMEM_LIMIT)


def _resident(block, index_map):
    return pl.BlockSpec(block, index_map, pipeline_mode=pl.Buffered(1))


def _sds(shape, dtype):
    return jax.ShapeDtypeStruct(tuple(shape), dtype)


def _mm(name, a, b, dims, grid, a_spec, b_spec, o_spec, out_shape, out_dtype, add=None, add_spec=None, add_scale=1.0, deps=()):
    nk = grid[1]
    n_in = 2 + (add is not None) + len(deps)
    oblk = tuple(d for d in o_spec.block_shape if d is not None)
    scratch = nk > 1 and out_dtype != F32

    def body(*refs):
        a_ref, b_ref = refs[:2]
        add_ref = None if add is None else refs[2]
        o_ref = refs[n_in]
        acc_ref = refs[-1] if scratch else None

        def finish(r):
            if add_ref is not None:
                r = r + add_scale * add_ref[...]
            o_ref[...] = r.astype(o_ref.dtype)

        part = lax.dot_general(a_ref[...], b_ref[...], (dims, ((), ())), preferred_element_type=F32)
        if nk == 1:
            finish(part)
        elif not scratch:
            k = pl.program_id(1)

            @pl.when(k == 0)
            def _():
                o_ref[...] = part

            @pl.when(k > 0)
            def _():
                o_ref[...] += part

            if add_ref is not None:
                @pl.when(k == nk - 1)
                def _():
                    o_ref[...] += add_scale * add_ref[...]
        else:
            k = pl.program_id(1)

            @pl.when(k == 0)
            def _():
                acc_ref[...] = part

            @pl.when(k > 0)
            def _():
                acc_ref[...] += part

            @pl.when(k == nk - 1)
            def _():
                finish(acc_ref[...])

    ins = [a, b] + ([] if add is None else [add]) + list(deps)
    in_specs = [a_spec, b_spec] + ([] if add is None else [add_spec]) + [ANY] * len(deps)
    return pl.pallas_call(
        body, grid=grid, in_specs=in_specs, out_specs=o_spec, out_shape=_sds(out_shape, out_dtype),
        scratch_shapes=[pltpu.VMEM(oblk, F32)] if scratch else [], compiler_params=_cp(), name=name,
    )(*ins)


NN = ((1,), (0,))
NT = ((1,), (1,))
TN = ((0,), (0,))


def _mm_split_k(name, a2, b, add, add_scale, deps=()):
    _, m, f = a2.shape
    n = b.shape[1]
    tm, tn = m // 2, _col_tile(n, 256)

    def body(a_ref, b_ref, add_ref, *rest):
        o_ref = rest[-1]
        o_ref[...] = (jnp.dot(a_ref[0], b_ref[:f, :], preferred_element_type=F32)
                      + jnp.dot(a_ref[1], b_ref[f:, :], preferred_element_type=F32) + add_scale * add_ref[...])

    tile = pl.BlockSpec((tm, tn), lambda i, j: (i, j))
    return pl.pallas_call(
        body, grid=(m // tm, n // tn),
        in_specs=[_resident((2, tm, f), lambda i, j: (0, i, 0)), pl.BlockSpec((2 * f, tn), lambda i, j: (0, j)), tile]
        + [ANY] * len(deps),
        out_specs=tile, out_shape=_sds((m, n), F32), compiler_params=_cp(), name=name,
    )(a2, b, add, *deps)


def _col_tile(n, cap=512):
    if n % V7X_LANES:
        return n
    t = min(cap, n)
    t -= t % V7X_LANES
    while n % t:
        t -= V7X_LANES
    return t


def _mm_nn(name, a, b, out_dtype=F32, cap=512, add=None, add_scale=1.0, deps=()):
    m, k = a.shape
    n = b.shape[1]
    tn = _col_tile(n, cap)
    o_spec = pl.BlockSpec((m, tn), lambda j, kk: (0, j))
    return _mm(name, a, b, NN, (n // tn, 1), _resident((m, k), lambda j, kk: (0, 0)),
               pl.BlockSpec((k, tn), lambda j, kk: (0, j)), o_spec, (m, n), out_dtype,
               add=add, add_spec=None if add is None else o_spec, add_scale=add_scale, deps=deps)


def _mm_nt(name, a, b, out_dtype=F32, add=None, add_scale=1.0, cap=512, deps=()):
    m, k = a.shape
    n = b.shape[0]
    tn = _col_tile(n, cap)
    o_spec = pl.BlockSpec((m, tn), lambda j, kk: (0, j))
    return _mm(name, a, b, NT, (n // tn, 1), _resident((m, k), lambda j, kk: (0, 0)),
               pl.BlockSpec((tn, k), lambda j, kk: (j, 0)), o_spec, (m, n), out_dtype,
               add=add, add_spec=None if add is None else o_spec, add_scale=add_scale, deps=deps)


def _mm_tn_bcols(name, a, b, out_dtype, cap=512):
    s, k = a.shape
    n = b.shape[1]
    tn = _col_tile(n, cap)
    return _mm(name, a, b, TN, (n // tn, 1), _resident((s, k), lambda j, kk: (0, 0)),
               pl.BlockSpec((s, tn), lambda j, kk: (0, j)), pl.BlockSpec((k, tn), lambda j, kk: (0, j)), (k, n), out_dtype)


def _mm_tn_acols(name, a, b, out_dtype, cap=512):
    s, k = a.shape
    n = b.shape[1]
    tk = _col_tile(k, cap)
    return _mm(name, a, b, TN, (k // tk, 1), pl.BlockSpec((s, tk), lambda i, kk: (0, i)),
               _resident((s, n), lambda i, kk: (0, 0)), pl.BlockSpec((tk, n), lambda i, kk: (i, 0)), (k, n), out_dtype)


def _ln_fwd(name, x, a, g, b, alpha):
    s, d = x.shape
    tr = min(256, s)

    def body(x_ref, a_ref, g_ref, b_ref, y_ref, yb_ref, xh_ref, rs_ref):
        r = alpha * x_ref[...] + a_ref[...]
        mu = jnp.mean(r, -1, keepdims=True)
        c = r - mu
        var = jnp.mean(c * c, -1, keepdims=True)
        rstd = lax.rsqrt(var + LN_EPS)
        xh = c * rstd
        y = xh * g_ref[...] + b_ref[...]
        y_ref[...] = y
        yb_ref[...] = y.astype(_MXU)
        xh_ref[...] = xh
        rs_ref[...] = rstd

    row = pl.BlockSpec((tr, d), lambda i: (i, 0))
    vec = pl.BlockSpec((1, d), lambda i: (0, 0))
    return pl.pallas_call(
        body, grid=(s // tr,), in_specs=[row, row, vec, vec],
        out_specs=[row, row, row, pl.BlockSpec((tr, 1), lambda i: (i, 0))],
        out_shape=[_sds((s, d), F32), _sds((s, d), _MXU), _sds((s, d), F32), _sds((s, 1), F32)],
        compiler_params=_cp(), name=name,
    )(x, a, g, b)


def _ln_bwd(name, dy, xh, rstd, g, deps=()):
    s, d = dy.shape
    tr = min(256, s)
    nd = len(deps)

    def body(dy_ref, xh_ref, rs_ref, g_ref, *rest):
        dr_ref, drb_ref, dg_ref, db_ref = rest[nd:]
        i = pl.program_id(0)
        dy_ = dy_ref[...]
        xh_ = xh_ref[...]
        dxh = dy_ * g_ref[...]
        m1 = jnp.mean(dxh, -1, keepdims=True)
        m2 = jnp.mean(dxh * xh_, -1, keepdims=True)
        dr = rs_ref[...] * (dxh - m1 - xh_ * m2)
        dr_ref[...] = dr
        drb_ref[...] = dr.astype(_MXU)
        pg = jnp.sum(dy_ * xh_, 0, keepdims=True)
        pb = jnp.sum(dy_, 0, keepdims=True)

        @pl.when(i == 0)
        def _():
            dg_ref[...] = pg
            db_ref[...] = pb

        @pl.when(i > 0)
        def _():
            dg_ref[...] += pg
            db_ref[...] += pb

    row = pl.BlockSpec((tr, d), lambda i: (i, 0))
    vec = pl.BlockSpec((1, d), lambda i: (0, 0))
    return pl.pallas_call(
        body, grid=(s // tr,), in_specs=[row, row, pl.BlockSpec((tr, 1), lambda i: (i, 0)), vec] + [ANY] * nd,
        out_specs=[row, row, vec, vec],
        out_shape=[_sds((s, d), F32), _sds((s, d), _MXU), _sds((1, d), F32), _sds((1, d), F32)],
        compiler_params=_cp(), name=name,
    )(dy, xh, rstd, g, *deps)


def _loss_head(name, y, target):
    s, d = y.shape
    tr = min(256, s)

    def body(y_ref, t_ref, dy_ref, l_ref):
        i = pl.program_id(0)
        e = y_ref[...] - t_ref[...]
        dy_ref[...] = e * (1.0 / d)
        part = 0.5 * jnp.sum(jnp.mean(e * e, -1, keepdims=True), 0, keepdims=True)

        @pl.when(i == 0)
        def _():
            l_ref[...] = part

        @pl.when(i > 0)
        def _():
            l_ref[...] += part

    row = pl.BlockSpec((tr, d), lambda i: (i, 0))
    return pl.pallas_call(
        body, grid=(s // tr,), in_specs=[row, row], out_specs=[row, pl.BlockSpec((1, 1), lambda i: (0, 0))],
        out_shape=[_sds((s, d), F32), _sds((1, 1), F32)], compiler_params=_cp(), name=name,
    )(y, target)


def _rope(name, t, width, cos, sin, out_dtype, splits):
    s = t.shape[0]
    tr = min(256, s)
    assert width % V7X_LANES == 0

    def body(t_ref, c_ref, s_ref, *o_refs):
        lane = lax.broadcasted_iota(jnp.int32, (tr, V7X_LANES), 1)
        first = (lane % HEAD_DIM) < (HEAD_DIM // 2)
        cs, sn = c_ref[...], s_ref[...]
        for (lo, hi), o_ref in zip(splits, o_refs):
            for c0 in range(lo, hi, V7X_LANES):
                v = t_ref[:, c0:c0 + V7X_LANES].astype(F32)
                partner = jnp.where(first, pltpu.roll(v, V7X_LANES - HEAD_DIM // 2, 1), pltpu.roll(v, HEAD_DIM // 2, 1))
                o_ref[:, c0 - lo:c0 - lo + V7X_LANES] = (v * cs + partner * sn).astype(o_ref.dtype)

    tab = pl.BlockSpec((tr, V7X_LANES), lambda i: (i, 0))
    return pl.pallas_call(
        body, grid=(s // tr,), in_specs=[pl.BlockSpec((tr, width), lambda i: (i, 0)), tab, tab],
        out_specs=[pl.BlockSpec((tr, hi - lo), lambda i: (i, 0)) for lo, hi in splits],
        out_shape=[_sds((s, hi - lo), out_dtype) for lo, hi in splits], compiler_params=_cp(), name=name,
    )(t, cos, sin)


def _attn_masks():
    i = lax.broadcasted_iota(jnp.int32, (GQA * ATTN_BLOCK, 2 * ATTN_BLOCK), 0) % ATTN_BLOCK
    j = lax.broadcasted_iota(jnp.int32, (GQA * ATTN_BLOCK, 2 * ATTN_BLOCK), 1)
    cur_ok = jnp.logical_and(j >= ATTN_BLOCK, j - ATTN_BLOCK <= i)
    prev_ok = jnp.logical_and(j < ATTN_BLOCK, j > i)
    return cur_ok, prev_ok


def _attn_scores(q4, kcat, n, cur_ok, prev_ok):
    sc = lax.dot_general(q4, kcat, (NT, ((), ())), preferred_element_type=F32) * (HEAD_DIM ** -0.5)
    return jnp.where(jnp.logical_or(cur_ok, jnp.logical_and(prev_ok, n > 0)), sc, -1e30)


def _stack_heads(ref, rows):
    return jnp.concatenate([ref[rows, g * HEAD_DIM:(g + 1) * HEAD_DIM] for g in range(GQA)], 0)


def _per_head_column(values):
    r = lax.broadcasted_iota(jnp.int32, (GQA * ATTN_BLOCK, 1), 0) // ATTN_BLOCK
    col = jnp.zeros((GQA * ATTN_BLOCK, 1), F32)
    for g, val in enumerate(values):
        col = jnp.where(r == g, val, col)
    return col


def _attn_fwd(name, q, k, v, sinks):
    s = q.shape[0]
    nkv = k.shape[0]
    gw = GQA * HEAD_DIM
    nb = s // ATTN_BLOCK

    def body(sk_ref, q_ref, k_ref, v_ref, o_ref, lse_ref):
        h = pl.program_id(0)
        cur_ok, prev_ok = _attn_masks()
        sink = _per_head_column([sk_ref[h, g] for g in range(GQA)])

        def blk(n, carry):
            rows = pl.ds(pl.multiple_of(n * ATTN_BLOCK, ATTN_BLOCK), ATTN_BLOCK)
            prows = pl.ds(pl.multiple_of(jnp.maximum(n - 1, 0) * ATTN_BLOCK, ATTN_BLOCK), ATTN_BLOCK)
            kcat = jnp.concatenate([k_ref[prows, :], k_ref[rows, :]], 0)
            vcat = jnp.concatenate([v_ref[prows, :], v_ref[rows, :]], 0)
            sc = _attn_scores(_stack_heads(q_ref, rows), kcat, n, cur_ok, prev_ok)
            m = jnp.maximum(sc.max(-1, keepdims=True), sink)
            p = jnp.exp(sc - m)
            den = p.sum(-1, keepdims=True) + jnp.exp(sink - m)
            o = jnp.dot((p / den).astype(_MXU), vcat, preferred_element_type=F32)
            lse = m + jnp.log(den)
            for g in range(GQA):
                mine = slice(g * ATTN_BLOCK, (g + 1) * ATTN_BLOCK)
                o_ref[rows, g * HEAD_DIM:(g + 1) * HEAD_DIM] = o[mine, :]
                lse_ref[rows, g:g + 1] = lse[mine, :]
            return carry

        lax.fori_loop(0, nb, blk, 0)

    kv_spec = pl.BlockSpec((None, s, HEAD_DIM), lambda h: (h, 0, 0))
    return pl.pallas_call(
        body, grid=(nkv,),
        in_specs=[pl.BlockSpec(memory_space=pltpu.SMEM), pl.BlockSpec((s, gw), lambda h: (0, h)), kv_spec, kv_spec],
        out_specs=[pl.BlockSpec((s, gw), lambda h: (0, h)), pl.BlockSpec((None, s, GQA), lambda h: (h, 0, 0))],
        out_shape=[_sds((s, nkv * gw), F32), _sds((nkv, s, GQA), F32)], compiler_params=_cp(), name=name,
    )(sinks, q, k, v)


def _attn_bwd(name, q, k, v, o, dmix, lse, sinks):
    s = q.shape[0]
    nkv = k.shape[0]
    gw = GQA * HEAD_DIM
    nb = s // ATTN_BLOCK
    scale = HEAD_DIM ** -0.5

    def body(sk_ref, q_ref, k_ref, v_ref, o_ref, do_ref, lse_ref, dq_ref, dk_ref, dv_ref, dsk_ref):
        h = pl.program_id(0)
        cur_ok, prev_ok = _attn_masks()
        dk_ref[...] = jnp.zeros_like(dk_ref)
        dv_ref[...] = jnp.zeros_like(dv_ref)

        sink = _per_head_column([sk_ref[h, g] for g in range(GQA)])

        def blk(n, acc):
            rows = pl.ds(pl.multiple_of(n * ATTN_BLOCK, ATTN_BLOCK), ATTN_BLOCK)
            prows = pl.ds(pl.multiple_of(jnp.maximum(n - 1, 0) * ATTN_BLOCK, ATTN_BLOCK), ATTN_BLOCK)
            kcat = jnp.concatenate([k_ref[prows, :], k_ref[rows, :]], 0)
            vcat = jnp.concatenate([v_ref[prows, :], v_ref[rows, :]], 0)
            q4 = _stack_heads(q_ref, rows)
            do4 = _stack_heads(do_ref, rows)
            delta = jnp.sum(do4 * _stack_heads(o_ref, rows), -1, keepdims=True)
            dob = do4.astype(_MXU)
            lse = jnp.concatenate([lse_ref[rows, g:g + 1] for g in range(GQA)], 0)
            p = jnp.exp(_attn_scores(q4, kcat, n, cur_ok, prev_ok) - lse)
            dp = lax.dot_general(dob, vcat, (NT, ((), ())), preferred_element_type=F32)
            ds = (p * (dp - delta) * scale).astype(_MXU)
            dq = jnp.dot(ds, kcat, preferred_element_type=F32)
            for g in range(GQA):
                dq_ref[rows, g * HEAD_DIM:(g + 1) * HEAD_DIM] = dq[g * ATTN_BLOCK:(g + 1) * ATTN_BLOCK, :]
            dk = jnp.dot(q4.T, ds, preferred_element_type=F32).T
            dv = jnp.dot(dob.T, p.astype(_MXU), preferred_element_type=F32).T
            dk_ref[prows, :] += dk[:ATTN_BLOCK, :]
            dv_ref[prows, :] += dv[:ATTN_BLOCK, :]
            dk_ref[rows, :] += dk[ATTN_BLOCK:, :]
            dv_ref[rows, :] += dv[ATTN_BLOCK:, :]
            return acc - jnp.exp(sink - lse) * delta

        acc = lax.fori_loop(0, nb, blk, jnp.zeros((GQA * ATTN_BLOCK, 1), F32))
        for g in range(GQA):
            dsk_ref[:, g:g + 1] = jnp.sum(acc[g * ATTN_BLOCK:(g + 1) * ATTN_BLOCK, :], 0, keepdims=True)

    kv_spec = pl.BlockSpec((None, s, HEAD_DIM), lambda h: (h, 0, 0))
    qcols = pl.BlockSpec((s, gw), lambda h: (0, h))
    return pl.pallas_call(
        body, grid=(nkv,),
        in_specs=[pl.BlockSpec(memory_space=pltpu.SMEM), qcols, kv_spec, kv_spec, qcols, qcols,
                  pl.BlockSpec((None, s, GQA), lambda h: (h, 0, 0))],
        out_specs=[qcols, kv_spec, kv_spec, pl.BlockSpec((None, 1, GQA), lambda h: (h, 0, 0))],
        out_shape=[_sds((s, nkv * gw), F32), _sds((nkv, s, HEAD_DIM), F32), _sds((nkv, s, HEAD_DIM), F32),
                   _sds((nkv, 1, GQA), F32)],
        compiler_params=_cp(), name=name,
    )(sinks, q, k, v, o, dmix, lse)


def _shift_down(a, k, t):
    return jnp.where(t >= k, pltpu.roll(a, k, 0), 0.0)


def _shift_up(a, k, t):
    n = a.shape[0]
    return jnp.where(t < n - k, pltpu.roll(a, n - k, 0), 0.0)


def _pool_fwd(name, h, col_block, pool_w, pool_scale):
    s = h.shape[0]
    ng, pg = pool_w.shape[0], pool_w.shape[1]
    pw_ = ng * pg

    def body(u_ref, w_ref, sc_ref, y_ref, pre_ref):
        t = lax.broadcasted_iota(jnp.int32, (s, pg), 0)
        for gi, win in enumerate(POOL_WINDOWS):
            cols = slice(gi * pg, (gi + 1) * pg)
            u = u_ref[:, cols]
            a = u
            k = 1
            while k < win:
                a = a + _shift_down(a, k, t)
                k *= 2
            div = jnp.minimum(t + 1, win).astype(F32)
            pre = (a / div - u).astype(_MXU)
            pre_ref[:, cols] = pre
            out = jnp.dot(pre, w_ref[gi], preferred_element_type=F32)
            y_ref[:, cols] = (out * sc_ref[:, cols]).astype(y_ref.dtype)

    blk = pl.BlockSpec((s, pw_), lambda i: (0, 0))
    return pl.pallas_call(
        body, grid=(1,),
        in_specs=[pl.BlockSpec((s, pw_), lambda i: (0, col_block)), pl.BlockSpec((ng, pg, pg), lambda i: (0, 0, 0)),
                  pl.BlockSpec((1, pw_), lambda i: (0, 0))],
        out_specs=[blk, blk], out_shape=[_sds((s, pw_), _MXU), _sds((s, pw_), _MXU)], compiler_params=_cp(), name=name,
    )(h, pool_w, pool_scale)


def _pool_bwd(name, dmix, col_block, pre, pool_w, pool_scale):
    s = pre.shape[0]
    ng, pg = pool_w.shape[0], pool_w.shape[1]
    pw_ = ng * pg

    def body(dy_ref, pre_ref, w_ref, sc_ref, du_ref, dw_ref, dsc_ref):
        t = lax.broadcasted_iota(jnp.int32, (s, pg), 0)
        for gi, win in enumerate(POOL_WINDOWS):
            cols = slice(gi * pg, (gi + 1) * pg)
            pre_g = pre_ref[:, cols]
            dy = dy_ref[:, cols]
            out = jnp.dot(pre_g, w_ref[gi], preferred_element_type=F32)
            dsc_ref[:, cols] = jnp.sum(dy * out, 0, keepdims=True)
            dout = (dy * sc_ref[:, cols]).astype(_MXU)
            dw_ref[gi] = lax.dot_general(pre_g, dout, (TN, ((), ())), preferred_element_type=F32)
            dpre = lax.dot_general(dout, w_ref[gi], (NT, ((), ())), preferred_element_type=F32)
            div = jnp.minimum(t + 1, win).astype(F32)
            a = dpre / div
            k = 1
            while k < win:
                a = a + _shift_up(a, k, t)
                k *= 2
            du_ref[:, cols] = (a - dpre).astype(du_ref.dtype)

    blk = pl.BlockSpec((s, pw_), lambda i: (0, 0))
    wspec = pl.BlockSpec((ng, pg, pg), lambda i: (0, 0, 0))
    vec = pl.BlockSpec((1, pw_), lambda i: (0, 0))
    return pl.pallas_call(
        body, grid=(1,), in_specs=[pl.BlockSpec((s, pw_), lambda i: (0, col_block)), blk, wspec, vec],
        out_specs=[blk, wspec, vec], out_shape=[_sds((s, pw_), _MXU), _sds((ng, pg, pg), F32), _sds((1, pw_), F32)],
        compiler_params=_cp(), name=name,
    )(dmix, pre, pool_w, pool_scale)


def _scan_chunks(xr_ref, xi_ref, sr_ref, si_ref, ar, ai, reverse):
    n, c = xr_ref.shape
    tt = min(SCAN_T, n)
    lw = min(SCAN_LANES, c)
    nchunk, ngroup = n // tt, tt // 8
    t8 = lax.broadcasted_iota(jnp.int32, (tt, lw), 0) % 8

    for l0 in range(0, c, lw):
        lanes = slice(l0, l0 + lw)
        a_r, a_i = ar[:, lanes], ai[:, lanes]

        def within8(vr, vi, a_r=a_r, a_i=a_i):
            rows = vr.shape[0]
            tq = t8[:rows, :]
            pr, pi = a_r, a_i
            k = 1
            while k < 8:
                if reverse:
                    hr = jnp.where(tq < 8 - k, pltpu.roll(vr, rows - k, 0), 0.0)
                    hi = jnp.where(tq < 8 - k, pltpu.roll(vi, rows - k, 0), 0.0)
                else:
                    hr = jnp.where(tq >= k, pltpu.roll(vr, k, 0), 0.0)
                    hi = jnp.where(tq >= k, pltpu.roll(vi, k, 0), 0.0)
                vr, vi = vr + pr * hr - pi * hi, vi + pr * hi + pi * hr
                pr, pi = pr * pr - pi * pi, 2.0 * pr * pi
                k *= 2
            return vr, vi

        at_edge = t8[:8, :] == (7 if reverse else 0)
        pw_r, pw_i = within8(jnp.where(at_edge, a_r, 0.0), jnp.where(at_edge, a_i, 0.0))
        last = 0 if reverse else 7

        def body(i, carry, lanes=lanes, within8=within8, pw_r=pw_r, pw_i=pw_i):
            cr, ci = carry
            ch = nchunk - 1 - i if reverse else i
            rows = pl.ds(pl.multiple_of(ch * tt, tt), tt)
            vr, vi = within8(xr_ref[rows, lanes], xi_ref[rows, lanes])
            out_r, out_i = [None] * ngroup, [None] * ngroup
            for g in (reversed(range(ngroup)) if reverse else range(ngroup)):
                br = vr[8 * g:8 * g + 8, :] + pw_r * cr - pw_i * ci
                bi = vi[8 * g:8 * g + 8, :] + pw_r * ci + pw_i * cr
                out_r[g], out_i[g] = br, bi
                cr, ci = br[last:last + 1, :], bi[last:last + 1, :]
            sr_ref[rows, lanes] = jnp.concatenate(out_r, 0)
            si_ref[rows, lanes] = jnp.concatenate(out_i, 0)
            return cr, ci

        lax.fori_loop(0, nchunk, body, (jnp.zeros((1, lw), F32), jnp.zeros((1, lw), F32)))


_GELU_K = math.sqrt(2.0 / math.pi)


def _gelu_grad(y):
    inner = _GELU_K * (y + 0.044715 * y * y * y)
    th = jnp.tanh(inner)
    return 0.5 * (1.0 + th) + 0.5 * y * (1.0 - th * th) * _GELU_K * (1.0 + 3.0 * 0.044715 * y * y)


def _ssm_fwd(name, h, u_block0, bdr, bdi, cdr, cdi, dvec, ar, ai):
    s = h.shape[0]
    nt, cw, lw = bdr.shape
    rc = min(256, s)

    def body(u_ref, bdr_ref, bdi_ref, cdr_ref, cdi_ref, d_ref, ar_ref, ai_ref, sr_ref, si_ref, y_ref, yg_ref):
        def mm_in(c, _):
            rows = pl.ds(pl.multiple_of(c * rc, rc), rc)
            ub = u_ref[rows, :].astype(_MXU)
            sr_ref[rows, :] = jnp.dot(ub, bdr_ref[...], preferred_element_type=F32)
            si_ref[rows, :] = jnp.dot(ub, bdi_ref[...], preferred_element_type=F32)
            return 0

        lax.fori_loop(0, s // rc, mm_in, 0)
        _scan_chunks(sr_ref, si_ref, sr_ref, si_ref, ar_ref[...], ai_ref[...], reverse=False)

        def mm_out(c, _):
            rows = pl.ds(pl.multiple_of(c * rc, rc), rc)
            y = (jnp.dot(sr_ref[rows, :].astype(_MXU), cdr_ref[...], preferred_element_type=F32)
                 - jnp.dot(si_ref[rows, :].astype(_MXU), cdi_ref[...], preferred_element_type=F32)
                 + d_ref[...] * u_ref[rows, :])
            y_ref[rows, :] = y
            yg_ref[rows, :] = jax.nn.gelu(y).astype(yg_ref.dtype)
            return 0

        lax.fori_loop(0, s // rc, mm_out, 0)

    st = pl.BlockSpec((s, lw), lambda j: (0, j))
    ch = pl.BlockSpec((s, cw), lambda j: (0, j))
    bspec = pl.BlockSpec((None, cw, lw), lambda j: (j, 0, 0))
    cspec = pl.BlockSpec((None, lw, cw), lambda j: (j, 0, 0))
    return pl.pallas_call(
        body, grid=(nt,),
        in_specs=[pl.BlockSpec((s, cw), lambda j: (0, u_block0 + j)), bspec, bspec, cspec, cspec,
                  pl.BlockSpec((1, cw), lambda j: (0, j)), pl.BlockSpec((1, lw), lambda j: (0, j)),
                  pl.BlockSpec((1, lw), lambda j: (0, j))],
        out_specs=[st, st, ch, ch],
        out_shape=[_sds((s, nt * lw), F32), _sds((s, nt * lw), F32), _sds((s, nt * cw), F32), _sds((s, nt * cw), _MXU)],
        compiler_params=_cp(), name=name,
    )(h, bdr, bdi, cdr, cdi, dvec, ar, ai)


def _ssm_bwd(name, dyg, ypre, h, u_block0, sr, si, bdr, bdi, cdr, cdi, dvec, ar, ai):
    s = h.shape[0]
    nt, cw, lw = bdr.shape
    rc = min(256, s)

    def body(dyg_ref, yp_ref, u_ref, sr_ref, si_ref, bdr_ref, bdi_ref, cdr_ref, cdi_ref, d_ref, ar_ref, ai_ref,
             du_ref, dd_ref, dcr_ref, dci_ref, dbr_ref, dbi_ref, dar_ref, dai_ref, lr_scr, li_scr, dy_scr):
        for ref in (dd_ref, dcr_ref, dci_ref, dbr_ref, dbi_ref, dar_ref, dai_ref):
            ref[...] = jnp.zeros_like(ref)

        def p1(c, _):
            rows = pl.ds(pl.multiple_of(c * rc, rc), rc)
            dy = dyg_ref[rows, :] * _gelu_grad(yp_ref[rows, :])
            dy_scr[rows, :] = dy
            dd_ref[...] += jnp.sum(dy * u_ref[rows, :], 0, keepdims=True)
            dyb = dy.astype(_MXU)
            lr_scr[rows, :] = lax.dot_general(dyb, cdr_ref[...], (NT, ((), ())), preferred_element_type=F32)
            li_scr[rows, :] = -lax.dot_general(dyb, cdi_ref[...], (NT, ((), ())), preferred_element_type=F32)
            dcr_ref[...] += lax.dot_general(sr_ref[rows, :].astype(_MXU), dyb, (TN, ((), ())), preferred_element_type=F32)
            dci_ref[...] -= lax.dot_general(si_ref[rows, :].astype(_MXU), dyb, (TN, ((), ())), preferred_element_type=F32)
            return 0

        lax.fori_loop(0, s // rc, p1, 0)
        _scan_chunks(lr_scr, li_scr, lr_scr, li_scr, ar_ref[...], -ai_ref[...], reverse=True)
        t = lax.broadcasted_iota(jnp.int32, (rc, lw), 0)

        def p2(c, _):
            r0 = pl.multiple_of(c * rc, rc)
            rows = pl.ds(r0, rc)
            before = pl.ds(pl.multiple_of(jnp.maximum(r0 - 8, 0), 8), 8)
            have = (c > 0).astype(F32)
            lr, li = lr_scr[rows, :], li_scr[rows, :]
            spr = jnp.where(t == 0, sr_ref[before, :][7:8, :] * have, pltpu.roll(sr_ref[rows, :], 1, 0))
            spi = jnp.where(t == 0, si_ref[before, :][7:8, :] * have, pltpu.roll(si_ref[rows, :], 1, 0))
            dar_ref[...] += jnp.sum(lr * spr + li * spi, 0, keepdims=True)
            dai_ref[...] += jnp.sum(li * spr - lr * spi, 0, keepdims=True)
            lrb, lib = lr.astype(_MXU), li.astype(_MXU)
            du = (dy_scr[rows, :] * d_ref[...]
                  + lax.dot_general(lrb, bdr_ref[...], (NT, ((), ())), preferred_element_type=F32)
                  + lax.dot_general(lib, bdi_ref[...], (NT, ((), ())), preferred_element_type=F32))
            du_ref[rows, :] = du.astype(du_ref.dtype)
            ub = u_ref[rows, :].astype(_MXU)
            dbr_ref[...] += lax.dot_general(ub, lrb, (TN, ((), ())), preferred_element_type=F32)
            dbi_ref[...] += lax.dot_general(ub, lib, (TN, ((), ())), preferred_element_type=F32)
            return 0

        lax.fori_loop(0, s // rc, p2, 0)

    st = pl.BlockSpec((s, lw), lambda j: (0, j))
    ch = pl.BlockSpec((s, cw), lambda j: (0, j))
    bspec = pl.BlockSpec((None, cw, lw), lambda j: (j, 0, 0))
    cspec = pl.BlockSpec((None, lw, cw), lambda j: (j, 0, 0))
    cvec = pl.BlockSpec((1, cw), lambda j: (0, j))
    svec = pl.BlockSpec((1, lw), lambda j: (0, j))
    return pl.pallas_call(
        body, grid=(nt,),
        in_specs=[ch, ch, pl.BlockSpec((s, cw), lambda j: (0, u_block0 + j)), st, st, bspec, bspec, cspec, cspec, cvec, svec, svec],
        out_specs=[ch, cvec, cspec, cspec, bspec, bspec, svec, svec],
        out_shape=[_sds((s, nt * cw), _MXU), _sds((1, nt * cw), F32), _sds((nt, lw, cw), F32), _sds((nt, lw, cw), F32),
                   _sds((nt, cw, lw), F32), _sds((nt, cw, lw), F32), _sds((1, nt * lw), F32), _sds((1, nt * lw), F32)],
        scratch_shapes=[pltpu.VMEM((s, lw), F32), pltpu.VMEM((s, lw), F32), pltpu.VMEM((s, cw), F32)],
        compiler_params=_cp(), name=name,
    )(dyg, ypre, h, sr, si, bdr, bdi, cdr, cdi, dvec, ar, ai)


def _glu_fwd(name, yg, gw):
    s, w = yg.shape
    tr = min(512, s)

    def body(y_ref, w_ref, o_ref, ab_ref):
        ab = jnp.dot(y_ref[...], w_ref[...], preferred_element_type=F32)
        ab_ref[...] = ab
        o_ref[...] = (ab[:, :w] * jax.nn.sigmoid(ab[:, w:])).astype(o_ref.dtype)

    return pl.pallas_call(
        body, grid=(s // tr,), in_specs=[pl.BlockSpec((tr, w), lambda i: (i, 0)), _resident((w, 2 * w), lambda i: (0, 0))],
        out_specs=[pl.BlockSpec((tr, w), lambda i: (i, 0)), pl.BlockSpec((tr, 2 * w), lambda i: (i, 0))],
        out_shape=[_sds((s, w), _MXU), _sds((s, 2 * w), F32)], compiler_params=_cp(), name=name,
    )(yg, gw)


def _glu_bwd(name, dmix, col_block, ab, gw):
    s = ab.shape[0]
    w = ab.shape[1] // 2
    tr = min(512, s)

    def body(do_ref, ab_ref, w_ref, dab_ref, dy_ref):
        do = do_ref[...]
        a, b = ab_ref[:, :w], ab_ref[:, w:]
        sg = jax.nn.sigmoid(b)
        da = (do * sg).astype(_MXU)
        db = (do * a * sg * (1.0 - sg)).astype(_MXU)
        dab_ref[:, :w] = da
        dab_ref[:, w:] = db
        dy_ref[...] = (lax.dot_general(da, w_ref[:, :w], (NT, ((), ())), preferred_element_type=F32)
                       + lax.dot_general(db, w_ref[:, w:], (NT, ((), ())), preferred_element_type=F32))

    return pl.pallas_call(
        body, grid=(s // tr,),
        in_specs=[pl.BlockSpec((tr, w), lambda i: (i, col_block)), pl.BlockSpec((tr, 2 * w), lambda i: (i, 0)),
                  _resident((w, 2 * w), lambda i: (0, 0))],
        out_specs=[pl.BlockSpec((tr, 2 * w), lambda i: (i, 0)), pl.BlockSpec((tr, w), lambda i: (i, 0))],
        out_shape=[_sds((s, 2 * w), _MXU), _sds((s, w), F32)], compiler_params=_cp(), name=name,
    )(dmix, ab, gw)


CONV_ROWS = 64


def _sigmoid(x):
    return 0.5 * jnp.tanh(0.5 * x) + 0.5


def _conv_chunk(ref, w_ref, b_ref, c, tt):
    r0 = pl.multiple_of(c * tt, tt)
    before = ref[pl.ds(pl.multiple_of(jnp.maximum(r0 - 8, 0), 8), 8), :]
    before = jnp.where(c > 0, before, 0.0)
    main = ref[pl.ds(r0, tt), :]
    ext = jnp.concatenate([before, main], 0)
    d1 = pltpu.roll(ext, 1, 0)[8:, :]
    d2 = pltpu.roll(ext, 2, 0)[8:, :]
    hc = b_ref[...] + d2 * w_ref[0:1, :]
    hc = hc + d1 * w_ref[1:2, :]
    return hc + main * w_ref[2:3, :], main, d1, d2


def _conv_act_fwd(name, hu, cw, cb):
    s, f2 = hu.shape
    f = f2 // 2
    tw = _col_tile(f, 256)
    nt = f // tw

    tt = min(CONV_ROWS, s)

    def body(v_ref, g_ref, wv_ref, wg_ref, bv_ref, bg_ref, act_ref):
        def chunk(c, _):
            val = _conv_chunk(v_ref, wv_ref, bv_ref, c, tt)[0]
            gate = _conv_chunk(g_ref, wg_ref, bg_ref, c, tt)[0]
            act_ref[pl.ds(pl.multiple_of(c * tt, tt), tt), :] = (gate * _sigmoid(gate) * val).astype(act_ref.dtype)
            return 0

        lax.fori_loop(0, s // tt, chunk, 0)

    cv = lambda rows: pl.BlockSpec((rows, tw), lambda i: (0, i))
    cg = lambda rows: pl.BlockSpec((rows, tw), lambda i: (0, nt + i))
    return pl.pallas_call(
        body, grid=(nt,), in_specs=[cv(s), cg(s), cv(CONV_WIDTH), cg(CONV_WIDTH), cv(1), cg(1)],
        out_specs=cv(s), out_shape=_sds((s, f), _MXU), compiler_params=_cp(), name=name,
    )(hu, hu, cw, cw, cb, cb)


def _conv_act_bwd(name, dact, hu, cw, cb):
    s, f2 = hu.shape
    f = f2 // 2
    tw = _col_tile(f, 256)
    nt = f // tw

    tt = min(CONV_ROWS, s)
    nchunk = s // tt

    def body(da_ref, v_ref, g_ref, wv_ref, wg_ref, bv_ref, bg_ref, dh_ref, dwv_ref, dwg_ref, dbv_ref, dbg_ref):
        def chunk(i, carry):
            c = nchunk - 1 - i
            rows = pl.ds(pl.multiple_of(c * tt, tt), tt)
            val, hv, hv1, hv2 = _conv_chunk(v_ref, wv_ref, bv_ref, c, tt)
            gate, hg, hg1, hg2 = _conv_chunk(g_ref, wg_ref, bg_ref, c, tt)
            sg = _sigmoid(gate)
            da = da_ref[rows, :]
            dval = da * (gate * sg)
            dgate = da * val * sg * (1.0 + gate * (1.0 - sg))
            out = []
            for part, dhc, taps, w_ref, (after, acc) in ((0, dval, (hv2, hv1, hv), wv_ref, carry[0]),
                                                        (1, dgate, (hg2, hg1, hg), wg_ref, carry[1])):
                ext = jnp.concatenate([dhc, after], 0)
                u1 = pltpu.roll(ext, tt + 8 - 1, 0)[:tt, :]
                u2 = pltpu.roll(ext, tt + 8 - 2, 0)[:tt, :]
                dh = dhc * w_ref[2:3, :] + u1 * w_ref[1:2, :] + u2 * w_ref[0:1, :]
                dh_ref[part, rows, :] = dh.astype(dh_ref.dtype)
                sums = [jnp.sum((dhc * tap).reshape(tt // 8, 8, tw), 0) for tap in taps]
                sums.append(jnp.sum(dhc.reshape(tt // 8, 8, tw), 0))
                out.append((dhc[0:8, :], tuple(a + b for a, b in zip(acc, sums))))
            return tuple(out)

        zero = (jnp.zeros((8, tw), F32), tuple(jnp.zeros((8, tw), F32) for _ in range(CONV_WIDTH + 1)))
        (_, acc_v), (_, acc_g) = lax.fori_loop(0, nchunk, chunk, (zero, zero))
        for acc, dw_ref, db_ref in ((acc_v, dwv_ref, dbv_ref), (acc_g, dwg_ref, dbg_ref)):
            for tap in range(CONV_WIDTH):
                dw_ref[tap:tap + 1, :] = jnp.sum(acc[tap], 0, keepdims=True)
            db_ref[...] = jnp.sum(acc[CONV_WIDTH], 0, keepdims=True)

    cv = lambda rows: pl.BlockSpec((rows, tw), lambda i: (0, i))
    cg = lambda rows: pl.BlockSpec((rows, tw), lambda i: (0, nt + i))
    both = pl.BlockSpec((2, s, tw), lambda i: (0, 0, i))
    dh, dwv, dwg, dbv, dbg = pl.pallas_call(
        body, grid=(nt,), in_specs=[cv(s), cv(s), cg(s), cv(CONV_WIDTH), cg(CONV_WIDTH), cv(1), cg(1)],
        out_specs=[both, cv(CONV_WIDTH), cv(CONV_WIDTH), cv(1), cv(1)],
        out_shape=[_sds((2, s, f), _MXU), _sds((CONV_WIDTH, f), F32), _sds((CONV_WIDTH, f), F32),
                   _sds((1, f), F32), _sds((1, f), F32)],
        compiler_params=_cp(), name=name,
    )(dact, hu, hu, cw, cw, cb, cb)
    return dh, jnp.concatenate([dwv, dwg], 1), jnp.concatenate([dbv, dbg], 1)


ELEM_BLOCK = 512 * 1024


def _elem_tiles(r, c, budget=ELEM_BLOCK):
    rows = [t for t in range(8, r + 1, 8) if r % t == 0] or [r]
    cols = [t for t in range(V7X_LANES, c + 1, V7X_LANES) if c % t == 0] or [c]
    fits = [(tr * tc, tc, tr) for tr in rows for tc in cols if tr * tc <= budget]
    if not fits:
        return min(rows), min(cols)
    _, tc, tr = max(fits)
    return tr, tc


def _sum_parts(name, parts):
    n, r, c = parts.shape
    tr, tc = _elem_tiles(r, c, ELEM_BLOCK // n)

    def body(p_ref, g_ref):
        g = p_ref[0]
        for i in range(1, n):
            g = g + p_ref[i]
        g_ref[...] = g

    return pl.pallas_call(
        body, grid=(r // tr, c // tc), in_specs=[pl.BlockSpec((n, tr, tc), lambda i, j: (0, i, j))],
        out_specs=pl.BlockSpec((tr, tc), lambda i, j: (i, j)), out_shape=_sds((r, c), F32), compiler_params=_cp(), name=name,
    )(parts)


def _adamw_2d(name, g, w, m, v):
    r, c = w.shape
    tc = c if c % V7X_LANES else _col_tile(c, 2048)
    rows = [t for t in range(8, r + 1, 8) if r % t == 0 and t * max(tc, V7X_LANES) <= ELEM_BLOCK // 4] or [r]
    tr = max(rows)
    c1 = 1.0 - ADAM_B1 ** ADAM_STEP
    c2 = 1.0 - ADAM_B2 ** ADAM_STEP

    def body(g_ref, w_ref, m_ref, v_ref, d_ref, nm_ref, nv_ref):
        g = g_ref[...]
        nm = ADAM_B1 * m_ref[...] + (1.0 - ADAM_B1) * g
        nv = ADAM_B2 * v_ref[...] + (1.0 - ADAM_B2) * (g * g)
        m_hat = nm / c1
        v_hat = nv / c2
        nm_ref[...] = nm
        nv_ref[...] = nv
        d_ref[...] = -ADAM_LR * (m_hat / (jnp.sqrt(v_hat) + ADAM_EPS) + ADAM_WD * w_ref[...])

    blk = pl.BlockSpec((tr, tc), lambda i, j: (i, j))
    out = _sds((r, c), F32)
    return pl.pallas_call(
        body, grid=(r // tr, c // tc), in_specs=[blk] * 4, out_specs=[blk] * 3, out_shape=[out] * 3,
        compiler_params=_cp(), name=name,
    )(g, w, m, v)


def _pair_sum(name, mine, theirs, c_idx):
    _, _, r, c = mine.shape
    tr, tc = _elem_tiles(r, c)

    def body(c_ref, a_ref, b_ref, o_ref):
        o_ref[...] = (a_ref[...].astype(F32) + b_ref[...].astype(F32)).astype(o_ref.dtype)

    return pl.pallas_call(
        body,
        grid_spec=pltpu.PrefetchScalarGridSpec(
            num_scalar_prefetch=1, grid=(4, r // tr, c // tc),
            in_specs=[pl.BlockSpec((None, None, tr, tc), lambda p, i, j, cref: (p, cref[0], i, j)),
                      pl.BlockSpec((None, tr, tc), lambda p, i, j, cref: (p, i, j))],
            out_specs=pl.BlockSpec((None, tr, tc), lambda p, i, j, cref: (p, i, j))),
        out_shape=_sds((4, r, c), _WIRE), compiler_params=_cp(), name=name,
    )(c_idx, mine, theirs)


def _place():
    return lax.axis_index("x"), lax.axis_index("y"), lax.axis_index("c")


def _all_gather(name, xs):
    n = len(xs)

    def body(*refs):
        x_refs, o_refs = refs[:n], refs[n:2 * n]
        send_sems, recv_sems, local_sems = refs[2 * n:]
        x, y, c = _place()
        me, sibling = (x, y, c), (x, y, 1 - c)
        chips = [(1 - x, y), (x, 1 - y), (1 - x, 1 - y)]

        def copy(a, k, block, to, src=None):
            px, py, pc = block
            rows = o_refs[a].at[4 * px + 2 * py + pc]
            return pltpu.make_async_remote_copy(
                src_ref=rows if src is None else src, dst_ref=rows, send_sem=send_sems.at[a, k], recv_sem=recv_sems.at[a, k],
                device_id=to, device_id_type=MESH)

        sent = []
        mine = []
        for a in range(n):
            mx, my, mc = me
            cp = pltpu.make_async_copy(x_refs[a], o_refs[a].at[4 * mx + 2 * my + mc], local_sems.at[a])
            cp.start()
            mine.append(cp)
            first = [copy(a, 0, me, sibling, src=x_refs[a])]
            first += [copy(a, 1 + j, me, (*chip, c), src=x_refs[a]) for j, chip in enumerate(chips)]
            for cp in first:
                cp.start()
            sent += first
        for a in range(n):
            for j, chip in enumerate(chips):
                copy(a, 1 + j, (*chip, c), me).wait_recv()
                fwd = copy(a, 4 + j, (*chip, c), sibling)
                fwd.start()
                sent.append(fwd)
        for a in range(n):
            copy(a, 0, sibling, me).wait_recv()
            for j, chip in enumerate(chips):
                copy(a, 4 + j, (*chip, 1 - c), me).wait_recv()
        for cp in sent:
            cp.wait_send()
        for cp in mine:
            cp.wait()

    return pl.pallas_call(
        body, in_specs=[ANY] * n, out_specs=[ANY] * n,
        out_shape=[_sds((N_DEV,) + a.shape, a.dtype) for a in xs],
        scratch_shapes=[pltpu.SemaphoreType.DMA((n, 7)), pltpu.SemaphoreType.DMA((n, 7)), pltpu.SemaphoreType.DMA((n,))],
        name=name,
    )(*xs)


HBM = pl.BlockSpec(memory_space=pltpu.HBM)
SEM = pl.BlockSpec(memory_space=pltpu.SEMAPHORE)
DATAFLOW = pltpu.SideEffectType.DATAFLOW_SIDE_EFFECTING


def _in_hbm(a):
    return pltpu.with_memory_space_constraint(a, pltpu.HBM)


def _split_copy_start(name, srcs, lands, copies, deps):
    n, nd = len(srcs), len(deps)
    per = len(copies([None] * n, [None] * n, probe=True)) // n

    def body(*refs):
        s_refs, l_refs = refs[:n], refs[n:2 * n]
        send_sems, recv_sems = refs[2 * n + nd], refs[2 * n + nd + 1]
        token = refs[-1]
        for a, k, src, dst, to in copies(s_refs, l_refs):
            pltpu.make_async_remote_copy(src_ref=src, dst_ref=dst, send_sem=send_sems.at[a * per + k],
                                         recv_sem=recv_sems.at[a * per + k], device_id=to, device_id_type=MESH).start()
        token[...] = jnp.zeros_like(token)

    both = list(srcs) + list(lands)
    outs = pl.pallas_call(
        body, name=name,
        out_shape=(pltpu.SemaphoreType.DMA((n * per,)), pltpu.SemaphoreType.DMA((n * per,)),
                   *[pltpu.HBM(a.shape, a.dtype) for a in both], _sds((8, V7X_LANES), F32)),
        in_specs=[HBM] * (2 * n) + [ANY] * nd,
        out_specs=(SEM, SEM, *[HBM] * (2 * n), pl.BlockSpec(memory_space=pltpu.VMEM)),
        input_output_aliases={i: 2 + i for i in range(2 * n)},
        compiler_params=pltpu.CompilerParams(has_side_effects=DATAFLOW),
    )(*[_in_hbm(a) for a in both], *deps)
    return outs[0], outs[1], list(outs[2:2 + n]), list(outs[2 + n:2 + 2 * n]), outs[-1]


def _split_copy_wait(name, send_sems, recv_sems, srcs, lands, arrivals, after):
    n = len(srcs)
    per = len(arrivals([None] * n, [None] * n, probe=True)) // n

    def body(*refs):
        s_refs, l_refs = refs[:n], refs[n:2 * n]
        send_sems_, recv_sems_ = refs[2 * n], refs[2 * n + 1]
        for a, k, src, dst, frm in arrivals(s_refs, l_refs):
            cp = pltpu.make_async_remote_copy(src_ref=src, dst_ref=dst, send_sem=send_sems_.at[a * per + k],
                                              recv_sem=recv_sems_.at[a * per + k], device_id=frm, device_id_type=MESH)
            cp.wait_send()
            cp.wait_recv()

    both = list(srcs) + list(lands)
    outs = pl.pallas_call(
        body, name=name, out_shape=tuple(pltpu.HBM(a.shape, a.dtype) for a in both),
        in_specs=[HBM] * (2 * n) + [SEM, SEM, ANY], out_specs=tuple([HBM] * (2 * n)),
        input_output_aliases={i: i for i in range(2 * n)},
        compiler_params=pltpu.CompilerParams(has_side_effects=DATAFLOW),
    )(*both, send_sems, recv_sems, after)
    return list(outs[:n]), list(outs[n:])


def _gather_copies(arriving):
    def copies(s_refs, l_refs, probe=False):
        if probe:
            return [None] * (4 * len(s_refs))
        x, y, c = _place()
        out = []
        for a in range(len(s_refs)):
            for k, (px, py, pc) in enumerate([(x, y, 1 - c), (1 - x, y, c), (x, 1 - y, c), (1 - x, 1 - y, c)]):
                slot = 4 * px + 2 * py + pc if arriving else 4 * x + 2 * y + c
                out.append((a, k, s_refs[a], l_refs[a].at[slot], (px, py, pc)))
        return out
    return copies


def _chip_copies(arriving):
    def copies(s_refs, l_refs, probe=False):
        if probe:
            return [None] * (3 * len(s_refs))
        x, y, c = _place()
        out = []
        for a in range(len(s_refs)):
            for j, (px, py) in enumerate([(1 - x, y), (x, 1 - y), (1 - x, 1 - y)]):
                src = s_refs[a].at[2 * x + y] if arriving else s_refs[a].at[2 * px + py]
                out.append((a, j, src, l_refs[a].at[j], (px, py, c)))
        return out
    return copies


def _sibling_copies(s_refs, l_refs, probe=False):
    if probe:
        return [None] * (4 * len(s_refs))
    x, y, c = _place()
    return [(a, p, s_refs[a].at[p, 1 - c], l_refs[a].at[p], (x, y, 1 - c)) for a in range(len(s_refs)) for p in range(4)]


def _sibling_begin(name, by_owner, deps):
    lands = [lax.empty((4,) + a.shape[2:], a.dtype) for a in by_owner]
    return _split_copy_start(name + "_start", by_owner, lands, _sibling_copies, deps)


def _sibling_end(name, handle, after):
    send_sems, recv_sems, srcs, lands, _ = handle
    return _split_copy_wait(name + "_wait", send_sems, recv_sems, srcs, lands, _sibling_copies, after)


def _gather_begin(name, shards, deps):
    x, y, c = _place()
    lands = [lax.dynamic_update_slice_in_dim(lax.empty((N_DEV,) + a.shape, a.dtype), a[None], 4 * x + 2 * y + c, 0)
             for a in shards]
    return _split_copy_start(name + "_start", shards, lands, _gather_copies(False), deps)


def _gather_end(name, handle, after):
    send_sems, recv_sems, srcs, lands, _ = handle
    _, lands = _split_copy_wait(name + "_wait", send_sems, recv_sems, srcs, lands, _gather_copies(True), after)
    return _gather_forward(name + "_forward", lands)


def _forward_copies(arriving):
    def copies(s_refs, l_refs, probe=False):
        if probe:
            return [None] * (3 * len(l_refs))
        x, y, c = _place()
        out = []
        for a in range(len(l_refs)):
            for j, (px, py) in enumerate([(1 - x, y), (x, 1 - y), (1 - x, 1 - y)]):
                mine, theirs = l_refs[a].at[4 * px + 2 * py + c], l_refs[a].at[4 * px + 2 * py + 1 - c]
                out.append((a, j, mine, theirs if arriving else mine, (x, y, 1 - c)))
        return out
    return copies


def _gather_arrived(name, handle, after):
    send_sems, recv_sems, srcs, lands, _ = handle
    srcs, lands = _split_copy_wait(name + "_wait", send_sems, recv_sems, srcs, lands, _gather_copies(True), after)
    return _split_copy_start(name + "_forward_start", srcs, lands, _forward_copies(False), [])


def _gather_done(name, handle, after):
    send_sems, recv_sems, srcs, lands, _ = handle
    return _split_copy_wait(name + "_forward_wait", send_sems, recv_sems, srcs, lands, _forward_copies(True), after)[1]


def _gather_forward(name, lands):
    n = len(lands)

    def body(*refs):
        o_refs = refs[n:2 * n]
        send_sems, recv_sems = refs[2 * n:]
        x, y, c = _place()
        sibling = (x, y, 1 - c)
        chips = [(1 - x, y), (x, 1 - y), (1 - x, 1 - y)]
        sent = []
        for a in range(n):
            for j, (px, py) in enumerate(chips):
                rows = o_refs[a].at[4 * px + 2 * py + c]
                cp = pltpu.make_async_remote_copy(src_ref=rows, dst_ref=rows, send_sem=send_sems.at[a, j],
                                                  recv_sem=recv_sems.at[a, j], device_id=sibling, device_id_type=MESH)
                cp.start()
                sent.append(cp)
        for a in range(n):
            for j, (px, py) in enumerate(chips):
                rows = o_refs[a].at[4 * px + 2 * py + 1 - c]
                pltpu.make_async_remote_copy(src_ref=rows, dst_ref=rows, send_sem=send_sems.at[a, j],
                                             recv_sem=recv_sems.at[a, j], device_id=sibling, device_id_type=MESH).wait_recv()
        for cp in sent:
            cp.wait_send()

    return pl.pallas_call(
        body, in_specs=[ANY] * n, out_specs=[ANY] * n, out_shape=[_sds(a.shape, a.dtype) for a in lands],
        input_output_aliases={i: i for i in range(n)},
        scratch_shapes=[pltpu.SemaphoreType.DMA((n, 3)), pltpu.SemaphoreType.DMA((n, 3))], name=name,
    )(*lands)


def _chips_begin(name, pairs, deps):
    lands = [lax.empty((3,) + a.shape[1:], a.dtype) for a in pairs]
    return _split_copy_start(name + "_start", pairs, lands, _chip_copies(False), deps)


def _chips_end(name, handle, after):
    send_sems, recv_sems, srcs, lands, _ = handle
    return _split_copy_wait(name + "_wait", send_sems, recv_sems, srcs, lands, _chip_copies(True), after)


def _adamw_layer(name, l, own, lands, w, m, v, prev):
    nl, ng, r, c = w.shape
    tr, tc = _elem_tiles(r, c)
    c1 = 1.0 - ADAM_B1 ** ADAM_STEP
    c2 = 1.0 - ADAM_B2 ** ADAM_STEP

    def body(own_ref, lands_ref, w_ref, m_ref, v_ref, *rest):
        g_ref, d_ref, nm_ref, nv_ref = rest[-4:]
        g = own_ref[...].astype(F32) + lands_ref[0].astype(F32) + lands_ref[1].astype(F32) + lands_ref[2].astype(F32)
        nm = ADAM_B1 * m_ref[...] + (1.0 - ADAM_B1) * g
        nv = ADAM_B2 * v_ref[...] + (1.0 - ADAM_B2) * (g * g)
        m_hat = nm / c1
        v_hat = nv / c2
        g_ref[...] = g
        nm_ref[...] = nm
        nv_ref[...] = nv
        d_ref[...] = -ADAM_LR * (m_hat / (jnp.sqrt(v_hat) + ADAM_EPS) + ADAM_WD * w_ref[...])

    lay = pl.BlockSpec((None, None, tr, tc), lambda g, i, j: (l, g, i, j))
    out = _sds((nl, ng, r, c), F32)
    prev = [] if prev is None else list(prev)
    return pl.pallas_call(
        body, grid=(ng, r // tr, c // tc),
        in_specs=[pl.BlockSpec((None, tr, tc), lambda g, i, j: (g, i, j)),
                  pl.BlockSpec((3, None, tr, tc), lambda g, i, j: (0, g, i, j)), lay, lay, lay] + [ANY] * len(prev),
        out_specs=[lay] * 4, out_shape=[out] * 4, input_output_aliases={5 + i: i for i in range(len(prev))},
        compiler_params=_cp(), name=name,
    )(own, lands, w, m, v, *prev)


def _pad_pairs(a, axis, half, half_pad):
    shp = a.shape
    a = a.reshape(shp[:axis] + (2, half) + shp[axis + 1:])
    pad = [(0, 0)] * a.ndim
    pad[axis + 1] = (0, half_pad - half)
    a = jnp.pad(a, pad)
    return a.reshape(shp[:axis] + (2 * half_pad,) + shp[axis + 1:])


def _unpad_pairs(a, axis, half, half_pad):
    shp = a.shape
    a = a.reshape(shp[:axis] + (2, half_pad) + shp[axis + 1:])
    a = lax.slice_in_dim(a, 0, half, axis=axis + 1)
    return a.reshape(shp[:axis] + (2 * half,) + shp[axis + 1:])


def _blockdiag(w, nt):
    g, a, b = w.shape
    gl = g // nt
    e = jnp.eye(gl, dtype=w.dtype).reshape(1, gl, 1, gl, 1)
    return (w.reshape(nt, gl, a, 1, b) * e).reshape(nt, gl * a, gl * b)


def _diagblocks(m, g, a, b):
    nt = m.shape[0]
    gl = g // nt
    d = jnp.diagonal(m.reshape(nt, gl, a, gl, b), axis1=1, axis2=3)
    return jnp.moveaxis(d, -1, 1).reshape(g, a, b)


_PIECE = 8 * V7X_LANES


def _pack(pieces, row_multiple=512):
    rows = []
    for p in pieces:
        flat = p.reshape(-1).astype(F32)
        rows.append(jnp.pad(flat, (0, -flat.shape[0] % _PIECE)).reshape(-1, V7X_LANES))
    fill = -sum(r.shape[0] for r in rows) % row_multiple
    if fill:
        rows.append(jnp.zeros((fill, V7X_LANES), F32))
    return jnp.concatenate(rows, 0)


def _unpack(packed, shapes):
    lead = packed.shape[:-2]
    out, off = [], 0
    for shp in shapes:
        n = math.prod(shp)
        r = -(-n // _PIECE) * 8
        piece = packed[..., off:off + r, :].reshape(lead + (r * V7X_LANES,))[..., :n]
        out.append(piece.reshape(lead + tuple(shp)))
        off += r
    return out


def _ssm_discretise(lam_re, lam_im, log_dt, b_re, b_im):
    dt = jnp.exp(log_dt)[..., None]
    mag = jnp.exp(lam_re * dt)
    ab_re, ab_im = mag * jnp.cos(lam_im * dt), mag * jnp.sin(lam_im * dt)
    nr, ni = ab_re - 1.0, ab_im
    den = lam_re * lam_re + lam_im * lam_im
    zr = (nr * lam_re + ni * lam_im) / den
    zi = (ni * lam_re - nr * lam_im) / den
    bbr = zr[..., None] * b_re - zi[..., None] * b_im
    bbi = zr[..., None] * b_im + zi[..., None] * b_re
    return ab_re, ab_im, bbr, bbi


def _rope_tables(s):
    half = HEAD_DIM // 2
    inv = ROPE_THETA ** (-jnp.arange(half, dtype=F32) / half)
    ang = jnp.arange(s).astype(F32)[:, None] * inv[None, :]
    cos, sin = jnp.cos(ang), jnp.sin(ang)
    reps = V7X_LANES // HEAD_DIM
    return jnp.tile(jnp.concatenate([cos, cos], -1), (1, reps)), jnp.tile(jnp.concatenate([-sin, sin], -1), (1, reps))


_SMALL = ("attn_sinks", "pool_w", "pool_scale", "ssm_lam_re", "ssm_lam_im", "ssm_log_dt", "ssm_b_re", "ssm_b_im",
          "ssm_c_re", "ssm_c_im", "ssm_d", "ln1_g", "ln1_b", "ffn_conv_b", "ln2_g", "ln2_b")
_BIG = ("w_in", "ssm_glu_w", "w_out", "ffn_w_up", "ffn_w_down")
_RAW = ("attn_sinks", "pool_w", "pool_scale", "ssm_lam_re", "ssm_lam_im", "ssm_b_re", "ssm_b_im", "ssm_c_re", "ssm_c_im",
        "ssm_d", "ln1_g", "ln1_b", "ffn_conv_b", "ln2_g", "ln2_b", "ffn_conv_w")
_ORDER = ("w_in", "attn_sinks", "pool_w", "pool_scale", "ssm_lam_re", "ssm_lam_im", "ssm_log_dt", "ssm_b_re", "ssm_b_im",
          "ssm_c_re", "ssm_c_im", "ssm_d", "ssm_glu_w", "w_out", "ln1_g", "ln1_b", "ffn_w_up", "ffn_conv_w", "ffn_conv_b",
          "ffn_w_down", "ln2_g", "ln2_b")


def kernel(x, w_in, attn_sinks, pool_w, pool_scale, ssm_lam_re, ssm_lam_im, ssm_log_dt, ssm_b_re, ssm_b_im, ssm_c_re, ssm_c_im, ssm_d, ssm_glu_w, w_out, ln1_g, ln1_b, ffn_w_up, ffn_conv_w, ffn_conv_b, ffn_w_down, ln2_g, ln2_b, loss_target, m_w_in, m_attn_sinks, m_pool_w, m_pool_scale, m_ssm_lam_re, m_ssm_lam_im, m_ssm_log_dt, m_ssm_b_re, m_ssm_b_im, m_ssm_c_re, m_ssm_c_im, m_ssm_d, m_ssm_glu_w, m_w_out, m_ln1_g, m_ln1_b, m_ffn_w_up, m_ffn_conv_w, m_ffn_conv_b, m_ffn_w_down, m_ln2_g, m_ln2_b, v_w_in, v_attn_sinks, v_pool_w, v_pool_scale, v_ssm_lam_re, v_ssm_lam_im, v_ssm_log_dt, v_ssm_b_re, v_ssm_b_im, v_ssm_c_re, v_ssm_c_im, v_ssm_d, v_ssm_glu_w, v_w_out, v_ln1_g, v_ln1_b, v_ffn_w_up, v_ffn_conv_w, v_ffn_conv_b, v_ffn_w_down, v_ln2_g, v_ln2_b):
    W = dict(w_in=w_in, attn_sinks=attn_sinks, pool_w=pool_w, pool_scale=pool_scale, ssm_lam_re=ssm_lam_re, ssm_lam_im=ssm_lam_im, ssm_log_dt=ssm_log_dt, ssm_b_re=ssm_b_re, ssm_b_im=ssm_b_im, ssm_c_re=ssm_c_re, ssm_c_im=ssm_c_im, ssm_d=ssm_d, ssm_glu_w=ssm_glu_w, w_out=w_out, ln1_g=ln1_g, ln1_b=ln1_b, ffn_w_up=ffn_w_up, ffn_conv_w=ffn_conv_w, ffn_conv_b=ffn_conv_b, ffn_w_down=ffn_w_down, ln2_g=ln2_g, ln2_b=ln2_b)
    M = dict(w_in=m_w_in, attn_sinks=m_attn_sinks, pool_w=m_pool_w, pool_scale=m_pool_scale, ssm_lam_re=m_ssm_lam_re, ssm_lam_im=m_ssm_lam_im, ssm_log_dt=m_ssm_log_dt, ssm_b_re=m_ssm_b_re, ssm_b_im=m_ssm_b_im, ssm_c_re=m_ssm_c_re, ssm_c_im=m_ssm_c_im, ssm_d=m_ssm_d, ssm_glu_w=m_ssm_glu_w, w_out=m_w_out, ln1_g=m_ln1_g, ln1_b=m_ln1_b, ffn_w_up=m_ffn_w_up, ffn_conv_w=m_ffn_conv_w, ffn_conv_b=m_ffn_conv_b, ffn_w_down=m_ffn_w_down, ln2_g=m_ln2_g, ln2_b=m_ln2_b)
    V = dict(w_in=v_w_in, attn_sinks=v_attn_sinks, pool_w=v_pool_w, pool_scale=v_pool_scale, ssm_lam_re=v_ssm_lam_re, ssm_lam_im=v_ssm_lam_im, ssm_log_dt=v_ssm_log_dt, ssm_b_re=v_ssm_b_re, ssm_b_im=v_ssm_b_im, ssm_c_re=v_ssm_c_re, ssm_c_im=v_ssm_c_im, ssm_d=v_ssm_d, ssm_glu_w=v_ssm_glu_w, w_out=v_w_out, ln1_g=v_ln1_g, ln1_b=v_ln1_b, ffn_w_up=v_ffn_w_up, ffn_conv_w=v_ffn_conv_w, ffn_conv_b=v_ffn_conv_b, ffn_w_down=v_ffn_w_down, ln2_g=v_ln2_g, ln2_b=v_ln2_b)

    depth = w_in.shape[0]
    s, d = x.shape[1], x.shape[2]
    alpha = (2 * depth) ** 0.25
    attn_w = d // 2
    kv_w = attn_w // GQA
    nkv = kv_w // HEAD_DIM
    pool_wd = d // 4
    ssm_wd = d // 4
    n_groups = ssm_wd // SSM_GROUP
    state_w = n_groups * SSM_STATE
    nt_ssm = max(1, state_w // 512)
    o_k, o_v, o_p, o_s = attn_w, attn_w + kv_w, attn_w + 2 * kv_w, attn_w + 2 * kv_w + pool_wd
    in_w = o_s + ssm_wd
    half = ffn_w_down.shape[1]
    half_pad = -(-half // 64) * 64
    ffp = 4 * 2 * half_pad
    xi, yi, ci = _place()
    me = 4 * xi + 2 * yi + ci
    c_idx = jnp.reshape(ci, (1,)).astype(jnp.int32)

    cos_t, sin_t = _rope_tables(s)

    def layer_shards(l):
        return [
            jnp.transpose(w_in[l]).astype(_WIRE), ssm_glu_w[l].astype(_WIRE), w_out[l].astype(_WIRE),
            _pad_pairs(jnp.transpose(ffn_w_up[l]).astype(_WIRE), 0, half, half_pad),
            jnp.pad(ffn_w_down[l].astype(_WIRE), ((0, half_pad - half), (0, 0))),
        ]

    (g_cw,) = _all_gather("gather_conv_w", [_pad_pairs(ffn_conv_w, 2, half, half_pad)])

    def in_weights(l, gathered):
        (g_in,) = gathered
        return dict(
            win_t=g_in.reshape(in_w, d),
            cw=jnp.transpose(g_cw[:, l], (1, 0, 2)).reshape(CONV_WIDTH, 2 * ffp),
            cb=_pad_pairs(ffn_conv_b[l].reshape(N_DEV, 2 * half), 1, half, half_pad).reshape(1, 2 * ffp),
        )

    def out_weights(gathered):
        g_glu, g_out = gathered
        return dict(glu=jnp.transpose(g_glu, (1, 0, 2)).reshape(ssm_wd, 2 * ssm_wd), wout=g_out.reshape(d, d))

    def mixer_weights(l, gathered):
        return {**in_weights(l, gathered[:1]), **out_weights(gathered[1:])}

    def up_weights(gathered):
        (g_up,) = gathered
        return dict(wup_t=g_up.reshape(2 * ffp, d))

    def down_weights(gathered):
        (g_down,) = gathered
        return dict(wdown=g_down.reshape(ffp, d))

    def ffn_weights(gathered):
        return {**up_weights(gathered[:1]), **down_weights(gathered[1:])}

    shards = [layer_shards(l) for l in range(depth)]
    full = [None] * depth
    gathers = {}

    def begin_gather(l, part, deps):
        arrays = shards[l][{"in": slice(0, 1), "out": slice(1, 3), "mixer": slice(0, 3), "up": slice(3, 4),
                            "down": slice(4, 5), "ffn": slice(3, 5)}[part]]
        gathers[l, part] = ("sent", _gather_begin(f"gather_{part}_weights_{l}", arrays, deps))
        return gathers[l, part][1][-1]

    def arrive_gather(l, part, after):
        gathers[l, part] = ("forwarding", _gather_arrived(f"gather_{part}_weights_{l}", gathers[l, part][1], after))
        return gathers[l, part][1][-1]

    def end_gather(l, part, after):
        stage, handle = gathers.pop((l, part))
        return (_gather_done if stage == "forwarding" else _gather_end)(f"gather_{part}_weights_{l}", handle, after)

    issued = begin_gather(0, "in", [g_cw])
    for part in ("out", "up", "down"):
        issued = begin_gather(0, part, [issued])
    full[0] = in_weights(0, end_gather(0, "in", g_cw))

    ssm_params = (ssm_lam_re, ssm_lam_im, ssm_log_dt, ssm_b_re, ssm_b_im)
    ab_re_all, ab_im_all, bbr_all, bbi_all = _ssm_discretise(*ssm_params)

    def ssm_maps(w):
        return jax.vmap(lambda t: _blockdiag(jnp.transpose(t, (0, 2, 1)), nt_ssm))(w).astype(_MXU)

    bdr_all, bdi_all, cdr_all, cdi_all = ssm_maps(bbr_all), ssm_maps(bbi_all), ssm_maps(ssm_c_re), ssm_maps(ssm_c_im)

    saved = []
    xf = x[0]
    xb = xf.astype(_MXU)
    for l in range(depth):
        fw = full[l]
        deps = [arrive_gather(l, "ffn", xb)] if l >= 2 else []
        if l + 1 < depth:
            issued = begin_gather(l + 1, "ffn", [begin_gather(l + 1, "mixer", [fw["win_t"], issued])])
            deps.append(issued)
        h = _mm_nt(f"in_proj_{l}", xb, fw["win_t"], deps=tuple(deps))
        q_rot, k_rot = _rope(f"rope_{l}", h, o_v, cos_t, sin_t, _MXU, ((0, o_k), (o_k, o_v)))
        k_hm = jnp.transpose(k_rot.reshape(s, nkv, HEAD_DIM), (1, 0, 2))
        v_hm = jnp.transpose(h[:, o_v:o_p].astype(_MXU).reshape(s, nkv, HEAD_DIM), (1, 0, 2))
        sinks = attn_sinks[l].reshape(nkv, GQA)
        o_attn, lse = _attn_fwd(f"attn_{l}", q_rot, k_hm, v_hm, sinks)
        pw_b = pool_w[l].astype(_MXU)
        psc = pool_scale[l].reshape(1, pool_wd)
        y_pool, pre = _pool_fwd(f"pool_{l}", h, o_p // pool_wd, pw_b, psc)
        bdr, bdi, cdr, cdi = bdr_all[l], bdi_all[l], cdr_all[l], cdi_all[l]
        dvec = ssm_d[l].reshape(1, ssm_wd)
        ar, ai = ab_re_all[l].reshape(1, state_w), ab_im_all[l].reshape(1, state_w)
        cw_ssm = ssm_wd // nt_ssm
        sr, si, ypre, yg = _ssm_fwd(f"ssm_{l}", h, o_s // cw_ssm, bdr, bdi, cdr, cdi, dvec, ar, ai)
        if l == 0:
            fw.update(out_weights(end_gather(0, "out", yg)))
        y_ssm, ab2 = _glu_fwd(f"glu_{l}", yg, fw["glu"])
        mix = jnp.concatenate([o_attn.astype(_MXU), y_pool, y_ssm], -1)
        a1 = _mm_nn(f"out_proj_{l}", mix, fw["wout"])
        g1, b1 = ln1_g[l].reshape(1, d), ln1_b[l].reshape(1, d)
        x1, x1b, xh1, rs1 = _ln_fwd(f"ln1_{l}", xf, a1, g1, b1, alpha)
        fw.update(up_weights(end_gather(0, "up", x1b)) if l == 0 else ffn_weights(end_gather(l, "ffn", x1b)))
        deps = (arrive_gather(l + 1, "mixer", x1b),) if 1 <= l < depth - 1 else ()
        hu = _mm_nt(f"ffn_up_{l}", x1b, fw["wup_t"], cap=2 * half_pad, deps=deps)
        act = _conv_act_fwd(f"ffn_act_{l}", hu, fw["cw"], fw["cb"])
        if l == 0:
            fw.update(down_weights(end_gather(0, "down", act)))
        f_out = _mm_nn(f"ffn_down_{l}", act, fw["wdown"], cap=256)
        g2, b2 = ln2_g[l].reshape(1, d), ln2_b[l].reshape(1, d)
        x2, x2b, xh2, rs2 = _ln_fwd(f"ln2_{l}", x1, f_out, g2, b2, alpha)
        saved.append(dict(xb=xb, h=h, q_rot=q_rot, k_hm=k_hm, v_hm=v_hm, sinks=sinks, o_attn=o_attn, lse=lse, pw_b=pw_b, psc=psc,
                          pre=pre, bdr=bdr, bdi=bdi, cdr=cdr, cdi=cdi, dvec=dvec, ar=ar, ai=ai, sr=sr, si=si, ypre=ypre, yg=yg,
                          ab2=ab2, mix=mix, g1=g1, xh1=xh1, rs1=rs1, x1b=x1b, hu=hu, act=act, g2=g2, xh2=xh2, rs2=rs2))
        xf, xb = x2, x2b
        if l + 1 < depth:
            full[l + 1] = mixer_weights(l + 1, end_gather(l + 1, "mixer", x2b))

    dy, loss_part = _loss_head("loss_head", xf, loss_target[0])
    loss = lax.psum(loss_part[0, 0], ("x", "y", "c"))

    small_handles = [None] * depth
    small_parts = [None] * depth
    outs = {}
    big_res = {k: None for k in _BIG}
    my_chip = 2 * xi + yi
    pending = []

    transposed = ("w_in", "ffn_w_up")

    def row_groups(name_, t):
        g = 2 if name_ == "ffn_w_up" else 1
        return t.reshape(t.shape[:-2] + (g, t.shape[-2] // g, t.shape[-1]))

    def as_groups(name_, t):
        return row_groups(name_, jnp.transpose(t, (0, 2, 1)) if name_ in transposed else t)

    def from_groups(name_, t):
        t = t.reshape(t.shape[0], t.shape[1] * t.shape[2], t.shape[3])
        return jnp.transpose(t, (0, 2, 1)) if name_ in transposed else t

    grouped = {name_: tuple(as_groups(name_, t[name_]) for t in (W, M, V)) for name_ in _BIG}

    def finish_exchanges(after):
        while pending:
            lay, part, names, handle = pending.pop(0)
            pairs, lands = _chips_end(f"grads_between_chips_{part}_{lay}", handle, after)
            for name_, p, ld in zip(names, pairs, lands):
                own = row_groups(name_, lax.dynamic_index_in_dim(p, my_chip, 0, keepdims=False))
                big_res[name_] = _adamw_layer(f"adamw_{name_}_{lay}", lay, own, row_groups(name_, ld), *grouped[name_],
                                              big_res[name_])

    def begin_swap(lay, part, names, grads):
        by_owner = [a.reshape((4, 2) + a.shape[1:]) for a in grads]
        return lay, part, names, _sibling_begin(f"grads_to_sibling_{part}_{lay}", by_owner, [])

    def begin_exchange(swap, after):
        lay, part, names, handle = swap
        by_owner, theirs = _sibling_end(f"grads_to_sibling_{part}_{lay}", handle, after)
        pair = [_pair_sum(f"pair_sum_{name_}_{lay}", a, b, c_idx) for name_, a, b in zip(names, by_owner, theirs)]
        finish_exchanges(after)
        handle = _chips_begin(f"grads_between_chips_{part}_{lay}", pair, [])
        pending.append((lay, part, names, handle))
        return handle[-1]

    token = None
    small_handle = None
    for l in reversed(range(depth)):
        fw, sv = full[l], saved[l]
        deps = () if token is None else (token, small_handles[l + 1][-1])
        dr2, dr2b, dg2, db2 = _ln_bwd(f"ln2_bwd_{l}", dy, sv["xh2"], sv["rs2"], sv["g2"], deps=deps)
        d_wdown = _mm_tn_acols(f"ffn_down_dw_{l}", sv["act"], dr2b, _WIRE, cap=2 * half_pad)
        dact = _mm_nt(f"ffn_down_dx_{l}", dr2b, fw["wdown"], cap=2 * half_pad)
        dhu, dcw, dcb = _conv_act_bwd(f"ffn_act_bwd_{l}", dact, sv["hu"], fw["cw"], fw["cb"])
        d_wup = _mm(f"ffn_up_dw_{l}", dhu, sv["x1b"], TN, (N_DEV, 1),
                    pl.BlockSpec((None, s, 2 * half_pad), lambda j, kk: (j // 4, 0, j % 4)),
                    _resident((s, d), lambda j, kk: (0, 0)),
                    pl.BlockSpec((2 * half_pad, d), lambda j, kk: (j, 0)), (2 * ffp, d), _WIRE)
        swap = begin_swap(l, "ffn", ("ffn_w_up", "ffn_w_down"),
                          [d_wup.reshape(N_DEV, 2 * half_pad, d), d_wdown.reshape(N_DEV, half_pad, d)])
        dy1 = _mm_split_k(f"ffn_up_dx_{l}", dhu, fw["wup_t"], dr2, alpha, deps=(swap[3][-1],))
        token = begin_exchange(swap, dy1)
        dr1, dr1b, dg1, db1 = _ln_bwd(f"ln1_bwd_{l}", dy1, sv["xh1"], sv["rs1"], sv["g1"], deps=(token,))
        d_wout = _mm_tn_acols(f"out_proj_dw_{l}", sv["mix"], dr1b, _WIRE, cap=d // N_DEV)
        dmix = _mm_nt(f"out_proj_dx_{l}", dr1b, fw["wout"])
        dq_rot, dk_hm, dv_hm, dsk = _attn_bwd(f"attn_bwd_{l}", sv["q_rot"], sv["k_hm"], sv["v_hm"], sv["o_attn"], dmix,
                                             sv["lse"], sv["sinks"])
        dqk = jnp.concatenate([dq_rot, jnp.transpose(dk_hm, (1, 0, 2)).reshape(s, kv_w)], -1)
        dhq, dhk = _rope(f"rope_bwd_{l}", dqk, o_v, cos_t, -sin_t, _MXU, ((0, o_k), (o_k, o_v)))
        dhv = jnp.transpose(dv_hm, (1, 0, 2)).reshape(s, kv_w).astype(_MXU)
        dhp, dpw, dpsc = _pool_bwd(f"pool_bwd_{l}", dmix, attn_w // pool_wd, sv["pre"], sv["pw_b"], sv["psc"])
        dab2, dyg = _glu_bwd(f"glu_bwd_{l}", dmix, (attn_w + pool_wd) // ssm_wd, sv["ab2"], fw["glu"])
        d_glu = _mm_tn_bcols(f"glu_dw_{l}", sv["yg"], dab2, _WIRE)
        cw_ssm = ssm_wd // nt_ssm
        dhs, dd, dcdr, dcdi, dbdr, dbdi, dar, dai = _ssm_bwd(
            f"ssm_bwd_{l}", dyg, sv["ypre"], sv["h"], o_s // cw_ssm, sv["sr"], sv["si"], sv["bdr"], sv["bdi"], sv["cdr"],
            sv["cdi"], sv["dvec"], sv["ar"], sv["ai"])
        dh = jnp.concatenate([dhq, dhk, dhv, dhp, dhs], -1)
        d_win = _mm_tn_acols(f"in_proj_dw_{l}", dh, sv["xb"], _WIRE)
        swap = begin_swap(l, "mixer", ("w_in", "ssm_glu_w", "w_out"),
                          [d_win.reshape(N_DEV, in_w // N_DEV, d),
                           jnp.transpose(d_glu.reshape(ssm_wd, N_DEV, 2 * ssm_wd // N_DEV), (1, 0, 2)),
                           d_wout.reshape(N_DEV, d // N_DEV, d)])
        dy = _mm_nn(f"in_proj_dx_{l}", dh, fw["win_t"], add=dr1, add_scale=alpha, deps=(swap[3][-1],))

        raw = dict(attn_sinks=dsk, pool_w=dpw, pool_scale=dpsc, ssm_lam_re=dar, ssm_lam_im=dai,
                   ssm_b_re=_diagblocks(dbdr, n_groups, SSM_GROUP, SSM_STATE),
                   ssm_b_im=_diagblocks(dbdi, n_groups, SSM_GROUP, SSM_STATE),
                   ssm_c_re=_diagblocks(dcdr, n_groups, SSM_STATE, SSM_GROUP),
                   ssm_c_im=_diagblocks(dcdi, n_groups, SSM_STATE, SSM_GROUP), ssm_d=dd, ln1_g=dg1, ln1_b=db1,
                   ffn_conv_b=_unpad_pairs(dcb.reshape(N_DEV, 2 * half_pad), 1, half, half_pad), ln2_g=dg2, ln2_b=db2,
                   ffn_conv_w=_unpad_pairs(dcw.reshape(CONV_WIDTH, N_DEV, 2 * half_pad), 2, half, half_pad))
        raw_shapes = {k: raw[k].shape for k in _RAW}
        small_handles[l] = _gather_begin(f"gather_small_grads_{l}", [_pack([raw[k] for k in _RAW])], [dy])
        token = begin_exchange(swap, small_handles[l][-1])
        if l + 1 < depth:
            small_parts[l + 1] = _sum_parts(f"sum_small_grads_{l + 1}",
                                            _gather_end(f"gather_small_grads_{l + 1}", small_handles[l + 1], dy)[0])

    small_parts[0] = _sum_parts("sum_small_grads_0", _gather_end("gather_small_grads_0", small_handles[0], token)[0])
    summed = jnp.stack(small_parts)
    g_small = dict(zip(_RAW, _unpack(summed, [raw_shapes[k] for k in _RAW])))
    swap_last = lambda t: jnp.transpose(t, (0, 1, 3, 2))
    _, vjp = jax.vjp(_ssm_discretise, *ssm_params)
    dlr, dli, dldt, dbr, dbi = vjp((g_small["ssm_lam_re"].reshape(depth, n_groups, SSM_STATE),
                                    g_small["ssm_lam_im"].reshape(depth, n_groups, SSM_STATE),
                                    swap_last(g_small["ssm_b_re"]), swap_last(g_small["ssm_b_im"])))
    g_small.update(ssm_lam_re=dlr, ssm_lam_im=dli, ssm_log_dt=dldt, ssm_b_re=dbr, ssm_b_im=dbi,
                   ssm_c_re=swap_last(g_small["ssm_c_re"]), ssm_c_im=swap_last(g_small["ssm_c_im"]),
                   ffn_conv_w=lax.dynamic_index_in_dim(g_small["ffn_conv_w"], me, axis=2, keepdims=False))
    for k in _SMALL + ("ffn_conv_w",):
        shp = W[k].shape
        flip = k in ("ssm_b_re", "ssm_b_im")
        view = (lambda t: jnp.swapaxes(t, -1, -2)) if flip else (lambda t: t)
        vshp = view(W[k]).shape
        two_d = (math.prod(vshp[:-1]), vshp[-1])
        res = _adamw_2d(f"adamw_{k}", view(g_small[k].reshape(shp)).reshape(two_d),
                        *(view(t[k]).reshape(two_d) for t in (W, M, V)))
        outs[k] = (g_small[k].reshape(shp),) + tuple(view(a.reshape(vshp)) for a in res)

    finish_exchanges(outs["ln2_b"][1])
    for name_ in _BIG:
        outs[name_] = tuple(from_groups(name_, t) for t in big_res[name_])

    grad_x = dy[None]
    result = [loss, grad_x]
    for i in range(4):
        result += [outs[k][i] for k in _ORDER]
    return tuple(result)
```
